```python
import math
import jax
import jax.numpy as jnp
from jax import lax
import numpy as np

D_MODEL = 1024
BATCH = 16
SEQ = 4096
DEPTH = 2

GRID_W = 64
CTX_LEN = 256
EPS = 1e-6
NEG_INF = -1e30

N_MIXERS = 4
D_MIX = D_MODEL
D_GROUP = D_MIX // N_MIXERS

HY_SHORT = 3
HY_EMB = 33
HY_FILTER_HIDDEN = 64
HY_FAST_DECAY = 0.3
HY_SLOW_DECAY = 1.5
HY_TARGET = 1e-2

CONF_KERNEL = 31

ATT_HEADS = 4
ATT_KV_HEADS = 2
HEAD_DIM = D_GROUP // ATT_HEADS
WINDOW = 128
ATT_BLOCK = 128
ROPE_BASE = 10000.0

LRU_HEADS = 4
LRU_CONV = 4
LRU_C = 8.0

N_GROUPS = 4
EXP_PER_GROUP = 8
N_EXPERTS = N_GROUPS * EXP_PER_GROUP
TOP_K = 2
D_EXPERT = 512
MOE_BLOCK = 256

HY_COLS = 3 * D_GROUP
CONF_COLS = 2 * D_GROUP
ATT_COLS = (ATT_HEADS + 2 * ATT_KV_HEADS) * HEAD_DIM
LRU_COLS = 2 * D_GROUP
D_IN_PROJ = HY_COLS + CONF_COLS + ATT_COLS + LRU_COLS

kernel_name = "hybrid_parallel_group_flow_block"


def rms_norm(x, g):
    xf = x.astype(jnp.float32)
    y = xf * lax.rsqrt(jnp.mean(xf * xf, axis=-1, keepdims=True) + EPS)
    return (y * g.astype(jnp.float32)).astype(x.dtype)


def layer_norm(x, g, b):
    xf = x.astype(jnp.float32)
    mu = jnp.mean(xf, axis=-1, keepdims=True)
    var = jnp.mean(jnp.square(xf - mu), axis=-1, keepdims=True)
    y = (xf - mu) * lax.rsqrt(var + EPS) * g.astype(jnp.float32) + b.astype(jnp.float32)
    return y.astype(x.dtype)


def modulated_norm(x, g, shift, scale):
    return rms_norm(x, g) * (1.0 + scale) + shift


def group_rms_norm(y, g):
    B, L, _ = y.shape
    yf = y.astype(jnp.float32).reshape(B, L, N_MIXERS, D_GROUP)
    yf = yf * lax.rsqrt(jnp.mean(yf * yf, axis=-1, keepdims=True) + EPS)
    return (yf.reshape(B, L, D_MIX) * g.astype(jnp.float32)).astype(y.dtype)


def dw_conv(x, w, b, pad_left, pad_right):
    C = x.shape[-1]
    y = lax.conv_general_dilated(
        x, w[:, None, :].astype(x.dtype), window_strides=(1,),
        padding=[(pad_left, pad_right)], dimension_numbers=("NWC", "WIO", "NWC"),
        feature_group_count=C)
    return y + b.astype(x.dtype)


def hyena_filters(L, lp):
    f32 = jnp.float32
    t = jnp.linspace(0.0, 1.0, L, dtype=f32)[:, None]
    bands = (HY_EMB - 1) // 2
    w = 2.0 * math.pi * jnp.arange(L, dtype=f32)[:, None] / L
    f = jnp.linspace(1e-4, bands - 1, bands, dtype=f32)[None]
    z = jnp.concatenate([t, jnp.cos(f * w), -jnp.sin(f * w)], axis=-1)
    hdn = jnp.sin(z @ lp["hy_ffn_w1"].astype(f32) + lp["hy_ffn_b1"].astype(f32))
    hdn = jnp.sin(hdn @ lp["hy_ffn_w2"].astype(f32) + lp["hy_ffn_b2"].astype(f32))
    h = (hdn @ lp["hy_ffn_w3"].astype(f32)).reshape(L, 2, D_GROUP)
    max_decay = math.log(HY_TARGET) / HY_FAST_DECAY
    min_decay = math.log(HY_TARGET) / HY_SLOW_DECAY
    deltas = jnp.linspace(min_decay, max_decay, D_GROUP, dtype=f32)
    h = h * jnp.exp(-t * jnp.abs(deltas))[:, None, :]
    h = h / (jnp.sum(jnp.abs(h), axis=(0, 1), keepdims=True) + EPS)
    return h[:, 0], h[:, 1]


def bidir_long_conv(z, h_fwd, h_bwd, bias):
    L, C = z.shape[1], z.shape[2]
    k = jnp.concatenate([h_fwd, jnp.zeros((1, C), jnp.float32), h_bwd[1:][::-1]], axis=0)
    zf = z.astype(jnp.float32)
    zk = jnp.fft.rfft(zf, n=2 * L, axis=1) * jnp.fft.rfft(k, axis=0)[None]
    y = jnp.fft.irfft(zk, n=2 * L, axis=1)[:, :L]
    return (y + zf * bias.astype(jnp.float32)).astype(z.dtype)


def hyena_mixer(u, lp):
    L = u.shape[1]
    uc = dw_conv(u, lp["hy_short_w"], lp["hy_short_b"], 1, 1)
    x0, x1, v = jnp.split(uc, 3, axis=-1)
    h_fwd, h_bwd = hyena_filters(L, lp)
    return x0 * bidir_long_conv(x1 * v, h_fwd, h_bwd, lp["hy_bias"])


def conformer_conv(u, lp):
    a, gate = jnp.split(u, 2, axis=-1)
    y = a * jax.nn.sigmoid(gate)
    pad = (CONF_KERNEL - 1) // 2
    y = dw_conv(y, lp["conf_dw_w"], lp["conf_dw_b"], pad, CONF_KERNEL - 1 - pad)
    y = layer_norm(y, lp["conf_ln_g"], lp["conf_ln_b"])
    return jax.nn.silu(y)


def rope_2d(x):
    L = x.shape[1]
    n_rows = L // GRID_W
    row = jnp.repeat(jnp.arange(n_rows), GRID_W).astype(jnp.float32)
    col = jnp.tile(jnp.arange(GRID_W), n_rows).astype(jnp.float32)
    half = HEAD_DIM // 2
    inv_freq = ROPE_BASE ** (-jnp.arange(0, half, 2, dtype=jnp.float32) / half)

    def rotate(xh, pos):
        ang = pos[:, None] * inv_freq[None]
        cos = jnp.cos(ang)[None, :, None, :]
        sin = jnp.sin(ang)[None, :, None, :]
        x1, x2 = jnp.split(xh.astype(jnp.float32), 2, axis=-1)
        return jnp.concatenate([x1 * cos - x2 * sin, x2 * cos + x1 * sin], axis=-1)

    return jnp.concatenate([rotate(x[..., :half], row), rotate(x[..., half:], col)], axis=-1).astype(x.dtype)


def attn_proj(u, rotary):
    B, L, _ = u.shape
    q, k, v = jnp.split(u, [ATT_HEADS * HEAD_DIM, (ATT_HEADS + ATT_KV_HEADS) * HEAD_DIM], axis=-1)
    q = q.reshape(B, L, ATT_HEADS, HEAD_DIM)
    k = k.reshape(B, L, ATT_KV_HEADS, HEAD_DIM)
    v = v.reshape(B, L, ATT_KV_HEADS, HEAD_DIM)
    if rotary:
        q, k = rope_2d(q), rope_2d(k)
    return q, k, v


def window_attention(q, k, v, kc, vc, sink):
    B, S = q.shape[0], q.shape[1]
    nb = S // ATT_BLOCK
    G = ATT_HEADS // ATT_KV_HEADS
    C = kc.shape[1]
    nw = 3 * ATT_BLOCK
    qb = q.reshape(B, nb, ATT_BLOCK, ATT_KV_HEADS, G, HEAD_DIM)
    pad = ((0, 0), (ATT_BLOCK, ATT_BLOCK), (0, 0), (0, 0))
    kp = jnp.pad(k, pad).reshape(B, nb + 2, ATT_BLOCK, ATT_KV_HEADS, HEAD_DIM)
    vp = jnp.pad(v, pad).reshape(B, nb + 2, ATT_BLOCK, ATT_KV_HEADS, HEAD_DIM)
    kw = jnp.concatenate([kp[:, :-2], kp[:, 1:-1], kp[:, 2:]], axis=2)
    vw = jnp.concatenate([vp[:, :-2], vp[:, 1:-1], vp[:, 2:]], axis=2)
    blk = jnp.arange(nb)[:, None]
    q_pos = blk * ATT_BLOCK + jnp.arange(ATT_BLOCK)[None]
    k_pos = (blk - 1) * ATT_BLOCK + jnp.arange(nw)[None]
    kpb = k_pos[:, None, :]
    valid = (jnp.abs(kpb - q_pos[:, :, None]) <= WINDOW) & (kpb >= 0) & (kpb < S)
    scale = HEAD_DIM ** -0.5
    s_win = jnp.einsum("bnqkgd,bnjkd->bnkgqj", qb, kw).astype(jnp.float32) * scale
    s_win = jnp.where(valid[None, :, None, None], s_win, NEG_INF)
    s_ctx = jnp.einsum("bnqkgd,bckd->bnkgqc", qb, kc).astype(jnp.float32) * scale
    s_sink = jnp.broadcast_to(sink.astype(jnp.float32).reshape(1, 1, ATT_KV_HEADS, G, 1, 1),
                              s_win.shape[:-1] + (1,))
    p = jax.nn.softmax(jnp.concatenate([s_win, s_ctx, s_sink], axis=-1), axis=-1).astype(v.dtype)
    o = (jnp.einsum("bnkgqj,bnjkd->bnqkgd", p[..., :nw], vw)
         + jnp.einsum("bnkgqc,bckd->bnqkgd", p[..., nw:nw + C], vc))
    return o.reshape(B, S, ATT_HEADS * HEAD_DIM)


def context_attention(qc, kc, vc, sink):
    B, C = qc.shape[0], qc.shape[1]
    G = ATT_HEADS // ATT_KV_HEADS
    qg = qc.reshape(B, C, ATT_KV_HEADS, G, HEAD_DIM)
    s = jnp.einsum("bqkgd,bckd->bkgqc", qg, kc).astype(jnp.float32) * HEAD_DIM ** -0.5
    s_sink = jnp.broadcast_to(sink.astype(jnp.float32).reshape(1, ATT_KV_HEADS, G, 1, 1), s.shape[:-1] + (1,))
    p = jax.nn.softmax(jnp.concatenate([s, s_sink], axis=-1), axis=-1)[..., :C].astype(vc.dtype)
    return jnp.einsum("bkgqc,bckd->bqkgd", p, vc).reshape(B, C, ATT_HEADS * HEAD_DIM)


def rglru_coeffs(x, wa, ba, wx, bx, lam):
    B, L, C = x.shape
    f32 = jnp.float32
    xb = x.reshape(B, L, LRU_HEADS, C // LRU_HEADS)
    r = jax.nn.sigmoid(jnp.einsum("blhi,hio->blho", xb, wa.astype(f32)).reshape(B, L, C) + ba.astype(f32))
    i = jax.nn.sigmoid(jnp.einsum("blhi,hio->blho", xb, wx.astype(f32)).reshape(B, L, C) + bx.astype(f32))
    log_a = -LRU_C * r * jax.nn.softplus(-lam.astype(f32))
    a = jnp.exp(log_a)
    b = jnp.sqrt(-jnp.expm1(2.0 * log_a)) * (i * x)
    return a, b


def linear_scan(a, b, h0, reverse):
    if h0 is not None:
        edge = -1 if reverse else 0
        b = b.at[:, edge].add(a[:, edge] * h0)

    def combine(e1, e2):
        a1, b1 = e1
        a2, b2 = e2
        return a1 * a2, a2 * b1 + b2

    _, h = lax.associative_scan(combine, (a, b), reverse=reverse, axis=1)
    return h


def rglru_mixer(uc, ux, lp, need_ctx):
    gate_c, rec_c = jnp.split(uc, 2, axis=-1)
    gate_x, rec_x = jnp.split(ux, 2, axis=-1)
    pl = (LRU_CONV - 1) // 2
    xc = dw_conv(rec_c, lp["lru_conv_w"], lp["lru_conv_b"], pl, LRU_CONV - 1 - pl).astype(jnp.float32)
    xx = dw_conv(rec_x, lp["lru_conv_w"], lp["lru_conv_b"], pl, LRU_CONV - 1 - pl).astype(jnp.float32)
    y_x = jnp.zeros_like(xx)
    y_c = jnp.zeros_like(xc)
    for d, reverse in enumerate((False, True)):
        params = (lp["lru_wa"][d], lp["lru_ba"][d], lp["lru_wx"][d], lp["lru_bx"][d], lp["lru_lambda"][d])
        a_c, b_c = rglru_coeffs(xc, *params)
        h_c = linear_scan(a_c, b_c, None, reverse)
        h0 = h_c[:, 0] if reverse else h_c[:, -1]
        a_x, b_x = rglru_coeffs(xx, *params)
        y_x = y_x + linear_scan(a_x, b_x, h0, reverse)
        if need_ctx:
            y_c = y_c + h_c
    out_x = jax.nn.gelu(gate_x) * y_x.astype(ux.dtype)
    out_c = jax.nn.gelu(gate_c) * y_c.astype(uc.dtype) if need_ctx else None
    return out_c, out_x


def token_mixers(uc, ux, lp, need_ctx):
    cuts = [HY_COLS, HY_COLS + CONF_COLS, HY_COLS + CONF_COLS + ATT_COLS]
    hy_c, cf_c, at_c, lr_c = jnp.split(uc, cuts, axis=-1)
    hy_x, cf_x, at_x, lr_x = jnp.split(ux, cuts, axis=-1)
    qc, kc, vc = attn_proj(at_c, rotary=False)
    qx, kx, vx = attn_proj(at_x, rotary=True)
    yd_c, yd_x = rglru_mixer(lr_c, lr_x, lp, need_ctx)
    y_x = jnp.concatenate([
        hyena_mixer(hy_x, lp),
        conformer_conv(cf_x, lp),
        window_attention(qx, kx, vx, kc, vc, lp["attn_sink"]),
        yd_x], axis=-1)
    if need_ctx:
        y_c = jnp.concatenate([
            hyena_mixer(hy_c, lp),
            conformer_conv(cf_c, lp),
            context_attention(qc, kc, vc, lp["attn_sink"]),
            yd_c], axis=-1)
    else:
        y_c = None
    return y_c, y_x


def hier_moe(h, lp):
    B, L, D = h.shape
    T = B * L
    xt = h.reshape(T, D)
    rows = jnp.arange(T)
    g_logits = (xt @ lp["router_g_w"] + lp["router_g_b"]).astype(jnp.float32)
    g_idx = jnp.argmax(g_logits, axis=-1)
    g_prob = jax.nn.softmax(g_logits, axis=-1)[rows, g_idx]
    e_logits = (xt @ lp["router_e_w"] + lp["router_e_b"]).astype(jnp.float32)
    e_logits = e_logits.reshape(T, N_GROUPS, EXP_PER_GROUP)[rows, g_idx]
    top_w, top_e = lax.top_k(jax.nn.softmax(e_logits, axis=-1), TOP_K)
    top_w = top_w / jnp.sum(top_w, axis=-1, keepdims=True)
    weights = (g_prob[:, None] * top_w).reshape(-1)
    experts = (g_idx[:, None] * EXP_PER_GROUP + top_e).reshape(-1).astype(jnp.int32)
    tokens = jnp.repeat(jnp.arange(T, dtype=jnp.int32), TOP_K)
    A = T * TOP_K
    order = jnp.argsort(experts)
    s_exp, s_tok, s_w = experts[order], tokens[order], weights[order]
    counts = jnp.bincount(experts, length=N_EXPERTS)
    padded = (counts + MOE_BLOCK - 1) // MOE_BLOCK * MOE_BLOCK
    pad_end = jnp.cumsum(padded)
    pad_start = pad_end - padded
    raw_start = jnp.cumsum(counts) - counts
    dest = pad_start[s_exp] + jnp.arange(A, dtype=jnp.int32) - raw_start[s_exp]
    n_blocks = -(-A // MOE_BLOCK) + N_EXPERTS
    P = n_blocks * MOE_BLOCK
    buf_tok = jnp.full((P,), T, jnp.int32).at[dest].set(s_tok)
    buf_w = jnp.zeros((P,), jnp.float32).at[dest].set(s_w)
    blk_exp = jnp.minimum(jnp.searchsorted(pad_end, jnp.arange(n_blocks) * MOE_BLOCK, side="right"),
                          N_EXPERTS - 1)
    x_pad = jnp.concatenate([xt, jnp.zeros((1, D), xt.dtype)], axis=0)
    w_gate, w_up, w_down = lp["exp_w_gate"], lp["exp_w_up"], lp["exp_w_down"]

    def expert_block(args):
        tok, e = args
        xb = x_pad[tok]
        hid = jax.nn.silu(xb @ w_gate[e]) * (xb @ w_up[e])
        return hid @ w_down[e]

    out = lax.map(expert_block, (buf_tok.reshape(n_blocks, MOE_BLOCK), blk_exp))
    y = jnp.zeros((T + 1, D), h.dtype).at[buf_tok].add(out.reshape(P, D) * buf_w[:, None].astype(h.dtype))
    return y[:T].reshape(B, L, D)


def hybrid_layer(hc, hx, c_silu, cctx_silu, lp, need_ctx):
    mod_x = (c_silu @ lp["ada_w"] + lp["ada_b"])[:, None, :]
    mod_c = cctx_silu @ lp["ada_w"] + lp["ada_b"]
    sh1x, sc1x, g1x, sh2x, sc2x, g2x = jnp.split(mod_x, 6, axis=-1)
    sh1c, sc1c, g1c, sh2c, sc2c, g2c = jnp.split(mod_c, 6, axis=-1)
    ux = modulated_norm(hx, lp["norm1_g"], sh1x, sc1x) @ lp["w_in"]
    uc = modulated_norm(hc, lp["norm1_g"], sh1c, sc1c) @ lp["w_in"]
    y_c, y_x = token_mixers(uc, ux, lp, need_ctx)
    hx = hx + g1x * (group_rms_norm(y_x, lp["group_norm_g"]) @ lp["w_out"])
    nx = modulated_norm(hx, lp["norm2_g"], sh2x, sc2x)
    if need_ctx:
        hc = hc + g1c * (group_rms_norm(y_c, lp["group_norm_g"]) @ lp["w_out"])
        nc = modulated_norm(hc, lp["norm2_g"], sh2c, sc2c)
        n_ctx = hc.shape[1]
        m = hier_moe(jnp.concatenate([nc, nx], axis=1), lp)
        hc = hc + g2c * m[:, :n_ctx]
        hx = hx + g2x * m[:, n_ctx:]
    else:
        hx = hx + g2x * hier_moe(nx, lp)
    return hc, hx


def setup_inputs(seed: int = 0) -> dict:
    key = jax.random.key(seed)
    ks = iter(jax.random.split(key, 64))

    def nrm(shape, scale):
        return scale * jax.random.normal(next(ks), shape, jnp.float32)

    def gain(shape):
        return 1.0 + nrm(shape, 0.02)

    u = jax.random.uniform(next(ks), (DEPTH, 2, D_GROUP), jnp.float32, minval=0.9, maxval=0.999)
    a_base = u ** (1.0 / LRU_C)
    lru_lambda = jnp.log(a_base) - jnp.log1p(-a_base)
    blk = D_GROUP // LRU_HEADS
    return {
        "x": nrm((BATCH, SEQ, D_MODEL), 1.0),
        "c": nrm((BATCH, D_MODEL), 1.0),
        "ctx": nrm((BATCH, CTX_LEN, D_MODEL), 1.0),
        "c_ctx": nrm((D_MODEL,), 1.0),
        "norm1_g": gain((DEPTH, D_MODEL)),
        "norm2_g": gain((DEPTH, D_MODEL)),
        "ada_w": nrm((DEPTH, D_MODEL, 6 * D_MODEL), D_MODEL ** -0.5),
        "ada_b": nrm((DEPTH, 6 * D_MODEL), 0.01),
        "w_in": nrm((DEPTH, D_MODEL, D_IN_PROJ), D_MODEL ** -0.5),
        "hy_short_w": nrm((DEPTH, HY_SHORT, HY_COLS), HY_SHORT ** -0.5),
        "hy_short_b": nrm((DEPTH, HY_COLS), 0.01),
        "hy_ffn_w1": nrm((DEPTH, HY_EMB, HY_FILTER_HIDDEN), HY_EMB ** -0.5),
        "hy_ffn_b1": nrm((DEPTH, HY_FILTER_HIDDEN), 0.1),
        "hy_ffn_w2": nrm((DEPTH, HY_FILTER_HIDDEN, HY_FILTER_HIDDEN), HY_FILTER_HIDDEN ** -0.5),
        "hy_ffn_b2": nrm((DEPTH, HY_FILTER_HIDDEN), 0.1),
        "hy_ffn_w3": nrm((DEPTH, HY_FILTER_HIDDEN, 2 * D_GROUP), HY_FILTER_HIDDEN ** -0.5),
        "hy_bias": nrm((DEPTH, D_GROUP), 0.1),
        "conf_dw_w": nrm((DEPTH, CONF_KERNEL, D_GROUP), CONF_KERNEL ** -0.5),
        "conf_dw_b": nrm((DEPTH, D_GROUP), 0.01),
        "conf_ln_g": gain((DEPTH, D_GROUP)),
        "conf_ln_b": nrm((DEPTH, D_GROUP), 0.01),
        "attn_sink": nrm((DEPTH, ATT_HEADS), 0.5),
        "lru_conv_w": nrm((DEPTH, LRU_CONV, D_GROUP), LRU_CONV ** -0.5),
        "lru_conv_b": nrm((DEPTH, D_GROUP), 0.01),
        "lru_wa": nrm((DEPTH, 2, LRU_HEADS, blk, blk), blk ** -0.5),
        "lru_ba": nrm((DEPTH, 2, D_GROUP), 0.01),
        "lru_wx": nrm((DEPTH, 2, LRU_HEADS, blk, blk), blk ** -0.5),
        "lru_bx": nrm((DEPTH, 2, D_GROUP), 0.01),
        "lru_lambda": lru_lambda,
        "group_norm_g": gain((DEPTH, D_MIX)),
        "w_out": nrm((DEPTH, D_MIX, D_MODEL), D_MIX ** -0.5),
        "router_g_w": nrm((DEPTH, D_MODEL, N_GROUPS), D_MODEL ** -0.5),
        "router_g_b": nrm((DEPTH, N_GROUPS), 0.01),
        "router_e_w": nrm((DEPTH, D_MODEL, N_EXPERTS), D_MODEL ** -0.5),
        "router_e_b": nrm((DEPTH, N_EXPERTS), 0.01),
        "exp_w_gate": nrm((DEPTH, N_EXPERTS, D_MODEL, D_EXPERT), D_MODEL ** -0.5),
        "exp_w_up": nrm((DEPTH, N_EXPERTS, D_MODEL, D_EXPERT), D_MODEL ** -0.5),
        "exp_w_down": nrm((DEPTH, N_EXPERTS, D_EXPERT, D_MODEL), D_EXPERT ** -0.5),
        "final_norm_g": gain((D_MODEL,)),
    }


def reference(x, c, ctx, c_ctx, norm1_g, norm2_g, ada_w, ada_b, w_in, hy_short_w, hy_short_b,
              hy_ffn_w1, hy_ffn_b1, hy_ffn_w2, hy_ffn_b2, hy_ffn_w3, hy_bias, conf_dw_w, conf_dw_b,
              conf_ln_g, conf_ln_b, attn_sink, lru_conv_w, lru_conv_b, lru_wa, lru_ba, lru_wx, lru_bx,
              lru_lambda, group_norm_g, w_out, router_g_w, router_g_b, router_e_w, router_e_b,
              exp_w_gate, exp_w_up, exp_w_down, final_norm_g):
    c_silu = jax.nn.silu(c)
    cctx_silu = jax.nn.silu(c_ctx)
    hc, hx = ctx, x
    for l in range(DEPTH):
        lp = {
            "norm1_g": norm1_g[l], "norm2_g": norm2_g[l], "ada_w": ada_w[l], "ada_b": ada_b[l],
            "w_in": w_in[l], "hy_short_w": hy_short_w[l], "hy_short_b": hy_short_b[l],
            "hy_ffn_w1": hy_ffn_w1[l], "hy_ffn_b1": hy_ffn_b1[l], "hy_ffn_w2": hy_ffn_w2[l],
            "hy_ffn_b2": hy_ffn_b2[l], "hy_ffn_w3": hy_ffn_w3[l], "hy_bias": hy_bias[l],
            "conf_dw_w": conf_dw_w[l], "conf_dw_b": conf_dw_b[l], "conf_ln_g": conf_ln_g[l],
            "conf_ln_b": conf_ln_b[l], "attn_sink": attn_sink[l], "lru_conv_w": lru_conv_w[l],
            "lru_conv_b": lru_conv_b[l], "lru_wa": lru_wa[l], "lru_ba": lru_ba[l], "lru_wx": lru_wx[l],
            "lru_bx": lru_bx[l], "lru_lambda": lru_lambda[l], "group_norm_g": group_norm_g[l],
            "w_out": w_out[l], "router_g_w": router_g_w[l], "router_g_b": router_g_b[l],
            "router_e_w": router_e_w[l], "router_e_b": router_e_b[l], "exp_w_gate": exp_w_gate[l],
            "exp_w_up": exp_w_up[l], "exp_w_down": exp_w_down[l],
        }
        hc, hx = hybrid_layer(hc, hx, c_silu, cctx_silu, lp, need_ctx=(l < DEPTH - 1))
    return rms_norm(hx, final_norm_g)
```

```python
import functools
import math

import jax
import jax.numpy as jnp
from jax import lax
from jax.experimental import pallas as pl
from jax.experimental.pallas import tpu as pltpu

F32 = jnp.float32
BF16 = jnp.bfloat16

EPS = 1e-6
NEG_INF = -1e30
GRID_W = 64
N_MIXERS = 4
D_GROUP = 256
HY_COLS = 3 * D_GROUP
CONF_COLS = 2 * D_GROUP
ATT_HEADS = 4
ATT_KV_HEADS = 2
HEAD_DIM = 64
ATT_COLS = (ATT_HEADS + 2 * ATT_KV_HEADS) * HEAD_DIM
LRU_COLS = 2 * D_GROUP
QK_COLS = (ATT_HEADS + ATT_KV_HEADS) * HEAD_DIM
WINDOW = 128
ATT_BLOCK = 128
ROPE_BASE = 10000.0
HY_EMB = 33
HY_FAST_DECAY = 0.3
HY_SLOW_DECAY = 1.5
HY_TARGET = 1e-2
CONF_KERNEL = 31
LRU_HEADS = 4
LRU_CONV = 4
LRU_C = 8.0
N_GROUPS = 4
EXP_PER_GROUP = 8
N_EXPERTS = N_GROUPS * EXP_PER_GROUP
TOP_K = 2
MOE_BLOCK = 256
ROUTER_COLS = 128

VMEM_LIMIT_BYTES = 56 * 1024 * 1024


def _cparams(*sem):
    return pltpu.CompilerParams(dimension_semantics=sem, vmem_limit_bytes=VMEM_LIMIT_BYTES)


def _linear_kernel(x_ref, w_ref, b_ref, o_ref):
    o_ref[...] = jnp.dot(x_ref[...], w_ref[...], preferred_element_type=F32,
                         precision=lax.Precision.HIGHEST) + b_ref[...]


def small_linear(x, w, b, tn=1024):
    M, K = x.shape
    N = w.shape[1]
    return pl.pallas_call(
        _linear_kernel,
        grid=(N // tn,),
        in_specs=[pl.BlockSpec((M, K), lambda j: (0, 0)),
                  pl.BlockSpec((K, tn), lambda j: (0, j)),
                  pl.BlockSpec((1, tn), lambda j: (0, j))],
        out_specs=pl.BlockSpec((M, tn), lambda j: (0, j)),
        out_shape=jax.ShapeDtypeStruct((M, N), F32),
        compiler_params=_cparams("parallel"),
        name="ada_linear",
    )(x, w, b.reshape(1, N))


def _in_proj_kernel(x_ref, sh_ref, sc_ref, g_ref, w_ref, cos_ref, sin_ref,
                    hy_ref, cf_ref, at_ref, lr_ref):
    x = x_ref[0]
    ms = jnp.mean(x * x, axis=-1, keepdims=True)
    y = x * lax.rsqrt(ms + EPS) * g_ref[...]
    y = y * (1.0 + sc_ref[0]) + sh_ref[0]
    u = jnp.dot(y.astype(BF16), w_ref[...], preferred_element_type=F32)
    c0 = HY_COLS
    c1 = c0 + CONF_COLS
    c2 = c1 + ATT_COLS
    c3 = c2 + LRU_COLS
    hy_ref[0] = u[:, :c0]
    cf_ref[0] = u[:, c0:c1]
    lr_ref[0] = u[:, c2:c3]
    qk = u[:, c1:c1 + QK_COLS]
    qk_rot = u[:, c3:c3 + QK_COLS]
    at_ref[0, :, :QK_COLS] = qk * cos_ref[...] + qk_rot * sin_ref[...]
    at_ref[0, :, QK_COLS:] = u[:, c1 + QK_COLS:c2]


def in_proj(h, shift, scale, g, w_ext, cos_t, sin_t, tm):
    B, L, D = h.shape
    NW = w_ext.shape[1]
    outs = [HY_COLS, CONF_COLS, ATT_COLS, LRU_COLS]
    return pl.pallas_call(
        _in_proj_kernel,
        grid=(B, L // tm),
        in_specs=[pl.BlockSpec((1, tm, D), lambda b, i: (b, i, 0)),
                  pl.BlockSpec((1, 1, D), lambda b, i: (b, 0, 0)),
                  pl.BlockSpec((1, 1, D), lambda b, i: (b, 0, 0)),
                  pl.BlockSpec((1, D), lambda b, i: (0, 0)),
                  pl.BlockSpec((D, NW), lambda b, i: (0, 0)),
                  pl.BlockSpec((tm, QK_COLS), lambda b, i: (i, 0)),
                  pl.BlockSpec((tm, QK_COLS), lambda b, i: (i, 0))],
        out_specs=[pl.BlockSpec((1, tm, n), lambda b, i: (b, i, 0)) for n in outs],
        out_shape=[jax.ShapeDtypeStruct((B, L, n), F32) for n in outs],
        compiler_params=_cparams("parallel", "parallel"),
        name="in_proj",
    )(h, shift, scale, g.reshape(1, D), w_ext, cos_t, sin_t)


def rope_tables(L, rotary):
    n_heads = ATT_HEADS + ATT_KV_HEADS
    if not rotary:
        return jnp.ones((L, QK_COLS), F32), jnp.zeros((L, QK_COLS), F32)
    pos = jnp.arange(L)
    row = (pos // GRID_W).astype(F32)
    col = (pos % GRID_W).astype(F32)
    half = HEAD_DIM // 2
    inv_freq = ROPE_BASE ** (-jnp.arange(0, half, 2, dtype=F32) / half)
    ang_r = row[:, None] * inv_freq[None]
    ang_c = col[:, None] * inv_freq[None]
    cos_h = jnp.concatenate([jnp.cos(ang_r)] * 2 + [jnp.cos(ang_c)] * 2, axis=-1)
    sin_h = jnp.concatenate([jnp.sin(ang_r)] * 2 + [jnp.sin(ang_c)] * 2, axis=-1)
    return jnp.tile(cos_h, (1, n_heads)), jnp.tile(sin_h, (1, n_heads))


def extend_w_in(w_in):
    c1 = HY_COLS + CONF_COLS
    wqk = w_in[:, c1:c1 + QK_COLS]
    D = w_in.shape[0]
    w4 = wqk.reshape(D, QK_COLS // 32, 2, 16)
    wrot = jnp.stack([-w4[:, :, 1], w4[:, :, 0]], axis=2).reshape(D, QK_COLS)
    return jnp.concatenate([w_in, wrot], axis=1).astype(BF16)


def _softmax_parts(q, k_list, extra_logit):
    scale = HEAD_DIM ** -0.5
    s_list = []
    for k, mask in k_list:
        s = lax.dot_general(q, k, (((1,), (1,)), ((), ())), preferred_element_type=F32) * scale
        if mask is not None:
            s = jnp.where(mask, s, NEG_INF)
        s_list.append(s)
    m = extra_logit
    for s in s_list:
        m = jnp.maximum(m, jnp.max(s, axis=-1, keepdims=True))
    p_list = [jnp.exp(s - m) for s in s_list]
    denom = jnp.exp(extra_logit - m)
    for p in p_list:
        denom = denom + jnp.sum(p, axis=-1, keepdims=True)
    return p_list, 1.0 / denom


def _win_attn_kernel(sink_ref, q_ref, kp_ref, kc_ref, kn_ref, vp_ref, vc_ref, vn_ref,
                     kx_ref, vx_ref, o_ref, *, seq_len):
    i = pl.program_id(1)
    blk = ATT_BLOCK
    q = q_ref[0].astype(BF16)
    kw = jnp.concatenate([kp_ref[0], kc_ref[0], kn_ref[0]], axis=0).astype(BF16)
    vw = jnp.concatenate([vp_ref[0], vc_ref[0], vn_ref[0]], axis=0).astype(BF16)
    kx = kx_ref[0].astype(BF16)
    vx = vx_ref[0].astype(BF16)
    q_pos = i * blk + lax.broadcasted_iota(jnp.int32, (blk, 3 * blk), 0)
    k_pos = (i - 1) * blk + lax.broadcasted_iota(jnp.int32, (blk, 3 * blk), 1)
    valid = (jnp.abs(k_pos - q_pos) <= WINDOW) & (k_pos >= 0) & (k_pos < seq_len)
    g = ATT_HEADS // ATT_KV_HEADS
    outs = []
    for h in range(ATT_HEADS):
        kv = h // g
        qs = q[:, h * HEAD_DIM:(h + 1) * HEAD_DIM]
        ksl = slice(kv * HEAD_DIM, (kv + 1) * HEAD_DIM)
        (p_win, p_ctx), inv = _softmax_parts(qs, [(kw[:, ksl], valid), (kx[:, ksl], None)], sink_ref[h])
        o = (jnp.dot(p_win.astype(BF16), vw[:, ksl], preferred_element_type=F32)
             + jnp.dot(p_ctx.astype(BF16), vx[:, ksl], preferred_element_type=F32))
        outs.append(o * inv)
    o_ref[0] = jnp.concatenate(outs, axis=-1)


def window_attention(at_x, at_c, sink):
    B, S, _ = at_x.shape
    C = at_c.shape[1]
    nb = S // ATT_BLOCK
    kcol = QK_COLS // 128 - 1
    vcol = kcol + 1
    blk = ATT_BLOCK

    def kv_spec(col, off):
        return pl.BlockSpec((1, blk, 128),
                            lambda b, i, s: (b, jnp.clip(i + off, 0, nb - 1), col))

    grid_spec = pltpu.PrefetchScalarGridSpec(
        num_scalar_prefetch=1,
        grid=(B, nb),
        in_specs=[pl.BlockSpec((1, blk, ATT_HEADS * HEAD_DIM), lambda b, i, s: (b, i, 0)),
                  kv_spec(kcol, -1), kv_spec(kcol, 0), kv_spec(kcol, 1),
                  kv_spec(vcol, -1), kv_spec(vcol, 0), kv_spec(vcol, 1),
                  pl.BlockSpec((1, C, 128), lambda b, i, s: (b, 0, kcol)),
                  pl.BlockSpec((1, C, 128), lambda b, i, s: (b, 0, vcol))],
        out_specs=pl.BlockSpec((1, blk, ATT_HEADS * HEAD_DIM), lambda b, i, s: (b, i, 0)),
    )
    return pl.pallas_call(
        functools.partial(_win_attn_kernel, seq_len=S),
        grid_spec=grid_spec,
        out_shape=jax.ShapeDtypeStruct((B, S, ATT_HEADS * HEAD_DIM), F32),
        compiler_params=_cparams("parallel", "parallel"),
        name="window_attention",
    )(sink.astype(F32), at_x, at_x, at_x, at_x, at_x, at_x, at_x, at_c, at_c)


def _ctx_attn_kernel(sink_ref, q_ref, kx_ref, vx_ref, o_ref):
    q = q_ref[0].astype(BF16)
    kx = kx_ref[0].astype(BF16)
    vx = vx_ref[0].astype(BF16)
    g = ATT_HEADS // ATT_KV_HEADS
    outs = []
    for h in range(ATT_HEADS):
        kv = h // g
        qs = q[:, h * HEAD_DIM:(h + 1) * HEAD_DIM]
        ksl = slice(kv * HEAD_DIM, (kv + 1) * HEAD_DIM)
        (p_ctx,), inv = _softmax_parts(qs, [(kx[:, ksl], None)], sink_ref[h])
        outs.append(jnp.dot(p_ctx.astype(BF16), vx[:, ksl], preferred_element_type=F32) * inv)
    o_ref[0] = jnp.concatenate(outs, axis=-1)


def context_attention(at_c, sink):
    B, C, _ = at_c.shape
    kcol = QK_COLS // 128 - 1
    grid_spec = pltpu.PrefetchScalarGridSpec(
        num_scalar_prefetch=1,
        grid=(B,),
        in_specs=[pl.BlockSpec((1, C, ATT_HEADS * HEAD_DIM), lambda b, s: (b, 0, 0)),
                  pl.BlockSpec((1, C, 128), lambda b, s: (b, 0, kcol)),
                  pl.BlockSpec((1, C, 128), lambda b, s: (b, 0, kcol + 1))],
        out_specs=pl.BlockSpec((1, C, ATT_HEADS * HEAD_DIM), lambda b, s: (b, 0, 0)),
    )
    return pl.pallas_call(
        _ctx_attn_kernel,
        grid_spec=grid_spec,
        out_shape=jax.ShapeDtypeStruct((B, C, ATT_HEADS * HEAD_DIM), F32),
        compiler_params=_cparams("parallel"),
        name="context_attention",
    )(sink.astype(F32), at_c, at_c, at_c)


def _out_proj_kernel(y0_ref, y1_ref, y2_ref, y3_ref, h_ref, g1_ref, gng_ref, w_ref,
                     n2g_ref, sh_ref, sc_ref, wr_ref, br_ref, ho_ref, nx_ref, lg_ref):
    parts = []
    for k, y_ref in enumerate((y0_ref, y1_ref, y2_ref, y3_ref)):
        y = y_ref[0]
        ms = jnp.mean(y * y, axis=-1, keepdims=True)
        yn = y * lax.rsqrt(ms + EPS) * gng_ref[:, k * D_GROUP:(k + 1) * D_GROUP]
        parts.append(yn.astype(BF16))
    yn = jnp.concatenate(parts, axis=-1)
    proj = jnp.dot(yn, w_ref[...], preferred_element_type=F32)
    h = h_ref[0] + g1_ref[0] * proj
    ho_ref[0] = h
    ms = jnp.mean(h * h, axis=-1, keepdims=True)
    n = h * lax.rsqrt(ms + EPS) * n2g_ref[...]
    n = n * (1.0 + sc_ref[0]) + sh_ref[0]
    nx_ref[0] = n
    lg_ref[0] = jnp.dot(n, wr_ref[...], preferred_element_type=F32,
                        precision=lax.Precision.HIGHEST) + br_ref[...]


def out_proj(ys, h, g1, gng, w_out, n2g, sh2, sc2, w_router, b_router, tm):
    B, L, D = h.shape
    row3 = lambda n: pl.BlockSpec((1, tm, n), lambda b, i: (b, i, 0))
    mod = pl.BlockSpec((1, 1, D), lambda b, i: (b, 0, 0))
    full = lambda r, c: pl.BlockSpec((r, c), lambda b, i: (0, 0))
    return pl.pallas_call(
        _out_proj_kernel,
        grid=(B, L // tm),
        in_specs=[row3(D_GROUP)] * 4 + [row3(D), mod, full(1, D), full(D, D), full(1, D), mod, mod,
                                        full(D, ROUTER_COLS), full(1, ROUTER_COLS)],
        out_specs=[row3(D), row3(D), row3(ROUTER_COLS)],
        out_shape=[jax.ShapeDtypeStruct((B, L, D), F32), jax.ShapeDtypeStruct((B, L, D), F32),
                   jax.ShapeDtypeStruct((B, L, ROUTER_COLS), F32)],
        compiler_params=_cparams("parallel", "parallel"),
        name="out_proj",
    )(*ys, h, g1, gng.reshape(1, D), w_out, n2g.reshape(1, D), sh2, sc2, w_router, b_router)


def _row_gather(idx_ref, src_hbm, buf, sem, n_rows):
    def body(r, carry):
        pltpu.make_async_copy(src_hbm.at[idx_ref[0, 0, r]], buf.at[r], sem).start()
        return carry
    lax.fori_loop(0, n_rows, body, 0)


def _expert_kernel(blk_exp_ref, tok_ref, tok_next_ref, x_hbm, wg_ref, wu_ref, wd_ref, o_ref,
                   xbuf, sems):
    del blk_exp_ref
    i = pl.program_id(0)
    n = pl.num_programs(0)
    slot = i % 2

    @pl.when(i == 0)
    def _():
        _row_gather(tok_ref, x_hbm, xbuf.at[0], sems.at[0], MOE_BLOCK)

    @pl.when(i + 1 < n)
    def _():
        _row_gather(tok_next_ref, x_hbm, xbuf.at[1 - slot], sems.at[1 - slot], MOE_BLOCK)

    pltpu.make_async_copy(xbuf.at[slot], xbuf.at[slot], sems.at[slot]).wait()
    xb = xbuf[slot].astype(BF16)
    gate = jnp.dot(xb, wg_ref[0], preferred_element_type=F32)
    up = jnp.dot(xb, wu_ref[0], preferred_element_type=F32)
    hid = (gate * jax.nn.sigmoid(gate) * up).astype(BF16)
    o_ref[...] = jnp.dot(hid, wd_ref[0], preferred_element_type=F32)


def expert_blocks(x_tokens, buf_tok, blk_exp, w_gate, w_up, w_down):
    T, D = x_tokens.shape
    n_blocks = buf_tok.shape[0]
    DE = w_gate.shape[-1]
    grid_spec = pltpu.PrefetchScalarGridSpec(
        num_scalar_prefetch=1,
        grid=(n_blocks,),
        in_specs=[pl.BlockSpec((1, 1, MOE_BLOCK), lambda i, e: (i, 0, 0), memory_space=pltpu.SMEM),
                  pl.BlockSpec((1, 1, MOE_BLOCK), lambda i, e: (jnp.minimum(i + 1, n_blocks - 1), 0, 0),
                               memory_space=pltpu.SMEM),
                  pl.BlockSpec(memory_space=pl.ANY),
                  pl.BlockSpec((1, D, DE), lambda i, e: (e[i], 0, 0)),
                  pl.BlockSpec((1, D, DE), lambda i, e: (e[i], 0, 0)),
                  pl.BlockSpec((1, DE, D), lambda i, e: (e[i], 0, 0))],
        out_specs=pl.BlockSpec((MOE_BLOCK, D), lambda i, e: (i, 0)),
        scratch_shapes=[pltpu.VMEM((2, MOE_BLOCK, D), F32), pltpu.SemaphoreType.DMA((2,))],
    )
    return pl.pallas_call(
        _expert_kernel,
        grid_spec=grid_spec,
        out_shape=jax.ShapeDtypeStruct((n_blocks * MOE_BLOCK, D), F32),
        compiler_params=_cparams("arbitrary"),
        name="moe_experts",
    )(blk_exp, buf_tok, buf_tok, x_tokens, w_gate, w_up, w_down)


COMBINE_TOKENS = 128


def _combine_kernel(pos_ref, pos_next_ref, o_hbm, w_ref, h_ref, g2_ref, fg_ref, out_ref,
                    obuf, sems, *, final_norm):
    i = pl.program_id(0)
    n = pl.num_programs(0)
    slot = i % 2
    rows = TOP_K * COMBINE_TOKENS

    @pl.when(i == 0)
    def _():
        _row_gather(pos_ref, o_hbm, obuf.at[0], sems.at[0], rows)

    @pl.when(i + 1 < n)
    def _():
        _row_gather(pos_next_ref, o_hbm, obuf.at[1 - slot], sems.at[1 - slot], rows)

    pltpu.make_async_copy(obuf.at[slot], obuf.at[slot], sems.at[slot]).wait()
    w = w_ref[...]
    m = (obuf[slot, :COMBINE_TOKENS] * w[:, 0:1] + obuf[slot, COMBINE_TOKENS:] * w[:, 1:2])
    h = h_ref[...] + g2_ref[0] * m
    if final_norm:
        ms = jnp.mean(h * h, axis=-1, keepdims=True)
        h = h * lax.rsqrt(ms + EPS) * fg_ref[...]
    out_ref[...] = h


def moe_combine(o_sorted, pos, w, h_tokens, g2, tokens_per_batch, final_g, final_norm):
    T, D = h_tokens.shape
    tb = COMBINE_TOKENS
    nt = T // tb
    per_b = tokens_per_batch // tb
    return pl.pallas_call(
        functools.partial(_combine_kernel, final_norm=final_norm),
        grid=(nt,),
        in_specs=[pl.BlockSpec((1, 1, TOP_K * tb), lambda i: (i, 0, 0), memory_space=pltpu.SMEM),
                  pl.BlockSpec((1, 1, TOP_K * tb), lambda i: (jnp.minimum(i + 1, nt - 1), 0, 0),
                               memory_space=pltpu.SMEM),
                  pl.BlockSpec(memory_space=pl.ANY),
                  pl.BlockSpec((tb, TOP_K), lambda i: (i, 0)),
                  pl.BlockSpec((tb, D), lambda i: (i, 0)),
                  pl.BlockSpec((1, 1, D), lambda i: (i // per_b, 0, 0)),
                  pl.BlockSpec((1, D), lambda i: (0, 0))],
        out_specs=pl.BlockSpec((tb, D), lambda i: (i, 0)),
        out_shape=jax.ShapeDtypeStruct((T, D), F32),
        scratch_shapes=[pltpu.VMEM((2, TOP_K * tb, D), F32), pltpu.SemaphoreType.DMA((2,))],
        compiler_params=_cparams("arbitrary"),
        name="moe_combine",
    )(pos, pos, o_sorted, w, h_tokens, g2, final_g.reshape(1, D))


def route(logits):
    T = logits.shape[0]
    rows = jnp.arange(T)
    g_logits = logits[:, :N_GROUPS]
    g_idx = jnp.argmax(g_logits, axis=-1)
    g_prob = jax.nn.softmax(g_logits, axis=-1)[rows, g_idx]
    e_logits = logits[:, N_GROUPS:N_GROUPS + N_EXPERTS].reshape(T, N_GROUPS, EXP_PER_GROUP)[rows, g_idx]
    top_w, top_e = lax.top_k(jax.nn.softmax(e_logits, axis=-1), TOP_K)
    top_w = top_w / jnp.sum(top_w, axis=-1, keepdims=True)
    weights = g_prob[:, None] * top_w
    experts = (g_idx[:, None] * EXP_PER_GROUP + top_e).astype(jnp.int32)
    A = T * TOP_K
    flat_e = experts.reshape(-1)
    onehot = (flat_e[:, None] == jnp.arange(N_EXPERTS, dtype=jnp.int32)[None]).astype(jnp.int32)
    rank = jnp.take_along_axis(jnp.cumsum(onehot, axis=0), flat_e[:, None], axis=1)[:, 0] - 1
    counts = jnp.sum(onehot, axis=0)
    padded = (counts + MOE_BLOCK - 1) // MOE_BLOCK * MOE_BLOCK
    pad_end = jnp.cumsum(padded)
    pad_start = pad_end - padded
    dest = (pad_start[flat_e] + rank).astype(jnp.int32)
    n_blocks = -(-A // MOE_BLOCK) + N_EXPERTS
    P = n_blocks * MOE_BLOCK
    tokens = jnp.repeat(jnp.arange(T, dtype=jnp.int32), TOP_K)
    buf_tok = jnp.zeros((P,), jnp.int32).at[dest].set(tokens)
    blk_exp = jnp.minimum(jnp.searchsorted(pad_end, jnp.arange(n_blocks) * MOE_BLOCK, side="right"),
                          N_EXPERTS - 1).astype(jnp.int32)
    tb = COMBINE_TOKENS
    pos = dest.reshape(T // tb, tb, TOP_K).transpose(0, 2, 1).reshape(T // tb, 1, TOP_K * tb)
    return weights, buf_tok.reshape(n_blocks, 1, MOE_BLOCK), blk_exp, pos


def _dw_conv(x, w, b, pad_left, pad_right):
    C = x.shape[-1]
    y = lax.conv_general_dilated(
        x, w[:, None, :].astype(x.dtype), window_strides=(1,),
        padding=[(pad_left, pad_right)], dimension_numbers=("NWC", "WIO", "NWC"),
        feature_group_count=C)
    return y + b.astype(x.dtype)


def _hyena_filters(L, lp):
    t = jnp.linspace(0.0, 1.0, L, dtype=F32)[:, None]
    bands = (HY_EMB - 1) // 2
    w = 2.0 * math.pi * jnp.arange(L, dtype=F32)[:, None] / L
    f = jnp.linspace(1e-4, bands - 1, bands, dtype=F32)[None]
    z = jnp.concatenate([t, jnp.cos(f * w), -jnp.sin(f * w)], axis=-1)
    hdn = jnp.sin(z @ lp["hy_ffn_w1"] + lp["hy_ffn_b1"])
    hdn = jnp.sin(hdn @ lp["hy_ffn_w2"] + lp["hy_ffn_b2"])
    h = (hdn @ lp["hy_ffn_w3"]).reshape(L, 2, D_GROUP)
    max_decay = math.log(HY_TARGET) / HY_FAST_DECAY
    min_decay = math.log(HY_TARGET) / HY_SLOW_DECAY
    deltas = jnp.linspace(min_decay, max_decay, D_GROUP, dtype=F32)
    h = h * jnp.exp(-t * jnp.abs(deltas))[:, None, :]
    h = h / (jnp.sum(jnp.abs(h), axis=(0, 1), keepdims=True) + EPS)
    return h[:, 0], h[:, 1]


def _hyena_mixer(u, lp):
    L, C = u.shape[1], D_GROUP
    uc = _dw_conv(u, lp["hy_short_w"], lp["hy_short_b"], 1, 1)
    x0, x1, v = jnp.split(uc, 3, axis=-1)
    h_fwd, h_bwd = _hyena_filters(L, lp)
    z = x1 * v
    k = jnp.concatenate([h_fwd, jnp.zeros((1, C), F32), h_bwd[1:][::-1]], axis=0)
    zk = jnp.fft.rfft(z, n=2 * L, axis=1) * jnp.fft.rfft(k, axis=0)[None]
    y = jnp.fft.irfft(zk, n=2 * L, axis=1)[:, :L]
    return x0 * (y + z * lp["hy_bias"])


def _conformer_conv(u, lp):
    a, gate = jnp.split(u, 2, axis=-1)
    y = a * jax.nn.sigmoid(gate)
    pad = (CONF_KERNEL - 1) // 2
    y = _dw_conv(y, lp["conf_dw_w"], lp["conf_dw_b"], pad, CONF_KERNEL - 1 - pad)
    mu = jnp.mean(y, axis=-1, keepdims=True)
    var = jnp.mean(jnp.square(y - mu), axis=-1, keepdims=True)
    y = (y - mu) * lax.rsqrt(var + EPS) * lp["conf_ln_g"] + lp["conf_ln_b"]
    return jax.nn.silu(y)


def _rglru_coeffs(x, wa, ba, wx, bx, lam):
    B, L, C = x.shape
    xb = x.reshape(B, L, LRU_HEADS, C // LRU_HEADS)
    r = jax.nn.sigmoid(jnp.einsum("blhi,hio->blho", xb, wa).reshape(B, L, C) + ba)
    i = jax.nn.sigmoid(jnp.einsum("blhi,hio->blho", xb, wx).reshape(B, L, C) + bx)
    log_a = -LRU_C * r * jax.nn.softplus(-lam)
    a = jnp.exp(log_a)
    b = jnp.sqrt(-jnp.expm1(2.0 * log_a)) * (i * x)
    return a, b


def _linear_scan(a, b, h0, reverse):
    if h0 is not None:
        edge = -1 if reverse else 0
        b = b.at[:, edge].add(a[:, edge] * h0)

    def combine(e1, e2):
        a1, b1 = e1
        a2, b2 = e2
        return a1 * a2, a2 * b1 + b2

    _, h = lax.associative_scan(combine, (a, b), reverse=reverse, axis=1)
    return h


def _rglru_mixer(uc, ux, lp, need_ctx):
    gate_c, rec_c = jnp.split(uc, 2, axis=-1)
    gate_x, rec_x = jnp.split(ux, 2, axis=-1)
    pl_ = (LRU_CONV - 1) // 2
    xc = _dw_conv(rec_c, lp["lru_conv_w"], lp["lru_conv_b"], pl_, LRU_CONV - 1 - pl_)
    xx = _dw_conv(rec_x, lp["lru_conv_w"], lp["lru_conv_b"], pl_, LRU_CONV - 1 - pl_)
    y_x = jnp.zeros_like(xx)
    y_c = jnp.zeros_like(xc)
    for d, reverse in enumerate((False, True)):
        params = (lp["lru_wa"][d], lp["lru_ba"][d], lp["lru_wx"][d], lp["lru_bx"][d], lp["lru_lambda"][d])
        a_c, b_c = _rglru_coeffs(xc, *params)
        h_c = _linear_scan(a_c, b_c, None, reverse)
        h0 = h_c[:, 0] if reverse else h_c[:, -1]
        a_x, b_x = _rglru_coeffs(xx, *params)
        y_x = y_x + _linear_scan(a_x, b_x, h0, reverse)
        if need_ctx:
            y_c = y_c + h_c
    out_x = jax.nn.gelu(gate_x) * y_x
    out_c = jax.nn.gelu(gate_c) * y_c if need_ctx else None
    return out_c, out_x


def _layer(hc, hx, c_silu_all, lp, need_ctx, final_g, final_norm):
    B, S, D = hx.shape
    C = hc.shape[1]
    mod = small_linear(c_silu_all, lp["ada_w"], lp["ada_b"])
    mod_x = mod[:B].reshape(B, 6, 1, D)
    mod_c = jnp.broadcast_to(mod[B].reshape(1, 6, 1, D), (B, 6, 1, D))
    w_ext = extend_w_in(lp["w_in"])
    cos_x, sin_x = rope_tables(S, True)
    cos_c, sin_c = rope_tables(C, False)
    hy_x, cf_x, at_x, lr_x = in_proj(hx, mod_x[:, 0], mod_x[:, 1], lp["norm1_g"], w_ext, cos_x, sin_x, tm=256)
    hy_c, cf_c, at_c, lr_c = in_proj(hc, mod_c[:, 0], mod_c[:, 1], lp["norm1_g"], w_ext, cos_c, sin_c, tm=256)

    yd_c, yd_x = _rglru_mixer(lr_c, lr_x, lp, need_ctx)
    ys_x = [_hyena_mixer(hy_x, lp), _conformer_conv(cf_x, lp),
            window_attention(at_x, at_c, lp["attn_sink"]), yd_x]

    w_out = lp["w_out"].astype(BF16)
    w_router = jnp.zeros((D, ROUTER_COLS), F32)
    w_router = w_router.at[:, :N_GROUPS].set(lp["router_g_w"]).at[:, N_GROUPS:N_GROUPS + N_EXPERTS].set(lp["router_e_w"])
    b_router = jnp.zeros((1, ROUTER_COLS), F32)
    b_router = b_router.at[0, :N_GROUPS].set(lp["router_g_b"]).at[0, N_GROUPS:N_GROUPS + N_EXPERTS].set(lp["router_e_b"])

    hx1, nx, lg_x = out_proj(ys_x, hx, mod_x[:, 2], lp["group_norm_g"], w_out, lp["norm2_g"],
                             mod_x[:, 3], mod_x[:, 4], w_router, b_router, tm=256)
    h_tok = hx1.reshape(B * S, D)
    n_tok = nx.reshape(B * S, D)
    lg = lg_x.reshape(B * S, ROUTER_COLS)
    if need_ctx:
        ys_c = [_hyena_mixer(hy_c, lp), _conformer_conv(cf_c, lp),
                context_attention(at_c, lp["attn_sink"]), yd_c]
        hc1, nc, lg_c = out_proj(ys_c, hc, mod_c[:, 2], lp["group_norm_g"], w_out, lp["norm2_g"],
                                 mod_c[:, 3], mod_c[:, 4], w_router, b_router, tm=256)
        n_tok = jnp.concatenate([n_tok, nc.reshape(B * C, D)], axis=0)
        lg = jnp.concatenate([lg, lg_c.reshape(B * C, ROUTER_COLS)], axis=0)

    weights, buf_tok, blk_exp, pos = route(lg)
    o_sorted = expert_blocks(n_tok, buf_tok, blk_exp, lp["exp_w_gate"].astype(BF16),
                             lp["exp_w_up"].astype(BF16), lp["exp_w_down"].astype(BF16))
    nt_x = B * S // COMBINE_TOKENS
    hx2 = moe_combine(o_sorted, pos[:nt_x], weights[:B * S], h_tok, mod_x[:, 5], S, final_g, final_norm)
    hx2 = hx2.reshape(B, S, D)
    if need_ctx:
        hc2 = moe_combine(o_sorted, pos[nt_x:], weights[B * S:], hc1.reshape(B * C, D), mod_c[:, 5], C,
                          final_g, False).reshape(B, C, D)
    else:
        hc2 = hc
    return hc2, hx2


def kernel(x, c, ctx, c_ctx, norm1_g, norm2_g, ada_w, ada_b, w_in, hy_short_w, hy_short_b, hy_ffn_w1, hy_ffn_b1, hy_ffn_w2, hy_ffn_b2, hy_ffn_w3, hy_bias, conf_dw_w, conf_dw_b, conf_ln_g, conf_ln_b, attn_sink, lru_conv_w, lru_conv_b, lru_wa, lru_ba, lru_wx, lru_bx, lru_lambda, group_norm_g, w_out, router_g_w, router_g_b, router_e_w, router_e_b, exp_w_gate, exp_w_up, exp_w_down, final_norm_g):
    stacked = dict(norm1_g=norm1_g, norm2_g=norm2_g, ada_w=ada_w, ada_b=ada_b, w_in=w_in,
                   hy_short_w=hy_short_w, hy_short_b=hy_short_b, hy_ffn_w1=hy_ffn_w1, hy_ffn_b1=hy_ffn_b1,
                   hy_ffn_w2=hy_ffn_w2, hy_ffn_b2=hy_ffn_b2, hy_ffn_w3=hy_ffn_w3, hy_bias=hy_bias,
                   conf_dw_w=conf_dw_w, conf_dw_b=conf_dw_b, conf_ln_g=conf_ln_g, conf_ln_b=conf_ln_b,
                   attn_sink=attn_sink, lru_conv_w=lru_conv_w, lru_conv_b=lru_conv_b, lru_wa=lru_wa,
                   lru_ba=lru_ba, lru_wx=lru_wx, lru_bx=lru_bx, lru_lambda=lru_lambda,
                   group_norm_g=group_norm_g, w_out=w_out, router_g_w=router_g_w, router_g_b=router_g_b,
                   router_e_w=router_e_w, router_e_b=router_e_b, exp_w_gate=exp_w_gate,
                   exp_w_up=exp_w_up, exp_w_down=exp_w_down)
    depth = norm1_g.shape[0]
    B = x.shape[0]
    cs = jnp.concatenate([jax.nn.silu(c), jnp.broadcast_to(jax.nn.silu(c_ctx)[None], (8, c.shape[1]))], axis=0)
    hc, hx = ctx, x
    for l in range(depth):
        lp = {k: v[l] for k, v in stacked.items()}
        hc, hx = _layer(hc, hx, cs, lp, need_ctx=(l < depth - 1), final_g=final_norm_g,
                        final_norm=(l == depth - 1))
    return hx
```

```python
import functools
import math

import jax
import jax.numpy as jnp
from jax import lax
from jax.experimental import pallas as pl
from jax.experimental.pallas import tpu as pltpu

F32 = jnp.float32
BF16 = jnp.bfloat16

EPS = 1e-6
NEG_INF = -1e30
GRID_W = 64
N_MIXERS = 4
D_GROUP = 256
HY_COLS = 3 * D_GROUP
CONF_COLS = 2 * D_GROUP
ATT_HEADS = 4
ATT_KV_HEADS = 2
HEAD_DIM = 64
ATT_COLS = (ATT_HEADS + 2 * ATT_KV_HEADS) * HEAD_DIM
LRU_COLS = 2 * D_GROUP
QK_COLS = (ATT_HEADS + ATT_KV_HEADS) * HEAD_DIM
WINDOW = 128
ATT_BLOCK = 128
ROPE_BASE = 10000.0
HY_EMB = 33
HY_FAST_DECAY = 0.3
HY_SLOW_DECAY = 1.5
HY_TARGET = 1e-2
CONF_KERNEL = 31
LRU_HEADS = 4
LRU_CONV = 4
LRU_C = 8.0
N_GROUPS = 4
EXP_PER_GROUP = 8
N_EXPERTS = N_GROUPS * EXP_PER_GROUP
TOP_K = 2
MOE_BLOCK = 256
ROUTER_COLS = 128

VMEM_LIMIT_BYTES = 56 * 1024 * 1024


def _cparams(*sem):
    return pltpu.CompilerParams(dimension_semantics=sem, vmem_limit_bytes=VMEM_LIMIT_BYTES)


def _linear_kernel(x_ref, w_ref, b_ref, o_ref):
    o_ref[...] = jnp.dot(x_ref[...], w_ref[...], preferred_element_type=F32,
                         precision=lax.Precision.HIGHEST) + b_ref[...]


def small_linear(x, w, b, tn=1024):
    M, K = x.shape
    N = w.shape[1]
    return pl.pallas_call(
        _linear_kernel,
        grid=(N // tn,),
        in_specs=[pl.BlockSpec((M, K), lambda j: (0, 0)),
                  pl.BlockSpec((K, tn), lambda j: (0, j)),
                  pl.BlockSpec((1, tn), lambda j: (0, j))],
        out_specs=pl.BlockSpec((M, tn), lambda j: (0, j)),
        out_shape=jax.ShapeDtypeStruct((M, N), F32),
        compiler_params=_cparams("parallel"),
        name="ada_linear",
    )(x, w, b.reshape(1, N))


def _in_proj_kernel(x_ref, sh_ref, sc_ref, g_ref, w_ref, cos_ref, sin_ref,
                    hy_ref, cf_ref, at_ref, lr_ref):
    x = x_ref[0]
    ms = jnp.mean(x * x, axis=-1, keepdims=True)
    y = x * lax.rsqrt(ms + EPS) * g_ref[...]
    y = y * (1.0 + sc_ref[0]) + sh_ref[0]
    u = jnp.dot(y.astype(BF16), w_ref[...], preferred_element_type=F32)
    c0 = HY_COLS
    c1 = c0 + CONF_COLS
    c2 = c1 + ATT_COLS
    c3 = c2 + LRU_COLS
    hy_ref[0] = u[:, :c0]
    cf_ref[0] = u[:, c0:c1]
    lr_ref[0] = u[:, c2:c3]
    qk = u[:, c1:c1 + QK_COLS]
    qk_rot = u[:, c3:c3 + QK_COLS]
    at_ref[0, :, :QK_COLS] = qk * cos_ref[...] + qk_rot * sin_ref[...]
    at_ref[0, :, QK_COLS:] = u[:, c1 + QK_COLS:c2]


def in_proj(h, shift, scale, g, w_ext, cos_t, sin_t, tm):
    B, L, D = h.shape
    NW = w_ext.shape[1]
    outs = [HY_COLS, CONF_COLS, ATT_COLS, LRU_COLS]
    return pl.pallas_call(
        _in_proj_kernel,
        grid=(B, L // tm),
        in_specs=[pl.BlockSpec((1, tm, D), lambda b, i: (b, i, 0)),
                  pl.BlockSpec((1, 1, D), lambda b, i: (b, 0, 0)),
                  pl.BlockSpec((1, 1, D), lambda b, i: (b, 0, 0)),
                  pl.BlockSpec((1, D), lambda b, i: (0, 0)),
                  pl.BlockSpec((D, NW), lambda b, i: (0, 0)),
                  pl.BlockSpec((tm, QK_COLS), lambda b, i: (i, 0)),
                  pl.BlockSpec((tm, QK_COLS), lambda b, i: (i, 0))],
        out_specs=[pl.BlockSpec((1, tm, n), lambda b, i: (b, i, 0)) for n in outs],
        out_shape=[jax.ShapeDtypeStruct((B, L, n), F32) for n in outs],
        compiler_params=_cparams("parallel", "parallel"),
        name="in_proj",
    )(h, shift, scale, g.reshape(1, D), w_ext, cos_t, sin_t)


def rope_tables(L, rotary):
    n_heads = ATT_HEADS + ATT_KV_HEADS
    if not rotary:
        return jnp.ones((L, QK_COLS), F32), jnp.zeros((L, QK_COLS), F32)
    pos = jnp.arange(L)
    row = (pos // GRID_W).astype(F32)
    col = (pos % GRID_W).astype(F32)
    half = HEAD_DIM // 2
    inv_freq = ROPE_BASE ** (-jnp.arange(0, half, 2, dtype=F32) / half)
    ang_r = row[:, None] * inv_freq[None]
    ang_c = col[:, None] * inv_freq[None]
    cos_h = jnp.concatenate([jnp.cos(ang_r)] * 2 + [jnp.cos(ang_c)] * 2, axis=-1)
    sin_h = jnp.concatenate([jnp.sin(ang_r)] * 2 + [jnp.sin(ang_c)] * 2, axis=-1)
    return jnp.tile(cos_h, (1, n_heads)), jnp.tile(sin_h, (1, n_heads))


def extend_w_in(w_in):
    c1 = HY_COLS + CONF_COLS
    wqk = w_in[:, c1:c1 + QK_COLS]
    D = w_in.shape[0]
    w4 = wqk.reshape(D, QK_COLS // 32, 2, 16)
    wrot = jnp.stack([-w4[:, :, 1], w4[:, :, 0]], axis=2).reshape(D, QK_COLS)
    return jnp.concatenate([w_in, wrot], axis=1).astype(BF16)


def _softmax_parts(q, k_list, extra_logit):
    scale = HEAD_DIM ** -0.5
    s_list = []
    for k, mask in k_list:
        s = lax.dot_general(q, k, (((1,), (1,)), ((), ())), preferred_element_type=F32) * scale
        if mask is not None:
            s = jnp.where(mask, s, NEG_INF)
        s_list.append(s)
    m = extra_logit
    for s in s_list:
        m = jnp.maximum(m, jnp.max(s, axis=-1, keepdims=True))
    p_list = [jnp.exp(s - m) for s in s_list]
    denom = jnp.exp(extra_logit - m)
    for p in p_list:
        denom = denom + jnp.sum(p, axis=-1, keepdims=True)
    return p_list, 1.0 / denom


def _win_attn_kernel(sink_ref, q_ref, kp_ref, kc_ref, kn_ref, vp_ref, vc_ref, vn_ref,
                     kx_ref, vx_ref, o_ref, *, seq_len):
    i = pl.program_id(1)
    blk = ATT_BLOCK
    q = q_ref[0].astype(BF16)
    kw = jnp.concatenate([kp_ref[0], kc_ref[0], kn_ref[0]], axis=0).astype(BF16)
    vw = jnp.concatenate([vp_ref[0], vc_ref[0], vn_ref[0]], axis=0).astype(BF16)
    kx = kx_ref[0].astype(BF16)
    vx = vx_ref[0].astype(BF16)
    q_pos = i * blk + lax.broadcasted_iota(jnp.int32, (blk, 3 * blk), 0)
    k_pos = (i - 1) * blk + lax.broadcasted_iota(jnp.int32, (blk, 3 * blk), 1)
    valid = (jnp.abs(k_pos - q_pos) <= WINDOW) & (k_pos >= 0) & (k_pos < seq_len)
    g = ATT_HEADS // ATT_KV_HEADS
    outs = []
    for h in range(ATT_HEADS):
        kv = h // g
        qs = q[:, h * HEAD_DIM:(h + 1) * HEAD_DIM]
        ksl = slice(kv * HEAD_DIM, (kv + 1) * HEAD_DIM)
        (p_win, p_ctx), inv = _softmax_parts(qs, [(kw[:, ksl], valid), (kx[:, ksl], None)], sink_ref[h])
        o = (jnp.dot(p_win.astype(BF16), vw[:, ksl], preferred_element_type=F32)
             + jnp.dot(p_ctx.astype(BF16), vx[:, ksl], preferred_element_type=F32))
        outs.append(o * inv)
    o_ref[0] = jnp.concatenate(outs, axis=-1)


def window_attention(at_x, at_c, sink):
    B, S, _ = at_x.shape
    C = at_c.shape[1]
    nb = S // ATT_BLOCK
    kcol = QK_COLS // 128 - 1
    vcol = kcol + 1
    blk = ATT_BLOCK

    def kv_spec(col, off):
        return pl.BlockSpec((1, blk, 128),
                            lambda b, i, s: (b, jnp.clip(i + off, 0, nb - 1), col))

    grid_spec = pltpu.PrefetchScalarGridSpec(
        num_scalar_prefetch=1,
        grid=(B, nb),
        in_specs=[pl.BlockSpec((1, blk, ATT_HEADS * HEAD_DIM), lambda b, i, s: (b, i, 0)),
                  kv_spec(kcol, -1), kv_spec(kcol, 0), kv_spec(kcol, 1),
                  kv_spec(vcol, -1), kv_spec(vcol, 0), kv_spec(vcol, 1),
                  pl.BlockSpec((1, C, 128), lambda b, i, s: (b, 0, kcol)),
                  pl.BlockSpec((1, C, 128), lambda b, i, s: (b, 0, vcol))],
        out_specs=pl.BlockSpec((1, blk, ATT_HEADS * HEAD_DIM), lambda b, i, s: (b, i, 0)),
    )
    return pl.pallas_call(
        functools.partial(_win_attn_kernel, seq_len=S),
        grid_spec=grid_spec,
        out_shape=jax.ShapeDtypeStruct((B, S, ATT_HEADS * HEAD_DIM), F32),
        compiler_params=_cparams("parallel", "parallel"),
        name="window_attention",
    )(sink.astype(F32), at_x, at_x, at_x, at_x, at_x, at_x, at_x, at_c, at_c)


def _ctx_attn_kernel(sink_ref, q_ref, kx_ref, vx_ref, o_ref):
    q = q_ref[0].astype(BF16)
    kx = kx_ref[0].astype(BF16)
    vx = vx_ref[0].astype(BF16)
    g = ATT_HEADS // ATT_KV_HEADS
    outs = []
    for h in range(ATT_HEADS):
        kv = h // g
        qs = q[:, h * HEAD_DIM:(h + 1) * HEAD_DIM]
        ksl = slice(kv * HEAD_DIM, (kv + 1) * HEAD_DIM)
        (p_ctx,), inv = _softmax_parts(qs, [(kx[:, ksl], None)], sink_ref[h])
        outs.append(jnp.dot(p_ctx.astype(BF16), vx[:, ksl], preferred_element_type=F32) * inv)
    o_ref[0] = jnp.concatenate(outs, axis=-1)


def context_attention(at_c, sink):
    B, C, _ = at_c.shape
    kcol = QK_COLS // 128 - 1
    grid_spec = pltpu.PrefetchScalarGridSpec(
        num_scalar_prefetch=1,
        grid=(B,),
        in_specs=[pl.BlockSpec((1, C, ATT_HEADS * HEAD_DIM), lambda b, s: (b, 0, 0)),
                  pl.BlockSpec((1, C, 128), lambda b, s: (b, 0, kcol)),
                  pl.BlockSpec((1, C, 128), lambda b, s: (b, 0, kcol + 1))],
        out_specs=pl.BlockSpec((1, C, ATT_HEADS * HEAD_DIM), lambda b, s: (b, 0, 0)),
    )
    return pl.pallas_call(
        _ctx_attn_kernel,
        grid_spec=grid_spec,
        out_shape=jax.ShapeDtypeStruct((B, C, ATT_HEADS * HEAD_DIM), F32),
        compiler_params=_cparams("parallel"),
        name="context_attention",
    )(sink.astype(F32), at_c, at_c, at_c)


def _out_proj_kernel(y0_ref, y1_ref, y2_ref, y3_ref, h_ref, g1_ref, gng_ref, w_ref,
                     n2g_ref, sh_ref, sc_ref, wr_ref, br_ref, ho_ref, nx_ref, lg_ref):
    parts = []
    for k, y_ref in enumerate((y0_ref, y1_ref, y2_ref, y3_ref)):
        y = y_ref[0]
        ms = jnp.mean(y * y, axis=-1, keepdims=True)
        yn = y * lax.rsqrt(ms + EPS) * gng_ref[:, k * D_GROUP:(k + 1) * D_GROUP]
        parts.append(yn.astype(BF16))
    yn = jnp.concatenate(parts, axis=-1)
    proj = jnp.dot(yn, w_ref[...], preferred_element_type=F32)
    h = h_ref[0] + g1_ref[0] * proj
    ho_ref[0] = h
    ms = jnp.mean(h * h, axis=-1, keepdims=True)
    n = h * lax.rsqrt(ms + EPS) * n2g_ref[...]
    n = n * (1.0 + sc_ref[0]) + sh_ref[0]
    nx_ref[0] = n
    lg_ref[0] = jnp.dot(n, wr_ref[...], preferred_element_type=F32,
                        precision=lax.Precision.HIGHEST) + br_ref[...]


def out_proj(ys, h, g1, gng, w_out, n2g, sh2, sc2, w_router, b_router, tm):
    B, L, D = h.shape
    row3 = lambda n: pl.BlockSpec((1, tm, n), lambda b, i: (b, i, 0))
    mod = pl.BlockSpec((1, 1, D), lambda b, i: (b, 0, 0))
    full = lambda r, c: pl.BlockSpec((r, c), lambda b, i: (0, 0))
    return pl.pallas_call(
        _out_proj_kernel,
        grid=(B, L // tm),
        in_specs=[row3(D_GROUP)] * 4 + [row3(D), mod, full(1, D), full(D, D), full(1, D), mod, mod,
                                        full(D, ROUTER_COLS), full(1, ROUTER_COLS)],
        out_specs=[row3(D), row3(D), row3(ROUTER_COLS)],
        out_shape=[jax.ShapeDtypeStruct((B, L, D), F32), jax.ShapeDtypeStruct((B, L, D), F32),
                   jax.ShapeDtypeStruct((B, L, ROUTER_COLS), F32)],
        compiler_params=_cparams("parallel", "parallel"),
        name="out_proj",
    )(*ys, h, g1, gng.reshape(1, D), w_out, n2g.reshape(1, D), sh2, sc2, w_router, b_router)


def _row_gather(idx_ref, src_hbm, buf, sem, n_rows):
    def body(r, carry):
        pltpu.make_async_copy(src_hbm.at[idx_ref[0, 0, r]], buf.at[r], sem).start()
        return carry
    lax.fori_loop(0, n_rows, body, 0)


def _expert_kernel(blk_exp_ref, tok_ref, tok_next_ref, x_hbm, wg_ref, wu_ref, wd_ref, o_ref,
                   xbuf, sems):
    del blk_exp_ref
    i = pl.program_id(0)
    n = pl.num_programs(0)
    slot = i % 2

    @pl.when(i == 0)
    def _():
        _row_gather(tok_ref, x_hbm, xbuf.at[0], sems.at[0], MOE_BLOCK)

    @pl.when(i + 1 < n)
    def _():
        _row_gather(tok_next_ref, x_hbm, xbuf.at[1 - slot], sems.at[1 - slot], MOE_BLOCK)

    pltpu.make_async_copy(xbuf.at[slot], xbuf.at[slot], sems.at[slot]).wait()
    xb = xbuf[slot].astype(BF16)
    gate = jnp.dot(xb, wg_ref[0], preferred_element_type=F32)
    up = jnp.dot(xb, wu_ref[0], preferred_element_type=F32)
    hid = (gate * jax.nn.sigmoid(gate) * up).astype(BF16)
    o_ref[...] = jnp.dot(hid, wd_ref[0], preferred_element_type=F32)


def expert_blocks(x_tokens, buf_tok, blk_exp, w_gate, w_up, w_down):
    T, D = x_tokens.shape
    n_blocks = buf_tok.shape[0]
    DE = w_gate.shape[-1]
    grid_spec = pltpu.PrefetchScalarGridSpec(
        num_scalar_prefetch=1,
        grid=(n_blocks,),
        in_specs=[pl.BlockSpec((1, 1, MOE_BLOCK), lambda i, e: (i, 0, 0), memory_space=pltpu.SMEM),
                  pl.BlockSpec((1, 1, MOE_BLOCK), lambda i, e: (jnp.minimum(i + 1, n_blocks - 1), 0, 0),
                               memory_space=pltpu.SMEM),
                  pl.BlockSpec(memory_space=pl.ANY),
                  pl.BlockSpec((1, D, DE), lambda i, e: (e[i], 0, 0)),
                  pl.BlockSpec((1, D, DE), lambda i, e: (e[i], 0, 0)),
                  pl.BlockSpec((1, DE, D), lambda i, e: (e[i], 0, 0))],
        out_specs=pl.BlockSpec((MOE_BLOCK, D), lambda i, e: (i, 0)),
        scratch_shapes=[pltpu.VMEM((2, MOE_BLOCK, D), F32), pltpu.SemaphoreType.DMA((2,))],
    )
    return pl.pallas_call(
        _expert_kernel,
        grid_spec=grid_spec,
        out_shape=jax.ShapeDtypeStruct((n_blocks * MOE_BLOCK, D), F32),
        compiler_params=_cparams("arbitrary"),
        name="moe_experts",
    )(blk_exp, buf_tok, buf_tok, x_tokens, w_gate, w_up, w_down)


COMBINE_TOKENS = 128


def _combine_kernel(pos_ref, pos_next_ref, o_hbm, w_ref, h_ref, g2_ref, fg_ref, out_ref,
                    obuf, sems, *, final_norm):
    i = pl.program_id(0)
    n = pl.num_programs(0)
    slot = i % 2
    rows = TOP_K * COMBINE_TOKENS

    @pl.when(i == 0)
    def _():
        _row_gather(pos_ref, o_hbm, obuf.at[0], sems.at[0], rows)

    @pl.when(i + 1 < n)
    def _():
        _row_gather(pos_next_ref, o_hbm, obuf.at[1 - slot], sems.at[1 - slot], rows)

    pltpu.make_async_copy(obuf.at[slot], obuf.at[slot], sems.at[slot]).wait()
    w = w_ref[...]
    m = (obuf[slot, :COMBINE_TOKENS] * w[:, 0:1] + obuf[slot, COMBINE_TOKENS:] * w[:, 1:2])
    h = h_ref[...] + g2_ref[0] * m
    if final_norm:
        ms = jnp.mean(h * h, axis=-1, keepdims=True)
        h = h * lax.rsqrt(ms + EPS) * fg_ref[...]
    out_ref[...] = h


def moe_combine(o_sorted, pos, w, h_tokens, g2, tokens_per_batch, final_g, final_norm):
    T, D = h_tokens.shape
    tb = COMBINE_TOKENS
    nt = T // tb
    per_b = tokens_per_batch // tb
    return pl.pallas_call(
        functools.partial(_combine_kernel, final_norm=final_norm),
        grid=(nt,),
        in_specs=[pl.BlockSpec((1, 1, TOP_K * tb), lambda i: (i, 0, 0), memory_space=pltpu.SMEM),
                  pl.BlockSpec((1, 1, TOP_K * tb), lambda i: (jnp.minimum(i + 1, nt - 1), 0, 0),
                               memory_space=pltpu.SMEM),
                  pl.BlockSpec(memory_space=pl.ANY),
                  pl.BlockSpec((tb, TOP_K), lambda i: (i, 0)),
                  pl.BlockSpec((tb, D), lambda i: (i, 0)),
                  pl.BlockSpec((1, 1, D), lambda i: (i // per_b, 0, 0)),
                  pl.BlockSpec((1, D), lambda i: (0, 0))],
        out_specs=pl.BlockSpec((tb, D), lambda i: (i, 0)),
        out_shape=jax.ShapeDtypeStruct((T, D), F32),
        scratch_shapes=[pltpu.VMEM((2, TOP_K * tb, D), F32), pltpu.SemaphoreType.DMA((2,))],
        compiler_params=_cparams("arbitrary"),
        name="moe_combine",
    )(pos, pos, o_sorted, w, h_tokens, g2, final_g.reshape(1, D))


def route(logits):
    T = logits.shape[0]
    rows = jnp.arange(T)
    g_logits = logits[:, :N_GROUPS]
    g_idx = jnp.argmax(g_logits, axis=-1)
    g_prob = jax.nn.softmax(g_logits, axis=-1)[rows, g_idx]
    e_logits = logits[:, N_GROUPS:N_GROUPS + N_EXPERTS].reshape(T, N_GROUPS, EXP_PER_GROUP)[rows, g_idx]
    top_w, top_e = lax.top_k(jax.nn.softmax(e_logits, axis=-1), TOP_K)
    top_w = top_w / jnp.sum(top_w, axis=-1, keepdims=True)
    weights = g_prob[:, None] * top_w
    experts = (g_idx[:, None] * EXP_PER_GROUP + top_e).astype(jnp.int32)
    A = T * TOP_K
    flat_e = experts.reshape(-1)
    onehot = (flat_e[:, None] == jnp.arange(N_EXPERTS, dtype=jnp.int32)[None]).astype(jnp.int32)
    rank = jnp.take_along_axis(jnp.cumsum(onehot, axis=0), flat_e[:, None], axis=1)[:, 0] - 1
    counts = jnp.sum(onehot, axis=0)
    padded = (counts + MOE_BLOCK - 1) // MOE_BLOCK * MOE_BLOCK
    pad_end = jnp.cumsum(padded)
    pad_start = pad_end - padded
    dest = (pad_start[flat_e] + rank).astype(jnp.int32)
    n_blocks = -(-A // MOE_BLOCK) + N_EXPERTS
    P = n_blocks * MOE_BLOCK
    tokens = jnp.repeat(jnp.arange(T, dtype=jnp.int32), TOP_K)
    buf_tok = jnp.zeros((P,), jnp.int32).at[dest].set(tokens)
    blk_exp = jnp.minimum(jnp.searchsorted(pad_end, jnp.arange(n_blocks) * MOE_BLOCK, side="right"),
                          N_EXPERTS - 1).astype(jnp.int32)
    tb = COMBINE_TOKENS
    pos = dest.reshape(T // tb, tb, TOP_K).transpose(0, 2, 1).reshape(T // tb, 1, TOP_K * tb)
    return weights, buf_tok.reshape(n_blocks, 1, MOE_BLOCK), blk_exp, pos


CONV_MARGIN = 16


def _time_chunk(L):
    return min(L, 256)


LANES = 128


def _zero_margins(pad_ref, L):
    zeros = jnp.zeros((CONV_MARGIN, LANES), F32)
    for s in range(pad_ref.shape[0]):
        pad_ref[s, pl.ds(0, CONV_MARGIN), :] = zeros
        pad_ref[s, pl.ds(CONV_MARGIN + L, CONV_MARGIN), :] = zeros


def _dw_conv_slab(pad_ref, s, base, T, w_ref, b_ref, col, taps, pad_left):
    acc = jnp.broadcast_to(b_ref[:, col:col + LANES], (T, LANES))
    for k in range(taps):
        acc = acc + w_ref[k:k + 1, col:col + LANES] * pad_ref[s, pl.ds(base + (CONV_MARGIN - pad_left + k), T), :]
    return acc


def _conformer_kernel(u_ref, w_ref, b_ref, g_ref, beta_ref, o_ref, ypad):
    L = o_ref.shape[1]
    T = _time_chunk(L)
    C = D_GROUP
    n_slabs = C // LANES
    pad = (CONF_KERNEL - 1) // 2
    _zero_margins(ypad, L)

    def glu(j, carry):
        base = pl.multiple_of(j * T, T)
        for s in range(n_slabs):
            a = u_ref[0, pl.ds(base, T), s * LANES:(s + 1) * LANES]
            gate = u_ref[0, pl.ds(base, T), C + s * LANES:C + (s + 1) * LANES]
            ypad[s, pl.ds(CONV_MARGIN + base, T), :] = a * jax.nn.sigmoid(gate)
        return carry
    lax.fori_loop(0, L // T, glu, 0)

    def conv(j, carry):
        base = pl.multiple_of(j * T, T)
        acc = jnp.concatenate([_dw_conv_slab(ypad, s, base, T, w_ref, b_ref, s * LANES, CONF_KERNEL, pad)
                               for s in range(n_slabs)], axis=-1)
        mu = jnp.mean(acc, axis=-1, keepdims=True)
        cen = acc - mu
        var = jnp.mean(cen * cen, axis=-1, keepdims=True)
        y = cen * lax.rsqrt(var + EPS) * g_ref[...] + beta_ref[...]
        o_ref[0, pl.ds(base, T), :] = y * jax.nn.sigmoid(y)
        return carry
    lax.fori_loop(0, L // T, conv, 0)


def conformer_conv(u, w, b, ln_g, ln_b):
    B, L, _ = u.shape
    C = D_GROUP
    vec = pl.BlockSpec((1, C), lambda i: (0, 0))
    return pl.pallas_call(
        _conformer_kernel,
        grid=(B,),
        in_specs=[pl.BlockSpec((1, L, 2 * C), lambda i: (i, 0, 0)),
                  pl.BlockSpec((CONF_KERNEL, C), lambda i: (0, 0)), vec, vec, vec],
        out_specs=pl.BlockSpec((1, L, C), lambda i: (i, 0, 0)),
        out_shape=jax.ShapeDtypeStruct((B, L, C), F32),
        scratch_shapes=[pltpu.VMEM((C // LANES, L + 2 * CONV_MARGIN, LANES), F32)],
        compiler_params=_cparams("parallel"),
        name="conformer_conv",
    )(u, w, b.reshape(1, C), ln_g.reshape(1, C), ln_b.reshape(1, C))


def _gelu_tanh(x):
    return 0.5 * x * (1.0 + jnp.tanh(math.sqrt(2.0 / math.pi) * (x + 0.044715 * (x * x * x))))


def _lru_kernel(uc_ref, ux_ref, cw_ref, cb_ref, wcat_ref, bcat_ref, lam_ref, *rest, need_ctx):
    if need_ctx:
        oc_ref, ox_ref, cpad, xpad, a_s, b_s, yx, yc = rest
    else:
        ox_ref, cpad, xpad, a_s, b_s, yx = rest
        oc_ref = yc = None
    C = D_GROUP
    n_slabs = C // LANES
    Lc = uc_ref.shape[1]
    Lx = ux_ref.shape[1]
    pad_l = (LRU_CONV - 1) // 2

    def fill(pad_ref, u_ref, L):
        T = _time_chunk(L)
        _zero_margins(pad_ref, L)

        def body(j, carry):
            base = pl.multiple_of(j * T, T)
            for s in range(n_slabs):
                pad_ref[s, pl.ds(CONV_MARGIN + base, T), :] = u_ref[0, pl.ds(base, T),
                                                                    C + s * LANES:C + (s + 1) * LANES]
            return carry
        lax.fori_loop(0, L // T, body, 0)

    fill(cpad, uc_ref, Lc)
    fill(xpad, ux_ref, Lx)

    def coeffs(pad_ref, base, T, d):
        x = jnp.concatenate([_dw_conv_slab(pad_ref, s, base, T, cw_ref, cb_ref, s * LANES, LRU_CONV, pad_l)
                             for s in range(n_slabs)], axis=-1)
        g = jnp.dot(x.astype(BF16), wcat_ref[:, 2 * d * C:2 * (d + 1) * C],
                    preferred_element_type=F32) + bcat_ref[:, 2 * d * C:2 * (d + 1) * C]
        r = jax.nn.sigmoid(g[:, :C])
        i = jax.nn.sigmoid(g[:, C:])
        z = -lam_ref[d:d + 1, :]
        softplus = jnp.maximum(z, 0.0) + jnp.log(1.0 + jnp.exp(-jnp.abs(z)))
        a = jnp.exp(-LRU_C * r * softplus)
        b = jnp.sqrt(1.0 - a * a) * (i * x)
        for s in range(n_slabs):
            a_s[s, pl.ds(0, T), :] = a[:, s * LANES:(s + 1) * LANES]
            b_s[s, pl.ds(0, T), :] = b[:, s * LANES:(s + 1) * LANES]

    def scan_rows(T, reverse, h, y_ref, base, first):
        def body(step, h):
            t = (T - 1 - step) if reverse else step
            new = []
            for s in range(n_slabs):
                hs = a_s[s, pl.ds(t, 1), :] * h[s] + b_s[s, pl.ds(t, 1), :]
                if y_ref is not None:
                    if first:
                        y_ref[s, pl.ds(base + t, 1), :] = hs
                    else:
                        y_ref[s, pl.ds(base + t, 1), :] = y_ref[s, pl.ds(base + t, 1), :] + hs
                new.append(hs)
            return tuple(new)
        return lax.fori_loop(0, T, body, h, unroll=8)

    def run(pad_ref, L, d, h, y_ref):
        T = _time_chunk(L)
        n = L // T
        reverse = d == 1

        def body(j, h):
            jj = (n - 1 - j) if reverse else j
            base = pl.multiple_of(jj * T, T)
            coeffs(pad_ref, base, T, d)
            return scan_rows(T, reverse, h, y_ref, base, d == 0)
        return lax.fori_loop(0, n, body, h)

    for d in range(2):
        h = tuple(jnp.zeros((1, LANES), F32) for _ in range(n_slabs))
        h = run(cpad, Lc, d, h, yc)
        run(xpad, Lx, d, h, yx)

    def finish(u_ref, y_ref, o_ref, L):
        T = _time_chunk(L)

        def body(j, carry):
            base = pl.multiple_of(j * T, T)
            y = jnp.concatenate([y_ref[s, pl.ds(base, T), :] for s in range(n_slabs)], axis=-1)
            o_ref[0, pl.ds(base, T), :] = _gelu_tanh(u_ref[0, pl.ds(base, T), :C]) * y
            return carry
        lax.fori_loop(0, L // T, body, 0)

    finish(ux_ref, yx, ox_ref, Lx)
    if need_ctx:
        finish(uc_ref, yc, oc_ref, Lc)


def _block_diag(w):
    H, n, _ = w.shape
    eye = jnp.eye(H, dtype=w.dtype)
    return (eye[:, None, :, None] * w[:, :, None, :]).reshape(H * n, H * n)


def rglru_mixer(uc, ux, lp, need_ctx):
    B, Lc, _ = uc.shape
    Lx = ux.shape[1]
    C = D_GROUP
    wcat = jnp.concatenate([_block_diag(lp["lru_wa"][0]), _block_diag(lp["lru_wx"][0]),
                            _block_diag(lp["lru_wa"][1]), _block_diag(lp["lru_wx"][1])], axis=1).astype(BF16)
    bcat = jnp.concatenate([lp["lru_ba"][0], lp["lru_bx"][0], lp["lru_ba"][1], lp["lru_bx"][1]]).reshape(1, 4 * C)
    full = lambda r, c: pl.BlockSpec((r, c), lambda i: (0, 0))
    seq = lambda L, n: pl.BlockSpec((1, L, n), lambda i: (i, 0, 0))
    out_specs = [seq(Lx, C)]
    out_shape = [jax.ShapeDtypeStruct((B, Lx, C), F32)]
    if need_ctx:
        out_specs = [seq(Lc, C)] + out_specs
        out_shape = [jax.ShapeDtypeStruct((B, Lc, C), F32)] + out_shape
    T = _time_chunk(Lx)
    slab = lambda rows: pltpu.VMEM((C // LANES, rows, LANES), F32)
    scratch = [slab(Lc + 2 * CONV_MARGIN), slab(Lx + 2 * CONV_MARGIN), slab(T), slab(T), slab(Lx)]
    if need_ctx:
        scratch.append(slab(Lc))
    res = pl.pallas_call(
        functools.partial(_lru_kernel, need_ctx=need_ctx),
        grid=(B,),
        in_specs=[seq(Lc, 2 * C), seq(Lx, 2 * C), full(LRU_CONV, C), full(1, C), full(C, 4 * C),
                  full(1, 4 * C), full(2, C)],
        out_specs=out_specs,
        out_shape=out_shape,
        scratch_shapes=scratch,
        compiler_params=_cparams("parallel"),
        name="rglru",
    )(uc, ux, lp["lru_conv_w"], lp["lru_conv_b"].reshape(1, C), wcat, bcat, lp["lru_lambda"])
    if need_ctx:
        return res[0], res[1]
    return None, res[0]


HY_SHORT = 3


def _short_conv(pad_ref, base, T, w_ref, b_ref, c0, c1):
    return jnp.concatenate([_dw_conv_slab(pad_ref, col // LANES, base, T, w_ref, b_ref, col, HY_SHORT, 1)
                            for col in range(c0, c1, LANES)], axis=-1)


def _fill_padded(pad_ref, u_ref, L, T):
    _zero_margins(pad_ref, L)

    def body(j, carry):
        base = pl.multiple_of(j * T, T)
        for s in range(pad_ref.shape[0]):
            pad_ref[s, pl.ds(CONV_MARGIN + base, T), :] = u_ref[0, pl.ds(base, T), s * LANES:(s + 1) * LANES]
        return carry
    lax.fori_loop(0, L // T, body, 0)


def _hyena_pre_kernel(u_ref, w_ref, b_ref, z_ref, upad):
    L = u_ref.shape[1]
    T = _time_chunk(L)
    C = D_GROUP
    _fill_padded(upad, u_ref, L, T)

    def body(j, carry):
        base = pl.multiple_of(j * T, T)
        x1 = _short_conv(upad, base, T, w_ref, b_ref, C, 2 * C)
        v = _short_conv(upad, base, T, w_ref, b_ref, 2 * C, 3 * C)
        z_ref[pl.ds(base, T), :] = (x1 * v).astype(BF16)
        return carry
    lax.fori_loop(0, L // T, body, 0)


def _hyena_post_kernel(u_ref, y_ref, w_ref, b_ref, bias_ref, o_ref, upad):
    L = u_ref.shape[1]
    T = _time_chunk(L)
    C = D_GROUP
    _fill_padded(upad, u_ref, L, T)

    def body(j, carry):
        base = pl.multiple_of(j * T, T)
        x0 = _short_conv(upad, base, T, w_ref, b_ref, 0, C)
        x1 = _short_conv(upad, base, T, w_ref, b_ref, C, 2 * C)
        v = _short_conv(upad, base, T, w_ref, b_ref, 2 * C, 3 * C)
        o_ref[0, pl.ds(base, T), :] = x0 * (y_ref[pl.ds(base, T), :] + (x1 * v) * bias_ref[...])
        return carry
    lax.fori_loop(0, L // T, body, 0)


def _spectrum_kernel(f_ref, z_ref, ha_ref, hb_ref, hc_ref, y_ref):
    tf = ha_ref.shape[0]
    acc = jnp.dot(f_ref[...], z_ref[...], preferred_element_type=F32)
    zr = acc[:tf]
    zi = acc[tf:]
    y_ref[:tf, :] = (zr * ha_ref[...] - zi * hb_ref[...]).astype(BF16)
    y_ref[tf:, :] = (zr * hb_ref[...] + zi * hc_ref[...]).astype(BF16)


def _idft_kernel(f_ref, y_ref, o_ref):
    o_ref[...] = jnp.dot(f_ref[...], y_ref[...], preferred_element_type=F32)


def dft_tables(L):
    N = 2 * L
    tf = min(256, L)
    k = jnp.arange(L, dtype=jnp.int32)
    n = jnp.arange(L, dtype=jnp.int32)
    ang = (2.0 * math.pi / N) * ((k[:, None] * n[None, :]) % N).astype(F32)
    cos = jnp.cos(ang)
    sin = jnp.sin(ang)
    nyq = jnp.where(n % 2 == 0, 1.0, -1.0).astype(F32)
    f_re = cos
    f_im = (-sin).at[0].set(nyq)
    fwd = jnp.stack([f_re.reshape(L // tf, tf, L), f_im.reshape(L // tf, tf, L)], axis=1).reshape(N, L)
    ck = jnp.where(k == 0, 1.0, 2.0).astype(F32)[:, None] / N
    i_re = cos * ck
    i_im = (-sin * ck).at[0].set(nyq / N)
    inv = jnp.stack([i_re.reshape(L // tf, tf, L), i_im.reshape(L // tf, tf, L)], axis=1).reshape(N, L).T
    return fwd.astype(BF16), inv.astype(BF16)


def filter_spectrum(h_fwd, h_bwd):
    L, C = h_fwd.shape
    k = jnp.concatenate([h_fwd, jnp.zeros((1, C), F32), h_bwd[1:][::-1]], axis=0)
    hf = jnp.fft.rfft(k, axis=0)
    hr = jnp.real(hf)
    hi = jnp.imag(hf)
    a = hr[:L]
    b = hi[:L].at[0].set(0.0)
    c = hr[:L].at[0].set(hr[L])
    return a, b, c


def hyena_mixer(u, lp, tables):
    B, L, _ = u.shape
    C = D_GROUP
    N = 2 * L
    fwd, inv = tables
    tf = min(256, L)
    T = _time_chunk(L)
    w, bsh = lp["hy_short_w"], lp["hy_short_b"].reshape(1, 3 * C)
    z2 = pl.pallas_call(
        _hyena_pre_kernel,
        grid=(B,),
        in_specs=[pl.BlockSpec((1, L, 3 * C), lambda b: (b, 0, 0)),
                  pl.BlockSpec((HY_SHORT, 3 * C), lambda b: (0, 0)),
                  pl.BlockSpec((1, 3 * C), lambda b: (0, 0))],
        out_specs=pl.BlockSpec((L, C), lambda b: (0, b)),
        out_shape=jax.ShapeDtypeStruct((L, B * C), BF16),
        scratch_shapes=[pltpu.VMEM((3 * C // LANES, L + 2 * CONV_MARGIN, LANES), F32)],
        compiler_params=_cparams("parallel"),
        name="hyena_pre",
    )(u, w, bsh)

    h_fwd, h_bwd = _hyena_filters(L, lp)
    tn = 2 * C
    ha, hb, hc = [jnp.tile(t, (1, tn // C)) for t in filter_spectrum(h_fwd, h_bwd)]
    hspec = pl.BlockSpec((tf, tn), lambda i, j: (i, 0))
    y2 = pl.pallas_call(
        _spectrum_kernel,
        grid=(L // tf, B * C // tn),
        in_specs=[pl.BlockSpec((2 * tf, L), lambda i, j: (i, 0)),
                  pl.BlockSpec((L, tn), lambda i, j: (0, j)), hspec, hspec, hspec],
        out_specs=pl.BlockSpec((2 * tf, tn), lambda i, j: (i, j)),
        out_shape=jax.ShapeDtypeStruct((N, B * C), BF16),
        compiler_params=_cparams("parallel", "parallel"),
        name="hyena_spectrum",
    )(fwd, z2, ha, hb, hc)

    tl = min(256, L)
    yt = pl.pallas_call(
        _idft_kernel,
        grid=(L // tl, B * C // tn),
        in_specs=[pl.BlockSpec((tl, N), lambda i, j: (i, 0)),
                  pl.BlockSpec((N, tn), lambda i, j: (0, j))],
        out_specs=pl.BlockSpec((tl, tn), lambda i, j: (i, j)),
        out_shape=jax.ShapeDtypeStruct((L, B * C), F32),
        compiler_params=_cparams("parallel", "parallel"),
        name="hyena_idft",
    )(inv, y2)

    return pl.pallas_call(
        _hyena_post_kernel,
        grid=(B,),
        in_specs=[pl.BlockSpec((1, L, 3 * C), lambda b: (b, 0, 0)),
                  pl.BlockSpec((L, C), lambda b: (0, b)),
                  pl.BlockSpec((HY_SHORT, 3 * C), lambda b: (0, 0)),
                  pl.BlockSpec((1, 3 * C), lambda b: (0, 0)),
                  pl.BlockSpec((1, C), lambda b: (0, 0))],
        out_specs=pl.BlockSpec((1, L, C), lambda b: (b, 0, 0)),
        out_shape=jax.ShapeDtypeStruct((B, L, C), F32),
        scratch_shapes=[pltpu.VMEM((3 * C // LANES, L + 2 * CONV_MARGIN, LANES), F32)],
        compiler_params=_cparams("parallel"),
        name="hyena_post",
    )(u, yt, w, bsh, lp["hy_bias"].reshape(1, C))


def _hyena_filters(L, lp):
    t = jnp.linspace(0.0, 1.0, L, dtype=F32)[:, None]
    bands = (HY_EMB - 1) // 2
    w = 2.0 * math.pi * jnp.arange(L, dtype=F32)[:, None] / L
    f = jnp.linspace(1e-4, bands - 1, bands, dtype=F32)[None]
    z = jnp.concatenate([t, jnp.cos(f * w), -jnp.sin(f * w)], axis=-1)
    hdn = jnp.sin(z @ lp["hy_ffn_w1"] + lp["hy_ffn_b1"])
    hdn = jnp.sin(hdn @ lp["hy_ffn_w2"] + lp["hy_ffn_b2"])
    h = (hdn @ lp["hy_ffn_w3"]).reshape(L, 2, D_GROUP)
    max_decay = math.log(HY_TARGET) / HY_FAST_DECAY
    min_decay = math.log(HY_TARGET) / HY_SLOW_DECAY
    deltas = jnp.linspace(min_decay, max_decay, D_GROUP, dtype=F32)
    h = h * jnp.exp(-t * jnp.abs(deltas))[:, None, :]
    h = h / (jnp.sum(jnp.abs(h), axis=(0, 1), keepdims=True) + EPS)
    return h[:, 0], h[:, 1]


def _layer(hc, hx, c_silu_all, lp, need_ctx, final_g, final_norm, tables_x, tables_c):
    B, S, D = hx.shape
    C = hc.shape[1]
    mod = small_linear(c_silu_all, lp["ada_w"], lp["ada_b"])
    mod_x = mod[:B].reshape(B, 6, 1, D)
    mod_c = jnp.broadcast_to(mod[B].reshape(1, 6, 1, D), (B, 6, 1, D))
    w_ext = extend_w_in(lp["w_in"])
    cos_x, sin_x = rope_tables(S, True)
    cos_c, sin_c = rope_tables(C, False)
    hy_x, cf_x, at_x, lr_x = in_proj(hx, mod_x[:, 0], mod_x[:, 1], lp["norm1_g"], w_ext, cos_x, sin_x, tm=256)
    hy_c, cf_c, at_c, lr_c = in_proj(hc, mod_c[:, 0], mod_c[:, 1], lp["norm1_g"], w_ext, cos_c, sin_c, tm=256)

    yd_c, yd_x = rglru_mixer(lr_c, lr_x, lp, need_ctx)
    conf = lambda u: conformer_conv(u, lp["conf_dw_w"], lp["conf_dw_b"], lp["conf_ln_g"], lp["conf_ln_b"])
    ys_x = [hyena_mixer(hy_x, lp, tables_x), conf(cf_x),
            window_attention(at_x, at_c, lp["attn_sink"]), yd_x]

    w_out = lp["w_out"].astype(BF16)
    w_router = jnp.zeros((D, ROUTER_COLS), F32)
    w_router = w_router.at[:, :N_GROUPS].set(lp["router_g_w"]).at[:, N_GROUPS:N_GROUPS + N_EXPERTS].set(lp["router_e_w"])
    b_router = jnp.zeros((1, ROUTER_COLS), F32)
    b_router = b_router.at[0, :N_GROUPS].set(lp["router_g_b"]).at[0, N_GROUPS:N_GROUPS + N_EXPERTS].set(lp["router_e_b"])

    hx1, nx, lg_x = out_proj(ys_x, hx, mod_x[:, 2], lp["group_norm_g"], w_out, lp["norm2_g"],
                             mod_x[:, 3], mod_x[:, 4], w_router, b_router, tm=256)
    h_tok = hx1.reshape(B * S, D)
    n_tok = nx.reshape(B * S, D)
    lg = lg_x.reshape(B * S, ROUTER_COLS)
    if need_ctx:
        ys_c = [hyena_mixer(hy_c, lp, tables_c), conf(cf_c),
                context_attention(at_c, lp["attn_sink"]), yd_c]
        hc1, nc, lg_c = out_proj(ys_c, hc, mod_c[:, 2], lp["group_norm_g"], w_out, lp["norm2_g"],
                                 mod_c[:, 3], mod_c[:, 4], w_router, b_router, tm=256)
        n_tok = jnp.concatenate([n_tok, nc.reshape(B * C, D)], axis=0)
        lg = jnp.concatenate([lg, lg_c.reshape(B * C, ROUTER_COLS)], axis=0)

    weights, buf_tok, blk_exp, pos = route(lg)
    o_sorted = expert_blocks(n_tok, buf_tok, blk_exp, lp["exp_w_gate"].astype(BF16),
                             lp["exp_w_up"].astype(BF16), lp["exp_w_down"].astype(BF16))
    nt_x = B * S // COMBINE_TOKENS
    hx2 = moe_combine(o_sorted, pos[:nt_x], weights[:B * S], h_tok, mod_x[:, 5], S, final_g, final_norm)
    hx2 = hx2.reshape(B, S, D)
    if need_ctx:
        hc2 = moe_combine(o_sorted, pos[nt_x:], weights[B * S:], hc1.reshape(B * C, D), mod_c[:, 5], C,
                          final_g, False).reshape(B, C, D)
    else:
        hc2 = hc
    return hc2, hx2


def kernel(x, c, ctx, c_ctx, norm1_g, norm2_g, ada_w, ada_b, w_in, hy_short_w, hy_short_b, hy_ffn_w1, hy_ffn_b1, hy_ffn_w2, hy_ffn_b2, hy_ffn_w3, hy_bias, conf_dw_w, conf_dw_b, conf_ln_g, conf_ln_b, attn_sink, lru_conv_w, lru_conv_b, lru_wa, lru_ba, lru_wx, lru_bx, lru_lambda, group_norm_g, w_out, router_g_w, router_g_b, router_e_w, router_e_b, exp_w_gate, exp_w_up, exp_w_down, final_norm_g):
    stacked = dict(norm1_g=norm1_g, norm2_g=norm2_g, ada_w=ada_w, ada_b=ada_b, w_in=w_in,
                   hy_short_w=hy_short_w, hy_short_b=hy_short_b, hy_ffn_w1=hy_ffn_w1, hy_ffn_b1=hy_ffn_b1,
                   hy_ffn_w2=hy_ffn_w2, hy_ffn_b2=hy_ffn_b2, hy_ffn_w3=hy_ffn_w3, hy_bias=hy_bias,
                   conf_dw_w=conf_dw_w, conf_dw_b=conf_dw_b, conf_ln_g=conf_ln_g, conf_ln_b=conf_ln_b,
                   attn_sink=attn_sink, lru_conv_w=lru_conv_w, lru_conv_b=lru_conv_b, lru_wa=lru_wa,
                   lru_ba=lru_ba, lru_wx=lru_wx, lru_bx=lru_bx, lru_lambda=lru_lambda,
                   group_norm_g=group_norm_g, w_out=w_out, router_g_w=router_g_w, router_g_b=router_g_b,
                   router_e_w=router_e_w, router_e_b=router_e_b, exp_w_gate=exp_w_gate,
                   exp_w_up=exp_w_up, exp_w_down=exp_w_down)
    depth = norm1_g.shape[0]
    B = x.shape[0]
    cs = jnp.concatenate([jax.nn.silu(c), jnp.broadcast_to(jax.nn.silu(c_ctx)[None], (8, c.shape[1]))], axis=0)
    hc, hx = ctx, x
    tables_x = dft_tables(x.shape[1])
    tables_c = dft_tables(ctx.shape[1])
    for l in range(depth):
        lp = {k: v[l] for k, v in stacked.items()}
        hc, hx = _layer(hc, hx, cs, lp, need_ctx=(l < depth - 1), final_g=final_norm_g,
                        final_norm=(l == depth - 1), tables_x=tables_x, tables_c=tables_c)
    return hx
```

```python
import functools
import math

import jax
import jax.numpy as jnp
from jax import lax
from jax.experimental import pallas as pl
from jax.experimental.pallas import tpu as pltpu

F32 = jnp.float32
BF16 = jnp.bfloat16

EPS = 1e-6
NEG_INF = -1e30
GRID_W = 64
N_MIXERS = 4
D_GROUP = 256
HY_COLS = 3 * D_GROUP
CONF_COLS = 2 * D_GROUP
ATT_HEADS = 4
ATT_KV_HEADS = 2
HEAD_DIM = 64
ATT_COLS = (ATT_HEADS + 2 * ATT_KV_HEADS) * HEAD_DIM
LRU_COLS = 2 * D_GROUP
QK_COLS = (ATT_HEADS + ATT_KV_HEADS) * HEAD_DIM
WINDOW = 128
ATT_BLOCK = 128
ROPE_BASE = 10000.0
HY_EMB = 33
HY_FAST_DECAY = 0.3
HY_SLOW_DECAY = 1.5
HY_TARGET = 1e-2
CONF_KERNEL = 31
LRU_HEADS = 4
LRU_CONV = 4
LRU_C = 8.0
N_GROUPS = 4
EXP_PER_GROUP = 8
N_EXPERTS = N_GROUPS * EXP_PER_GROUP
TOP_K = 2
MOE_BLOCK = 256
ROUTER_COLS = 128

VMEM_LIMIT_BYTES = 56 * 1024 * 1024


def _cparams(*sem):
    return pltpu.CompilerParams(dimension_semantics=sem, vmem_limit_bytes=VMEM_LIMIT_BYTES)


def _linear_kernel(x_ref, w_ref, b_ref, o_ref):
    o_ref[...] = jnp.dot(x_ref[...], w_ref[...], preferred_element_type=F32,
                         precision=lax.Precision.HIGHEST) + b_ref[...]


def small_linear(x, w, b, tn=1024):
    M, K = x.shape
    N = w.shape[1]
    return pl.pallas_call(
        _linear_kernel,
        grid=(N // tn,),
        in_specs=[pl.BlockSpec((M, K), lambda j: (0, 0)),
                  pl.BlockSpec((K, tn), lambda j: (0, j)),
                  pl.BlockSpec((1, tn), lambda j: (0, j))],
        out_specs=pl.BlockSpec((M, tn), lambda j: (0, j)),
        out_shape=jax.ShapeDtypeStruct((M, N), F32),
        compiler_params=_cparams("parallel"),
        name="ada_linear",
    )(x, w, b.reshape(1, N))


def _in_proj_kernel(x_ref, sh_ref, sc_ref, g_ref, w_ref, cos_ref, sin_ref,
                    hy_ref, cf_ref, at_ref, lr_ref):
    x = x_ref[0]
    ms = jnp.mean(x * x, axis=-1, keepdims=True)
    y = x * lax.rsqrt(ms + EPS) * g_ref[...]
    y = y * (1.0 + sc_ref[0]) + sh_ref[0]
    u = jnp.dot(y.astype(BF16), w_ref[...], preferred_element_type=F32)
    c0 = HY_COLS
    c1 = c0 + CONF_COLS
    c2 = c1 + ATT_COLS
    c3 = c2 + LRU_COLS
    hy_ref[0] = u[:, :c0]
    cf_ref[0] = u[:, c0:c1]
    lr_ref[0] = u[:, c2:c3]
    qk = u[:, c1:c1 + QK_COLS]
    qk_rot = u[:, c3:c3 + QK_COLS]
    at_ref[0, :, :QK_COLS] = qk * cos_ref[...] + qk_rot * sin_ref[...]
    at_ref[0, :, QK_COLS:] = u[:, c1 + QK_COLS:c2]


def in_proj(h, shift, scale, g, w_ext, cos_t, sin_t, tm):
    B, L, D = h.shape
    NW = w_ext.shape[1]
    outs = [HY_COLS, CONF_COLS, ATT_COLS, LRU_COLS]
    return pl.pallas_call(
        _in_proj_kernel,
        grid=(B, L // tm),
        in_specs=[pl.BlockSpec((1, tm, D), lambda b, i: (b, i, 0)),
                  pl.BlockSpec((1, 1, D), lambda b, i: (b, 0, 0)),
                  pl.BlockSpec((1, 1, D), lambda b, i: (b, 0, 0)),
                  pl.BlockSpec((1, D), lambda b, i: (0, 0)),
                  pl.BlockSpec((D, NW), lambda b, i: (0, 0)),
                  pl.BlockSpec((tm, QK_COLS), lambda b, i: (i, 0)),
                  pl.BlockSpec((tm, QK_COLS), lambda b, i: (i, 0))],
        out_specs=[pl.BlockSpec((1, tm, n), lambda b, i: (b, i, 0)) for n in outs],
        out_shape=[jax.ShapeDtypeStruct((B, L, n), F32) for n in outs],
        compiler_params=_cparams("parallel", "parallel"),
        name="in_proj",
    )(h, shift, scale, g.reshape(1, D), w_ext, cos_t, sin_t)


def rope_tables(L, rotary):
    n_heads = ATT_HEADS + ATT_KV_HEADS
    if not rotary:
        return jnp.ones((L, QK_COLS), F32), jnp.zeros((L, QK_COLS), F32)
    pos = jnp.arange(L)
    row = (pos // GRID_W).astype(F32)
    col = (pos % GRID_W).astype(F32)
    half = HEAD_DIM // 2
    inv_freq = ROPE_BASE ** (-jnp.arange(0, half, 2, dtype=F32) / half)
    ang_r = row[:, None] * inv_freq[None]
    ang_c = col[:, None] * inv_freq[None]
    cos_h = jnp.concatenate([jnp.cos(ang_r)] * 2 + [jnp.cos(ang_c)] * 2, axis=-1)
    sin_h = jnp.concatenate([jnp.sin(ang_r)] * 2 + [jnp.sin(ang_c)] * 2, axis=-1)
    return jnp.tile(cos_h, (1, n_heads)), jnp.tile(sin_h, (1, n_heads))


def extend_w_in(w_in):
    c1 = HY_COLS + CONF_COLS
    wqk = w_in[:, c1:c1 + QK_COLS]
    D = w_in.shape[0]
    w4 = wqk.reshape(D, QK_COLS // 32, 2, 16)
    wrot = jnp.stack([-w4[:, :, 1], w4[:, :, 0]], axis=2).reshape(D, QK_COLS)
    return jnp.concatenate([w_in, wrot], axis=1).astype(BF16)


def _softmax_parts(q, k_list, extra_logit):
    scale = HEAD_DIM ** -0.5
    s_list = []
    for k, mask in k_list:
        s = lax.dot_general(q, k, (((1,), (1,)), ((), ())), preferred_element_type=F32) * scale
        if mask is not None:
            s = jnp.where(mask, s, NEG_INF)
        s_list.append(s)
    m = extra_logit
    for s in s_list:
        m = jnp.maximum(m, jnp.max(s, axis=-1, keepdims=True))
    p_list = [jnp.exp(s - m) for s in s_list]
    denom = jnp.exp(extra_logit - m)
    for p in p_list:
        denom = denom + jnp.sum(p, axis=-1, keepdims=True)
    return p_list, 1.0 / denom


def _win_attn_kernel(sink_ref, q_ref, kp_ref, kc_ref, kn_ref, vp_ref, vc_ref, vn_ref,
                     kx_ref, vx_ref, o_ref, *, seq_len):
    i = pl.program_id(1)
    blk = ATT_BLOCK
    q = q_ref[0].astype(BF16)
    kw = jnp.concatenate([kp_ref[0], kc_ref[0], kn_ref[0]], axis=0).astype(BF16)
    vw = jnp.concatenate([vp_ref[0], vc_ref[0], vn_ref[0]], axis=0).astype(BF16)
    kx = kx_ref[0].astype(BF16)
    vx = vx_ref[0].astype(BF16)
    q_pos = i * blk + lax.broadcasted_iota(jnp.int32, (blk, 3 * blk), 0)
    k_pos = (i - 1) * blk + lax.broadcasted_iota(jnp.int32, (blk, 3 * blk), 1)
    valid = (jnp.abs(k_pos - q_pos) <= WINDOW) & (k_pos >= 0) & (k_pos < seq_len)
    g = ATT_HEADS // ATT_KV_HEADS
    outs = []
    for h in range(ATT_HEADS):
        kv = h // g
        qs = q[:, h * HEAD_DIM:(h + 1) * HEAD_DIM]
        ksl = slice(kv * HEAD_DIM, (kv + 1) * HEAD_DIM)
        (p_win, p_ctx), inv = _softmax_parts(qs, [(kw[:, ksl], valid), (kx[:, ksl], None)], sink_ref[h])
        o = (jnp.dot(p_win.astype(BF16), vw[:, ksl], preferred_element_type=F32)
             + jnp.dot(p_ctx.astype(BF16), vx[:, ksl], preferred_element_type=F32))
        outs.append(o * inv)
    o_ref[0] = jnp.concatenate(outs, axis=-1)


def window_attention(at_x, at_c, sink):
    B, S, _ = at_x.shape
    C = at_c.shape[1]
    nb = S // ATT_BLOCK
    kcol = QK_COLS // 128 - 1
    vcol = kcol + 1
    blk = ATT_BLOCK

    def kv_spec(col, off):
        return pl.BlockSpec((1, blk, 128),
                            lambda b, i, s: (b, jnp.clip(i + off, 0, nb - 1), col))

    grid_spec = pltpu.PrefetchScalarGridSpec(
        num_scalar_prefetch=1,
        grid=(B, nb),
        in_specs=[pl.BlockSpec((1, blk, ATT_HEADS * HEAD_DIM), lambda b, i, s: (b, i, 0)),
                  kv_spec(kcol, -1), kv_spec(kcol, 0), kv_spec(kcol, 1),
                  kv_spec(vcol, -1), kv_spec(vcol, 0), kv_spec(vcol, 1),
                  pl.BlockSpec((1, C, 128), lambda b, i, s: (b, 0, kcol)),
                  pl.BlockSpec((1, C, 128), lambda b, i, s: (b, 0, vcol))],
        out_specs=pl.BlockSpec((1, blk, ATT_HEADS * HEAD_DIM), lambda b, i, s: (b, i, 0)),
    )
    return pl.pallas_call(
        functools.partial(_win_attn_kernel, seq_len=S),
        grid_spec=grid_spec,
        out_shape=jax.ShapeDtypeStruct((B, S, ATT_HEADS * HEAD_DIM), F32),
        compiler_params=_cparams("parallel", "parallel"),
        name="window_attention",
    )(sink.astype(F32), at_x, at_x, at_x, at_x, at_x, at_x, at_x, at_c, at_c)


def _ctx_attn_kernel(sink_ref, q_ref, kx_ref, vx_ref, o_ref):
    q = q_ref[0].astype(BF16)
    kx = kx_ref[0].astype(BF16)
    vx = vx_ref[0].astype(BF16)
    g = ATT_HEADS // ATT_KV_HEADS
    outs = []
    for h in range(ATT_HEADS):
        kv = h // g
        qs = q[:, h * HEAD_DIM:(h + 1) * HEAD_DIM]
        ksl = slice(kv * HEAD_DIM, (kv + 1) * HEAD_DIM)
        (p_ctx,), inv = _softmax_parts(qs, [(kx[:, ksl], None)], sink_ref[h])
        outs.append(jnp.dot(p_ctx.astype(BF16), vx[:, ksl], preferred_element_type=F32) * inv)
    o_ref[0] = jnp.concatenate(outs, axis=-1)


def context_attention(at_c, sink):
    B, C, _ = at_c.shape
    kcol = QK_COLS // 128 - 1
    grid_spec = pltpu.PrefetchScalarGridSpec(
        num_scalar_prefetch=1,
        grid=(B,),
        in_specs=[pl.BlockSpec((1, C, ATT_HEADS * HEAD_DIM), lambda b, s: (b, 0, 0)),
                  pl.BlockSpec((1, C, 128), lambda b, s: (b, 0, kcol)),
                  pl.BlockSpec((1, C, 128), lambda b, s: (b, 0, kcol + 1))],
        out_specs=pl.BlockSpec((1, C, ATT_HEADS * HEAD_DIM), lambda b, s: (b, 0, 0)),
    )
    return pl.pallas_call(
        _ctx_attn_kernel,
        grid_spec=grid_spec,
        out_shape=jax.ShapeDtypeStruct((B, C, ATT_HEADS * HEAD_DIM), F32),
        compiler_params=_cparams("parallel"),
        name="context_attention",
    )(sink.astype(F32), at_c, at_c, at_c)


def _out_proj_kernel(y0_ref, y1_ref, y2_ref, y3_ref, h_ref, g1_ref, gng_ref, w_ref,
                     n2g_ref, sh_ref, sc_ref, wr_ref, br_ref, ho_ref, nx_ref, lg_ref):
    parts = []
    for k, y_ref in enumerate((y0_ref, y1_ref, y2_ref, y3_ref)):
        y = y_ref[0]
        ms = jnp.mean(y * y, axis=-1, keepdims=True)
        yn = y * lax.rsqrt(ms + EPS) * gng_ref[:, k * D_GROUP:(k + 1) * D_GROUP]
        parts.append(yn.astype(BF16))
    yn = jnp.concatenate(parts, axis=-1)
    proj = jnp.dot(yn, w_ref[...], preferred_element_type=F32)
    h = h_ref[0] + g1_ref[0] * proj
    ho_ref[0] = h
    ms = jnp.mean(h * h, axis=-1, keepdims=True)
    n = h * lax.rsqrt(ms + EPS) * n2g_ref[...]
    n = n * (1.0 + sc_ref[0]) + sh_ref[0]
    nx_ref[0] = n
    lg_ref[0] = jnp.dot(n, wr_ref[...], preferred_element_type=F32,
                        precision=lax.Precision.HIGHEST) + br_ref[...]


def out_proj(ys, h, g1, gng, w_out, n2g, sh2, sc2, w_router, b_router, tm):
    B, L, D = h.shape
    row3 = lambda n: pl.BlockSpec((1, tm, n), lambda b, i: (b, i, 0))
    mod = pl.BlockSpec((1, 1, D), lambda b, i: (b, 0, 0))
    full = lambda r, c: pl.BlockSpec((r, c), lambda b, i: (0, 0))
    return pl.pallas_call(
        _out_proj_kernel,
        grid=(B, L // tm),
        in_specs=[row3(D_GROUP)] * 4 + [row3(D), mod, full(1, D), full(D, D), full(1, D), mod, mod,
                                        full(D, ROUTER_COLS), full(1, ROUTER_COLS)],
        out_specs=[row3(D), row3(D), row3(ROUTER_COLS)],
        out_shape=[jax.ShapeDtypeStruct((B, L, D), F32), jax.ShapeDtypeStruct((B, L, D), F32),
                   jax.ShapeDtypeStruct((B, L, ROUTER_COLS), F32)],
        compiler_params=_cparams("parallel", "parallel"),
        name="out_proj",
    )(*ys, h, g1, gng.reshape(1, D), w_out, n2g.reshape(1, D), sh2, sc2, w_router, b_router)


N_PAIRS = EXP_PER_GROUP * (EXP_PER_GROUP - 1) // 2
N_CLASSES = N_GROUPS * N_PAIRS
ROUTE_TOKENS = 512
INFO_CLASS, INFO_RANK, INFO_WA, INFO_WB = 0, 1, 2, 3


def _route_kernel(lg_ref, info_ref, cnt_ref, run):
    i = pl.program_id(0)
    tb = lg_ref.shape[0]

    @pl.when(i == 0)
    def _():
        run[...] = jnp.zeros_like(run)

    lg = lg_ref[...]
    li = lax.broadcasted_iota(jnp.int32, lg.shape, 1)
    big = jnp.int32(ROUTER_COLS)

    def first_argmax(vals):
        m = jnp.max(vals, axis=-1, keepdims=True)
        return m, jnp.min(jnp.where(vals == m, li, big), axis=-1, keepdims=True)

    gl = jnp.where(li < N_GROUPS, lg, NEG_INF)
    gmax, g_idx = first_argmax(gl)
    g_prob = 1.0 / jnp.sum(jnp.exp(gl - gmax), axis=-1, keepdims=True)
    lo = N_GROUPS + EXP_PER_GROUP * g_idx
    el = jnp.where((li >= lo) & (li < lo + EXP_PER_GROUP), lg, NEG_INF)
    m1, i1 = first_argmax(el)
    m2, i2 = first_argmax(jnp.where(li == i1, NEG_INF, el))
    e2 = jnp.exp(m2 - m1)
    w1 = g_prob / (1.0 + e2)
    w2 = g_prob * e2 / (1.0 + e2)
    j1 = i1 - lo
    j2 = i2 - lo
    a = jnp.minimum(j1, j2)
    b = jnp.maximum(j1, j2)
    cls = g_idx * N_PAIRS + ((a * (2 * EXP_PER_GROUP - 1 - a)) >> 1) + (b - a - 1)
    w_a = jnp.where(j1 < j2, w1, w2)
    w_b = jnp.where(j1 < j2, w2, w1)

    hit = li == cls
    onehot = jnp.where(hit, 1.0, 0.0)
    r_i = lax.broadcasted_iota(jnp.int32, (tb, tb), 0)
    c_i = lax.broadcasted_iota(jnp.int32, (tb, tb), 1)
    below = jnp.where(c_i < r_i, 1.0, 0.0).astype(BF16)
    before = jnp.dot(below, onehot.astype(BF16), preferred_element_type=F32)
    rank = jnp.sum(jnp.where(hit, before + run[...], 0.0), axis=-1, keepdims=True)
    run[...] = run[...] + jnp.sum(onehot, axis=0, keepdims=True)
    cnt_ref[...] = run[...]
    info = jnp.where(li == INFO_CLASS, cls.astype(F32), 0.0)
    info = jnp.where(li == INFO_RANK, rank, info)
    info = jnp.where(li == INFO_WA, w_a, info)
    info = jnp.where(li == INFO_WB, w_b, info)
    info_ref[...] = info


def route_tokens(logits):
    T = logits.shape[0]
    tb = ROUTE_TOKENS
    return pl.pallas_call(
        _route_kernel,
        grid=(T // tb,),
        in_specs=[pl.BlockSpec((tb, ROUTER_COLS), lambda i: (i, 0))],
        out_specs=[pl.BlockSpec((tb, ROUTER_COLS), lambda i: (i, 0)),
                   pl.BlockSpec((1, ROUTER_COLS), lambda i: (0, 0))],
        out_shape=[jax.ShapeDtypeStruct((T, ROUTER_COLS), F32), jax.ShapeDtypeStruct((1, ROUTER_COLS), F32)],
        scratch_shapes=[pltpu.VMEM((1, ROUTER_COLS), F32)],
        compiler_params=_cparams("arbitrary"),
        name="moe_route",
    )(logits)


def _pair_tables():
    a_tab, b_tab = [], []
    for g in range(N_GROUPS):
        for a in range(EXP_PER_GROUP):
            for b in range(a + 1, EXP_PER_GROUP):
                a_tab.append(g * EXP_PER_GROUP + a)
                b_tab.append(g * EXP_PER_GROUP + b)
    return jnp.array(a_tab, jnp.int32), jnp.array(b_tab, jnp.int32)


def slot_plan(info, counts, n_blocks):
    cnt = counts[0, :N_CLASSES].astype(jnp.int32)
    padded = (cnt + MOE_BLOCK - 1) // MOE_BLOCK * MOE_BLOCK
    pad_end = jnp.cumsum(padded)
    pad_start = pad_end - padded
    cls = info[:, INFO_CLASS].astype(jnp.int32)
    dest = pad_start[cls] + info[:, INFO_RANK].astype(jnp.int32)
    n_used = (pad_end[-1] // MOE_BLOCK).astype(jnp.int32).reshape(1)
    blk_cls = jnp.minimum(jnp.searchsorted(pad_end, jnp.arange(n_blocks) * MOE_BLOCK, side="right"),
                          N_CLASSES - 1)
    a_tab, b_tab = _pair_tables()
    return dest, a_tab[blk_cls], b_tab[blk_cls], n_used


DISPATCH_TOKENS = 256


def _wait_rows(buf, sem):
    pltpu.make_async_copy(buf, buf, sem).wait()


def _dispatch_kernel(dest_ref, x_ref, info_ref, zeros_hbm, xs_hbm, rows, sems):
    del zeros_hbm
    i = pl.program_id(0)
    n = pl.num_programs(0)
    slot = i % 2
    D = x_ref.shape[1]
    tb = x_ref.shape[0]

    @pl.when(i >= 2)
    def _():
        _wait_rows(rows.at[slot], sems.at[slot])

    rows[slot, :, :D] = x_ref[...]
    rows[slot, :, D:] = info_ref[...]

    def body(r, carry):
        pltpu.make_async_copy(rows.at[slot, r], xs_hbm.at[dest_ref[0, 0, r]], sems.at[slot]).start()
        return carry
    lax.fori_loop(0, tb, body, 0)

    @pl.when(i == n - 1)
    def _():
        _wait_rows(rows.at[slot], sems.at[slot])

        @pl.when(n >= 2)
        def _():
            _wait_rows(rows.at[1 - slot], sems.at[1 - slot])


def moe_dispatch(x_tokens, info, dest, n_blocks):
    T, D = x_tokens.shape
    tb = DISPATCH_TOKENS
    W = D + ROUTER_COLS
    P = n_blocks * MOE_BLOCK
    return pl.pallas_call(
        _dispatch_kernel,
        grid=(T // tb,),
        in_specs=[pl.BlockSpec((1, 1, tb), lambda i: (i, 0, 0), memory_space=pltpu.SMEM),
                  pl.BlockSpec((tb, D), lambda i: (i, 0)),
                  pl.BlockSpec((tb, ROUTER_COLS), lambda i: (i, 0)),
                  pl.BlockSpec(memory_space=pl.ANY)],
        out_specs=pl.BlockSpec(memory_space=pl.ANY),
        out_shape=jax.ShapeDtypeStruct((P, W), F32),
        scratch_shapes=[pltpu.VMEM((2, tb, W), F32), pltpu.SemaphoreType.DMA((2,))],
        input_output_aliases={3: 0},
        compiler_params=_cparams("arbitrary"),
        name="moe_dispatch",
    )(dest.reshape(T // tb, 1, tb), x_tokens, info, jnp.zeros((P, W), F32))


def _expert_pair_kernel(ea_ref, eb_ref, nused_ref, xs_ref, wga_ref, wua_ref, wda_ref, wgb_ref, wub_ref, wdb_ref,
                        o_ref):
    del ea_ref, eb_ref
    i = pl.program_id(0)
    D = o_ref.shape[1]

    @pl.when(i < nused_ref[0])
    def _():
        xb = xs_ref[:, :D].astype(BF16)

        def ffn(wg_ref, wu_ref, wd_ref):
            gate = jnp.dot(xb, wg_ref[0], preferred_element_type=F32)
            up = jnp.dot(xb, wu_ref[0], preferred_element_type=F32)
            hid = (gate * jax.nn.sigmoid(gate) * up).astype(BF16)
            return jnp.dot(hid, wd_ref[0], preferred_element_type=F32)

        w_a = xs_ref[:, D + INFO_WA:D + INFO_WA + 1]
        w_b = xs_ref[:, D + INFO_WB:D + INFO_WB + 1]
        o_ref[...] = w_a * ffn(wga_ref, wua_ref, wda_ref) + w_b * ffn(wgb_ref, wub_ref, wdb_ref)

    @pl.when(i >= nused_ref[0])
    def _():
        o_ref[...] = jnp.zeros_like(o_ref)


def expert_pairs(xs, blk_a, blk_b, n_used, w_gate, w_up, w_down):
    P, W = xs.shape
    D = W - ROUTER_COLS
    n_blocks = P // MOE_BLOCK
    DE = w_gate.shape[-1]
    wspec = lambda shape, which: pl.BlockSpec(shape, lambda i, ea, eb, nu: ((ea, eb)[which][i], 0, 0))
    grid_spec = pltpu.PrefetchScalarGridSpec(
        num_scalar_prefetch=3,
        grid=(n_blocks,),
        in_specs=[pl.BlockSpec((MOE_BLOCK, W), lambda i, ea, eb, nu: (i, 0)),
                  wspec((1, D, DE), 0), wspec((1, D, DE), 0), wspec((1, DE, D), 0),
                  wspec((1, D, DE), 1), wspec((1, D, DE), 1), wspec((1, DE, D), 1)],
        out_specs=pl.BlockSpec((MOE_BLOCK, D), lambda i, ea, eb, nu: (i, 0)),
    )
    return pl.pallas_call(
        _expert_pair_kernel,
        grid_spec=grid_spec,
        out_shape=jax.ShapeDtypeStruct((P, D), F32),
        compiler_params=_cparams("arbitrary"),
        name="moe_experts",
    )(blk_a, blk_b, n_used, xs, w_gate, w_up, w_down, w_gate, w_up, w_down)


def _gather_rows(idx_ref, src_hbm, buf, sem, n_rows):
    def body(r, carry):
        pltpu.make_async_copy(src_hbm.at[idx_ref[0, 0, r]], buf.at[r], sem).start()
        return carry
    lax.fori_loop(0, n_rows, body, 0)


def _collect_kernel(dest_ref, dest_next_ref, o_hbm, h_ref, g2_ref, fg_ref, out_ref, obuf, sems, *, final_norm):
    i = pl.program_id(0)
    n = pl.num_programs(0)
    slot = i % 2
    tb = h_ref.shape[0]

    @pl.when(i == 0)
    def _():
        _gather_rows(dest_ref, o_hbm, obuf.at[0], sems.at[0], tb)

    @pl.when(i + 1 < n)
    def _():
        _gather_rows(dest_next_ref, o_hbm, obuf.at[1 - slot], sems.at[1 - slot], tb)

    _wait_rows(obuf.at[slot], sems.at[slot])
    h = h_ref[...] + g2_ref[0] * obuf[slot]
    if final_norm:
        ms = jnp.mean(h * h, axis=-1, keepdims=True)
        h = h * lax.rsqrt(ms + EPS) * fg_ref[...]
    out_ref[...] = h


def moe_collect(o_sorted, dest, h_tokens, g2, tokens_per_batch, final_g, final_norm):
    T, D = h_tokens.shape
    tb = DISPATCH_TOKENS
    nt = T // tb
    per_b = tokens_per_batch // tb
    dest3 = dest.reshape(nt, 1, tb)
    return pl.pallas_call(
        functools.partial(_collect_kernel, final_norm=final_norm),
        grid=(nt,),
        in_specs=[pl.BlockSpec((1, 1, tb), lambda i: (i, 0, 0), memory_space=pltpu.SMEM),
                  pl.BlockSpec((1, 1, tb), lambda i: (jnp.minimum(i + 1, nt - 1), 0, 0), memory_space=pltpu.SMEM),
                  pl.BlockSpec(memory_space=pl.ANY),
                  pl.BlockSpec((tb, D), lambda i: (i, 0)),
                  pl.BlockSpec((1, 1, D), lambda i: (i // per_b, 0, 0)),
                  pl.BlockSpec((1, D), lambda i: (0, 0))],
        out_specs=pl.BlockSpec((tb, D), lambda i: (i, 0)),
        out_shape=jax.ShapeDtypeStruct((T, D), F32),
        scratch_shapes=[pltpu.VMEM((2, tb, D), F32), pltpu.SemaphoreType.DMA((2,))],
        compiler_params=_cparams("arbitrary"),
        name="moe_collect",
    )(dest3, dest3, o_sorted, h_tokens, g2, final_g.reshape(1, D))


CONV_MARGIN = 16


def _time_chunk(L):
    return min(L, 256)


LANES = 128


def _zero_margins(pad_ref, L):
    zeros = jnp.zeros((CONV_MARGIN, LANES), F32)
    for s in range(pad_ref.shape[0]):
        pad_ref[s, pl.ds(0, CONV_MARGIN), :] = zeros
        pad_ref[s, pl.ds(CONV_MARGIN + L, CONV_MARGIN), :] = zeros


def _dw_conv_slab(pad_ref, s, base, T, w_ref, b_ref, col, taps, pad_left):
    acc = jnp.broadcast_to(b_ref[:, col:col + LANES], (T, LANES))
    for k in range(taps):
        acc = acc + w_ref[k:k + 1, col:col + LANES] * pad_ref[s, pl.ds(base + (CONV_MARGIN - pad_left + k), T), :]
    return acc


def _conformer_kernel(u_ref, w_ref, b_ref, g_ref, beta_ref, o_ref, ypad):
    L = o_ref.shape[1]
    T = _time_chunk(L)
    C = D_GROUP
    n_slabs = C // LANES
    pad = (CONF_KERNEL - 1) // 2
    _zero_margins(ypad, L)

    def glu(j, carry):
        base = pl.multiple_of(j * T, T)
        for s in range(n_slabs):
            a = u_ref[0, pl.ds(base, T), s * LANES:(s + 1) * LANES]
            gate = u_ref[0, pl.ds(base, T), C + s * LANES:C + (s + 1) * LANES]
            ypad[s, pl.ds(CONV_MARGIN + base, T), :] = a * jax.nn.sigmoid(gate)
        return carry
    lax.fori_loop(0, L // T, glu, 0)

    def conv(j, carry):
        base = pl.multiple_of(j * T, T)
        acc = jnp.concatenate([_dw_conv_slab(ypad, s, base, T, w_ref, b_ref, s * LANES, CONF_KERNEL, pad)
                               for s in range(n_slabs)], axis=-1)
        mu = jnp.mean(acc, axis=-1, keepdims=True)
        cen = acc - mu
        var = jnp.mean(cen * cen, axis=-1, keepdims=True)
        y = cen * lax.rsqrt(var + EPS) * g_ref[...] + beta_ref[...]
        o_ref[0, pl.ds(base, T), :] = y * jax.nn.sigmoid(y)
        return carry
    lax.fori_loop(0, L // T, conv, 0)


def conformer_conv(u, w, b, ln_g, ln_b):
    B, L, _ = u.shape
    C = D_GROUP
    vec = pl.BlockSpec((1, C), lambda i: (0, 0))
    return pl.pallas_call(
        _conformer_kernel,
        grid=(B,),
        in_specs=[pl.BlockSpec((1, L, 2 * C), lambda i: (i, 0, 0)),
                  pl.BlockSpec((CONF_KERNEL, C), lambda i: (0, 0)), vec, vec, vec],
        out_specs=pl.BlockSpec((1, L, C), lambda i: (i, 0, 0)),
        out_shape=jax.ShapeDtypeStruct((B, L, C), F32),
        scratch_shapes=[pltpu.VMEM((C // LANES, L + 2 * CONV_MARGIN, LANES), F32)],
        compiler_params=_cparams("parallel"),
        name="conformer_conv",
    )(u, w, b.reshape(1, C), ln_g.reshape(1, C), ln_b.reshape(1, C))


def _gelu_tanh(x):
    return 0.5 * x * (1.0 + jnp.tanh(math.sqrt(2.0 / math.pi) * (x + 0.044715 * (x * x * x))))


def _lru_kernel(uc_ref, ux_ref, cw_ref, cb_ref, wcat_ref, bcat_ref, lam_ref, *rest, need_ctx):
    if need_ctx:
        oc_ref, ox_ref, cpad, xpad, a_s, b_s, yx, yc = rest
    else:
        ox_ref, cpad, xpad, a_s, b_s, yx = rest
        oc_ref = yc = None
    C = D_GROUP
    n_slabs = C // LANES
    Lc = uc_ref.shape[1]
    Lx = ux_ref.shape[1]
    pad_l = (LRU_CONV - 1) // 2

    def fill(pad_ref, u_ref, L):
        T = _time_chunk(L)
        _zero_margins(pad_ref, L)

        def body(j, carry):
            base = pl.multiple_of(j * T, T)
            for s in range(n_slabs):
                pad_ref[s, pl.ds(CONV_MARGIN + base, T), :] = u_ref[0, pl.ds(base, T),
                                                                    C + s * LANES:C + (s + 1) * LANES]
            return carry
        lax.fori_loop(0, L // T, body, 0)

    fill(cpad, uc_ref, Lc)
    fill(xpad, ux_ref, Lx)

    def coeffs(pad_ref, base, T, d):
        x = jnp.concatenate([_dw_conv_slab(pad_ref, s, base, T, cw_ref, cb_ref, s * LANES, LRU_CONV, pad_l)
                             for s in range(n_slabs)], axis=-1)
        g = jnp.dot(x.astype(BF16), wcat_ref[:, 2 * d * C:2 * (d + 1) * C],
                    preferred_element_type=F32) + bcat_ref[:, 2 * d * C:2 * (d + 1) * C]
        r = jax.nn.sigmoid(g[:, :C])
        i = jax.nn.sigmoid(g[:, C:])
        z = -lam_ref[d:d + 1, :]
        softplus = jnp.maximum(z, 0.0) + jnp.log(1.0 + jnp.exp(-jnp.abs(z)))
        a = jnp.exp(-LRU_C * r * softplus)
        b = jnp.sqrt(1.0 - a * a) * (i * x)
        for s in range(n_slabs):
            a_s[s, pl.ds(0, T), :] = a[:, s * LANES:(s + 1) * LANES]
            b_s[s, pl.ds(0, T), :] = b[:, s * LANES:(s + 1) * LANES]

    def scan_rows(T, reverse, h, y_ref, base, first):
        def body(step, h):
            t = (T - 1 - step) if reverse else step
            new = []
            for s in range(n_slabs):
                hs = a_s[s, pl.ds(t, 1), :] * h[s] + b_s[s, pl.ds(t, 1), :]
                if y_ref is not None:
                    if first:
                        y_ref[s, pl.ds(base + t, 1), :] = hs
                    else:
                        y_ref[s, pl.ds(base + t, 1), :] = y_ref[s, pl.ds(base + t, 1), :] + hs
                new.append(hs)
            return tuple(new)
        return lax.fori_loop(0, T, body, h, unroll=8)

    def run(pad_ref, L, d, h, y_ref):
        T = _time_chunk(L)
        n = L // T
        reverse = d == 1

        def body(j, h):
            jj = (n - 1 - j) if reverse else j
            base = pl.multiple_of(jj * T, T)
            coeffs(pad_ref, base, T, d)
            return scan_rows(T, reverse, h, y_ref, base, d == 0)
        return lax.fori_loop(0, n, body, h)

    for d in range(2):
        h = tuple(jnp.zeros((1, LANES), F32) for _ in range(n_slabs))
        h = run(cpad, Lc, d, h, yc)
        run(xpad, Lx, d, h, yx)

    def finish(u_ref, y_ref, o_ref, L):
        T = _time_chunk(L)

        def body(j, carry):
            base = pl.multiple_of(j * T, T)
            y = jnp.concatenate([y_ref[s, pl.ds(base, T), :] for s in range(n_slabs)], axis=-1)
            o_ref[0, pl.ds(base, T), :] = _gelu_tanh(u_ref[0, pl.ds(base, T), :C]) * y
            return carry
        lax.fori_loop(0, L // T, body, 0)

    finish(ux_ref, yx, ox_ref, Lx)
    if need_ctx:
        finish(uc_ref, yc, oc_ref, Lc)


def _block_diag(w):
    H, n, _ = w.shape
    eye = jnp.eye(H, dtype=w.dtype)
    return (eye[:, None, :, None] * w[:, :, None, :]).reshape(H * n, H * n)


def rglru_mixer(uc, ux, lp, need_ctx):
    B, Lc, _ = uc.shape
    Lx = ux.shape[1]
    C = D_GROUP
    wcat = jnp.concatenate([_block_diag(lp["lru_wa"][0]), _block_diag(lp["lru_wx"][0]),
                            _block_diag(lp["lru_wa"][1]), _block_diag(lp["lru_wx"][1])], axis=1).astype(BF16)
    bcat = jnp.concatenate([lp["lru_ba"][0], lp["lru_bx"][0], lp["lru_ba"][1], lp["lru_bx"][1]]).reshape(1, 4 * C)
    full = lambda r, c: pl.BlockSpec((r, c), lambda i: (0, 0))
    seq = lambda L, n: pl.BlockSpec((1, L, n), lambda i: (i, 0, 0))
    out_specs = [seq(Lx, C)]
    out_shape = [jax.ShapeDtypeStruct((B, Lx, C), F32)]
    if need_ctx:
        out_specs = [seq(Lc, C)] + out_specs
        out_shape = [jax.ShapeDtypeStruct((B, Lc, C), F32)] + out_shape
    T = _time_chunk(Lx)
    slab = lambda rows: pltpu.VMEM((C // LANES, rows, LANES), F32)
    scratch = [slab(Lc + 2 * CONV_MARGIN), slab(Lx + 2 * CONV_MARGIN), slab(T), slab(T), slab(Lx)]
    if need_ctx:
        scratch.append(slab(Lc))
    res = pl.pallas_call(
        functools.partial(_lru_kernel, need_ctx=need_ctx),
        grid=(B,),
        in_specs=[seq(Lc, 2 * C), seq(Lx, 2 * C), full(LRU_CONV, C), full(1, C), full(C, 4 * C),
                  full(1, 4 * C), full(2, C)],
        out_specs=out_specs,
        out_shape=out_shape,
        scratch_shapes=scratch,
        compiler_params=_cparams("parallel"),
        name="rglru",
    )(uc, ux, lp["lru_conv_w"], lp["lru_conv_b"].reshape(1, C), wcat, bcat, lp["lru_lambda"])
    if need_ctx:
        return res[0], res[1]
    return None, res[0]


HY_SHORT = 3


def _short_conv(pad_ref, base, T, w_ref, b_ref, c0, c1):
    return jnp.concatenate([_dw_conv_slab(pad_ref, col // LANES, base, T, w_ref, b_ref, col, HY_SHORT, 1)
                            for col in range(c0, c1, LANES)], axis=-1)


def _fill_padded(pad_ref, u_ref, L, T):
    _zero_margins(pad_ref, L)

    def body(j, carry):
        base = pl.multiple_of(j * T, T)
        for s in range(pad_ref.shape[0]):
            pad_ref[s, pl.ds(CONV_MARGIN + base, T), :] = u_ref[0, pl.ds(base, T), s * LANES:(s + 1) * LANES]
        return carry
    lax.fori_loop(0, L // T, body, 0)


def _hyena_pre_kernel(u_ref, w_ref, b_ref, z_ref, upad):
    L = u_ref.shape[1]
    T = _time_chunk(L)
    C = D_GROUP
    _fill_padded(upad, u_ref, L, T)

    def body(j, carry):
        base = pl.multiple_of(j * T, T)
        x1 = _short_conv(upad, base, T, w_ref, b_ref, C, 2 * C)
        v = _short_conv(upad, base, T, w_ref, b_ref, 2 * C, 3 * C)
        z_ref[pl.ds(base, T), :] = (x1 * v).astype(BF16)
        return carry
    lax.fori_loop(0, L // T, body, 0)


def _hyena_post_kernel(u_ref, y_ref, w_ref, b_ref, bias_ref, o_ref, upad):
    L = u_ref.shape[1]
    T = _time_chunk(L)
    C = D_GROUP
    _fill_padded(upad, u_ref, L, T)

    def body(j, carry):
        base = pl.multiple_of(j * T, T)
        x0 = _short_conv(upad, base, T, w_ref, b_ref, 0, C)
        x1 = _short_conv(upad, base, T, w_ref, b_ref, C, 2 * C)
        v = _short_conv(upad, base, T, w_ref, b_ref, 2 * C, 3 * C)
        o_ref[0, pl.ds(base, T), :] = x0 * (y_ref[pl.ds(base, T), :] + (x1 * v) * bias_ref[...])
        return carry
    lax.fori_loop(0, L // T, body, 0)


def _spectrum_kernel(f_ref, z_ref, ha_ref, hb_ref, hc_ref, y_ref):
    tf = ha_ref.shape[0]
    acc = jnp.dot(f_ref[...], z_ref[...], preferred_element_type=F32)
    zr = acc[:tf]
    zi = acc[tf:]
    y_ref[:tf, :] = (zr * ha_ref[...] - zi * hb_ref[...]).astype(BF16)
    y_ref[tf:, :] = (zr * hb_ref[...] + zi * hc_ref[...]).astype(BF16)


def _idft_kernel(f_ref, y_ref, o_ref):
    o_ref[...] = jnp.dot(f_ref[...], y_ref[...], preferred_element_type=F32)


def dft_tables(L):
    N = 2 * L
    tf = min(256, L)
    k = jnp.arange(L, dtype=jnp.int32)
    n = jnp.arange(L, dtype=jnp.int32)
    ang = (2.0 * math.pi / N) * ((k[:, None] * n[None, :]) % N).astype(F32)
    cos = jnp.cos(ang)
    sin = jnp.sin(ang)
    nyq = jnp.where(n % 2 == 0, 1.0, -1.0).astype(F32)
    f_re = cos
    f_im = (-sin).at[0].set(nyq)
    fwd = jnp.stack([f_re.reshape(L // tf, tf, L), f_im.reshape(L // tf, tf, L)], axis=1).reshape(N, L)
    ck = jnp.where(k == 0, 1.0, 2.0).astype(F32)[:, None] / N
    i_re = cos * ck
    i_im = (-sin * ck).at[0].set(nyq / N)
    inv = jnp.stack([i_re.reshape(L // tf, tf, L), i_im.reshape(L // tf, tf, L)], axis=1).reshape(N, L).T
    return fwd.astype(BF16), inv.astype(BF16)


def filter_spectrum(h_fwd, h_bwd):
    L, C = h_fwd.shape
    k = jnp.concatenate([h_fwd, jnp.zeros((1, C), F32), h_bwd[1:][::-1]], axis=0)
    hf = jnp.fft.rfft(k, axis=0)
    hr = jnp.real(hf)
    hi = jnp.imag(hf)
    a = hr[:L]
    b = hi[:L].at[0].set(0.0)
    c = hr[:L].at[0].set(hr[L])
    return a, b, c


def hyena_mixer(u, lp, tables):
    B, L, _ = u.shape
    C = D_GROUP
    N = 2 * L
    fwd, inv = tables
    tf = min(256, L)
    T = _time_chunk(L)
    w, bsh = lp["hy_short_w"], lp["hy_short_b"].reshape(1, 3 * C)
    z2 = pl.pallas_call(
        _hyena_pre_kernel,
        grid=(B,),
        in_specs=[pl.BlockSpec((1, L, 3 * C), lambda b: (b, 0, 0)),
                  pl.BlockSpec((HY_SHORT, 3 * C), lambda b: (0, 0)),
                  pl.BlockSpec((1, 3 * C), lambda b: (0, 0))],
        out_specs=pl.BlockSpec((L, C), lambda b: (0, b)),
        out_shape=jax.ShapeDtypeStruct((L, B * C), BF16),
        scratch_shapes=[pltpu.VMEM((3 * C // LANES, L + 2 * CONV_MARGIN, LANES), F32)],
        compiler_params=_cparams("parallel"),
        name="hyena_pre",
    )(u, w, bsh)

    h_fwd, h_bwd = _hyena_filters(L, lp)
    tn = 2 * C
    ha, hb, hc = [jnp.tile(t, (1, tn // C)) for t in filter_spectrum(h_fwd, h_bwd)]
    hspec = pl.BlockSpec((tf, tn), lambda i, j: (i, 0))
    y2 = pl.pallas_call(
        _spectrum_kernel,
        grid=(L // tf, B * C // tn),
        in_specs=[pl.BlockSpec((2 * tf, L), lambda i, j: (i, 0)),
                  pl.BlockSpec((L, tn), lambda i, j: (0, j)), hspec, hspec, hspec],
        out_specs=pl.BlockSpec((2 * tf, tn), lambda i, j: (i, j)),
        out_shape=jax.ShapeDtypeStruct((N, B * C), BF16),
        compiler_params=_cparams("parallel", "parallel"),
        name="hyena_spectrum",
    )(fwd, z2, ha, hb, hc)

    tl = min(256, L)
    yt = pl.pallas_call(
        _idft_kernel,
        grid=(L // tl, B * C // tn),
        in_specs=[pl.BlockSpec((tl, N), lambda i, j: (i, 0)),
                  pl.BlockSpec((N, tn), lambda i, j: (0, j))],
        out_specs=pl.BlockSpec((tl, tn), lambda i, j: (i, j)),
        out_shape=jax.ShapeDtypeStruct((L, B * C), F32),
        compiler_params=_cparams("parallel", "parallel"),
        name="hyena_idft",
    )(inv, y2)

    return pl.pallas_call(
        _hyena_post_kernel,
        grid=(B,),
        in_specs=[pl.BlockSpec((1, L, 3 * C), lambda b: (b, 0, 0)),
                  pl.BlockSpec((L, C), lambda b: (0, b)),
                  pl.BlockSpec((HY_SHORT, 3 * C), lambda b: (0, 0)),
                  pl.BlockSpec((1, 3 * C), lambda b: (0, 0)),
                  pl.BlockSpec((1, C), lambda b: (0, 0))],
        out_specs=pl.BlockSpec((1, L, C), lambda b: (b, 0, 0)),
        out_shape=jax.ShapeDtypeStruct((B, L, C), F32),
        scratch_shapes=[pltpu.VMEM((3 * C // LANES, L + 2 * CONV_MARGIN, LANES), F32)],
        compiler_params=_cparams("parallel"),
        name="hyena_post",
    )(u, yt, w, bsh, lp["hy_bias"].reshape(1, C))


def _hyena_filters(L, lp):
    t = jnp.linspace(0.0, 1.0, L, dtype=F32)[:, None]
    bands = (HY_EMB - 1) // 2
    w = 2.0 * math.pi * jnp.arange(L, dtype=F32)[:, None] / L
    f = jnp.linspace(1e-4, bands - 1, bands, dtype=F32)[None]
    z = jnp.concatenate([t, jnp.cos(f * w), -jnp.sin(f * w)], axis=-1)
    hdn = jnp.sin(z @ lp["hy_ffn_w1"] + lp["hy_ffn_b1"])
    hdn = jnp.sin(hdn @ lp["hy_ffn_w2"] + lp["hy_ffn_b2"])
    h = (hdn @ lp["hy_ffn_w3"]).reshape(L, 2, D_GROUP)
    max_decay = math.log(HY_TARGET) / HY_FAST_DECAY
    min_decay = math.log(HY_TARGET) / HY_SLOW_DECAY
    deltas = jnp.linspace(min_decay, max_decay, D_GROUP, dtype=F32)
    h = h * jnp.exp(-t * jnp.abs(deltas))[:, None, :]
    h = h / (jnp.sum(jnp.abs(h), axis=(0, 1), keepdims=True) + EPS)
    return h[:, 0], h[:, 1]


def _layer(hc, hx, c_silu_all, lp, need_ctx, final_g, final_norm, tables_x, tables_c):
    B, S, D = hx.shape
    C = hc.shape[1]
    mod = small_linear(c_silu_all, lp["ada_w"], lp["ada_b"])
    mod_x = mod[:B].reshape(B, 6, 1, D)
    mod_c = jnp.broadcast_to(mod[B].reshape(1, 6, 1, D), (B, 6, 1, D))
    w_ext = extend_w_in(lp["w_in"])
    cos_x, sin_x = rope_tables(S, True)
    cos_c, sin_c = rope_tables(C, False)
    hy_x, cf_x, at_x, lr_x = in_proj(hx, mod_x[:, 0], mod_x[:, 1], lp["norm1_g"], w_ext, cos_x, sin_x, tm=256)
    hy_c, cf_c, at_c, lr_c = in_proj(hc, mod_c[:, 0], mod_c[:, 1], lp["norm1_g"], w_ext, cos_c, sin_c, tm=256)

    yd_c, yd_x = rglru_mixer(lr_c, lr_x, lp, need_ctx)
    conf = lambda u: conformer_conv(u, lp["conf_dw_w"], lp["conf_dw_b"], lp["conf_ln_g"], lp["conf_ln_b"])
    ys_x = [hyena_mixer(hy_x, lp, tables_x), conf(cf_x),
            window_attention(at_x, at_c, lp["attn_sink"]), yd_x]

    w_out = lp["w_out"].astype(BF16)
    w_router = jnp.zeros((D, ROUTER_COLS), F32)
    w_router = w_router.at[:, :N_GROUPS].set(lp["router_g_w"]).at[:, N_GROUPS:N_GROUPS + N_EXPERTS].set(lp["router_e_w"])
    b_router = jnp.zeros((1, ROUTER_COLS), F32)
    b_router = b_router.at[0, :N_GROUPS].set(lp["router_g_b"]).at[0, N_GROUPS:N_GROUPS + N_EXPERTS].set(lp["router_e_b"])

    hx1, nx, lg_x = out_proj(ys_x, hx, mod_x[:, 2], lp["group_norm_g"], w_out, lp["norm2_g"],
                             mod_x[:, 3], mod_x[:, 4], w_router, b_router, tm=256)
    h_tok = hx1.reshape(B * S, D)
    n_tok = nx.reshape(B * S, D)
    lg = lg_x.reshape(B * S, ROUTER_COLS)
    if need_ctx:
        ys_c = [hyena_mixer(hy_c, lp, tables_c), conf(cf_c),
                context_attention(at_c, lp["attn_sink"]), yd_c]
        hc1, nc, lg_c = out_proj(ys_c, hc, mod_c[:, 2], lp["group_norm_g"], w_out, lp["norm2_g"],
                                 mod_c[:, 3], mod_c[:, 4], w_router, b_router, tm=256)
        n_tok = jnp.concatenate([n_tok, nc.reshape(B * C, D)], axis=0)
        lg = jnp.concatenate([lg, lg_c.reshape(B * C, ROUTER_COLS)], axis=0)

    T = n_tok.shape[0]
    n_blocks = -(-T // MOE_BLOCK) + N_CLASSES
    info, counts = route_tokens(lg)
    dest, blk_a, blk_b, n_used = slot_plan(info, counts, n_blocks)
    xs = moe_dispatch(n_tok, info, dest, n_blocks)
    o_sorted = expert_pairs(xs, blk_a, blk_b, n_used, lp["exp_w_gate"].astype(BF16),
                            lp["exp_w_up"].astype(BF16), lp["exp_w_down"].astype(BF16))
    hx2 = moe_collect(o_sorted, dest[:B * S], h_tok, mod_x[:, 5], S, final_g, final_norm)
    hx2 = hx2.reshape(B, S, D)
    if need_ctx:
        hc2 = moe_collect(o_sorted, dest[B * S:], hc1.reshape(B * C, D), mod_c[:, 5], C,
                          final_g, False).reshape(B, C, D)
    else:
        hc2 = hc
    return hc2, hx2


def kernel(x, c, ctx, c_ctx, norm1_g, norm2_g, ada_w, ada_b, w_in, hy_short_w, hy_short_b, hy_ffn_w1, hy_ffn_b1, hy_ffn_w2, hy_ffn_b2, hy_ffn_w3, hy_bias, conf_dw_w, conf_dw_b, conf_ln_g, conf_ln_b, attn_sink, lru_conv_w, lru_conv_b, lru_wa, lru_ba, lru_wx, lru_bx, lru_lambda, group_norm_g, w_out, router_g_w, router_g_b, router_e_w, router_e_b, exp_w_gate, exp_w_up, exp_w_down, final_norm_g):
    stacked = dict(norm1_g=norm1_g, norm2_g=norm2_g, ada_w=ada_w, ada_b=ada_b, w_in=w_in,
                   hy_short_w=hy_short_w, hy_short_b=hy_short_b, hy_ffn_w1=hy_ffn_w1, hy_ffn_b1=hy_ffn_b1,
                   hy_ffn_w2=hy_ffn_w2, hy_ffn_b2=hy_ffn_b2, hy_ffn_w3=hy_ffn_w3, hy_bias=hy_bias,
                   conf_dw_w=conf_dw_w, conf_dw_b=conf_dw_b, conf_ln_g=conf_ln_g, conf_ln_b=conf_ln_b,
                   attn_sink=attn_sink, lru_conv_w=lru_conv_w, lru_conv_b=lru_conv_b, lru_wa=lru_wa,
                   lru_ba=lru_ba, lru_wx=lru_wx, lru_bx=lru_bx, lru_lambda=lru_lambda,
                   group_norm_g=group_norm_g, w_out=w_out, router_g_w=router_g_w, router_g_b=router_g_b,
                   router_e_w=router_e_w, router_e_b=router_e_b, exp_w_gate=exp_w_gate,
                   exp_w_up=exp_w_up, exp_w_down=exp_w_down)
    depth = norm1_g.shape[0]
    B = x.shape[0]
    cs = jnp.concatenate([jax.nn.silu(c), jnp.broadcast_to(jax.nn.silu(c_ctx)[None], (8, c.shape[1]))], axis=0)
    hc, hx = ctx, x
    tables_x = dft_tables(x.shape[1])
    tables_c = dft_tables(ctx.shape[1])
    for l in range(depth):
        lp = {k: v[l] for k, v in stacked.items()}
        hc, hx = _layer(hc, hx, cs, lp, need_ctx=(l < depth - 1), final_g=final_norm_g,
                        final_norm=(l == depth - 1), tables_x=tables_x, tables_c=tables_c)
    return hx
```

```python
import functools
import math

import jax
import jax.numpy as jnp
from jax import lax
from jax.experimental import pallas as pl
from jax.experimental.pallas import tpu as pltpu

F32 = jnp.float32
BF16 = jnp.bfloat16

EPS = 1e-6
NEG_INF = -1e30
GRID_W = 64
N_MIXERS = 4
D_GROUP = 256
HY_COLS = 3 * D_GROUP
CONF_COLS = 2 * D_GROUP
ATT_HEADS = 4
ATT_KV_HEADS = 2
HEAD_DIM = 64
ATT_COLS = (ATT_HEADS + 2 * ATT_KV_HEADS) * HEAD_DIM
LRU_COLS = 2 * D_GROUP
QK_COLS = (ATT_HEADS + ATT_KV_HEADS) * HEAD_DIM
WINDOW = 128
ATT_BLOCK = 128
ROPE_BASE = 10000.0
HY_EMB = 33
HY_FAST_DECAY = 0.3
HY_SLOW_DECAY = 1.5
HY_TARGET = 1e-2
CONF_KERNEL = 31
LRU_HEADS = 4
LRU_CONV = 4
LRU_C = 8.0
N_GROUPS = 4
EXP_PER_GROUP = 8
N_EXPERTS = N_GROUPS * EXP_PER_GROUP
TOP_K = 2
MOE_BLOCK = 256
ROUTER_COLS = 128

VMEM_LIMIT_BYTES = 56 * 1024 * 1024


def _cparams(*sem):
    return pltpu.CompilerParams(dimension_semantics=sem, vmem_limit_bytes=VMEM_LIMIT_BYTES)


def _linear_kernel(x_ref, w_ref, b_ref, o_ref):
    o_ref[...] = jnp.dot(x_ref[...], w_ref[...], preferred_element_type=F32,
                         precision=lax.Precision.HIGHEST) + b_ref[...]


def small_linear(x, w, b, tn=1024):
    M, K = x.shape
    N = w.shape[1]
    return pl.pallas_call(
        _linear_kernel,
        grid=(N // tn,),
        in_specs=[pl.BlockSpec((M, K), lambda j: (0, 0)),
                  pl.BlockSpec((K, tn), lambda j: (0, j)),
                  pl.BlockSpec((1, tn), lambda j: (0, j))],
        out_specs=pl.BlockSpec((M, tn), lambda j: (0, j)),
        out_shape=jax.ShapeDtypeStruct((M, N), F32),
        compiler_params=_cparams("parallel"),
        name="ada_linear",
    )(x, w, b.reshape(1, N))


def _in_proj_kernel(x_ref, sh_ref, sc_ref, g_ref, w_ref, cos_ref, sin_ref,
                    hy_ref, cf_ref, at_ref, lr_ref):
    x = x_ref[0]
    ms = jnp.mean(x * x, axis=-1, keepdims=True)
    y = x * lax.rsqrt(ms + EPS) * g_ref[...]
    y = y * (1.0 + sc_ref[0]) + sh_ref[0]
    u = jnp.dot(y.astype(BF16), w_ref[...], preferred_element_type=F32)
    c0 = HY_COLS
    c1 = c0 + CONF_COLS
    c2 = c1 + ATT_COLS
    c3 = c2 + LRU_COLS
    hy_ref[0] = u[:, :c0]
    cf_ref[0] = u[:, c0:c1]
    lr_ref[0] = u[:, c2:c3]
    qk = u[:, c1:c1 + QK_COLS]
    qk_rot = u[:, c3:c3 + QK_COLS]
    at_ref[0, :, :QK_COLS] = qk * cos_ref[...] + qk_rot * sin_ref[...]
    at_ref[0, :, QK_COLS:] = u[:, c1 + QK_COLS:c2]


def in_proj(h, shift, scale, g, w_ext, cos_t, sin_t, tm):
    B, L, D = h.shape
    NW = w_ext.shape[1]
    outs = [HY_COLS, CONF_COLS, ATT_COLS, LRU_COLS]
    return pl.pallas_call(
        _in_proj_kernel,
        grid=(B, L // tm),
        in_specs=[pl.BlockSpec((1, tm, D), lambda b, i: (b, i, 0)),
                  pl.BlockSpec((1, 1, D), lambda b, i: (b, 0, 0)),
                  pl.BlockSpec((1, 1, D), lambda b, i: (b, 0, 0)),
                  pl.BlockSpec((1, D), lambda b, i: (0, 0)),
                  pl.BlockSpec((D, NW), lambda b, i: (0, 0)),
                  pl.BlockSpec((tm, QK_COLS), lambda b, i: (i, 0)),
                  pl.BlockSpec((tm, QK_COLS), lambda b, i: (i, 0))],
        out_specs=[pl.BlockSpec((1, tm, n), lambda b, i: (b, i, 0)) for n in outs],
        out_shape=[jax.ShapeDtypeStruct((B, L, n), F32) for n in outs],
        compiler_params=_cparams("parallel", "parallel"),
        name="in_proj",
    )(h, shift, scale, g.reshape(1, D), w_ext, cos_t, sin_t)


def rope_tables(L, rotary):
    n_heads = ATT_HEADS + ATT_KV_HEADS
    if not rotary:
        return jnp.ones((L, QK_COLS), F32), jnp.zeros((L, QK_COLS), F32)
    pos = jnp.arange(L)
    row = (pos // GRID_W).astype(F32)
    col = (pos % GRID_W).astype(F32)
    half = HEAD_DIM // 2
    inv_freq = ROPE_BASE ** (-jnp.arange(0, half, 2, dtype=F32) / half)
    ang_r = row[:, None] * inv_freq[None]
    ang_c = col[:, None] * inv_freq[None]
    cos_h = jnp.concatenate([jnp.cos(ang_r)] * 2 + [jnp.cos(ang_c)] * 2, axis=-1)
    sin_h = jnp.concatenate([jnp.sin(ang_r)] * 2 + [jnp.sin(ang_c)] * 2, axis=-1)
    return jnp.tile(cos_h, (1, n_heads)), jnp.tile(sin_h, (1, n_heads))


def extend_w_in(w_in):
    c1 = HY_COLS + CONF_COLS
    wqk = w_in[:, c1:c1 + QK_COLS]
    D = w_in.shape[0]
    w4 = wqk.reshape(D, QK_COLS // 32, 2, 16)
    wrot = jnp.stack([-w4[:, :, 1], w4[:, :, 0]], axis=2).reshape(D, QK_COLS)
    return jnp.concatenate([w_in, wrot], axis=1).astype(BF16)


def _softmax_parts(q, k_list, extra_logit):
    scale = HEAD_DIM ** -0.5
    s_list = []
    for k, mask in k_list:
        s = lax.dot_general(q, k, (((1,), (1,)), ((), ())), preferred_element_type=F32) * scale
        if mask is not None:
            s = jnp.where(mask, s, NEG_INF)
        s_list.append(s)
    m = extra_logit
    for s in s_list:
        m = jnp.maximum(m, jnp.max(s, axis=-1, keepdims=True))
    p_list = [jnp.exp(s - m) for s in s_list]
    denom = jnp.exp(extra_logit - m)
    for p in p_list:
        denom = denom + jnp.sum(p, axis=-1, keepdims=True)
    return p_list, 1.0 / denom


ATT_Q_BLOCKS = 4


def _win_attn_kernel(sink_ref, q_ref, kp_ref, kc_ref, kn_ref, vp_ref, vc_ref, vn_ref,
                     kx_ref, vx_ref, o_ref, *, seq_len):
    i = pl.program_id(1)
    blk = ATT_BLOCK
    qb = q_ref.shape[1] // blk
    scale = HEAD_DIM ** -0.5
    g = ATT_HEADS // ATT_KV_HEADS
    kw = jnp.concatenate([kp_ref[0], kc_ref[0], kn_ref[0]], axis=0)
    vw = jnp.concatenate([vp_ref[0], vc_ref[0], vn_ref[0]], axis=0).astype(BF16)
    kwt = kw.T.astype(BF16)
    kxt = kx_ref[0].T.astype(BF16)
    vx = vx_ref[0].astype(BF16)
    row = lax.broadcasted_iota(jnp.int32, (g * blk, 3 * blk), 0) % blk
    col = lax.broadcasted_iota(jnp.int32, (g * blk, 3 * blk), 1)
    in_band = jnp.abs(col - blk - row) <= WINDOW
    for j in range(qb):
        q_blk = i * qb + j
        k_pos = (q_blk - 1) * blk + col
        valid = in_band & (k_pos >= 0) & (k_pos < seq_len)
        outs = []
        for kv in range(ATT_KV_HEADS):
            ksl = slice(kv * HEAD_DIM, (kv + 1) * HEAD_DIM)
            heads = range(kv * g, (kv + 1) * g)
            qs = jnp.concatenate([q_ref[0, j * blk:(j + 1) * blk, h * HEAD_DIM:(h + 1) * HEAD_DIM]
                                  for h in heads], axis=0).astype(BF16)
            sink = jnp.concatenate([jnp.full((blk, 1), sink_ref[h], F32) for h in heads], axis=0)
            s_win = jnp.dot(qs, kwt[ksl, j * blk:(j + 3) * blk], preferred_element_type=F32) * scale
            s_win = jnp.where(valid, s_win, NEG_INF)
            s_ctx = jnp.dot(qs, kxt[ksl, :], preferred_element_type=F32) * scale
            m = jnp.maximum(jnp.maximum(jnp.max(s_win, axis=-1, keepdims=True),
                                        jnp.max(s_ctx, axis=-1, keepdims=True)), sink)
            p_win = jnp.exp(s_win - m)
            p_ctx = jnp.exp(s_ctx - m)
            denom = (jnp.exp(sink - m) + jnp.sum(p_win, axis=-1, keepdims=True)
                     + jnp.sum(p_ctx, axis=-1, keepdims=True))
            o = (jnp.dot(p_win.astype(BF16), vw[j * blk:(j + 3) * blk, ksl], preferred_element_type=F32)
                 + jnp.dot(p_ctx.astype(BF16), vx[:, ksl], preferred_element_type=F32)) * (1.0 / denom)
            outs.extend([o[k * blk:(k + 1) * blk] for k in range(g)])
        o_ref[0, j * blk:(j + 1) * blk, :] = jnp.concatenate(outs, axis=-1)


def window_attention(at_x, at_c, sink):
    B, S, _ = at_x.shape
    C = at_c.shape[1]
    blk = ATT_BLOCK
    qb = ATT_Q_BLOCKS
    nb = S // blk
    kcol = QK_COLS // 128 - 1
    vcol = kcol + 1

    def edge_spec(col, off):
        return pl.BlockSpec((1, blk, 128), lambda b, i, s: (b, jnp.clip(i * qb + off, 0, nb - 1), col))

    def mid_spec(col):
        return pl.BlockSpec((1, qb * blk, 128), lambda b, i, s: (b, i, col))

    grid_spec = pltpu.PrefetchScalarGridSpec(
        num_scalar_prefetch=1,
        grid=(B, nb // qb),
        in_specs=[pl.BlockSpec((1, qb * blk, ATT_HEADS * HEAD_DIM), lambda b, i, s: (b, i, 0)),
                  edge_spec(kcol, -1), mid_spec(kcol), edge_spec(kcol, qb),
                  edge_spec(vcol, -1), mid_spec(vcol), edge_spec(vcol, qb),
                  pl.BlockSpec((1, C, 128), lambda b, i, s: (b, 0, kcol)),
                  pl.BlockSpec((1, C, 128), lambda b, i, s: (b, 0, vcol))],
        out_specs=pl.BlockSpec((1, qb * blk, ATT_HEADS * HEAD_DIM), lambda b, i, s: (b, i, 0)),
    )
    return pl.pallas_call(
        functools.partial(_win_attn_kernel, seq_len=S),
        grid_spec=grid_spec,
        out_shape=jax.ShapeDtypeStruct((B, S, ATT_HEADS * HEAD_DIM), F32),
        compiler_params=_cparams("parallel", "parallel"),
        name="window_attention",
    )(sink.astype(F32), at_x, at_x, at_x, at_x, at_x, at_x, at_x, at_c, at_c)


def _ctx_attn_kernel(sink_ref, q_ref, kx_ref, vx_ref, o_ref):
    q = q_ref[0].astype(BF16)
    kx = kx_ref[0].astype(BF16)
    vx = vx_ref[0].astype(BF16)
    g = ATT_HEADS // ATT_KV_HEADS
    outs = []
    for h in range(ATT_HEADS):
        kv = h // g
        qs = q[:, h * HEAD_DIM:(h + 1) * HEAD_DIM]
        ksl = slice(kv * HEAD_DIM, (kv + 1) * HEAD_DIM)
        (p_ctx,), inv = _softmax_parts(qs, [(kx[:, ksl], None)], sink_ref[h])
        outs.append(jnp.dot(p_ctx.astype(BF16), vx[:, ksl], preferred_element_type=F32) * inv)
    o_ref[0] = jnp.concatenate(outs, axis=-1)


def context_attention(at_c, sink):
    B, C, _ = at_c.shape
    kcol = QK_COLS // 128 - 1
    grid_spec = pltpu.PrefetchScalarGridSpec(
        num_scalar_prefetch=1,
        grid=(B,),
        in_specs=[pl.BlockSpec((1, C, ATT_HEADS * HEAD_DIM), lambda b, s: (b, 0, 0)),
                  pl.BlockSpec((1, C, 128), lambda b, s: (b, 0, kcol)),
                  pl.BlockSpec((1, C, 128), lambda b, s: (b, 0, kcol + 1))],
        out_specs=pl.BlockSpec((1, C, ATT_HEADS * HEAD_DIM), lambda b, s: (b, 0, 0)),
    )
    return pl.pallas_call(
        _ctx_attn_kernel,
        grid_spec=grid_spec,
        out_shape=jax.ShapeDtypeStruct((B, C, ATT_HEADS * HEAD_DIM), F32),
        compiler_params=_cparams("parallel"),
        name="context_attention",
    )(sink.astype(F32), at_c, at_c, at_c)


def _out_proj_kernel(y0_ref, y1_ref, y2_ref, y3_ref, h_ref, g1_ref, gng_ref, w_ref,
                     n2g_ref, sh_ref, sc_ref, wr_ref, br_ref, ho_ref, nx_ref, lg_ref):
    parts = []
    for k, y_ref in enumerate((y0_ref, y1_ref, y2_ref, y3_ref)):
        y = y_ref[0]
        ms = jnp.mean(y * y, axis=-1, keepdims=True)
        yn = y * lax.rsqrt(ms + EPS) * gng_ref[:, k * D_GROUP:(k + 1) * D_GROUP]
        parts.append(yn.astype(BF16))
    yn = jnp.concatenate(parts, axis=-1)
    proj = jnp.dot(yn, w_ref[...], preferred_element_type=F32)
    h = h_ref[0] + g1_ref[0] * proj
    ho_ref[0] = h
    ms = jnp.mean(h * h, axis=-1, keepdims=True)
    n = h * lax.rsqrt(ms + EPS) * n2g_ref[...]
    n = n * (1.0 + sc_ref[0]) + sh_ref[0]
    nx_ref[0] = n
    lg_ref[0] = jnp.dot(n.astype(BF16), wr_ref[...], preferred_element_type=F32) + br_ref[...]


def out_proj(ys, h, g1, gng, w_out, n2g, sh2, sc2, w_router, b_router, tm):
    B, L, D = h.shape
    row3 = lambda n: pl.BlockSpec((1, tm, n), lambda b, i: (b, i, 0))
    mod = pl.BlockSpec((1, 1, D), lambda b, i: (b, 0, 0))
    full = lambda r, c: pl.BlockSpec((r, c), lambda b, i: (0, 0))
    return pl.pallas_call(
        _out_proj_kernel,
        grid=(B, L // tm),
        in_specs=[row3(D_GROUP)] * 4 + [row3(D), mod, full(1, D), full(D, D), full(1, D), mod, mod,
                                        full(D, ROUTER_COLS), full(1, ROUTER_COLS)],
        out_specs=[row3(D), row3(D), row3(ROUTER_COLS)],
        out_shape=[jax.ShapeDtypeStruct((B, L, D), F32), jax.ShapeDtypeStruct((B, L, D), F32),
                   jax.ShapeDtypeStruct((B, L, ROUTER_COLS), F32)],
        compiler_params=_cparams("parallel", "parallel"),
        name="out_proj",
    )(*ys, h, g1, gng.reshape(1, D), w_out, n2g.reshape(1, D), sh2, sc2, w_router, b_router)


N_PAIRS = EXP_PER_GROUP * (EXP_PER_GROUP - 1) // 2
N_CLASSES = N_GROUPS * N_PAIRS
ROUTE_TOKENS = 512
INFO_CLASS, INFO_RANK, INFO_WA, INFO_WB = 0, 1, 2, 3


def _route_kernel(lg_ref, info_ref, cnt_ref, run):
    i = pl.program_id(0)
    tb = lg_ref.shape[0]

    @pl.when(i == 0)
    def _():
        run[...] = jnp.zeros_like(run)

    lg = lg_ref[...]
    li = lax.broadcasted_iota(jnp.int32, lg.shape, 1)
    big = jnp.int32(ROUTER_COLS)

    def first_argmax(vals):
        m = jnp.max(vals, axis=-1, keepdims=True)
        return m, jnp.min(jnp.where(vals == m, li, big), axis=-1, keepdims=True)

    gl = jnp.where(li < N_GROUPS, lg, NEG_INF)
    gmax, g_idx = first_argmax(gl)
    g_prob = 1.0 / jnp.sum(jnp.exp(gl - gmax), axis=-1, keepdims=True)
    lo = N_GROUPS + EXP_PER_GROUP * g_idx
    el = jnp.where((li >= lo) & (li < lo + EXP_PER_GROUP), lg, NEG_INF)
    m1, i1 = first_argmax(el)
    m2, i2 = first_argmax(jnp.where(li == i1, NEG_INF, el))
    e2 = jnp.exp(m2 - m1)
    w1 = g_prob / (1.0 + e2)
    w2 = g_prob * e2 / (1.0 + e2)
    j1 = i1 - lo
    j2 = i2 - lo
    a = jnp.minimum(j1, j2)
    b = jnp.maximum(j1, j2)
    cls = g_idx * N_PAIRS + ((a * (2 * EXP_PER_GROUP - 1 - a)) >> 1) + (b - a - 1)
    w_a = jnp.where(j1 < j2, w1, w2)
    w_b = jnp.where(j1 < j2, w2, w1)

    hit = li == cls
    onehot = jnp.where(hit, 1.0, 0.0)
    r_i = lax.broadcasted_iota(jnp.int32, (tb, tb), 0)
    c_i = lax.broadcasted_iota(jnp.int32, (tb, tb), 1)
    below = jnp.where(c_i < r_i, 1.0, 0.0).astype(BF16)
    before = jnp.dot(below, onehot.astype(BF16), preferred_element_type=F32)
    rank = jnp.sum(jnp.where(hit, before + run[...], 0.0), axis=-1, keepdims=True)
    run[...] = run[...] + jnp.sum(onehot, axis=0, keepdims=True)
    cnt_ref[...] = run[...]
    info = jnp.where(li == INFO_CLASS, cls.astype(F32), 0.0)
    info = jnp.where(li == INFO_RANK, rank, info)
    info = jnp.where(li == INFO_WA, w_a, info)
    info = jnp.where(li == INFO_WB, w_b, info)
    info_ref[...] = info


def route_tokens(logits):
    T = logits.shape[0]
    tb = ROUTE_TOKENS
    return pl.pallas_call(
        _route_kernel,
        grid=(T // tb,),
        in_specs=[pl.BlockSpec((tb, ROUTER_COLS), lambda i: (i, 0))],
        out_specs=[pl.BlockSpec((tb, ROUTER_COLS), lambda i: (i, 0)),
                   pl.BlockSpec((1, ROUTER_COLS), lambda i: (0, 0))],
        out_shape=[jax.ShapeDtypeStruct((T, ROUTER_COLS), F32), jax.ShapeDtypeStruct((1, ROUTER_COLS), F32)],
        scratch_shapes=[pltpu.VMEM((1, ROUTER_COLS), F32)],
        compiler_params=_cparams("arbitrary"),
        name="moe_route",
    )(logits)


def _pair_tables():
    a_tab, b_tab = [], []
    for g in range(N_GROUPS):
        for a in range(EXP_PER_GROUP):
            for b in range(a + 1, EXP_PER_GROUP):
                a_tab.append(g * EXP_PER_GROUP + a)
                b_tab.append(g * EXP_PER_GROUP + b)
    return jnp.array(a_tab, jnp.int32), jnp.array(b_tab, jnp.int32)


SUBLANES = 8


def _slot_kernel(info_ref, start_ref, dest_ref):
    info = info_ref[...]
    li = lax.broadcasted_iota(jnp.int32, info.shape, 1)
    cls = info[:, INFO_CLASS:INFO_CLASS + 1].astype(jnp.int32)
    start = jnp.sum(jnp.where(li == cls, start_ref[...], 0.0), axis=-1, keepdims=True)
    dest = start + info[:, INFO_RANK:INFO_RANK + 1]
    rec = jnp.where(li == 0, dest, 0.0).T
    dest_ref[0] = rec[:SUBLANES].astype(jnp.int32)


def slot_plan(info, counts, n_blocks):
    T = info.shape[0]
    tb = ROUTE_TOKENS
    cnt = counts[0, :N_CLASSES].astype(jnp.int32)
    padded = (cnt + MOE_BLOCK - 1) // MOE_BLOCK * MOE_BLOCK
    pad_end = jnp.cumsum(padded)
    pad_start = jnp.zeros((1, ROUTER_COLS), F32).at[0, :N_CLASSES].set((pad_end - padded).astype(F32))
    dest = pl.pallas_call(
        _slot_kernel,
        grid=(T // tb,),
        in_specs=[pl.BlockSpec((tb, ROUTER_COLS), lambda i: (i, 0)),
                  pl.BlockSpec((1, ROUTER_COLS), lambda i: (0, 0))],
        out_specs=pl.BlockSpec((1, SUBLANES, tb), lambda i: (i, 0, 0)),
        out_shape=jax.ShapeDtypeStruct((T // tb, SUBLANES, tb), jnp.int32),
        compiler_params=_cparams("parallel"),
        name="moe_slots",
    )(info, pad_start)[:, 0, :].reshape(T)
    n_used = (pad_end[-1] // MOE_BLOCK).astype(jnp.int32).reshape(1)
    blk_cls = jnp.minimum(jnp.searchsorted(pad_end, jnp.arange(n_blocks) * MOE_BLOCK, side="right"),
                          N_CLASSES - 1)
    a_tab, b_tab = _pair_tables()
    return dest, a_tab[blk_cls], b_tab[blk_cls], n_used


DISPATCH_TOKENS = 256


def _wait_rows(buf, sem):
    pltpu.make_async_copy(buf, buf, sem).wait()


def _dispatch_kernel(dest_ref, x_ref, info_ref, zeros_hbm, xs_hbm, rows, sems):
    del zeros_hbm
    i = pl.program_id(0)
    n = pl.num_programs(0)
    slot = i % 2
    D = x_ref.shape[1]
    tb = x_ref.shape[0]

    @pl.when(i >= 2)
    def _():
        _wait_rows(rows.at[slot], sems.at[slot])

    rows[slot, :, :D] = x_ref[...]
    rows[slot, :, D:] = info_ref[...]

    def body(r, carry):
        pltpu.make_async_copy(rows.at[slot, r], xs_hbm.at[dest_ref[0, 0, r]], sems.at[slot]).start()
        return carry
    lax.fori_loop(0, tb, body, 0)

    @pl.when(i == n - 1)
    def _():
        _wait_rows(rows.at[slot], sems.at[slot])

        @pl.when(n >= 2)
        def _():
            _wait_rows(rows.at[1 - slot], sems.at[1 - slot])


def moe_dispatch(x_tokens, info, dest, n_blocks):
    T, D = x_tokens.shape
    tb = DISPATCH_TOKENS
    W = D + ROUTER_COLS
    P = n_blocks * MOE_BLOCK
    return pl.pallas_call(
        _dispatch_kernel,
        grid=(T // tb,),
        in_specs=[pl.BlockSpec((1, 1, tb), lambda i: (i, 0, 0), memory_space=pltpu.SMEM),
                  pl.BlockSpec((tb, D), lambda i: (i, 0)),
                  pl.BlockSpec((tb, ROUTER_COLS), lambda i: (i, 0)),
                  pl.BlockSpec(memory_space=pl.ANY)],
        out_specs=pl.BlockSpec(memory_space=pl.ANY),
        out_shape=jax.ShapeDtypeStruct((P, W), F32),
        scratch_shapes=[pltpu.VMEM((2, tb, W), F32), pltpu.SemaphoreType.DMA((2,))],
        input_output_aliases={3: 0},
        compiler_params=_cparams("arbitrary"),
        name="moe_dispatch",
    )(dest.reshape(T // tb, 1, tb), x_tokens, info, jnp.zeros((P, W), F32))


def _expert_pair_kernel(ea_ref, eb_ref, nused_ref, xs_ref, wga_ref, wua_ref, wda_ref, wgb_ref, wub_ref, wdb_ref,
                        o_ref):
    del ea_ref, eb_ref
    i = pl.program_id(0)
    D = o_ref.shape[1]

    @pl.when(i < nused_ref[0])
    def _():
        xb = xs_ref[:, :D].astype(BF16)

        def ffn(wg_ref, wu_ref, wd_ref):
            gate = jnp.dot(xb, wg_ref[0], preferred_element_type=F32)
            up = jnp.dot(xb, wu_ref[0], preferred_element_type=F32)
            hid = (gate * jax.nn.sigmoid(gate) * up).astype(BF16)
            return jnp.dot(hid, wd_ref[0], preferred_element_type=F32)

        w_a = xs_ref[:, D + INFO_WA:D + INFO_WA + 1]
        w_b = xs_ref[:, D + INFO_WB:D + INFO_WB + 1]
        o_ref[...] = w_a * ffn(wga_ref, wua_ref, wda_ref) + w_b * ffn(wgb_ref, wub_ref, wdb_ref)

    @pl.when(i >= nused_ref[0])
    def _():
        o_ref[...] = jnp.zeros_like(o_ref)


def expert_pairs(xs, blk_a, blk_b, n_used, w_gate, w_up, w_down):
    P, W = xs.shape
    D = W - ROUTER_COLS
    n_blocks = P // MOE_BLOCK
    DE = w_gate.shape[-1]
    wspec = lambda shape, which: pl.BlockSpec(shape, lambda i, ea, eb, nu: ((ea, eb)[which][i], 0, 0))
    grid_spec = pltpu.PrefetchScalarGridSpec(
        num_scalar_prefetch=3,
        grid=(n_blocks,),
        in_specs=[pl.BlockSpec((MOE_BLOCK, W), lambda i, ea, eb, nu: (i, 0)),
                  wspec((1, D, DE), 0), wspec((1, D, DE), 0), wspec((1, DE, D), 0),
                  wspec((1, D, DE), 1), wspec((1, D, DE), 1), wspec((1, DE, D), 1)],
        out_specs=pl.BlockSpec((MOE_BLOCK, D), lambda i, ea, eb, nu: (i, 0)),
    )
    return pl.pallas_call(
        _expert_pair_kernel,
        grid_spec=grid_spec,
        out_shape=jax.ShapeDtypeStruct((P, D), F32),
        compiler_params=_cparams("arbitrary"),
        name="moe_experts",
    )(blk_a, blk_b, n_used, xs, w_gate, w_up, w_down, w_gate, w_up, w_down)


def _gather_rows(idx_ref, src_hbm, buf, sem, n_rows):
    def body(r, carry):
        pltpu.make_async_copy(src_hbm.at[idx_ref[0, 0, r]], buf.at[r], sem).start()
        return carry
    lax.fori_loop(0, n_rows, body, 0)


def _collect_kernel(dest_ref, dest_next_ref, o_hbm, h_ref, g2_ref, fg_ref, out_ref, obuf, sems, *, final_norm):
    i = pl.program_id(0)
    n = pl.num_programs(0)
    slot = i % 2
    tb = h_ref.shape[0]

    @pl.when(i == 0)
    def _():
        _gather_rows(dest_ref, o_hbm, obuf.at[0], sems.at[0], tb)

    @pl.when(i + 1 < n)
    def _():
        _gather_rows(dest_next_ref, o_hbm, obuf.at[1 - slot], sems.at[1 - slot], tb)

    _wait_rows(obuf.at[slot], sems.at[slot])
    h = h_ref[...] + g2_ref[0] * obuf[slot]
    if final_norm:
        ms = jnp.mean(h * h, axis=-1, keepdims=True)
        h = h * lax.rsqrt(ms + EPS) * fg_ref[...]
    out_ref[...] = h


def moe_collect(o_sorted, dest, h_tokens, g2, tokens_per_batch, final_g, final_norm):
    T, D = h_tokens.shape
    tb = DISPATCH_TOKENS
    nt = T // tb
    per_b = tokens_per_batch // tb
    dest3 = dest.reshape(nt, 1, tb)
    return pl.pallas_call(
        functools.partial(_collect_kernel, final_norm=final_norm),
        grid=(nt,),
        in_specs=[pl.BlockSpec((1, 1, tb), lambda i: (i, 0, 0), memory_space=pltpu.SMEM),
                  pl.BlockSpec((1, 1, tb), lambda i: (jnp.minimum(i + 1, nt - 1), 0, 0), memory_space=pltpu.SMEM),
                  pl.BlockSpec(memory_space=pl.ANY),
                  pl.BlockSpec((tb, D), lambda i: (i, 0)),
                  pl.BlockSpec((1, 1, D), lambda i: (i // per_b, 0, 0)),
                  pl.BlockSpec((1, D), lambda i: (0, 0))],
        out_specs=pl.BlockSpec((tb, D), lambda i: (i, 0)),
        out_shape=jax.ShapeDtypeStruct((T, D), F32),
        scratch_shapes=[pltpu.VMEM((2, tb, D), F32), pltpu.SemaphoreType.DMA((2,))],
        compiler_params=_cparams("arbitrary"),
        name="moe_collect",
    )(dest3, dest3, o_sorted, h_tokens, g2, final_g.reshape(1, D))


CONV_MARGIN = 16


def _time_chunk(L):
    return min(L, 256)


LANES = 128


def _zero_margins(pad_ref, L):
    zeros = jnp.zeros((CONV_MARGIN, LANES), F32)
    for s in range(pad_ref.shape[0]):
        pad_ref[s, pl.ds(0, CONV_MARGIN), :] = zeros
        pad_ref[s, pl.ds(CONV_MARGIN + L, CONV_MARGIN), :] = zeros


def _dw_conv_slab(pad_ref, s, base, T, w_ref, b_ref, col, taps, pad_left):
    acc = jnp.broadcast_to(b_ref[:, col:col + LANES], (T, LANES))
    for k in range(taps):
        acc = acc + w_ref[k:k + 1, col:col + LANES] * pad_ref[s, pl.ds(base + (CONV_MARGIN - pad_left + k), T), :]
    return acc


def _conformer_kernel(u_ref, w_ref, b_ref, g_ref, beta_ref, o_ref, ypad):
    L = o_ref.shape[1]
    T = _time_chunk(L)
    C = D_GROUP
    n_slabs = C // LANES
    pad = (CONF_KERNEL - 1) // 2
    _zero_margins(ypad, L)

    def glu(j, carry):
        base = pl.multiple_of(j * T, T)
        for s in range(n_slabs):
            a = u_ref[0, pl.ds(base, T), s * LANES:(s + 1) * LANES]
            gate = u_ref[0, pl.ds(base, T), C + s * LANES:C + (s + 1) * LANES]
            ypad[s, pl.ds(CONV_MARGIN + base, T), :] = a * jax.nn.sigmoid(gate)
        return carry
    lax.fori_loop(0, L // T, glu, 0)

    def conv(j, carry):
        base = pl.multiple_of(j * T, T)
        acc = jnp.concatenate([_dw_conv_slab(ypad, s, base, T, w_ref, b_ref, s * LANES, CONF_KERNEL, pad)
                               for s in range(n_slabs)], axis=-1)
        mu = jnp.mean(acc, axis=-1, keepdims=True)
        cen = acc - mu
        var = jnp.mean(cen * cen, axis=-1, keepdims=True)
        y = cen * lax.rsqrt(var + EPS) * g_ref[...] + beta_ref[...]
        o_ref[0, pl.ds(base, T), :] = y * jax.nn.sigmoid(y)
        return carry
    lax.fori_loop(0, L // T, conv, 0)


def conformer_conv(u, w, b, ln_g, ln_b):
    B, L, _ = u.shape
    C = D_GROUP
    vec = pl.BlockSpec((1, C), lambda i: (0, 0))
    return pl.pallas_call(
        _conformer_kernel,
        grid=(B,),
        in_specs=[pl.BlockSpec((1, L, 2 * C), lambda i: (i, 0, 0)),
                  pl.BlockSpec((CONF_KERNEL, C), lambda i: (0, 0)), vec, vec, vec],
        out_specs=pl.BlockSpec((1, L, C), lambda i: (i, 0, 0)),
        out_shape=jax.ShapeDtypeStruct((B, L, C), F32),
        scratch_shapes=[pltpu.VMEM((C // LANES, L + 2 * CONV_MARGIN, LANES), F32)],
        compiler_params=_cparams("parallel"),
        name="conformer_conv",
    )(u, w, b.reshape(1, C), ln_g.reshape(1, C), ln_b.reshape(1, C))


def _gelu_tanh(x):
    return 0.5 * x * (1.0 + jnp.tanh(math.sqrt(2.0 / math.pi) * (x + 0.044715 * (x * x * x))))


def _lru_kernel(uc_ref, ux_ref, cw_ref, cb_ref, wcat_ref, bcat_ref, lam_ref, *rest, need_ctx):
    if need_ctx:
        oc_ref, ox_ref, cpad, xpad, a_s, b_s, yx, yc = rest
    else:
        ox_ref, cpad, xpad, a_s, b_s, yx = rest
        oc_ref = yc = None
    C = D_GROUP
    n_slabs = C // LANES
    Lc = uc_ref.shape[1]
    Lx = ux_ref.shape[1]
    pad_l = (LRU_CONV - 1) // 2

    def fill(pad_ref, u_ref, L):
        T = _time_chunk(L)
        _zero_margins(pad_ref, L)

        def body(j, carry):
            base = pl.multiple_of(j * T, T)
            for s in range(n_slabs):
                pad_ref[s, pl.ds(CONV_MARGIN + base, T), :] = u_ref[0, pl.ds(base, T),
                                                                    C + s * LANES:C + (s + 1) * LANES]
            return carry
        lax.fori_loop(0, L // T, body, 0)

    fill(cpad, uc_ref, Lc)
    fill(xpad, ux_ref, Lx)

    def coeffs(pad_ref, base, T, d):
        x = jnp.concatenate([_dw_conv_slab(pad_ref, s, base, T, cw_ref, cb_ref, s * LANES, LRU_CONV, pad_l)
                             for s in range(n_slabs)], axis=-1)
        g = jnp.dot(x.astype(BF16), wcat_ref[:, 2 * d * C:2 * (d + 1) * C],
                    preferred_element_type=F32) + bcat_ref[:, 2 * d * C:2 * (d + 1) * C]
        r = jax.nn.sigmoid(g[:, :C])
        i = jax.nn.sigmoid(g[:, C:])
        z = -lam_ref[d:d + 1, :]
        softplus = jnp.maximum(z, 0.0) + jnp.log(1.0 + jnp.exp(-jnp.abs(z)))
        a = jnp.exp(-LRU_C * r * softplus)
        b = jnp.sqrt(1.0 - a * a) * (i * x)
        for s in range(n_slabs):
            a_s[s, pl.ds(0, T), :] = a[:, s * LANES:(s + 1) * LANES]
            b_s[s, pl.ds(0, T), :] = b[:, s * LANES:(s + 1) * LANES]

    def scan_rows(T, reverse, h, y_ref, base, first):
        def body(step, h):
            t = (T - 1 - step) if reverse else step
            new = []
            for s in range(n_slabs):
                hs = a_s[s, pl.ds(t, 1), :] * h[s] + b_s[s, pl.ds(t, 1), :]
                if y_ref is not None:
                    if first:
                        y_ref[s, pl.ds(base + t, 1), :] = hs
                    else:
                        y_ref[s, pl.ds(base + t, 1), :] = y_ref[s, pl.ds(base + t, 1), :] + hs
                new.append(hs)
            return tuple(new)
        return lax.fori_loop(0, T, body, h, unroll=8)

    def run(pad_ref, L, d, h, y_ref):
        T = _time_chunk(L)
        n = L // T
        reverse = d == 1

        def body(j, h):
            jj = (n - 1 - j) if reverse else j
            base = pl.multiple_of(jj * T, T)
            coeffs(pad_ref, base, T, d)
            return scan_rows(T, reverse, h, y_ref, base, d == 0)
        return lax.fori_loop(0, n, body, h)

    for d in range(2):
        h = tuple(jnp.zeros((1, LANES), F32) for _ in range(n_slabs))
        h = run(cpad, Lc, d, h, yc)
        run(xpad, Lx, d, h, yx)

    def finish(u_ref, y_ref, o_ref, L):
        T = _time_chunk(L)

        def body(j, carry):
            base = pl.multiple_of(j * T, T)
            y = jnp.concatenate([y_ref[s, pl.ds(base, T), :] for s in range(n_slabs)], axis=-1)
            o_ref[0, pl.ds(base, T), :] = _gelu_tanh(u_ref[0, pl.ds(base, T), :C]) * y
            return carry
        lax.fori_loop(0, L // T, body, 0)

    finish(ux_ref, yx, ox_ref, Lx)
    if need_ctx:
        finish(uc_ref, yc, oc_ref, Lc)


def _block_diag(w):
    H, n, _ = w.shape
    eye = jnp.eye(H, dtype=w.dtype)
    return (eye[:, None, :, None] * w[:, :, None, :]).reshape(H * n, H * n)


def rglru_mixer(uc, ux, lp, need_ctx):
    B, Lc, _ = uc.shape
    Lx = ux.shape[1]
    C = D_GROUP
    wcat = jnp.concatenate([_block_diag(lp["lru_wa"][0]), _block_diag(lp["lru_wx"][0]),
                            _block_diag(lp["lru_wa"][1]), _block_diag(lp["lru_wx"][1])], axis=1).astype(BF16)
    bcat = jnp.concatenate([lp["lru_ba"][0], lp["lru_bx"][0], lp["lru_ba"][1], lp["lru_bx"][1]]).reshape(1, 4 * C)
    full = lambda r, c: pl.BlockSpec((r, c), lambda i: (0, 0))
    seq = lambda L, n: pl.BlockSpec((1, L, n), lambda i: (i, 0, 0))
    out_specs = [seq(Lx, C)]
    out_shape = [jax.ShapeDtypeStruct((B, Lx, C), F32)]
    if need_ctx:
        out_specs = [seq(Lc, C)] + out_specs
        out_shape = [jax.ShapeDtypeStruct((B, Lc, C), F32)] + out_shape
    T = _time_chunk(Lx)
    slab = lambda rows: pltpu.VMEM((C // LANES, rows, LANES), F32)
    scratch = [slab(Lc + 2 * CONV_MARGIN), slab(Lx + 2 * CONV_MARGIN), slab(T), slab(T), slab(Lx)]
    if need_ctx:
        scratch.append(slab(Lc))
    res = pl.pallas_call(
        functools.partial(_lru_kernel, need_ctx=need_ctx),
        grid=(B,),
        in_specs=[seq(Lc, 2 * C), seq(Lx, 2 * C), full(LRU_CONV, C), full(1, C), full(C, 4 * C),
                  full(1, 4 * C), full(2, C)],
        out_specs=out_specs,
        out_shape=out_shape,
        scratch_shapes=scratch,
        compiler_params=_cparams("parallel"),
        name="rglru",
    )(uc, ux, lp["lru_conv_w"], lp["lru_conv_b"].reshape(1, C), wcat, bcat, lp["lru_lambda"])
    if need_ctx:
        return res[0], res[1]
    return None, res[0]


HY_SHORT = 3


def _short_conv(pad_ref, base, T, w_ref, b_ref, c0, c1):
    return jnp.concatenate([_dw_conv_slab(pad_ref, col // LANES, base, T, w_ref, b_ref, col, HY_SHORT, 1)
                            for col in range(c0, c1, LANES)], axis=-1)


def _fill_padded(pad_ref, u_ref, L, T):
    _zero_margins(pad_ref, L)

    def body(j, carry):
        base = pl.multiple_of(j * T, T)
        for s in range(pad_ref.shape[0]):
            pad_ref[s, pl.ds(CONV_MARGIN + base, T), :] = u_ref[0, pl.ds(base, T), s * LANES:(s + 1) * LANES]
        return carry
    lax.fori_loop(0, L // T, body, 0)


def _hyena_pre_kernel(u_ref, w_ref, b_ref, z_ref, upad):
    L = u_ref.shape[1]
    T = _time_chunk(L)
    C = D_GROUP
    _fill_padded(upad, u_ref, L, T)

    def body(j, carry):
        base = pl.multiple_of(j * T, T)
        x1 = _short_conv(upad, base, T, w_ref, b_ref, C, 2 * C)
        v = _short_conv(upad, base, T, w_ref, b_ref, 2 * C, 3 * C)
        z_ref[pl.ds(base, T), :] = (x1 * v).astype(BF16)
        return carry
    lax.fori_loop(0, L // T, body, 0)


def _hyena_post_kernel(u_ref, y_ref, w_ref, b_ref, bias_ref, o_ref, upad):
    L = u_ref.shape[1]
    T = _time_chunk(L)
    C = D_GROUP
    _fill_padded(upad, u_ref, L, T)

    def body(j, carry):
        base = pl.multiple_of(j * T, T)
        x0 = _short_conv(upad, base, T, w_ref, b_ref, 0, C)
        x1 = _short_conv(upad, base, T, w_ref, b_ref, C, 2 * C)
        v = _short_conv(upad, base, T, w_ref, b_ref, 2 * C, 3 * C)
        o_ref[0, pl.ds(base, T), :] = x0 * (y_ref[pl.ds(base, T), :] + (x1 * v) * bias_ref[...])
        return carry
    lax.fori_loop(0, L // T, body, 0)


def _spectrum_kernel(f_ref, z_ref, ha_ref, hb_ref, hc_ref, y_ref):
    tf = ha_ref.shape[0]
    acc = jnp.dot(f_ref[...], z_ref[...], preferred_element_type=F32)
    zr = acc[:tf]
    zi = acc[tf:]
    y_ref[:tf, :] = (zr * ha_ref[...] - zi * hb_ref[...]).astype(BF16)
    y_ref[tf:, :] = (zr * hb_ref[...] + zi * hc_ref[...]).astype(BF16)


def _idft_kernel(f_ref, y_ref, o_ref):
    o_ref[...] = jnp.dot(f_ref[...], y_ref[...], preferred_element_type=F32)


def dft_tables(L):
    N = 2 * L
    tf = min(256, L)
    k = jnp.arange(L, dtype=jnp.int32)
    n = jnp.arange(L, dtype=jnp.int32)
    ang = (2.0 * math.pi / N) * ((k[:, None] * n[None, :]) % N).astype(F32)
    cos = jnp.cos(ang)
    sin = jnp.sin(ang)
    nyq = jnp.where(n % 2 == 0, 1.0, -1.0).astype(F32)
    f_re = cos
    f_im = (-sin).at[0].set(nyq)
    fwd = jnp.stack([f_re.reshape(L // tf, tf, L), f_im.reshape(L // tf, tf, L)], axis=1).reshape(N, L)
    ck = jnp.where(k == 0, 1.0, 2.0).astype(F32)[:, None] / N
    i_re = cos * ck
    i_im = (-sin * ck).at[0].set(nyq / N)
    inv = jnp.stack([i_re.reshape(L // tf, tf, L), i_im.reshape(L // tf, tf, L)], axis=1).reshape(N, L).T
    return fwd.astype(BF16), inv.astype(BF16)


def filter_spectrum(h_fwd, h_bwd):
    L, C = h_fwd.shape
    k = jnp.concatenate([h_fwd, jnp.zeros((1, C), F32), h_bwd[1:][::-1]], axis=0)
    hf = jnp.fft.rfft(k, axis=0)
    hr = jnp.real(hf)
    hi = jnp.imag(hf)
    a = hr[:L]
    b = hi[:L].at[0].set(0.0)
    c = hr[:L].at[0].set(hr[L])
    return a, b, c


def hyena_mixer(u, lp, tables):
    B, L, _ = u.shape
    C = D_GROUP
    N = 2 * L
    fwd, inv = tables
    tf = min(256, L)
    T = _time_chunk(L)
    w, bsh = lp["hy_short_w"], lp["hy_short_b"].reshape(1, 3 * C)
    z2 = pl.pallas_call(
        _hyena_pre_kernel,
        grid=(B,),
        in_specs=[pl.BlockSpec((1, L, 3 * C), lambda b: (b, 0, 0)),
                  pl.BlockSpec((HY_SHORT, 3 * C), lambda b: (0, 0)),
                  pl.BlockSpec((1, 3 * C), lambda b: (0, 0))],
        out_specs=pl.BlockSpec((L, C), lambda b: (0, b)),
        out_shape=jax.ShapeDtypeStruct((L, B * C), BF16),
        scratch_shapes=[pltpu.VMEM((3 * C // LANES, L + 2 * CONV_MARGIN, LANES), F32)],
        compiler_params=_cparams("parallel"),
        name="hyena_pre",
    )(u, w, bsh)

    h_fwd, h_bwd = _hyena_filters(L, lp)
    tn = 2 * C
    ha, hb, hc = [jnp.tile(t, (1, tn // C)) for t in filter_spectrum(h_fwd, h_bwd)]
    hspec = pl.BlockSpec((tf, tn), lambda i, j: (i, 0))
    y2 = pl.pallas_call(
        _spectrum_kernel,
        grid=(L // tf, B * C // tn),
        in_specs=[pl.BlockSpec((2 * tf, L), lambda i, j: (i, 0)),
                  pl.BlockSpec((L, tn), lambda i, j: (0, j)), hspec, hspec, hspec],
        out_specs=pl.BlockSpec((2 * tf, tn), lambda i, j: (i, j)),
        out_shape=jax.ShapeDtypeStruct((N, B * C), BF16),
        compiler_params=_cparams("parallel", "parallel"),
        name="hyena_spectrum",
    )(fwd, z2, ha, hb, hc)

    tl = min(256, L)
    yt = pl.pallas_call(
        _idft_kernel,
        grid=(L // tl, B * C // tn),
        in_specs=[pl.BlockSpec((tl, N), lambda i, j: (i, 0)),
                  pl.BlockSpec((N, tn), lambda i, j: (0, j))],
        out_specs=pl.BlockSpec((tl, tn), lambda i, j: (i, j)),
        out_shape=jax.ShapeDtypeStruct((L, B * C), F32),
        compiler_params=_cparams("parallel", "parallel"),
        name="hyena_idft",
    )(inv, y2)

    return pl.pallas_call(
        _hyena_post_kernel,
        grid=(B,),
        in_specs=[pl.BlockSpec((1, L, 3 * C), lambda b: (b, 0, 0)),
                  pl.BlockSpec((L, C), lambda b: (0, b)),
                  pl.BlockSpec((HY_SHORT, 3 * C), lambda b: (0, 0)),
                  pl.BlockSpec((1, 3 * C), lambda b: (0, 0)),
                  pl.BlockSpec((1, C), lambda b: (0, 0))],
        out_specs=pl.BlockSpec((1, L, C), lambda b: (b, 0, 0)),
        out_shape=jax.ShapeDtypeStruct((B, L, C), F32),
        scratch_shapes=[pltpu.VMEM((3 * C // LANES, L + 2 * CONV_MARGIN, LANES), F32)],
        compiler_params=_cparams("parallel"),
        name="hyena_post",
    )(u, yt, w, bsh, lp["hy_bias"].reshape(1, C))


def _hyena_filters(L, lp):
    t = jnp.linspace(0.0, 1.0, L, dtype=F32)[:, None]
    bands = (HY_EMB - 1) // 2
    w = 2.0 * math.pi * jnp.arange(L, dtype=F32)[:, None] / L
    f = jnp.linspace(1e-4, bands - 1, bands, dtype=F32)[None]
    z = jnp.concatenate([t, jnp.cos(f * w), -jnp.sin(f * w)], axis=-1)
    hdn = jnp.sin(z @ lp["hy_ffn_w1"] + lp["hy_ffn_b1"])
    hdn = jnp.sin(hdn @ lp["hy_ffn_w2"] + lp["hy_ffn_b2"])
    h = (hdn @ lp["hy_ffn_w3"]).reshape(L, 2, D_GROUP)
    max_decay = math.log(HY_TARGET) / HY_FAST_DECAY
    min_decay = math.log(HY_TARGET) / HY_SLOW_DECAY
    deltas = jnp.linspace(min_decay, max_decay, D_GROUP, dtype=F32)
    h = h * jnp.exp(-t * jnp.abs(deltas))[:, None, :]
    h = h / (jnp.sum(jnp.abs(h), axis=(0, 1), keepdims=True) + EPS)
    return h[:, 0], h[:, 1]


def _layer(hc, hx, c_silu_all, lp, need_ctx, final_g, final_norm, tables_x, tables_c):
    B, S, D = hx.shape
    C = hc.shape[1]
    mod = small_linear(c_silu_all, lp["ada_w"], lp["ada_b"])
    mod_x = mod[:B].reshape(B, 6, 1, D)
    mod_c = jnp.broadcast_to(mod[B].reshape(1, 6, 1, D), (B, 6, 1, D))
    w_ext = extend_w_in(lp["w_in"])
    cos_x, sin_x = rope_tables(S, True)
    cos_c, sin_c = rope_tables(C, False)
    hy_x, cf_x, at_x, lr_x = in_proj(hx, mod_x[:, 0], mod_x[:, 1], lp["norm1_g"], w_ext, cos_x, sin_x, tm=256)
    hy_c, cf_c, at_c, lr_c = in_proj(hc, mod_c[:, 0], mod_c[:, 1], lp["norm1_g"], w_ext, cos_c, sin_c, tm=256)

    yd_c, yd_x = rglru_mixer(lr_c, lr_x, lp, need_ctx)
    conf = lambda u: conformer_conv(u, lp["conf_dw_w"], lp["conf_dw_b"], lp["conf_ln_g"], lp["conf_ln_b"])
    ys_x = [hyena_mixer(hy_x, lp, tables_x), conf(cf_x),
            window_attention(at_x, at_c, lp["attn_sink"]), yd_x]

    w_out = lp["w_out"].astype(BF16)
    w_router = jnp.zeros((D, ROUTER_COLS), F32)
    w_router = w_router.at[:, :N_GROUPS].set(lp["router_g_w"]).at[:, N_GROUPS:N_GROUPS + N_EXPERTS].set(lp["router_e_w"])
    w_router = w_router.astype(BF16)
    b_router = jnp.zeros((1, ROUTER_COLS), F32)
    b_router = b_router.at[0, :N_GROUPS].set(lp["router_g_b"]).at[0, N_GROUPS:N_GROUPS + N_EXPERTS].set(lp["router_e_b"])

    hx1, nx, lg_x = out_proj(ys_x, hx, mod_x[:, 2], lp["group_norm_g"], w_out, lp["norm2_g"],
                             mod_x[:, 3], mod_x[:, 4], w_router, b_router, tm=256)
    h_tok = hx1.reshape(B * S, D)
    n_tok = nx.reshape(B * S, D)
    lg = lg_x.reshape(B * S, ROUTER_COLS)
    if need_ctx:
        ys_c = [hyena_mixer(hy_c, lp, tables_c), conf(cf_c),
                context_attention(at_c, lp["attn_sink"]), yd_c]
        hc1, nc, lg_c = out_proj(ys_c, hc, mod_c[:, 2], lp["group_norm_g"], w_out, lp["norm2_g"],
                                 mod_c[:, 3], mod_c[:, 4], w_router, b_router, tm=256)
        n_tok = jnp.concatenate([n_tok, nc.reshape(B * C, D)], axis=0)
        lg = jnp.concatenate([lg, lg_c.reshape(B * C, ROUTER_COLS)], axis=0)

    T = n_tok.shape[0]
    n_blocks = -(-T // MOE_BLOCK) + N_CLASSES
    info, counts = route_tokens(lg)
    dest, blk_a, blk_b, n_used = slot_plan(info, counts, n_blocks)
    xs = moe_dispatch(n_tok, info, dest, n_blocks)
    o_sorted = expert_pairs(xs, blk_a, blk_b, n_used, lp["exp_w_gate"].astype(BF16),
                            lp["exp_w_up"].astype(BF16), lp["exp_w_down"].astype(BF16))
    hx2 = moe_collect(o_sorted, dest[:B * S], h_tok, mod_x[:, 5], S, final_g, final_norm)
    hx2 = hx2.reshape(B, S, D)
    if need_ctx:
        hc2 = moe_collect(o_sorted, dest[B * S:], hc1.reshape(B * C, D), mod_c[:, 5], C,
                          final_g, False).reshape(B, C, D)
    else:
        hc2 = hc
    return hc2, hx2


def kernel(x, c, ctx, c_ctx, norm1_g, norm2_g, ada_w, ada_b, w_in, hy_short_w, hy_short_b, hy_ffn_w1, hy_ffn_b1, hy_ffn_w2, hy_ffn_b2, hy_ffn_w3, hy_bias, conf_dw_w, conf_dw_b, conf_ln_g, conf_ln_b, attn_sink, lru_conv_w, lru_conv_b, lru_wa, lru_ba, lru_wx, lru_bx, lru_lambda, group_norm_g, w_out, router_g_w, router_g_b, router_e_w, router_e_b, exp_w_gate, exp_w_up, exp_w_down, final_norm_g):
    stacked = dict(norm1_g=norm1_g, norm2_g=norm2_g, ada_w=ada_w, ada_b=ada_b, w_in=w_in,
                   hy_short_w=hy_short_w, hy_short_b=hy_short_b, hy_ffn_w1=hy_ffn_w1, hy_ffn_b1=hy_ffn_b1,
                   hy_ffn_w2=hy_ffn_w2, hy_ffn_b2=hy_ffn_b2, hy_ffn_w3=hy_ffn_w3, hy_bias=hy_bias,
                   conf_dw_w=conf_dw_w, conf_dw_b=conf_dw_b, conf_ln_g=conf_ln_g, conf_ln_b=conf_ln_b,
                   attn_sink=attn_sink, lru_conv_w=lru_conv_w, lru_conv_b=lru_conv_b, lru_wa=lru_wa,
                   lru_ba=lru_ba, lru_wx=lru_wx, lru_bx=lru_bx, lru_lambda=lru_lambda,
                   group_norm_g=group_norm_g, w_out=w_out, router_g_w=router_g_w, router_g_b=router_g_b,
                   router_e_w=router_e_w, router_e_b=router_e_b, exp_w_gate=exp_w_gate,
                   exp_w_up=exp_w_up, exp_w_down=exp_w_down)
    depth = norm1_g.shape[0]
    B = x.shape[0]
    cs = jnp.concatenate([jax.nn.silu(c), jnp.broadcast_to(jax.nn.silu(c_ctx)[None], (8, c.shape[1]))], axis=0)
    hc, hx = ctx, x
    tables_x = dft_tables(x.shape[1])
    tables_c = dft_tables(ctx.shape[1])
    for l in range(depth):
        lp = {k: v[l] for k, v in stacked.items()}
        hc, hx = _layer(hc, hx, cs, lp, need_ctx=(l < depth - 1), final_g=final_norm_g,
                        final_norm=(l == depth - 1), tables_x=tables_x, tables_c=tables_c)
    return hx
```

```python
import functools
import math

import jax
import jax.numpy as jnp
from jax import lax
from jax.experimental import pallas as pl
from jax.experimental.pallas import tpu as pltpu

F32 = jnp.float32
BF16 = jnp.bfloat16

EPS = 1e-6
NEG_INF = -1e30
GRID_W = 64
N_MIXERS = 4
D_GROUP = 256
HY_COLS = 3 * D_GROUP
CONF_COLS = 2 * D_GROUP
ATT_HEADS = 4
ATT_KV_HEADS = 2
HEAD_DIM = 64
ATT_COLS = (ATT_HEADS + 2 * ATT_KV_HEADS) * HEAD_DIM
LRU_COLS = 2 * D_GROUP
QK_COLS = (ATT_HEADS + ATT_KV_HEADS) * HEAD_DIM
WINDOW = 128
ATT_BLOCK = 128
ROPE_BASE = 10000.0
HY_EMB = 33
HY_FAST_DECAY = 0.3
HY_SLOW_DECAY = 1.5
HY_TARGET = 1e-2
CONF_KERNEL = 31
LRU_HEADS = 4
LRU_CONV = 4
LRU_C = 8.0
N_GROUPS = 4
EXP_PER_GROUP = 8
N_EXPERTS = N_GROUPS * EXP_PER_GROUP
TOP_K = 2
MOE_BLOCK = 256
ROUTER_COLS = 128

VMEM_LIMIT_BYTES = 56 * 1024 * 1024


def _cparams(*sem):
    return pltpu.CompilerParams(dimension_semantics=sem, vmem_limit_bytes=VMEM_LIMIT_BYTES)


def _linear_kernel(x_ref, w_ref, b_ref, o_ref):
    o_ref[...] = jnp.dot(x_ref[...], w_ref[...], preferred_element_type=F32,
                         precision=lax.Precision.HIGHEST) + b_ref[...]


def small_linear(x, w, b, tn=1024):
    M, K = x.shape
    N = w.shape[1]
    return pl.pallas_call(
        _linear_kernel,
        grid=(N // tn,),
        in_specs=[pl.BlockSpec((M, K), lambda j: (0, 0)),
                  pl.BlockSpec((K, tn), lambda j: (0, j)),
                  pl.BlockSpec((1, tn), lambda j: (0, j))],
        out_specs=pl.BlockSpec((M, tn), lambda j: (0, j)),
        out_shape=jax.ShapeDtypeStruct((M, N), F32),
        compiler_params=_cparams("parallel"),
        name="ada_linear",
    )(x, w, b.reshape(1, N))


def _in_proj_kernel(x_ref, sh_ref, sc_ref, g_ref, w_ref, cos_ref, sin_ref,
                    hy_ref, cf_ref, at_ref, lr_ref):
    x = x_ref[0]
    ms = jnp.mean(x * x, axis=-1, keepdims=True)
    y = x * lax.rsqrt(ms + EPS) * g_ref[...]
    y = y * (1.0 + sc_ref[0]) + sh_ref[0]
    u = jnp.dot(y.astype(BF16), w_ref[...], preferred_element_type=F32)
    c0 = HY_COLS
    c1 = c0 + CONF_COLS
    c2 = c1 + ATT_COLS
    c3 = c2 + LRU_COLS
    hy_ref[0] = u[:, :c0]
    cf_ref[0] = u[:, c0:c1]
    lr_ref[0] = u[:, c2:c3]
    qk = u[:, c1:c1 + QK_COLS]
    qk_rot = u[:, c3:c3 + QK_COLS]
    at_ref[0, :, :QK_COLS] = qk * cos_ref[...] + qk_rot * sin_ref[...]
    at_ref[0, :, QK_COLS:] = u[:, c1 + QK_COLS:c2]


def in_proj(h, shift, scale, g, w_ext, cos_t, sin_t, tm):
    B, L, D = h.shape
    NW = w_ext.shape[1]
    outs = [HY_COLS, CONF_COLS, ATT_COLS, LRU_COLS]
    return pl.pallas_call(
        _in_proj_kernel,
        grid=(B, L // tm),
        in_specs=[pl.BlockSpec((1, tm, D), lambda b, i: (b, i, 0)),
                  pl.BlockSpec((1, 1, D), lambda b, i: (b, 0, 0)),
                  pl.BlockSpec((1, 1, D), lambda b, i: (b, 0, 0)),
                  pl.BlockSpec((1, D), lambda b, i: (0, 0)),
                  pl.BlockSpec((D, NW), lambda b, i: (0, 0)),
                  pl.BlockSpec((tm, QK_COLS), lambda b, i: (i, 0)),
                  pl.BlockSpec((tm, QK_COLS), lambda b, i: (i, 0))],
        out_specs=[pl.BlockSpec((1, tm, n), lambda b, i: (b, i, 0)) for n in outs],
        out_shape=[jax.ShapeDtypeStruct((B, L, n), F32) for n in outs],
        compiler_params=_cparams("parallel", "parallel"),
        name="in_proj",
    )(h, shift, scale, g.reshape(1, D), w_ext, cos_t, sin_t)


def rope_tables(L, rotary):
    n_heads = ATT_HEADS + ATT_KV_HEADS
    if not rotary:
        return jnp.ones((L, QK_COLS), F32), jnp.zeros((L, QK_COLS), F32)
    pos = jnp.arange(L)
    row = (pos // GRID_W).astype(F32)
    col = (pos % GRID_W).astype(F32)
    half = HEAD_DIM // 2
    inv_freq = ROPE_BASE ** (-jnp.arange(0, half, 2, dtype=F32) / half)
    ang_r = row[:, None] * inv_freq[None]
    ang_c = col[:, None] * inv_freq[None]
    cos_h = jnp.concatenate([jnp.cos(ang_r)] * 2 + [jnp.cos(ang_c)] * 2, axis=-1)
    sin_h = jnp.concatenate([jnp.sin(ang_r)] * 2 + [jnp.sin(ang_c)] * 2, axis=-1)
    return jnp.tile(cos_h, (1, n_heads)), jnp.tile(sin_h, (1, n_heads))


def extend_w_in(w_in):
    c1 = HY_COLS + CONF_COLS
    wqk = w_in[:, c1:c1 + QK_COLS]
    D = w_in.shape[0]
    w4 = wqk.reshape(D, QK_COLS // 32, 2, 16)
    wrot = jnp.stack([-w4[:, :, 1], w4[:, :, 0]], axis=2).reshape(D, QK_COLS)
    return jnp.concatenate([w_in, wrot], axis=1).astype(BF16)


def _softmax_parts(q, k_list, extra_logit):
    scale = HEAD_DIM ** -0.5
    s_list = []
    for k, mask in k_list:
        s = lax.dot_general(q, k, (((1,), (1,)), ((), ())), preferred_element_type=F32) * scale
        if mask is not None:
            s = jnp.where(mask, s, NEG_INF)
        s_list.append(s)
    m = extra_logit
    for s in s_list:
        m = jnp.maximum(m, jnp.max(s, axis=-1, keepdims=True))
    p_list = [jnp.exp(s - m) for s in s_list]
    denom = jnp.exp(extra_logit - m)
    for p in p_list:
        denom = denom + jnp.sum(p, axis=-1, keepdims=True)
    return p_list, 1.0 / denom


ATT_Q_BLOCKS = 4


def _win_attn_kernel(sink_ref, q_ref, kp_ref, kc_ref, kn_ref, vp_ref, vc_ref, vn_ref,
                     kx_ref, vx_ref, o_ref, *, seq_len):
    i = pl.program_id(1)
    blk = ATT_BLOCK
    qb = q_ref.shape[1] // blk
    scale = HEAD_DIM ** -0.5
    g = ATT_HEADS // ATT_KV_HEADS
    kw = jnp.concatenate([kp_ref[0], kc_ref[0], kn_ref[0]], axis=0)
    vw = jnp.concatenate([vp_ref[0], vc_ref[0], vn_ref[0]], axis=0).astype(BF16)
    kwt = kw.T.astype(BF16)
    kxt = kx_ref[0].T.astype(BF16)
    vx = vx_ref[0].astype(BF16)
    row = lax.broadcasted_iota(jnp.int32, (g * blk, 3 * blk), 0) % blk
    col = lax.broadcasted_iota(jnp.int32, (g * blk, 3 * blk), 1)
    in_band = jnp.abs(col - blk - row) <= WINDOW
    for j in range(qb):
        q_blk = i * qb + j
        k_pos = (q_blk - 1) * blk + col
        valid = in_band & (k_pos >= 0) & (k_pos < seq_len)
        outs = []
        for kv in range(ATT_KV_HEADS):
            ksl = slice(kv * HEAD_DIM, (kv + 1) * HEAD_DIM)
            heads = range(kv * g, (kv + 1) * g)
            qs = jnp.concatenate([q_ref[0, j * blk:(j + 1) * blk, h * HEAD_DIM:(h + 1) * HEAD_DIM]
                                  for h in heads], axis=0).astype(BF16)
            sink = jnp.concatenate([jnp.full((blk, 1), sink_ref[h], F32) for h in heads], axis=0)
            s_win = jnp.dot(qs, kwt[ksl, j * blk:(j + 3) * blk], preferred_element_type=F32) * scale
            s_win = jnp.where(valid, s_win, NEG_INF)
            s_ctx = jnp.dot(qs, kxt[ksl, :], preferred_element_type=F32) * scale
            m = jnp.maximum(jnp.maximum(jnp.max(s_win, axis=-1, keepdims=True),
                                        jnp.max(s_ctx, axis=-1, keepdims=True)), sink)
            p_win = jnp.exp(s_win - m)
            p_ctx = jnp.exp(s_ctx - m)
            denom = (jnp.exp(sink - m) + jnp.sum(p_win, axis=-1, keepdims=True)
                     + jnp.sum(p_ctx, axis=-1, keepdims=True))
            o = (jnp.dot(p_win.astype(BF16), vw[j * blk:(j + 3) * blk, ksl], preferred_element_type=F32)
                 + jnp.dot(p_ctx.astype(BF16), vx[:, ksl], preferred_element_type=F32)) * (1.0 / denom)
            outs.extend([o[k * blk:(k + 1) * blk] for k in range(g)])
        o_ref[0, j * blk:(j + 1) * blk, :] = jnp.concatenate(outs, axis=-1)


def window_attention(at_x, at_c, sink):
    B, S, _ = at_x.shape
    C = at_c.shape[1]
    blk = ATT_BLOCK
    qb = ATT_Q_BLOCKS
    nb = S // blk
    kcol = QK_COLS // 128 - 1
    vcol = kcol + 1

    def edge_spec(col, off):
        return pl.BlockSpec((1, blk, 128), lambda b, i, s: (b, jnp.clip(i * qb + off, 0, nb - 1), col))

    def mid_spec(col):
        return pl.BlockSpec((1, qb * blk, 128), lambda b, i, s: (b, i, col))

    grid_spec = pltpu.PrefetchScalarGridSpec(
        num_scalar_prefetch=1,
        grid=(B, nb // qb),
        in_specs=[pl.BlockSpec((1, qb * blk, ATT_HEADS * HEAD_DIM), lambda b, i, s: (b, i, 0)),
                  edge_spec(kcol, -1), mid_spec(kcol), edge_spec(kcol, qb),
                  edge_spec(vcol, -1), mid_spec(vcol), edge_spec(vcol, qb),
                  pl.BlockSpec((1, C, 128), lambda b, i, s: (b, 0, kcol)),
                  pl.BlockSpec((1, C, 128), lambda b, i, s: (b, 0, vcol))],
        out_specs=pl.BlockSpec((1, qb * blk, ATT_HEADS * HEAD_DIM), lambda b, i, s: (b, i, 0)),
    )
    return pl.pallas_call(
        functools.partial(_win_attn_kernel, seq_len=S),
        grid_spec=grid_spec,
        out_shape=jax.ShapeDtypeStruct((B, S, ATT_HEADS * HEAD_DIM), F32),
        compiler_params=_cparams("parallel", "parallel"),
        name="window_attention",
    )(sink.astype(F32), at_x, at_x, at_x, at_x, at_x, at_x, at_x, at_c, at_c)


def _ctx_attn_kernel(sink_ref, q_ref, kx_ref, vx_ref, o_ref):
    q = q_ref[0].astype(BF16)
    kx = kx_ref[0].astype(BF16)
    vx = vx_ref[0].astype(BF16)
    g = ATT_HEADS // ATT_KV_HEADS
    outs = []
    for h in range(ATT_HEADS):
        kv = h // g
        qs = q[:, h * HEAD_DIM:(h + 1) * HEAD_DIM]
        ksl = slice(kv * HEAD_DIM, (kv + 1) * HEAD_DIM)
        (p_ctx,), inv = _softmax_parts(qs, [(kx[:, ksl], None)], sink_ref[h])
        outs.append(jnp.dot(p_ctx.astype(BF16), vx[:, ksl], preferred_element_type=F32) * inv)
    o_ref[0] = jnp.concatenate(outs, axis=-1)


def context_attention(at_c, sink):
    B, C, _ = at_c.shape
    kcol = QK_COLS // 128 - 1
    grid_spec = pltpu.PrefetchScalarGridSpec(
        num_scalar_prefetch=1,
        grid=(B,),
        in_specs=[pl.BlockSpec((1, C, ATT_HEADS * HEAD_DIM), lambda b, s: (b, 0, 0)),
                  pl.BlockSpec((1, C, 128), lambda b, s: (b, 0, kcol)),
                  pl.BlockSpec((1, C, 128), lambda b, s: (b, 0, kcol + 1))],
        out_specs=pl.BlockSpec((1, C, ATT_HEADS * HEAD_DIM), lambda b, s: (b, 0, 0)),
    )
    return pl.pallas_call(
        _ctx_attn_kernel,
        grid_spec=grid_spec,
        out_shape=jax.ShapeDtypeStruct((B, C, ATT_HEADS * HEAD_DIM), F32),
        compiler_params=_cparams("parallel"),
        name="context_attention",
    )(sink.astype(F32), at_c, at_c, at_c)


def _out_proj_kernel(y0_ref, y1_ref, y2_ref, y3_ref, h_ref, g1_ref, gng_ref, w_ref,
                     n2g_ref, sh_ref, sc_ref, wr_ref, br_ref, ho_ref, nx_ref, lg_ref):
    parts = []
    for k, y_ref in enumerate((y0_ref, y1_ref, y2_ref, y3_ref)):
        y = y_ref[0]
        ms = jnp.mean(y * y, axis=-1, keepdims=True)
        yn = y * lax.rsqrt(ms + EPS) * gng_ref[:, k * D_GROUP:(k + 1) * D_GROUP]
        parts.append(yn.astype(BF16))
    yn = jnp.concatenate(parts, axis=-1)
    proj = jnp.dot(yn, w_ref[...], preferred_element_type=F32)
    h = h_ref[0] + g1_ref[0] * proj
    ho_ref[0] = h
    ms = jnp.mean(h * h, axis=-1, keepdims=True)
    n = h * lax.rsqrt(ms + EPS) * n2g_ref[...]
    n = n * (1.0 + sc_ref[0]) + sh_ref[0]
    nx_ref[0] = n
    lg_ref[0] = jnp.dot(n.astype(BF16), wr_ref[...], preferred_element_type=F32) + br_ref[...]


def out_proj(ys, h, g1, gng, w_out, n2g, sh2, sc2, w_router, b_router, tm):
    B, L, D = h.shape
    row3 = lambda n: pl.BlockSpec((1, tm, n), lambda b, i: (b, i, 0))
    mod = pl.BlockSpec((1, 1, D), lambda b, i: (b, 0, 0))
    full = lambda r, c: pl.BlockSpec((r, c), lambda b, i: (0, 0))
    return pl.pallas_call(
        _out_proj_kernel,
        grid=(B, L // tm),
        in_specs=[row3(D_GROUP)] * 4 + [row3(D), mod, full(1, D), full(D, D), full(1, D), mod, mod,
                                        full(D, ROUTER_COLS), full(1, ROUTER_COLS)],
        out_specs=[row3(D), row3(D), row3(ROUTER_COLS)],
        out_shape=[jax.ShapeDtypeStruct((B, L, D), F32), jax.ShapeDtypeStruct((B, L, D), F32),
                   jax.ShapeDtypeStruct((B, L, ROUTER_COLS), F32)],
        compiler_params=_cparams("parallel", "parallel"),
        name="out_proj",
    )(*ys, h, g1, gng.reshape(1, D), w_out, n2g.reshape(1, D), sh2, sc2, w_router, b_router)


N_PAIRS = EXP_PER_GROUP * (EXP_PER_GROUP - 1) // 2
N_CLASSES = N_GROUPS * N_PAIRS
ROUTE_TOKENS = 512
INFO_CLASS, INFO_RANK, INFO_WA, INFO_WB = 0, 1, 2, 3


def _route_kernel(lg_ref, info_ref, cnt_ref, run):
    i = pl.program_id(0)
    tb = lg_ref.shape[0]

    @pl.when(i == 0)
    def _():
        run[...] = jnp.zeros_like(run)

    lg = lg_ref[...]
    li = lax.broadcasted_iota(jnp.int32, lg.shape, 1)
    big = jnp.int32(ROUTER_COLS)

    def first_argmax(vals):
        m = jnp.max(vals, axis=-1, keepdims=True)
        return m, jnp.min(jnp.where(vals == m, li, big), axis=-1, keepdims=True)

    gl = jnp.where(li < N_GROUPS, lg, NEG_INF)
    gmax, g_idx = first_argmax(gl)
    g_prob = 1.0 / jnp.sum(jnp.exp(gl - gmax), axis=-1, keepdims=True)
    lo = N_GROUPS + EXP_PER_GROUP * g_idx
    el = jnp.where((li >= lo) & (li < lo + EXP_PER_GROUP), lg, NEG_INF)
    m1, i1 = first_argmax(el)
    m2, i2 = first_argmax(jnp.where(li == i1, NEG_INF, el))
    e2 = jnp.exp(m2 - m1)
    w1 = g_prob / (1.0 + e2)
    w2 = g_prob * e2 / (1.0 + e2)
    j1 = i1 - lo
    j2 = i2 - lo
    a = jnp.minimum(j1, j2)
    b = jnp.maximum(j1, j2)
    cls = g_idx * N_PAIRS + ((a * (2 * EXP_PER_GROUP - 1 - a)) >> 1) + (b - a - 1)
    w_a = jnp.where(j1 < j2, w1, w2)
    w_b = jnp.where(j1 < j2, w2, w1)

    hit = li == cls
    onehot = jnp.where(hit, 1.0, 0.0)
    r_i = lax.broadcasted_iota(jnp.int32, (tb, tb), 0)
    c_i = lax.broadcasted_iota(jnp.int32, (tb, tb), 1)
    below = jnp.where(c_i < r_i, 1.0, 0.0).astype(BF16)
    before = jnp.dot(below, onehot.astype(BF16), preferred_element_type=F32)
    rank = jnp.sum(jnp.where(hit, before + run[...], 0.0), axis=-1, keepdims=True)
    run[...] = run[...] + jnp.sum(onehot, axis=0, keepdims=True)
    cnt_ref[...] = run[...]
    info = jnp.where(li == INFO_CLASS, cls.astype(F32), 0.0)
    info = jnp.where(li == INFO_RANK, rank, info)
    info = jnp.where(li == INFO_WA, w_a, info)
    info = jnp.where(li == INFO_WB, w_b, info)
    info_ref[...] = info


def route_tokens(logits):
    T = logits.shape[0]
    tb = ROUTE_TOKENS
    return pl.pallas_call(
        _route_kernel,
        grid=(T // tb,),
        in_specs=[pl.BlockSpec((tb, ROUTER_COLS), lambda i: (i, 0))],
        out_specs=[pl.BlockSpec((tb, ROUTER_COLS), lambda i: (i, 0)),
                   pl.BlockSpec((1, ROUTER_COLS), lambda i: (0, 0))],
        out_shape=[jax.ShapeDtypeStruct((T, ROUTER_COLS), F32), jax.ShapeDtypeStruct((1, ROUTER_COLS), F32)],
        scratch_shapes=[pltpu.VMEM((1, ROUTER_COLS), F32)],
        compiler_params=_cparams("arbitrary"),
        name="moe_route",
    )(logits)


def _pair_tables():
    a_tab, b_tab = [], []
    for g in range(N_GROUPS):
        for a in range(EXP_PER_GROUP):
            for b in range(a + 1, EXP_PER_GROUP):
                a_tab.append(g * EXP_PER_GROUP + a)
                b_tab.append(g * EXP_PER_GROUP + b)
    return jnp.array(a_tab, jnp.int32), jnp.array(b_tab, jnp.int32)


SUBLANES = 8


def _slot_kernel(info_ref, start_ref, dest_ref):
    info = info_ref[...]
    li = lax.broadcasted_iota(jnp.int32, info.shape, 1)
    cls = info[:, INFO_CLASS:INFO_CLASS + 1].astype(jnp.int32)
    start = jnp.sum(jnp.where(li == cls, start_ref[...], 0.0), axis=-1, keepdims=True)
    dest = start + info[:, INFO_RANK:INFO_RANK + 1]
    rec = jnp.where(li == 0, dest, 0.0).T
    dest_ref[0] = rec[:SUBLANES].astype(jnp.int32)


def slot_plan(info, counts, n_blocks):
    T = info.shape[0]
    tb = ROUTE_TOKENS
    cnt = counts[0, :N_CLASSES].astype(jnp.int32)
    padded = (cnt + MOE_BLOCK - 1) // MOE_BLOCK * MOE_BLOCK
    pad_end = jnp.cumsum(padded)
    pad_start = jnp.zeros((1, ROUTER_COLS), F32).at[0, :N_CLASSES].set((pad_end - padded).astype(F32))
    dest = pl.pallas_call(
        _slot_kernel,
        grid=(T // tb,),
        in_specs=[pl.BlockSpec((tb, ROUTER_COLS), lambda i: (i, 0)),
                  pl.BlockSpec((1, ROUTER_COLS), lambda i: (0, 0))],
        out_specs=pl.BlockSpec((1, SUBLANES, tb), lambda i: (i, 0, 0)),
        out_shape=jax.ShapeDtypeStruct((T // tb, SUBLANES, tb), jnp.int32),
        compiler_params=_cparams("parallel"),
        name="moe_slots",
    )(info, pad_start)[:, 0, :].reshape(T)
    n_used = (pad_end[-1] // MOE_BLOCK).astype(jnp.int32).reshape(1)
    blk_cls = jnp.minimum(jnp.searchsorted(pad_end, jnp.arange(n_blocks) * MOE_BLOCK, side="right"),
                          N_CLASSES - 1)
    a_tab, b_tab = _pair_tables()
    return dest, a_tab[blk_cls], b_tab[blk_cls], n_used


DISPATCH_TOKENS = 256


def _wait_rows(buf, sem):
    pltpu.make_async_copy(buf, buf, sem).wait()


def _dispatch_kernel(dest_ref, x_ref, info_ref, zeros_hbm, xs_hbm, rows, sems):
    del zeros_hbm
    i = pl.program_id(0)
    n = pl.num_programs(0)
    slot = i % 2
    D = x_ref.shape[1]
    tb = x_ref.shape[0]

    @pl.when(i >= 2)
    def _():
        _wait_rows(rows.at[slot], sems.at[slot])

    rows[slot, :, :D] = x_ref[...]
    rows[slot, :, D:] = info_ref[...]

    def body(r, carry):
        pltpu.make_async_copy(rows.at[slot, r], xs_hbm.at[dest_ref[0, 0, r]], sems.at[slot]).start()
        return carry
    lax.fori_loop(0, tb, body, 0)

    @pl.when(i == n - 1)
    def _():
        _wait_rows(rows.at[slot], sems.at[slot])

        @pl.when(n >= 2)
        def _():
            _wait_rows(rows.at[1 - slot], sems.at[1 - slot])


def moe_dispatch(x_tokens, info, dest, n_blocks):
    T, D = x_tokens.shape
    tb = DISPATCH_TOKENS
    W = D + ROUTER_COLS
    P = n_blocks * MOE_BLOCK
    return pl.pallas_call(
        _dispatch_kernel,
        grid=(T // tb,),
        in_specs=[pl.BlockSpec((1, 1, tb), lambda i: (i, 0, 0), memory_space=pltpu.SMEM),
                  pl.BlockSpec((tb, D), lambda i: (i, 0)),
                  pl.BlockSpec((tb, ROUTER_COLS), lambda i: (i, 0)),
                  pl.BlockSpec(memory_space=pl.ANY)],
        out_specs=pl.BlockSpec(memory_space=pl.ANY),
        out_shape=jax.ShapeDtypeStruct((P, W), F32),
        scratch_shapes=[pltpu.VMEM((2, tb, W), F32), pltpu.SemaphoreType.DMA((2,))],
        input_output_aliases={3: 0},
        compiler_params=_cparams("arbitrary"),
        name="moe_dispatch",
    )(dest.reshape(T // tb, 1, tb), x_tokens, info, jnp.zeros((P, W), F32))


def _expert_pair_kernel(ea_ref, eb_ref, nused_ref, xs_ref, wga_ref, wua_ref, wda_ref, wgb_ref, wub_ref, wdb_ref,
                        o_ref):
    del ea_ref, eb_ref
    i = pl.program_id(0)
    D = o_ref.shape[1]

    @pl.when(i < nused_ref[0])
    def _():
        xb = xs_ref[:, :D].astype(BF16)

        def ffn(wg_ref, wu_ref, wd_ref):
            gate = jnp.dot(xb, wg_ref[0], preferred_element_type=F32)
            up = jnp.dot(xb, wu_ref[0], preferred_element_type=F32)
            hid = (gate * jax.nn.sigmoid(gate) * up).astype(BF16)
            return jnp.dot(hid, wd_ref[0], preferred_element_type=F32)

        w_a = xs_ref[:, D + INFO_WA:D + INFO_WA + 1]
        w_b = xs_ref[:, D + INFO_WB:D + INFO_WB + 1]
        o_ref[...] = w_a * ffn(wga_ref, wua_ref, wda_ref) + w_b * ffn(wgb_ref, wub_ref, wdb_ref)

    @pl.when(i >= nused_ref[0])
    def _():
        o_ref[...] = jnp.zeros_like(o_ref)


def expert_pairs(xs, blk_a, blk_b, n_used, w_gate, w_up, w_down):
    P, W = xs.shape
    D = W - ROUTER_COLS
    n_blocks = P // MOE_BLOCK
    DE = w_gate.shape[-1]
    wspec = lambda shape, which: pl.BlockSpec(shape, lambda i, ea, eb, nu: ((ea, eb)[which][i], 0, 0))
    grid_spec = pltpu.PrefetchScalarGridSpec(
        num_scalar_prefetch=3,
        grid=(n_blocks,),
        in_specs=[pl.BlockSpec((MOE_BLOCK, W), lambda i, ea, eb, nu: (i, 0)),
                  wspec((1, D, DE), 0), wspec((1, D, DE), 0), wspec((1, DE, D), 0),
                  wspec((1, D, DE), 1), wspec((1, D, DE), 1), wspec((1, DE, D), 1)],
        out_specs=pl.BlockSpec((MOE_BLOCK, D), lambda i, ea, eb, nu: (i, 0)),
    )
    return pl.pallas_call(
        _expert_pair_kernel,
        grid_spec=grid_spec,
        out_shape=jax.ShapeDtypeStruct((P, D), F32),
        compiler_params=_cparams("arbitrary"),
        name="moe_experts",
    )(blk_a, blk_b, n_used, xs, w_gate, w_up, w_down, w_gate, w_up, w_down)


def _gather_rows(idx_ref, src_hbm, buf, sem, n_rows):
    def body(r, carry):
        pltpu.make_async_copy(src_hbm.at[idx_ref[0, 0, r]], buf.at[r], sem).start()
        return carry
    lax.fori_loop(0, n_rows, body, 0)


def _collect_kernel(dest_ref, dest_next_ref, o_hbm, h_ref, g2_ref, fg_ref, out_ref, obuf, sems, *, final_norm):
    i = pl.program_id(0)
    n = pl.num_programs(0)
    slot = i % 2
    tb = h_ref.shape[0]

    @pl.when(i == 0)
    def _():
        _gather_rows(dest_ref, o_hbm, obuf.at[0], sems.at[0], tb)

    @pl.when(i + 1 < n)
    def _():
        _gather_rows(dest_next_ref, o_hbm, obuf.at[1 - slot], sems.at[1 - slot], tb)

    _wait_rows(obuf.at[slot], sems.at[slot])
    h = h_ref[...] + g2_ref[0] * obuf[slot]
    if final_norm:
        ms = jnp.mean(h * h, axis=-1, keepdims=True)
        h = h * lax.rsqrt(ms + EPS) * fg_ref[...]
    out_ref[...] = h


def moe_collect(o_sorted, dest, h_tokens, g2, tokens_per_batch, final_g, final_norm):
    T, D = h_tokens.shape
    tb = DISPATCH_TOKENS
    nt = T // tb
    per_b = tokens_per_batch // tb
    dest3 = dest.reshape(nt, 1, tb)
    return pl.pallas_call(
        functools.partial(_collect_kernel, final_norm=final_norm),
        grid=(nt,),
        in_specs=[pl.BlockSpec((1, 1, tb), lambda i: (i, 0, 0), memory_space=pltpu.SMEM),
                  pl.BlockSpec((1, 1, tb), lambda i: (jnp.minimum(i + 1, nt - 1), 0, 0), memory_space=pltpu.SMEM),
                  pl.BlockSpec(memory_space=pl.ANY),
                  pl.BlockSpec((tb, D), lambda i: (i, 0)),
                  pl.BlockSpec((1, 1, D), lambda i: (i // per_b, 0, 0)),
                  pl.BlockSpec((1, D), lambda i: (0, 0))],
        out_specs=pl.BlockSpec((tb, D), lambda i: (i, 0)),
        out_shape=jax.ShapeDtypeStruct((T, D), F32),
        scratch_shapes=[pltpu.VMEM((2, tb, D), F32), pltpu.SemaphoreType.DMA((2,))],
        compiler_params=_cparams("arbitrary"),
        name="moe_collect",
    )(dest3, dest3, o_sorted, h_tokens, g2, final_g.reshape(1, D))


CONV_MARGIN = 16


def _time_chunk(L):
    return min(L, 256)


LANES = 128


def _zero_margins(pad_ref, L):
    zeros = jnp.zeros((CONV_MARGIN, LANES), F32)
    for s in range(pad_ref.shape[0]):
        pad_ref[s, pl.ds(0, CONV_MARGIN), :] = zeros
        pad_ref[s, pl.ds(CONV_MARGIN + L, CONV_MARGIN), :] = zeros


def _dw_conv_slab(pad_ref, s, base, T, w_ref, b_ref, col, taps, pad_left):
    acc = jnp.broadcast_to(b_ref[:, col:col + LANES], (T, LANES))
    for k in range(taps):
        acc = acc + w_ref[k:k + 1, col:col + LANES] * pad_ref[s, pl.ds(base + (CONV_MARGIN - pad_left + k), T), :]
    return acc


def _conformer_kernel(u_ref, w_ref, b_ref, g_ref, beta_ref, o_ref, ypad):
    L = o_ref.shape[1]
    T = _time_chunk(L)
    C = D_GROUP
    n_slabs = C // LANES
    pad = (CONF_KERNEL - 1) // 2
    _zero_margins(ypad, L)

    def glu(j, carry):
        base = pl.multiple_of(j * T, T)
        for s in range(n_slabs):
            a = u_ref[0, pl.ds(base, T), s * LANES:(s + 1) * LANES]
            gate = u_ref[0, pl.ds(base, T), C + s * LANES:C + (s + 1) * LANES]
            ypad[s, pl.ds(CONV_MARGIN + base, T), :] = a * jax.nn.sigmoid(gate)
        return carry
    lax.fori_loop(0, L // T, glu, 0)

    def conv(j, carry):
        base = pl.multiple_of(j * T, T)
        acc = jnp.concatenate([_dw_conv_slab(ypad, s, base, T, w_ref, b_ref, s * LANES, CONF_KERNEL, pad)
                               for s in range(n_slabs)], axis=-1)
        mu = jnp.mean(acc, axis=-1, keepdims=True)
        cen = acc - mu
        var = jnp.mean(cen * cen, axis=-1, keepdims=True)
        y = cen * lax.rsqrt(var + EPS) * g_ref[...] + beta_ref[...]
        o_ref[0, pl.ds(base, T), :] = y * jax.nn.sigmoid(y)
        return carry
    lax.fori_loop(0, L // T, conv, 0)


def conformer_conv(u, w, b, ln_g, ln_b):
    B, L, _ = u.shape
    C = D_GROUP
    vec = pl.BlockSpec((1, C), lambda i: (0, 0))
    return pl.pallas_call(
        _conformer_kernel,
        grid=(B,),
        in_specs=[pl.BlockSpec((1, L, 2 * C), lambda i: (i, 0, 0)),
                  pl.BlockSpec((CONF_KERNEL, C), lambda i: (0, 0)), vec, vec, vec],
        out_specs=pl.BlockSpec((1, L, C), lambda i: (i, 0, 0)),
        out_shape=jax.ShapeDtypeStruct((B, L, C), F32),
        scratch_shapes=[pltpu.VMEM((C // LANES, L + 2 * CONV_MARGIN, LANES), F32)],
        compiler_params=_cparams("parallel"),
        name="conformer_conv",
    )(u, w, b.reshape(1, C), ln_g.reshape(1, C), ln_b.reshape(1, C))


def _gelu_tanh(x):
    return 0.5 * x * (1.0 + jnp.tanh(math.sqrt(2.0 / math.pi) * (x + 0.044715 * (x * x * x))))


def _lru_kernel(uc_ref, ux_ref, cw_ref, cb_ref, wcat_ref, bcat_ref, lam_ref, *rest, need_ctx):
    if need_ctx:
        oc_ref, ox_ref, cpad, xpad, a_s, b_s, yx, yc = rest
    else:
        ox_ref, cpad, xpad, a_s, b_s, yx = rest
        oc_ref = yc = None
    C = D_GROUP
    n_slabs = C // LANES
    Lc = uc_ref.shape[1]
    Lx = ux_ref.shape[1]
    pad_l = (LRU_CONV - 1) // 2

    def fill(pad_ref, u_ref, L):
        T = _time_chunk(L)
        _zero_margins(pad_ref, L)

        def body(j, carry):
            base = pl.multiple_of(j * T, T)
            for s in range(n_slabs):
                pad_ref[s, pl.ds(CONV_MARGIN + base, T), :] = u_ref[0, pl.ds(base, T),
                                                                    C + s * LANES:C + (s + 1) * LANES]
            return carry
        lax.fori_loop(0, L // T, body, 0)

    fill(cpad, uc_ref, Lc)
    fill(xpad, ux_ref, Lx)

    def coeffs(pad_ref, base, T, d):
        x = jnp.concatenate([_dw_conv_slab(pad_ref, s, base, T, cw_ref, cb_ref, s * LANES, LRU_CONV, pad_l)
                             for s in range(n_slabs)], axis=-1)
        g = jnp.dot(x.astype(BF16), wcat_ref[:, 2 * d * C:2 * (d + 1) * C],
                    preferred_element_type=F32) + bcat_ref[:, 2 * d * C:2 * (d + 1) * C]
        r = jax.nn.sigmoid(g[:, :C])
        i = jax.nn.sigmoid(g[:, C:])
        z = -lam_ref[d:d + 1, :]
        softplus = jnp.maximum(z, 0.0) + jnp.log(1.0 + jnp.exp(-jnp.abs(z)))
        a = jnp.exp(-LRU_C * r * softplus)
        b = jnp.sqrt(1.0 - a * a) * (i * x)
        for s in range(n_slabs):
            a_s[d * n_slabs + s, pl.ds(0, T), :] = a[:, s * LANES:(s + 1) * LANES]
            b_s[d * n_slabs + s, pl.ds(0, T), :] = b[:, s * LANES:(s + 1) * LANES]

    def run(pad_ref, L, h, y_ref):
        T = _time_chunk(L)
        n = L // T

        def chunk(j, h):
            base_f = pl.multiple_of(j * T, T)
            base_b = pl.multiple_of((n - 1 - j) * T, T)
            coeffs(pad_ref, base_f, T, 0)
            coeffs(pad_ref, base_b, T, 1)

            def step(t, h):
                new = []
                for d, (base, row) in enumerate(((base_f, t), (base_b, T - 1 - t))):
                    for s in range(n_slabs):
                        k = d * n_slabs + s
                        hs = a_s[k, pl.ds(row, 1), :] * h[k] + b_s[k, pl.ds(row, 1), :]
                        if y_ref is not None:
                            y_ref[k, pl.ds(base + row, 1), :] = hs
                        new.append(hs)
                return tuple(new)
            return lax.fori_loop(0, T, step, h, unroll=8)
        return lax.fori_loop(0, n, chunk, h)

    h = tuple(jnp.zeros((1, LANES), F32) for _ in range(2 * n_slabs))
    h = run(cpad, Lc, h, yc)
    run(xpad, Lx, h, yx)

    def finish(u_ref, y_ref, o_ref, L):
        T = _time_chunk(L)

        def body(j, carry):
            base = pl.multiple_of(j * T, T)
            y = jnp.concatenate([y_ref[s, pl.ds(base, T), :] + y_ref[n_slabs + s, pl.ds(base, T), :]
                                 for s in range(n_slabs)], axis=-1)
            o_ref[0, pl.ds(base, T), :] = _gelu_tanh(u_ref[0, pl.ds(base, T), :C]) * y
            return carry
        lax.fori_loop(0, L // T, body, 0)

    finish(ux_ref, yx, ox_ref, Lx)
    if need_ctx:
        finish(uc_ref, yc, oc_ref, Lc)


def _block_diag(w):
    H, n, _ = w.shape
    eye = jnp.eye(H, dtype=w.dtype)
    return (eye[:, None, :, None] * w[:, :, None, :]).reshape(H * n, H * n)


def rglru_mixer(uc, ux, lp, need_ctx):
    B, Lc, _ = uc.shape
    Lx = ux.shape[1]
    C = D_GROUP
    wcat = jnp.concatenate([_block_diag(lp["lru_wa"][0]), _block_diag(lp["lru_wx"][0]),
                            _block_diag(lp["lru_wa"][1]), _block_diag(lp["lru_wx"][1])], axis=1).astype(BF16)
    bcat = jnp.concatenate([lp["lru_ba"][0], lp["lru_bx"][0], lp["lru_ba"][1], lp["lru_bx"][1]]).reshape(1, 4 * C)
    full = lambda r, c: pl.BlockSpec((r, c), lambda i: (0, 0))
    seq = lambda L, n: pl.BlockSpec((1, L, n), lambda i: (i, 0, 0))
    out_specs = [seq(Lx, C)]
    out_shape = [jax.ShapeDtypeStruct((B, Lx, C), F32)]
    if need_ctx:
        out_specs = [seq(Lc, C)] + out_specs
        out_shape = [jax.ShapeDtypeStruct((B, Lc, C), F32)] + out_shape
    T = _time_chunk(Lx)
    slab = lambda rows, n=1: pltpu.VMEM((n * C // LANES, rows, LANES), F32)
    scratch = [slab(Lc + 2 * CONV_MARGIN), slab(Lx + 2 * CONV_MARGIN), slab(T, 2), slab(T, 2), slab(Lx, 2)]
    if need_ctx:
        scratch.append(slab(Lc, 2))
    res = pl.pallas_call(
        functools.partial(_lru_kernel, need_ctx=need_ctx),
        grid=(B,),
        in_specs=[seq(Lc, 2 * C), seq(Lx, 2 * C), full(LRU_CONV, C), full(1, C), full(C, 4 * C),
                  full(1, 4 * C), full(2, C)],
        out_specs=out_specs,
        out_shape=out_shape,
        scratch_shapes=scratch,
        compiler_params=_cparams("parallel"),
        name="rglru",
    )(uc, ux, lp["lru_conv_w"], lp["lru_conv_b"].reshape(1, C), wcat, bcat, lp["lru_lambda"])
    if need_ctx:
        return res[0], res[1]
    return None, res[0]


HY_SHORT = 3


def _short_conv(pad_ref, base, T, w_ref, b_ref, c0, c1):
    return jnp.concatenate([_dw_conv_slab(pad_ref, col // LANES, base, T, w_ref, b_ref, col, HY_SHORT, 1)
                            for col in range(c0, c1, LANES)], axis=-1)


def _fill_padded(pad_ref, u_ref, L, T):
    _zero_margins(pad_ref, L)

    def body(j, carry):
        base = pl.multiple_of(j * T, T)
        for s in range(pad_ref.shape[0]):
            pad_ref[s, pl.ds(CONV_MARGIN + base, T), :] = u_ref[0, pl.ds(base, T), s * LANES:(s + 1) * LANES]
        return carry
    lax.fori_loop(0, L // T, body, 0)


def _hyena_pre_kernel(u_ref, w_ref, b_ref, z_ref, upad):
    L = u_ref.shape[1]
    T = _time_chunk(L)
    C = D_GROUP
    _fill_padded(upad, u_ref, L, T)

    def body(j, carry):
        base = pl.multiple_of(j * T, T)
        x1 = _short_conv(upad, base, T, w_ref, b_ref, C, 2 * C)
        v = _short_conv(upad, base, T, w_ref, b_ref, 2 * C, 3 * C)
        z_ref[pl.ds(base, T), :] = (x1 * v).astype(BF16)
        return carry
    lax.fori_loop(0, L // T, body, 0)


def _hyena_post_kernel(u_ref, y_ref, w_ref, b_ref, bias_ref, o_ref, upad):
    L = u_ref.shape[1]
    T = _time_chunk(L)
    C = D_GROUP
    _fill_padded(upad, u_ref, L, T)

    def body(j, carry):
        base = pl.multiple_of(j * T, T)
        x0 = _short_conv(upad, base, T, w_ref, b_ref, 0, C)
        x1 = _short_conv(upad, base, T, w_ref, b_ref, C, 2 * C)
        v = _short_conv(upad, base, T, w_ref, b_ref, 2 * C, 3 * C)
        o_ref[0, pl.ds(base, T), :] = x0 * (y_ref[pl.ds(base, T), :] + (x1 * v) * bias_ref[...])
        return carry
    lax.fori_loop(0, L // T, body, 0)


def _spectrum_kernel(f_ref, z_ref, ha_ref, hb_ref, hc_ref, y_ref):
    tf = ha_ref.shape[0]
    acc = jnp.dot(f_ref[...], z_ref[...], preferred_element_type=F32)
    zr = acc[:tf]
    zi = acc[tf:]
    y_ref[:tf, :] = (zr * ha_ref[...] - zi * hb_ref[...]).astype(BF16)
    y_ref[tf:, :] = (zr * hb_ref[...] + zi * hc_ref[...]).astype(BF16)


def _idft_kernel(f_ref, y_ref, o_ref):
    o_ref[...] = jnp.dot(f_ref[...], y_ref[...], preferred_element_type=F32)


def dft_tables(L):
    N = 2 * L
    tf = min(256, L)
    k = jnp.arange(L, dtype=jnp.int32)
    n = jnp.arange(L, dtype=jnp.int32)
    ang = (2.0 * math.pi / N) * ((k[:, None] * n[None, :]) % N).astype(F32)
    cos = jnp.cos(ang)
    sin = jnp.sin(ang)
    nyq = jnp.where(n % 2 == 0, 1.0, -1.0).astype(F32)
    f_re = cos
    f_im = (-sin).at[0].set(nyq)
    fwd = jnp.stack([f_re.reshape(L // tf, tf, L), f_im.reshape(L // tf, tf, L)], axis=1).reshape(N, L)
    ck = jnp.where(k == 0, 1.0, 2.0).astype(F32)[:, None] / N
    i_re = cos * ck
    i_im = (-sin * ck).at[0].set(nyq / N)
    inv = jnp.stack([i_re.reshape(L // tf, tf, L), i_im.reshape(L // tf, tf, L)], axis=1).reshape(N, L).T
    return fwd.astype(BF16), inv.astype(BF16)


def filter_spectrum(h_fwd, h_bwd):
    L, C = h_fwd.shape
    k = jnp.concatenate([h_fwd, jnp.zeros((1, C), F32), h_bwd[1:][::-1]], axis=0)
    hf = jnp.fft.rfft(k, axis=0)
    hr = jnp.real(hf)
    hi = jnp.imag(hf)
    a = hr[:L]
    b = hi[:L].at[0].set(0.0)
    c = hr[:L].at[0].set(hr[L])
    return a, b, c


def hyena_mixer(u, lp, tables):
    B, L, _ = u.shape
    C = D_GROUP
    N = 2 * L
    fwd, inv = tables
    tf = min(256, L)
    T = _time_chunk(L)
    w, bsh = lp["hy_short_w"], lp["hy_short_b"].reshape(1, 3 * C)
    z2 = pl.pallas_call(
        _hyena_pre_kernel,
        grid=(B,),
        in_specs=[pl.BlockSpec((1, L, 3 * C), lambda b: (b, 0, 0)),
                  pl.BlockSpec((HY_SHORT, 3 * C), lambda b: (0, 0)),
                  pl.BlockSpec((1, 3 * C), lambda b: (0, 0))],
        out_specs=pl.BlockSpec((L, C), lambda b: (0, b)),
        out_shape=jax.ShapeDtypeStruct((L, B * C), BF16),
        scratch_shapes=[pltpu.VMEM((3 * C // LANES, L + 2 * CONV_MARGIN, LANES), F32)],
        compiler_params=_cparams("parallel"),
        name="hyena_pre",
    )(u, w, bsh)

    h_fwd, h_bwd = _hyena_filters(L, lp)
    tn = 2 * C
    ha, hb, hc = [jnp.tile(t, (1, tn // C)) for t in filter_spectrum(h_fwd, h_bwd)]
    hspec = pl.BlockSpec((tf, tn), lambda i, j: (i, 0))
    y2 = pl.pallas_call(
        _spectrum_kernel,
        grid=(L // tf, B * C // tn),
        in_specs=[pl.BlockSpec((2 * tf, L), lambda i, j: (i, 0)),
                  pl.BlockSpec((L, tn), lambda i, j: (0, j)), hspec, hspec, hspec],
        out_specs=pl.BlockSpec((2 * tf, tn), lambda i, j: (i, j)),
        out_shape=jax.ShapeDtypeStruct((N, B * C), BF16),
        compiler_params=_cparams("parallel", "parallel"),
        name="hyena_spectrum",
    )(fwd, z2, ha, hb, hc)

    tl = min(256, L)
    yt = pl.pallas_call(
        _idft_kernel,
        grid=(L // tl, B * C // tn),
        in_specs=[pl.BlockSpec((tl, N), lambda i, j: (i, 0)),
                  pl.BlockSpec((N, tn), lambda i, j: (0, j))],
        out_specs=pl.BlockSpec((tl, tn), lambda i, j: (i, j)),
        out_shape=jax.ShapeDtypeStruct((L, B * C), F32),
        compiler_params=_cparams("parallel", "parallel"),
        name="hyena_idft",
    )(inv, y2)

    return pl.pallas_call(
        _hyena_post_kernel,
        grid=(B,),
        in_specs=[pl.BlockSpec((1, L, 3 * C), lambda b: (b, 0, 0)),
                  pl.BlockSpec((L, C), lambda b: (0, b)),
                  pl.BlockSpec((HY_SHORT, 3 * C), lambda b: (0, 0)),
                  pl.BlockSpec((1, 3 * C), lambda b: (0, 0)),
                  pl.BlockSpec((1, C), lambda b: (0, 0))],
        out_specs=pl.BlockSpec((1, L, C), lambda b: (b, 0, 0)),
        out_shape=jax.ShapeDtypeStruct((B, L, C), F32),
        scratch_shapes=[pltpu.VMEM((3 * C // LANES, L + 2 * CONV_MARGIN, LANES), F32)],
        compiler_params=_cparams("parallel"),
        name="hyena_post",
    )(u, yt, w, bsh, lp["hy_bias"].reshape(1, C))


FFT_N2 = 128
FFT_UNROLL = 8


class _FftPlan:
    def __init__(self, L):
        self.L = L
        self.N = 2 * L
        self.N1 = self.N // FFT_N2
        self.KH = self.N1 // 2 + 1
        self.KP = -(-self.KH // 8) * 8
        self.PA = 2 * self.KP + 4


def fft_tables(L):
    p = _FftPlan(L)
    N, N1, KH, KP = p.N, p.N1, p.KH, p.KP
    n2 = jnp.arange(FFT_N2, dtype=jnp.int32)
    k1 = jnp.arange(KP, dtype=jnp.int32)
    n1 = jnp.arange(N1, dtype=jnp.int32)
    n = FFT_N2 * n1[None, None, :] + n2[:, None, None]
    ang = (2.0 * math.pi / N) * ((k1[None, :, None] * n) % N).astype(F32)
    keep = (k1 < KH)[None, :, None]
    g_re = jnp.where(keep, jnp.cos(ang), 0.0)
    g_im = jnp.where(keep, -jnp.sin(ang), 0.0)
    ga_full = jnp.concatenate([g_re, g_im], axis=1)
    ck = jnp.where((k1 == 0) | (k1 == N1 // 2), 1.0, 2.0) / N
    ga_inv = jnp.swapaxes(ga_full[:, :, :N1 // 2] * jnp.tile(ck, 2)[None, :, None], 1, 2)
    kk = jnp.arange(FFT_N2, dtype=jnp.int32)
    ang2 = (2.0 * math.pi / FFT_N2) * ((kk[:, None] * kk[None, :]) % FFT_N2).astype(F32)
    fr, fi = jnp.cos(ang2), -jnp.sin(ang2)
    fb = jnp.block([[fr, -fi], [fi, fr]])
    fb_inv = jnp.block([[fr, fi], [-fi, fr]])
    return dict(ga_half=ga_full[:, :, :N1 // 2].astype(BF16), ga_full=ga_full.astype(BF16),
                ga_inv=ga_inv.astype(BF16), fb=fb.astype(BF16), fb_inv=fb_inv.astype(BF16))


def _fft_stage_a(x_ref, ga_ref, s_ref, plan, n1_count):
    n_slabs = x_ref.shape[0]

    def body(n2, carry):
        xs = jnp.concatenate([x_ref[s, pl.ds(n2, n1_count, stride=FFT_N2), :] for s in range(n_slabs)], axis=-1)
        a = jnp.dot(ga_ref[n2], xs.astype(BF16), preferred_element_type=F32)
        for s in range(n_slabs):
            s_ref[s, pl.ds(n2 * plan.PA, 2 * plan.KP), :] = a[:, s * LANES:(s + 1) * LANES]
        return carry
    lax.fori_loop(0, FFT_N2, body, 0, unroll=FFT_UNROLL)


def _fft_load_k1(s_ref, k1, plan):
    n_slabs = s_ref.shape[0]
    re = jnp.concatenate([s_ref[s, pl.ds(k1, FFT_N2, stride=plan.PA), :] for s in range(n_slabs)], axis=-1)
    im = jnp.concatenate([s_ref[s, pl.ds(plan.KP + k1, FFT_N2, stride=plan.PA), :] for s in range(n_slabs)], axis=-1)
    return jnp.concatenate([re, im], axis=0).astype(BF16)


def _fft_filter_kernel(k_ref, ga_ref, fb_ref, h_ref, s_ref, *, plan):
    _fft_stage_a(k_ref, ga_ref, s_ref, plan, plan.N1)

    def body(k1, carry):
        h_ref[k1] = jnp.dot(fb_ref[...], _fft_load_k1(s_ref, k1, plan), preferred_element_type=F32).astype(BF16)
        return carry
    lax.fori_loop(0, plan.KH, body, 0)


def _fft_conv_kernel(z_ref, ga_ref, gi_ref, fb_ref, fbi_ref, h_ref, y_ref, s_ref, *, plan):
    zs = z_ref.at[0]
    ys = y_ref.at[0]
    n_slabs = zs.shape[0]
    half = FFT_N2
    _fft_stage_a(zs, ga_ref, s_ref, plan, plan.N1 // 2)

    def body_b(k1, carry):
        x = jnp.dot(fb_ref[...], _fft_load_k1(s_ref, k1, plan), preferred_element_type=F32)
        h = h_ref[k1].astype(F32)
        xr, xi, hr, hi = x[:half], x[half:], h[:half], h[half:]
        y = jnp.concatenate([xr * hr - xi * hi, xr * hi + xi * hr], axis=0).astype(BF16)
        b = jnp.dot(fbi_ref[...], y, preferred_element_type=F32)
        for s in range(n_slabs):
            s_ref[s, pl.ds(k1, FFT_N2, stride=plan.PA), :] = b[:half, s * LANES:(s + 1) * LANES]
            s_ref[s, pl.ds(plan.KP + k1, FFT_N2, stride=plan.PA), :] = b[half:, s * LANES:(s + 1) * LANES]
        return carry
    lax.fori_loop(0, plan.KH, body_b, 0, unroll=3)

    def body_a(n2, carry):
        b = jnp.concatenate([s_ref[s, pl.ds(n2 * plan.PA, 2 * plan.KP), :] for s in range(n_slabs)], axis=-1)
        y = jnp.dot(gi_ref[n2], b.astype(BF16), preferred_element_type=F32)
        for s in range(n_slabs):
            ys[s, pl.ds(n2, plan.N1 // 2, stride=FFT_N2), :] = y[:, s * LANES:(s + 1) * LANES]
        return carry
    lax.fori_loop(0, FFT_N2, body_a, 0, unroll=FFT_UNROLL)


def fft_filter_spectrum(h_fwd, h_bwd, tabs):
    L, C = h_fwd.shape
    plan = _FftPlan(L)
    n_slabs = C // LANES
    k = jnp.concatenate([h_fwd, jnp.zeros((1, C), F32), h_bwd[1:][::-1]], axis=0)
    k = k.reshape(plan.N, n_slabs, LANES).transpose(1, 0, 2)
    full = lambda shape: pl.BlockSpec(shape, lambda i: (0,) * len(shape))
    return pl.pallas_call(
        functools.partial(_fft_filter_kernel, plan=plan),
        grid=(1,),
        in_specs=[full((n_slabs, plan.N, LANES)), full((FFT_N2, 2 * plan.KP, plan.N1)),
                  full((2 * FFT_N2, 2 * FFT_N2))],
        out_specs=full((plan.KH, 2 * FFT_N2, C)),
        out_shape=jax.ShapeDtypeStruct((plan.KH, 2 * FFT_N2, C), BF16),
        scratch_shapes=[pltpu.VMEM((n_slabs, FFT_N2 * plan.PA, LANES), F32)],
        compiler_params=_cparams("arbitrary"),
        name="hyena_filter_fft",
    )(k, tabs["ga_full"], tabs["fb"])


def fft_long_conv(z, h_spec, tabs):
    B, n_slabs, L, _ = z.shape
    plan = _FftPlan(L)
    C = n_slabs * LANES
    full = lambda shape: pl.BlockSpec(shape, lambda b: (0,) * len(shape))
    seq = pl.BlockSpec((1, n_slabs, L, LANES), lambda b: (b, 0, 0, 0))
    return pl.pallas_call(
        functools.partial(_fft_conv_kernel, plan=plan),
        grid=(B,),
        in_specs=[seq, full((FFT_N2, 2 * plan.KP, plan.N1 // 2)), full((FFT_N2, plan.N1 // 2, 2 * plan.KP)),
                  full((2 * FFT_N2, 2 * FFT_N2)), full((2 * FFT_N2, 2 * FFT_N2)),
                  full((plan.KH, 2 * FFT_N2, C))],
        out_specs=seq,
        out_shape=jax.ShapeDtypeStruct((B, n_slabs, L, LANES), F32),
        scratch_shapes=[pltpu.VMEM((n_slabs, FFT_N2 * plan.PA, LANES), F32)],
        compiler_params=_cparams("parallel"),
        name="hyena_fft_conv",
    )(z, tabs["ga_half"], tabs["ga_inv"], tabs["fb"], tabs["fb_inv"], h_spec)


def _hyena_pre_slab_kernel(u_ref, w_ref, b_ref, z_ref, upad):
    L = u_ref.shape[1]
    T = _time_chunk(L)
    C = D_GROUP
    _fill_padded(upad, u_ref, L, T)

    def body(j, carry):
        base = pl.multiple_of(j * T, T)
        for s in range(C // LANES):
            x1 = _dw_conv_slab(upad, C // LANES + s, base, T, w_ref, b_ref, C + s * LANES, HY_SHORT, 1)
            v = _dw_conv_slab(upad, 2 * C // LANES + s, base, T, w_ref, b_ref, 2 * C + s * LANES, HY_SHORT, 1)
            z_ref[0, s, pl.ds(base, T), :] = x1 * v
        return carry
    lax.fori_loop(0, L // T, body, 0)


def _hyena_post_slab_kernel(u_ref, y_ref, w_ref, b_ref, bias_ref, o_ref, upad):
    L = u_ref.shape[1]
    T = _time_chunk(L)
    C = D_GROUP
    _fill_padded(upad, u_ref, L, T)

    def body(j, carry):
        base = pl.multiple_of(j * T, T)
        x0 = _short_conv(upad, base, T, w_ref, b_ref, 0, C)
        x1 = _short_conv(upad, base, T, w_ref, b_ref, C, 2 * C)
        v = _short_conv(upad, base, T, w_ref, b_ref, 2 * C, 3 * C)
        y = jnp.concatenate([y_ref[0, s, pl.ds(base, T), :] for s in range(C // LANES)], axis=-1)
        o_ref[0, pl.ds(base, T), :] = x0 * (y + (x1 * v) * bias_ref[...])
        return carry
    lax.fori_loop(0, L // T, body, 0)


def hyena_mixer_fft(u, lp, tabs):
    B, L, _ = u.shape
    C = D_GROUP
    n_slabs = C // LANES
    w, bsh = lp["hy_short_w"], lp["hy_short_b"].reshape(1, 3 * C)
    useq = pl.BlockSpec((1, L, 3 * C), lambda b: (b, 0, 0))
    slabs = pl.BlockSpec((1, n_slabs, L, LANES), lambda b: (b, 0, 0, 0))
    wspec = pl.BlockSpec((HY_SHORT, 3 * C), lambda b: (0, 0))
    bspec = pl.BlockSpec((1, 3 * C), lambda b: (0, 0))
    pad_scratch = pltpu.VMEM((3 * C // LANES, L + 2 * CONV_MARGIN, LANES), F32)
    z = pl.pallas_call(
        _hyena_pre_slab_kernel,
        grid=(B,),
        in_specs=[useq, wspec, bspec],
        out_specs=slabs,
        out_shape=jax.ShapeDtypeStruct((B, n_slabs, L, LANES), F32),
        scratch_shapes=[pad_scratch],
        compiler_params=_cparams("parallel"),
        name="hyena_pre",
    )(u, w, bsh)
    h_fwd, h_bwd = _hyena_filters(L, lp)
    y = fft_long_conv(z, fft_filter_spectrum(h_fwd, h_bwd, tabs), tabs)
    return pl.pallas_call(
        _hyena_post_slab_kernel,
        grid=(B,),
        in_specs=[useq, slabs, wspec, bspec, pl.BlockSpec((1, C), lambda b: (0, 0))],
        out_specs=pl.BlockSpec((1, L, C), lambda b: (b, 0, 0)),
        out_shape=jax.ShapeDtypeStruct((B, L, C), F32),
        scratch_shapes=[pad_scratch],
        compiler_params=_cparams("parallel"),
        name="hyena_post",
    )(u, y, w, bsh, lp["hy_bias"].reshape(1, C))


def _hyena_filters(L, lp):
    t = jnp.linspace(0.0, 1.0, L, dtype=F32)[:, None]
    bands = (HY_EMB - 1) // 2
    w = 2.0 * math.pi * jnp.arange(L, dtype=F32)[:, None] / L
    f = jnp.linspace(1e-4, bands - 1, bands, dtype=F32)[None]
    z = jnp.concatenate([t, jnp.cos(f * w), -jnp.sin(f * w)], axis=-1)
    hdn = jnp.sin(z @ lp["hy_ffn_w1"] + lp["hy_ffn_b1"])
    hdn = jnp.sin(hdn @ lp["hy_ffn_w2"] + lp["hy_ffn_b2"])
    h = (hdn @ lp["hy_ffn_w3"]).reshape(L, 2, D_GROUP)
    max_decay = math.log(HY_TARGET) / HY_FAST_DECAY
    min_decay = math.log(HY_TARGET) / HY_SLOW_DECAY
    deltas = jnp.linspace(min_decay, max_decay, D_GROUP, dtype=F32)
    h = h * jnp.exp(-t * jnp.abs(deltas))[:, None, :]
    h = h / (jnp.sum(jnp.abs(h), axis=(0, 1), keepdims=True) + EPS)
    return h[:, 0], h[:, 1]


def _layer(hc, hx, c_silu_all, lp, need_ctx, final_g, final_norm, tables_x, tables_c):
    B, S, D = hx.shape
    C = hc.shape[1]
    mod = small_linear(c_silu_all, lp["ada_w"], lp["ada_b"])
    mod_x = mod[:B].reshape(B, 6, 1, D)
    mod_c = jnp.broadcast_to(mod[B].reshape(1, 6, 1, D), (B, 6, 1, D))
    w_ext = extend_w_in(lp["w_in"])
    cos_x, sin_x = rope_tables(S, True)
    cos_c, sin_c = rope_tables(C, False)
    hy_x, cf_x, at_x, lr_x = in_proj(hx, mod_x[:, 0], mod_x[:, 1], lp["norm1_g"], w_ext, cos_x, sin_x, tm=256)
    hy_c, cf_c, at_c, lr_c = in_proj(hc, mod_c[:, 0], mod_c[:, 1], lp["norm1_g"], w_ext, cos_c, sin_c, tm=256)

    yd_c, yd_x = rglru_mixer(lr_c, lr_x, lp, need_ctx)
    conf = lambda u: conformer_conv(u, lp["conf_dw_w"], lp["conf_dw_b"], lp["conf_ln_g"], lp["conf_ln_b"])
    ys_x = [hyena_mixer_fft(hy_x, lp, tables_x), conf(cf_x),
            window_attention(at_x, at_c, lp["attn_sink"]), yd_x]

    w_out = lp["w_out"].astype(BF16)
    w_router = jnp.zeros((D, ROUTER_COLS), F32)
    w_router = w_router.at[:, :N_GROUPS].set(lp["router_g_w"]).at[:, N_GROUPS:N_GROUPS + N_EXPERTS].set(lp["router_e_w"])
    w_router = w_router.astype(BF16)
    b_router = jnp.zeros((1, ROUTER_COLS), F32)
    b_router = b_router.at[0, :N_GROUPS].set(lp["router_g_b"]).at[0, N_GROUPS:N_GROUPS + N_EXPERTS].set(lp["router_e_b"])

    hx1, nx, lg_x = out_proj(ys_x, hx, mod_x[:, 2], lp["group_norm_g"], w_out, lp["norm2_g"],
                             mod_x[:, 3], mod_x[:, 4], w_router, b_router, tm=256)
    h_tok = hx1.reshape(B * S, D)
    n_tok = nx.reshape(B * S, D)
    lg = lg_x.reshape(B * S, ROUTER_COLS)
    if need_ctx:
        ys_c = [hyena_mixer(hy_c, lp, tables_c), conf(cf_c),
                context_attention(at_c, lp["attn_sink"]), yd_c]
        hc1, nc, lg_c = out_proj(ys_c, hc, mod_c[:, 2], lp["group_norm_g"], w_out, lp["norm2_g"],
                                 mod_c[:, 3], mod_c[:, 4], w_router, b_router, tm=256)
        n_tok = jnp.concatenate([n_tok, nc.reshape(B * C, D)], axis=0)
        lg = jnp.concatenate([lg, lg_c.reshape(B * C, ROUTER_COLS)], axis=0)

    T = n_tok.shape[0]
    n_blocks = -(-T // MOE_BLOCK) + N_CLASSES
    info, counts = route_tokens(lg)
    dest, blk_a, blk_b, n_used = slot_plan(info, counts, n_blocks)
    xs = moe_dispatch(n_tok, info, dest, n_blocks)
    o_sorted = expert_pairs(xs, blk_a, blk_b, n_used, lp["exp_w_gate"].astype(BF16),
                            lp["exp_w_up"].astype(BF16), lp["exp_w_down"].astype(BF16))
    hx2 = moe_collect(o_sorted, dest[:B * S], h_tok, mod_x[:, 5], S, final_g, final_norm)
    hx2 = hx2.reshape(B, S, D)
    if need_ctx:
        hc2 = moe_collect(o_sorted, dest[B * S:], hc1.reshape(B * C, D), mod_c[:, 5], C,
                          final_g, False).reshape(B, C, D)
    else:
        hc2 = hc
    return hc2, hx2


def kernel(x, c, ctx, c_ctx, norm1_g, norm2_g, ada_w, ada_b, w_in, hy_short_w, hy_short_b, hy_ffn_w1, hy_ffn_b1, hy_ffn_w2, hy_ffn_b2, hy_ffn_w3, hy_bias, conf_dw_w, conf_dw_b, conf_ln_g, conf_ln_b, attn_sink, lru_conv_w, lru_conv_b, lru_wa, lru_ba, lru_wx, lru_bx, lru_lambda, group_norm_g, w_out, router_g_w, router_g_b, router_e_w, router_e_b, exp_w_gate, exp_w_up, exp_w_down, final_norm_g):
    stacked = dict(norm1_g=norm1_g, norm2_g=norm2_g, ada_w=ada_w, ada_b=ada_b, w_in=w_in,
                   hy_short_w=hy_short_w, hy_short_b=hy_short_b, hy_ffn_w1=hy_ffn_w1, hy_ffn_b1=hy_ffn_b1,
                   hy_ffn_w2=hy_ffn_w2, hy_ffn_b2=hy_ffn_b2, hy_ffn_w3=hy_ffn_w3, hy_bias=hy_bias,
                   conf_dw_w=conf_dw_w, conf_dw_b=conf_dw_b, conf_ln_g=conf_ln_g, conf_ln_b=conf_ln_b,
                   attn_sink=attn_sink, lru_conv_w=lru_conv_w, lru_conv_b=lru_conv_b, lru_wa=lru_wa,
                   lru_ba=lru_ba, lru_wx=lru_wx, lru_bx=lru_bx, lru_lambda=lru_lambda,
                   group_norm_g=group_norm_g, w_out=w_out, router_g_w=router_g_w, router_g_b=router_g_b,
                   router_e_w=router_e_w, router_e_b=router_e_b, exp_w_gate=exp_w_gate,
                   exp_w_up=exp_w_up, exp_w_down=exp_w_down)
    depth = norm1_g.shape[0]
    B = x.shape[0]
    cs = jnp.concatenate([jax.nn.silu(c), jnp.broadcast_to(jax.nn.silu(c_ctx)[None], (8, c.shape[1]))], axis=0)
    hc, hx = ctx, x
    tables_x = fft_tables(x.shape[1])
    tables_c = dft_tables(ctx.shape[1])
    for l in range(depth):
        lp = {k: v[l] for k, v in stacked.items()}
        hc, hx = _layer(hc, hx, cs, lp, need_ctx=(l < depth - 1), final_g=final_norm_g,
                        final_norm=(l == depth - 1), tables_x=tables_x, tables_c=tables_c)
    return hx
```

```python
import functools
import math

import jax
import jax.numpy as jnp
from jax import lax
from jax.experimental import pallas as pl
from jax.experimental.pallas import tpu as pltpu

F32 = jnp.float32
BF16 = jnp.bfloat16

EPS = 1e-6
NEG_INF = -1e30
GRID_W = 64
N_MIXERS = 4
D_GROUP = 256
HY_COLS = 3 * D_GROUP
CONF_COLS = 2 * D_GROUP
ATT_HEADS = 4
ATT_KV_HEADS = 2
HEAD_DIM = 64
ATT_COLS = (ATT_HEADS + 2 * ATT_KV_HEADS) * HEAD_DIM
LRU_COLS = 2 * D_GROUP
QK_COLS = (ATT_HEADS + ATT_KV_HEADS) * HEAD_DIM
WINDOW = 128
ATT_BLOCK = 128
ROPE_BASE = 10000.0
HY_EMB = 33
HY_FAST_DECAY = 0.3
HY_SLOW_DECAY = 1.5
HY_TARGET = 1e-2
CONF_KERNEL = 31
LRU_HEADS = 4
LRU_CONV = 4
LRU_C = 8.0
N_GROUPS = 4
EXP_PER_GROUP = 8
N_EXPERTS = N_GROUPS * EXP_PER_GROUP
TOP_K = 2
MOE_BLOCK = 256
ROUTER_COLS = 128

VMEM_LIMIT_BYTES = 56 * 1024 * 1024


def _cparams(*sem):
    return pltpu.CompilerParams(dimension_semantics=sem, vmem_limit_bytes=VMEM_LIMIT_BYTES)


def _linear_kernel(x_ref, w_ref, b_ref, o_ref):
    o_ref[...] = jnp.dot(x_ref[...], w_ref[...], preferred_element_type=F32,
                         precision=lax.Precision.HIGHEST) + b_ref[...]


def small_linear(x, w, b, tn=1024):
    M, K = x.shape
    N = w.shape[1]
    return pl.pallas_call(
        _linear_kernel,
        grid=(N // tn,),
        in_specs=[pl.BlockSpec((M, K), lambda j: (0, 0)),
                  pl.BlockSpec((K, tn), lambda j: (0, j)),
                  pl.BlockSpec((1, tn), lambda j: (0, j))],
        out_specs=pl.BlockSpec((M, tn), lambda j: (0, j)),
        out_shape=jax.ShapeDtypeStruct((M, N), F32),
        compiler_params=_cparams("parallel"),
        name="ada_linear",
    )(x, w, b.reshape(1, N))


def _in_proj_kernel(x_ref, sh_ref, sc_ref, g_ref, w_ref, cos_ref, sin_ref,
                    hy_ref, cf_ref, at_ref, lr_ref):
    x = x_ref[0]
    ms = jnp.mean(x * x, axis=-1, keepdims=True)
    y = x * lax.rsqrt(ms + EPS) * g_ref[...]
    y = y * (1.0 + sc_ref[0]) + sh_ref[0]
    u = jnp.dot(y.astype(BF16), w_ref[...], preferred_element_type=F32)
    c0 = HY_COLS
    c1 = c0 + CONF_COLS
    c2 = c1 + ATT_COLS
    c3 = c2 + LRU_COLS
    hy_ref[0] = u[:, :c0]
    cf_ref[0] = u[:, c0:c1]
    lr_ref[0] = u[:, c2:c3]
    qk = u[:, c1:c1 + QK_COLS]
    qk_rot = u[:, c3:c3 + QK_COLS]
    at_ref[0, :, :QK_COLS] = qk * cos_ref[...] + qk_rot * sin_ref[...]
    at_ref[0, :, QK_COLS:] = u[:, c1 + QK_COLS:c2]


def in_proj(h, shift, scale, g, w_ext, cos_t, sin_t, tm):
    B, L, D = h.shape
    NW = w_ext.shape[1]
    outs = [HY_COLS, CONF_COLS, ATT_COLS, LRU_COLS]
    return pl.pallas_call(
        _in_proj_kernel,
        grid=(B, L // tm),
        in_specs=[pl.BlockSpec((1, tm, D), lambda b, i: (b, i, 0)),
                  pl.BlockSpec((1, 1, D), lambda b, i: (b, 0, 0)),
                  pl.BlockSpec((1, 1, D), lambda b, i: (b, 0, 0)),
                  pl.BlockSpec((1, D), lambda b, i: (0, 0)),
                  pl.BlockSpec((D, NW), lambda b, i: (0, 0)),
                  pl.BlockSpec((tm, QK_COLS), lambda b, i: (i, 0)),
                  pl.BlockSpec((tm, QK_COLS), lambda b, i: (i, 0))],
        out_specs=[pl.BlockSpec((1, tm, n), lambda b, i: (b, i, 0)) for n in outs],
        out_shape=[jax.ShapeDtypeStruct((B, L, n), F32) for n in outs],
        compiler_params=_cparams("parallel", "parallel"),
        name="in_proj",
    )(h, shift, scale, g.reshape(1, D), w_ext, cos_t, sin_t)


def rope_tables(L, rotary):
    n_heads = ATT_HEADS + ATT_KV_HEADS
    if not rotary:
        return jnp.ones((L, QK_COLS), F32), jnp.zeros((L, QK_COLS), F32)
    pos = jnp.arange(L)
    row = (pos // GRID_W).astype(F32)
    col = (pos % GRID_W).astype(F32)
    half = HEAD_DIM // 2
    inv_freq = ROPE_BASE ** (-jnp.arange(0, half, 2, dtype=F32) / half)
    ang_r = row[:, None] * inv_freq[None]
    ang_c = col[:, None] * inv_freq[None]
    cos_h = jnp.concatenate([jnp.cos(ang_r)] * 2 + [jnp.cos(ang_c)] * 2, axis=-1)
    sin_h = jnp.concatenate([jnp.sin(ang_r)] * 2 + [jnp.sin(ang_c)] * 2, axis=-1)
    return jnp.tile(cos_h, (1, n_heads)), jnp.tile(sin_h, (1, n_heads))


def extend_w_in(w_in):
    c1 = HY_COLS + CONF_COLS
    wqk = w_in[:, c1:c1 + QK_COLS]
    D = w_in.shape[0]
    w4 = wqk.reshape(D, QK_COLS // 32, 2, 16)
    wrot = jnp.stack([-w4[:, :, 1], w4[:, :, 0]], axis=2).reshape(D, QK_COLS)
    return jnp.concatenate([w_in, wrot], axis=1).astype(BF16)


def _softmax_parts(q, k_list, extra_logit):
    scale = HEAD_DIM ** -0.5
    s_list = []
    for k, mask in k_list:
        s = lax.dot_general(q, k, (((1,), (1,)), ((), ())), preferred_element_type=F32) * scale
        if mask is not None:
            s = jnp.where(mask, s, NEG_INF)
        s_list.append(s)
    m = extra_logit
    for s in s_list:
        m = jnp.maximum(m, jnp.max(s, axis=-1, keepdims=True))
    p_list = [jnp.exp(s - m) for s in s_list]
    denom = jnp.exp(extra_logit - m)
    for p in p_list:
        denom = denom + jnp.sum(p, axis=-1, keepdims=True)
    return p_list, 1.0 / denom


ATT_Q_BLOCKS = 4


def _win_attn_kernel(sink_ref, q_ref, kp_ref, kc_ref, kn_ref, vp_ref, vc_ref, vn_ref,
                     kx_ref, vx_ref, o_ref, *, seq_len):
    i = pl.program_id(1)
    blk = ATT_BLOCK
    qb = q_ref.shape[1] // blk
    scale = HEAD_DIM ** -0.5
    g = ATT_HEADS // ATT_KV_HEADS
    kw = jnp.concatenate([kp_ref[0], kc_ref[0], kn_ref[0]], axis=0)
    vw = jnp.concatenate([vp_ref[0], vc_ref[0], vn_ref[0]], axis=0).astype(BF16)
    kwt = kw.T.astype(BF16)
    kxt = kx_ref[0].T.astype(BF16)
    vx = vx_ref[0].astype(BF16)
    row = lax.broadcasted_iota(jnp.int32, (g * blk, 3 * blk), 0) % blk
    col = lax.broadcasted_iota(jnp.int32, (g * blk, 3 * blk), 1)
    in_band = jnp.abs(col - blk - row) <= WINDOW
    for j in range(qb):
        q_blk = i * qb + j
        k_pos = (q_blk - 1) * blk + col
        valid = in_band & (k_pos >= 0) & (k_pos < seq_len)
        outs = []
        for kv in range(ATT_KV_HEADS):
            ksl = slice(kv * HEAD_DIM, (kv + 1) * HEAD_DIM)
            heads = range(kv * g, (kv + 1) * g)
            qs = jnp.concatenate([q_ref[0, j * blk:(j + 1) * blk, h * HEAD_DIM:(h + 1) * HEAD_DIM]
                                  for h in heads], axis=0).astype(BF16)
            sink = jnp.concatenate([jnp.full((blk, 1), sink_ref[h], F32) for h in heads], axis=0)
            s_win = jnp.dot(qs, kwt[ksl, j * blk:(j + 3) * blk], preferred_element_type=F32) * scale
            s_win = jnp.where(valid, s_win, NEG_INF)
            s_ctx = jnp.dot(qs, kxt[ksl, :], preferred_element_type=F32) * scale
            m = jnp.maximum(jnp.maximum(jnp.max(s_win, axis=-1, keepdims=True),
                                        jnp.max(s_ctx, axis=-1, keepdims=True)), sink)
            p_win = jnp.exp(s_win - m)
            p_ctx = jnp.exp(s_ctx - m)
            denom = (jnp.exp(sink - m) + jnp.sum(p_win, axis=-1, keepdims=True)
                     + jnp.sum(p_ctx, axis=-1, keepdims=True))
            o = (jnp.dot(p_win.astype(BF16), vw[j * blk:(j + 3) * blk, ksl], preferred_element_type=F32)
                 + jnp.dot(p_ctx.astype(BF16), vx[:, ksl], preferred_element_type=F32)) * (1.0 / denom)
            outs.extend([o[k * blk:(k + 1) * blk] for k in range(g)])
        o_ref[0, j * blk:(j + 1) * blk, :] = jnp.concatenate(outs, axis=-1)


def window_attention(at_x, at_c, sink):
    B, S, _ = at_x.shape
    C = at_c.shape[1]
    blk = ATT_BLOCK
    qb = ATT_Q_BLOCKS
    nb = S // blk
    kcol = QK_COLS // 128 - 1
    vcol = kcol + 1

    def edge_spec(col, off):
        return pl.BlockSpec((1, blk, 128), lambda b, i, s: (b, jnp.clip(i * qb + off, 0, nb - 1), col))

    def mid_spec(col):
        return pl.BlockSpec((1, qb * blk, 128), lambda b, i, s: (b, i, col))

    grid_spec = pltpu.PrefetchScalarGridSpec(
        num_scalar_prefetch=1,
        grid=(B, nb // qb),
        in_specs=[pl.BlockSpec((1, qb * blk, ATT_HEADS * HEAD_DIM), lambda b, i, s: (b, i, 0)),
                  edge_spec(kcol, -1), mid_spec(kcol), edge_spec(kcol, qb),
                  edge_spec(vcol, -1), mid_spec(vcol), edge_spec(vcol, qb),
                  pl.BlockSpec((1, C, 128), lambda b, i, s: (b, 0, kcol)),
                  pl.BlockSpec((1, C, 128), lambda b, i, s: (b, 0, vcol))],
        out_specs=pl.BlockSpec((1, qb * blk, ATT_HEADS * HEAD_DIM), lambda b, i, s: (b, i, 0)),
    )
    return pl.pallas_call(
        functools.partial(_win_attn_kernel, seq_len=S),
        grid_spec=grid_spec,
        out_shape=jax.ShapeDtypeStruct((B, S, ATT_HEADS * HEAD_DIM), F32),
        compiler_params=_cparams("parallel", "parallel"),
        name="window_attention",
    )(sink.astype(F32), at_x, at_x, at_x, at_x, at_x, at_x, at_x, at_c, at_c)


def _ctx_attn_kernel(sink_ref, q_ref, kx_ref, vx_ref, o_ref):
    q = q_ref[0].astype(BF16)
    kx = kx_ref[0].astype(BF16)
    vx = vx_ref[0].astype(BF16)
    g = ATT_HEADS // ATT_KV_HEADS
    outs = []
    for h in range(ATT_HEADS):
        kv = h // g
        qs = q[:, h * HEAD_DIM:(h + 1) * HEAD_DIM]
        ksl = slice(kv * HEAD_DIM, (kv + 1) * HEAD_DIM)
        (p_ctx,), inv = _softmax_parts(qs, [(kx[:, ksl], None)], sink_ref[h])
        outs.append(jnp.dot(p_ctx.astype(BF16), vx[:, ksl], preferred_element_type=F32) * inv)
    o_ref[0] = jnp.concatenate(outs, axis=-1)


def context_attention(at_c, sink):
    B, C, _ = at_c.shape
    kcol = QK_COLS // 128 - 1
    grid_spec = pltpu.PrefetchScalarGridSpec(
        num_scalar_prefetch=1,
        grid=(B,),
        in_specs=[pl.BlockSpec((1, C, ATT_HEADS * HEAD_DIM), lambda b, s: (b, 0, 0)),
                  pl.BlockSpec((1, C, 128), lambda b, s: (b, 0, kcol)),
                  pl.BlockSpec((1, C, 128), lambda b, s: (b, 0, kcol + 1))],
        out_specs=pl.BlockSpec((1, C, ATT_HEADS * HEAD_DIM), lambda b, s: (b, 0, 0)),
    )
    return pl.pallas_call(
        _ctx_attn_kernel,
        grid_spec=grid_spec,
        out_shape=jax.ShapeDtypeStruct((B, C, ATT_HEADS * HEAD_DIM), F32),
        compiler_params=_cparams("parallel"),
        name="context_attention",
    )(sink.astype(F32), at_c, at_c, at_c)


def _out_proj_kernel(y0_ref, y1_ref, y2_ref, y3_ref, h_ref, g1_ref, gng_ref, w_ref,
                     n2g_ref, sh_ref, sc_ref, wr_ref, br_ref, ho_ref, lg_ref):
    parts = []
    for k, y_ref in enumerate((y0_ref, y1_ref, y2_ref, y3_ref)):
        y = y_ref[0]
        ms = jnp.mean(y * y, axis=-1, keepdims=True)
        yn = y * lax.rsqrt(ms + EPS) * gng_ref[:, k * D_GROUP:(k + 1) * D_GROUP]
        parts.append(yn.astype(BF16))
    yn = jnp.concatenate(parts, axis=-1)
    proj = jnp.dot(yn, w_ref[...], preferred_element_type=F32)
    h = h_ref[0] + g1_ref[0] * proj
    ho_ref[0] = h
    ms = jnp.mean(h * h, axis=-1, keepdims=True)
    n = h * lax.rsqrt(ms + EPS) * n2g_ref[...]
    n = n * (1.0 + sc_ref[0]) + sh_ref[0]
    lg_ref[0] = jnp.dot(n.astype(BF16), wr_ref[...], preferred_element_type=F32) + br_ref[...]


def out_proj(ys, h, g1, gng, w_out, n2g, sh2, sc2, w_router, b_router, tm):
    B, L, D = h.shape
    row3 = lambda n: pl.BlockSpec((1, tm, n), lambda b, i: (b, i, 0))
    mod = pl.BlockSpec((1, 1, D), lambda b, i: (b, 0, 0))
    full = lambda r, c: pl.BlockSpec((r, c), lambda b, i: (0, 0))
    return pl.pallas_call(
        _out_proj_kernel,
        grid=(B, L // tm),
        in_specs=[row3(D_GROUP)] * 4 + [row3(D), mod, full(1, D), full(D, D), full(1, D), mod, mod,
                                        full(D, ROUTER_COLS), full(1, ROUTER_COLS)],
        out_specs=[row3(D), row3(ROUTER_COLS)],
        out_shape=[jax.ShapeDtypeStruct((B, L, D), F32), jax.ShapeDtypeStruct((B, L, ROUTER_COLS), F32)],
        compiler_params=_cparams("parallel", "parallel"),
        name="out_proj",
    )(*ys, h, g1, gng.reshape(1, D), w_out, n2g.reshape(1, D), sh2, sc2, w_router, b_router)


N_PAIRS = EXP_PER_GROUP * (EXP_PER_GROUP - 1) // 2
N_CLASSES = N_GROUPS * N_PAIRS
ROUTE_TOKENS = 512
INFO_CLASS, INFO_RANK, INFO_WA, INFO_WB = 0, 1, 2, 3


def _route_kernel(lg_ref, info_ref, cnt_ref, run):
    i = pl.program_id(0)
    tb = lg_ref.shape[0]

    @pl.when(i == 0)
    def _():
        run[...] = jnp.zeros_like(run)

    lg = lg_ref[...]
    li = lax.broadcasted_iota(jnp.int32, lg.shape, 1)
    big = jnp.int32(ROUTER_COLS)

    def first_argmax(vals):
        m = jnp.max(vals, axis=-1, keepdims=True)
        return m, jnp.min(jnp.where(vals == m, li, big), axis=-1, keepdims=True)

    gl = jnp.where(li < N_GROUPS, lg, NEG_INF)
    gmax, g_idx = first_argmax(gl)
    g_prob = 1.0 / jnp.sum(jnp.exp(gl - gmax), axis=-1, keepdims=True)
    lo = N_GROUPS + EXP_PER_GROUP * g_idx
    el = jnp.where((li >= lo) & (li < lo + EXP_PER_GROUP), lg, NEG_INF)
    m1, i1 = first_argmax(el)
    m2, i2 = first_argmax(jnp.where(li == i1, NEG_INF, el))
    e2 = jnp.exp(m2 - m1)
    w1 = g_prob / (1.0 + e2)
    w2 = g_prob * e2 / (1.0 + e2)
    j1 = i1 - lo
    j2 = i2 - lo
    a = jnp.minimum(j1, j2)
    b = jnp.maximum(j1, j2)
    cls = g_idx * N_PAIRS + ((a * (2 * EXP_PER_GROUP - 1 - a)) >> 1) + (b - a - 1)
    w_a = jnp.where(j1 < j2, w1, w2)
    w_b = jnp.where(j1 < j2, w2, w1)

    hit = li == cls
    onehot = jnp.where(hit, 1.0, 0.0)
    r_i = lax.broadcasted_iota(jnp.int32, (tb, tb), 0)
    c_i = lax.broadcasted_iota(jnp.int32, (tb, tb), 1)
    below = jnp.where(c_i < r_i, 1.0, 0.0).astype(BF16)
    before = jnp.dot(below, onehot.astype(BF16), preferred_element_type=F32)
    rank = jnp.sum(jnp.where(hit, before + run[...], 0.0), axis=-1, keepdims=True)
    run[...] = run[...] + jnp.sum(onehot, axis=0, keepdims=True)
    cnt_ref[...] = run[...]
    info = jnp.where(li == INFO_CLASS, cls.astype(F32), 0.0)
    info = jnp.where(li == INFO_RANK, rank, info)
    info = jnp.where(li == INFO_WA, w_a, info)
    info = jnp.where(li == INFO_WB, w_b, info)
    info_ref[...] = info


def route_tokens(logits):
    T = logits.shape[0]
    tb = ROUTE_TOKENS
    return pl.pallas_call(
        _route_kernel,
        grid=(T // tb,),
        in_specs=[pl.BlockSpec((tb, ROUTER_COLS), lambda i: (i, 0))],
        out_specs=[pl.BlockSpec((tb, ROUTER_COLS), lambda i: (i, 0)),
                   pl.BlockSpec((1, ROUTER_COLS), lambda i: (0, 0))],
        out_shape=[jax.ShapeDtypeStruct((T, ROUTER_COLS), F32), jax.ShapeDtypeStruct((1, ROUTER_COLS), F32)],
        scratch_shapes=[pltpu.VMEM((1, ROUTER_COLS), F32)],
        compiler_params=_cparams("arbitrary"),
        name="moe_route",
    )(logits)


def _pair_tables():
    a_tab, b_tab = [], []
    for g in range(N_GROUPS):
        for a in range(EXP_PER_GROUP):
            for b in range(a + 1, EXP_PER_GROUP):
                a_tab.append(g * EXP_PER_GROUP + a)
                b_tab.append(g * EXP_PER_GROUP + b)
    return jnp.array(a_tab, jnp.int32), jnp.array(b_tab, jnp.int32)


SUBLANES = 8


def _slot_kernel(info_ref, start_ref, dest_ref):
    info = info_ref[...]
    li = lax.broadcasted_iota(jnp.int32, info.shape, 1)
    cls = info[:, INFO_CLASS:INFO_CLASS + 1].astype(jnp.int32)
    start = jnp.sum(jnp.where(li == cls, start_ref[...], 0.0), axis=-1, keepdims=True)
    dest = start + info[:, INFO_RANK:INFO_RANK + 1]
    rec = jnp.where(li == 0, dest, 0.0).T
    dest_ref[0] = rec[:SUBLANES].astype(jnp.int32)


def slot_plan(info, counts, n_blocks):
    T = info.shape[0]
    tb = ROUTE_TOKENS
    cnt = counts[0, :N_CLASSES].astype(jnp.int32)
    padded = (cnt + MOE_BLOCK - 1) // MOE_BLOCK * MOE_BLOCK
    pad_end = jnp.cumsum(padded)
    pad_start = jnp.zeros((1, ROUTER_COLS), F32).at[0, :N_CLASSES].set((pad_end - padded).astype(F32))
    dest = pl.pallas_call(
        _slot_kernel,
        grid=(T // tb,),
        in_specs=[pl.BlockSpec((tb, ROUTER_COLS), lambda i: (i, 0)),
                  pl.BlockSpec((1, ROUTER_COLS), lambda i: (0, 0))],
        out_specs=pl.BlockSpec((1, SUBLANES, tb), lambda i: (i, 0, 0)),
        out_shape=jax.ShapeDtypeStruct((T // tb, SUBLANES, tb), jnp.int32),
        compiler_params=_cparams("parallel"),
        name="moe_slots",
    )(info, pad_start)[:, 0, :].reshape(T)
    n_used = (pad_end[-1] // MOE_BLOCK).astype(jnp.int32).reshape(1)
    blk_cls = jnp.minimum(jnp.searchsorted(pad_end, jnp.arange(n_blocks) * MOE_BLOCK, side="right"),
                          N_CLASSES - 1)
    a_tab, b_tab = _pair_tables()
    return dest, a_tab[blk_cls], b_tab[blk_cls], n_used


DISPATCH_TOKENS = 256


def _wait_rows(buf, sem):
    pltpu.make_async_copy(buf, buf, sem).wait()


DMA_UNROLL = 8


def _dispatch_kernel(dest_ref, hx_ref, hc_ref, info_ref, g_ref, shx_ref, scx_ref, shc_ref, scc_ref, zeros_hbm,
                     xs_hbm, rows, sems, *, n_latent_blocks):
    del zeros_hbm
    i = pl.program_id(0)
    n = pl.num_programs(0)
    slot = i % 2
    D = hx_ref.shape[1]
    tb = hx_ref.shape[0]

    @pl.when(i >= 2)
    def _():
        _wait_rows(rows.at[slot], sems.at[slot])

    def normed(h_ref, sh_ref, sc_ref):
        h = h_ref[...]
        ms = jnp.mean(h * h, axis=-1, keepdims=True)
        return h * lax.rsqrt(ms + EPS) * g_ref[...] * (1.0 + sc_ref[0]) + sh_ref[0]

    @pl.when(i < n_latent_blocks)
    def _():
        rows[slot, :, :D] = normed(hx_ref, shx_ref, scx_ref)

    @pl.when(i >= n_latent_blocks)
    def _():
        rows[slot, :, :D] = normed(hc_ref, shc_ref, scc_ref)

    rows[slot, :, D:] = info_ref[...]

    def body(r, carry):
        pltpu.make_async_copy(rows.at[slot, r], xs_hbm.at[dest_ref[0, 0, r]], sems.at[slot]).start()
        return carry
    lax.fori_loop(0, tb, body, 0, unroll=DMA_UNROLL)

    @pl.when(i == n - 1)
    def _():
        _wait_rows(rows.at[slot], sems.at[slot])

        @pl.when(n >= 2)
        def _():
            _wait_rows(rows.at[1 - slot], sems.at[1 - slot])


def moe_dispatch(h_x, h_c, info, dest, n_blocks, n2g, mod_x, mod_c, tokens_per_batch):
    Tx, D = h_x.shape
    tb = DISPATCH_TOKENS
    nxb = Tx // tb
    if h_c is None:
        h_c, mod_c, ncb = h_x, mod_x, 0
    else:
        ncb = h_c.shape[0] // tb
    per_b = tokens_per_batch // tb
    W = D + ROUTER_COLS
    P = n_blocks * MOE_BLOCK
    xi = lambda i: jnp.minimum(i, nxb - 1)
    ci = lambda i: jnp.maximum(i - nxb, 0)
    modx = pl.BlockSpec((1, 1, D), lambda i: (xi(i) // per_b, 0, 0))
    modc = pl.BlockSpec((1, 1, D), lambda i: (0, 0, 0))
    return pl.pallas_call(
        functools.partial(_dispatch_kernel, n_latent_blocks=nxb),
        grid=(nxb + ncb,),
        in_specs=[pl.BlockSpec((1, 1, tb), lambda i: (i, 0, 0), memory_space=pltpu.SMEM),
                  pl.BlockSpec((tb, D), lambda i: (xi(i), 0)),
                  pl.BlockSpec((tb, D), lambda i: (ci(i), 0)),
                  pl.BlockSpec((tb, ROUTER_COLS), lambda i: (i, 0)),
                  pl.BlockSpec((1, D), lambda i: (0, 0)),
                  modx, modx, modc, modc,
                  pl.BlockSpec(memory_space=pl.ANY)],
        out_specs=pl.BlockSpec(memory_space=pl.ANY),
        out_shape=jax.ShapeDtypeStruct((P, W), F32),
        scratch_shapes=[pltpu.VMEM((2, tb, W), F32), pltpu.SemaphoreType.DMA((2,))],
        input_output_aliases={9: 0},
        compiler_params=_cparams("arbitrary"),
        name="moe_dispatch",
    )(dest.reshape(-1, 1, tb), h_x, h_c, info, n2g.reshape(1, D), mod_x[0], mod_x[1], mod_c[0], mod_c[1],
      jnp.zeros((P, W), F32))


def _expert_pair_kernel(ea_ref, eb_ref, nused_ref, xs_ref, wga_ref, wua_ref, wda_ref, wgb_ref, wub_ref, wdb_ref,
                        o_ref):
    del ea_ref, eb_ref
    i = pl.program_id(0)
    D = o_ref.shape[1]

    @pl.when(i < nused_ref[0])
    def _():
        xb = xs_ref[:, :D].astype(BF16)

        def ffn(wg_ref, wu_ref, wd_ref):
            gate = jnp.dot(xb, wg_ref[0], preferred_element_type=F32)
            up = jnp.dot(xb, wu_ref[0], preferred_element_type=F32)
            hid = (gate * jax.nn.sigmoid(gate) * up).astype(BF16)
            return jnp.dot(hid, wd_ref[0], preferred_element_type=F32)

        w_a = xs_ref[:, D + INFO_WA:D + INFO_WA + 1]
        w_b = xs_ref[:, D + INFO_WB:D + INFO_WB + 1]
        o_ref[...] = w_a * ffn(wga_ref, wua_ref, wda_ref) + w_b * ffn(wgb_ref, wub_ref, wdb_ref)

    @pl.when(i >= nused_ref[0])
    def _():
        o_ref[...] = jnp.zeros_like(o_ref)


def expert_pairs(xs, blk_a, blk_b, n_used, w_gate, w_up, w_down):
    P, W = xs.shape
    D = W - ROUTER_COLS
    n_blocks = P // MOE_BLOCK
    DE = w_gate.shape[-1]
    wspec = lambda shape, which: pl.BlockSpec(shape, lambda i, ea, eb, nu: ((ea, eb)[which][i], 0, 0))
    grid_spec = pltpu.PrefetchScalarGridSpec(
        num_scalar_prefetch=3,
        grid=(n_blocks,),
        in_specs=[pl.BlockSpec((MOE_BLOCK, W), lambda i, ea, eb, nu: (i, 0)),
                  wspec((1, D, DE), 0), wspec((1, D, DE), 0), wspec((1, DE, D), 0),
                  wspec((1, D, DE), 1), wspec((1, D, DE), 1), wspec((1, DE, D), 1)],
        out_specs=pl.BlockSpec((MOE_BLOCK, D), lambda i, ea, eb, nu: (i, 0)),
    )
    return pl.pallas_call(
        _expert_pair_kernel,
        grid_spec=grid_spec,
        out_shape=jax.ShapeDtypeStruct((P, D), F32),
        compiler_params=_cparams("arbitrary"),
        name="moe_experts",
    )(blk_a, blk_b, n_used, xs, w_gate, w_up, w_down, w_gate, w_up, w_down)


def _gather_rows(idx_ref, src_hbm, buf, sem, n_rows):
    def body(r, carry):
        pltpu.make_async_copy(src_hbm.at[idx_ref[0, 0, r]], buf.at[r], sem).start()
        return carry
    lax.fori_loop(0, n_rows, body, 0, unroll=DMA_UNROLL)


def _collect_kernel(dest_ref, dest_next_ref, o_hbm, h_ref, g2_ref, fg_ref, out_ref, obuf, sems, *, final_norm):
    i = pl.program_id(0)
    n = pl.num_programs(0)
    slot = i % 2
    tb = h_ref.shape[0]

    @pl.when(i == 0)
    def _():
        _gather_rows(dest_ref, o_hbm, obuf.at[0], sems.at[0], tb)

    @pl.when(i + 1 < n)
    def _():
        _gather_rows(dest_next_ref, o_hbm, obuf.at[1 - slot], sems.at[1 - slot], tb)

    _wait_rows(obuf.at[slot], sems.at[slot])
    h = h_ref[...] + g2_ref[0] * obuf[slot]
    if final_norm:
        ms = jnp.mean(h * h, axis=-1, keepdims=True)
        h = h * lax.rsqrt(ms + EPS) * fg_ref[...]
    out_ref[...] = h


def moe_collect(o_sorted, dest, h_tokens, g2, tokens_per_batch, final_g, final_norm):
    T, D = h_tokens.shape
    tb = DISPATCH_TOKENS
    nt = T // tb
    per_b = tokens_per_batch // tb
    dest3 = dest.reshape(nt, 1, tb)
    return pl.pallas_call(
        functools.partial(_collect_kernel, final_norm=final_norm),
        grid=(nt,),
        in_specs=[pl.BlockSpec((1, 1, tb), lambda i: (i, 0, 0), memory_space=pltpu.SMEM),
                  pl.BlockSpec((1, 1, tb), lambda i: (jnp.minimum(i + 1, nt - 1), 0, 0), memory_space=pltpu.SMEM),
                  pl.BlockSpec(memory_space=pl.ANY),
                  pl.BlockSpec((tb, D), lambda i: (i, 0)),
                  pl.BlockSpec((1, 1, D), lambda i: (i // per_b, 0, 0)),
                  pl.BlockSpec((1, D), lambda i: (0, 0))],
        out_specs=pl.BlockSpec((tb, D), lambda i: (i, 0)),
        out_shape=jax.ShapeDtypeStruct((T, D), F32),
        scratch_shapes=[pltpu.VMEM((2, tb, D), F32), pltpu.SemaphoreType.DMA((2,))],
        compiler_params=_cparams("arbitrary"),
        name="moe_collect",
    )(dest3, dest3, o_sorted, h_tokens, g2, final_g.reshape(1, D))


CONV_MARGIN = 16


def _time_chunk(L):
    return min(L, 256)


LANES = 128


def _zero_margins(pad_ref, L):
    zeros = jnp.zeros((CONV_MARGIN, LANES), F32)
    for s in range(pad_ref.shape[0]):
        pad_ref[s, pl.ds(0, CONV_MARGIN), :] = zeros
        pad_ref[s, pl.ds(CONV_MARGIN + L, CONV_MARGIN), :] = zeros


def _dw_conv_slab(pad_ref, s, base, T, w_ref, b_ref, col, taps, pad_left):
    acc = jnp.broadcast_to(b_ref[:, col:col + LANES], (T, LANES))
    for k in range(taps):
        acc = acc + w_ref[k:k + 1, col:col + LANES] * pad_ref[s, pl.ds(base + (CONV_MARGIN - pad_left + k), T), :]
    return acc


def _conformer_kernel(u_ref, w_ref, b_ref, g_ref, beta_ref, o_ref, ypad):
    L = o_ref.shape[1]
    T = _time_chunk(L)
    C = D_GROUP
    n_slabs = C // LANES
    pad = (CONF_KERNEL - 1) // 2
    _zero_margins(ypad, L)

    def glu(j, carry):
        base = pl.multiple_of(j * T, T)
        for s in range(n_slabs):
            a = u_ref[0, pl.ds(base, T), s * LANES:(s + 1) * LANES]
            gate = u_ref[0, pl.ds(base, T), C + s * LANES:C + (s + 1) * LANES]
            ypad[s, pl.ds(CONV_MARGIN + base, T), :] = a * jax.nn.sigmoid(gate)
        return carry
    lax.fori_loop(0, L // T, glu, 0)

    def conv(j, carry):
        base = pl.multiple_of(j * T, T)
        acc = jnp.concatenate([_dw_conv_slab(ypad, s, base, T, w_ref, b_ref, s * LANES, CONF_KERNEL, pad)
                               for s in range(n_slabs)], axis=-1)
        mu = jnp.mean(acc, axis=-1, keepdims=True)
        cen = acc - mu
        var = jnp.mean(cen * cen, axis=-1, keepdims=True)
        y = cen * lax.rsqrt(var + EPS) * g_ref[...] + beta_ref[...]
        o_ref[0, pl.ds(base, T), :] = y * jax.nn.sigmoid(y)
        return carry
    lax.fori_loop(0, L // T, conv, 0)


def conformer_conv(u, w, b, ln_g, ln_b):
    B, L, _ = u.shape
    C = D_GROUP
    vec = pl.BlockSpec((1, C), lambda i: (0, 0))
    return pl.pallas_call(
        _conformer_kernel,
        grid=(B,),
        in_specs=[pl.BlockSpec((1, L, 2 * C), lambda i: (i, 0, 0)),
                  pl.BlockSpec((CONF_KERNEL, C), lambda i: (0, 0)), vec, vec, vec],
        out_specs=pl.BlockSpec((1, L, C), lambda i: (i, 0, 0)),
        out_shape=jax.ShapeDtypeStruct((B, L, C), F32),
        scratch_shapes=[pltpu.VMEM((C // LANES, L + 2 * CONV_MARGIN, LANES), F32)],
        compiler_params=_cparams("parallel"),
        name="conformer_conv",
    )(u, w, b.reshape(1, C), ln_g.reshape(1, C), ln_b.reshape(1, C))


def _gelu_tanh(x):
    return 0.5 * x * (1.0 + jnp.tanh(math.sqrt(2.0 / math.pi) * (x + 0.044715 * (x * x * x))))


def _lru_kernel(uc_ref, ux_ref, cw_ref, cb_ref, wcat_ref, bcat_ref, lam_ref, *rest, need_ctx):
    if need_ctx:
        oc_ref, ox_ref, cpad, xpad, a_s, b_s, yx, yc = rest
    else:
        ox_ref, cpad, xpad, a_s, b_s, yx = rest
        oc_ref = yc = None
    C = D_GROUP
    n_slabs = C // LANES
    Lc = uc_ref.shape[1]
    Lx = ux_ref.shape[1]
    pad_l = (LRU_CONV - 1) // 2

    def fill(pad_ref, u_ref, L):
        T = _time_chunk(L)
        _zero_margins(pad_ref, L)

        def body(j, carry):
            base = pl.multiple_of(j * T, T)
            for s in range(n_slabs):
                pad_ref[s, pl.ds(CONV_MARGIN + base, T), :] = u_ref[0, pl.ds(base, T),
                                                                    C + s * LANES:C + (s + 1) * LANES]
            return carry
        lax.fori_loop(0, L // T, body, 0)

    fill(cpad, uc_ref, Lc)
    fill(xpad, ux_ref, Lx)

    def coeffs(pad_ref, base, T, d):
        x = jnp.concatenate([_dw_conv_slab(pad_ref, s, base, T, cw_ref, cb_ref, s * LANES, LRU_CONV, pad_l)
                             for s in range(n_slabs)], axis=-1)
        g = jnp.dot(x.astype(BF16), wcat_ref[:, 2 * d * C:2 * (d + 1) * C],
                    preferred_element_type=F32) + bcat_ref[:, 2 * d * C:2 * (d + 1) * C]
        r = jax.nn.sigmoid(g[:, :C])
        i = jax.nn.sigmoid(g[:, C:])
        z = -lam_ref[d:d + 1, :]
        softplus = jnp.maximum(z, 0.0) + jnp.log(1.0 + jnp.exp(-jnp.abs(z)))
        a = jnp.exp(-LRU_C * r * softplus)
        b = jnp.sqrt(1.0 - a * a) * (i * x)
        for s in range(n_slabs):
            a_s[d * n_slabs + s, pl.ds(0, T), :] = a[:, s * LANES:(s + 1) * LANES]
            b_s[d * n_slabs + s, pl.ds(0, T), :] = b[:, s * LANES:(s + 1) * LANES]

    def run(pad_ref, L, h, y_ref):
        T = _time_chunk(L)
        n = L // T

        def chunk(j, h):
            base_f = pl.multiple_of(j * T, T)
            base_b = pl.multiple_of((n - 1 - j) * T, T)
            coeffs(pad_ref, base_f, T, 0)
            coeffs(pad_ref, base_b, T, 1)

            def step(t, h):
                new = []
                for d, (base, row) in enumerate(((base_f, t), (base_b, T - 1 - t))):
                    for s in range(n_slabs):
                        k = d * n_slabs + s
                        hs = a_s[k, pl.ds(row, 1), :] * h[k] + b_s[k, pl.ds(row, 1), :]
                        if y_ref is not None:
                            y_ref[k, pl.ds(base + row, 1), :] = hs
                        new.append(hs)
                return tuple(new)
            return lax.fori_loop(0, T, step, h, unroll=8)
        return lax.fori_loop(0, n, chunk, h)

    h = tuple(jnp.zeros((1, LANES), F32) for _ in range(2 * n_slabs))
    h = run(cpad, Lc, h, yc)
    run(xpad, Lx, h, yx)

    def finish(u_ref, y_ref, o_ref, L):
        T = _time_chunk(L)

        def body(j, carry):
            base = pl.multiple_of(j * T, T)
            y = jnp.concatenate([y_ref[s, pl.ds(base, T), :] + y_ref[n_slabs + s, pl.ds(base, T), :]
                                 for s in range(n_slabs)], axis=-1)
            o_ref[0, pl.ds(base, T), :] = _gelu_tanh(u_ref[0, pl.ds(base, T), :C]) * y
            return carry
        lax.fori_loop(0, L // T, body, 0)

    finish(ux_ref, yx, ox_ref, Lx)
    if need_ctx:
        finish(uc_ref, yc, oc_ref, Lc)


def _block_diag(w):
    H, n, _ = w.shape
    eye = jnp.eye(H, dtype=w.dtype)
    return (eye[:, None, :, None] * w[:, :, None, :]).reshape(H * n, H * n)


def rglru_mixer(uc, ux, lp, need_ctx):
    B, Lc, _ = uc.shape
    Lx = ux.shape[1]
    C = D_GROUP
    wcat = jnp.concatenate([_block_diag(lp["lru_wa"][0]), _block_diag(lp["lru_wx"][0]),
                            _block_diag(lp["lru_wa"][1]), _block_diag(lp["lru_wx"][1])], axis=1).astype(BF16)
    bcat = jnp.concatenate([lp["lru_ba"][0], lp["lru_bx"][0], lp["lru_ba"][1], lp["lru_bx"][1]]).reshape(1, 4 * C)
    full = lambda r, c: pl.BlockSpec((r, c), lambda i: (0, 0))
    seq = lambda L, n: pl.BlockSpec((1, L, n), lambda i: (i, 0, 0))
    out_specs = [seq(Lx, C)]
    out_shape = [jax.ShapeDtypeStruct((B, Lx, C), F32)]
    if need_ctx:
        out_specs = [seq(Lc, C)] + out_specs
        out_shape = [jax.ShapeDtypeStruct((B, Lc, C), F32)] + out_shape
    T = _time_chunk(Lx)
    slab = lambda rows, n=1: pltpu.VMEM((n * C // LANES, rows, LANES), F32)
    scratch = [slab(Lc + 2 * CONV_MARGIN), slab(Lx + 2 * CONV_MARGIN), slab(T, 2), slab(T, 2), slab(Lx, 2)]
    if need_ctx:
        scratch.append(slab(Lc, 2))
    res = pl.pallas_call(
        functools.partial(_lru_kernel, need_ctx=need_ctx),
        grid=(B,),
        in_specs=[seq(Lc, 2 * C), seq(Lx, 2 * C), full(LRU_CONV, C), full(1, C), full(C, 4 * C),
                  full(1, 4 * C), full(2, C)],
        out_specs=out_specs,
        out_shape=out_shape,
        scratch_shapes=scratch,
        compiler_params=_cparams("parallel"),
        name="rglru",
    )(uc, ux, lp["lru_conv_w"], lp["lru_conv_b"].reshape(1, C), wcat, bcat, lp["lru_lambda"])
    if need_ctx:
        return res[0], res[1]
    return None, res[0]


HY_SHORT = 3


def _short_conv(pad_ref, base, T, w_ref, b_ref, c0, c1):
    return jnp.concatenate([_dw_conv_slab(pad_ref, col // LANES, base, T, w_ref, b_ref, col, HY_SHORT, 1)
                            for col in range(c0, c1, LANES)], axis=-1)


def _fill_padded(pad_ref, u_ref, L, T):
    _zero_margins(pad_ref, L)

    def body(j, carry):
        base = pl.multiple_of(j * T, T)
        for s in range(pad_ref.shape[0]):
            pad_ref[s, pl.ds(CONV_MARGIN + base, T), :] = u_ref[0, pl.ds(base, T), s * LANES:(s + 1) * LANES]
        return carry
    lax.fori_loop(0, L // T, body, 0)


def _hyena_pre_kernel(u_ref, w_ref, b_ref, z_ref, upad):
    L = u_ref.shape[1]
    T = _time_chunk(L)
    C = D_GROUP
    _fill_padded(upad, u_ref, L, T)

    def body(j, carry):
        base = pl.multiple_of(j * T, T)
        x1 = _short_conv(upad, base, T, w_ref, b_ref, C, 2 * C)
        v = _short_conv(upad, base, T, w_ref, b_ref, 2 * C, 3 * C)
        z_ref[pl.ds(base, T), :] = (x1 * v).astype(BF16)
        return carry
    lax.fori_loop(0, L // T, body, 0)


def _hyena_post_kernel(u_ref, y_ref, w_ref, b_ref, bias_ref, o_ref, upad):
    L = u_ref.shape[1]
    T = _time_chunk(L)
    C = D_GROUP
    _fill_padded(upad, u_ref, L, T)

    def body(j, carry):
        base = pl.multiple_of(j * T, T)
        x0 = _short_conv(upad, base, T, w_ref, b_ref, 0, C)
        x1 = _short_conv(upad, base, T, w_ref, b_ref, C, 2 * C)
        v = _short_conv(upad, base, T, w_ref, b_ref, 2 * C, 3 * C)
        o_ref[0, pl.ds(base, T), :] = x0 * (y_ref[pl.ds(base, T), :] + (x1 * v) * bias_ref[...])
        return carry
    lax.fori_loop(0, L // T, body, 0)


def _spectrum_kernel(f_ref, z_ref, ha_ref, hb_ref, hc_ref, y_ref):
    tf = ha_ref.shape[0]
    acc = jnp.dot(f_ref[...], z_ref[...], preferred_element_type=F32)
    zr = acc[:tf]
    zi = acc[tf:]
    y_ref[:tf, :] = (zr * ha_ref[...] - zi * hb_ref[...]).astype(BF16)
    y_ref[tf:, :] = (zr * hb_ref[...] + zi * hc_ref[...]).astype(BF16)


def _idft_kernel(f_ref, y_ref, o_ref):
    o_ref[...] = jnp.dot(f_ref[...], y_ref[...], preferred_element_type=F32)


def dft_tables(L):
    N = 2 * L
    tf = min(256, L)
    k = jnp.arange(L, dtype=jnp.int32)
    n = jnp.arange(L, dtype=jnp.int32)
    ang = (2.0 * math.pi / N) * ((k[:, None] * n[None, :]) % N).astype(F32)
    cos = jnp.cos(ang)
    sin = jnp.sin(ang)
    nyq = jnp.where(n % 2 == 0, 1.0, -1.0).astype(F32)
    f_re = cos
    f_im = (-sin).at[0].set(nyq)
    fwd = jnp.stack([f_re.reshape(L // tf, tf, L), f_im.reshape(L // tf, tf, L)], axis=1).reshape(N, L)
    ck = jnp.where(k == 0, 1.0, 2.0).astype(F32)[:, None] / N
    i_re = cos * ck
    i_im = (-sin * ck).at[0].set(nyq / N)
    inv = jnp.stack([i_re.reshape(L // tf, tf, L), i_im.reshape(L // tf, tf, L)], axis=1).reshape(N, L).T
    return fwd.astype(BF16), inv.astype(BF16)


def filter_spectrum(h_fwd, h_bwd):
    L, C = h_fwd.shape
    k = jnp.concatenate([h_fwd, jnp.zeros((1, C), F32), h_bwd[1:][::-1]], axis=0)
    hf = jnp.fft.rfft(k, axis=0)
    hr = jnp.real(hf)
    hi = jnp.imag(hf)
    a = hr[:L]
    b = hi[:L].at[0].set(0.0)
    c = hr[:L].at[0].set(hr[L])
    return a, b, c


def hyena_mixer(u, lp, tables):
    B, L, _ = u.shape
    C = D_GROUP
    N = 2 * L
    fwd, inv = tables
    tf = min(256, L)
    T = _time_chunk(L)
    w, bsh = lp["hy_short_w"], lp["hy_short_b"].reshape(1, 3 * C)
    z2 = pl.pallas_call(
        _hyena_pre_kernel,
        grid=(B,),
        in_specs=[pl.BlockSpec((1, L, 3 * C), lambda b: (b, 0, 0)),
                  pl.BlockSpec((HY_SHORT, 3 * C), lambda b: (0, 0)),
                  pl.BlockSpec((1, 3 * C), lambda b: (0, 0))],
        out_specs=pl.BlockSpec((L, C), lambda b: (0, b)),
        out_shape=jax.ShapeDtypeStruct((L, B * C), BF16),
        scratch_shapes=[pltpu.VMEM((3 * C // LANES, L + 2 * CONV_MARGIN, LANES), F32)],
        compiler_params=_cparams("parallel"),
        name="hyena_pre",
    )(u, w, bsh)

    h_fwd, h_bwd = _hyena_filters(L, lp)
    tn = 2 * C
    ha, hb, hc = [jnp.tile(t, (1, tn // C)) for t in filter_spectrum(h_fwd, h_bwd)]
    hspec = pl.BlockSpec((tf, tn), lambda i, j: (i, 0))
    y2 = pl.pallas_call(
        _spectrum_kernel,
        grid=(L // tf, B * C // tn),
        in_specs=[pl.BlockSpec((2 * tf, L), lambda i, j: (i, 0)),
                  pl.BlockSpec((L, tn), lambda i, j: (0, j)), hspec, hspec, hspec],
        out_specs=pl.BlockSpec((2 * tf, tn), lambda i, j: (i, j)),
        out_shape=jax.ShapeDtypeStruct((N, B * C), BF16),
        compiler_params=_cparams("parallel", "parallel"),
        name="hyena_spectrum",
    )(fwd, z2, ha, hb, hc)

    tl = min(256, L)
    yt = pl.pallas_call(
        _idft_kernel,
        grid=(L // tl, B * C // tn),
        in_specs=[pl.BlockSpec((tl, N), lambda i, j: (i, 0)),
                  pl.BlockSpec((N, tn), lambda i, j: (0, j))],
        out_specs=pl.BlockSpec((tl, tn), lambda i, j: (i, j)),
        out_shape=jax.ShapeDtypeStruct((L, B * C), F32),
        compiler_params=_cparams("parallel", "parallel"),
        name="hyena_idft",
    )(inv, y2)

    return pl.pallas_call(
        _hyena_post_kernel,
        grid=(B,),
        in_specs=[pl.BlockSpec((1, L, 3 * C), lambda b: (b, 0, 0)),
                  pl.BlockSpec((L, C), lambda b: (0, b)),
                  pl.BlockSpec((HY_SHORT, 3 * C), lambda b: (0, 0)),
                  pl.BlockSpec((1, 3 * C), lambda b: (0, 0)),
                  pl.BlockSpec((1, C), lambda b: (0, 0))],
        out_specs=pl.BlockSpec((1, L, C), lambda b: (b, 0, 0)),
        out_shape=jax.ShapeDtypeStruct((B, L, C), F32),
        scratch_shapes=[pltpu.VMEM((3 * C // LANES, L + 2 * CONV_MARGIN, LANES), F32)],
        compiler_params=_cparams("parallel"),
        name="hyena_post",
    )(u, yt, w, bsh, lp["hy_bias"].reshape(1, C))


FFT_N2 = 128
FFT_UNROLL = 8


class _FftPlan:
    def __init__(self, L):
        self.L = L
        self.N = 2 * L
        self.N1 = self.N // FFT_N2
        self.KH = self.N1 // 2 + 1
        self.KP = -(-self.KH // 8) * 8
        self.PA = 2 * self.KP + 4


def fft_tables(L):
    p = _FftPlan(L)
    N, N1, KH, KP = p.N, p.N1, p.KH, p.KP
    n2 = jnp.arange(FFT_N2, dtype=jnp.int32)
    k1 = jnp.arange(KP, dtype=jnp.int32)
    n1 = jnp.arange(N1, dtype=jnp.int32)
    n = FFT_N2 * n1[None, None, :] + n2[:, None, None]
    ang = (2.0 * math.pi / N) * ((k1[None, :, None] * n) % N).astype(F32)
    keep = (k1 < KH)[None, :, None]
    g_re = jnp.where(keep, jnp.cos(ang), 0.0)
    g_im = jnp.where(keep, -jnp.sin(ang), 0.0)
    ga_full = jnp.concatenate([g_re, g_im], axis=1)
    ck = jnp.where((k1 == 0) | (k1 == N1 // 2), 1.0, 2.0) / N
    ga_inv = jnp.swapaxes(ga_full[:, :, :N1 // 2] * jnp.tile(ck, 2)[None, :, None], 1, 2)
    kk = jnp.arange(FFT_N2, dtype=jnp.int32)
    ang2 = (2.0 * math.pi / FFT_N2) * ((kk[:, None] * kk[None, :]) % FFT_N2).astype(F32)
    fr, fi = jnp.cos(ang2), -jnp.sin(ang2)
    fb = jnp.block([[fr, -fi], [fi, fr]])
    fb_inv = jnp.block([[fr, fi], [-fi, fr]])
    return dict(ga_half=ga_full[:, :, :N1 // 2].astype(BF16), ga_full=ga_full.astype(BF16),
                ga_inv=ga_inv.astype(BF16), fb=fb.astype(BF16), fb_inv=fb_inv.astype(BF16))


def _fft_stage_a(x_ref, ga_ref, s_ref, plan, n1_count):
    n_slabs = x_ref.shape[0]

    def body(n2, carry):
        xs = jnp.concatenate([x_ref[s, pl.ds(n2, n1_count, stride=FFT_N2), :] for s in range(n_slabs)], axis=-1)
        a = jnp.dot(ga_ref[n2], xs.astype(BF16), preferred_element_type=F32)
        for s in range(n_slabs):
            s_ref[s, pl.ds(n2 * plan.PA, 2 * plan.KP), :] = a[:, s * LANES:(s + 1) * LANES]
        return carry
    lax.fori_loop(0, FFT_N2, body, 0, unroll=FFT_UNROLL)


def _fft_load_k1(s_ref, k1, plan):
    n_slabs = s_ref.shape[0]
    re = jnp.concatenate([s_ref[s, pl.ds(k1, FFT_N2, stride=plan.PA), :] for s in range(n_slabs)], axis=-1)
    im = jnp.concatenate([s_ref[s, pl.ds(plan.KP + k1, FFT_N2, stride=plan.PA), :] for s in range(n_slabs)], axis=-1)
    return jnp.concatenate([re, im], axis=0).astype(BF16)


def _fft_filter_kernel(k_ref, ga_ref, fb_ref, h_ref, s_ref, *, plan):
    _fft_stage_a(k_ref, ga_ref, s_ref, plan, plan.N1)

    def body(k1, carry):
        h_ref[k1] = jnp.dot(fb_ref[...], _fft_load_k1(s_ref, k1, plan), preferred_element_type=F32).astype(BF16)
        return carry
    lax.fori_loop(0, plan.KH, body, 0)


def _fft_conv_kernel(z_ref, ga_ref, gi_ref, fb_ref, fbi_ref, h_ref, y_ref, s_ref, *, plan):
    zs = z_ref.at[0]
    ys = y_ref.at[0]
    n_slabs = zs.shape[0]
    half = FFT_N2
    _fft_stage_a(zs, ga_ref, s_ref, plan, plan.N1 // 2)

    def body_b(k1, carry):
        x = jnp.dot(fb_ref[...], _fft_load_k1(s_ref, k1, plan), preferred_element_type=F32)
        h = h_ref[k1].astype(F32)
        xr, xi, hr, hi = x[:half], x[half:], h[:half], h[half:]
        y = jnp.concatenate([xr * hr - xi * hi, xr * hi + xi * hr], axis=0).astype(BF16)
        b = jnp.dot(fbi_ref[...], y, preferred_element_type=F32)
        for s in range(n_slabs):
            s_ref[s, pl.ds(k1, FFT_N2, stride=plan.PA), :] = b[:half, s * LANES:(s + 1) * LANES]
            s_ref[s, pl.ds(plan.KP + k1, FFT_N2, stride=plan.PA), :] = b[half:, s * LANES:(s + 1) * LANES]
        return carry
    lax.fori_loop(0, plan.KH, body_b, 0, unroll=3)

    def body_a(n2, carry):
        b = jnp.concatenate([s_ref[s, pl.ds(n2 * plan.PA, 2 * plan.KP), :] for s in range(n_slabs)], axis=-1)
        y = jnp.dot(gi_ref[n2], b.astype(BF16), preferred_element_type=F32)
        for s in range(n_slabs):
            ys[s, pl.ds(n2, plan.N1 // 2, stride=FFT_N2), :] = y[:, s * LANES:(s + 1) * LANES]
        return carry
    lax.fori_loop(0, FFT_N2, body_a, 0, unroll=FFT_UNROLL)


def fft_filter_spectrum(h_fwd, h_bwd, tabs):
    L, C = h_fwd.shape
    plan = _FftPlan(L)
    n_slabs = C // LANES
    k = jnp.concatenate([h_fwd, jnp.zeros((1, C), F32), h_bwd[1:][::-1]], axis=0)
    k = k.reshape(plan.N, n_slabs, LANES).transpose(1, 0, 2)
    full = lambda shape: pl.BlockSpec(shape, lambda i: (0,) * len(shape))
    return pl.pallas_call(
        functools.partial(_fft_filter_kernel, plan=plan),
        grid=(1,),
        in_specs=[full((n_slabs, plan.N, LANES)), full((FFT_N2, 2 * plan.KP, plan.N1)),
                  full((2 * FFT_N2, 2 * FFT_N2))],
        out_specs=full((plan.KH, 2 * FFT_N2, C)),
        out_shape=jax.ShapeDtypeStruct((plan.KH, 2 * FFT_N2, C), BF16),
        scratch_shapes=[pltpu.VMEM((n_slabs, FFT_N2 * plan.PA, LANES), F32)],
        compiler_params=_cparams("arbitrary"),
        name="hyena_filter_fft",
    )(k, tabs["ga_full"], tabs["fb"])


def fft_long_conv(z, h_spec, tabs):
    B, n_slabs, L, _ = z.shape
    plan = _FftPlan(L)
    C = n_slabs * LANES
    full = lambda shape: pl.BlockSpec(shape, lambda b: (0,) * len(shape))
    seq = pl.BlockSpec((1, n_slabs, L, LANES), lambda b: (b, 0, 0, 0))
    return pl.pallas_call(
        functools.partial(_fft_conv_kernel, plan=plan),
        grid=(B,),
        in_specs=[seq, full((FFT_N2, 2 * plan.KP, plan.N1 // 2)), full((FFT_N2, plan.N1 // 2, 2 * plan.KP)),
                  full((2 * FFT_N2, 2 * FFT_N2)), full((2 * FFT_N2, 2 * FFT_N2)),
                  full((plan.KH, 2 * FFT_N2, C))],
        out_specs=seq,
        out_shape=jax.ShapeDtypeStruct((B, n_slabs, L, LANES), F32),
        scratch_shapes=[pltpu.VMEM((n_slabs, FFT_N2 * plan.PA, LANES), F32)],
        compiler_params=_cparams("parallel"),
        name="hyena_fft_conv",
    )(z, tabs["ga_half"], tabs["ga_inv"], tabs["fb"], tabs["fb_inv"], h_spec)


def _hyena_pre_slab_kernel(u_ref, w_ref, b_ref, z_ref, upad):
    L = u_ref.shape[1]
    T = _time_chunk(L)
    C = D_GROUP
    _fill_padded(upad, u_ref, L, T)

    def body(j, carry):
        base = pl.multiple_of(j * T, T)
        for s in range(C // LANES):
            x1 = _dw_conv_slab(upad, C // LANES + s, base, T, w_ref, b_ref, C + s * LANES, HY_SHORT, 1)
            v = _dw_conv_slab(upad, 2 * C // LANES + s, base, T, w_ref, b_ref, 2 * C + s * LANES, HY_SHORT, 1)
            z_ref[0, s, pl.ds(base, T), :] = x1 * v
        return carry
    lax.fori_loop(0, L // T, body, 0)


def _hyena_post_slab_kernel(u_ref, y_ref, w_ref, b_ref, bias_ref, o_ref, upad):
    L = u_ref.shape[1]
    T = _time_chunk(L)
    C = D_GROUP
    _fill_padded(upad, u_ref, L, T)

    def body(j, carry):
        base = pl.multiple_of(j * T, T)
        x0 = _short_conv(upad, base, T, w_ref, b_ref, 0, C)
        x1 = _short_conv(upad, base, T, w_ref, b_ref, C, 2 * C)
        v = _short_conv(upad, base, T, w_ref, b_ref, 2 * C, 3 * C)
        y = jnp.concatenate([y_ref[0, s, pl.ds(base, T), :] for s in range(C // LANES)], axis=-1)
        o_ref[0, pl.ds(base, T), :] = x0 * (y + (x1 * v) * bias_ref[...])
        return carry
    lax.fori_loop(0, L // T, body, 0)


def hyena_mixer_fft(u, lp, tabs):
    B, L, _ = u.shape
    C = D_GROUP
    n_slabs = C // LANES
    w, bsh = lp["hy_short_w"], lp["hy_short_b"].reshape(1, 3 * C)
    useq = pl.BlockSpec((1, L, 3 * C), lambda b: (b, 0, 0))
    slabs = pl.BlockSpec((1, n_slabs, L, LANES), lambda b: (b, 0, 0, 0))
    wspec = pl.BlockSpec((HY_SHORT, 3 * C), lambda b: (0, 0))
    bspec = pl.BlockSpec((1, 3 * C), lambda b: (0, 0))
    pad_scratch = pltpu.VMEM((3 * C // LANES, L + 2 * CONV_MARGIN, LANES), F32)
    z = pl.pallas_call(
        _hyena_pre_slab_kernel,
        grid=(B,),
        in_specs=[useq, wspec, bspec],
        out_specs=slabs,
        out_shape=jax.ShapeDtypeStruct((B, n_slabs, L, LANES), F32),
        scratch_shapes=[pad_scratch],
        compiler_params=_cparams("parallel"),
        name="hyena_pre",
    )(u, w, bsh)
    h_fwd, h_bwd = _hyena_filters(L, lp)
    y = fft_long_conv(z, fft_filter_spectrum(h_fwd, h_bwd, tabs), tabs)
    return pl.pallas_call(
        _hyena_post_slab_kernel,
        grid=(B,),
        in_specs=[useq, slabs, wspec, bspec, pl.BlockSpec((1, C), lambda b: (0, 0))],
        out_specs=pl.BlockSpec((1, L, C), lambda b: (b, 0, 0)),
        out_shape=jax.ShapeDtypeStruct((B, L, C), F32),
        scratch_shapes=[pad_scratch],
        compiler_params=_cparams("parallel"),
        name="hyena_post",
    )(u, y, w, bsh, lp["hy_bias"].reshape(1, C))


def _hyena_filters(L, lp):
    t = jnp.linspace(0.0, 1.0, L, dtype=F32)[:, None]
    bands = (HY_EMB - 1) // 2
    w = 2.0 * math.pi * jnp.arange(L, dtype=F32)[:, None] / L
    f = jnp.linspace(1e-4, bands - 1, bands, dtype=F32)[None]
    z = jnp.concatenate([t, jnp.cos(f * w), -jnp.sin(f * w)], axis=-1)
    hdn = jnp.sin(z @ lp["hy_ffn_w1"] + lp["hy_ffn_b1"])
    hdn = jnp.sin(hdn @ lp["hy_ffn_w2"] + lp["hy_ffn_b2"])
    h = (hdn @ lp["hy_ffn_w3"]).reshape(L, 2, D_GROUP)
    max_decay = math.log(HY_TARGET) / HY_FAST_DECAY
    min_decay = math.log(HY_TARGET) / HY_SLOW_DECAY
    deltas = jnp.linspace(min_decay, max_decay, D_GROUP, dtype=F32)
    h = h * jnp.exp(-t * jnp.abs(deltas))[:, None, :]
    h = h / (jnp.sum(jnp.abs(h), axis=(0, 1), keepdims=True) + EPS)
    return h[:, 0], h[:, 1]


def _layer(hc, hx, c_silu_all, lp, need_ctx, final_g, final_norm, tables_x, tables_c):
    B, S, D = hx.shape
    C = hc.shape[1]
    mod = small_linear(c_silu_all, lp["ada_w"], lp["ada_b"])
    mod_x = mod[:B].reshape(B, 6, 1, D)
    mod_c = jnp.broadcast_to(mod[B].reshape(1, 6, 1, D), (B, 6, 1, D))
    w_ext = extend_w_in(lp["w_in"])
    cos_x, sin_x = rope_tables(S, True)
    cos_c, sin_c = rope_tables(C, False)
    hy_x, cf_x, at_x, lr_x = in_proj(hx, mod_x[:, 0], mod_x[:, 1], lp["norm1_g"], w_ext, cos_x, sin_x, tm=512)
    hy_c, cf_c, at_c, lr_c = in_proj(hc, mod_c[:, 0], mod_c[:, 1], lp["norm1_g"], w_ext, cos_c, sin_c, tm=256)

    yd_c, yd_x = rglru_mixer(lr_c, lr_x, lp, need_ctx)
    conf = lambda u: conformer_conv(u, lp["conf_dw_w"], lp["conf_dw_b"], lp["conf_ln_g"], lp["conf_ln_b"])
    ys_x = [hyena_mixer_fft(hy_x, lp, tables_x), conf(cf_x),
            window_attention(at_x, at_c, lp["attn_sink"]), yd_x]

    w_out = lp["w_out"].astype(BF16)
    w_router = jnp.zeros((D, ROUTER_COLS), F32)
    w_router = w_router.at[:, :N_GROUPS].set(lp["router_g_w"]).at[:, N_GROUPS:N_GROUPS + N_EXPERTS].set(lp["router_e_w"])
    w_router = w_router.astype(BF16)
    b_router = jnp.zeros((1, ROUTER_COLS), F32)
    b_router = b_router.at[0, :N_GROUPS].set(lp["router_g_b"]).at[0, N_GROUPS:N_GROUPS + N_EXPERTS].set(lp["router_e_b"])

    hx1, lg_x = out_proj(ys_x, hx, mod_x[:, 2], lp["group_norm_g"], w_out, lp["norm2_g"],
                         mod_x[:, 3], mod_x[:, 4], w_router, b_router, tm=512)
    h_tok = hx1.reshape(B * S, D)
    hc_tok = None
    lg = lg_x.reshape(B * S, ROUTER_COLS)
    if need_ctx:
        ys_c = [hyena_mixer(hy_c, lp, tables_c), conf(cf_c),
                context_attention(at_c, lp["attn_sink"]), yd_c]
        hc1, lg_c = out_proj(ys_c, hc, mod_c[:, 2], lp["group_norm_g"], w_out, lp["norm2_g"],
                             mod_c[:, 3], mod_c[:, 4], w_router, b_router, tm=256)
        hc_tok = hc1.reshape(B * C, D)
        lg = jnp.concatenate([lg, lg_c.reshape(B * C, ROUTER_COLS)], axis=0)

    T = lg.shape[0]
    n_blocks = -(-T // MOE_BLOCK) + N_CLASSES
    info, counts = route_tokens(lg)
    dest, blk_a, blk_b, n_used = slot_plan(info, counts, n_blocks)
    xs = moe_dispatch(h_tok, hc_tok, info, dest, n_blocks, lp["norm2_g"], (mod_x[:, 3], mod_x[:, 4]),
                      (mod_c[:, 3], mod_c[:, 4]), S)
    o_sorted = expert_pairs(xs, blk_a, blk_b, n_used, lp["exp_w_gate"].astype(BF16),
                            lp["exp_w_up"].astype(BF16), lp["exp_w_down"].astype(BF16))
    hx2 = moe_collect(o_sorted, dest[:B * S], h_tok, mod_x[:, 5], S, final_g, final_norm)
    hx2 = hx2.reshape(B, S, D)
    if need_ctx:
        hc2 = moe_collect(o_sorted, dest[B * S:], hc_tok, mod_c[:, 5], C,
                          final_g, False).reshape(B, C, D)
    else:
        hc2 = hc
    return hc2, hx2


def kernel(x, c, ctx, c_ctx, norm1_g, norm2_g, ada_w, ada_b, w_in, hy_short_w, hy_short_b, hy_ffn_w1, hy_ffn_b1, hy_ffn_w2, hy_ffn_b2, hy_ffn_w3, hy_bias, conf_dw_w, conf_dw_b, conf_ln_g, conf_ln_b, attn_sink, lru_conv_w, lru_conv_b, lru_wa, lru_ba, lru_wx, lru_bx, lru_lambda, group_norm_g, w_out, router_g_w, router_g_b, router_e_w, router_e_b, exp_w_gate, exp_w_up, exp_w_down, final_norm_g):
    stacked = dict(norm1_g=norm1_g, norm2_g=norm2_g, ada_w=ada_w, ada_b=ada_b, w_in=w_in,
                   hy_short_w=hy_short_w, hy_short_b=hy_short_b, hy_ffn_w1=hy_ffn_w1, hy_ffn_b1=hy_ffn_b1,
                   hy_ffn_w2=hy_ffn_w2, hy_ffn_b2=hy_ffn_b2, hy_ffn_w3=hy_ffn_w3, hy_bias=hy_bias,
                   conf_dw_w=conf_dw_w, conf_dw_b=conf_dw_b, conf_ln_g=conf_ln_g, conf_ln_b=conf_ln_b,
                   attn_sink=attn_sink, lru_conv_w=lru_conv_w, lru_conv_b=lru_conv_b, lru_wa=lru_wa,
                   lru_ba=lru_ba, lru_wx=lru_wx, lru_bx=lru_bx, lru_lambda=lru_lambda,
                   group_norm_g=group_norm_g, w_out=w_out, router_g_w=router_g_w, router_g_b=router_g_b,
                   router_e_w=router_e_w, router_e_b=router_e_b, exp_w_gate=exp_w_gate,
                   exp_w_up=exp_w_up, exp_w_down=exp_w_down)
    depth = norm1_g.shape[0]
    B = x.shape[0]
    cs = jnp.concatenate([jax.nn.silu(c), jnp.broadcast_to(jax.nn.silu(c_ctx)[None], (8, c.shape[1]))], axis=0)
    hc, hx = ctx, x
    tables_x = fft_tables(x.shape[1])
    tables_c = dft_tables(ctx.shape[1])
    for l in range(depth):
        lp = {k: v[l] for k, v in stacked.items()}
        hc, hx = _layer(hc, hx, cs, lp, need_ctx=(l < depth - 1), final_g=final_norm_g,
                        final_norm=(l == depth - 1), tables_x=tables_x, tables_c=tables_c)
    return hx
```

```python
import functools
import math

import jax
import jax.numpy as jnp
from jax import lax
from jax.experimental import pallas as pl
from jax.experimental.pallas import tpu as pltpu

F32 = jnp.float32
BF16 = jnp.bfloat16

EPS = 1e-6
NEG_INF = -1e30
GRID_W = 64
N_MIXERS = 4
D_GROUP = 256
HY_COLS = 3 * D_GROUP
CONF_COLS = 2 * D_GROUP
ATT_HEADS = 4
ATT_KV_HEADS = 2
HEAD_DIM = 64
ATT_COLS = (ATT_HEADS + 2 * ATT_KV_HEADS) * HEAD_DIM
LRU_COLS = 2 * D_GROUP
QK_COLS = (ATT_HEADS + ATT_KV_HEADS) * HEAD_DIM
WINDOW = 128
ATT_BLOCK = 128
ROPE_BASE = 10000.0
HY_EMB = 33
HY_FAST_DECAY = 0.3
HY_SLOW_DECAY = 1.5
HY_TARGET = 1e-2
CONF_KERNEL = 31
LRU_HEADS = 4
LRU_CONV = 4
LRU_C = 8.0
N_GROUPS = 4
EXP_PER_GROUP = 8
N_EXPERTS = N_GROUPS * EXP_PER_GROUP
TOP_K = 2
MOE_BLOCK = 256
ROUTER_COLS = 128

VMEM_LIMIT_BYTES = 56 * 1024 * 1024


def _cparams(*sem):
    return pltpu.CompilerParams(dimension_semantics=sem, vmem_limit_bytes=VMEM_LIMIT_BYTES)


def _linear_kernel(x_ref, w_ref, b_ref, o_ref):
    o_ref[...] = jnp.dot(x_ref[...], w_ref[...], preferred_element_type=F32,
                         precision=lax.Precision.HIGHEST) + b_ref[...]


def small_linear(x, w, b, tn=1024):
    M, K = x.shape
    N = w.shape[1]
    return pl.pallas_call(
        _linear_kernel,
        grid=(N // tn,),
        in_specs=[pl.BlockSpec((M, K), lambda j: (0, 0)),
                  pl.BlockSpec((K, tn), lambda j: (0, j)),
                  pl.BlockSpec((1, tn), lambda j: (0, j))],
        out_specs=pl.BlockSpec((M, tn), lambda j: (0, j)),
        out_shape=jax.ShapeDtypeStruct((M, N), F32),
        compiler_params=_cparams("parallel"),
        name="ada_linear",
    )(x, w, b.reshape(1, N))


def _in_proj_kernel(x_ref, sh_ref, sc_ref, g_ref, w_ref, cos_ref, sin_ref,
                    hy_ref, cf_ref, at_ref, lr_ref):
    x = x_ref[0]
    ms = jnp.mean(x * x, axis=-1, keepdims=True)
    y = x * lax.rsqrt(ms + EPS) * g_ref[...]
    y = y * (1.0 + sc_ref[0]) + sh_ref[0]
    u = jnp.dot(y.astype(BF16), w_ref[...], preferred_element_type=F32)
    c0 = HY_COLS
    c1 = c0 + CONF_COLS
    c2 = c1 + ATT_COLS
    c3 = c2 + LRU_COLS
    hy_ref[0] = u[:, :c0]
    cf_ref[0] = u[:, c0:c1]
    lr_ref[0] = u[:, c2:c3]
    qk = u[:, c1:c1 + QK_COLS]
    qk_rot = u[:, c3:c3 + QK_COLS]
    at_ref[0, :, :QK_COLS] = qk * cos_ref[...] + qk_rot * sin_ref[...]
    at_ref[0, :, QK_COLS:] = u[:, c1 + QK_COLS:c2]


def in_proj(h, shift, scale, g, w_ext, cos_t, sin_t, tm):
    B, L, D = h.shape
    NW = w_ext.shape[1]
    outs = [HY_COLS, CONF_COLS, ATT_COLS, LRU_COLS]
    return pl.pallas_call(
        _in_proj_kernel,
        grid=(B, L // tm),
        in_specs=[pl.BlockSpec((1, tm, D), lambda b, i: (b, i, 0)),
                  pl.BlockSpec((1, 1, D), lambda b, i: (b, 0, 0)),
                  pl.BlockSpec((1, 1, D), lambda b, i: (b, 0, 0)),
                  pl.BlockSpec((1, D), lambda b, i: (0, 0)),
                  pl.BlockSpec((D, NW), lambda b, i: (0, 0)),
                  pl.BlockSpec((tm, QK_COLS), lambda b, i: (i, 0)),
                  pl.BlockSpec((tm, QK_COLS), lambda b, i: (i, 0))],
        out_specs=[pl.BlockSpec((1, tm, n), lambda b, i: (b, i, 0)) for n in outs],
        out_shape=[jax.ShapeDtypeStruct((B, L, n), F32) for n in outs],
        compiler_params=_cparams("parallel", "parallel"),
        name="in_proj",
    )(h, shift, scale, g.reshape(1, D), w_ext, cos_t, sin_t)


def rope_tables(L, rotary):
    n_heads = ATT_HEADS + ATT_KV_HEADS
    if not rotary:
        return jnp.ones((L, QK_COLS), F32), jnp.zeros((L, QK_COLS), F32)
    pos = jnp.arange(L)
    row = (pos // GRID_W).astype(F32)
    col = (pos % GRID_W).astype(F32)
    half = HEAD_DIM // 2
    inv_freq = ROPE_BASE ** (-jnp.arange(0, half, 2, dtype=F32) / half)
    ang_r = row[:, None] * inv_freq[None]
    ang_c = col[:, None] * inv_freq[None]
    cos_h = jnp.concatenate([jnp.cos(ang_r)] * 2 + [jnp.cos(ang_c)] * 2, axis=-1)
    sin_h = jnp.concatenate([jnp.sin(ang_r)] * 2 + [jnp.sin(ang_c)] * 2, axis=-1)
    return jnp.tile(cos_h, (1, n_heads)), jnp.tile(sin_h, (1, n_heads))


def extend_w_in(w_in):
    c1 = HY_COLS + CONF_COLS
    wqk = w_in[:, c1:c1 + QK_COLS]
    D = w_in.shape[0]
    w4 = wqk.reshape(D, QK_COLS // 32, 2, 16)
    wrot = jnp.stack([-w4[:, :, 1], w4[:, :, 0]], axis=2).reshape(D, QK_COLS)
    return jnp.concatenate([w_in, wrot], axis=1).astype(BF16)


def _softmax_parts(q, k_list, extra_logit):
    scale = HEAD_DIM ** -0.5
    s_list = []
    for k, mask in k_list:
        s = lax.dot_general(q, k, (((1,), (1,)), ((), ())), preferred_element_type=F32) * scale
        if mask is not None:
            s = jnp.where(mask, s, NEG_INF)
        s_list.append(s)
    m = extra_logit
    for s in s_list:
        m = jnp.maximum(m, jnp.max(s, axis=-1, keepdims=True))
    p_list = [jnp.exp(s - m) for s in s_list]
    denom = jnp.exp(extra_logit - m)
    for p in p_list:
        denom = denom + jnp.sum(p, axis=-1, keepdims=True)
    return p_list, 1.0 / denom


ATT_Q_BLOCKS = 4


def _win_attn_kernel(sink_ref, q_ref, kp_ref, kc_ref, kn_ref, vp_ref, vc_ref, vn_ref,
                     kx_ref, vx_ref, o_ref, *, seq_len):
    i = pl.program_id(1)
    blk = ATT_BLOCK
    qb = q_ref.shape[1] // blk
    scale = HEAD_DIM ** -0.5
    g = ATT_HEADS // ATT_KV_HEADS
    kw = jnp.concatenate([kp_ref[0], kc_ref[0], kn_ref[0]], axis=0)
    vw = jnp.concatenate([vp_ref[0], vc_ref[0], vn_ref[0]], axis=0).astype(BF16)
    kwt = kw.T.astype(BF16)
    kxt = kx_ref[0].T.astype(BF16)
    vx = vx_ref[0].astype(BF16)
    row = lax.broadcasted_iota(jnp.int32, (g * blk, 3 * blk), 0) % blk
    col = lax.broadcasted_iota(jnp.int32, (g * blk, 3 * blk), 1)
    in_band = jnp.abs(col - blk - row) <= WINDOW
    for j in range(qb):
        q_blk = i * qb + j
        k_pos = (q_blk - 1) * blk + col
        valid = in_band & (k_pos >= 0) & (k_pos < seq_len)
        outs = []
        for kv in range(ATT_KV_HEADS):
            ksl = slice(kv * HEAD_DIM, (kv + 1) * HEAD_DIM)
            heads = range(kv * g, (kv + 1) * g)
            qs = jnp.concatenate([q_ref[0, j * blk:(j + 1) * blk, h * HEAD_DIM:(h + 1) * HEAD_DIM]
                                  for h in heads], axis=0).astype(BF16)
            sink = jnp.concatenate([jnp.full((blk, 1), sink_ref[h], F32) for h in heads], axis=0)
            s_win = jnp.dot(qs, kwt[ksl, j * blk:(j + 3) * blk], preferred_element_type=F32) * scale
            s_win = jnp.where(valid, s_win, NEG_INF)
            s_ctx = jnp.dot(qs, kxt[ksl, :], preferred_element_type=F32) * scale
            m = jnp.maximum(jnp.maximum(jnp.max(s_win, axis=-1, keepdims=True),
                                        jnp.max(s_ctx, axis=-1, keepdims=True)), sink)
            p_win = jnp.exp(s_win - m)
            p_ctx = jnp.exp(s_ctx - m)
            denom = (jnp.exp(sink - m) + jnp.sum(p_win, axis=-1, keepdims=True)
                     + jnp.sum(p_ctx, axis=-1, keepdims=True))
            o = (jnp.dot(p_win.astype(BF16), vw[j * blk:(j + 3) * blk, ksl], preferred_element_type=F32)
                 + jnp.dot(p_ctx.astype(BF16), vx[:, ksl], preferred_element_type=F32)) * (1.0 / denom)
            outs.extend([o[k * blk:(k + 1) * blk] for k in range(g)])
        o_ref[0, j * blk:(j + 1) * blk, :] = jnp.concatenate(outs, axis=-1)


def window_attention(at_x, at_c, sink):
    B, S, _ = at_x.shape
    C = at_c.shape[1]
    blk = ATT_BLOCK
    qb = ATT_Q_BLOCKS
    nb = S // blk
    kcol = QK_COLS // 128 - 1
    vcol = kcol + 1

    def edge_spec(col, off):
        return pl.BlockSpec((1, blk, 128), lambda b, i, s: (b, jnp.clip(i * qb + off, 0, nb - 1), col))

    def mid_spec(col):
        return pl.BlockSpec((1, qb * blk, 128), lambda b, i, s: (b, i, col))

    grid_spec = pltpu.PrefetchScalarGridSpec(
        num_scalar_prefetch=1,
        grid=(B, nb // qb),
        in_specs=[pl.BlockSpec((1, qb * blk, ATT_HEADS * HEAD_DIM), lambda b, i, s: (b, i, 0)),
                  edge_spec(kcol, -1), mid_spec(kcol), edge_spec(kcol, qb),
                  edge_spec(vcol, -1), mid_spec(vcol), edge_spec(vcol, qb),
                  pl.BlockSpec((1, C, 128), lambda b, i, s: (b, 0, kcol)),
                  pl.BlockSpec((1, C, 128), lambda b, i, s: (b, 0, vcol))],
        out_specs=pl.BlockSpec((1, qb * blk, ATT_HEADS * HEAD_DIM), lambda b, i, s: (b, i, 0)),
    )
    return pl.pallas_call(
        functools.partial(_win_attn_kernel, seq_len=S),
        grid_spec=grid_spec,
        out_shape=jax.ShapeDtypeStruct((B, S, ATT_HEADS * HEAD_DIM), F32),
        compiler_params=_cparams("parallel", "parallel"),
        name="window_attention",
    )(sink.astype(F32), at_x, at_x, at_x, at_x, at_x, at_x, at_x, at_c, at_c)


def _ctx_attn_kernel(sink_ref, q_ref, kx_ref, vx_ref, o_ref):
    q = q_ref[0].astype(BF16)
    kx = kx_ref[0].astype(BF16)
    vx = vx_ref[0].astype(BF16)
    g = ATT_HEADS // ATT_KV_HEADS
    outs = []
    for h in range(ATT_HEADS):
        kv = h // g
        qs = q[:, h * HEAD_DIM:(h + 1) * HEAD_DIM]
        ksl = slice(kv * HEAD_DIM, (kv + 1) * HEAD_DIM)
        (p_ctx,), inv = _softmax_parts(qs, [(kx[:, ksl], None)], sink_ref[h])
        outs.append(jnp.dot(p_ctx.astype(BF16), vx[:, ksl], preferred_element_type=F32) * inv)
    o_ref[0] = jnp.concatenate(outs, axis=-1)


def context_attention(at_c, sink):
    B, C, _ = at_c.shape
    kcol = QK_COLS // 128 - 1
    grid_spec = pltpu.PrefetchScalarGridSpec(
        num_scalar_prefetch=1,
        grid=(B,),
        in_specs=[pl.BlockSpec((1, C, ATT_HEADS * HEAD_DIM), lambda b, s: (b, 0, 0)),
                  pl.BlockSpec((1, C, 128), lambda b, s: (b, 0, kcol)),
                  pl.BlockSpec((1, C, 128), lambda b, s: (b, 0, kcol + 1))],
        out_specs=pl.BlockSpec((1, C, ATT_HEADS * HEAD_DIM), lambda b, s: (b, 0, 0)),
    )
    return pl.pallas_call(
        _ctx_attn_kernel,
        grid_spec=grid_spec,
        out_shape=jax.ShapeDtypeStruct((B, C, ATT_HEADS * HEAD_DIM), F32),
        compiler_params=_cparams("parallel"),
        name="context_attention",
    )(sink.astype(F32), at_c, at_c, at_c)


def _out_proj_kernel(y0_ref, y1_ref, y2_ref, y3_ref, h_ref, g1_ref, gng_ref, w_ref,
                     n2g_ref, sh_ref, sc_ref, wr_ref, br_ref, ho_ref, lg_ref):
    parts = []
    for k, y_ref in enumerate((y0_ref, y1_ref, y2_ref, y3_ref)):
        y = y_ref[0]
        ms = jnp.mean(y * y, axis=-1, keepdims=True)
        yn = y * lax.rsqrt(ms + EPS) * gng_ref[:, k * D_GROUP:(k + 1) * D_GROUP]
        parts.append(yn.astype(BF16))
    yn = jnp.concatenate(parts, axis=-1)
    proj = jnp.dot(yn, w_ref[...], preferred_element_type=F32)
    h = h_ref[0] + g1_ref[0] * proj
    ho_ref[0] = h
    ms = jnp.mean(h * h, axis=-1, keepdims=True)
    n = h * lax.rsqrt(ms + EPS) * n2g_ref[...]
    n = n * (1.0 + sc_ref[0]) + sh_ref[0]
    lg_ref[0] = jnp.dot(n.astype(BF16), wr_ref[...], preferred_element_type=F32) + br_ref[...]


def out_proj(ys, h, g1, gng, w_out, n2g, sh2, sc2, w_router, b_router, tm):
    B, L, D = h.shape
    row3 = lambda n: pl.BlockSpec((1, tm, n), lambda b, i: (b, i, 0))
    mod = pl.BlockSpec((1, 1, D), lambda b, i: (b, 0, 0))
    full = lambda r, c: pl.BlockSpec((r, c), lambda b, i: (0, 0))
    return pl.pallas_call(
        _out_proj_kernel,
        grid=(B, L // tm),
        in_specs=[row3(D_GROUP)] * 4 + [row3(D), mod, full(1, D), full(D, D), full(1, D), mod, mod,
                                        full(D, ROUTER_COLS), full(1, ROUTER_COLS)],
        out_specs=[row3(D), row3(ROUTER_COLS)],
        out_shape=[jax.ShapeDtypeStruct((B, L, D), F32), jax.ShapeDtypeStruct((B, L, ROUTER_COLS), F32)],
        compiler_params=_cparams("parallel", "parallel"),
        name="out_proj",
    )(*ys, h, g1, gng.reshape(1, D), w_out, n2g.reshape(1, D), sh2, sc2, w_router, b_router)


N_PAIRS = EXP_PER_GROUP * (EXP_PER_GROUP - 1) // 2
N_CLASSES = N_GROUPS * N_PAIRS
ROUTE_TOKENS = 512
INFO_CLASS, INFO_RANK, INFO_WA, INFO_WB = 0, 1, 2, 3


def _route_kernel(lg_ref, info_ref, cnt_ref, run):
    i = pl.program_id(0)
    tb = lg_ref.shape[0]

    @pl.when(i == 0)
    def _():
        run[...] = jnp.zeros_like(run)

    lg = lg_ref[...]
    li = lax.broadcasted_iota(jnp.int32, lg.shape, 1)
    big = jnp.int32(ROUTER_COLS)

    def first_argmax(vals):
        m = jnp.max(vals, axis=-1, keepdims=True)
        return m, jnp.min(jnp.where(vals == m, li, big), axis=-1, keepdims=True)

    gl = jnp.where(li < N_GROUPS, lg, NEG_INF)
    gmax, g_idx = first_argmax(gl)
    g_prob = 1.0 / jnp.sum(jnp.exp(gl - gmax), axis=-1, keepdims=True)
    lo = N_GROUPS + EXP_PER_GROUP * g_idx
    el = jnp.where((li >= lo) & (li < lo + EXP_PER_GROUP), lg, NEG_INF)
    m1, i1 = first_argmax(el)
    m2, i2 = first_argmax(jnp.where(li == i1, NEG_INF, el))
    e2 = jnp.exp(m2 - m1)
    w1 = g_prob / (1.0 + e2)
    w2 = g_prob * e2 / (1.0 + e2)
    j1 = i1 - lo
    j2 = i2 - lo
    a = jnp.minimum(j1, j2)
    b = jnp.maximum(j1, j2)
    cls = g_idx * N_PAIRS + ((a * (2 * EXP_PER_GROUP - 1 - a)) >> 1) + (b - a - 1)
    w_a = jnp.where(j1 < j2, w1, w2)
    w_b = jnp.where(j1 < j2, w2, w1)

    hit = li == cls
    onehot = jnp.where(hit, 1.0, 0.0)
    r_i = lax.broadcasted_iota(jnp.int32, (tb, tb), 0)
    c_i = lax.broadcasted_iota(jnp.int32, (tb, tb), 1)
    below = jnp.where(c_i < r_i, 1.0, 0.0).astype(BF16)
    before = jnp.dot(below, onehot.astype(BF16), preferred_element_type=F32)
    rank = jnp.sum(jnp.where(hit, before + run[...], 0.0), axis=-1, keepdims=True)
    run[...] = run[...] + jnp.sum(onehot, axis=0, keepdims=True)
    cnt_ref[...] = run[...]
    info = jnp.where(li == INFO_CLASS, cls.astype(F32), 0.0)
    info = jnp.where(li == INFO_RANK, rank, info)
    info = jnp.where(li == INFO_WA, w_a, info)
    info = jnp.where(li == INFO_WB, w_b, info)
    info_ref[...] = info


def route_tokens(logits):
    T = logits.shape[0]
    tb = ROUTE_TOKENS
    return pl.pallas_call(
        _route_kernel,
        grid=(T // tb,),
        in_specs=[pl.BlockSpec((tb, ROUTER_COLS), lambda i: (i, 0))],
        out_specs=[pl.BlockSpec((tb, ROUTER_COLS), lambda i: (i, 0)),
                   pl.BlockSpec((1, ROUTER_COLS), lambda i: (0, 0))],
        out_shape=[jax.ShapeDtypeStruct((T, ROUTER_COLS), F32), jax.ShapeDtypeStruct((1, ROUTER_COLS), F32)],
        scratch_shapes=[pltpu.VMEM((1, ROUTER_COLS), F32)],
        compiler_params=_cparams("arbitrary"),
        name="moe_route",
    )(logits)


def _pair_tables():
    a_tab, b_tab = [], []
    for g in range(N_GROUPS):
        for a in range(EXP_PER_GROUP):
            for b in range(a + 1, EXP_PER_GROUP):
                a_tab.append(g * EXP_PER_GROUP + a)
                b_tab.append(g * EXP_PER_GROUP + b)
    return jnp.array(a_tab, jnp.int32), jnp.array(b_tab, jnp.int32)


SUBLANES = 8


def _slot_kernel(info_ref, start_ref, dest_ref):
    info = info_ref[...]
    li = lax.broadcasted_iota(jnp.int32, info.shape, 1)
    cls = info[:, INFO_CLASS:INFO_CLASS + 1].astype(jnp.int32)
    start = jnp.sum(jnp.where(li == cls, start_ref[...], 0.0), axis=-1, keepdims=True)
    dest = start + info[:, INFO_RANK:INFO_RANK + 1]
    rec = jnp.where(li == 0, dest, 0.0).T
    dest_ref[0] = rec[:SUBLANES].astype(jnp.int32)


def slot_plan(info, counts, n_blocks):
    T = info.shape[0]
    tb = ROUTE_TOKENS
    cnt = counts[0, :N_CLASSES].astype(jnp.int32)
    padded = (cnt + MOE_BLOCK - 1) // MOE_BLOCK * MOE_BLOCK
    pad_end = jnp.cumsum(padded)
    pad_start = jnp.zeros((1, ROUTER_COLS), F32).at[0, :N_CLASSES].set((pad_end - padded).astype(F32))
    dest = pl.pallas_call(
        _slot_kernel,
        grid=(T // tb,),
        in_specs=[pl.BlockSpec((tb, ROUTER_COLS), lambda i: (i, 0)),
                  pl.BlockSpec((1, ROUTER_COLS), lambda i: (0, 0))],
        out_specs=pl.BlockSpec((1, SUBLANES, tb), lambda i: (i, 0, 0)),
        out_shape=jax.ShapeDtypeStruct((T // tb, SUBLANES, tb), jnp.int32),
        compiler_params=_cparams("parallel"),
        name="moe_slots",
    )(info, pad_start)[:, 0, :].reshape(T)
    n_used = (pad_end[-1] // MOE_BLOCK).astype(jnp.int32).reshape(1)
    blk_cls = jnp.minimum(jnp.searchsorted(pad_end, jnp.arange(n_blocks) * MOE_BLOCK, side="right"),
                          N_CLASSES - 1)
    a_tab, b_tab = _pair_tables()
    return dest, a_tab[blk_cls], b_tab[blk_cls], n_used


DISPATCH_TOKENS = 256


def _wait_rows(buf, sem):
    pltpu.make_async_copy(buf, buf, sem).wait()


DMA_UNROLL = 8
TOKEN_TILE_ROWS = 8
PAIR_ROWS = TOP_K * TOKEN_TILE_ROWS
PAIR_PITCH = PAIR_ROWS + 4


def _store_token_tiles(tiles_ref, offset, pitch, x):
    n = x.shape[0]
    for j in range(x.shape[1] // LANES):
        tiles_ref[pl.ds(offset + j, n, stride=pitch), :] = x[:, j * LANES:(j + 1) * LANES]


def _load_token_tiles(tiles_ref, offset, pitch, n, width):
    return jnp.concatenate([tiles_ref[pl.ds(offset + j, n, stride=pitch), :] for j in range(width // LANES)],
                           axis=-1)


def _dispatch_kernel(dest_ref, hx_ref, hc_ref, g_ref, shx_ref, scx_ref, shc_ref, scc_ref, zeros_hbm,
                     xs_hbm, rows, sems, *, n_latent_blocks):
    del zeros_hbm
    i = pl.program_id(0)
    n = pl.num_programs(0)
    slot = i % 2
    D = hx_ref.shape[1]
    tb = hx_ref.shape[0]

    @pl.when(i >= 2)
    def _():
        _wait_rows(rows.at[slot], sems.at[slot])

    def normed(h_ref, sh_ref, sc_ref):
        h = h_ref[...]
        ms = jnp.mean(h * h, axis=-1, keepdims=True)
        return h * lax.rsqrt(ms + EPS) * g_ref[...] * (1.0 + sc_ref[0]) + sh_ref[0]

    @pl.when(i < n_latent_blocks)
    def _():
        _store_token_tiles(rows.at[slot], 0, TOKEN_TILE_ROWS, normed(hx_ref, shx_ref, scx_ref))

    @pl.when(i >= n_latent_blocks)
    def _():
        _store_token_tiles(rows.at[slot], 0, TOKEN_TILE_ROWS, normed(hc_ref, shc_ref, scc_ref))

    def body(r, carry):
        dst = pl.multiple_of(dest_ref[0, 0, r] * TOKEN_TILE_ROWS, TOKEN_TILE_ROWS)
        pltpu.make_async_copy(rows.at[slot, pl.ds(r * TOKEN_TILE_ROWS, TOKEN_TILE_ROWS)],
                              xs_hbm.at[pl.ds(dst, TOKEN_TILE_ROWS)], sems.at[slot]).start()
        return carry
    lax.fori_loop(0, tb, body, 0, unroll=DMA_UNROLL)

    @pl.when(i == n - 1)
    def _():
        _wait_rows(rows.at[slot], sems.at[slot])

        @pl.when(n >= 2)
        def _():
            _wait_rows(rows.at[1 - slot], sems.at[1 - slot])


def moe_dispatch(h_x, h_c, dest, n_blocks, n2g, mod_x, mod_c, tokens_per_batch):
    Tx, D = h_x.shape
    tb = DISPATCH_TOKENS
    nxb = Tx // tb
    if h_c is None:
        h_c, mod_c, ncb = h_x, mod_x, 0
    else:
        ncb = h_c.shape[0] // tb
    per_b = tokens_per_batch // tb
    P = n_blocks * MOE_BLOCK
    tile_rows = D // LANES
    assert tile_rows == TOKEN_TILE_ROWS
    xi = lambda i: jnp.minimum(i, nxb - 1)
    ci = lambda i: jnp.maximum(i - nxb, 0)
    modx = pl.BlockSpec((1, 1, D), lambda i: (xi(i) // per_b, 0, 0))
    modc = pl.BlockSpec((1, 1, D), lambda i: (0, 0, 0))
    return pl.pallas_call(
        functools.partial(_dispatch_kernel, n_latent_blocks=nxb),
        grid=(nxb + ncb,),
        in_specs=[pl.BlockSpec((1, 1, tb), lambda i: (i, 0, 0), memory_space=pltpu.SMEM),
                  pl.BlockSpec((tb, D), lambda i: (xi(i), 0)),
                  pl.BlockSpec((tb, D), lambda i: (ci(i), 0)),
                  pl.BlockSpec((1, D), lambda i: (0, 0)),
                  modx, modx, modc, modc,
                  pl.BlockSpec(memory_space=pl.ANY)],
        out_specs=pl.BlockSpec(memory_space=pl.ANY),
        out_shape=jax.ShapeDtypeStruct((P * tile_rows, LANES), F32),
        scratch_shapes=[pltpu.VMEM((2, tb * tile_rows, LANES), F32), pltpu.SemaphoreType.DMA((2,))],
        input_output_aliases={8: 0},
        compiler_params=_cparams("arbitrary"),
        name="moe_dispatch",
    )(dest.reshape(-1, 1, tb), h_x, h_c, n2g.reshape(1, D), mod_x[0], mod_x[1], mod_c[0], mod_c[1],
      jnp.zeros((P * tile_rows, LANES), F32))


def _expert_pair_kernel(ea_ref, eb_ref, nused_ref, xs_ref, wga_ref, wua_ref, wda_ref, wgb_ref, wub_ref, wdb_ref,
                        o_ref):
    del ea_ref, eb_ref
    i = pl.program_id(0)
    D = wga_ref.shape[1]

    @pl.when(i < nused_ref[0])
    def _():
        xb = _load_token_tiles(xs_ref, 0, TOKEN_TILE_ROWS, MOE_BLOCK, D).astype(BF16)
        for e, (wg_ref, wu_ref, wd_ref) in enumerate(((wga_ref, wua_ref, wda_ref), (wgb_ref, wub_ref, wdb_ref))):
            gate = jnp.dot(xb, wg_ref[0], preferred_element_type=F32)
            up = jnp.dot(xb, wu_ref[0], preferred_element_type=F32)
            hid = (gate * jax.nn.sigmoid(gate) * up).astype(BF16)
            out = jnp.dot(hid, wd_ref[0], preferred_element_type=F32)
            _store_token_tiles(o_ref, e * TOKEN_TILE_ROWS, PAIR_PITCH, out)
        for j in range(PAIR_ROWS, PAIR_PITCH):
            o_ref[pl.ds(j, MOE_BLOCK, stride=PAIR_PITCH), :] = jnp.zeros((MOE_BLOCK, LANES), F32)

    @pl.when(i >= nused_ref[0])
    def _():
        o_ref[...] = jnp.zeros_like(o_ref)


def expert_pairs(xs, blk_a, blk_b, n_used, w_gate, w_up, w_down):
    D, DE = w_gate.shape[1:]
    P = xs.shape[0] // TOKEN_TILE_ROWS
    n_blocks = P // MOE_BLOCK
    wspec = lambda shape, which: pl.BlockSpec(shape, lambda i, ea, eb, nu: ((ea, eb)[which][i], 0, 0))
    grid_spec = pltpu.PrefetchScalarGridSpec(
        num_scalar_prefetch=3,
        grid=(n_blocks,),
        in_specs=[pl.BlockSpec((MOE_BLOCK * TOKEN_TILE_ROWS, LANES), lambda i, ea, eb, nu: (i, 0)),
                  wspec((1, D, DE), 0), wspec((1, D, DE), 0), wspec((1, DE, D), 0),
                  wspec((1, D, DE), 1), wspec((1, D, DE), 1), wspec((1, DE, D), 1)],
        out_specs=pl.BlockSpec((MOE_BLOCK * PAIR_PITCH, LANES), lambda i, ea, eb, nu: (i, 0)),
    )
    return pl.pallas_call(
        _expert_pair_kernel,
        grid_spec=grid_spec,
        out_shape=jax.ShapeDtypeStruct((P * PAIR_PITCH, LANES), F32),
        compiler_params=_cparams("arbitrary"),
        name="moe_experts",
    )(blk_a, blk_b, n_used, xs, w_gate, w_up, w_down, w_gate, w_up, w_down)


def _gather_pairs(idx_ref, src_hbm, buf, sem, n_tokens):
    def body(r, carry):
        src = idx_ref[0, 0, r] * PAIR_PITCH
        pltpu.make_async_copy(src_hbm.at[pl.ds(src, PAIR_ROWS)], buf.at[pl.ds(r * PAIR_PITCH, PAIR_ROWS)], sem).start()
        return carry
    lax.fori_loop(0, n_tokens, body, 0, unroll=DMA_UNROLL)


def _collect_kernel(dest_ref, dest_next_ref, o_hbm, info_ref, h_ref, g2_ref, fg_ref, out_ref, obuf, sems, *,
                    final_norm):
    i = pl.program_id(0)
    n = pl.num_programs(0)
    slot = i % 2
    tb, D = h_ref.shape

    @pl.when(i == 0)
    def _():
        _gather_pairs(dest_ref, o_hbm, obuf.at[0], sems.at[0], tb)

    @pl.when(i + 1 < n)
    def _():
        _gather_pairs(dest_next_ref, o_hbm, obuf.at[1 - slot], sems.at[1 - slot], tb)

    _wait_rows(obuf.at[slot, pl.ds(0, tb * PAIR_ROWS)], sems.at[slot])
    w_a = info_ref[:, INFO_WA:INFO_WA + 1]
    w_b = info_ref[:, INFO_WB:INFO_WB + 1]
    m = (w_a * _load_token_tiles(obuf.at[slot], 0, PAIR_PITCH, tb, D)
         + w_b * _load_token_tiles(obuf.at[slot], TOKEN_TILE_ROWS, PAIR_PITCH, tb, D))
    h = h_ref[...] + g2_ref[0] * m
    if final_norm:
        ms = jnp.mean(h * h, axis=-1, keepdims=True)
        h = h * lax.rsqrt(ms + EPS) * fg_ref[...]
    out_ref[...] = h


def moe_collect(o_sorted, dest, info, block_offset, h_tokens, g2, tokens_per_batch, final_g, final_norm):
    T, D = h_tokens.shape
    tb = DISPATCH_TOKENS
    nt = T // tb
    per_b = tokens_per_batch // tb
    last = block_offset + nt - 1
    dest3 = dest.reshape(-1, 1, tb)
    return pl.pallas_call(
        functools.partial(_collect_kernel, final_norm=final_norm),
        grid=(nt,),
        in_specs=[pl.BlockSpec((1, 1, tb), lambda i: (block_offset + i, 0, 0), memory_space=pltpu.SMEM),
                  pl.BlockSpec((1, 1, tb), lambda i: (jnp.minimum(block_offset + i + 1, last), 0, 0),
                               memory_space=pltpu.SMEM),
                  pl.BlockSpec(memory_space=pl.ANY),
                  pl.BlockSpec((tb, ROUTER_COLS), lambda i: (block_offset + i, 0)),
                  pl.BlockSpec((tb, D), lambda i: (i, 0)),
                  pl.BlockSpec((1, 1, D), lambda i: (i // per_b, 0, 0)),
                  pl.BlockSpec((1, D), lambda i: (0, 0))],
        out_specs=pl.BlockSpec((tb, D), lambda i: (i, 0)),
        out_shape=jax.ShapeDtypeStruct((T, D), F32),
        scratch_shapes=[pltpu.VMEM((2, tb * PAIR_PITCH, LANES), F32), pltpu.SemaphoreType.DMA((2,))],
        compiler_params=_cparams("arbitrary"),
        name="moe_collect",
    )(dest3, dest3, o_sorted, info, h_tokens, g2, final_g.reshape(1, D))


CONV_MARGIN = 16


def _time_chunk(L):
    return min(L, 256)


LANES = 128


def _zero_margins(pad_ref, L):
    zeros = jnp.zeros((CONV_MARGIN, LANES), F32)
    for s in range(pad_ref.shape[0]):
        pad_ref[s, pl.ds(0, CONV_MARGIN), :] = zeros
        pad_ref[s, pl.ds(CONV_MARGIN + L, CONV_MARGIN), :] = zeros


def _dw_conv_slab(pad_ref, s, base, T, w_ref, b_ref, col, taps, pad_left):
    acc = jnp.broadcast_to(b_ref[:, col:col + LANES], (T, LANES))
    for k in range(taps):
        acc = acc + w_ref[k:k + 1, col:col + LANES] * pad_ref[s, pl.ds(base + (CONV_MARGIN - pad_left + k), T), :]
    return acc


def _conformer_kernel(u_ref, w_ref, b_ref, g_ref, beta_ref, o_ref, ypad):
    L = o_ref.shape[1]
    T = _time_chunk(L)
    C = D_GROUP
    n_slabs = C // LANES
    pad = (CONF_KERNEL - 1) // 2
    _zero_margins(ypad, L)

    def glu(j, carry):
        base = pl.multiple_of(j * T, T)
        for s in range(n_slabs):
            a = u_ref[0, pl.ds(base, T), s * LANES:(s + 1) * LANES]
            gate = u_ref[0, pl.ds(base, T), C + s * LANES:C + (s + 1) * LANES]
            ypad[s, pl.ds(CONV_MARGIN + base, T), :] = a * jax.nn.sigmoid(gate)
        return carry
    lax.fori_loop(0, L // T, glu, 0)

    def conv(j, carry):
        base = pl.multiple_of(j * T, T)
        acc = jnp.concatenate([_dw_conv_slab(ypad, s, base, T, w_ref, b_ref, s * LANES, CONF_KERNEL, pad)
                               for s in range(n_slabs)], axis=-1)
        mu = jnp.mean(acc, axis=-1, keepdims=True)
        cen = acc - mu
        var = jnp.mean(cen * cen, axis=-1, keepdims=True)
        y = cen * lax.rsqrt(var + EPS) * g_ref[...] + beta_ref[...]
        o_ref[0, pl.ds(base, T), :] = y * jax.nn.sigmoid(y)
        return carry
    lax.fori_loop(0, L // T, conv, 0)


def conformer_conv(u, w, b, ln_g, ln_b):
    B, L, _ = u.shape
    C = D_GROUP
    vec = pl.BlockSpec((1, C), lambda i: (0, 0))
    return pl.pallas_call(
        _conformer_kernel,
        grid=(B,),
        in_specs=[pl.BlockSpec((1, L, 2 * C), lambda i: (i, 0, 0)),
                  pl.BlockSpec((CONF_KERNEL, C), lambda i: (0, 0)), vec, vec, vec],
        out_specs=pl.BlockSpec((1, L, C), lambda i: (i, 0, 0)),
        out_shape=jax.ShapeDtypeStruct((B, L, C), F32),
        scratch_shapes=[pltpu.VMEM((C // LANES, L + 2 * CONV_MARGIN, LANES), F32)],
        compiler_params=_cparams("parallel"),
        name="conformer_conv",
    )(u, w, b.reshape(1, C), ln_g.reshape(1, C), ln_b.reshape(1, C))


def _gelu_tanh(x):
    return 0.5 * x * (1.0 + jnp.tanh(math.sqrt(2.0 / math.pi) * (x + 0.044715 * (x * x * x))))


def _lru_kernel(uc_ref, ux_ref, cw_ref, cb_ref, wcat_ref, bcat_ref, lam_ref, *rest, need_ctx):
    if need_ctx:
        oc_ref, ox_ref, cpad, xpad, a_s, b_s, yx, yc = rest
    else:
        ox_ref, cpad, xpad, a_s, b_s, yx = rest
        oc_ref = yc = None
    C = D_GROUP
    n_slabs = C // LANES
    Lc = uc_ref.shape[1]
    Lx = ux_ref.shape[1]
    pad_l = (LRU_CONV - 1) // 2

    def fill(pad_ref, u_ref, L):
        T = _time_chunk(L)
        _zero_margins(pad_ref, L)

        def body(j, carry):
            base = pl.multiple_of(j * T, T)
            for s in range(n_slabs):
                pad_ref[s, pl.ds(CONV_MARGIN + base, T), :] = u_ref[0, pl.ds(base, T),
                                                                    C + s * LANES:C + (s + 1) * LANES]
            return carry
        lax.fori_loop(0, L // T, body, 0)

    fill(cpad, uc_ref, Lc)
    fill(xpad, ux_ref, Lx)

    def coeffs(pad_ref, base, T, d):
        x = jnp.concatenate([_dw_conv_slab(pad_ref, s, base, T, cw_ref, cb_ref, s * LANES, LRU_CONV, pad_l)
                             for s in range(n_slabs)], axis=-1)
        g = jnp.dot(x.astype(BF16), wcat_ref[:, 2 * d * C:2 * (d + 1) * C],
                    preferred_element_type=F32) + bcat_ref[:, 2 * d * C:2 * (d + 1) * C]
        r = jax.nn.sigmoid(g[:, :C])
        i = jax.nn.sigmoid(g[:, C:])
        z = -lam_ref[d:d + 1, :]
        softplus = jnp.maximum(z, 0.0) + jnp.log(1.0 + jnp.exp(-jnp.abs(z)))
        a = jnp.exp(-LRU_C * r * softplus)
        b = jnp.sqrt(1.0 - a * a) * (i * x)
        for s in range(n_slabs):
            a_s[d * n_slabs + s, pl.ds(0, T), :] = a[:, s * LANES:(s + 1) * LANES]
            b_s[d * n_slabs + s, pl.ds(0, T), :] = b[:, s * LANES:(s + 1) * LANES]

    def run(pad_ref, L, h, y_ref):
        T = _time_chunk(L)
        n = L // T

        def chunk(j, h):
            base_f = pl.multiple_of(j * T, T)
            base_b = pl.multiple_of((n - 1 - j) * T, T)
            coeffs(pad_ref, base_f, T, 0)
            coeffs(pad_ref, base_b, T, 1)

            def step(t, h):
                new = []
                for d, (base, row) in enumerate(((base_f, t), (base_b, T - 1 - t))):
                    for s in range(n_slabs):
                        k = d * n_slabs + s
                        hs = a_s[k, pl.ds(row, 1), :] * h[k] + b_s[k, pl.ds(row, 1), :]
                        if y_ref is not None:
                            y_ref[k, pl.ds(base + row, 1), :] = hs
                        new.append(hs)
                return tuple(new)
            return lax.fori_loop(0, T, step, h, unroll=8)
        return lax.fori_loop(0, n, chunk, h)

    h = tuple(jnp.zeros((1, LANES), F32) for _ in range(2 * n_slabs))
    h = run(cpad, Lc, h, yc)
    run(xpad, Lx, h, yx)

    def finish(u_ref, y_ref, o_ref, L):
        T = _time_chunk(L)

        def body(j, carry):
            base = pl.multiple_of(j * T, T)
            y = jnp.concatenate([y_ref[s, pl.ds(base, T), :] + y_ref[n_slabs + s, pl.ds(base, T), :]
                                 for s in range(n_slabs)], axis=-1)
            o_ref[0, pl.ds(base, T), :] = _gelu_tanh(u_ref[0, pl.ds(base, T), :C]) * y
            return carry
        lax.fori_loop(0, L // T, body, 0)

    finish(ux_ref, yx, ox_ref, Lx)
    if need_ctx:
        finish(uc_ref, yc, oc_ref, Lc)


def _block_diag(w):
    H, n, _ = w.shape
    eye = jnp.eye(H, dtype=w.dtype)
    return (eye[:, None, :, None] * w[:, :, None, :]).reshape(H * n, H * n)


def rglru_mixer(uc, ux, lp, need_ctx):
    B, Lc, _ = uc.shape
    Lx = ux.shape[1]
    C = D_GROUP
    wcat = jnp.concatenate([_block_diag(lp["lru_wa"][0]), _block_diag(lp["lru_wx"][0]),
                            _block_diag(lp["lru_wa"][1]), _block_diag(lp["lru_wx"][1])], axis=1).astype(BF16)
    bcat = jnp.concatenate([lp["lru_ba"][0], lp["lru_bx"][0], lp["lru_ba"][1], lp["lru_bx"][1]]).reshape(1, 4 * C)
    full = lambda r, c: pl.BlockSpec((r, c), lambda i: (0, 0))
    seq = lambda L, n: pl.BlockSpec((1, L, n), lambda i: (i, 0, 0))
    out_specs = [seq(Lx, C)]
    out_shape = [jax.ShapeDtypeStruct((B, Lx, C), F32)]
    if need_ctx:
        out_specs = [seq(Lc, C)] + out_specs
        out_shape = [jax.ShapeDtypeStruct((B, Lc, C), F32)] + out_shape
    T = _time_chunk(Lx)
    slab = lambda rows, n=1: pltpu.VMEM((n * C // LANES, rows, LANES), F32)
    scratch = [slab(Lc + 2 * CONV_MARGIN), slab(Lx + 2 * CONV_MARGIN), slab(T, 2), slab(T, 2), slab(Lx, 2)]
    if need_ctx:
        scratch.append(slab(Lc, 2))
    res = pl.pallas_call(
        functools.partial(_lru_kernel, need_ctx=need_ctx),
        grid=(B,),
        in_specs=[seq(Lc, 2 * C), seq(Lx, 2 * C), full(LRU_CONV, C), full(1, C), full(C, 4 * C),
                  full(1, 4 * C), full(2, C)],
        out_specs=out_specs,
        out_shape=out_shape,
        scratch_shapes=scratch,
        compiler_params=_cparams("parallel"),
        name="rglru",
    )(uc, ux, lp["lru_conv_w"], lp["lru_conv_b"].reshape(1, C), wcat, bcat, lp["lru_lambda"])
    if need_ctx:
        return res[0], res[1]
    return None, res[0]


HY_SHORT = 3


def _short_conv(pad_ref, base, T, w_ref, b_ref, c0, c1):
    return jnp.concatenate([_dw_conv_slab(pad_ref, col // LANES, base, T, w_ref, b_ref, col, HY_SHORT, 1)
                            for col in range(c0, c1, LANES)], axis=-1)


def _fill_padded(pad_ref, u_ref, L, T):
    _zero_margins(pad_ref, L)

    def body(j, carry):
        base = pl.multiple_of(j * T, T)
        for s in range(pad_ref.shape[0]):
            pad_ref[s, pl.ds(CONV_MARGIN + base, T), :] = u_ref[0, pl.ds(base, T), s * LANES:(s + 1) * LANES]
        return carry
    lax.fori_loop(0, L // T, body, 0)


def _hyena_pre_kernel(u_ref, w_ref, b_ref, z_ref, upad):
    L = u_ref.shape[1]
    T = _time_chunk(L)
    C = D_GROUP
    _fill_padded(upad, u_ref, L, T)

    def body(j, carry):
        base = pl.multiple_of(j * T, T)
        x1 = _short_conv(upad, base, T, w_ref, b_ref, C, 2 * C)
        v = _short_conv(upad, base, T, w_ref, b_ref, 2 * C, 3 * C)
        z_ref[pl.ds(base, T), :] = (x1 * v).astype(BF16)
        return carry
    lax.fori_loop(0, L // T, body, 0)


def _hyena_post_kernel(u_ref, y_ref, w_ref, b_ref, bias_ref, o_ref, upad):
    L = u_ref.shape[1]
    T = _time_chunk(L)
    C = D_GROUP
    _fill_padded(upad, u_ref, L, T)

    def body(j, carry):
        base = pl.multiple_of(j * T, T)
        x0 = _short_conv(upad, base, T, w_ref, b_ref, 0, C)
        x1 = _short_conv(upad, base, T, w_ref, b_ref, C, 2 * C)
        v = _short_conv(upad, base, T, w_ref, b_ref, 2 * C, 3 * C)
        o_ref[0, pl.ds(base, T), :] = x0 * (y_ref[pl.ds(base, T), :] + (x1 * v) * bias_ref[...])
        return carry
    lax.fori_loop(0, L // T, body, 0)


def _spectrum_kernel(f_ref, z_ref, ha_ref, hb_ref, hc_ref, y_ref):
    tf = ha_ref.shape[0]
    acc = jnp.dot(f_ref[...], z_ref[...], preferred_element_type=F32)
    zr = acc[:tf]
    zi = acc[tf:]
    y_ref[:tf, :] = (zr * ha_ref[...] - zi * hb_ref[...]).astype(BF16)
    y_ref[tf:, :] = (zr * hb_ref[...] + zi * hc_ref[...]).astype(BF16)


def _idft_kernel(f_ref, y_ref, o_ref):
    o_ref[...] = jnp.dot(f_ref[...], y_ref[...], preferred_element_type=F32)


def dft_tables(L):
    N = 2 * L
    tf = min(256, L)
    k = jnp.arange(L, dtype=jnp.int32)
    n = jnp.arange(L, dtype=jnp.int32)
    ang = (2.0 * math.pi / N) * ((k[:, None] * n[None, :]) % N).astype(F32)
    cos = jnp.cos(ang)
    sin = jnp.sin(ang)
    nyq = jnp.where(n % 2 == 0, 1.0, -1.0).astype(F32)
    f_re = cos
    f_im = (-sin).at[0].set(nyq)
    fwd = jnp.stack([f_re.reshape(L // tf, tf, L), f_im.reshape(L // tf, tf, L)], axis=1).reshape(N, L)
    ck = jnp.where(k == 0, 1.0, 2.0).astype(F32)[:, None] / N
    i_re = cos * ck
    i_im = (-sin * ck).at[0].set(nyq / N)
    inv = jnp.stack([i_re.reshape(L // tf, tf, L), i_im.reshape(L // tf, tf, L)], axis=1).reshape(N, L).T
    return fwd.astype(BF16), inv.astype(BF16)


def filter_spectrum(h_fwd, h_bwd):
    L, C = h_fwd.shape
    k = jnp.concatenate([h_fwd, jnp.zeros((1, C), F32), h_bwd[1:][::-1]], axis=0)
    hf = jnp.fft.rfft(k, axis=0)
    hr = jnp.real(hf)
    hi = jnp.imag(hf)
    a = hr[:L]
    b = hi[:L].at[0].set(0.0)
    c = hr[:L].at[0].set(hr[L])
    return a, b, c


def hyena_mixer(u, lp, tables):
    B, L, _ = u.shape
    C = D_GROUP
    N = 2 * L
    fwd, inv = tables
    tf = min(256, L)
    T = _time_chunk(L)
    w, bsh = lp["hy_short_w"], lp["hy_short_b"].reshape(1, 3 * C)
    z2 = pl.pallas_call(
        _hyena_pre_kernel,
        grid=(B,),
        in_specs=[pl.BlockSpec((1, L, 3 * C), lambda b: (b, 0, 0)),
                  pl.BlockSpec((HY_SHORT, 3 * C), lambda b: (0, 0)),
                  pl.BlockSpec((1, 3 * C), lambda b: (0, 0))],
        out_specs=pl.BlockSpec((L, C), lambda b: (0, b)),
        out_shape=jax.ShapeDtypeStruct((L, B * C), BF16),
        scratch_shapes=[pltpu.VMEM((3 * C // LANES, L + 2 * CONV_MARGIN, LANES), F32)],
        compiler_params=_cparams("parallel"),
        name="hyena_pre",
    )(u, w, bsh)

    h_fwd, h_bwd = _hyena_filters(L, lp)
    tn = 2 * C
    ha, hb, hc = [jnp.tile(t, (1, tn // C)) for t in filter_spectrum(h_fwd, h_bwd)]
    hspec = pl.BlockSpec((tf, tn), lambda i, j: (i, 0))
    y2 = pl.pallas_call(
        _spectrum_kernel,
        grid=(L // tf, B * C // tn),
        in_specs=[pl.BlockSpec((2 * tf, L), lambda i, j: (i, 0)),
                  pl.BlockSpec((L, tn), lambda i, j: (0, j)), hspec, hspec, hspec],
        out_specs=pl.BlockSpec((2 * tf, tn), lambda i, j: (i, j)),
        out_shape=jax.ShapeDtypeStruct((N, B * C), BF16),
        compiler_params=_cparams("parallel", "parallel"),
        name="hyena_spectrum",
    )(fwd, z2, ha, hb, hc)

    tl = min(256, L)
    yt = pl.pallas_call(
        _idft_kernel,
        grid=(L // tl, B * C // tn),
        in_specs=[pl.BlockSpec((tl, N), lambda i, j: (i, 0)),
                  pl.BlockSpec((N, tn), lambda i, j: (0, j))],
        out_specs=pl.BlockSpec((tl, tn), lambda i, j: (i, j)),
        out_shape=jax.ShapeDtypeStruct((L, B * C), F32),
        compiler_params=_cparams("parallel", "parallel"),
        name="hyena_idft",
    )(inv, y2)

    return pl.pallas_call(
        _hyena_post_kernel,
        grid=(B,),
        in_specs=[pl.BlockSpec((1, L, 3 * C), lambda b: (b, 0, 0)),
                  pl.BlockSpec((L, C), lambda b: (0, b)),
                  pl.BlockSpec((HY_SHORT, 3 * C), lambda b: (0, 0)),
                  pl.BlockSpec((1, 3 * C), lambda b: (0, 0)),
                  pl.BlockSpec((1, C), lambda b: (0, 0))],
        out_specs=pl.BlockSpec((1, L, C), lambda b: (b, 0, 0)),
        out_shape=jax.ShapeDtypeStruct((B, L, C), F32),
        scratch_shapes=[pltpu.VMEM((3 * C // LANES, L + 2 * CONV_MARGIN, LANES), F32)],
        compiler_params=_cparams("parallel"),
        name="hyena_post",
    )(u, yt, w, bsh, lp["hy_bias"].reshape(1, C))


FFT_N2 = 128
FFT_UNROLL = 8


class _FftPlan:
    def __init__(self, L):
        self.L = L
        self.N = 2 * L
        self.N1 = self.N // FFT_N2
        self.KH = self.N1 // 2 + 1
        self.KP = -(-self.KH // 8) * 8
        self.PA = 2 * self.KP + 4


def fft_tables(L):
    p = _FftPlan(L)
    N, N1, KH, KP = p.N, p.N1, p.KH, p.KP
    n2 = jnp.arange(FFT_N2, dtype=jnp.int32)
    k1 = jnp.arange(KP, dtype=jnp.int32)
    n1 = jnp.arange(N1, dtype=jnp.int32)
    n = FFT_N2 * n1[None, None, :] + n2[:, None, None]
    ang = (2.0 * math.pi / N) * ((k1[None, :, None] * n) % N).astype(F32)
    keep = (k1 < KH)[None, :, None]
    g_re = jnp.where(keep, jnp.cos(ang), 0.0)
    g_im = jnp.where(keep, -jnp.sin(ang), 0.0)
    ga_full = jnp.concatenate([g_re, g_im], axis=1)
    ck = jnp.where((k1 == 0) | (k1 == N1 // 2), 1.0, 2.0) / N
    ga_inv = jnp.swapaxes(ga_full[:, :, :N1 // 2] * jnp.tile(ck, 2)[None, :, None], 1, 2)
    kk = jnp.arange(FFT_N2, dtype=jnp.int32)
    ang2 = (2.0 * math.pi / FFT_N2) * ((kk[:, None] * kk[None, :]) % FFT_N2).astype(F32)
    fr, fi = jnp.cos(ang2), -jnp.sin(ang2)
    fb = jnp.block([[fr, -fi], [fi, fr]])
    fb_inv = jnp.block([[fr, fi], [-fi, fr]])
    return dict(ga_half=ga_full[:, :, :N1 // 2].astype(BF16), ga_full=ga_full.astype(BF16),
                ga_inv=ga_inv.astype(BF16), fb=fb.astype(BF16), fb_inv=fb_inv.astype(BF16))


def _fft_stage_a(x_ref, ga_ref, s_ref, plan, n1_count):
    n_slabs = x_ref.shape[0]

    def body(n2, carry):
        xs = jnp.concatenate([x_ref[s, pl.ds(n2, n1_count, stride=FFT_N2), :] for s in range(n_slabs)], axis=-1)
        a = jnp.dot(ga_ref[n2], xs.astype(BF16), preferred_element_type=F32)
        for s in range(n_slabs):
            s_ref[s, pl.ds(n2 * plan.PA, 2 * plan.KP), :] = a[:, s * LANES:(s + 1) * LANES]
        return carry
    lax.fori_loop(0, FFT_N2, body, 0, unroll=FFT_UNROLL)


def _fft_load_k1(s_ref, k1, plan):
    n_slabs = s_ref.shape[0]
    re = jnp.concatenate([s_ref[s, pl.ds(k1, FFT_N2, stride=plan.PA), :] for s in range(n_slabs)], axis=-1)
    im = jnp.concatenate([s_ref[s, pl.ds(plan.KP + k1, FFT_N2, stride=plan.PA), :] for s in range(n_slabs)], axis=-1)
    return jnp.concatenate([re, im], axis=0).astype(BF16)


def _fft_filter_kernel(k_ref, ga_ref, fb_ref, h_ref, s_ref, *, plan):
    _fft_stage_a(k_ref, ga_ref, s_ref, plan, plan.N1)

    def body(k1, carry):
        h_ref[k1] = jnp.dot(fb_ref[...], _fft_load_k1(s_ref, k1, plan), preferred_element_type=F32).astype(BF16)
        return carry
    lax.fori_loop(0, plan.KH, body, 0)


def _fft_conv_kernel(z_ref, ga_ref, gi_ref, fb_ref, fbi_ref, h_ref, y_ref, s_ref, *, plan):
    zs = z_ref.at[0]
    ys = y_ref.at[0]
    n_slabs = zs.shape[0]
    half = FFT_N2
    _fft_stage_a(zs, ga_ref, s_ref, plan, plan.N1 // 2)

    def body_b(k1, carry):
        x = jnp.dot(fb_ref[...], _fft_load_k1(s_ref, k1, plan), preferred_element_type=F32)
        h = h_ref[k1].astype(F32)
        xr, xi, hr, hi = x[:half], x[half:], h[:half], h[half:]
        y = jnp.concatenate([xr * hr - xi * hi, xr * hi + xi * hr], axis=0).astype(BF16)
        b = jnp.dot(fbi_ref[...], y, preferred_element_type=F32)
        for s in range(n_slabs):
            s_ref[s, pl.ds(k1, FFT_N2, stride=plan.PA), :] = b[:half, s * LANES:(s + 1) * LANES]
            s_ref[s, pl.ds(plan.KP + k1, FFT_N2, stride=plan.PA), :] = b[half:, s * LANES:(s + 1) * LANES]
        return carry
    lax.fori_loop(0, plan.KH, body_b, 0, unroll=3)

    def body_a(n2, carry):
        b = jnp.concatenate([s_ref[s, pl.ds(n2 * plan.PA, 2 * plan.KP), :] for s in range(n_slabs)], axis=-1)
        y = jnp.dot(gi_ref[n2], b.astype(BF16), preferred_element_type=F32)
        for s in range(n_slabs):
            ys[s, pl.ds(n2, plan.N1 // 2, stride=FFT_N2), :] = y[:, s * LANES:(s + 1) * LANES]
        return carry
    lax.fori_loop(0, FFT_N2, body_a, 0, unroll=FFT_UNROLL)


def fft_filter_spectrum(h_fwd, h_bwd, tabs):
    L, C = h_fwd.shape
    plan = _FftPlan(L)
    n_slabs = C // LANES
    k = jnp.concatenate([h_fwd, jnp.zeros((1, C), F32), h_bwd[1:][::-1]], axis=0)
    k = k.reshape(plan.N, n_slabs, LANES).transpose(1, 0, 2)
    full = lambda shape: pl.BlockSpec(shape, lambda i: (0,) * len(shape))
    return pl.pallas_call(
        functools.partial(_fft_filter_kernel, plan=plan),
        grid=(1,),
        in_specs=[full((n_slabs, plan.N, LANES)), full((FFT_N2, 2 * plan.KP, plan.N1)),
                  full((2 * FFT_N2, 2 * FFT_N2))],
        out_specs=full((plan.KH, 2 * FFT_N2, C)),
        out_shape=jax.ShapeDtypeStruct((plan.KH, 2 * FFT_N2, C), BF16),
        scratch_shapes=[pltpu.VMEM((n_slabs, FFT_N2 * plan.PA, LANES), F32)],
        compiler_params=_cparams("arbitrary"),
        name="hyena_filter_fft",
    )(k, tabs["ga_full"], tabs["fb"])


def fft_long_conv(z, h_spec, tabs):
    B, n_slabs, L, _ = z.shape
    plan = _FftPlan(L)
    C = n_slabs * LANES
    full = lambda shape: pl.BlockSpec(shape, lambda b: (0,) * len(shape))
    seq = pl.BlockSpec((1, n_slabs, L, LANES), lambda b: (b, 0, 0, 0))
    return pl.pallas_call(
        functools.partial(_fft_conv_kernel, plan=plan),
        grid=(B,),
        in_specs=[seq, full((FFT_N2, 2 * plan.KP, plan.N1 // 2)), full((FFT_N2, plan.N1 // 2, 2 * plan.KP)),
                  full((2 * FFT_N2, 2 * FFT_N2)), full((2 * FFT_N2, 2 * FFT_N2)),
                  full((plan.KH, 2 * FFT_N2, C))],
        out_specs=seq,
        out_shape=jax.ShapeDtypeStruct((B, n_slabs, L, LANES), F32),
        scratch_shapes=[pltpu.VMEM((n_slabs, FFT_N2 * plan.PA, LANES), F32)],
        compiler_params=_cparams("parallel"),
        name="hyena_fft_conv",
    )(z, tabs["ga_half"], tabs["ga_inv"], tabs["fb"], tabs["fb_inv"], h_spec)


def _hyena_pre_slab_kernel(u_ref, w_ref, b_ref, z_ref, upad):
    L = u_ref.shape[1]
    T = _time_chunk(L)
    C = D_GROUP
    _fill_padded(upad, u_ref, L, T)

    def body(j, carry):
        base = pl.multiple_of(j * T, T)
        for s in range(C // LANES):
            x1 = _dw_conv_slab(upad, C // LANES + s, base, T, w_ref, b_ref, C + s * LANES, HY_SHORT, 1)
            v = _dw_conv_slab(upad, 2 * C // LANES + s, base, T, w_ref, b_ref, 2 * C + s * LANES, HY_SHORT, 1)
            z_ref[0, s, pl.ds(base, T), :] = x1 * v
        return carry
    lax.fori_loop(0, L // T, body, 0)


def _hyena_post_slab_kernel(u_ref, y_ref, w_ref, b_ref, bias_ref, o_ref, upad):
    L = u_ref.shape[1]
    T = _time_chunk(L)
    C = D_GROUP
    _fill_padded(upad, u_ref, L, T)

    def body(j, carry):
        base = pl.multiple_of(j * T, T)
        x0 = _short_conv(upad, base, T, w_ref, b_ref, 0, C)
        x1 = _short_conv(upad, base, T, w_ref, b_ref, C, 2 * C)
        v = _short_conv(upad, base, T, w_ref, b_ref, 2 * C, 3 * C)
        y = jnp.concatenate([y_ref[0, s, pl.ds(base, T), :] for s in range(C // LANES)], axis=-1)
        o_ref[0, pl.ds(base, T), :] = x0 * (y + (x1 * v) * bias_ref[...])
        return carry
    lax.fori_loop(0, L // T, body, 0)


def hyena_mixer_fft(u, lp, tabs):
    B, L, _ = u.shape
    C = D_GROUP
    n_slabs = C // LANES
    w, bsh = lp["hy_short_w"], lp["hy_short_b"].reshape(1, 3 * C)
    useq = pl.BlockSpec((1, L, 3 * C), lambda b: (b, 0, 0))
    slabs = pl.BlockSpec((1, n_slabs, L, LANES), lambda b: (b, 0, 0, 0))
    wspec = pl.BlockSpec((HY_SHORT, 3 * C), lambda b: (0, 0))
    bspec = pl.BlockSpec((1, 3 * C), lambda b: (0, 0))
    pad_scratch = pltpu.VMEM((3 * C // LANES, L + 2 * CONV_MARGIN, LANES), F32)
    z = pl.pallas_call(
        _hyena_pre_slab_kernel,
        grid=(B,),
        in_specs=[useq, wspec, bspec],
        out_specs=slabs,
        out_shape=jax.ShapeDtypeStruct((B, n_slabs, L, LANES), F32),
        scratch_shapes=[pad_scratch],
        compiler_params=_cparams("parallel"),
        name="hyena_pre",
    )(u, w, bsh)
    h_fwd, h_bwd = _hyena_filters(L, lp)
    y = fft_long_conv(z, fft_filter_spectrum(h_fwd, h_bwd, tabs), tabs)
    return pl.pallas_call(
        _hyena_post_slab_kernel,
        grid=(B,),
        in_specs=[useq, slabs, wspec, bspec, pl.BlockSpec((1, C), lambda b: (0, 0))],
        out_specs=pl.BlockSpec((1, L, C), lambda b: (b, 0, 0)),
        out_shape=jax.ShapeDtypeStruct((B, L, C), F32),
        scratch_shapes=[pad_scratch],
        compiler_params=_cparams("parallel"),
        name="hyena_post",
    )(u, y, w, bsh, lp["hy_bias"].reshape(1, C))


def _hyena_filters(L, lp):
    t = jnp.linspace(0.0, 1.0, L, dtype=F32)[:, None]
    bands = (HY_EMB - 1) // 2
    w = 2.0 * math.pi * jnp.arange(L, dtype=F32)[:, None] / L
    f = jnp.linspace(1e-4, bands - 1, bands, dtype=F32)[None]
    z = jnp.concatenate([t, jnp.cos(f * w), -jnp.sin(f * w)], axis=-1)
    hdn = jnp.sin(z @ lp["hy_ffn_w1"] + lp["hy_ffn_b1"])
    hdn = jnp.sin(hdn @ lp["hy_ffn_w2"] + lp["hy_ffn_b2"])
    h = (hdn @ lp["hy_ffn_w3"]).reshape(L, 2, D_GROUP)
    max_decay = math.log(HY_TARGET) / HY_FAST_DECAY
    min_decay = math.log(HY_TARGET) / HY_SLOW_DECAY
    deltas = jnp.linspace(min_decay, max_decay, D_GROUP, dtype=F32)
    h = h * jnp.exp(-t * jnp.abs(deltas))[:, None, :]
    h = h / (jnp.sum(jnp.abs(h), axis=(0, 1), keepdims=True) + EPS)
    return h[:, 0], h[:, 1]


def _layer(hc, hx, c_silu_all, lp, need_ctx, final_g, final_norm, tables_x, tables_c):
    B, S, D = hx.shape
    C = hc.shape[1]
    mod = small_linear(c_silu_all, lp["ada_w"], lp["ada_b"])
    mod_x = mod[:B].reshape(B, 6, 1, D)
    mod_c = jnp.broadcast_to(mod[B].reshape(1, 6, 1, D), (B, 6, 1, D))
    w_ext = extend_w_in(lp["w_in"])
    cos_x, sin_x = rope_tables(S, True)
    cos_c, sin_c = rope_tables(C, False)
    hy_x, cf_x, at_x, lr_x = in_proj(hx, mod_x[:, 0], mod_x[:, 1], lp["norm1_g"], w_ext, cos_x, sin_x, tm=512)
    hy_c, cf_c, at_c, lr_c = in_proj(hc, mod_c[:, 0], mod_c[:, 1], lp["norm1_g"], w_ext, cos_c, sin_c, tm=256)

    yd_c, yd_x = rglru_mixer(lr_c, lr_x, lp, need_ctx)
    conf = lambda u: conformer_conv(u, lp["conf_dw_w"], lp["conf_dw_b"], lp["conf_ln_g"], lp["conf_ln_b"])
    ys_x = [hyena_mixer_fft(hy_x, lp, tables_x), conf(cf_x),
            window_attention(at_x, at_c, lp["attn_sink"]), yd_x]

    w_out = lp["w_out"].astype(BF16)
    w_router = jnp.zeros((D, ROUTER_COLS), F32)
    w_router = w_router.at[:, :N_GROUPS].set(lp["router_g_w"]).at[:, N_GROUPS:N_GROUPS + N_EXPERTS].set(lp["router_e_w"])
    w_router = w_router.astype(BF16)
    b_router = jnp.zeros((1, ROUTER_COLS), F32)
    b_router = b_router.at[0, :N_GROUPS].set(lp["router_g_b"]).at[0, N_GROUPS:N_GROUPS + N_EXPERTS].set(lp["router_e_b"])

    hx1, lg_x = out_proj(ys_x, hx, mod_x[:, 2], lp["group_norm_g"], w_out, lp["norm2_g"],
                         mod_x[:, 3], mod_x[:, 4], w_router, b_router, tm=512)
    h_tok = hx1.reshape(B * S, D)
    hc_tok = None
    lg = lg_x.reshape(B * S, ROUTER_COLS)
    if need_ctx:
        ys_c = [hyena_mixer(hy_c, lp, tables_c), conf(cf_c),
                context_attention(at_c, lp["attn_sink"]), yd_c]
        hc1, lg_c = out_proj(ys_c, hc, mod_c[:, 2], lp["group_norm_g"], w_out, lp["norm2_g"],
                             mod_c[:, 3], mod_c[:, 4], w_router, b_router, tm=256)
        hc_tok = hc1.reshape(B * C, D)
        lg = jnp.concatenate([lg, lg_c.reshape(B * C, ROUTER_COLS)], axis=0)

    T = lg.shape[0]
    n_blocks = -(-T // MOE_BLOCK) + N_CLASSES
    info, counts = route_tokens(lg)
    dest, blk_a, blk_b, n_used = slot_plan(info, counts, n_blocks)
    xs = moe_dispatch(h_tok, hc_tok, dest, n_blocks, lp["norm2_g"], (mod_x[:, 3], mod_x[:, 4]),
                      (mod_c[:, 3], mod_c[:, 4]), S)
    o_sorted = expert_pairs(xs, blk_a, blk_b, n_used, lp["exp_w_gate"].astype(BF16),
                            lp["exp_w_up"].astype(BF16), lp["exp_w_down"].astype(BF16))
    hx2 = moe_collect(o_sorted, dest, info, 0, h_tok, mod_x[:, 5], S, final_g, final_norm)
    hx2 = hx2.reshape(B, S, D)
    if need_ctx:
        hc2 = moe_collect(o_sorted, dest, info, B * S // DISPATCH_TOKENS, hc_tok, mod_c[:, 5], C,
                          final_g, False).reshape(B, C, D)
    else:
        hc2 = hc
    return hc2, hx2


def kernel(x, c, ctx, c_ctx, norm1_g, norm2_g, ada_w, ada_b, w_in, hy_short_w, hy_short_b, hy_ffn_w1, hy_ffn_b1, hy_ffn_w2, hy_ffn_b2, hy_ffn_w3, hy_bias, conf_dw_w, conf_dw_b, conf_ln_g, conf_ln_b, attn_sink, lru_conv_w, lru_conv_b, lru_wa, lru_ba, lru_wx, lru_bx, lru_lambda, group_norm_g, w_out, router_g_w, router_g_b, router_e_w, router_e_b, exp_w_gate, exp_w_up, exp_w_down, final_norm_g):
    stacked = dict(norm1_g=norm1_g, norm2_g=norm2_g, ada_w=ada_w, ada_b=ada_b, w_in=w_in,
                   hy_short_w=hy_short_w, hy_short_b=hy_short_b, hy_ffn_w1=hy_ffn_w1, hy_ffn_b1=hy_ffn_b1,
                   hy_ffn_w2=hy_ffn_w2, hy_ffn_b2=hy_ffn_b2, hy_ffn_w3=hy_ffn_w3, hy_bias=hy_bias,
                   conf_dw_w=conf_dw_w, conf_dw_b=conf_dw_b, conf_ln_g=conf_ln_g, conf_ln_b=conf_ln_b,
                   attn_sink=attn_sink, lru_conv_w=lru_conv_w, lru_conv_b=lru_conv_b, lru_wa=lru_wa,
                   lru_ba=lru_ba, lru_wx=lru_wx, lru_bx=lru_bx, lru_lambda=lru_lambda,
                   group_norm_g=group_norm_g, w_out=w_out, router_g_w=router_g_w, router_g_b=router_g_b,
                   router_e_w=router_e_w, router_e_b=router_e_b, exp_w_gate=exp_w_gate,
                   exp_w_up=exp_w_up, exp_w_down=exp_w_down)
    depth = norm1_g.shape[0]
    B = x.shape[0]
    cs = jnp.concatenate([jax.nn.silu(c), jnp.broadcast_to(jax.nn.silu(c_ctx)[None], (8, c.shape[1]))], axis=0)
    hc, hx = ctx, x
    tables_x = fft_tables(x.shape[1])
    tables_c = dft_tables(ctx.shape[1])
    for l in range(depth):
        lp = {k: v[l] for k, v in stacked.items()}
        hc, hx = _layer(hc, hx, cs, lp, need_ctx=(l < depth - 1), final_g=final_norm_g,
                        final_norm=(l == depth - 1), tables_x=tables_x, tables_c=tables_c)
    return hx
```

```python
import functools
import math

import jax
import jax.numpy as jnp
from jax import lax
from jax.experimental import pallas as pl
from jax.experimental.pallas import tpu as pltpu

F32 = jnp.float32
BF16 = jnp.bfloat16

EPS = 1e-6
NEG_INF = -1e30
GRID_W = 64
N_MIXERS = 4
D_GROUP = 256
HY_COLS = 3 * D_GROUP
CONF_COLS = 2 * D_GROUP
ATT_HEADS = 4
ATT_KV_HEADS = 2
HEAD_DIM = 64
ATT_COLS = (ATT_HEADS + 2 * ATT_KV_HEADS) * HEAD_DIM
LRU_COLS = 2 * D_GROUP
QK_COLS = (ATT_HEADS + ATT_KV_HEADS) * HEAD_DIM
WINDOW = 128
ATT_BLOCK = 128
ROPE_BASE = 10000.0
HY_EMB = 33
HY_FAST_DECAY = 0.3
HY_SLOW_DECAY = 1.5
HY_TARGET = 1e-2
CONF_KERNEL = 31
LRU_HEADS = 4
LRU_CONV = 4
LRU_C = 8.0
N_GROUPS = 4
EXP_PER_GROUP = 8
N_EXPERTS = N_GROUPS * EXP_PER_GROUP
TOP_K = 2
MOE_BLOCK = 256
ROUTER_COLS = 128

VMEM_LIMIT_BYTES = 56 * 1024 * 1024


def _cparams(*sem):
    return pltpu.CompilerParams(dimension_semantics=sem, vmem_limit_bytes=VMEM_LIMIT_BYTES)


def _linear_kernel(x_ref, w_ref, b_ref, o_ref):
    o_ref[...] = jnp.dot(x_ref[...], w_ref[...], preferred_element_type=F32,
                         precision=lax.Precision.HIGHEST) + b_ref[...]


def small_linear(x, w, b, tn=1024):
    M, K = x.shape
    N = w.shape[1]
    return pl.pallas_call(
        _linear_kernel,
        grid=(N // tn,),
        in_specs=[pl.BlockSpec((M, K), lambda j: (0, 0)),
                  pl.BlockSpec((K, tn), lambda j: (0, j)),
                  pl.BlockSpec((1, tn), lambda j: (0, j))],
        out_specs=pl.BlockSpec((M, tn), lambda j: (0, j)),
        out_shape=jax.ShapeDtypeStruct((M, N), F32),
        compiler_params=_cparams("parallel"),
        name="ada_linear",
    )(x, w, b.reshape(1, N))


def _in_proj_kernel(x_ref, sh_ref, sc_ref, g_ref, w_ref, cos_ref, sin_ref,
                    hy_ref, cf_ref, at_ref, lr_ref):
    x = x_ref[0]
    ms = jnp.mean(x * x, axis=-1, keepdims=True)
    y = x * lax.rsqrt(ms + EPS) * g_ref[...]
    y = y * (1.0 + sc_ref[0]) + sh_ref[0]
    u = jnp.dot(y.astype(BF16), w_ref[...], preferred_element_type=F32)
    c0 = HY_COLS
    c1 = c0 + CONF_COLS
    c2 = c1 + ATT_COLS
    c3 = c2 + LRU_COLS
    hy_ref[0] = u[:, :c0]
    cf_ref[0] = u[:, c0:c1]
    lr_ref[0] = u[:, c2:c3]
    qk = u[:, c1:c1 + QK_COLS]
    qk_rot = u[:, c3:c3 + QK_COLS]
    at_ref[0, :, :QK_COLS] = qk * cos_ref[...] + qk_rot * sin_ref[...]
    at_ref[0, :, QK_COLS:] = u[:, c1 + QK_COLS:c2]


def in_proj(h, shift, scale, g, w_ext, cos_t, sin_t, tm):
    B, L, D = h.shape
    NW = w_ext.shape[1]
    outs = [HY_COLS, CONF_COLS, ATT_COLS, LRU_COLS]
    return pl.pallas_call(
        _in_proj_kernel,
        grid=(B, L // tm),
        in_specs=[pl.BlockSpec((1, tm, D), lambda b, i: (b, i, 0)),
                  pl.BlockSpec((1, 1, D), lambda b, i: (b, 0, 0)),
                  pl.BlockSpec((1, 1, D), lambda b, i: (b, 0, 0)),
                  pl.BlockSpec((1, D), lambda b, i: (0, 0)),
                  pl.BlockSpec((D, NW), lambda b, i: (0, 0)),
                  pl.BlockSpec((tm, QK_COLS), lambda b, i: (i, 0)),
                  pl.BlockSpec((tm, QK_COLS), lambda b, i: (i, 0))],
        out_specs=[pl.BlockSpec((1, tm, n), lambda b, i: (b, i, 0)) for n in outs],
        out_shape=[jax.ShapeDtypeStruct((B, L, n), F32) for n in outs],
        compiler_params=_cparams("parallel", "parallel"),
        name="in_proj",
    )(h, shift, scale, g.reshape(1, D), w_ext, cos_t, sin_t)


def rope_tables(L, rotary):
    n_heads = ATT_HEADS + ATT_KV_HEADS
    if not rotary:
        return jnp.ones((L, QK_COLS), F32), jnp.zeros((L, QK_COLS), F32)
    pos = jnp.arange(L)
    row = (pos // GRID_W).astype(F32)
    col = (pos % GRID_W).astype(F32)
    half = HEAD_DIM // 2
    inv_freq = ROPE_BASE ** (-jnp.arange(0, half, 2, dtype=F32) / half)
    ang_r = row[:, None] * inv_freq[None]
    ang_c = col[:, None] * inv_freq[None]
    cos_h = jnp.concatenate([jnp.cos(ang_r)] * 2 + [jnp.cos(ang_c)] * 2, axis=-1)
    sin_h = jnp.concatenate([jnp.sin(ang_r)] * 2 + [jnp.sin(ang_c)] * 2, axis=-1)
    return jnp.tile(cos_h, (1, n_heads)), jnp.tile(sin_h, (1, n_heads))


def extend_w_in(w_in):
    c1 = HY_COLS + CONF_COLS
    wqk = w_in[:, c1:c1 + QK_COLS]
    D = w_in.shape[0]
    w4 = wqk.reshape(D, QK_COLS // 32, 2, 16)
    wrot = jnp.stack([-w4[:, :, 1], w4[:, :, 0]], axis=2).reshape(D, QK_COLS)
    return jnp.concatenate([w_in, wrot], axis=1).astype(BF16)


def _softmax_parts(q, k_list, extra_logit):
    scale = HEAD_DIM ** -0.5
    s_list = []
    for k, mask in k_list:
        s = lax.dot_general(q, k, (((1,), (1,)), ((), ())), preferred_element_type=F32) * scale
        if mask is not None:
            s = jnp.where(mask, s, NEG_INF)
        s_list.append(s)
    m = extra_logit
    for s in s_list:
        m = jnp.maximum(m, jnp.max(s, axis=-1, keepdims=True))
    p_list = [jnp.exp(s - m) for s in s_list]
    denom = jnp.exp(extra_logit - m)
    for p in p_list:
        denom = denom + jnp.sum(p, axis=-1, keepdims=True)
    return p_list, 1.0 / denom


ATT_Q_BLOCKS = 4


def _win_attn_kernel(sink_ref, q_ref, kp_ref, kc_ref, kn_ref, vp_ref, vc_ref, vn_ref,
                     kx_ref, vx_ref, o_ref, *, seq_len):
    i = pl.program_id(1)
    blk = ATT_BLOCK
    qb = q_ref.shape[1] // blk
    scale = HEAD_DIM ** -0.5
    g = ATT_HEADS // ATT_KV_HEADS
    kw = jnp.concatenate([kp_ref[0], kc_ref[0], kn_ref[0]], axis=0)
    vw = jnp.concatenate([vp_ref[0], vc_ref[0], vn_ref[0]], axis=0).astype(BF16)
    kwt = kw.T.astype(BF16)
    kxt = kx_ref[0].T.astype(BF16)
    vx = vx_ref[0].astype(BF16)
    row = lax.broadcasted_iota(jnp.int32, (g * blk, 3 * blk), 0) % blk
    col = lax.broadcasted_iota(jnp.int32, (g * blk, 3 * blk), 1)
    band_bias = jnp.where(jnp.abs(col - blk - row) <= WINDOW, 0.0, NEG_INF)
    col1 = lax.broadcasted_iota(jnp.int32, (1, 3 * blk), 1)
    for j in range(qb):
        q_blk = i * qb + j
        k_pos = (q_blk - 1) * blk + col1
        edge_bias = jnp.where(k_pos >= 0, jnp.where(k_pos < seq_len, 0.0, NEG_INF), NEG_INF)
        bias = band_bias + edge_bias
        outs = []
        for kv in range(ATT_KV_HEADS):
            ksl = slice(kv * HEAD_DIM, (kv + 1) * HEAD_DIM)
            heads = range(kv * g, (kv + 1) * g)
            qs = (jnp.concatenate([q_ref[0, j * blk:(j + 1) * blk, h * HEAD_DIM:(h + 1) * HEAD_DIM]
                                   for h in heads], axis=0) * scale).astype(BF16)
            sink = jnp.concatenate([jnp.full((blk, 1), sink_ref[h], F32) for h in heads], axis=0)
            s_win = jnp.dot(qs, kwt[ksl, j * blk:(j + 3) * blk], preferred_element_type=F32) + bias
            s_ctx = jnp.dot(qs, kxt[ksl, :], preferred_element_type=F32)
            m = jnp.maximum(jnp.maximum(jnp.max(s_win, axis=-1, keepdims=True),
                                        jnp.max(s_ctx, axis=-1, keepdims=True)), sink)
            p_win = jnp.exp(s_win - m)
            p_ctx = jnp.exp(s_ctx - m)
            denom = (jnp.exp(sink - m) + jnp.sum(p_win, axis=-1, keepdims=True)
                     + jnp.sum(p_ctx, axis=-1, keepdims=True))
            o = (jnp.dot(p_win.astype(BF16), vw[j * blk:(j + 3) * blk, ksl], preferred_element_type=F32)
                 + jnp.dot(p_ctx.astype(BF16), vx[:, ksl], preferred_element_type=F32)) * (1.0 / denom)
            outs.extend([o[k * blk:(k + 1) * blk] for k in range(g)])
        o_ref[0, j * blk:(j + 1) * blk, :] = jnp.concatenate(outs, axis=-1)


def window_attention(at_x, at_c, sink):
    B, S, _ = at_x.shape
    C = at_c.shape[1]
    blk = ATT_BLOCK
    qb = ATT_Q_BLOCKS
    nb = S // blk
    kcol = QK_COLS // 128 - 1
    vcol = kcol + 1

    def edge_spec(col, off):
        return pl.BlockSpec((1, blk, 128), lambda b, i, s: (b, jnp.clip(i * qb + off, 0, nb - 1), col))

    def mid_spec(col):
        return pl.BlockSpec((1, qb * blk, 128), lambda b, i, s: (b, i, col))

    grid_spec = pltpu.PrefetchScalarGridSpec(
        num_scalar_prefetch=1,
        grid=(B, nb // qb),
        in_specs=[pl.BlockSpec((1, qb * blk, ATT_HEADS * HEAD_DIM), lambda b, i, s: (b, i, 0)),
                  edge_spec(kcol, -1), mid_spec(kcol), edge_spec(kcol, qb),
                  edge_spec(vcol, -1), mid_spec(vcol), edge_spec(vcol, qb),
                  pl.BlockSpec((1, C, 128), lambda b, i, s: (b, 0, kcol)),
                  pl.BlockSpec((1, C, 128), lambda b, i, s: (b, 0, vcol))],
        out_specs=pl.BlockSpec((1, qb * blk, ATT_HEADS * HEAD_DIM), lambda b, i, s: (b, i, 0)),
    )
    return pl.pallas_call(
        functools.partial(_win_attn_kernel, seq_len=S),
        grid_spec=grid_spec,
        out_shape=jax.ShapeDtypeStruct((B, S, ATT_HEADS * HEAD_DIM), F32),
        compiler_params=_cparams("parallel", "parallel"),
        name="window_attention",
    )(sink.astype(F32), at_x, at_x, at_x, at_x, at_x, at_x, at_x, at_c, at_c)


def _ctx_attn_kernel(sink_ref, q_ref, kx_ref, vx_ref, o_ref):
    q = q_ref[0].astype(BF16)
    kx = kx_ref[0].astype(BF16)
    vx = vx_ref[0].astype(BF16)
    g = ATT_HEADS // ATT_KV_HEADS
    outs = []
    for h in range(ATT_HEADS):
        kv = h // g
        qs = q[:, h * HEAD_DIM:(h + 1) * HEAD_DIM]
        ksl = slice(kv * HEAD_DIM, (kv + 1) * HEAD_DIM)
        (p_ctx,), inv = _softmax_parts(qs, [(kx[:, ksl], None)], sink_ref[h])
        outs.append(jnp.dot(p_ctx.astype(BF16), vx[:, ksl], preferred_element_type=F32) * inv)
    o_ref[0] = jnp.concatenate(outs, axis=-1)


def context_attention(at_c, sink):
    B, C, _ = at_c.shape
    kcol = QK_COLS // 128 - 1
    grid_spec = pltpu.PrefetchScalarGridSpec(
        num_scalar_prefetch=1,
        grid=(B,),
        in_specs=[pl.BlockSpec((1, C, ATT_HEADS * HEAD_DIM), lambda b, s: (b, 0, 0)),
                  pl.BlockSpec((1, C, 128), lambda b, s: (b, 0, kcol)),
                  pl.BlockSpec((1, C, 128), lambda b, s: (b, 0, kcol + 1))],
        out_specs=pl.BlockSpec((1, C, ATT_HEADS * HEAD_DIM), lambda b, s: (b, 0, 0)),
    )
    return pl.pallas_call(
        _ctx_attn_kernel,
        grid_spec=grid_spec,
        out_shape=jax.ShapeDtypeStruct((B, C, ATT_HEADS * HEAD_DIM), F32),
        compiler_params=_cparams("parallel"),
        name="context_attention",
    )(sink.astype(F32), at_c, at_c, at_c)


def _out_proj_kernel(y0_ref, y1_ref, y2_ref, y3_ref, h_ref, g1_ref, gng_ref, w_ref,
                     n2g_ref, sh_ref, sc_ref, wr_ref, br_ref, ho_ref, lg_ref):
    parts = []
    for k, y_ref in enumerate((y0_ref, y1_ref, y2_ref, y3_ref)):
        y = y_ref[0]
        ms = jnp.mean(y * y, axis=-1, keepdims=True)
        yn = y * lax.rsqrt(ms + EPS) * gng_ref[:, k * D_GROUP:(k + 1) * D_GROUP]
        parts.append(yn.astype(BF16))
    yn = jnp.concatenate(parts, axis=-1)
    proj = jnp.dot(yn, w_ref[...], preferred_element_type=F32)
    h = h_ref[0] + g1_ref[0] * proj
    ho_ref[0] = h
    ms = jnp.mean(h * h, axis=-1, keepdims=True)
    n = h * lax.rsqrt(ms + EPS) * n2g_ref[...]
    n = n * (1.0 + sc_ref[0]) + sh_ref[0]
    lg_ref[0] = jnp.dot(n.astype(BF16), wr_ref[...], preferred_element_type=F32) + br_ref[...]


def out_proj(ys, h, g1, gng, w_out, n2g, sh2, sc2, w_router, b_router, tm):
    B, L, D = h.shape
    row3 = lambda n: pl.BlockSpec((1, tm, n), lambda b, i: (b, i, 0))
    mod = pl.BlockSpec((1, 1, D), lambda b, i: (b, 0, 0))
    full = lambda r, c: pl.BlockSpec((r, c), lambda b, i: (0, 0))
    return pl.pallas_call(
        _out_proj_kernel,
        grid=(B, L // tm),
        in_specs=[row3(D_GROUP)] * 4 + [row3(D), mod, full(1, D), full(D, D), full(1, D), mod, mod,
                                        full(D, ROUTER_COLS), full(1, ROUTER_COLS)],
        out_specs=[row3(D), row3(ROUTER_COLS)],
        out_shape=[jax.ShapeDtypeStruct((B, L, D), F32), jax.ShapeDtypeStruct((B, L, ROUTER_COLS), F32)],
        compiler_params=_cparams("parallel", "parallel"),
        name="out_proj",
    )(*ys, h, g1, gng.reshape(1, D), w_out, n2g.reshape(1, D), sh2, sc2, w_router, b_router)


N_PAIRS = EXP_PER_GROUP * (EXP_PER_GROUP - 1) // 2
N_CLASSES = N_GROUPS * N_PAIRS
ROUTE_TOKENS = 512
INFO_CLASS, INFO_RANK, INFO_WA, INFO_WB = 0, 1, 2, 3


SUBLANES = 8


def _route_kernel(lg_ref, info_ref, cnt_ref, ids_ref, run):
    i = pl.program_id(0)
    tb = lg_ref.shape[0]

    @pl.when(i == 0)
    def _():
        run[...] = jnp.zeros_like(run)

    lg = lg_ref[...]
    li = lax.broadcasted_iota(jnp.int32, lg.shape, 1)
    big = jnp.int32(ROUTER_COLS)

    def first_argmax(vals):
        m = jnp.max(vals, axis=-1, keepdims=True)
        return m, jnp.min(jnp.where(vals == m, li, big), axis=-1, keepdims=True)

    gl = jnp.where(li < N_GROUPS, lg, NEG_INF)
    gmax, g_idx = first_argmax(gl)
    g_prob = 1.0 / jnp.sum(jnp.exp(gl - gmax), axis=-1, keepdims=True)
    lo = N_GROUPS + EXP_PER_GROUP * g_idx
    el = jnp.where((li >= lo) & (li < lo + EXP_PER_GROUP), lg, NEG_INF)
    m1, i1 = first_argmax(el)
    m2, i2 = first_argmax(jnp.where(li == i1, NEG_INF, el))
    e2 = jnp.exp(m2 - m1)
    w1 = g_prob / (1.0 + e2)
    w2 = g_prob * e2 / (1.0 + e2)
    j1 = i1 - lo
    j2 = i2 - lo
    a = jnp.minimum(j1, j2)
    b = jnp.maximum(j1, j2)
    cls = g_idx * N_PAIRS + ((a * (2 * EXP_PER_GROUP - 1 - a)) >> 1) + (b - a - 1)
    w_a = jnp.where(j1 < j2, w1, w2)
    w_b = jnp.where(j1 < j2, w2, w1)

    hit = li == cls
    onehot = jnp.where(hit, 1.0, 0.0)
    r_i = lax.broadcasted_iota(jnp.int32, (tb, tb), 0)
    c_i = lax.broadcasted_iota(jnp.int32, (tb, tb), 1)
    below = jnp.where(c_i < r_i, 1.0, 0.0).astype(BF16)
    before = jnp.dot(below, onehot.astype(BF16), preferred_element_type=F32)
    rank = jnp.sum(jnp.where(hit, before + run[...], 0.0), axis=-1, keepdims=True)
    run[...] = run[...] + jnp.sum(onehot, axis=0, keepdims=True)
    cnt_ref[...] = run[...]
    info = jnp.where(li == INFO_CLASS, cls.astype(F32), 0.0)
    info = jnp.where(li == INFO_RANK, rank, info)
    info = jnp.where(li == INFO_WA, w_a, info)
    info = jnp.where(li == INFO_WB, w_b, info)
    info_ref[...] = info
    ids_ref[0] = info.T[:SUBLANES].astype(jnp.int32)


def route_tokens(logits):
    T = logits.shape[0]
    tb = ROUTE_TOKENS
    return pl.pallas_call(
        _route_kernel,
        grid=(T // tb,),
        in_specs=[pl.BlockSpec((tb, ROUTER_COLS), lambda i: (i, 0))],
        out_specs=[pl.BlockSpec((tb, ROUTER_COLS), lambda i: (i, 0)),
                   pl.BlockSpec((1, ROUTER_COLS), lambda i: (0, 0)),
                   pl.BlockSpec((1, SUBLANES, tb), lambda i: (i, 0, 0))],
        out_shape=[jax.ShapeDtypeStruct((T, ROUTER_COLS), F32), jax.ShapeDtypeStruct((1, ROUTER_COLS), F32),
                   jax.ShapeDtypeStruct((T // tb, SUBLANES, tb), jnp.int32)],
        scratch_shapes=[pltpu.VMEM((1, ROUTER_COLS), F32)],
        compiler_params=_cparams("arbitrary"),
        name="moe_route",
    )(logits)


def _pair_tables():
    a_tab, b_tab = [], []
    for g in range(N_GROUPS):
        for a in range(EXP_PER_GROUP):
            for b in range(a + 1, EXP_PER_GROUP):
                a_tab.append(g * EXP_PER_GROUP + a)
                b_tab.append(g * EXP_PER_GROUP + b)
    return jnp.array(a_tab, jnp.int32), jnp.array(b_tab, jnp.int32)


def slot_plan(counts, n_blocks):
    cnt = counts[0, :N_CLASSES].astype(jnp.int32)
    padded = (cnt + MOE_BLOCK - 1) // MOE_BLOCK * MOE_BLOCK
    pad_end = jnp.cumsum(padded)
    class_start = (pad_end - padded).astype(jnp.int32)
    n_used = (pad_end[-1] // MOE_BLOCK).astype(jnp.int32).reshape(1)
    blk_cls = jnp.minimum(jnp.searchsorted(pad_end, jnp.arange(n_blocks) * MOE_BLOCK, side="right"),
                          N_CLASSES - 1)
    a_tab, b_tab = _pair_tables()
    return class_start, a_tab[blk_cls], b_tab[blk_cls], n_used


def _token_slot(start_ref, ids_ref, k):
    return start_ref[ids_ref[0, INFO_CLASS, k]] + ids_ref[0, INFO_RANK, k]


DISPATCH_TOKENS = 256


def _wait_rows(buf, sem):
    pltpu.make_async_copy(buf, buf, sem).wait()


DMA_UNROLL = 8
TOKEN_TILE_ROWS = 8
PAIR_ROWS = TOP_K * TOKEN_TILE_ROWS
PAIR_PITCH = PAIR_ROWS + 4


def _store_token_tiles(tiles_ref, offset, pitch, x):
    n = x.shape[0]
    for j in range(x.shape[1] // LANES):
        tiles_ref[pl.ds(offset + j, n, stride=pitch), :] = x[:, j * LANES:(j + 1) * LANES]


def _load_token_tiles(tiles_ref, offset, pitch, n, width):
    return jnp.concatenate([tiles_ref[pl.ds(offset + j, n, stride=pitch), :] for j in range(width // LANES)],
                           axis=-1)


def _dispatch_kernel(start_ref, ids_ref, hx_ref, hc_ref, g_ref, shx_ref, scx_ref, shc_ref, scc_ref, zeros_hbm,
                     xs_hbm, rows, sems, *, n_latent_blocks):
    del zeros_hbm
    i = pl.program_id(0)
    n = pl.num_programs(0)
    slot = i % 2
    tb = hx_ref.shape[0]
    first = (i % (ROUTE_TOKENS // tb)) * tb

    @pl.when(i >= 2)
    def _():
        _wait_rows(rows.at[slot], sems.at[slot])

    def normed(h_ref, sh_ref, sc_ref):
        h = h_ref[...]
        ms = jnp.mean(h * h, axis=-1, keepdims=True)
        return h * lax.rsqrt(ms + EPS) * g_ref[...] * (1.0 + sc_ref[0]) + sh_ref[0]

    @pl.when(i < n_latent_blocks)
    def _():
        _store_token_tiles(rows.at[slot], 0, TOKEN_TILE_ROWS, normed(hx_ref, shx_ref, scx_ref))

    @pl.when(i >= n_latent_blocks)
    def _():
        _store_token_tiles(rows.at[slot], 0, TOKEN_TILE_ROWS, normed(hc_ref, shc_ref, scc_ref))

    def body(r, carry):
        dst = pl.multiple_of(_token_slot(start_ref, ids_ref, first + r) * TOKEN_TILE_ROWS, TOKEN_TILE_ROWS)
        pltpu.make_async_copy(rows.at[slot, pl.ds(r * TOKEN_TILE_ROWS, TOKEN_TILE_ROWS)],
                              xs_hbm.at[pl.ds(dst, TOKEN_TILE_ROWS)], sems.at[slot]).start()
        return carry
    lax.fori_loop(0, tb, body, 0, unroll=DMA_UNROLL)

    @pl.when(i == n - 1)
    def _():
        _wait_rows(rows.at[slot], sems.at[slot])

        @pl.when(n >= 2)
        def _():
            _wait_rows(rows.at[1 - slot], sems.at[1 - slot])


def moe_dispatch(h_x, h_c, class_start, ids, n_blocks, n2g, mod_x, mod_c, tokens_per_batch):
    Tx, D = h_x.shape
    tb = DISPATCH_TOKENS
    nxb = Tx // tb
    if h_c is None:
        h_c, mod_c, ncb = h_x, mod_x, 0
    else:
        ncb = h_c.shape[0] // tb
    per_b = tokens_per_batch // tb
    P = n_blocks * MOE_BLOCK
    tile_rows = D // LANES
    assert tile_rows == TOKEN_TILE_ROWS
    xi = lambda i: jnp.minimum(i, nxb - 1)
    ci = lambda i: jnp.maximum(i - nxb, 0)
    sub = ROUTE_TOKENS // tb
    modx = pl.BlockSpec((1, 1, D), lambda i, st: (xi(i) // per_b, 0, 0))
    modc = pl.BlockSpec((1, 1, D), lambda i, st: (0, 0, 0))
    grid_spec = pltpu.PrefetchScalarGridSpec(
        num_scalar_prefetch=1,
        grid=(nxb + ncb,),
        in_specs=[pl.BlockSpec((1, SUBLANES, ROUTE_TOKENS), lambda i, st: (i // sub, 0, 0), memory_space=pltpu.SMEM),
                  pl.BlockSpec((tb, D), lambda i, st: (xi(i), 0)),
                  pl.BlockSpec((tb, D), lambda i, st: (ci(i), 0)),
                  pl.BlockSpec((1, D), lambda i, st: (0, 0)),
                  modx, modx, modc, modc,
                  pl.BlockSpec(memory_space=pl.ANY)],
        out_specs=pl.BlockSpec(memory_space=pl.ANY),
        scratch_shapes=[pltpu.VMEM((2, tb * tile_rows, LANES), F32), pltpu.SemaphoreType.DMA((2,))],
    )
    return pl.pallas_call(
        functools.partial(_dispatch_kernel, n_latent_blocks=nxb),
        grid_spec=grid_spec,
        out_shape=jax.ShapeDtypeStruct((P * tile_rows, LANES), F32),
        input_output_aliases={9: 0},
        compiler_params=_cparams("arbitrary"),
        name="moe_dispatch",
    )(class_start, ids, h_x, h_c, n2g.reshape(1, D), mod_x[0], mod_x[1], mod_c[0], mod_c[1],
      jnp.zeros((P * tile_rows, LANES), F32))


def _expert_pair_kernel(ea_ref, eb_ref, nused_ref, xs_ref, wga_ref, wua_ref, wda_ref, wgb_ref, wub_ref, wdb_ref,
                        o_ref):
    del ea_ref, eb_ref
    i = pl.program_id(0)
    D = wga_ref.shape[1]

    @pl.when(i < nused_ref[0])
    def _():
        xb = _load_token_tiles(xs_ref, 0, TOKEN_TILE_ROWS, MOE_BLOCK, D).astype(BF16)
        for e, (wg_ref, wu_ref, wd_ref) in enumerate(((wga_ref, wua_ref, wda_ref), (wgb_ref, wub_ref, wdb_ref))):
            gate = jnp.dot(xb, wg_ref[0], preferred_element_type=F32)
            up = jnp.dot(xb, wu_ref[0], preferred_element_type=F32)
            hid = (gate * jax.nn.sigmoid(gate) * up).astype(BF16)
            out = jnp.dot(hid, wd_ref[0], preferred_element_type=F32)
            _store_token_tiles(o_ref, e * TOKEN_TILE_ROWS, PAIR_PITCH, out)
        for j in range(PAIR_ROWS, PAIR_PITCH):
            o_ref[pl.ds(j, MOE_BLOCK, stride=PAIR_PITCH), :] = jnp.zeros((MOE_BLOCK, LANES), F32)

    @pl.when(i >= nused_ref[0])
    def _():
        o_ref[...] = jnp.zeros_like(o_ref)


def expert_pairs(xs, blk_a, blk_b, n_used, w_gate, w_up, w_down):
    D, DE = w_gate.shape[1:]
    P = xs.shape[0] // TOKEN_TILE_ROWS
    n_blocks = P // MOE_BLOCK
    wspec = lambda shape, which: pl.BlockSpec(shape, lambda i, ea, eb, nu: ((ea, eb)[which][i], 0, 0))
    grid_spec = pltpu.PrefetchScalarGridSpec(
        num_scalar_prefetch=3,
        grid=(n_blocks,),
        in_specs=[pl.BlockSpec((MOE_BLOCK * TOKEN_TILE_ROWS, LANES), lambda i, ea, eb, nu: (i, 0)),
                  wspec((1, D, DE), 0), wspec((1, D, DE), 0), wspec((1, DE, D), 0),
                  wspec((1, D, DE), 1), wspec((1, D, DE), 1), wspec((1, DE, D), 1)],
        out_specs=pl.BlockSpec((MOE_BLOCK * PAIR_PITCH, LANES), lambda i, ea, eb, nu: (i, 0)),
    )
    return pl.pallas_call(
        _expert_pair_kernel,
        grid_spec=grid_spec,
        out_shape=jax.ShapeDtypeStruct((P * PAIR_PITCH, LANES), F32),
        compiler_params=_cparams("arbitrary"),
        name="moe_experts",
    )(blk_a, blk_b, n_used, xs, w_gate, w_up, w_down, w_gate, w_up, w_down)


def _gather_pairs(start_ref, ids_ref, first, src_hbm, buf, sem, n_tokens):
    def body(r, carry):
        src = _token_slot(start_ref, ids_ref, first + r) * PAIR_PITCH
        pltpu.make_async_copy(src_hbm.at[pl.ds(src, PAIR_ROWS)], buf.at[pl.ds(r * PAIR_PITCH, PAIR_ROWS)], sem).start()
        return carry
    lax.fori_loop(0, n_tokens, body, 0, unroll=DMA_UNROLL)


def _collect_kernel(start_ref, ids_ref, ids_next_ref, o_hbm, info_ref, h_ref, g2_ref, fg_ref, out_ref, obuf, sems,
                    *, final_norm, block_offset):
    i = pl.program_id(0)
    n = pl.num_programs(0)
    slot = i % 2
    tb, D = h_ref.shape
    sub = ROUTE_TOKENS // tb
    first = ((block_offset + i) % sub) * tb
    first_next = ((block_offset + i + 1) % sub) * tb

    @pl.when(i == 0)
    def _():
        _gather_pairs(start_ref, ids_ref, first, o_hbm, obuf.at[0], sems.at[0], tb)

    @pl.when(i + 1 < n)
    def _():
        _gather_pairs(start_ref, ids_next_ref, first_next, o_hbm, obuf.at[1 - slot], sems.at[1 - slot], tb)

    _wait_rows(obuf.at[slot, pl.ds(0, tb * PAIR_ROWS)], sems.at[slot])
    w_a = info_ref[:, INFO_WA:INFO_WA + 1]
    w_b = info_ref[:, INFO_WB:INFO_WB + 1]
    m = (w_a * _load_token_tiles(obuf.at[slot], 0, PAIR_PITCH, tb, D)
         + w_b * _load_token_tiles(obuf.at[slot], TOKEN_TILE_ROWS, PAIR_PITCH, tb, D))
    h = h_ref[...] + g2_ref[0] * m
    if final_norm:
        ms = jnp.mean(h * h, axis=-1, keepdims=True)
        h = h * lax.rsqrt(ms + EPS) * fg_ref[...]
    out_ref[...] = h


def moe_collect(o_sorted, class_start, ids, info, block_offset, h_tokens, g2, tokens_per_batch, final_g,
                final_norm):
    T, D = h_tokens.shape
    tb = DISPATCH_TOKENS
    nt = T // tb
    per_b = tokens_per_batch // tb
    sub = ROUTE_TOKENS // tb
    last = block_offset + nt - 1
    ids_spec = lambda nxt: pl.BlockSpec(
        (1, SUBLANES, ROUTE_TOKENS), lambda i, st: (jnp.minimum(block_offset + i + nxt, last) // sub, 0, 0),
        memory_space=pltpu.SMEM)
    grid_spec = pltpu.PrefetchScalarGridSpec(
        num_scalar_prefetch=1,
        grid=(nt,),
        in_specs=[ids_spec(0), ids_spec(1),
                  pl.BlockSpec(memory_space=pl.ANY),
                  pl.BlockSpec((tb, ROUTER_COLS), lambda i, st: (block_offset + i, 0)),
                  pl.BlockSpec((tb, D), lambda i, st: (i, 0)),
                  pl.BlockSpec((1, 1, D), lambda i, st: (i // per_b, 0, 0)),
                  pl.BlockSpec((1, D), lambda i, st: (0, 0))],
        out_specs=pl.BlockSpec((tb, D), lambda i, st: (i, 0)),
        scratch_shapes=[pltpu.VMEM((2, tb * PAIR_PITCH, LANES), F32), pltpu.SemaphoreType.DMA((2,))],
    )
    return pl.pallas_call(
        functools.partial(_collect_kernel, final_norm=final_norm, block_offset=block_offset),
        grid_spec=grid_spec,
        out_shape=jax.ShapeDtypeStruct((T, D), F32),
        compiler_params=_cparams("arbitrary"),
        name="moe_collect",
    )(class_start, ids, ids, o_sorted, info, h_tokens, g2, final_g.reshape(1, D))


CONV_MARGIN = 16


def _time_chunk(L):
    return min(L, 256)


LANES = 128


def _zero_margins(pad_ref, L):
    zeros = jnp.zeros((CONV_MARGIN, LANES), F32)
    for s in range(pad_ref.shape[0]):
        pad_ref[s, pl.ds(0, CONV_MARGIN), :] = zeros
        pad_ref[s, pl.ds(CONV_MARGIN + L, CONV_MARGIN), :] = zeros


def _dw_conv_slab(pad_ref, s, base, T, w_ref, b_ref, col, taps, pad_left):
    acc = jnp.broadcast_to(b_ref[:, col:col + LANES], (T, LANES))
    for k in range(taps):
        acc = acc + w_ref[k:k + 1, col:col + LANES] * pad_ref[s, pl.ds(base + (CONV_MARGIN - pad_left + k), T), :]
    return acc


def _conformer_kernel(u_ref, w_ref, b_ref, g_ref, beta_ref, o_ref, ypad):
    L = o_ref.shape[1]
    T = _time_chunk(L)
    C = D_GROUP
    n_slabs = C // LANES
    pad = (CONF_KERNEL - 1) // 2
    _zero_margins(ypad, L)

    def glu(j, carry):
        base = pl.multiple_of(j * T, T)
        for s in range(n_slabs):
            a = u_ref[0, pl.ds(base, T), s * LANES:(s + 1) * LANES]
            gate = u_ref[0, pl.ds(base, T), C + s * LANES:C + (s + 1) * LANES]
            ypad[s, pl.ds(CONV_MARGIN + base, T), :] = a * jax.nn.sigmoid(gate)
        return carry
    lax.fori_loop(0, L // T, glu, 0)

    def conv(j, carry):
        base = pl.multiple_of(j * T, T)
        acc = jnp.concatenate([_dw_conv_slab(ypad, s, base, T, w_ref, b_ref, s * LANES, CONF_KERNEL, pad)
                               for s in range(n_slabs)], axis=-1)
        mu = jnp.mean(acc, axis=-1, keepdims=True)
        cen = acc - mu
        var = jnp.mean(cen * cen, axis=-1, keepdims=True)
        y = cen * lax.rsqrt(var + EPS) * g_ref[...] + beta_ref[...]
        o_ref[0, pl.ds(base, T), :] = y * jax.nn.sigmoid(y)
        return carry
    lax.fori_loop(0, L // T, conv, 0)


def conformer_conv(u, w, b, ln_g, ln_b):
    B, L, _ = u.shape
    C = D_GROUP
    vec = pl.BlockSpec((1, C), lambda i: (0, 0))
    return pl.pallas_call(
        _conformer_kernel,
        grid=(B,),
        in_specs=[pl.BlockSpec((1, L, 2 * C), lambda i: (i, 0, 0)),
                  pl.BlockSpec((CONF_KERNEL, C), lambda i: (0, 0)), vec, vec, vec],
        out_specs=pl.BlockSpec((1, L, C), lambda i: (i, 0, 0)),
        out_shape=jax.ShapeDtypeStruct((B, L, C), F32),
        scratch_shapes=[pltpu.VMEM((C // LANES, L + 2 * CONV_MARGIN, LANES), F32)],
        compiler_params=_cparams("parallel"),
        name="conformer_conv",
    )(u, w, b.reshape(1, C), ln_g.reshape(1, C), ln_b.reshape(1, C))


def _gelu_tanh(x):
    return 0.5 * x * (1.0 + jnp.tanh(math.sqrt(2.0 / math.pi) * (x + 0.044715 * (x * x * x))))


def _lru_kernel(uc_ref, ux_ref, cw_ref, cb_ref, wcat_ref, bcat_ref, lam_ref, *rest, need_ctx):
    if need_ctx:
        oc_ref, ox_ref, cpad, xpad, a_s, b_s, yx, yc = rest
    else:
        ox_ref, cpad, xpad, a_s, b_s, yx = rest
        oc_ref = yc = None
    C = D_GROUP
    n_slabs = C // LANES
    Lc = uc_ref.shape[1]
    Lx = ux_ref.shape[1]
    pad_l = (LRU_CONV - 1) // 2

    def fill(pad_ref, u_ref, L):
        T = _time_chunk(L)
        _zero_margins(pad_ref, L)

        def body(j, carry):
            base = pl.multiple_of(j * T, T)
            for s in range(n_slabs):
                pad_ref[s, pl.ds(CONV_MARGIN + base, T), :] = u_ref[0, pl.ds(base, T),
                                                                    C + s * LANES:C + (s + 1) * LANES]
            return carry
        lax.fori_loop(0, L // T, body, 0)

    fill(cpad, uc_ref, Lc)
    fill(xpad, ux_ref, Lx)

    def coeffs(pad_ref, base, T, d):
        x = jnp.concatenate([_dw_conv_slab(pad_ref, s, base, T, cw_ref, cb_ref, s * LANES, LRU_CONV, pad_l)
                             for s in range(n_slabs)], axis=-1)
        t = jnp.tanh(jnp.dot(x.astype(BF16), wcat_ref[:, 2 * d * C:2 * (d + 1) * C],
                             preferred_element_type=F32) + bcat_ref[:, 2 * d * C:2 * (d + 1) * C])
        i = 0.5 * t[:, C:] + 0.5
        z = -lam_ref[d:d + 1, :]
        softplus = jnp.maximum(z, 0.0) + jnp.log(1.0 + jnp.exp(-jnp.abs(z)))
        half_rate = (-0.5 * LRU_C) * softplus
        a = jnp.exp(half_rate * t[:, :C] + half_rate)
        b = jnp.sqrt(1.0 - a * a) * (i * x)
        for s in range(n_slabs):
            a_s[d * n_slabs + s, pl.ds(0, T), :] = a[:, s * LANES:(s + 1) * LANES]
            b_s[d * n_slabs + s, pl.ds(0, T), :] = b[:, s * LANES:(s + 1) * LANES]

    def run(pad_ref, L, h, y_ref):
        T = _time_chunk(L)
        n = L // T

        def chunk(j, h):
            base_f = pl.multiple_of(j * T, T)
            base_b = pl.multiple_of((n - 1 - j) * T, T)
            coeffs(pad_ref, base_f, T, 0)
            coeffs(pad_ref, base_b, T, 1)

            def step(t, h):
                new = []
                for d, (base, row) in enumerate(((base_f, t), (base_b, T - 1 - t))):
                    for s in range(n_slabs):
                        k = d * n_slabs + s
                        hs = a_s[k, pl.ds(row, 1), :] * h[k] + b_s[k, pl.ds(row, 1), :]
                        if y_ref is not None:
                            y_ref[k, pl.ds(base + row, 1), :] = hs
                        new.append(hs)
                return tuple(new)
            return lax.fori_loop(0, T, step, h, unroll=8)
        return lax.fori_loop(0, n, chunk, h)

    h = tuple(jnp.zeros((1, LANES), F32) for _ in range(2 * n_slabs))
    h = run(cpad, Lc, h, yc)
    run(xpad, Lx, h, yx)

    def finish(u_ref, y_ref, o_ref, L):
        T = _time_chunk(L)

        def body(j, carry):
            base = pl.multiple_of(j * T, T)
            y = jnp.concatenate([y_ref[s, pl.ds(base, T), :] + y_ref[n_slabs + s, pl.ds(base, T), :]
                                 for s in range(n_slabs)], axis=-1)
            o_ref[0, pl.ds(base, T), :] = _gelu_tanh(u_ref[0, pl.ds(base, T), :C]) * y
            return carry
        lax.fori_loop(0, L // T, body, 0)

    finish(ux_ref, yx, ox_ref, Lx)
    if need_ctx:
        finish(uc_ref, yc, oc_ref, Lc)


def _block_diag(w):
    H, n, _ = w.shape
    eye = jnp.eye(H, dtype=w.dtype)
    return (eye[:, None, :, None] * w[:, :, None, :]).reshape(H * n, H * n)


def rglru_mixer(uc, ux, lp, need_ctx):
    B, Lc, _ = uc.shape
    Lx = ux.shape[1]
    C = D_GROUP
    wcat = (0.5 * jnp.concatenate([_block_diag(lp["lru_wa"][0]), _block_diag(lp["lru_wx"][0]),
                                   _block_diag(lp["lru_wa"][1]), _block_diag(lp["lru_wx"][1])], axis=1)).astype(BF16)
    bcat = 0.5 * jnp.concatenate([lp["lru_ba"][0], lp["lru_bx"][0], lp["lru_ba"][1], lp["lru_bx"][1]]).reshape(1, 4 * C)
    full = lambda r, c: pl.BlockSpec((r, c), lambda i: (0, 0))
    seq = lambda L, n: pl.BlockSpec((1, L, n), lambda i: (i, 0, 0))
    out_specs = [seq(Lx, C)]
    out_shape = [jax.ShapeDtypeStruct((B, Lx, C), F32)]
    if need_ctx:
        out_specs = [seq(Lc, C)] + out_specs
        out_shape = [jax.ShapeDtypeStruct((B, Lc, C), F32)] + out_shape
    T = _time_chunk(Lx)
    slab = lambda rows, n=1: pltpu.VMEM((n * C // LANES, rows, LANES), F32)
    scratch = [slab(Lc + 2 * CONV_MARGIN), slab(Lx + 2 * CONV_MARGIN), slab(T, 2), slab(T, 2), slab(Lx, 2)]
    if need_ctx:
        scratch.append(slab(Lc, 2))
    res = pl.pallas_call(
        functools.partial(_lru_kernel, need_ctx=need_ctx),
        grid=(B,),
        in_specs=[seq(Lc, 2 * C), seq(Lx, 2 * C), full(LRU_CONV, C), full(1, C), full(C, 4 * C),
                  full(1, 4 * C), full(2, C)],
        out_specs=out_specs,
        out_shape=out_shape,
        scratch_shapes=scratch,
        compiler_params=_cparams("parallel"),
        name="rglru",
    )(uc, ux, lp["lru_conv_w"], lp["lru_conv_b"].reshape(1, C), wcat, bcat, lp["lru_lambda"])
    if need_ctx:
        return res[0], res[1]
    return None, res[0]


HY_SHORT = 3


def _short_conv(pad_ref, base, T, w_ref, b_ref, c0, c1):
    return jnp.concatenate([_dw_conv_slab(pad_ref, col // LANES, base, T, w_ref, b_ref, col, HY_SHORT, 1)
                            for col in range(c0, c1, LANES)], axis=-1)


def _fill_padded(pad_ref, u_ref, L, T):
    _zero_margins(pad_ref, L)

    def body(j, carry):
        base = pl.multiple_of(j * T, T)
        for s in range(pad_ref.shape[0]):
            pad_ref[s, pl.ds(CONV_MARGIN + base, T), :] = u_ref[0, pl.ds(base, T), s * LANES:(s + 1) * LANES]
        return carry
    lax.fori_loop(0, L // T, body, 0)


def _hyena_pre_kernel(u_ref, w_ref, b_ref, z_ref, upad):
    L = u_ref.shape[1]
    T = _time_chunk(L)
    C = D_GROUP
    _fill_padded(upad, u_ref, L, T)

    def body(j, carry):
        base = pl.multiple_of(j * T, T)
        x1 = _short_conv(upad, base, T, w_ref, b_ref, C, 2 * C)
        v = _short_conv(upad, base, T, w_ref, b_ref, 2 * C, 3 * C)
        z_ref[pl.ds(base, T), :] = (x1 * v).astype(BF16)
        return carry
    lax.fori_loop(0, L // T, body, 0)


def _hyena_post_kernel(u_ref, y_ref, w_ref, b_ref, bias_ref, o_ref, upad):
    L = u_ref.shape[1]
    T = _time_chunk(L)
    C = D_GROUP
    _fill_padded(upad, u_ref, L, T)

    def body(j, carry):
        base = pl.multiple_of(j * T, T)
        x0 = _short_conv(upad, base, T, w_ref, b_ref, 0, C)
        x1 = _short_conv(upad, base, T, w_ref, b_ref, C, 2 * C)
        v = _short_conv(upad, base, T, w_ref, b_ref, 2 * C, 3 * C)
        o_ref[0, pl.ds(base, T), :] = x0 * (y_ref[pl.ds(base, T), :] + (x1 * v) * bias_ref[...])
        return carry
    lax.fori_loop(0, L // T, body, 0)


def _spectrum_kernel(f_ref, z_ref, ha_ref, hb_ref, hc_ref, y_ref):
    tf = ha_ref.shape[0]
    acc = jnp.dot(f_ref[...], z_ref[...], preferred_element_type=F32)
    zr = acc[:tf]
    zi = acc[tf:]
    y_ref[:tf, :] = (zr * ha_ref[...] - zi * hb_ref[...]).astype(BF16)
    y_ref[tf:, :] = (zr * hb_ref[...] + zi * hc_ref[...]).astype(BF16)


def _idft_kernel(f_ref, y_ref, o_ref):
    o_ref[...] = jnp.dot(f_ref[...], y_ref[...], preferred_element_type=F32)


def dft_tables(L):
    N = 2 * L
    tf = min(256, L)
    k = jnp.arange(L, dtype=jnp.int32)
    n = jnp.arange(L, dtype=jnp.int32)
    ang = (2.0 * math.pi / N) * ((k[:, None] * n[None, :]) % N).astype(F32)
    cos = jnp.cos(ang)
    sin = jnp.sin(ang)
    nyq = jnp.where(n % 2 == 0, 1.0, -1.0).astype(F32)
    f_re = cos
    f_im = (-sin).at[0].set(nyq)
    fwd = jnp.stack([f_re.reshape(L // tf, tf, L), f_im.reshape(L // tf, tf, L)], axis=1).reshape(N, L)
    ck = jnp.where(k == 0, 1.0, 2.0).astype(F32)[:, None] / N
    i_re = cos * ck
    i_im = (-sin * ck).at[0].set(nyq / N)
    inv = jnp.stack([i_re.reshape(L // tf, tf, L), i_im.reshape(L // tf, tf, L)], axis=1).reshape(N, L).T
    return fwd.astype(BF16), inv.astype(BF16)


def filter_spectrum(h_fwd, h_bwd):
    L, C = h_fwd.shape
    k = jnp.concatenate([h_fwd, jnp.zeros((1, C), F32), h_bwd[1:][::-1]], axis=0)
    hf = jnp.fft.rfft(k, axis=0)
    hr = jnp.real(hf)
    hi = jnp.imag(hf)
    a = hr[:L]
    b = hi[:L].at[0].set(0.0)
    c = hr[:L].at[0].set(hr[L])
    return a, b, c


def hyena_mixer(u, lp, tables):
    B, L, _ = u.shape
    C = D_GROUP
    N = 2 * L
    fwd, inv = tables
    tf = min(256, L)
    T = _time_chunk(L)
    w, bsh = lp["hy_short_w"], lp["hy_short_b"].reshape(1, 3 * C)
    z2 = pl.pallas_call(
        _hyena_pre_kernel,
        grid=(B,),
        in_specs=[pl.BlockSpec((1, L, 3 * C), lambda b: (b, 0, 0)),
                  pl.BlockSpec((HY_SHORT, 3 * C), lambda b: (0, 0)),
                  pl.BlockSpec((1, 3 * C), lambda b: (0, 0))],
        out_specs=pl.BlockSpec((L, C), lambda b: (0, b)),
        out_shape=jax.ShapeDtypeStruct((L, B * C), BF16),
        scratch_shapes=[pltpu.VMEM((3 * C // LANES, L + 2 * CONV_MARGIN, LANES), F32)],
        compiler_params=_cparams("parallel"),
        name="hyena_pre",
    )(u, w, bsh)

    h_fwd, h_bwd = _hyena_filters(L, lp)
    tn = 2 * C
    ha, hb, hc = [jnp.tile(t, (1, tn // C)) for t in filter_spectrum(h_fwd, h_bwd)]
    hspec = pl.BlockSpec((tf, tn), lambda i, j: (i, 0))
    y2 = pl.pallas_call(
        _spectrum_kernel,
        grid=(L // tf, B * C // tn),
        in_specs=[pl.BlockSpec((2 * tf, L), lambda i, j: (i, 0)),
                  pl.BlockSpec((L, tn), lambda i, j: (0, j)), hspec, hspec, hspec],
        out_specs=pl.BlockSpec((2 * tf, tn), lambda i, j: (i, j)),
        out_shape=jax.ShapeDtypeStruct((N, B * C), BF16),
        compiler_params=_cparams("parallel", "parallel"),
        name="hyena_spectrum",
    )(fwd, z2, ha, hb, hc)

    tl = min(256, L)
    yt = pl.pallas_call(
        _idft_kernel,
        grid=(L // tl, B * C // tn),
        in_specs=[pl.BlockSpec((tl, N), lambda i, j: (i, 0)),
                  pl.BlockSpec((N, tn), lambda i, j: (0, j))],
        out_specs=pl.BlockSpec((tl, tn), lambda i, j: (i, j)),
        out_shape=jax.ShapeDtypeStruct((L, B * C), F32),
        compiler_params=_cparams("parallel", "parallel"),
        name="hyena_idft",
    )(inv, y2)

    return pl.pallas_call(
        _hyena_post_kernel,
        grid=(B,),
        in_specs=[pl.BlockSpec((1, L, 3 * C), lambda b: (b, 0, 0)),
                  pl.BlockSpec((L, C), lambda b: (0, b)),
                  pl.BlockSpec((HY_SHORT, 3 * C), lambda b: (0, 0)),
                  pl.BlockSpec((1, 3 * C), lambda b: (0, 0)),
                  pl.BlockSpec((1, C), lambda b: (0, 0))],
        out_specs=pl.BlockSpec((1, L, C), lambda b: (b, 0, 0)),
        out_shape=jax.ShapeDtypeStruct((B, L, C), F32),
        scratch_shapes=[pltpu.VMEM((3 * C // LANES, L + 2 * CONV_MARGIN, LANES), F32)],
        compiler_params=_cparams("parallel"),
        name="hyena_post",
    )(u, yt, w, bsh, lp["hy_bias"].reshape(1, C))


FFT_N2 = 128
FFT_UNROLL = 8


class _FftPlan:
    def __init__(self, L):
        self.L = L
        self.N = 2 * L
        self.N1 = self.N // FFT_N2
        self.KH = self.N1 // 2 + 1
        self.KP = -(-self.KH // 8) * 8
        self.PA = 2 * self.KP + 4


def fft_tables(L):
    p = _FftPlan(L)
    N, N1, KH, KP = p.N, p.N1, p.KH, p.KP
    n2 = jnp.arange(FFT_N2, dtype=jnp.int32)
    k1 = jnp.arange(KP, dtype=jnp.int32)
    n1 = jnp.arange(N1, dtype=jnp.int32)
    n = FFT_N2 * n1[None, None, :] + n2[:, None, None]
    ang = (2.0 * math.pi / N) * ((k1[None, :, None] * n) % N).astype(F32)
    keep = (k1 < KH)[None, :, None]
    g_re = jnp.where(keep, jnp.cos(ang), 0.0)
    g_im = jnp.where(keep, -jnp.sin(ang), 0.0)
    ga_full = jnp.concatenate([g_re, g_im], axis=1)
    ck = jnp.where((k1 == 0) | (k1 == N1 // 2), 1.0, 2.0) / N
    ga_inv = jnp.swapaxes(ga_full[:, :, :N1 // 2] * jnp.tile(ck, 2)[None, :, None], 1, 2)
    kk = jnp.arange(FFT_N2, dtype=jnp.int32)
    ang2 = (2.0 * math.pi / FFT_N2) * ((kk[:, None] * kk[None, :]) % FFT_N2).astype(F32)
    fr, fi = jnp.cos(ang2), -jnp.sin(ang2)
    fb = jnp.block([[fr, -fi], [fi, fr]])
    fb_inv = jnp.block([[fr, fi], [-fi, fr]])
    return dict(ga_half=ga_full[:, :, :N1 // 2].astype(BF16), ga_full=ga_full.astype(BF16),
                ga_inv=ga_inv.astype(BF16), fb=fb.astype(BF16), fb_inv=fb_inv.astype(BF16))


def _fft_stage_a(x_ref, ga_ref, s_ref, plan, n1_count):
    n_slabs = x_ref.shape[0]

    def body(n2, carry):
        xs = jnp.concatenate([x_ref[s, pl.ds(n2, n1_count, stride=FFT_N2), :] for s in range(n_slabs)], axis=-1)
        a = jnp.dot(ga_ref[n2], xs.astype(BF16), preferred_element_type=F32)
        for s in range(n_slabs):
            s_ref[s, pl.ds(n2 * plan.PA, 2 * plan.KP), :] = a[:, s * LANES:(s + 1) * LANES]
        return carry
    lax.fori_loop(0, FFT_N2, body, 0, unroll=FFT_UNROLL)


def _fft_load_k1(s_ref, k1, plan):
    n_slabs = s_ref.shape[0]
    re = jnp.concatenate([s_ref[s, pl.ds(k1, FFT_N2, stride=plan.PA), :] for s in range(n_slabs)], axis=-1)
    im = jnp.concatenate([s_ref[s, pl.ds(plan.KP + k1, FFT_N2, stride=plan.PA), :] for s in range(n_slabs)], axis=-1)
    return jnp.concatenate([re, im], axis=0).astype(BF16)


def _fft_filter_kernel(k_ref, ga_ref, fb_ref, h_ref, s_ref, *, plan):
    _fft_stage_a(k_ref, ga_ref, s_ref, plan, plan.N1)

    def body(k1, carry):
        h_ref[k1] = jnp.dot(fb_ref[...], _fft_load_k1(s_ref, k1, plan), preferred_element_type=F32).astype(BF16)
        return carry
    lax.fori_loop(0, plan.KH, body, 0)


def _fft_conv_kernel(z_ref, ga_ref, gi_ref, fb_ref, fbi_ref, h_ref, y_ref, s_ref, *, plan):
    zs = z_ref.at[0]
    ys = y_ref.at[0]
    n_slabs = zs.shape[0]
    half = FFT_N2
    _fft_stage_a(zs, ga_ref, s_ref, plan, plan.N1 // 2)

    def body_b(k1, carry):
        x = jnp.dot(fb_ref[...], _fft_load_k1(s_ref, k1, plan), preferred_element_type=F32)
        h = h_ref[k1].astype(F32)
        xr, xi, hr, hi = x[:half], x[half:], h[:half], h[half:]
        y = jnp.concatenate([xr * hr - xi * hi, xr * hi + xi * hr], axis=0).astype(BF16)
        b = jnp.dot(fbi_ref[...], y, preferred_element_type=F32)
        for s in range(n_slabs):
            s_ref[s, pl.ds(k1, FFT_N2, stride=plan.PA), :] = b[:half, s * LANES:(s + 1) * LANES]
            s_ref[s, pl.ds(plan.KP + k1, FFT_N2, stride=plan.PA), :] = b[half:, s * LANES:(s + 1) * LANES]
        return carry
    lax.fori_loop(0, plan.KH, body_b, 0, unroll=3)

    def body_a(n2, carry):
        b = jnp.concatenate([s_ref[s, pl.ds(n2 * plan.PA, 2 * plan.KP), :] for s in range(n_slabs)], axis=-1)
        y = jnp.dot(gi_ref[n2], b.astype(BF16), preferred_element_type=F32)
        for s in range(n_slabs):
            ys[s, pl.ds(n2, plan.N1 // 2, stride=FFT_N2), :] = y[:, s * LANES:(s + 1) * LANES]
        return carry
    lax.fori_loop(0, FFT_N2, body_a, 0, unroll=FFT_UNROLL)


def fft_filter_spectrum(h_fwd, h_bwd, tabs):
    L, C = h_fwd.shape
    plan = _FftPlan(L)
    n_slabs = C // LANES
    k = jnp.concatenate([h_fwd, jnp.zeros((1, C), F32), h_bwd[1:][::-1]], axis=0)
    k = k.reshape(plan.N, n_slabs, LANES).transpose(1, 0, 2)
    full = lambda shape: pl.BlockSpec(shape, lambda i: (0,) * len(shape))
    return pl.pallas_call(
        functools.partial(_fft_filter_kernel, plan=plan),
        grid=(1,),
        in_specs=[full((n_slabs, plan.N, LANES)), full((FFT_N2, 2 * plan.KP, plan.N1)),
                  full((2 * FFT_N2, 2 * FFT_N2))],
        out_specs=full((plan.KH, 2 * FFT_N2, C)),
        out_shape=jax.ShapeDtypeStruct((plan.KH, 2 * FFT_N2, C), BF16),
        scratch_shapes=[pltpu.VMEM((n_slabs, FFT_N2 * plan.PA, LANES), F32)],
        compiler_params=_cparams("arbitrary"),
        name="hyena_filter_fft",
    )(k, tabs["ga_full"], tabs["fb"])


def fft_long_conv(z, h_spec, tabs):
    B, n_slabs, L, _ = z.shape
    plan = _FftPlan(L)
    C = n_slabs * LANES
    full = lambda shape: pl.BlockSpec(shape, lambda b: (0,) * len(shape))
    seq = pl.BlockSpec((1, n_slabs, L, LANES), lambda b: (b, 0, 0, 0))
    return pl.pallas_call(
        functools.partial(_fft_conv_kernel, plan=plan),
        grid=(B,),
        in_specs=[seq, full((FFT_N2, 2 * plan.KP, plan.N1 // 2)), full((FFT_N2, plan.N1 // 2, 2 * plan.KP)),
                  full((2 * FFT_N2, 2 * FFT_N2)), full((2 * FFT_N2, 2 * FFT_N2)),
                  full((plan.KH, 2 * FFT_N2, C))],
        out_specs=seq,
        out_shape=jax.ShapeDtypeStruct((B, n_slabs, L, LANES), F32),
        scratch_shapes=[pltpu.VMEM((n_slabs, FFT_N2 * plan.PA, LANES), F32)],
        compiler_params=_cparams("parallel"),
        name="hyena_fft_conv",
    )(z, tabs["ga_half"], tabs["ga_inv"], tabs["fb"], tabs["fb_inv"], h_spec)


def _hyena_pre_slab_kernel(u_ref, w_ref, b_ref, z_ref, upad):
    L = u_ref.shape[1]
    T = _time_chunk(L)
    C = D_GROUP
    _fill_padded(upad, u_ref, L, T)

    def body(j, carry):
        base = pl.multiple_of(j * T, T)
        for s in range(C // LANES):
            x1 = _dw_conv_slab(upad, C // LANES + s, base, T, w_ref, b_ref, C + s * LANES, HY_SHORT, 1)
            v = _dw_conv_slab(upad, 2 * C // LANES + s, base, T, w_ref, b_ref, 2 * C + s * LANES, HY_SHORT, 1)
            z_ref[0, s, pl.ds(base, T), :] = x1 * v
        return carry
    lax.fori_loop(0, L // T, body, 0)


def _hyena_post_slab_kernel(u0_ref, z_ref, y_ref, w_ref, b_ref, bias_ref, o_ref, upad):
    L = u0_ref.shape[1]
    T = _time_chunk(L)
    C = D_GROUP
    _fill_padded(upad, u0_ref, L, T)

    def body(j, carry):
        base = pl.multiple_of(j * T, T)
        x0 = _short_conv(upad, base, T, w_ref, b_ref, 0, C)
        z = jnp.concatenate([z_ref[0, s, pl.ds(base, T), :] for s in range(C // LANES)], axis=-1)
        y = jnp.concatenate([y_ref[0, s, pl.ds(base, T), :] for s in range(C // LANES)], axis=-1)
        o_ref[0, pl.ds(base, T), :] = x0 * (y + z * bias_ref[...])
        return carry
    lax.fori_loop(0, L // T, body, 0)


def hyena_mixer_fft(u, lp, tabs):
    B, L, _ = u.shape
    C = D_GROUP
    n_slabs = C // LANES
    w, bsh = lp["hy_short_w"], lp["hy_short_b"].reshape(1, 3 * C)
    useq = pl.BlockSpec((1, L, 3 * C), lambda b: (b, 0, 0))
    slabs = pl.BlockSpec((1, n_slabs, L, LANES), lambda b: (b, 0, 0, 0))
    wspec = pl.BlockSpec((HY_SHORT, 3 * C), lambda b: (0, 0))
    bspec = pl.BlockSpec((1, 3 * C), lambda b: (0, 0))
    pad_scratch = pltpu.VMEM((3 * C // LANES, L + 2 * CONV_MARGIN, LANES), F32)
    z = pl.pallas_call(
        _hyena_pre_slab_kernel,
        grid=(B,),
        in_specs=[useq, wspec, bspec],
        out_specs=slabs,
        out_shape=jax.ShapeDtypeStruct((B, n_slabs, L, LANES), F32),
        scratch_shapes=[pad_scratch],
        compiler_params=_cparams("parallel"),
        name="hyena_pre",
    )(u, w, bsh)
    h_fwd, h_bwd = _hyena_filters(L, lp)
    y = fft_long_conv(z, fft_filter_spectrum(h_fwd, h_bwd, tabs), tabs)
    return pl.pallas_call(
        _hyena_post_slab_kernel,
        grid=(B,),
        in_specs=[pl.BlockSpec((1, L, C), lambda b: (b, 0, 0)), slabs, slabs, wspec, bspec,
                  pl.BlockSpec((1, C), lambda b: (0, 0))],
        out_specs=pl.BlockSpec((1, L, C), lambda b: (b, 0, 0)),
        out_shape=jax.ShapeDtypeStruct((B, L, C), F32),
        scratch_shapes=[pltpu.VMEM((n_slabs, L + 2 * CONV_MARGIN, LANES), F32)],
        compiler_params=_cparams("parallel"),
        name="hyena_post",
    )(u, z, y, w, bsh, lp["hy_bias"].reshape(1, C))


def _hyena_filters(L, lp):
    t = jnp.linspace(0.0, 1.0, L, dtype=F32)[:, None]
    bands = (HY_EMB - 1) // 2
    w = 2.0 * math.pi * jnp.arange(L, dtype=F32)[:, None] / L
    f = jnp.linspace(1e-4, bands - 1, bands, dtype=F32)[None]
    z = jnp.concatenate([t, jnp.cos(f * w), -jnp.sin(f * w)], axis=-1)
    hdn = jnp.sin(z @ lp["hy_ffn_w1"] + lp["hy_ffn_b1"])
    hdn = jnp.sin(hdn @ lp["hy_ffn_w2"] + lp["hy_ffn_b2"])
    h = (hdn @ lp["hy_ffn_w3"]).reshape(L, 2, D_GROUP)
    max_decay = math.log(HY_TARGET) / HY_FAST_DECAY
    min_decay = math.log(HY_TARGET) / HY_SLOW_DECAY
    deltas = jnp.linspace(min_decay, max_decay, D_GROUP, dtype=F32)
    h = h * jnp.exp(-t * jnp.abs(deltas))[:, None, :]
    h = h / (jnp.sum(jnp.abs(h), axis=(0, 1), keepdims=True) + EPS)
    return h[:, 0], h[:, 1]


def _layer(hc, hx, c_silu_all, lp, need_ctx, final_g, final_norm, tables_x, tables_c):
    B, S, D = hx.shape
    C = hc.shape[1]
    mod = small_linear(c_silu_all, lp["ada_w"], lp["ada_b"])
    mod_x = mod[:B].reshape(B, 6, 1, D)
    mod_c = jnp.broadcast_to(mod[B].reshape(1, 6, 1, D), (B, 6, 1, D))
    w_ext = extend_w_in(lp["w_in"])
    cos_x, sin_x = rope_tables(S, True)
    cos_c, sin_c = rope_tables(C, False)
    hy_x, cf_x, at_x, lr_x = in_proj(hx, mod_x[:, 0], mod_x[:, 1], lp["norm1_g"], w_ext, cos_x, sin_x, tm=512)
    hy_c, cf_c, at_c, lr_c = in_proj(hc, mod_c[:, 0], mod_c[:, 1], lp["norm1_g"], w_ext, cos_c, sin_c, tm=256)

    yd_c, yd_x = rglru_mixer(lr_c, lr_x, lp, need_ctx)
    conf = lambda u: conformer_conv(u, lp["conf_dw_w"], lp["conf_dw_b"], lp["conf_ln_g"], lp["conf_ln_b"])
    ys_x = [hyena_mixer_fft(hy_x, lp, tables_x), conf(cf_x),
            window_attention(at_x, at_c, lp["attn_sink"]), yd_x]

    w_out = lp["w_out"].astype(BF16)
    w_router = jnp.zeros((D, ROUTER_COLS), F32)
    w_router = w_router.at[:, :N_GROUPS].set(lp["router_g_w"]).at[:, N_GROUPS:N_GROUPS + N_EXPERTS].set(lp["router_e_w"])
    w_router = w_router.astype(BF16)
    b_router = jnp.zeros((1, ROUTER_COLS), F32)
    b_router = b_router.at[0, :N_GROUPS].set(lp["router_g_b"]).at[0, N_GROUPS:N_GROUPS + N_EXPERTS].set(lp["router_e_b"])

    hx1, lg_x = out_proj(ys_x, hx, mod_x[:, 2], lp["group_norm_g"], w_out, lp["norm2_g"],
                         mod_x[:, 3], mod_x[:, 4], w_router, b_router, tm=512)
    h_tok = hx1.reshape(B * S, D)
    hc_tok = None
    lg = lg_x.reshape(B * S, ROUTER_COLS)
    if need_ctx:
        ys_c = [hyena_mixer(hy_c, lp, tables_c), conf(cf_c),
                context_attention(at_c, lp["attn_sink"]), yd_c]
        hc1, lg_c = out_proj(ys_c, hc, mod_c[:, 2], lp["group_norm_g"], w_out, lp["norm2_g"],
                             mod_c[:, 3], mod_c[:, 4], w_router, b_router, tm=256)
        hc_tok = hc1.reshape(B * C, D)
        lg = jnp.concatenate([lg, lg_c.reshape(B * C, ROUTER_COLS)], axis=0)

    T = lg.shape[0]
    n_blocks = -(-T // MOE_BLOCK) + N_CLASSES
    info, counts, ids = route_tokens(lg)
    class_start, blk_a, blk_b, n_used = slot_plan(counts, n_blocks)
    xs = moe_dispatch(h_tok, hc_tok, class_start, ids, n_blocks, lp["norm2_g"], (mod_x[:, 3], mod_x[:, 4]),
                      (mod_c[:, 3], mod_c[:, 4]), S)
    o_sorted = expert_pairs(xs, blk_a, blk_b, n_used, lp["exp_w_gate"].astype(BF16),
                            lp["exp_w_up"].astype(BF16), lp["exp_w_down"].astype(BF16))
    hx2 = moe_collect(o_sorted, class_start, ids, info, 0, h_tok, mod_x[:, 5], S, final_g, final_norm)
    hx2 = hx2.reshape(B, S, D)
    if need_ctx:
        hc2 = moe_collect(o_sorted, class_start, ids, info, B * S // DISPATCH_TOKENS, hc_tok, mod_c[:, 5], C,
                          final_g, False).reshape(B, C, D)
    else:
        hc2 = hc
    return hc2, hx2


def kernel(x, c, ctx, c_ctx, norm1_g, norm2_g, ada_w, ada_b, w_in, hy_short_w, hy_short_b, hy_ffn_w1, hy_ffn_b1, hy_ffn_w2, hy_ffn_b2, hy_ffn_w3, hy_bias, conf_dw_w, conf_dw_b, conf_ln_g, conf_ln_b, attn_sink, lru_conv_w, lru_conv_b, lru_wa, lru_ba, lru_wx, lru_bx, lru_lambda, group_norm_g, w_out, router_g_w, router_g_b, router_e_w, router_e_b, exp_w_gate, exp_w_up, exp_w_down, final_norm_g):
    stacked = dict(norm1_g=norm1_g, norm2_g=norm2_g, ada_w=ada_w, ada_b=ada_b, w_in=w_in,
                   hy_short_w=hy_short_w, hy_short_b=hy_short_b, hy_ffn_w1=hy_ffn_w1, hy_ffn_b1=hy_ffn_b1,
                   hy_ffn_w2=hy_ffn_w2, hy_ffn_b2=hy_ffn_b2, hy_ffn_w3=hy_ffn_w3, hy_bias=hy_bias,
                   conf_dw_w=conf_dw_w, conf_dw_b=conf_dw_b, conf_ln_g=conf_ln_g, conf_ln_b=conf_ln_b,
                   attn_sink=attn_sink, lru_conv_w=lru_conv_w, lru_conv_b=lru_conv_b, lru_wa=lru_wa,
                   lru_ba=lru_ba, lru_wx=lru_wx, lru_bx=lru_bx, lru_lambda=lru_lambda,
                   group_norm_g=group_norm_g, w_out=w_out, router_g_w=router_g_w, router_g_b=router_g_b,
                   router_e_w=router_e_w, router_e_b=router_e_b, exp_w_gate=exp_w_gate,
                   exp_w_up=exp_w_up, exp_w_down=exp_w_down)
    depth = norm1_g.shape[0]
    B = x.shape[0]
    cs = jnp.concatenate([jax.nn.silu(c), jnp.broadcast_to(jax.nn.silu(c_ctx)[None], (8, c.shape[1]))], axis=0)
    hc, hx = ctx, x
    tables_x = fft_tables(x.shape[1])
    tables_c = dft_tables(ctx.shape[1])
    for l in range(depth):
        lp = {k: v[l] for k, v in stacked.items()}
        hc, hx = _layer(hc, hx, cs, lp, need_ctx=(l < depth - 1), final_g=final_norm_g,
                        final_norm=(l == depth - 1), tables_x=tables_x, tables_c=tables_c)
    return hx
```

```python
import functools
import math

import jax
import jax.numpy as jnp
from jax import lax
from jax.experimental import pallas as pl
from jax.experimental.pallas import tpu as pltpu

F32 = jnp.float32
BF16 = jnp.bfloat16

EPS = 1e-6
NEG_INF = -1e30
GRID_W = 64
N_MIXERS = 4
D_GROUP = 256
HY_COLS = 3 * D_GROUP
CONF_COLS = 2 * D_GROUP
ATT_HEADS = 4
ATT_KV_HEADS = 2
HEAD_DIM = 64
ATT_COLS = (ATT_HEADS + 2 * ATT_KV_HEADS) * HEAD_DIM
LRU_COLS = 2 * D_GROUP
QK_COLS = (ATT_HEADS + ATT_KV_HEADS) * HEAD_DIM
WINDOW = 128
ATT_BLOCK = 128
ROPE_BASE = 10000.0
HY_EMB = 33
HY_FAST_DECAY = 0.3
HY_SLOW_DECAY = 1.5
HY_TARGET = 1e-2
CONF_KERNEL = 31
LRU_HEADS = 4
LRU_CONV = 4
LRU_C = 8.0
N_GROUPS = 4
EXP_PER_GROUP = 8
N_EXPERTS = N_GROUPS * EXP_PER_GROUP
TOP_K = 2
MOE_BLOCK = 256
ROUTER_COLS = 128

VMEM_LIMIT_BYTES = 56 * 1024 * 1024


def _cparams(*sem):
    return pltpu.CompilerParams(dimension_semantics=sem, vmem_limit_bytes=VMEM_LIMIT_BYTES)


def _linear_kernel(x_ref, w_ref, b_ref, o_ref):
    o_ref[...] = jnp.dot(x_ref[...], w_ref[...], preferred_element_type=F32,
                         precision=lax.Precision.HIGHEST) + b_ref[...]


def small_linear(x, w, b, tn=1024):
    M, K = x.shape
    N = w.shape[1]
    return pl.pallas_call(
        _linear_kernel,
        grid=(N // tn,),
        in_specs=[pl.BlockSpec((M, K), lambda j: (0, 0)),
                  pl.BlockSpec((K, tn), lambda j: (0, j)),
                  pl.BlockSpec((1, tn), lambda j: (0, j))],
        out_specs=pl.BlockSpec((M, tn), lambda j: (0, j)),
        out_shape=jax.ShapeDtypeStruct((M, N), F32),
        compiler_params=_cparams("parallel"),
        name="ada_linear",
    )(x, w, b.reshape(1, N))


def _in_proj_kernel(x_ref, sh_ref, sc_ref, g_ref, w_ref, cos_ref, sin_ref,
                    hy_ref, cf_ref, at_ref, lr_ref):
    x = x_ref[0]
    ms = jnp.mean(x * x, axis=-1, keepdims=True)
    y = x * lax.rsqrt(ms + EPS) * g_ref[...]
    y = y * (1.0 + sc_ref[0]) + sh_ref[0]
    u = jnp.dot(y.astype(BF16), w_ref[...], preferred_element_type=F32)
    c0 = HY_COLS
    c1 = c0 + CONF_COLS
    c2 = c1 + ATT_COLS
    c3 = c2 + LRU_COLS
    hy_ref[0] = u[:, :c0]
    cf_ref[0] = u[:, c0:c1]
    lr_ref[0] = u[:, c2:c3]
    qk = u[:, c1:c1 + QK_COLS]
    qk_rot = u[:, c3:c3 + QK_COLS]
    at_ref[0, :, :QK_COLS] = qk * cos_ref[...] + qk_rot * sin_ref[...]
    at_ref[0, :, QK_COLS:] = u[:, c1 + QK_COLS:c2]


def in_proj(h, shift, scale, g, w_ext, cos_t, sin_t, tm):
    B, L, D = h.shape
    NW = w_ext.shape[1]
    outs = [HY_COLS, CONF_COLS, ATT_COLS, LRU_COLS]
    return pl.pallas_call(
        _in_proj_kernel,
        grid=(B, L // tm),
        in_specs=[pl.BlockSpec((1, tm, D), lambda b, i: (b, i, 0)),
                  pl.BlockSpec((1, 1, D), lambda b, i: (b, 0, 0)),
                  pl.BlockSpec((1, 1, D), lambda b, i: (b, 0, 0)),
                  pl.BlockSpec((1, D), lambda b, i: (0, 0)),
                  pl.BlockSpec((D, NW), lambda b, i: (0, 0)),
                  pl.BlockSpec((tm, QK_COLS), lambda b, i: (i, 0)),
                  pl.BlockSpec((tm, QK_COLS), lambda b, i: (i, 0))],
        out_specs=[pl.BlockSpec((1, tm, n), lambda b, i: (b, i, 0)) for n in outs],
        out_shape=[jax.ShapeDtypeStruct((B, L, n), F32) for n in outs],
        compiler_params=_cparams("parallel", "parallel"),
        name="in_proj",
    )(h, shift, scale, g.reshape(1, D), w_ext, cos_t, sin_t)


def rope_tables(L, rotary):
    n_heads = ATT_HEADS + ATT_KV_HEADS
    if not rotary:
        return jnp.ones((L, QK_COLS), F32), jnp.zeros((L, QK_COLS), F32)
    pos = jnp.arange(L)
    row = (pos // GRID_W).astype(F32)
    col = (pos % GRID_W).astype(F32)
    half = HEAD_DIM // 2
    inv_freq = ROPE_BASE ** (-jnp.arange(0, half, 2, dtype=F32) / half)
    ang_r = row[:, None] * inv_freq[None]
    ang_c = col[:, None] * inv_freq[None]
    cos_h = jnp.concatenate([jnp.cos(ang_r)] * 2 + [jnp.cos(ang_c)] * 2, axis=-1)
    sin_h = jnp.concatenate([jnp.sin(ang_r)] * 2 + [jnp.sin(ang_c)] * 2, axis=-1)
    return jnp.tile(cos_h, (1, n_heads)), jnp.tile(sin_h, (1, n_heads))


def extend_w_in(w_in):
    c1 = HY_COLS + CONF_COLS
    wqk = w_in[:, c1:c1 + QK_COLS]
    D = w_in.shape[0]
    w4 = wqk.reshape(D, QK_COLS // 32, 2, 16)
    wrot = jnp.stack([-w4[:, :, 1], w4[:, :, 0]], axis=2).reshape(D, QK_COLS)
    return jnp.concatenate([w_in, wrot], axis=1).astype(BF16)


def _softmax_parts(q, k_list, extra_logit):
    scale = HEAD_DIM ** -0.5
    s_list = []
    for k, mask in k_list:
        s = lax.dot_general(q, k, (((1,), (1,)), ((), ())), preferred_element_type=F32) * scale
        if mask is not None:
            s = jnp.where(mask, s, NEG_INF)
        s_list.append(s)
    m = extra_logit
    for s in s_list:
        m = jnp.maximum(m, jnp.max(s, axis=-1, keepdims=True))
    p_list = [jnp.exp(s - m) for s in s_list]
    denom = jnp.exp(extra_logit - m)
    for p in p_list:
        denom = denom + jnp.sum(p, axis=-1, keepdims=True)
    return p_list, 1.0 / denom


ATT_Q_BLOCKS = 4


def _win_attn_kernel(sink_ref, q_ref, kp_ref, kc_ref, kn_ref, vp_ref, vc_ref, vn_ref,
                     kx_ref, vx_ref, o_ref, *, seq_len):
    i = pl.program_id(1)
    blk = ATT_BLOCK
    qb = q_ref.shape[1] // blk
    scale = HEAD_DIM ** -0.5
    g = ATT_HEADS // ATT_KV_HEADS
    kw = jnp.concatenate([kp_ref[0], kc_ref[0], kn_ref[0]], axis=0)
    vw = jnp.concatenate([vp_ref[0], vc_ref[0], vn_ref[0]], axis=0).astype(BF16)
    kwt = kw.T.astype(BF16)
    kxt = kx_ref[0].T.astype(BF16)
    vx = vx_ref[0].astype(BF16)
    row = lax.broadcasted_iota(jnp.int32, (g * blk, 3 * blk), 0) % blk
    col = lax.broadcasted_iota(jnp.int32, (g * blk, 3 * blk), 1)
    band_bias = jnp.where(jnp.abs(col - blk - row) <= WINDOW, 0.0, NEG_INF)
    col1 = lax.broadcasted_iota(jnp.int32, (1, 3 * blk), 1)
    for j in range(qb):
        q_blk = i * qb + j
        k_pos = (q_blk - 1) * blk + col1
        edge_bias = jnp.where(k_pos >= 0, jnp.where(k_pos < seq_len, 0.0, NEG_INF), NEG_INF)
        bias = band_bias + edge_bias
        outs = []
        for kv in range(ATT_KV_HEADS):
            ksl = slice(kv * HEAD_DIM, (kv + 1) * HEAD_DIM)
            heads = range(kv * g, (kv + 1) * g)
            qs = (jnp.concatenate([q_ref[0, j * blk:(j + 1) * blk, h * HEAD_DIM:(h + 1) * HEAD_DIM]
                                   for h in heads], axis=0) * scale).astype(BF16)
            sink = jnp.concatenate([jnp.full((blk, 1), sink_ref[h], F32) for h in heads], axis=0)
            s_win = jnp.dot(qs, kwt[ksl, j * blk:(j + 3) * blk], preferred_element_type=F32) + bias
            s_ctx = jnp.dot(qs, kxt[ksl, :], preferred_element_type=F32)
            m = jnp.maximum(jnp.maximum(jnp.max(s_win, axis=-1, keepdims=True),
                                        jnp.max(s_ctx, axis=-1, keepdims=True)), sink)
            p_win = jnp.exp(s_win - m)
            p_ctx = jnp.exp(s_ctx - m)
            denom = (jnp.exp(sink - m) + jnp.sum(p_win, axis=-1, keepdims=True)
                     + jnp.sum(p_ctx, axis=-1, keepdims=True))
            o = (jnp.dot(p_win.astype(BF16), vw[j * blk:(j + 3) * blk, ksl], preferred_element_type=F32)
                 + jnp.dot(p_ctx.astype(BF16), vx[:, ksl], preferred_element_type=F32)) * (1.0 / denom)
            outs.extend([o[k * blk:(k + 1) * blk] for k in range(g)])
        o_ref[0, j * blk:(j + 1) * blk, :] = jnp.concatenate(outs, axis=-1)


def window_attention(at_x, at_c, sink):
    B, S, _ = at_x.shape
    C = at_c.shape[1]
    blk = ATT_BLOCK
    qb = ATT_Q_BLOCKS
    nb = S // blk
    kcol = QK_COLS // 128 - 1
    vcol = kcol + 1

    def edge_spec(col, off):
        return pl.BlockSpec((1, blk, 128), lambda b, i, s: (b, jnp.clip(i * qb + off, 0, nb - 1), col))

    def mid_spec(col):
        return pl.BlockSpec((1, qb * blk, 128), lambda b, i, s: (b, i, col))

    grid_spec = pltpu.PrefetchScalarGridSpec(
        num_scalar_prefetch=1,
        grid=(B, nb // qb),
        in_specs=[pl.BlockSpec((1, qb * blk, ATT_HEADS * HEAD_DIM), lambda b, i, s: (b, i, 0)),
                  edge_spec(kcol, -1), mid_spec(kcol), edge_spec(kcol, qb),
                  edge_spec(vcol, -1), mid_spec(vcol), edge_spec(vcol, qb),
                  pl.BlockSpec((1, C, 128), lambda b, i, s: (b, 0, kcol)),
                  pl.BlockSpec((1, C, 128), lambda b, i, s: (b, 0, vcol))],
        out_specs=pl.BlockSpec((1, qb * blk, ATT_HEADS * HEAD_DIM), lambda b, i, s: (b, i, 0)),
    )
    return pl.pallas_call(
        functools.partial(_win_attn_kernel, seq_len=S),
        grid_spec=grid_spec,
        out_shape=jax.ShapeDtypeStruct((B, S, ATT_HEADS * HEAD_DIM), F32),
        compiler_params=_cparams("parallel", "parallel"),
        name="window_attention",
    )(sink.astype(F32), at_x, at_x, at_x, at_x, at_x, at_x, at_x, at_c, at_c)


def _ctx_attn_kernel(sink_ref, q_ref, kx_ref, vx_ref, o_ref):
    q = q_ref[0].astype(BF16)
    kx = kx_ref[0].astype(BF16)
    vx = vx_ref[0].astype(BF16)
    g = ATT_HEADS // ATT_KV_HEADS
    outs = []
    for h in range(ATT_HEADS):
        kv = h // g
        qs = q[:, h * HEAD_DIM:(h + 1) * HEAD_DIM]
        ksl = slice(kv * HEAD_DIM, (kv + 1) * HEAD_DIM)
        (p_ctx,), inv = _softmax_parts(qs, [(kx[:, ksl], None)], sink_ref[h])
        outs.append(jnp.dot(p_ctx.astype(BF16), vx[:, ksl], preferred_element_type=F32) * inv)
    o_ref[0] = jnp.concatenate(outs, axis=-1)


def context_attention(at_c, sink):
    B, C, _ = at_c.shape
    kcol = QK_COLS // 128 - 1
    grid_spec = pltpu.PrefetchScalarGridSpec(
        num_scalar_prefetch=1,
        grid=(B,),
        in_specs=[pl.BlockSpec((1, C, ATT_HEADS * HEAD_DIM), lambda b, s: (b, 0, 0)),
                  pl.BlockSpec((1, C, 128), lambda b, s: (b, 0, kcol)),
                  pl.BlockSpec((1, C, 128), lambda b, s: (b, 0, kcol + 1))],
        out_specs=pl.BlockSpec((1, C, ATT_HEADS * HEAD_DIM), lambda b, s: (b, 0, 0)),
    )
    return pl.pallas_call(
        _ctx_attn_kernel,
        grid_spec=grid_spec,
        out_shape=jax.ShapeDtypeStruct((B, C, ATT_HEADS * HEAD_DIM), F32),
        compiler_params=_cparams("parallel"),
        name="context_attention",
    )(sink.astype(F32), at_c, at_c, at_c)


def _out_proj_kernel(y0_ref, y1_ref, y2_ref, y3_ref, h_ref, g1_ref, gng_ref, w_ref,
                     n2g_ref, sh_ref, sc_ref, wr_ref, br_ref, ho_ref, lg_ref):
    parts = []
    for k, y_ref in enumerate((y0_ref, y1_ref, y2_ref, y3_ref)):
        y = y_ref[0]
        ms = jnp.mean(y * y, axis=-1, keepdims=True)
        yn = y * lax.rsqrt(ms + EPS) * gng_ref[:, k * D_GROUP:(k + 1) * D_GROUP]
        parts.append(yn.astype(BF16))
    yn = jnp.concatenate(parts, axis=-1)
    proj = jnp.dot(yn, w_ref[...], preferred_element_type=F32)
    h = h_ref[0] + g1_ref[0] * proj
    ho_ref[0] = h
    ms = jnp.mean(h * h, axis=-1, keepdims=True)
    n = h * lax.rsqrt(ms + EPS) * n2g_ref[...]
    n = n * (1.0 + sc_ref[0]) + sh_ref[0]
    lg_ref[0] = jnp.dot(n.astype(BF16), wr_ref[...], preferred_element_type=F32) + br_ref[...]


def out_proj(ys, h, g1, gng, w_out, n2g, sh2, sc2, w_router, b_router, tm):
    B, L, D = h.shape
    row3 = lambda n: pl.BlockSpec((1, tm, n), lambda b, i: (b, i, 0))
    mod = pl.BlockSpec((1, 1, D), lambda b, i: (b, 0, 0))
    full = lambda r, c: pl.BlockSpec((r, c), lambda b, i: (0, 0))
    return pl.pallas_call(
        _out_proj_kernel,
        grid=(B, L // tm),
        in_specs=[row3(D_GROUP)] * 4 + [row3(D), mod, full(1, D), full(D, D), full(1, D), mod, mod,
                                        full(D, ROUTER_COLS), full(1, ROUTER_COLS)],
        out_specs=[row3(D), row3(ROUTER_COLS)],
        out_shape=[jax.ShapeDtypeStruct((B, L, D), F32), jax.ShapeDtypeStruct((B, L, ROUTER_COLS), F32)],
        compiler_params=_cparams("parallel", "parallel"),
        name="out_proj",
    )(*ys, h, g1, gng.reshape(1, D), w_out, n2g.reshape(1, D), sh2, sc2, w_router, b_router)


N_PAIRS = EXP_PER_GROUP * (EXP_PER_GROUP - 1) // 2
N_CLASSES = N_GROUPS * N_PAIRS
ROUTE_TOKENS = 512
INFO_CLASS, INFO_RANK, INFO_WA, INFO_WB = 0, 1, 2, 3


SUBLANES = 8


def _route_kernel(lg_ref, info_ref, cnt_ref, ids_ref, run):
    i = pl.program_id(0)
    tb = lg_ref.shape[0]

    @pl.when(i == 0)
    def _():
        run[...] = jnp.zeros_like(run)

    lg = lg_ref[...]
    li = lax.broadcasted_iota(jnp.int32, lg.shape, 1)
    big = jnp.int32(ROUTER_COLS)

    def first_argmax(vals):
        m = jnp.max(vals, axis=-1, keepdims=True)
        return m, jnp.min(jnp.where(vals == m, li, big), axis=-1, keepdims=True)

    gl = jnp.where(li < N_GROUPS, lg, NEG_INF)
    gmax, g_idx = first_argmax(gl)
    g_prob = 1.0 / jnp.sum(jnp.exp(gl - gmax), axis=-1, keepdims=True)
    lo = N_GROUPS + EXP_PER_GROUP * g_idx
    el = jnp.where((li >= lo) & (li < lo + EXP_PER_GROUP), lg, NEG_INF)
    m1, i1 = first_argmax(el)
    m2, i2 = first_argmax(jnp.where(li == i1, NEG_INF, el))
    e2 = jnp.exp(m2 - m1)
    w1 = g_prob / (1.0 + e2)
    w2 = g_prob * e2 / (1.0 + e2)
    j1 = i1 - lo
    j2 = i2 - lo
    a = jnp.minimum(j1, j2)
    b = jnp.maximum(j1, j2)
    cls = g_idx * N_PAIRS + ((a * (2 * EXP_PER_GROUP - 1 - a)) >> 1) + (b - a - 1)
    w_a = jnp.where(j1 < j2, w1, w2)
    w_b = jnp.where(j1 < j2, w2, w1)

    hit = li == cls
    onehot = jnp.where(hit, 1.0, 0.0)
    r_i = lax.broadcasted_iota(jnp.int32, (tb, tb), 0)
    c_i = lax.broadcasted_iota(jnp.int32, (tb, tb), 1)
    below = jnp.where(c_i < r_i, 1.0, 0.0).astype(BF16)
    before = jnp.dot(below, onehot.astype(BF16), preferred_element_type=F32)
    rank = jnp.sum(jnp.where(hit, before + run[...], 0.0), axis=-1, keepdims=True)
    run[...] = run[...] + jnp.sum(onehot, axis=0, keepdims=True)
    cnt_ref[...] = run[...]
    info = jnp.where(li == INFO_CLASS, cls.astype(F32), 0.0)
    info = jnp.where(li == INFO_RANK, rank, info)
    info = jnp.where(li == INFO_WA, w_a, info)
    info = jnp.where(li == INFO_WB, w_b, info)
    info_ref[...] = info
    ids_ref[0] = info.T[:SUBLANES].astype(jnp.int32)


def route_tokens(logits):
    T = logits.shape[0]
    tb = ROUTE_TOKENS
    return pl.pallas_call(
        _route_kernel,
        grid=(T // tb,),
        in_specs=[pl.BlockSpec((tb, ROUTER_COLS), lambda i: (i, 0))],
        out_specs=[pl.BlockSpec((tb, ROUTER_COLS), lambda i: (i, 0)),
                   pl.BlockSpec((1, ROUTER_COLS), lambda i: (0, 0)),
                   pl.BlockSpec((1, SUBLANES, tb), lambda i: (i, 0, 0))],
        out_shape=[jax.ShapeDtypeStruct((T, ROUTER_COLS), F32), jax.ShapeDtypeStruct((1, ROUTER_COLS), F32),
                   jax.ShapeDtypeStruct((T // tb, SUBLANES, tb), jnp.int32)],
        scratch_shapes=[pltpu.VMEM((1, ROUTER_COLS), F32)],
        compiler_params=_cparams("arbitrary"),
        name="moe_route",
    )(logits)


def _pair_tables():
    a_tab, b_tab = [], []
    for g in range(N_GROUPS):
        for a in range(EXP_PER_GROUP):
            for b in range(a + 1, EXP_PER_GROUP):
                a_tab.append(g * EXP_PER_GROUP + a)
                b_tab.append(g * EXP_PER_GROUP + b)
    return jnp.array(a_tab, jnp.int32), jnp.array(b_tab, jnp.int32)


def _slot_kernel(ids_ref, start_ref, dest_ref):
    cls = ids_ref[0, INFO_CLASS:INFO_CLASS + 1, :]
    rank = ids_ref[0, INFO_RANK:INFO_RANK + 1, :]
    ci = lax.broadcasted_iota(jnp.int32, (ROUTER_COLS, cls.shape[1]), 0)
    start = jnp.sum(jnp.where(ci == cls, start_ref[...], 0), axis=0, keepdims=True)
    dest_ref[0] = jnp.broadcast_to(start + rank, dest_ref.shape[1:])


def slot_plan(ids, counts, n_blocks):
    nt, _, tb = ids.shape
    cnt = counts[0, :N_CLASSES].astype(jnp.int32)
    padded = (cnt + MOE_BLOCK - 1) // MOE_BLOCK * MOE_BLOCK
    pad_end = jnp.cumsum(padded)
    class_start = jnp.zeros((ROUTER_COLS, 1), jnp.int32).at[:N_CLASSES, 0].set(pad_end - padded)
    dest = pl.pallas_call(
        _slot_kernel,
        grid=(nt,),
        in_specs=[pl.BlockSpec((1, SUBLANES, tb), lambda i: (i, 0, 0)),
                  pl.BlockSpec((ROUTER_COLS, 1), lambda i: (0, 0))],
        out_specs=pl.BlockSpec((1, SUBLANES, tb), lambda i: (i, 0, 0)),
        out_shape=jax.ShapeDtypeStruct((nt, SUBLANES, tb), jnp.int32),
        compiler_params=_cparams("parallel"),
        name="moe_slots",
    )(ids, class_start)[:, 0, :].reshape(nt * tb)
    n_used = (pad_end[-1] // MOE_BLOCK).astype(jnp.int32).reshape(1)
    blk_cls = jnp.minimum(jnp.searchsorted(pad_end, jnp.arange(n_blocks) * MOE_BLOCK, side="right"),
                          N_CLASSES - 1)
    a_tab, b_tab = _pair_tables()
    return dest, a_tab[blk_cls], b_tab[blk_cls], n_used


DISPATCH_TOKENS = 256


def _wait_rows(buf, sem):
    pltpu.make_async_copy(buf, buf, sem).wait()


DMA_UNROLL = 8
TOKEN_TILE_ROWS = 8
PAIR_ROWS = TOP_K * TOKEN_TILE_ROWS
PAIR_PITCH = PAIR_ROWS + 4


def _store_token_tiles(tiles_ref, offset, pitch, x):
    n = x.shape[0]
    for j in range(x.shape[1] // LANES):
        tiles_ref[pl.ds(offset + j, n, stride=pitch), :] = x[:, j * LANES:(j + 1) * LANES]


def _load_token_tiles(tiles_ref, offset, pitch, n, width):
    return jnp.concatenate([tiles_ref[pl.ds(offset + j, n, stride=pitch), :] for j in range(width // LANES)],
                           axis=-1)


def _dispatch_kernel(dest_ref, hx_ref, hc_ref, g_ref, shx_ref, scx_ref, shc_ref, scc_ref, zeros_hbm,
                     xs_hbm, rows, sems, *, n_latent_blocks):
    del zeros_hbm
    i = pl.program_id(0)
    n = pl.num_programs(0)
    slot = i % 2
    tb = hx_ref.shape[0]

    @pl.when(i >= 2)
    def _():
        _wait_rows(rows.at[slot], sems.at[slot])

    def normed(h_ref, sh_ref, sc_ref):
        h = h_ref[...]
        ms = jnp.mean(h * h, axis=-1, keepdims=True)
        return h * lax.rsqrt(ms + EPS) * g_ref[...] * (1.0 + sc_ref[0]) + sh_ref[0]

    @pl.when(i < n_latent_blocks)
    def _():
        _store_token_tiles(rows.at[slot], 0, TOKEN_TILE_ROWS, normed(hx_ref, shx_ref, scx_ref))

    @pl.when(i >= n_latent_blocks)
    def _():
        _store_token_tiles(rows.at[slot], 0, TOKEN_TILE_ROWS, normed(hc_ref, shc_ref, scc_ref))

    def body(r, carry):
        dst = pl.multiple_of(dest_ref[0, 0, r] * TOKEN_TILE_ROWS, TOKEN_TILE_ROWS)
        pltpu.make_async_copy(rows.at[slot, pl.ds(r * TOKEN_TILE_ROWS, TOKEN_TILE_ROWS)],
                              xs_hbm.at[pl.ds(dst, TOKEN_TILE_ROWS)], sems.at[slot]).start()
        return carry
    lax.fori_loop(0, tb, body, 0, unroll=DMA_UNROLL)

    @pl.when(i == n - 1)
    def _():
        _wait_rows(rows.at[slot], sems.at[slot])

        @pl.when(n >= 2)
        def _():
            _wait_rows(rows.at[1 - slot], sems.at[1 - slot])


def moe_dispatch(h_x, h_c, dest, n_blocks, n2g, mod_x, mod_c, tokens_per_batch):
    Tx, D = h_x.shape
    tb = DISPATCH_TOKENS
    nxb = Tx // tb
    if h_c is None:
        h_c, mod_c, ncb = h_x, mod_x, 0
    else:
        ncb = h_c.shape[0] // tb
    per_b = tokens_per_batch // tb
    P = n_blocks * MOE_BLOCK
    tile_rows = D // LANES
    assert tile_rows == TOKEN_TILE_ROWS
    xi = lambda i: jnp.minimum(i, nxb - 1)
    ci = lambda i: jnp.maximum(i - nxb, 0)
    modx = pl.BlockSpec((1, 1, D), lambda i: (xi(i) // per_b, 0, 0))
    modc = pl.BlockSpec((1, 1, D), lambda i: (0, 0, 0))
    return pl.pallas_call(
        functools.partial(_dispatch_kernel, n_latent_blocks=nxb),
        grid=(nxb + ncb,),
        in_specs=[pl.BlockSpec((1, 1, tb), lambda i: (i, 0, 0), memory_space=pltpu.SMEM),
                  pl.BlockSpec((tb, D), lambda i: (xi(i), 0)),
                  pl.BlockSpec((tb, D), lambda i: (ci(i), 0)),
                  pl.BlockSpec((1, D), lambda i: (0, 0)),
                  modx, modx, modc, modc,
                  pl.BlockSpec(memory_space=pl.ANY)],
        out_specs=pl.BlockSpec(memory_space=pl.ANY),
        out_shape=jax.ShapeDtypeStruct((P * tile_rows, LANES), F32),
        scratch_shapes=[pltpu.VMEM((2, tb * tile_rows, LANES), F32), pltpu.SemaphoreType.DMA((2,))],
        input_output_aliases={8: 0},
        compiler_params=_cparams("arbitrary"),
        name="moe_dispatch",
    )(dest.reshape(-1, 1, tb), h_x, h_c, n2g.reshape(1, D), mod_x[0], mod_x[1], mod_c[0], mod_c[1],
      jnp.zeros((P * tile_rows, LANES), F32))


def _expert_pair_kernel(ea_ref, eb_ref, nused_ref, xs_ref, wga_ref, wua_ref, wda_ref, wgb_ref, wub_ref, wdb_ref,
                        o_ref):
    del ea_ref, eb_ref
    i = pl.program_id(0)
    D = wga_ref.shape[1]

    @pl.when(i < nused_ref[0])
    def _():
        xb = _load_token_tiles(xs_ref, 0, TOKEN_TILE_ROWS, MOE_BLOCK, D).astype(BF16)
        for e, (wg_ref, wu_ref, wd_ref) in enumerate(((wga_ref, wua_ref, wda_ref), (wgb_ref, wub_ref, wdb_ref))):
            gate = jnp.dot(xb, wg_ref[0], preferred_element_type=F32)
            up = jnp.dot(xb, wu_ref[0], preferred_element_type=F32)
            hid = (gate * jax.nn.sigmoid(gate) * up).astype(BF16)
            out = jnp.dot(hid, wd_ref[0], preferred_element_type=F32)
            _store_token_tiles(o_ref, e * TOKEN_TILE_ROWS, PAIR_PITCH, out)
        for j in range(PAIR_ROWS, PAIR_PITCH):
            o_ref[pl.ds(j, MOE_BLOCK, stride=PAIR_PITCH), :] = jnp.zeros((MOE_BLOCK, LANES), F32)

    @pl.when(i >= nused_ref[0])
    def _():
        o_ref[...] = jnp.zeros_like(o_ref)


def expert_pairs(xs, blk_a, blk_b, n_used, w_gate, w_up, w_down):
    D, DE = w_gate.shape[1:]
    P = xs.shape[0] // TOKEN_TILE_ROWS
    n_blocks = P // MOE_BLOCK
    wspec = lambda shape, which: pl.BlockSpec(shape, lambda i, ea, eb, nu: ((ea, eb)[which][i], 0, 0))
    grid_spec = pltpu.PrefetchScalarGridSpec(
        num_scalar_prefetch=3,
        grid=(n_blocks,),
        in_specs=[pl.BlockSpec((MOE_BLOCK * TOKEN_TILE_ROWS, LANES), lambda i, ea, eb, nu: (i, 0)),
                  wspec((1, D, DE), 0), wspec((1, D, DE), 0), wspec((1, DE, D), 0),
                  wspec((1, D, DE), 1), wspec((1, D, DE), 1), wspec((1, DE, D), 1)],
        out_specs=pl.BlockSpec((MOE_BLOCK * PAIR_PITCH, LANES), lambda i, ea, eb, nu: (i, 0)),
    )
    return pl.pallas_call(
        _expert_pair_kernel,
        grid_spec=grid_spec,
        out_shape=jax.ShapeDtypeStruct((P * PAIR_PITCH, LANES), F32),
        compiler_params=_cparams("arbitrary"),
        name="moe_experts",
    )(blk_a, blk_b, n_used, xs, w_gate, w_up, w_down, w_gate, w_up, w_down)


def _gather_pairs(idx_ref, src_hbm, buf, sem, n_tokens):
    def body(r, carry):
        src = idx_ref[0, 0, r] * PAIR_PITCH
        pltpu.make_async_copy(src_hbm.at[pl.ds(src, PAIR_ROWS)], buf.at[pl.ds(r * PAIR_PITCH, PAIR_ROWS)], sem).start()
        return carry
    lax.fori_loop(0, n_tokens, body, 0, unroll=DMA_UNROLL)


def _collect_kernel(dest_ref, dest_next_ref, o_hbm, info_ref, h_ref, g2_ref, fg_ref, out_ref, obuf, sems, *,
                    final_norm):
    i = pl.program_id(0)
    n = pl.num_programs(0)
    slot = i % 2
    tb, D = h_ref.shape

    @pl.when(i == 0)
    def _():
        _gather_pairs(dest_ref, o_hbm, obuf.at[0], sems.at[0], tb)

    @pl.when(i + 1 < n)
    def _():
        _gather_pairs(dest_next_ref, o_hbm, obuf.at[1 - slot], sems.at[1 - slot], tb)

    _wait_rows(obuf.at[slot, pl.ds(0, tb * PAIR_ROWS)], sems.at[slot])
    w_a = info_ref[:, INFO_WA:INFO_WA + 1]
    w_b = info_ref[:, INFO_WB:INFO_WB + 1]
    m = (w_a * _load_token_tiles(obuf.at[slot], 0, PAIR_PITCH, tb, D)
         + w_b * _load_token_tiles(obuf.at[slot], TOKEN_TILE_ROWS, PAIR_PITCH, tb, D))
    h = h_ref[...] + g2_ref[0] * m
    if final_norm:
        ms = jnp.mean(h * h, axis=-1, keepdims=True)
        h = h * lax.rsqrt(ms + EPS) * fg_ref[...]
    out_ref[...] = h


def moe_collect(o_sorted, dest, info, block_offset, h_tokens, g2, tokens_per_batch, final_g, final_norm):
    T, D = h_tokens.shape
    tb = DISPATCH_TOKENS
    nt = T // tb
    per_b = tokens_per_batch // tb
    last = block_offset + nt - 1
    dest3 = dest.reshape(-1, 1, tb)
    return pl.pallas_call(
        functools.partial(_collect_kernel, final_norm=final_norm),
        grid=(nt,),
        in_specs=[pl.BlockSpec((1, 1, tb), lambda i: (block_offset + i, 0, 0), memory_space=pltpu.SMEM),
                  pl.BlockSpec((1, 1, tb), lambda i: (jnp.minimum(block_offset + i + 1, last), 0, 0),
                               memory_space=pltpu.SMEM),
                  pl.BlockSpec(memory_space=pl.ANY),
                  pl.BlockSpec((tb, ROUTER_COLS), lambda i: (block_offset + i, 0)),
                  pl.BlockSpec((tb, D), lambda i: (i, 0)),
                  pl.BlockSpec((1, 1, D), lambda i: (i // per_b, 0, 0)),
                  pl.BlockSpec((1, D), lambda i: (0, 0))],
        out_specs=pl.BlockSpec((tb, D), lambda i: (i, 0)),
        out_shape=jax.ShapeDtypeStruct((T, D), F32),
        scratch_shapes=[pltpu.VMEM((2, tb * PAIR_PITCH, LANES), F32), pltpu.SemaphoreType.DMA((2,))],
        compiler_params=_cparams("arbitrary"),
        name="moe_collect",
    )(dest3, dest3, o_sorted, info, h_tokens, g2, final_g.reshape(1, D))


CONV_MARGIN = 16


def _time_chunk(L):
    return min(L, 256)


LANES = 128


def _zero_margins(pad_ref, L):
    zeros = jnp.zeros((CONV_MARGIN, LANES), F32)
    for s in range(pad_ref.shape[0]):
        pad_ref[s, pl.ds(0, CONV_MARGIN), :] = zeros
        pad_ref[s, pl.ds(CONV_MARGIN + L, CONV_MARGIN), :] = zeros


def _dw_conv_slab(pad_ref, s, base, T, w_ref, b_ref, col, taps, pad_left):
    acc = jnp.broadcast_to(b_ref[:, col:col + LANES], (T, LANES))
    for k in range(taps):
        acc = acc + w_ref[k:k + 1, col:col + LANES] * pad_ref[s, pl.ds(base + (CONV_MARGIN - pad_left + k), T), :]
    return acc


def _conformer_kernel(u_ref, w_ref, b_ref, g_ref, beta_ref, o_ref, ypad):
    L = o_ref.shape[1]
    T = _time_chunk(L)
    C = D_GROUP
    n_slabs = C // LANES
    pad = (CONF_KERNEL - 1) // 2
    _zero_margins(ypad, L)

    def glu(j, carry):
        base = pl.multiple_of(j * T, T)
        for s in range(n_slabs):
            a = u_ref[0, pl.ds(base, T), s * LANES:(s + 1) * LANES]
            gate = u_ref[0, pl.ds(base, T), C + s * LANES:C + (s + 1) * LANES]
            ypad[s, pl.ds(CONV_MARGIN + base, T), :] = a * jax.nn.sigmoid(gate)
        return carry
    lax.fori_loop(0, L // T, glu, 0)

    def conv(j, carry):
        base = pl.multiple_of(j * T, T)
        acc = jnp.concatenate([_dw_conv_slab(ypad, s, base, T, w_ref, b_ref, s * LANES, CONF_KERNEL, pad)
                               for s in range(n_slabs)], axis=-1)
        mu = jnp.mean(acc, axis=-1, keepdims=True)
        cen = acc - mu
        var = jnp.mean(cen * cen, axis=-1, keepdims=True)
        y = cen * lax.rsqrt(var + EPS) * g_ref[...] + beta_ref[...]
        o_ref[0, pl.ds(base, T), :] = y * jax.nn.sigmoid(y)
        return carry
    lax.fori_loop(0, L // T, conv, 0)


def conformer_conv(u, w, b, ln_g, ln_b):
    B, L, _ = u.shape
    C = D_GROUP
    vec = pl.BlockSpec((1, C), lambda i: (0, 0))
    return pl.pallas_call(
        _conformer_kernel,
        grid=(B,),
        in_specs=[pl.BlockSpec((1, L, 2 * C), lambda i: (i, 0, 0)),
                  pl.BlockSpec((CONF_KERNEL, C), lambda i: (0, 0)), vec, vec, vec],
        out_specs=pl.BlockSpec((1, L, C), lambda i: (i, 0, 0)),
        out_shape=jax.ShapeDtypeStruct((B, L, C), F32),
        scratch_shapes=[pltpu.VMEM((C // LANES, L + 2 * CONV_MARGIN, LANES), F32)],
        compiler_params=_cparams("parallel"),
        name="conformer_conv",
    )(u, w, b.reshape(1, C), ln_g.reshape(1, C), ln_b.reshape(1, C))


def _gelu_tanh(x):
    return 0.5 * x * (1.0 + jnp.tanh(math.sqrt(2.0 / math.pi) * (x + 0.044715 * (x * x * x))))


def _lru_kernel(uc_ref, ux_ref, cw_ref, cb_ref, wcat_ref, bcat_ref, lam_ref, *rest, need_ctx):
    if need_ctx:
        oc_ref, ox_ref, cpad, xpad, a_s, b_s, yx, yc = rest
    else:
        ox_ref, cpad, xpad, a_s, b_s, yx = rest
        oc_ref = yc = None
    C = D_GROUP
    n_slabs = C // LANES
    Lc = uc_ref.shape[1]
    Lx = ux_ref.shape[1]
    pad_l = (LRU_CONV - 1) // 2

    def fill(pad_ref, u_ref, L):
        T = _time_chunk(L)
        _zero_margins(pad_ref, L)

        def body(j, carry):
            base = pl.multiple_of(j * T, T)
            for s in range(n_slabs):
                pad_ref[s, pl.ds(CONV_MARGIN + base, T), :] = u_ref[0, pl.ds(base, T),
                                                                    C + s * LANES:C + (s + 1) * LANES]
            return carry
        lax.fori_loop(0, L // T, body, 0)

    fill(cpad, uc_ref, Lc)
    fill(xpad, ux_ref, Lx)

    def coeffs(pad_ref, base, T, d):
        x = jnp.concatenate([_dw_conv_slab(pad_ref, s, base, T, cw_ref, cb_ref, s * LANES, LRU_CONV, pad_l)
                             for s in range(n_slabs)], axis=-1)
        t = jnp.tanh(jnp.dot(x.astype(BF16), wcat_ref[:, 2 * d * C:2 * (d + 1) * C],
                             preferred_element_type=F32) + bcat_ref[:, 2 * d * C:2 * (d + 1) * C])
        i = 0.5 * t[:, C:] + 0.5
        z = -lam_ref[d:d + 1, :]
        softplus = jnp.maximum(z, 0.0) + jnp.log(1.0 + jnp.exp(-jnp.abs(z)))
        half_rate = (-0.5 * LRU_C) * softplus
        a = jnp.exp(half_rate * t[:, :C] + half_rate)
        b = jnp.sqrt(1.0 - a * a) * (i * x)
        for s in range(n_slabs):
            a_s[d * n_slabs + s, pl.ds(0, T), :] = a[:, s * LANES:(s + 1) * LANES]
            b_s[d * n_slabs + s, pl.ds(0, T), :] = b[:, s * LANES:(s + 1) * LANES]

    def run(pad_ref, L, h, y_ref):
        T = _time_chunk(L)
        n = L // T

        def chunk(j, h):
            base_f = pl.multiple_of(j * T, T)
            base_b = pl.multiple_of((n - 1 - j) * T, T)
            coeffs(pad_ref, base_f, T, 0)
            coeffs(pad_ref, base_b, T, 1)

            def step(t, h):
                new = []
                for d, (base, row) in enumerate(((base_f, t), (base_b, T - 1 - t))):
                    for s in range(n_slabs):
                        k = d * n_slabs + s
                        hs = a_s[k, pl.ds(row, 1), :] * h[k] + b_s[k, pl.ds(row, 1), :]
                        if y_ref is not None:
                            y_ref[k, pl.ds(base + row, 1), :] = hs
                        new.append(hs)
                return tuple(new)
            return lax.fori_loop(0, T, step, h, unroll=8)
        return lax.fori_loop(0, n, chunk, h)

    h = tuple(jnp.zeros((1, LANES), F32) for _ in range(2 * n_slabs))
    h = run(cpad, Lc, h, yc)
    run(xpad, Lx, h, yx)

    def finish(u_ref, y_ref, o_ref, L):
        T = _time_chunk(L)

        def body(j, carry):
            base = pl.multiple_of(j * T, T)
            y = jnp.concatenate([y_ref[s, pl.ds(base, T), :] + y_ref[n_slabs + s, pl.ds(base, T), :]
                                 for s in range(n_slabs)], axis=-1)
            o_ref[0, pl.ds(base, T), :] = _gelu_tanh(u_ref[0, pl.ds(base, T), :C]) * y
            return carry
        lax.fori_loop(0, L // T, body, 0)

    finish(ux_ref, yx, ox_ref, Lx)
    if need_ctx:
        finish(uc_ref, yc, oc_ref, Lc)


def _block_diag(w):
    H, n, _ = w.shape
    eye = jnp.eye(H, dtype=w.dtype)
    return (eye[:, None, :, None] * w[:, :, None, :]).reshape(H * n, H * n)


def rglru_mixer(uc, ux, lp, need_ctx):
    B, Lc, _ = uc.shape
    Lx = ux.shape[1]
    C = D_GROUP
    wcat = (0.5 * jnp.concatenate([_block_diag(lp["lru_wa"][0]), _block_diag(lp["lru_wx"][0]),
                                   _block_diag(lp["lru_wa"][1]), _block_diag(lp["lru_wx"][1])], axis=1)).astype(BF16)
    bcat = 0.5 * jnp.concatenate([lp["lru_ba"][0], lp["lru_bx"][0], lp["lru_ba"][1], lp["lru_bx"][1]]).reshape(1, 4 * C)
    full = lambda r, c: pl.BlockSpec((r, c), lambda i: (0, 0))
    seq = lambda L, n: pl.BlockSpec((1, L, n), lambda i: (i, 0, 0))
    out_specs = [seq(Lx, C)]
    out_shape = [jax.ShapeDtypeStruct((B, Lx, C), F32)]
    if need_ctx:
        out_specs = [seq(Lc, C)] + out_specs
        out_shape = [jax.ShapeDtypeStruct((B, Lc, C), F32)] + out_shape
    T = _time_chunk(Lx)
    slab = lambda rows, n=1: pltpu.VMEM((n * C // LANES, rows, LANES), F32)
    scratch = [slab(Lc + 2 * CONV_MARGIN), slab(Lx + 2 * CONV_MARGIN), slab(T, 2), slab(T, 2), slab(Lx, 2)]
    if need_ctx:
        scratch.append(slab(Lc, 2))
    res = pl.pallas_call(
        functools.partial(_lru_kernel, need_ctx=need_ctx),
        grid=(B,),
        in_specs=[seq(Lc, 2 * C), seq(Lx, 2 * C), full(LRU_CONV, C), full(1, C), full(C, 4 * C),
                  full(1, 4 * C), full(2, C)],
        out_specs=out_specs,
        out_shape=out_shape,
        scratch_shapes=scratch,
        compiler_params=_cparams("parallel"),
        name="rglru",
    )(uc, ux, lp["lru_conv_w"], lp["lru_conv_b"].reshape(1, C), wcat, bcat, lp["lru_lambda"])
    if need_ctx:
        return res[0], res[1]
    return None, res[0]


HY_SHORT = 3


def _short_conv(pad_ref, base, T, w_ref, b_ref, c0, c1):
    return jnp.concatenate([_dw_conv_slab(pad_ref, col // LANES, base, T, w_ref, b_ref, col, HY_SHORT, 1)
                            for col in range(c0, c1, LANES)], axis=-1)


def _fill_padded(pad_ref, u_ref, L, T):
    _zero_margins(pad_ref, L)

    def body(j, carry):
        base = pl.multiple_of(j * T, T)
        for s in range(pad_ref.shape[0]):
            pad_ref[s, pl.ds(CONV_MARGIN + base, T), :] = u_ref[0, pl.ds(base, T), s * LANES:(s + 1) * LANES]
        return carry
    lax.fori_loop(0, L // T, body, 0)


def _hyena_pre_kernel(u_ref, w_ref, b_ref, z_ref, upad):
    L = u_ref.shape[1]
    T = _time_chunk(L)
    C = D_GROUP
    _fill_padded(upad, u_ref, L, T)

    def body(j, carry):
        base = pl.multiple_of(j * T, T)
        x1 = _short_conv(upad, base, T, w_ref, b_ref, C, 2 * C)
        v = _short_conv(upad, base, T, w_ref, b_ref, 2 * C, 3 * C)
        z_ref[pl.ds(base, T), :] = (x1 * v).astype(BF16)
        return carry
    lax.fori_loop(0, L // T, body, 0)


def _hyena_post_kernel(u_ref, y_ref, w_ref, b_ref, bias_ref, o_ref, upad):
    L = u_ref.shape[1]
    T = _time_chunk(L)
    C = D_GROUP
    _fill_padded(upad, u_ref, L, T)

    def body(j, carry):
        base = pl.multiple_of(j * T, T)
        x0 = _short_conv(upad, base, T, w_ref, b_ref, 0, C)
        x1 = _short_conv(upad, base, T, w_ref, b_ref, C, 2 * C)
        v = _short_conv(upad, base, T, w_ref, b_ref, 2 * C, 3 * C)
        o_ref[0, pl.ds(base, T), :] = x0 * (y_ref[pl.ds(base, T), :] + (x1 * v) * bias_ref[...])
        return carry
    lax.fori_loop(0, L // T, body, 0)


def _spectrum_kernel(f_ref, z_ref, ha_ref, hb_ref, hc_ref, y_ref):
    tf = ha_ref.shape[0]
    acc = jnp.dot(f_ref[...], z_ref[...], preferred_element_type=F32)
    zr = acc[:tf]
    zi = acc[tf:]
    y_ref[:tf, :] = (zr * ha_ref[...] - zi * hb_ref[...]).astype(BF16)
    y_ref[tf:, :] = (zr * hb_ref[...] + zi * hc_ref[...]).astype(BF16)


def _idft_kernel(f_ref, y_ref, o_ref):
    o_ref[...] = jnp.dot(f_ref[...], y_ref[...], preferred_element_type=F32)


def dft_tables(L):
    N = 2 * L
    tf = min(256, L)
    k = jnp.arange(L, dtype=jnp.int32)
    n = jnp.arange(L, dtype=jnp.int32)
    ang = (2.0 * math.pi / N) * ((k[:, None] * n[None, :]) % N).astype(F32)
    cos = jnp.cos(ang)
    sin = jnp.sin(ang)
    nyq = jnp.where(n % 2 == 0, 1.0, -1.0).astype(F32)
    f_re = cos
    f_im = (-sin).at[0].set(nyq)
    fwd = jnp.stack([f_re.reshape(L // tf, tf, L), f_im.reshape(L // tf, tf, L)], axis=1).reshape(N, L)
    ck = jnp.where(k == 0, 1.0, 2.0).astype(F32)[:, None] / N
    i_re = cos * ck
    i_im = (-sin * ck).at[0].set(nyq / N)
    inv = jnp.stack([i_re.reshape(L // tf, tf, L), i_im.reshape(L // tf, tf, L)], axis=1).reshape(N, L).T
    return fwd.astype(BF16), inv.astype(BF16)


def filter_spectrum(h_fwd, h_bwd):
    L, C = h_fwd.shape
    k = jnp.concatenate([h_fwd, jnp.zeros((1, C), F32), h_bwd[1:][::-1]], axis=0)
    hf = jnp.fft.rfft(k, axis=0)
    hr = jnp.real(hf)
    hi = jnp.imag(hf)
    a = hr[:L]
    b = hi[:L].at[0].set(0.0)
    c = hr[:L].at[0].set(hr[L])
    return a, b, c


def hyena_mixer(u, lp, tables):
    B, L, _ = u.shape
    C = D_GROUP
    N = 2 * L
    fwd, inv = tables
    tf = min(256, L)
    T = _time_chunk(L)
    w, bsh = lp["hy_short_w"], lp["hy_short_b"].reshape(1, 3 * C)
    z2 = pl.pallas_call(
        _hyena_pre_kernel,
        grid=(B,),
        in_specs=[pl.BlockSpec((1, L, 3 * C), lambda b: (b, 0, 0)),
                  pl.BlockSpec((HY_SHORT, 3 * C), lambda b: (0, 0)),
                  pl.BlockSpec((1, 3 * C), lambda b: (0, 0))],
        out_specs=pl.BlockSpec((L, C), lambda b: (0, b)),
        out_shape=jax.ShapeDtypeStruct((L, B * C), BF16),
        scratch_shapes=[pltpu.VMEM((3 * C // LANES, L + 2 * CONV_MARGIN, LANES), F32)],
        compiler_params=_cparams("parallel"),
        name="hyena_pre",
    )(u, w, bsh)

    h_fwd, h_bwd = _hyena_filters(L, lp)
    tn = 2 * C
    ha, hb, hc = [jnp.tile(t, (1, tn // C)) for t in filter_spectrum(h_fwd, h_bwd)]
    hspec = pl.BlockSpec((tf, tn), lambda i, j: (i, 0))
    y2 = pl.pallas_call(
        _spectrum_kernel,
        grid=(L // tf, B * C // tn),
        in_specs=[pl.BlockSpec((2 * tf, L), lambda i, j: (i, 0)),
                  pl.BlockSpec((L, tn), lambda i, j: (0, j)), hspec, hspec, hspec],
        out_specs=pl.BlockSpec((2 * tf, tn), lambda i, j: (i, j)),
        out_shape=jax.ShapeDtypeStruct((N, B * C), BF16),
        compiler_params=_cparams("parallel", "parallel"),
        name="hyena_spectrum",
    )(fwd, z2, ha, hb, hc)

    tl = min(256, L)
    yt = pl.pallas_call(
        _idft_kernel,
        grid=(L // tl, B * C // tn),
        in_specs=[pl.BlockSpec((tl, N), lambda i, j: (i, 0)),
                  pl.BlockSpec((N, tn), lambda i, j: (0, j))],
        out_specs=pl.BlockSpec((tl, tn), lambda i, j: (i, j)),
        out_shape=jax.ShapeDtypeStruct((L, B * C), F32),
        compiler_params=_cparams("parallel", "parallel"),
        name="hyena_idft",
    )(inv, y2)

    return pl.pallas_call(
        _hyena_post_kernel,
        grid=(B,),
        in_specs=[pl.BlockSpec((1, L, 3 * C), lambda b: (b, 0, 0)),
                  pl.BlockSpec((L, C), lambda b: (0, b)),
                  pl.BlockSpec((HY_SHORT, 3 * C), lambda b: (0, 0)),
                  pl.BlockSpec((1, 3 * C), lambda b: (0, 0)),
                  pl.BlockSpec((1, C), lambda b: (0, 0))],
        out_specs=pl.BlockSpec((1, L, C), lambda b: (b, 0, 0)),
        out_shape=jax.ShapeDtypeStruct((B, L, C), F32),
        scratch_shapes=[pltpu.VMEM((3 * C // LANES, L + 2 * CONV_MARGIN, LANES), F32)],
        compiler_params=_cparams("parallel"),
        name="hyena_post",
    )(u, yt, w, bsh, lp["hy_bias"].reshape(1, C))


FFT_N2 = 128
FFT_UNROLL = 8


class _FftPlan:
    def __init__(self, L):
        self.L = L
        self.N = 2 * L
        self.N1 = self.N // FFT_N2
        self.KH = self.N1 // 2 + 1
        self.KP = -(-self.KH // 8) * 8
        self.PA = 2 * self.KP + 4


def fft_tables(L):
    p = _FftPlan(L)
    N, N1, KH, KP = p.N, p.N1, p.KH, p.KP
    n2 = jnp.arange(FFT_N2, dtype=jnp.int32)
    k1 = jnp.arange(KP, dtype=jnp.int32)
    n1 = jnp.arange(N1, dtype=jnp.int32)
    n = FFT_N2 * n1[None, None, :] + n2[:, None, None]
    ang = (2.0 * math.pi / N) * ((k1[None, :, None] * n) % N).astype(F32)
    keep = (k1 < KH)[None, :, None]
    g_re = jnp.where(keep, jnp.cos(ang), 0.0)
    g_im = jnp.where(keep, -jnp.sin(ang), 0.0)
    ga_full = jnp.concatenate([g_re, g_im], axis=1)
    ck = jnp.where((k1 == 0) | (k1 == N1 // 2), 1.0, 2.0) / N
    ga_inv = jnp.swapaxes(ga_full[:, :, :N1 // 2] * jnp.tile(ck, 2)[None, :, None], 1, 2)
    kk = jnp.arange(FFT_N2, dtype=jnp.int32)
    ang2 = (2.0 * math.pi / FFT_N2) * ((kk[:, None] * kk[None, :]) % FFT_N2).astype(F32)
    fr, fi = jnp.cos(ang2), -jnp.sin(ang2)
    fb = jnp.block([[fr, -fi], [fi, fr]])
    fb_inv = jnp.block([[fr, fi], [-fi, fr]])
    return dict(ga_half=ga_full[:, :, :N1 // 2].astype(BF16), ga_full=ga_full.astype(BF16),
                ga_inv=ga_inv.astype(BF16), fb=fb.astype(BF16), fb_inv=fb_inv.astype(BF16))


def _fft_stage_a(x_ref, ga_ref, s_ref, plan, n1_count):
    n_slabs = x_ref.shape[0]

    def body(n2, carry):
        xs = jnp.concatenate([x_ref[s, pl.ds(n2, n1_count, stride=FFT_N2), :] for s in range(n_slabs)], axis=-1)
        a = jnp.dot(ga_ref[n2], xs.astype(BF16), preferred_element_type=F32)
        for s in range(n_slabs):
            s_ref[s, pl.ds(n2 * plan.PA, 2 * plan.KP), :] = a[:, s * LANES:(s + 1) * LANES]
        return carry
    lax.fori_loop(0, FFT_N2, body, 0, unroll=FFT_UNROLL)


def _fft_load_k1(s_ref, k1, plan):
    n_slabs = s_ref.shape[0]
    re = jnp.concatenate([s_ref[s, pl.ds(k1, FFT_N2, stride=plan.PA), :] for s in range(n_slabs)], axis=-1)
    im = jnp.concatenate([s_ref[s, pl.ds(plan.KP + k1, FFT_N2, stride=plan.PA), :] for s in range(n_slabs)], axis=-1)
    return jnp.concatenate([re, im], axis=0).astype(BF16)


def _fft_filter_kernel(k_ref, ga_ref, fb_ref, h_ref, s_ref, *, plan):
    _fft_stage_a(k_ref, ga_ref, s_ref, plan, plan.N1)

    def body(k1, carry):
        h_ref[k1] = jnp.dot(fb_ref[...], _fft_load_k1(s_ref, k1, plan), preferred_element_type=F32).astype(BF16)
        return carry
    lax.fori_loop(0, plan.KH, body, 0)


def _fft_conv_kernel(z_ref, ga_ref, gi_ref, fb_ref, fbi_ref, h_ref, y_ref, s_ref, *, plan):
    zs = z_ref.at[0]
    ys = y_ref.at[0]
    n_slabs = zs.shape[0]
    half = FFT_N2
    _fft_stage_a(zs, ga_ref, s_ref, plan, plan.N1 // 2)

    def body_b(k1, carry):
        x = jnp.dot(fb_ref[...], _fft_load_k1(s_ref, k1, plan), preferred_element_type=F32)
        h = h_ref[k1].astype(F32)
        xr, xi, hr, hi = x[:half], x[half:], h[:half], h[half:]
        y = jnp.concatenate([xr * hr - xi * hi, xr * hi + xi * hr], axis=0).astype(BF16)
        b = jnp.dot(fbi_ref[...], y, preferred_element_type=F32)
        for s in range(n_slabs):
            s_ref[s, pl.ds(k1, FFT_N2, stride=plan.PA), :] = b[:half, s * LANES:(s + 1) * LANES]
            s_ref[s, pl.ds(plan.KP + k1, FFT_N2, stride=plan.PA), :] = b[half:, s * LANES:(s + 1) * LANES]
        return carry
    lax.fori_loop(0, plan.KH, body_b, 0, unroll=3)

    def body_a(n2, carry):
        b = jnp.concatenate([s_ref[s, pl.ds(n2 * plan.PA, 2 * plan.KP), :] for s in range(n_slabs)], axis=-1)
        y = jnp.dot(gi_ref[n2], b.astype(BF16), preferred_element_type=F32)
        for s in range(n_slabs):
            ys[s, pl.ds(n2, plan.N1 // 2, stride=FFT_N2), :] = y[:, s * LANES:(s + 1) * LANES]
        return carry
    lax.fori_loop(0, FFT_N2, body_a, 0, unroll=FFT_UNROLL)


def fft_filter_spectrum(h_fwd, h_bwd, tabs):
    L, C = h_fwd.shape
    plan = _FftPlan(L)
    n_slabs = C // LANES
    k = jnp.concatenate([h_fwd, jnp.zeros((1, C), F32), h_bwd[1:][::-1]], axis=0)
    k = k.reshape(plan.N, n_slabs, LANES).transpose(1, 0, 2)
    full = lambda shape: pl.BlockSpec(shape, lambda i: (0,) * len(shape))
    return pl.pallas_call(
        functools.partial(_fft_filter_kernel, plan=plan),
        grid=(1,),
        in_specs=[full((n_slabs, plan.N, LANES)), full((FFT_N2, 2 * plan.KP, plan.N1)),
                  full((2 * FFT_N2, 2 * FFT_N2))],
        out_specs=full((plan.KH, 2 * FFT_N2, C)),
        out_shape=jax.ShapeDtypeStruct((plan.KH, 2 * FFT_N2, C), BF16),
        scratch_shapes=[pltpu.VMEM((n_slabs, FFT_N2 * plan.PA, LANES), F32)],
        compiler_params=_cparams("arbitrary"),
        name="hyena_filter_fft",
    )(k, tabs["ga_full"], tabs["fb"])


def fft_long_conv(z, h_spec, tabs):
    B, n_slabs, L, _ = z.shape
    plan = _FftPlan(L)
    C = n_slabs * LANES
    full = lambda shape: pl.BlockSpec(shape, lambda b: (0,) * len(shape))
    seq = pl.BlockSpec((1, n_slabs, L, LANES), lambda b: (b, 0, 0, 0))
    return pl.pallas_call(
        functools.partial(_fft_conv_kernel, plan=plan),
        grid=(B,),
        in_specs=[seq, full((FFT_N2, 2 * plan.KP, plan.N1 // 2)), full((FFT_N2, plan.N1 // 2, 2 * plan.KP)),
                  full((2 * FFT_N2, 2 * FFT_N2)), full((2 * FFT_N2, 2 * FFT_N2)),
                  full((plan.KH, 2 * FFT_N2, C))],
        out_specs=seq,
        out_shape=jax.ShapeDtypeStruct((B, n_slabs, L, LANES), F32),
        scratch_shapes=[pltpu.VMEM((n_slabs, FFT_N2 * plan.PA, LANES), F32)],
        compiler_params=_cparams("parallel"),
        name="hyena_fft_conv",
    )(z, tabs["ga_half"], tabs["ga_inv"], tabs["fb"], tabs["fb_inv"], h_spec)


def _hyena_pre_slab_kernel(u_ref, w_ref, b_ref, z_ref, upad):
    L = u_ref.shape[1]
    T = _time_chunk(L)
    C = D_GROUP
    _fill_padded(upad, u_ref, L, T)

    def body(j, carry):
        base = pl.multiple_of(j * T, T)
        for s in range(C // LANES):
            x1 = _dw_conv_slab(upad, C // LANES + s, base, T, w_ref, b_ref, C + s * LANES, HY_SHORT, 1)
            v = _dw_conv_slab(upad, 2 * C // LANES + s, base, T, w_ref, b_ref, 2 * C + s * LANES, HY_SHORT, 1)
            z_ref[0, s, pl.ds(base, T), :] = x1 * v
        return carry
    lax.fori_loop(0, L // T, body, 0)


def _hyena_post_slab_kernel(u0_ref, z_ref, y_ref, w_ref, b_ref, bias_ref, o_ref, upad):
    L = u0_ref.shape[1]
    T = _time_chunk(L)
    C = D_GROUP
    _fill_padded(upad, u0_ref, L, T)

    def body(j, carry):
        base = pl.multiple_of(j * T, T)
        x0 = _short_conv(upad, base, T, w_ref, b_ref, 0, C)
        z = jnp.concatenate([z_ref[0, s, pl.ds(base, T), :] for s in range(C // LANES)], axis=-1)
        y = jnp.concatenate([y_ref[0, s, pl.ds(base, T), :] for s in range(C // LANES)], axis=-1)
        o_ref[0, pl.ds(base, T), :] = x0 * (y + z * bias_ref[...])
        return carry
    lax.fori_loop(0, L // T, body, 0)


def hyena_mixer_fft(u, lp, tabs):
    B, L, _ = u.shape
    C = D_GROUP
    n_slabs = C // LANES
    w, bsh = lp["hy_short_w"], lp["hy_short_b"].reshape(1, 3 * C)
    useq = pl.BlockSpec((1, L, 3 * C), lambda b: (b, 0, 0))
    slabs = pl.BlockSpec((1, n_slabs, L, LANES), lambda b: (b, 0, 0, 0))
    wspec = pl.BlockSpec((HY_SHORT, 3 * C), lambda b: (0, 0))
    bspec = pl.BlockSpec((1, 3 * C), lambda b: (0, 0))
    pad_scratch = pltpu.VMEM((3 * C // LANES, L + 2 * CONV_MARGIN, LANES), F32)
    z = pl.pallas_call(
        _hyena_pre_slab_kernel,
        grid=(B,),
        in_specs=[useq, wspec, bspec],
        out_specs=slabs,
        out_shape=jax.ShapeDtypeStruct((B, n_slabs, L, LANES), F32),
        scratch_shapes=[pad_scratch],
        compiler_params=_cparams("parallel"),
        name="hyena_pre",
    )(u, w, bsh)
    h_fwd, h_bwd = _hyena_filters(L, lp)
    y = fft_long_conv(z, fft_filter_spectrum(h_fwd, h_bwd, tabs), tabs)
    return pl.pallas_call(
        _hyena_post_slab_kernel,
        grid=(B,),
        in_specs=[pl.BlockSpec((1, L, C), lambda b: (b, 0, 0)), slabs, slabs, wspec, bspec,
                  pl.BlockSpec((1, C), lambda b: (0, 0))],
        out_specs=pl.BlockSpec((1, L, C), lambda b: (b, 0, 0)),
        out_shape=jax.ShapeDtypeStruct((B, L, C), F32),
        scratch_shapes=[pltpu.VMEM((n_slabs, L + 2 * CONV_MARGIN, LANES), F32)],
        compiler_params=_cparams("parallel"),
        name="hyena_post",
    )(u, z, y, w, bsh, lp["hy_bias"].reshape(1, C))


def _hyena_filters(L, lp):
    t = jnp.linspace(0.0, 1.0, L, dtype=F32)[:, None]
    bands = (HY_EMB - 1) // 2
    w = 2.0 * math.pi * jnp.arange(L, dtype=F32)[:, None] / L
    f = jnp.linspace(1e-4, bands - 1, bands, dtype=F32)[None]
    z = jnp.concatenate([t, jnp.cos(f * w), -jnp.sin(f * w)], axis=-1)
    hdn = jnp.sin(z @ lp["hy_ffn_w1"] + lp["hy_ffn_b1"])
    hdn = jnp.sin(hdn @ lp["hy_ffn_w2"] + lp["hy_ffn_b2"])
    h = (hdn @ lp["hy_ffn_w3"]).reshape(L, 2, D_GROUP)
    max_decay = math.log(HY_TARGET) / HY_FAST_DECAY
    min_decay = math.log(HY_TARGET) / HY_SLOW_DECAY
    deltas = jnp.linspace(min_decay, max_decay, D_GROUP, dtype=F32)
    h = h * jnp.exp(-t * jnp.abs(deltas))[:, None, :]
    h = h / (jnp.sum(jnp.abs(h), axis=(0, 1), keepdims=True) + EPS)
    return h[:, 0], h[:, 1]


def _layer(hc, hx, c_silu_all, lp, need_ctx, final_g, final_norm, tables_x, tables_c):
    B, S, D = hx.shape
    C = hc.shape[1]
    mod = small_linear(c_silu_all, lp["ada_w"], lp["ada_b"])
    mod_x = mod[:B].reshape(B, 6, 1, D)
    mod_c = jnp.broadcast_to(mod[B].reshape(1, 6, 1, D), (B, 6, 1, D))
    w_ext = extend_w_in(lp["w_in"])
    cos_x, sin_x = rope_tables(S, True)
    cos_c, sin_c = rope_tables(C, False)
    hy_x, cf_x, at_x, lr_x = in_proj(hx, mod_x[:, 0], mod_x[:, 1], lp["norm1_g"], w_ext, cos_x, sin_x, tm=512)
    hy_c, cf_c, at_c, lr_c = in_proj(hc, mod_c[:, 0], mod_c[:, 1], lp["norm1_g"], w_ext, cos_c, sin_c, tm=256)

    yd_c, yd_x = rglru_mixer(lr_c, lr_x, lp, need_ctx)
    conf = lambda u: conformer_conv(u, lp["conf_dw_w"], lp["conf_dw_b"], lp["conf_ln_g"], lp["conf_ln_b"])
    ys_x = [hyena_mixer_fft(hy_x, lp, tables_x), conf(cf_x),
            window_attention(at_x, at_c, lp["attn_sink"]), yd_x]

    w_out = lp["w_out"].astype(BF16)
    w_router = jnp.zeros((D, ROUTER_COLS), F32)
    w_router = w_router.at[:, :N_GROUPS].set(lp["router_g_w"]).at[:, N_GROUPS:N_GROUPS + N_EXPERTS].set(lp["router_e_w"])
    w_router = w_router.astype(BF16)
    b_router = jnp.zeros((1, ROUTER_COLS), F32)
    b_router = b_router.at[0, :N_GROUPS].set(lp["router_g_b"]).at[0, N_GROUPS:N_GROUPS + N_EXPERTS].set(lp["router_e_b"])

    hx1, lg_x = out_proj(ys_x, hx, mod_x[:, 2], lp["group_norm_g"], w_out, lp["norm2_g"],
                         mod_x[:, 3], mod_x[:, 4], w_router, b_router, tm=512)
    h_tok = hx1.reshape(B * S, D)
    hc_tok = None
    lg = lg_x.reshape(B * S, ROUTER_COLS)
    if need_ctx:
        ys_c = [hyena_mixer(hy_c, lp, tables_c), conf(cf_c),
                context_attention(at_c, lp["attn_sink"]), yd_c]
        hc1, lg_c = out_proj(ys_c, hc, mod_c[:, 2], lp["group_norm_g"], w_out, lp["norm2_g"],
                             mod_c[:, 3], mod_c[:, 4], w_router, b_router, tm=256)
        hc_tok = hc1.reshape(B * C, D)
        lg = jnp.concatenate([lg, lg_c.reshape(B * C, ROUTER_COLS)], axis=0)

    T = lg.shape[0]
    n_blocks = -(-T // MOE_BLOCK) + N_CLASSES
    info, counts, ids = route_tokens(lg)
    dest, blk_a, blk_b, n_used = slot_plan(ids, counts, n_blocks)
    xs = moe_dispatch(h_tok, hc_tok, dest, n_blocks, lp["norm2_g"], (mod_x[:, 3], mod_x[:, 4]),
                      (mod_c[:, 3], mod_c[:, 4]), S)
    o_sorted = expert_pairs(xs, blk_a, blk_b, n_used, lp["exp_w_gate"].astype(BF16),
                            lp["exp_w_up"].astype(BF16), lp["exp_w_down"].astype(BF16))
    hx2 = moe_collect(o_sorted, dest, info, 0, h_tok, mod_x[:, 5], S, final_g, final_norm)
    hx2 = hx2.reshape(B, S, D)
    if need_ctx:
        hc2 = moe_collect(o_sorted, dest, info, B * S // DISPATCH_TOKENS, hc_tok, mod_c[:, 5], C,
                          final_g, False).reshape(B, C, D)
    else:
        hc2 = hc
    return hc2, hx2


def kernel(x, c, ctx, c_ctx, norm1_g, norm2_g, ada_w, ada_b, w_in, hy_short_w, hy_short_b, hy_ffn_w1, hy_ffn_b1, hy_ffn_w2, hy_ffn_b2, hy_ffn_w3, hy_bias, conf_dw_w, conf_dw_b, conf_ln_g, conf_ln_b, attn_sink, lru_conv_w, lru_conv_b, lru_wa, lru_ba, lru_wx, lru_bx, lru_lambda, group_norm_g, w_out, router_g_w, router_g_b, router_e_w, router_e_b, exp_w_gate, exp_w_up, exp_w_down, final_norm_g):
    stacked = dict(norm1_g=norm1_g, norm2_g=norm2_g, ada_w=ada_w, ada_b=ada_b, w_in=w_in,
                   hy_short_w=hy_short_w, hy_short_b=hy_short_b, hy_ffn_w1=hy_ffn_w1, hy_ffn_b1=hy_ffn_b1,
                   hy_ffn_w2=hy_ffn_w2, hy_ffn_b2=hy_ffn_b2, hy_ffn_w3=hy_ffn_w3, hy_bias=hy_bias,
                   conf_dw_w=conf_dw_w, conf_dw_b=conf_dw_b, conf_ln_g=conf_ln_g, conf_ln_b=conf_ln_b,
                   attn_sink=attn_sink, lru_conv_w=lru_conv_w, lru_conv_b=lru_conv_b, lru_wa=lru_wa,
                   lru_ba=lru_ba, lru_wx=lru_wx, lru_bx=lru_bx, lru_lambda=lru_lambda,
                   group_norm_g=group_norm_g, w_out=w_out, router_g_w=router_g_w, router_g_b=router_g_b,
                   router_e_w=router_e_w, router_e_b=router_e_b, exp_w_gate=exp_w_gate,
                   exp_w_up=exp_w_up, exp_w_down=exp_w_down)
    depth = norm1_g.shape[0]
    B = x.shape[0]
    cs = jnp.concatenate([jax.nn.silu(c), jnp.broadcast_to(jax.nn.silu(c_ctx)[None], (8, c.shape[1]))], axis=0)
    hc, hx = ctx, x
    tables_x = fft_tables(x.shape[1])
    tables_c = dft_tables(ctx.shape[1])
    for l in range(depth):
        lp = {k: v[l] for k, v in stacked.items()}
        hc, hx = _layer(hc, hx, cs, lp, need_ctx=(l < depth - 1), final_g=final_norm_g,
                        final_norm=(l == depth - 1), tables_x=tables_x, tables_c=tables_c)
    return hx
```

```python
import functools
import math

import jax
import jax.numpy as jnp
from jax import lax
from jax.experimental import pallas as pl
from jax.experimental.pallas import tpu as pltpu

F32 = jnp.float32
BF16 = jnp.bfloat16

EPS = 1e-6
NEG_INF = -1e30
GRID_W = 64
N_MIXERS = 4
D_GROUP = 256
HY_COLS = 3 * D_GROUP
CONF_COLS = 2 * D_GROUP
ATT_HEADS = 4
ATT_KV_HEADS = 2
HEAD_DIM = 64
ATT_COLS = (ATT_HEADS + 2 * ATT_KV_HEADS) * HEAD_DIM
LRU_COLS = 2 * D_GROUP
QK_COLS = (ATT_HEADS + ATT_KV_HEADS) * HEAD_DIM
WINDOW = 128
ATT_BLOCK = 128
ROPE_BASE = 10000.0
HY_EMB = 33
HY_FAST_DECAY = 0.3
HY_SLOW_DECAY = 1.5
HY_TARGET = 1e-2
CONF_KERNEL = 31
LRU_HEADS = 4
LRU_CONV = 4
LRU_C = 8.0
N_GROUPS = 4
EXP_PER_GROUP = 8
N_EXPERTS = N_GROUPS * EXP_PER_GROUP
TOP_K = 2
MOE_BLOCK = 256
ROUTER_COLS = 128

VMEM_LIMIT_BYTES = 56 * 1024 * 1024


def _cparams(*sem):
    return pltpu.CompilerParams(dimension_semantics=sem, vmem_limit_bytes=VMEM_LIMIT_BYTES)


def _linear_kernel(x_ref, w_ref, b_ref, o_ref):
    o_ref[...] = jnp.dot(x_ref[...], w_ref[...], preferred_element_type=F32,
                         precision=lax.Precision.HIGHEST) + b_ref[...]


def small_linear(x, w, b, tn=1024):
    M, K = x.shape
    N = w.shape[1]
    return pl.pallas_call(
        _linear_kernel,
        grid=(N // tn,),
        in_specs=[pl.BlockSpec((M, K), lambda j: (0, 0)),
                  pl.BlockSpec((K, tn), lambda j: (0, j)),
                  pl.BlockSpec((1, tn), lambda j: (0, j))],
        out_specs=pl.BlockSpec((M, tn), lambda j: (0, j)),
        out_shape=jax.ShapeDtypeStruct((M, N), F32),
        compiler_params=_cparams("parallel"),
        name="ada_linear",
    )(x, w, b.reshape(1, N))


def _in_proj_kernel(x_ref, sh_ref, sc_ref, g_ref, w_ref, cos_ref, sin_ref,
                    hy_ref, cf_ref, at_ref, lr_ref):
    x = x_ref[0]
    ms = jnp.mean(x * x, axis=-1, keepdims=True)
    y = x * lax.rsqrt(ms + EPS) * g_ref[...]
    y = y * (1.0 + sc_ref[0]) + sh_ref[0]
    u = jnp.dot(y.astype(BF16), w_ref[...], preferred_element_type=F32)
    c0 = HY_COLS
    c1 = c0 + CONF_COLS
    c2 = c1 + ATT_COLS
    c3 = c2 + LRU_COLS
    hy_ref[0] = u[:, :c0]
    cf_ref[0] = u[:, c0:c1]
    lr_ref[0] = u[:, c2:c3]
    qk = u[:, c1:c1 + QK_COLS]
    qk_rot = u[:, c3:c3 + QK_COLS]
    at_ref[0, :, :QK_COLS] = qk * cos_ref[...] + qk_rot * sin_ref[...]
    at_ref[0, :, QK_COLS:] = u[:, c1 + QK_COLS:c2]


def in_proj(h, shift, scale, g, w_ext, cos_t, sin_t, tm):
    B, L, D = h.shape
    NW = w_ext.shape[1]
    outs = [HY_COLS, CONF_COLS, ATT_COLS, LRU_COLS]
    return pl.pallas_call(
        _in_proj_kernel,
        grid=(B, L // tm),
        in_specs=[pl.BlockSpec((1, tm, D), lambda b, i: (b, i, 0)),
                  pl.BlockSpec((1, 1, D), lambda b, i: (b, 0, 0)),
                  pl.BlockSpec((1, 1, D), lambda b, i: (b, 0, 0)),
                  pl.BlockSpec((1, D), lambda b, i: (0, 0)),
                  pl.BlockSpec((D, NW), lambda b, i: (0, 0)),
                  pl.BlockSpec((tm, QK_COLS), lambda b, i: (i, 0)),
                  pl.BlockSpec((tm, QK_COLS), lambda b, i: (i, 0))],
        out_specs=[pl.BlockSpec((1, tm, n), lambda b, i: (b, i, 0)) for n in outs],
        out_shape=[jax.ShapeDtypeStruct((B, L, n), F32) for n in outs],
        compiler_params=_cparams("parallel", "parallel"),
        name="in_proj",
    )(h, shift, scale, g.reshape(1, D), w_ext, cos_t, sin_t)


def rope_tables(L, rotary):
    n_heads = ATT_HEADS + ATT_KV_HEADS
    if not rotary:
        return jnp.ones((L, QK_COLS), F32), jnp.zeros((L, QK_COLS), F32)
    pos = jnp.arange(L)
    row = (pos // GRID_W).astype(F32)
    col = (pos % GRID_W).astype(F32)
    half = HEAD_DIM // 2
    inv_freq = ROPE_BASE ** (-jnp.arange(0, half, 2, dtype=F32) / half)
    ang_r = row[:, None] * inv_freq[None]
    ang_c = col[:, None] * inv_freq[None]
    cos_h = jnp.concatenate([jnp.cos(ang_r)] * 2 + [jnp.cos(ang_c)] * 2, axis=-1)
    sin_h = jnp.concatenate([jnp.sin(ang_r)] * 2 + [jnp.sin(ang_c)] * 2, axis=-1)
    return jnp.tile(cos_h, (1, n_heads)), jnp.tile(sin_h, (1, n_heads))


def extend_w_in(w_in):
    c1 = HY_COLS + CONF_COLS
    wqk = w_in[:, c1:c1 + QK_COLS]
    D = w_in.shape[0]
    w4 = wqk.reshape(D, QK_COLS // 32, 2, 16)
    wrot = jnp.stack([-w4[:, :, 1], w4[:, :, 0]], axis=2).reshape(D, QK_COLS)
    return jnp.concatenate([w_in, wrot], axis=1).astype(BF16)


def _softmax_parts(q, k_list, extra_logit):
    scale = HEAD_DIM ** -0.5
    s_list = []
    for k, mask in k_list:
        s = lax.dot_general(q, k, (((1,), (1,)), ((), ())), preferred_element_type=F32) * scale
        if mask is not None:
            s = jnp.where(mask, s, NEG_INF)
        s_list.append(s)
    m = extra_logit
    for s in s_list:
        m = jnp.maximum(m, jnp.max(s, axis=-1, keepdims=True))
    p_list = [jnp.exp(s - m) for s in s_list]
    denom = jnp.exp(extra_logit - m)
    for p in p_list:
        denom = denom + jnp.sum(p, axis=-1, keepdims=True)
    return p_list, 1.0 / denom


ATT_Q_BLOCKS = 4


def _win_attn_kernel(sink_ref, q_ref, kp_ref, kc_ref, kn_ref, vp_ref, vc_ref, vn_ref,
                     kx_ref, vx_ref, o_ref, *, seq_len):
    i = pl.program_id(1)
    blk = ATT_BLOCK
    qb = q_ref.shape[1] // blk
    scale = HEAD_DIM ** -0.5
    g = ATT_HEADS // ATT_KV_HEADS
    kw = jnp.concatenate([kp_ref[0], kc_ref[0], kn_ref[0]], axis=0)
    vw = jnp.concatenate([vp_ref[0], vc_ref[0], vn_ref[0]], axis=0).astype(BF16)
    kwt = kw.T.astype(BF16)
    kxt = kx_ref[0].T.astype(BF16)
    vx = vx_ref[0].astype(BF16)
    row = lax.broadcasted_iota(jnp.int32, (g * blk, 3 * blk), 0) % blk
    col = lax.broadcasted_iota(jnp.int32, (g * blk, 3 * blk), 1)
    band_bias = jnp.where(jnp.abs(col - blk - row) <= WINDOW, 0.0, NEG_INF)
    col1 = lax.broadcasted_iota(jnp.int32, (1, 3 * blk), 1)
    for j in range(qb):
        q_blk = i * qb + j
        k_pos = (q_blk - 1) * blk + col1
        edge_bias = jnp.where(k_pos >= 0, jnp.where(k_pos < seq_len, 0.0, NEG_INF), NEG_INF)
        bias = band_bias + edge_bias
        outs = []
        for kv in range(ATT_KV_HEADS):
            ksl = slice(kv * HEAD_DIM, (kv + 1) * HEAD_DIM)
            heads = range(kv * g, (kv + 1) * g)
            qs = (jnp.concatenate([q_ref[0, j * blk:(j + 1) * blk, h * HEAD_DIM:(h + 1) * HEAD_DIM]
                                   for h in heads], axis=0) * scale).astype(BF16)
            sink = jnp.concatenate([jnp.full((blk, 1), sink_ref[h], F32) for h in heads], axis=0)
            s_win = jnp.dot(qs, kwt[ksl, j * blk:(j + 3) * blk], preferred_element_type=F32) + bias
            s_ctx = jnp.dot(qs, kxt[ksl, :], preferred_element_type=F32)
            m = jnp.maximum(jnp.maximum(jnp.max(s_win, axis=-1, keepdims=True),
                                        jnp.max(s_ctx, axis=-1, keepdims=True)), sink)
            p_win = jnp.exp(s_win - m)
            p_ctx = jnp.exp(s_ctx - m)
            denom = (jnp.exp(sink - m) + jnp.sum(p_win, axis=-1, keepdims=True)
                     + jnp.sum(p_ctx, axis=-1, keepdims=True))
            o = (jnp.dot(p_win.astype(BF16), vw[j * blk:(j + 3) * blk, ksl], preferred_element_type=F32)
                 + jnp.dot(p_ctx.astype(BF16), vx[:, ksl], preferred_element_type=F32)) * (1.0 / denom)
            outs.extend([o[k * blk:(k + 1) * blk] for k in range(g)])
        o_ref[0, j * blk:(j + 1) * blk, :] = jnp.concatenate(outs, axis=-1)


def window_attention(at_x, at_c, sink):
    B, S, _ = at_x.shape
    C = at_c.shape[1]
    blk = ATT_BLOCK
    qb = ATT_Q_BLOCKS
    nb = S // blk
    kcol = QK_COLS // 128 - 1
    vcol = kcol + 1

    def edge_spec(col, off):
        return pl.BlockSpec((1, blk, 128), lambda b, i, s: (b, jnp.clip(i * qb + off, 0, nb - 1), col))

    def mid_spec(col):
        return pl.BlockSpec((1, qb * blk, 128), lambda b, i, s: (b, i, col))

    grid_spec = pltpu.PrefetchScalarGridSpec(
        num_scalar_prefetch=1,
        grid=(B, nb // qb),
        in_specs=[pl.BlockSpec((1, qb * blk, ATT_HEADS * HEAD_DIM), lambda b, i, s: (b, i, 0)),
                  edge_spec(kcol, -1), mid_spec(kcol), edge_spec(kcol, qb),
                  edge_spec(vcol, -1), mid_spec(vcol), edge_spec(vcol, qb),
                  pl.BlockSpec((1, C, 128), lambda b, i, s: (b, 0, kcol)),
                  pl.BlockSpec((1, C, 128), lambda b, i, s: (b, 0, vcol))],
        out_specs=pl.BlockSpec((1, qb * blk, ATT_HEADS * HEAD_DIM), lambda b, i, s: (b, i, 0)),
    )
    return pl.pallas_call(
        functools.partial(_win_attn_kernel, seq_len=S),
        grid_spec=grid_spec,
        out_shape=jax.ShapeDtypeStruct((B, S, ATT_HEADS * HEAD_DIM), F32),
        compiler_params=_cparams("parallel", "parallel"),
        name="window_attention",
    )(sink.astype(F32), at_x, at_x, at_x, at_x, at_x, at_x, at_x, at_c, at_c)


def _ctx_attn_kernel(sink_ref, q_ref, kx_ref, vx_ref, o_ref):
    q = q_ref[0].astype(BF16)
    kx = kx_ref[0].astype(BF16)
    vx = vx_ref[0].astype(BF16)
    g = ATT_HEADS // ATT_KV_HEADS
    outs = []
    for h in range(ATT_HEADS):
        kv = h // g
        qs = q[:, h * HEAD_DIM:(h + 1) * HEAD_DIM]
        ksl = slice(kv * HEAD_DIM, (kv + 1) * HEAD_DIM)
        (p_ctx,), inv = _softmax_parts(qs, [(kx[:, ksl], None)], sink_ref[h])
        outs.append(jnp.dot(p_ctx.astype(BF16), vx[:, ksl], preferred_element_type=F32) * inv)
    o_ref[0] = jnp.concatenate(outs, axis=-1)


def context_attention(at_c, sink):
    B, C, _ = at_c.shape
    kcol = QK_COLS // 128 - 1
    grid_spec = pltpu.PrefetchScalarGridSpec(
        num_scalar_prefetch=1,
        grid=(B,),
        in_specs=[pl.BlockSpec((1, C, ATT_HEADS * HEAD_DIM), lambda b, s: (b, 0, 0)),
                  pl.BlockSpec((1, C, 128), lambda b, s: (b, 0, kcol)),
                  pl.BlockSpec((1, C, 128), lambda b, s: (b, 0, kcol + 1))],
        out_specs=pl.BlockSpec((1, C, ATT_HEADS * HEAD_DIM), lambda b, s: (b, 0, 0)),
    )
    return pl.pallas_call(
        _ctx_attn_kernel,
        grid_spec=grid_spec,
        out_shape=jax.ShapeDtypeStruct((B, C, ATT_HEADS * HEAD_DIM), F32),
        compiler_params=_cparams("parallel"),
        name="context_attention",
    )(sink.astype(F32), at_c, at_c, at_c)


def _out_proj_kernel(y0_ref, y1_ref, y2_ref, y3_ref, h_ref, g1_ref, gng_ref, w_ref,
                     n2g_ref, sh_ref, sc_ref, wr_ref, br_ref, ho_ref, lg_ref):
    parts = []
    for k, y_ref in enumerate((y0_ref, y1_ref, y2_ref, y3_ref)):
        y = y_ref[0]
        ms = jnp.mean(y * y, axis=-1, keepdims=True)
        yn = y * lax.rsqrt(ms + EPS) * gng_ref[:, k * D_GROUP:(k + 1) * D_GROUP]
        parts.append(yn.astype(BF16))
    yn = jnp.concatenate(parts, axis=-1)
    proj = jnp.dot(yn, w_ref[...], preferred_element_type=F32)
    h = h_ref[0] + g1_ref[0] * proj
    ho_ref[0] = h
    ms = jnp.mean(h * h, axis=-1, keepdims=True)
    n = h * lax.rsqrt(ms + EPS) * n2g_ref[...]
    n = n * (1.0 + sc_ref[0]) + sh_ref[0]
    lg_ref[0] = jnp.dot(n.astype(BF16), wr_ref[...], preferred_element_type=F32) + br_ref[...]


def out_proj(ys, h, g1, gng, w_out, n2g, sh2, sc2, w_router, b_router, tm):
    B, L, D = h.shape
    row3 = lambda n: pl.BlockSpec((1, tm, n), lambda b, i: (b, i, 0))
    mod = pl.BlockSpec((1, 1, D), lambda b, i: (b, 0, 0))
    full = lambda r, c: pl.BlockSpec((r, c), lambda b, i: (0, 0))
    return pl.pallas_call(
        _out_proj_kernel,
        grid=(B, L // tm),
        in_specs=[row3(D_GROUP)] * 4 + [row3(D), mod, full(1, D), full(D, D), full(1, D), mod, mod,
                                        full(D, ROUTER_COLS), full(1, ROUTER_COLS)],
        out_specs=[row3(D), row3(ROUTER_COLS)],
        out_shape=[jax.ShapeDtypeStruct((B, L, D), F32), jax.ShapeDtypeStruct((B, L, ROUTER_COLS), F32)],
        compiler_params=_cparams("parallel", "parallel"),
        name="out_proj",
    )(*ys, h, g1, gng.reshape(1, D), w_out, n2g.reshape(1, D), sh2, sc2, w_router, b_router)


N_PAIRS = EXP_PER_GROUP * (EXP_PER_GROUP - 1) // 2
N_CLASSES = N_GROUPS * N_PAIRS
ROUTE_TOKENS = 512
INFO_CLASS, INFO_RANK, INFO_WA, INFO_WB = 0, 1, 2, 3


SUBLANES = 8


def _route_kernel(lg_ref, info_ref, cnt_ref, ids_ref, run):
    i = pl.program_id(0)
    tb = lg_ref.shape[0]

    @pl.when(i == 0)
    def _():
        run[...] = jnp.zeros_like(run)

    lg = lg_ref[...]
    li = lax.broadcasted_iota(jnp.int32, lg.shape, 1)
    big = jnp.int32(ROUTER_COLS)

    def first_argmax(vals):
        m = jnp.max(vals, axis=-1, keepdims=True)
        return m, jnp.min(jnp.where(vals == m, li, big), axis=-1, keepdims=True)

    gl = jnp.where(li < N_GROUPS, lg, NEG_INF)
    gmax, g_idx = first_argmax(gl)
    g_prob = 1.0 / jnp.sum(jnp.exp(gl - gmax), axis=-1, keepdims=True)
    lo = N_GROUPS + EXP_PER_GROUP * g_idx
    el = jnp.where((li >= lo) & (li < lo + EXP_PER_GROUP), lg, NEG_INF)
    m1, i1 = first_argmax(el)
    m2, i2 = first_argmax(jnp.where(li == i1, NEG_INF, el))
    e2 = jnp.exp(m2 - m1)
    w1 = g_prob / (1.0 + e2)
    w2 = g_prob * e2 / (1.0 + e2)
    j1 = i1 - lo
    j2 = i2 - lo
    a = jnp.minimum(j1, j2)
    b = jnp.maximum(j1, j2)
    cls = g_idx * N_PAIRS + ((a * (2 * EXP_PER_GROUP - 1 - a)) >> 1) + (b - a - 1)
    w_a = jnp.where(j1 < j2, w1, w2)
    w_b = jnp.where(j1 < j2, w2, w1)

    hit = li == cls
    onehot = jnp.where(hit, 1.0, 0.0)
    r_i = lax.broadcasted_iota(jnp.int32, (tb, tb), 0)
    c_i = lax.broadcasted_iota(jnp.int32, (tb, tb), 1)
    below = jnp.where(c_i < r_i, 1.0, 0.0).astype(BF16)
    before = jnp.dot(below, onehot.astype(BF16), preferred_element_type=F32)
    rank = jnp.sum(jnp.where(hit, before + run[...], 0.0), axis=-1, keepdims=True)
    run[...] = run[...] + jnp.sum(onehot, axis=0, keepdims=True)
    cnt_ref[...] = run[...]
    info = jnp.where(li == INFO_CLASS, cls.astype(F32), 0.0)
    info = jnp.where(li == INFO_RANK, rank, info)
    info = jnp.where(li == INFO_WA, w_a, info)
    info = jnp.where(li == INFO_WB, w_b, info)
    info_ref[...] = info
    ids_ref[0] = info.T[:SUBLANES].astype(jnp.int32)


def route_tokens(logits):
    T = logits.shape[0]
    tb = ROUTE_TOKENS
    return pl.pallas_call(
        _route_kernel,
        grid=(T // tb,),
        in_specs=[pl.BlockSpec((tb, ROUTER_COLS), lambda i: (i, 0))],
        out_specs=[pl.BlockSpec((tb, ROUTER_COLS), lambda i: (i, 0)),
                   pl.BlockSpec((1, ROUTER_COLS), lambda i: (0, 0)),
                   pl.BlockSpec((1, SUBLANES, tb), lambda i: (i, 0, 0))],
        out_shape=[jax.ShapeDtypeStruct((T, ROUTER_COLS), F32), jax.ShapeDtypeStruct((1, ROUTER_COLS), F32),
                   jax.ShapeDtypeStruct((T // tb, SUBLANES, tb), jnp.int32)],
        scratch_shapes=[pltpu.VMEM((1, ROUTER_COLS), F32)],
        compiler_params=_cparams("arbitrary"),
        name="moe_route",
    )(logits)


def _pair_tables():
    a_tab, b_tab = [], []
    for g in range(N_GROUPS):
        for a in range(EXP_PER_GROUP):
            for b in range(a + 1, EXP_PER_GROUP):
                a_tab.append(g * EXP_PER_GROUP + a)
                b_tab.append(g * EXP_PER_GROUP + b)
    return jnp.array(a_tab, jnp.int32), jnp.array(b_tab, jnp.int32)


def _slot_kernel(ids_ref, start_ref, dest_ref):
    cls = ids_ref[0, INFO_CLASS:INFO_CLASS + 1, :]
    rank = ids_ref[0, INFO_RANK:INFO_RANK + 1, :]
    ci = lax.broadcasted_iota(jnp.int32, (ROUTER_COLS, cls.shape[1]), 0)
    start = jnp.sum(jnp.where(ci == cls, start_ref[...], 0), axis=0, keepdims=True)
    dest_ref[0] = jnp.broadcast_to(start + rank, dest_ref.shape[1:])


def slot_plan(ids, counts, n_blocks):
    nt, _, tb = ids.shape
    cnt = counts[0, :N_CLASSES].astype(jnp.int32)
    padded = (cnt + MOE_BLOCK - 1) // MOE_BLOCK * MOE_BLOCK
    pad_end = jnp.cumsum(padded)
    class_start = jnp.zeros((ROUTER_COLS, 1), jnp.int32).at[:N_CLASSES, 0].set(pad_end - padded)
    dest = pl.pallas_call(
        _slot_kernel,
        grid=(nt,),
        in_specs=[pl.BlockSpec((1, SUBLANES, tb), lambda i: (i, 0, 0)),
                  pl.BlockSpec((ROUTER_COLS, 1), lambda i: (0, 0))],
        out_specs=pl.BlockSpec((1, SUBLANES, tb), lambda i: (i, 0, 0)),
        out_shape=jax.ShapeDtypeStruct((nt, SUBLANES, tb), jnp.int32),
        compiler_params=_cparams("parallel"),
        name="moe_slots",
    )(ids, class_start)[:, 0, :].reshape(nt * tb)
    n_used = (pad_end[-1] // MOE_BLOCK).astype(jnp.int32).reshape(1)
    blk_cls = jnp.minimum(jnp.searchsorted(pad_end, jnp.arange(n_blocks) * MOE_BLOCK, side="right"),
                          N_CLASSES - 1)
    a_tab, b_tab = _pair_tables()
    return dest, a_tab[blk_cls], b_tab[blk_cls], n_used


DISPATCH_TOKENS = 256


def _wait_rows(buf, sem):
    pltpu.make_async_copy(buf, buf, sem).wait()


DMA_UNROLL = 8
TOKEN_TILE_ROWS = 8


def _store_token_tiles(tiles_ref, offset, pitch, x):
    n = x.shape[0]
    for j in range(x.shape[1] // LANES):
        tiles_ref[pl.ds(offset + j, n, stride=pitch), :] = x[:, j * LANES:(j + 1) * LANES]


def _load_token_tiles(tiles_ref, offset, pitch, n, width):
    return jnp.concatenate([tiles_ref[pl.ds(offset + j, n, stride=pitch), :] for j in range(width // LANES)],
                           axis=-1)


def _dispatch_kernel(dest_ref, hx_ref, hc_ref, g_ref, shx_ref, scx_ref, shc_ref, scc_ref, zeros_hbm,
                     xs_hbm, rows, sems, *, n_latent_blocks):
    del zeros_hbm
    i = pl.program_id(0)
    n = pl.num_programs(0)
    slot = i % 2
    tb = hx_ref.shape[0]

    @pl.when(i >= 2)
    def _():
        _wait_rows(rows.at[slot], sems.at[slot])

    def normed(h_ref, sh_ref, sc_ref):
        h = h_ref[...]
        ms = jnp.mean(h * h, axis=-1, keepdims=True)
        return h * lax.rsqrt(ms + EPS) * g_ref[...] * (1.0 + sc_ref[0]) + sh_ref[0]

    @pl.when(i < n_latent_blocks)
    def _():
        _store_token_tiles(rows.at[slot], 0, TOKEN_TILE_ROWS, normed(hx_ref, shx_ref, scx_ref))

    @pl.when(i >= n_latent_blocks)
    def _():
        _store_token_tiles(rows.at[slot], 0, TOKEN_TILE_ROWS, normed(hc_ref, shc_ref, scc_ref))

    def body(r, carry):
        dst = pl.multiple_of(dest_ref[0, 0, r] * TOKEN_TILE_ROWS, TOKEN_TILE_ROWS)
        pltpu.make_async_copy(rows.at[slot, pl.ds(r * TOKEN_TILE_ROWS, TOKEN_TILE_ROWS)],
                              xs_hbm.at[pl.ds(dst, TOKEN_TILE_ROWS)], sems.at[slot]).start()
        return carry
    lax.fori_loop(0, tb, body, 0, unroll=DMA_UNROLL)

    @pl.when(i == n - 1)
    def _():
        _wait_rows(rows.at[slot], sems.at[slot])

        @pl.when(n >= 2)
        def _():
            _wait_rows(rows.at[1 - slot], sems.at[1 - slot])


def moe_dispatch(h_x, h_c, dest, n_blocks, n2g, mod_x, mod_c, tokens_per_batch):
    Tx, D = h_x.shape
    tb = DISPATCH_TOKENS
    nxb = Tx // tb
    if h_c is None:
        h_c, mod_c, ncb = h_x, mod_x, 0
    else:
        ncb = h_c.shape[0] // tb
    per_b = tokens_per_batch // tb
    P = n_blocks * MOE_BLOCK
    tile_rows = D // LANES
    assert tile_rows == TOKEN_TILE_ROWS
    xi = lambda i: jnp.minimum(i, nxb - 1)
    ci = lambda i: jnp.maximum(i - nxb, 0)
    modx = pl.BlockSpec((1, 1, D), lambda i: (xi(i) // per_b, 0, 0))
    modc = pl.BlockSpec((1, 1, D), lambda i: (0, 0, 0))
    return pl.pallas_call(
        functools.partial(_dispatch_kernel, n_latent_blocks=nxb),
        grid=(nxb + ncb,),
        in_specs=[pl.BlockSpec((1, 1, tb), lambda i: (i, 0, 0), memory_space=pltpu.SMEM),
                  pl.BlockSpec((tb, D), lambda i: (xi(i), 0)),
                  pl.BlockSpec((tb, D), lambda i: (ci(i), 0)),
                  pl.BlockSpec((1, D), lambda i: (0, 0)),
                  modx, modx, modc, modc,
                  pl.BlockSpec(memory_space=pl.ANY)],
        out_specs=pl.BlockSpec(memory_space=pl.ANY),
        out_shape=jax.ShapeDtypeStruct((P * tile_rows, LANES), F32),
        scratch_shapes=[pltpu.VMEM((2, tb * tile_rows, LANES), F32), pltpu.SemaphoreType.DMA((2,))],
        input_output_aliases={8: 0},
        compiler_params=_cparams("arbitrary"),
        name="moe_dispatch",
    )(dest.reshape(-1, 1, tb), h_x, h_c, n2g.reshape(1, D), mod_x[0], mod_x[1], mod_c[0], mod_c[1],
      jnp.zeros((P * tile_rows, LANES), F32))


def _expert_pair_kernel(ea_ref, eb_ref, nused_ref, xs_ref, wga_ref, wua_ref, wda_ref, wgb_ref, wub_ref, wdb_ref,
                        o_ref):
    del ea_ref, eb_ref
    i = pl.program_id(0)
    D = wga_ref.shape[1]

    @pl.when(i < nused_ref[0])
    def _():
        xb = _load_token_tiles(xs_ref, 0, TOKEN_TILE_ROWS, MOE_BLOCK, D).astype(BF16)
        halves = []
        for wg_ref, wu_ref, wd_ref in ((wga_ref, wua_ref, wda_ref), (wgb_ref, wub_ref, wdb_ref)):
            gate = jnp.dot(xb, wg_ref[0], preferred_element_type=F32)
            up = jnp.dot(xb, wu_ref[0], preferred_element_type=F32)
            hid = (gate * jax.nn.sigmoid(gate) * up).astype(BF16)
            out = jnp.dot(hid, wd_ref[0], preferred_element_type=F32)
            halves.append(lax.bitcast_convert_type(out.astype(BF16).astype(F32), jnp.uint32))
        _store_token_tiles(o_ref, 0, TOKEN_TILE_ROWS, halves[0] | (halves[1] >> 16))

    @pl.when(i >= nused_ref[0])
    def _():
        o_ref[...] = jnp.zeros_like(o_ref)


def _unpack_pair(words):
    hi = lax.bitcast_convert_type(words & jnp.uint32(0xFFFF0000), F32)
    lo = lax.bitcast_convert_type(words << 16, F32)
    return hi, lo


def expert_pairs(xs, blk_a, blk_b, n_used, w_gate, w_up, w_down):
    D, DE = w_gate.shape[1:]
    P = xs.shape[0] // TOKEN_TILE_ROWS
    n_blocks = P // MOE_BLOCK
    wspec = lambda shape, which: pl.BlockSpec(shape, lambda i, ea, eb, nu: ((ea, eb)[which][i], 0, 0))
    grid_spec = pltpu.PrefetchScalarGridSpec(
        num_scalar_prefetch=3,
        grid=(n_blocks,),
        in_specs=[pl.BlockSpec((MOE_BLOCK * TOKEN_TILE_ROWS, LANES), lambda i, ea, eb, nu: (i, 0)),
                  wspec((1, D, DE), 0), wspec((1, D, DE), 0), wspec((1, DE, D), 0),
                  wspec((1, D, DE), 1), wspec((1, D, DE), 1), wspec((1, DE, D), 1)],
        out_specs=pl.BlockSpec((MOE_BLOCK * TOKEN_TILE_ROWS, LANES), lambda i, ea, eb, nu: (i, 0)),
    )
    return pl.pallas_call(
        _expert_pair_kernel,
        grid_spec=grid_spec,
        out_shape=jax.ShapeDtypeStruct((P * TOKEN_TILE_ROWS, LANES), jnp.uint32),
        compiler_params=_cparams("arbitrary"),
        name="moe_experts",
    )(blk_a, blk_b, n_used, xs, w_gate, w_up, w_down, w_gate, w_up, w_down)


def _gather_pairs(idx_ref, src_hbm, buf, sem, n_tokens):
    def body(r, carry):
        src = pl.multiple_of(idx_ref[0, 0, r] * TOKEN_TILE_ROWS, TOKEN_TILE_ROWS)
        pltpu.make_async_copy(src_hbm.at[pl.ds(src, TOKEN_TILE_ROWS)],
                              buf.at[pl.ds(r * TOKEN_TILE_ROWS, TOKEN_TILE_ROWS)], sem).start()
        return carry
    lax.fori_loop(0, n_tokens, body, 0, unroll=DMA_UNROLL)


def _collect_kernel(dest_ref, dest_next_ref, o_hbm, info_ref, h_ref, g2_ref, fg_ref, out_ref, obuf, sems, *,
                    final_norm):
    i = pl.program_id(0)
    n = pl.num_programs(0)
    slot = i % 2
    tb, D = h_ref.shape

    @pl.when(i == 0)
    def _():
        _gather_pairs(dest_ref, o_hbm, obuf.at[0], sems.at[0], tb)

    @pl.when(i + 1 < n)
    def _():
        _gather_pairs(dest_next_ref, o_hbm, obuf.at[1 - slot], sems.at[1 - slot], tb)

    _wait_rows(obuf.at[slot], sems.at[slot])
    e_a, e_b = _unpack_pair(_load_token_tiles(obuf.at[slot], 0, TOKEN_TILE_ROWS, tb, D))
    m = info_ref[:, INFO_WA:INFO_WA + 1] * e_a + info_ref[:, INFO_WB:INFO_WB + 1] * e_b
    h = h_ref[...] + g2_ref[0] * m
    if final_norm:
        ms = jnp.mean(h * h, axis=-1, keepdims=True)
        h = h * lax.rsqrt(ms + EPS) * fg_ref[...]
    out_ref[...] = h


def moe_collect(o_sorted, dest, info, block_offset, h_tokens, g2, tokens_per_batch, final_g, final_norm):
    T, D = h_tokens.shape
    tb = DISPATCH_TOKENS
    nt = T // tb
    per_b = tokens_per_batch // tb
    last = block_offset + nt - 1
    dest3 = dest.reshape(-1, 1, tb)
    return pl.pallas_call(
        functools.partial(_collect_kernel, final_norm=final_norm),
        grid=(nt,),
        in_specs=[pl.BlockSpec((1, 1, tb), lambda i: (block_offset + i, 0, 0), memory_space=pltpu.SMEM),
                  pl.BlockSpec((1, 1, tb), lambda i: (jnp.minimum(block_offset + i + 1, last), 0, 0),
                               memory_space=pltpu.SMEM),
                  pl.BlockSpec(memory_space=pl.ANY),
                  pl.BlockSpec((tb, ROUTER_COLS), lambda i: (block_offset + i, 0)),
                  pl.BlockSpec((tb, D), lambda i: (i, 0)),
                  pl.BlockSpec((1, 1, D), lambda i: (i // per_b, 0, 0)),
                  pl.BlockSpec((1, D), lambda i: (0, 0))],
        out_specs=pl.BlockSpec((tb, D), lambda i: (i, 0)),
        out_shape=jax.ShapeDtypeStruct((T, D), F32),
        scratch_shapes=[pltpu.VMEM((2, tb * TOKEN_TILE_ROWS, LANES), jnp.uint32), pltpu.SemaphoreType.DMA((2,))],
        compiler_params=_cparams("arbitrary"),
        name="moe_collect",
    )(dest3, dest3, o_sorted, info, h_tokens, g2, final_g.reshape(1, D))


CONV_MARGIN = 16


def _time_chunk(L):
    return min(L, 256)


LANES = 128


def _zero_margins(pad_ref, L):
    zeros = jnp.zeros((CONV_MARGIN, LANES), F32)
    for s in range(pad_ref.shape[0]):
        pad_ref[s, pl.ds(0, CONV_MARGIN), :] = zeros
        pad_ref[s, pl.ds(CONV_MARGIN + L, CONV_MARGIN), :] = zeros


def _dw_conv_slab(pad_ref, s, base, T, w_ref, b_ref, col, taps, pad_left):
    acc = jnp.broadcast_to(b_ref[:, col:col + LANES], (T, LANES))
    for k in range(taps):
        acc = acc + w_ref[k:k + 1, col:col + LANES] * pad_ref[s, pl.ds(base + (CONV_MARGIN - pad_left + k), T), :]
    return acc


def _conformer_kernel(u_ref, w_ref, b_ref, g_ref, beta_ref, o_ref, ypad):
    L = o_ref.shape[1]
    T = _time_chunk(L)
    C = D_GROUP
    n_slabs = C // LANES
    pad = (CONF_KERNEL - 1) // 2
    _zero_margins(ypad, L)

    def glu(j, carry):
        base = pl.multiple_of(j * T, T)
        for s in range(n_slabs):
            a = u_ref[0, pl.ds(base, T), s * LANES:(s + 1) * LANES]
            gate = u_ref[0, pl.ds(base, T), C + s * LANES:C + (s + 1) * LANES]
            ypad[s, pl.ds(CONV_MARGIN + base, T), :] = a * jax.nn.sigmoid(gate)
        return carry
    lax.fori_loop(0, L // T, glu, 0)

    def conv(j, carry):
        base = pl.multiple_of(j * T, T)
        acc = jnp.concatenate([_dw_conv_slab(ypad, s, base, T, w_ref, b_ref, s * LANES, CONF_KERNEL, pad)
                               for s in range(n_slabs)], axis=-1)
        mu = jnp.mean(acc, axis=-1, keepdims=True)
        cen = acc - mu
        var = jnp.mean(cen * cen, axis=-1, keepdims=True)
        y = cen * lax.rsqrt(var + EPS) * g_ref[...] + beta_ref[...]
        o_ref[0, pl.ds(base, T), :] = y * jax.nn.sigmoid(y)
        return carry
    lax.fori_loop(0, L // T, conv, 0)


def conformer_conv(u, w, b, ln_g, ln_b):
    B, L, _ = u.shape
    C = D_GROUP
    vec = pl.BlockSpec((1, C), lambda i: (0, 0))
    return pl.pallas_call(
        _conformer_kernel,
        grid=(B,),
        in_specs=[pl.BlockSpec((1, L, 2 * C), lambda i: (i, 0, 0)),
                  pl.BlockSpec((CONF_KERNEL, C), lambda i: (0, 0)), vec, vec, vec],
        out_specs=pl.BlockSpec((1, L, C), lambda i: (i, 0, 0)),
        out_shape=jax.ShapeDtypeStruct((B, L, C), F32),
        scratch_shapes=[pltpu.VMEM((C // LANES, L + 2 * CONV_MARGIN, LANES), F32)],
        compiler_params=_cparams("parallel"),
        name="conformer_conv",
    )(u, w, b.reshape(1, C), ln_g.reshape(1, C), ln_b.reshape(1, C))


def _gelu_tanh(x):
    return 0.5 * x * (1.0 + jnp.tanh(math.sqrt(2.0 / math.pi) * (x + 0.044715 * (x * x * x))))


def _lru_kernel(uc_ref, ux_ref, cw_ref, cb_ref, wcat_ref, bcat_ref, lam_ref, *rest, need_ctx):
    if need_ctx:
        oc_ref, ox_ref, cpad, xpad, a_s, b_s, yx, yc = rest
    else:
        ox_ref, cpad, xpad, a_s, b_s, yx = rest
        oc_ref = yc = None
    C = D_GROUP
    n_slabs = C // LANES
    Lc = uc_ref.shape[1]
    Lx = ux_ref.shape[1]
    pad_l = (LRU_CONV - 1) // 2

    def fill(pad_ref, u_ref, L):
        T = _time_chunk(L)
        _zero_margins(pad_ref, L)

        def body(j, carry):
            base = pl.multiple_of(j * T, T)
            for s in range(n_slabs):
                pad_ref[s, pl.ds(CONV_MARGIN + base, T), :] = u_ref[0, pl.ds(base, T),
                                                                    C + s * LANES:C + (s + 1) * LANES]
            return carry
        lax.fori_loop(0, L // T, body, 0)

    fill(cpad, uc_ref, Lc)
    fill(xpad, ux_ref, Lx)

    def coeffs(pad_ref, base, T, d):
        x = jnp.concatenate([_dw_conv_slab(pad_ref, s, base, T, cw_ref, cb_ref, s * LANES, LRU_CONV, pad_l)
                             for s in range(n_slabs)], axis=-1)
        t = jnp.tanh(jnp.dot(x.astype(BF16), wcat_ref[:, 2 * d * C:2 * (d + 1) * C],
                             preferred_element_type=F32) + bcat_ref[:, 2 * d * C:2 * (d + 1) * C])
        i = 0.5 * t[:, C:] + 0.5
        z = -lam_ref[d:d + 1, :]
        softplus = jnp.maximum(z, 0.0) + jnp.log(1.0 + jnp.exp(-jnp.abs(z)))
        half_rate = (-0.5 * LRU_C) * softplus
        a = jnp.exp(half_rate * t[:, :C] + half_rate)
        b = jnp.sqrt(1.0 - a * a) * (i * x)
        for s in range(n_slabs):
            a_s[d * n_slabs + s, pl.ds(0, T), :] = a[:, s * LANES:(s + 1) * LANES]
            b_s[d * n_slabs + s, pl.ds(0, T), :] = b[:, s * LANES:(s + 1) * LANES]

    def run(pad_ref, L, h, y_ref):
        T = _time_chunk(L)
        n = L // T

        def chunk(j, h):
            base_f = pl.multiple_of(j * T, T)
            base_b = pl.multiple_of((n - 1 - j) * T, T)
            coeffs(pad_ref, base_f, T, 0)
            coeffs(pad_ref, base_b, T, 1)

            def step(t, h):
                new = []
                for d, (base, row) in enumerate(((base_f, t), (base_b, T - 1 - t))):
                    for s in range(n_slabs):
                        k = d * n_slabs + s
                        hs = a_s[k, pl.ds(row, 1), :] * h[k] + b_s[k, pl.ds(row, 1), :]
                        if y_ref is not None:
                            y_ref[k, pl.ds(base + row, 1), :] = hs
                        new.append(hs)
                return tuple(new)
            return lax.fori_loop(0, T, step, h, unroll=8)
        return lax.fori_loop(0, n, chunk, h)

    h = tuple(jnp.zeros((1, LANES), F32) for _ in range(2 * n_slabs))
    h = run(cpad, Lc, h, yc)
    run(xpad, Lx, h, yx)

    def finish(u_ref, y_ref, o_ref, L):
        T = _time_chunk(L)

        def body(j, carry):
            base = pl.multiple_of(j * T, T)
            y = jnp.concatenate([y_ref[s, pl.ds(base, T), :] + y_ref[n_slabs + s, pl.ds(base, T), :]
                                 for s in range(n_slabs)], axis=-1)
            o_ref[0, pl.ds(base, T), :] = _gelu_tanh(u_ref[0, pl.ds(base, T), :C]) * y
            return carry
        lax.fori_loop(0, L // T, body, 0)

    finish(ux_ref, yx, ox_ref, Lx)
    if need_ctx:
        finish(uc_ref, yc, oc_ref, Lc)


def _block_diag(w):
    H, n, _ = w.shape
    eye = jnp.eye(H, dtype=w.dtype)
    return (eye[:, None, :, None] * w[:, :, None, :]).reshape(H * n, H * n)


def rglru_mixer(uc, ux, lp, need_ctx):
    B, Lc, _ = uc.shape
    Lx = ux.shape[1]
    C = D_GROUP
    wcat = (0.5 * jnp.concatenate([_block_diag(lp["lru_wa"][0]), _block_diag(lp["lru_wx"][0]),
                                   _block_diag(lp["lru_wa"][1]), _block_diag(lp["lru_wx"][1])], axis=1)).astype(BF16)
    bcat = 0.5 * jnp.concatenate([lp["lru_ba"][0], lp["lru_bx"][0], lp["lru_ba"][1], lp["lru_bx"][1]]).reshape(1, 4 * C)
    full = lambda r, c: pl.BlockSpec((r, c), lambda i: (0, 0))
    seq = lambda L, n: pl.BlockSpec((1, L, n), lambda i: (i, 0, 0))
    out_specs = [seq(Lx, C)]
    out_shape = [jax.ShapeDtypeStruct((B, Lx, C), F32)]
    if need_ctx:
        out_specs = [seq(Lc, C)] + out_specs
        out_shape = [jax.ShapeDtypeStruct((B, Lc, C), F32)] + out_shape
    T = _time_chunk(Lx)
    slab = lambda rows, n=1: pltpu.VMEM((n * C // LANES, rows, LANES), F32)
    scratch = [slab(Lc + 2 * CONV_MARGIN), slab(Lx + 2 * CONV_MARGIN), slab(T, 2), slab(T, 2), slab(Lx, 2)]
    if need_ctx:
        scratch.append(slab(Lc, 2))
    res = pl.pallas_call(
        functools.partial(_lru_kernel, need_ctx=need_ctx),
        grid=(B,),
        in_specs=[seq(Lc, 2 * C), seq(Lx, 2 * C), full(LRU_CONV, C), full(1, C), full(C, 4 * C),
                  full(1, 4 * C), full(2, C)],
        out_specs=out_specs,
        out_shape=out_shape,
        scratch_shapes=scratch,
        compiler_params=_cparams("parallel"),
        name="rglru",
    )(uc, ux, lp["lru_conv_w"], lp["lru_conv_b"].reshape(1, C), wcat, bcat, lp["lru_lambda"])
    if need_ctx:
        return res[0], res[1]
    return None, res[0]


HY_SHORT = 3


def _short_conv(pad_ref, base, T, w_ref, b_ref, c0, c1):
    return jnp.concatenate([_dw_conv_slab(pad_ref, col // LANES, base, T, w_ref, b_ref, col, HY_SHORT, 1)
                            for col in range(c0, c1, LANES)], axis=-1)


def _fill_padded(pad_ref, u_ref, L, T):
    _zero_margins(pad_ref, L)

    def body(j, carry):
        base = pl.multiple_of(j * T, T)
        for s in range(pad_ref.shape[0]):
            pad_ref[s, pl.ds(CONV_MARGIN + base, T), :] = u_ref[0, pl.ds(base, T), s * LANES:(s + 1) * LANES]
        return carry
    lax.fori_loop(0, L // T, body, 0)


def _hyena_pre_kernel(u_ref, w_ref, b_ref, z_ref, upad):
    L = u_ref.shape[1]
    T = _time_chunk(L)
    C = D_GROUP
    _fill_padded(upad, u_ref, L, T)

    def body(j, carry):
        base = pl.multiple_of(j * T, T)
        x1 = _short_conv(upad, base, T, w_ref, b_ref, C, 2 * C)
        v = _short_conv(upad, base, T, w_ref, b_ref, 2 * C, 3 * C)
        z_ref[pl.ds(base, T), :] = (x1 * v).astype(BF16)
        return carry
    lax.fori_loop(0, L // T, body, 0)


def _hyena_post_kernel(u_ref, y_ref, w_ref, b_ref, bias_ref, o_ref, upad):
    L = u_ref.shape[1]
    T = _time_chunk(L)
    C = D_GROUP
    _fill_padded(upad, u_ref, L, T)

    def body(j, carry):
        base = pl.multiple_of(j * T, T)
        x0 = _short_conv(upad, base, T, w_ref, b_ref, 0, C)
        x1 = _short_conv(upad, base, T, w_ref, b_ref, C, 2 * C)
        v = _short_conv(upad, base, T, w_ref, b_ref, 2 * C, 3 * C)
        o_ref[0, pl.ds(base, T), :] = x0 * (y_ref[pl.ds(base, T), :] + (x1 * v) * bias_ref[...])
        return carry
    lax.fori_loop(0, L // T, body, 0)


def _spectrum_kernel(f_ref, z_ref, ha_ref, hb_ref, hc_ref, y_ref):
    tf = ha_ref.shape[0]
    acc = jnp.dot(f_ref[...], z_ref[...], preferred_element_type=F32)
    zr = acc[:tf]
    zi = acc[tf:]
    y_ref[:tf, :] = (zr * ha_ref[...] - zi * hb_ref[...]).astype(BF16)
    y_ref[tf:, :] = (zr * hb_ref[...] + zi * hc_ref[...]).astype(BF16)


def _idft_kernel(f_ref, y_ref, o_ref):
    o_ref[...] = jnp.dot(f_ref[...], y_ref[...], preferred_element_type=F32)


def dft_tables(L):
    N = 2 * L
    tf = min(256, L)
    k = jnp.arange(L, dtype=jnp.int32)
    n = jnp.arange(L, dtype=jnp.int32)
    ang = (2.0 * math.pi / N) * ((k[:, None] * n[None, :]) % N).astype(F32)
    cos = jnp.cos(ang)
    sin = jnp.sin(ang)
    nyq = jnp.where(n % 2 == 0, 1.0, -1.0).astype(F32)
    f_re = cos
    f_im = (-sin).at[0].set(nyq)
    fwd = jnp.stack([f_re.reshape(L // tf, tf, L), f_im.reshape(L // tf, tf, L)], axis=1).reshape(N, L)
    ck = jnp.where(k == 0, 1.0, 2.0).astype(F32)[:, None] / N
    i_re = cos * ck
    i_im = (-sin * ck).at[0].set(nyq / N)
    inv = jnp.stack([i_re.reshape(L // tf, tf, L), i_im.reshape(L // tf, tf, L)], axis=1).reshape(N, L).T
    return fwd.astype(BF16), inv.astype(BF16)


def filter_spectrum(h_fwd, h_bwd):
    L, C = h_fwd.shape
    k = jnp.concatenate([h_fwd, jnp.zeros((1, C), F32), h_bwd[1:][::-1]], axis=0)
    hf = jnp.fft.rfft(k, axis=0)
    hr = jnp.real(hf)
    hi = jnp.imag(hf)
    a = hr[:L]
    b = hi[:L].at[0].set(0.0)
    c = hr[:L].at[0].set(hr[L])
    return a, b, c


def hyena_mixer(u, lp, tables):
    B, L, _ = u.shape
    C = D_GROUP
    N = 2 * L
    fwd, inv = tables
    tf = min(256, L)
    T = _time_chunk(L)
    w, bsh = lp["hy_short_w"], lp["hy_short_b"].reshape(1, 3 * C)
    z2 = pl.pallas_call(
        _hyena_pre_kernel,
        grid=(B,),
        in_specs=[pl.BlockSpec((1, L, 3 * C), lambda b: (b, 0, 0)),
                  pl.BlockSpec((HY_SHORT, 3 * C), lambda b: (0, 0)),
                  pl.BlockSpec((1, 3 * C), lambda b: (0, 0))],
        out_specs=pl.BlockSpec((L, C), lambda b: (0, b)),
        out_shape=jax.ShapeDtypeStruct((L, B * C), BF16),
        scratch_shapes=[pltpu.VMEM((3 * C // LANES, L + 2 * CONV_MARGIN, LANES), F32)],
        compiler_params=_cparams("parallel"),
        name="hyena_pre",
    )(u, w, bsh)

    h_fwd, h_bwd = _hyena_filters(L, lp)
    tn = 2 * C
    ha, hb, hc = [jnp.tile(t, (1, tn // C)) for t in filter_spectrum(h_fwd, h_bwd)]
    hspec = pl.BlockSpec((tf, tn), lambda i, j: (i, 0))
    y2 = pl.pallas_call(
        _spectrum_kernel,
        grid=(L // tf, B * C // tn),
        in_specs=[pl.BlockSpec((2 * tf, L), lambda i, j: (i, 0)),
                  pl.BlockSpec((L, tn), lambda i, j: (0, j)), hspec, hspec, hspec],
        out_specs=pl.BlockSpec((2 * tf, tn), lambda i, j: (i, j)),
        out_shape=jax.ShapeDtypeStruct((N, B * C), BF16),
        compiler_params=_cparams("parallel", "parallel"),
        name="hyena_spectrum",
    )(fwd, z2, ha, hb, hc)

    tl = min(256, L)
    yt = pl.pallas_call(
        _idft_kernel,
        grid=(L // tl, B * C // tn),
        in_specs=[pl.BlockSpec((tl, N), lambda i, j: (i, 0)),
                  pl.BlockSpec((N, tn), lambda i, j: (0, j))],
        out_specs=pl.BlockSpec((tl, tn), lambda i, j: (i, j)),
        out_shape=jax.ShapeDtypeStruct((L, B * C), F32),
        compiler_params=_cparams("parallel", "parallel"),
        name="hyena_idft",
    )(inv, y2)

    return pl.pallas_call(
        _hyena_post_kernel,
        grid=(B,),
        in_specs=[pl.BlockSpec((1, L, 3 * C), lambda b: (b, 0, 0)),
                  pl.BlockSpec((L, C), lambda b: (0, b)),
                  pl.BlockSpec((HY_SHORT, 3 * C), lambda b: (0, 0)),
                  pl.BlockSpec((1, 3 * C), lambda b: (0, 0)),
                  pl.BlockSpec((1, C), lambda b: (0, 0))],
        out_specs=pl.BlockSpec((1, L, C), lambda b: (b, 0, 0)),
        out_shape=jax.ShapeDtypeStruct((B, L, C), F32),
        scratch_shapes=[pltpu.VMEM((3 * C // LANES, L + 2 * CONV_MARGIN, LANES), F32)],
        compiler_params=_cparams("parallel"),
        name="hyena_post",
    )(u, yt, w, bsh, lp["hy_bias"].reshape(1, C))


FFT_N2 = 128
FFT_UNROLL = 8


class _FftPlan:
    def __init__(self, L):
        self.L = L
        self.N = 2 * L
        self.N1 = self.N // FFT_N2
        self.KH = self.N1 // 2 + 1
        self.KP = -(-self.KH // 8) * 8
        self.PA = 2 * self.KP + 4


def fft_tables(L):
    p = _FftPlan(L)
    N, N1, KH, KP = p.N, p.N1, p.KH, p.KP
    n2 = jnp.arange(FFT_N2, dtype=jnp.int32)
    k1 = jnp.arange(KP, dtype=jnp.int32)
    n1 = jnp.arange(N1, dtype=jnp.int32)
    n = FFT_N2 * n1[None, None, :] + n2[:, None, None]
    ang = (2.0 * math.pi / N) * ((k1[None, :, None] * n) % N).astype(F32)
    keep = (k1 < KH)[None, :, None]
    g_re = jnp.where(keep, jnp.cos(ang), 0.0)
    g_im = jnp.where(keep, -jnp.sin(ang), 0.0)
    ga_full = jnp.concatenate([g_re, g_im], axis=1)
    ck = jnp.where((k1 == 0) | (k1 == N1 // 2), 1.0, 2.0) / N
    ga_inv = jnp.swapaxes(ga_full[:, :, :N1 // 2] * jnp.tile(ck, 2)[None, :, None], 1, 2)
    kk = jnp.arange(FFT_N2, dtype=jnp.int32)
    ang2 = (2.0 * math.pi / FFT_N2) * ((kk[:, None] * kk[None, :]) % FFT_N2).astype(F32)
    fr, fi = jnp.cos(ang2), -jnp.sin(ang2)
    fb = jnp.block([[fr, -fi], [fi, fr]])
    fb_inv = jnp.block([[fr, fi], [-fi, fr]])
    return dict(ga_half=ga_full[:, :, :N1 // 2].astype(BF16), ga_full=ga_full.astype(BF16),
                ga_inv=ga_inv.astype(BF16), fb=fb.astype(BF16), fb_inv=fb_inv.astype(BF16))


def _fft_stage_a(x_ref, ga_ref, s_ref, plan, n1_count):
    n_slabs = x_ref.shape[0]

    def body(n2, carry):
        xs = jnp.concatenate([x_ref[s, pl.ds(n2, n1_count, stride=FFT_N2), :] for s in range(n_slabs)], axis=-1)
        a = jnp.dot(ga_ref[n2], xs.astype(BF16), preferred_element_type=F32)
        for s in range(n_slabs):
            s_ref[s, pl.ds(n2 * plan.PA, 2 * plan.KP), :] = a[:, s * LANES:(s + 1) * LANES]
        return carry
    lax.fori_loop(0, FFT_N2, body, 0, unroll=FFT_UNROLL)


def _fft_load_k1(s_ref, k1, plan):
    n_slabs = s_ref.shape[0]
    re = jnp.concatenate([s_ref[s, pl.ds(k1, FFT_N2, stride=plan.PA), :] for s in range(n_slabs)], axis=-1)
    im = jnp.concatenate([s_ref[s, pl.ds(plan.KP + k1, FFT_N2, stride=plan.PA), :] for s in range(n_slabs)], axis=-1)
    return jnp.concatenate([re, im], axis=0).astype(BF16)


def _fft_filter_kernel(k_ref, ga_ref, fb_ref, h_ref, s_ref, *, plan):
    _fft_stage_a(k_ref, ga_ref, s_ref, plan, plan.N1)

    def body(k1, carry):
        h_ref[k1] = jnp.dot(fb_ref[...], _fft_load_k1(s_ref, k1, plan), preferred_element_type=F32).astype(BF16)
        return carry
    lax.fori_loop(0, plan.KH, body, 0)


def _fft_conv_kernel(z_ref, ga_ref, gi_ref, fb_ref, fbi_ref, h_ref, y_ref, s_ref, *, plan):
    zs = z_ref.at[0]
    ys = y_ref.at[0]
    n_slabs = zs.shape[0]
    half = FFT_N2
    _fft_stage_a(zs, ga_ref, s_ref, plan, plan.N1 // 2)

    def body_b(k1, carry):
        x = jnp.dot(fb_ref[...], _fft_load_k1(s_ref, k1, plan), preferred_element_type=F32)
        h = h_ref[k1].astype(F32)
        xr, xi, hr, hi = x[:half], x[half:], h[:half], h[half:]
        y = jnp.concatenate([xr * hr - xi * hi, xr * hi + xi * hr], axis=0).astype(BF16)
        b = jnp.dot(fbi_ref[...], y, preferred_element_type=F32)
        for s in range(n_slabs):
            s_ref[s, pl.ds(k1, FFT_N2, stride=plan.PA), :] = b[:half, s * LANES:(s + 1) * LANES]
            s_ref[s, pl.ds(plan.KP + k1, FFT_N2, stride=plan.PA), :] = b[half:, s * LANES:(s + 1) * LANES]
        return carry
    lax.fori_loop(0, plan.KH, body_b, 0, unroll=3)

    def body_a(n2, carry):
        b = jnp.concatenate([s_ref[s, pl.ds(n2 * plan.PA, 2 * plan.KP), :] for s in range(n_slabs)], axis=-1)
        y = jnp.dot(gi_ref[n2], b.astype(BF16), preferred_element_type=F32)
        for s in range(n_slabs):
            ys[s, pl.ds(n2, plan.N1 // 2, stride=FFT_N2), :] = y[:, s * LANES:(s + 1) * LANES]
        return carry
    lax.fori_loop(0, FFT_N2, body_a, 0, unroll=FFT_UNROLL)


def fft_filter_spectrum(h_fwd, h_bwd, tabs):
    L, C = h_fwd.shape
    plan = _FftPlan(L)
    n_slabs = C // LANES
    k = jnp.concatenate([h_fwd, jnp.zeros((1, C), F32), h_bwd[1:][::-1]], axis=0)
    k = k.reshape(plan.N, n_slabs, LANES).transpose(1, 0, 2)
    full = lambda shape: pl.BlockSpec(shape, lambda i: (0,) * len(shape))
    return pl.pallas_call(
        functools.partial(_fft_filter_kernel, plan=plan),
        grid=(1,),
        in_specs=[full((n_slabs, plan.N, LANES)), full((FFT_N2, 2 * plan.KP, plan.N1)),
                  full((2 * FFT_N2, 2 * FFT_N2))],
        out_specs=full((plan.KH, 2 * FFT_N2, C)),
        out_shape=jax.ShapeDtypeStruct((plan.KH, 2 * FFT_N2, C), BF16),
        scratch_shapes=[pltpu.VMEM((n_slabs, FFT_N2 * plan.PA, LANES), F32)],
        compiler_params=_cparams("arbitrary"),
        name="hyena_filter_fft",
    )(k, tabs["ga_full"], tabs["fb"])


def fft_long_conv(z, h_spec, tabs):
    B, n_slabs, L, _ = z.shape
    plan = _FftPlan(L)
    C = n_slabs * LANES
    full = lambda shape: pl.BlockSpec(shape, lambda b: (0,) * len(shape))
    seq = pl.BlockSpec((1, n_slabs, L, LANES), lambda b: (b, 0, 0, 0))
    return pl.pallas_call(
        functools.partial(_fft_conv_kernel, plan=plan),
        grid=(B,),
        in_specs=[seq, full((FFT_N2, 2 * plan.KP, plan.N1 // 2)), full((FFT_N2, plan.N1 // 2, 2 * plan.KP)),
                  full((2 * FFT_N2, 2 * FFT_N2)), full((2 * FFT_N2, 2 * FFT_N2)),
                  full((plan.KH, 2 * FFT_N2, C))],
        out_specs=seq,
        out_shape=jax.ShapeDtypeStruct((B, n_slabs, L, LANES), F32),
        scratch_shapes=[pltpu.VMEM((n_slabs, FFT_N2 * plan.PA, LANES), F32)],
        compiler_params=_cparams("parallel"),
        name="hyena_fft_conv",
    )(z, tabs["ga_half"], tabs["ga_inv"], tabs["fb"], tabs["fb_inv"], h_spec)


def _hyena_pre_slab_kernel(u_ref, w_ref, b_ref, z_ref, upad):
    L = u_ref.shape[1]
    T = _time_chunk(L)
    C = D_GROUP
    _fill_padded(upad, u_ref, L, T)

    def body(j, carry):
        base = pl.multiple_of(j * T, T)
        for s in range(C // LANES):
            x1 = _dw_conv_slab(upad, C // LANES + s, base, T, w_ref, b_ref, C + s * LANES, HY_SHORT, 1)
            v = _dw_conv_slab(upad, 2 * C // LANES + s, base, T, w_ref, b_ref, 2 * C + s * LANES, HY_SHORT, 1)
            z_ref[0, s, pl.ds(base, T), :] = x1 * v
        return carry
    lax.fori_loop(0, L // T, body, 0)


def _hyena_post_slab_kernel(u0_ref, z_ref, y_ref, w_ref, b_ref, bias_ref, o_ref, upad):
    L = u0_ref.shape[1]
    T = _time_chunk(L)
    C = D_GROUP
    _fill_padded(upad, u0_ref, L, T)

    def body(j, carry):
        base = pl.multiple_of(j * T, T)
        x0 = _short_conv(upad, base, T, w_ref, b_ref, 0, C)
        z = jnp.concatenate([z_ref[0, s, pl.ds(base, T), :] for s in range(C // LANES)], axis=-1)
        y = jnp.concatenate([y_ref[0, s, pl.ds(base, T), :] for s in range(C // LANES)], axis=-1)
        o_ref[0, pl.ds(base, T), :] = x0 * (y + z * bias_ref[...])
        return carry
    lax.fori_loop(0, L // T, body, 0)


def hyena_mixer_fft(u, lp, tabs):
    B, L, _ = u.shape
    C = D_GROUP
    n_slabs = C // LANES
    w, bsh = lp["hy_short_w"], lp["hy_short_b"].reshape(1, 3 * C)
    useq = pl.BlockSpec((1, L, 3 * C), lambda b: (b, 0, 0))
    slabs = pl.BlockSpec((1, n_slabs, L, LANES), lambda b: (b, 0, 0, 0))
    wspec = pl.BlockSpec((HY_SHORT, 3 * C), lambda b: (0, 0))
    bspec = pl.BlockSpec((1, 3 * C), lambda b: (0, 0))
    pad_scratch = pltpu.VMEM((3 * C // LANES, L + 2 * CONV_MARGIN, LANES), F32)
    z = pl.pallas_call(
        _hyena_pre_slab_kernel,
        grid=(B,),
        in_specs=[useq, wspec, bspec],
        out_specs=slabs,
        out_shape=jax.ShapeDtypeStruct((B, n_slabs, L, LANES), F32),
        scratch_shapes=[pad_scratch],
        compiler_params=_cparams("parallel"),
        name="hyena_pre",
    )(u, w, bsh)
    h_fwd, h_bwd = _hyena_filters(L, lp)
    y = fft_long_conv(z, fft_filter_spectrum(h_fwd, h_bwd, tabs), tabs)
    return pl.pallas_call(
        _hyena_post_slab_kernel,
        grid=(B,),
        in_specs=[pl.BlockSpec((1, L, C), lambda b: (b, 0, 0)), slabs, slabs, wspec, bspec,
                  pl.BlockSpec((1, C), lambda b: (0, 0))],
        out_specs=pl.BlockSpec((1, L, C), lambda b: (b, 0, 0)),
        out_shape=jax.ShapeDtypeStruct((B, L, C), F32),
        scratch_shapes=[pltpu.VMEM((n_slabs, L + 2 * CONV_MARGIN, LANES), F32)],
        compiler_params=_cparams("parallel"),
        name="hyena_post",
    )(u, z, y, w, bsh, lp["hy_bias"].reshape(1, C))


def _hyena_filters(L, lp):
    t = jnp.linspace(0.0, 1.0, L, dtype=F32)[:, None]
    bands = (HY_EMB - 1) // 2
    w = 2.0 * math.pi * jnp.arange(L, dtype=F32)[:, None] / L
    f = jnp.linspace(1e-4, bands - 1, bands, dtype=F32)[None]
    z = jnp.concatenate([t, jnp.cos(f * w), -jnp.sin(f * w)], axis=-1)
    hdn = jnp.sin(z @ lp["hy_ffn_w1"] + lp["hy_ffn_b1"])
    hdn = jnp.sin(hdn @ lp["hy_ffn_w2"] + lp["hy_ffn_b2"])
    h = (hdn @ lp["hy_ffn_w3"]).reshape(L, 2, D_GROUP)
    max_decay = math.log(HY_TARGET) / HY_FAST_DECAY
    min_decay = math.log(HY_TARGET) / HY_SLOW_DECAY
    deltas = jnp.linspace(min_decay, max_decay, D_GROUP, dtype=F32)
    h = h * jnp.exp(-t * jnp.abs(deltas))[:, None, :]
    h = h / (jnp.sum(jnp.abs(h), axis=(0, 1), keepdims=True) + EPS)
    return h[:, 0], h[:, 1]


def _layer(hc, hx, c_silu_all, lp, need_ctx, final_g, final_norm, tables_x, tables_c):
    B, S, D = hx.shape
    C = hc.shape[1]
    mod = small_linear(c_silu_all, lp["ada_w"], lp["ada_b"])
    mod_x = mod[:B].reshape(B, 6, 1, D)
    mod_c = jnp.broadcast_to(mod[B].reshape(1, 6, 1, D), (B, 6, 1, D))
    w_ext = extend_w_in(lp["w_in"])
    cos_x, sin_x = rope_tables(S, True)
    cos_c, sin_c = rope_tables(C, False)
    hy_x, cf_x, at_x, lr_x = in_proj(hx, mod_x[:, 0], mod_x[:, 1], lp["norm1_g"], w_ext, cos_x, sin_x, tm=512)
    hy_c, cf_c, at_c, lr_c = in_proj(hc, mod_c[:, 0], mod_c[:, 1], lp["norm1_g"], w_ext, cos_c, sin_c, tm=256)

    yd_c, yd_x = rglru_mixer(lr_c, lr_x, lp, need_ctx)
    conf = lambda u: conformer_conv(u, lp["conf_dw_w"], lp["conf_dw_b"], lp["conf_ln_g"], lp["conf_ln_b"])
    ys_x = [hyena_mixer_fft(hy_x, lp, tables_x), conf(cf_x),
            window_attention(at_x, at_c, lp["attn_sink"]), yd_x]

    w_out = lp["w_out"].astype(BF16)
    w_router = jnp.zeros((D, ROUTER_COLS), F32)
    w_router = w_router.at[:, :N_GROUPS].set(lp["router_g_w"]).at[:, N_GROUPS:N_GROUPS + N_EXPERTS].set(lp["router_e_w"])
    w_router = w_router.astype(BF16)
    b_router = jnp.zeros((1, ROUTER_COLS), F32)
    b_router = b_router.at[0, :N_GROUPS].set(lp["router_g_b"]).at[0, N_GROUPS:N_GROUPS + N_EXPERTS].set(lp["router_e_b"])

    hx1, lg_x = out_proj(ys_x, hx, mod_x[:, 2], lp["group_norm_g"], w_out, lp["norm2_g"],
                         mod_x[:, 3], mod_x[:, 4], w_router, b_router, tm=512)
    h_tok = hx1.reshape(B * S, D)
    hc_tok = None
    lg = lg_x.reshape(B * S, ROUTER_COLS)
    if need_ctx:
        ys_c = [hyena_mixer(hy_c, lp, tables_c), conf(cf_c),
                context_attention(at_c, lp["attn_sink"]), yd_c]
        hc1, lg_c = out_proj(ys_c, hc, mod_c[:, 2], lp["group_norm_g"], w_out, lp["norm2_g"],
                             mod_c[:, 3], mod_c[:, 4], w_router, b_router, tm=256)
        hc_tok = hc1.reshape(B * C, D)
        lg = jnp.concatenate([lg, lg_c.reshape(B * C, ROUTER_COLS)], axis=0)

    T = lg.shape[0]
    n_blocks = -(-T // MOE_BLOCK) + N_CLASSES
    info, counts, ids = route_tokens(lg)
    dest, blk_a, blk_b, n_used = slot_plan(ids, counts, n_blocks)
    xs = moe_dispatch(h_tok, hc_tok, dest, n_blocks, lp["norm2_g"], (mod_x[:, 3], mod_x[:, 4]),
                      (mod_c[:, 3], mod_c[:, 4]), S)
    o_sorted = expert_pairs(xs, blk_a, blk_b, n_used, lp["exp_w_gate"].astype(BF16),
                            lp["exp_w_up"].astype(BF16), lp["exp_w_down"].astype(BF16))
    hx2 = moe_collect(o_sorted, dest, info, 0, h_tok, mod_x[:, 5], S, final_g, final_norm)
    hx2 = hx2.reshape(B, S, D)
    if need_ctx:
        hc2 = moe_collect(o_sorted, dest, info, B * S // DISPATCH_TOKENS, hc_tok, mod_c[:, 5], C,
                          final_g, False).reshape(B, C, D)
    else:
        hc2 = hc
    return hc2, hx2


def kernel(x, c, ctx, c_ctx, norm1_g, norm2_g, ada_w, ada_b, w_in, hy_short_w, hy_short_b, hy_ffn_w1, hy_ffn_b1, hy_ffn_w2, hy_ffn_b2, hy_ffn_w3, hy_bias, conf_dw_w, conf_dw_b, conf_ln_g, conf_ln_b, attn_sink, lru_conv_w, lru_conv_b, lru_wa, lru_ba, lru_wx, lru_bx, lru_lambda, group_norm_g, w_out, router_g_w, router_g_b, router_e_w, router_e_b, exp_w_gate, exp_w_up, exp_w_down, final_norm_g):
    stacked = dict(norm1_g=norm1_g, norm2_g=norm2_g, ada_w=ada_w, ada_b=ada_b, w_in=w_in,
                   hy_short_w=hy_short_w, hy_short_b=hy_short_b, hy_ffn_w1=hy_ffn_w1, hy_ffn_b1=hy_ffn_b1,
                   hy_ffn_w2=hy_ffn_w2, hy_ffn_b2=hy_ffn_b2, hy_ffn_w3=hy_ffn_w3, hy_bias=hy_bias,
                   conf_dw_w=conf_dw_w, conf_dw_b=conf_dw_b, conf_ln_g=conf_ln_g, conf_ln_b=conf_ln_b,
                   attn_sink=attn_sink, lru_conv_w=lru_conv_w, lru_conv_b=lru_conv_b, lru_wa=lru_wa,
                   lru_ba=lru_ba, lru_wx=lru_wx, lru_bx=lru_bx, lru_lambda=lru_lambda,
                   group_norm_g=group_norm_g, w_out=w_out, router_g_w=router_g_w, router_g_b=router_g_b,
                   router_e_w=router_e_w, router_e_b=router_e_b, exp_w_gate=exp_w_gate,
                   exp_w_up=exp_w_up, exp_w_down=exp_w_down)
    depth = norm1_g.shape[0]
    B = x.shape[0]
    cs = jnp.concatenate([jax.nn.silu(c), jnp.broadcast_to(jax.nn.silu(c_ctx)[None], (8, c.shape[1]))], axis=0)
    hc, hx = ctx, x
    tables_x = fft_tables(x.shape[1])
    tables_c = dft_tables(ctx.shape[1])
    for l in range(depth):
        lp = {k: v[l] for k, v in stacked.items()}
        hc, hx = _layer(hc, hx, cs, lp, need_ctx=(l < depth - 1), final_g=final_norm_g,
                        final_norm=(l == depth - 1), tables_x=tables_x, tables_c=tables_c)
    return hx
```

```python
import functools
import math

import jax
import jax.numpy as jnp
from jax import lax
from jax.experimental import pallas as pl
from jax.experimental.pallas import tpu as pltpu

F32 = jnp.float32
BF16 = jnp.bfloat16

EPS = 1e-6
NEG_INF = -1e30
GRID_W = 64
N_MIXERS = 4
D_GROUP = 256
HY_COLS = 3 * D_GROUP
CONF_COLS = 2 * D_GROUP
ATT_HEADS = 4
ATT_KV_HEADS = 2
HEAD_DIM = 64
ATT_COLS = (ATT_HEADS + 2 * ATT_KV_HEADS) * HEAD_DIM
LRU_COLS = 2 * D_GROUP
QK_COLS = (ATT_HEADS + ATT_KV_HEADS) * HEAD_DIM
WINDOW = 128
ATT_BLOCK = 128
ROPE_BASE = 10000.0
HY_EMB = 33
HY_FAST_DECAY = 0.3
HY_SLOW_DECAY = 1.5
HY_TARGET = 1e-2
CONF_KERNEL = 31
LRU_HEADS = 4
LRU_CONV = 4
LRU_C = 8.0
N_GROUPS = 4
EXP_PER_GROUP = 8
N_EXPERTS = N_GROUPS * EXP_PER_GROUP
TOP_K = 2
MOE_BLOCK = 256
ROUTER_COLS = 128

VMEM_LIMIT_BYTES = 56 * 1024 * 1024


def _cparams(*sem):
    return pltpu.CompilerParams(dimension_semantics=sem, vmem_limit_bytes=VMEM_LIMIT_BYTES)


def _linear_kernel(x_ref, w_ref, b_ref, o_ref):
    o_ref[...] = jnp.dot(x_ref[...], w_ref[...], preferred_element_type=F32,
                         precision=lax.Precision.HIGHEST) + b_ref[...]


def small_linear(x, w, b, tn=1024):
    M, K = x.shape
    N = w.shape[1]
    return pl.pallas_call(
        _linear_kernel,
        grid=(N // tn,),
        in_specs=[pl.BlockSpec((M, K), lambda j: (0, 0)),
                  pl.BlockSpec((K, tn), lambda j: (0, j)),
                  pl.BlockSpec((1, tn), lambda j: (0, j))],
        out_specs=pl.BlockSpec((M, tn), lambda j: (0, j)),
        out_shape=jax.ShapeDtypeStruct((M, N), F32),
        compiler_params=_cparams("parallel"),
        name="ada_linear",
    )(x, w, b.reshape(1, N))


def _in_proj_kernel(x_ref, sh_ref, sc_ref, g_ref, w_ref, cos_ref, sin_ref,
                    hy_ref, cf_ref, at_ref, lr_ref):
    x = x_ref[0]
    ms = jnp.mean(x * x, axis=-1, keepdims=True)
    y = x * lax.rsqrt(ms + EPS) * g_ref[...]
    y = y * (1.0 + sc_ref[0]) + sh_ref[0]
    u = jnp.dot(y.astype(BF16), w_ref[...], preferred_element_type=F32)
    c0 = HY_COLS
    c1 = c0 + CONF_COLS
    c2 = c1 + ATT_COLS
    c3 = c2 + LRU_COLS
    hy_ref[0] = u[:, :c0]
    cf_ref[0] = u[:, c0:c1]
    lr_ref[0] = u[:, c2:c3]
    qk = u[:, c1:c1 + QK_COLS]
    qk_rot = u[:, c3:c3 + QK_COLS]
    at_ref[0, :, :QK_COLS] = qk * cos_ref[...] + qk_rot * sin_ref[...]
    at_ref[0, :, QK_COLS:] = u[:, c1 + QK_COLS:c2]


def in_proj(h, shift, scale, g, w_ext, cos_t, sin_t, tm):
    B, L, D = h.shape
    NW = w_ext.shape[1]
    outs = [HY_COLS, CONF_COLS, ATT_COLS, LRU_COLS]
    return pl.pallas_call(
        _in_proj_kernel,
        grid=(B, L // tm),
        in_specs=[pl.BlockSpec((1, tm, D), lambda b, i: (b, i, 0)),
                  pl.BlockSpec((1, 1, D), lambda b, i: (b, 0, 0)),
                  pl.BlockSpec((1, 1, D), lambda b, i: (b, 0, 0)),
                  pl.BlockSpec((1, D), lambda b, i: (0, 0)),
                  pl.BlockSpec((D, NW), lambda b, i: (0, 0)),
                  pl.BlockSpec((tm, QK_COLS), lambda b, i: (i, 0)),
                  pl.BlockSpec((tm, QK_COLS), lambda b, i: (i, 0))],
        out_specs=[pl.BlockSpec((1, tm, n), lambda b, i: (b, i, 0)) for n in outs],
        out_shape=[jax.ShapeDtypeStruct((B, L, n), F32) for n in outs],
        compiler_params=_cparams("parallel", "parallel"),
        name="in_proj",
    )(h, shift, scale, g.reshape(1, D), w_ext, cos_t, sin_t)


def rope_tables(L, rotary):
    n_heads = ATT_HEADS + ATT_KV_HEADS
    if not rotary:
        return jnp.ones((L, QK_COLS), F32), jnp.zeros((L, QK_COLS), F32)
    pos = jnp.arange(L)
    row = (pos // GRID_W).astype(F32)
    col = (pos % GRID_W).astype(F32)
    half = HEAD_DIM // 2
    inv_freq = ROPE_BASE ** (-jnp.arange(0, half, 2, dtype=F32) / half)
    ang_r = row[:, None] * inv_freq[None]
    ang_c = col[:, None] * inv_freq[None]
    cos_h = jnp.concatenate([jnp.cos(ang_r)] * 2 + [jnp.cos(ang_c)] * 2, axis=-1)
    sin_h = jnp.concatenate([jnp.sin(ang_r)] * 2 + [jnp.sin(ang_c)] * 2, axis=-1)
    return jnp.tile(cos_h, (1, n_heads)), jnp.tile(sin_h, (1, n_heads))


def extend_w_in(w_in):
    c1 = HY_COLS + CONF_COLS
    wqk = w_in[:, c1:c1 + QK_COLS]
    D = w_in.shape[0]
    w4 = wqk.reshape(D, QK_COLS // 32, 2, 16)
    wrot = jnp.stack([-w4[:, :, 1], w4[:, :, 0]], axis=2).reshape(D, QK_COLS)
    return jnp.concatenate([w_in, wrot], axis=1).astype(BF16)


def _softmax_parts(q, k_list, extra_logit):
    scale = HEAD_DIM ** -0.5
    s_list = []
    for k, mask in k_list:
        s = lax.dot_general(q, k, (((1,), (1,)), ((), ())), preferred_element_type=F32) * scale
        if mask is not None:
            s = jnp.where(mask, s, NEG_INF)
        s_list.append(s)
    m = extra_logit
    for s in s_list:
        m = jnp.maximum(m, jnp.max(s, axis=-1, keepdims=True))
    p_list = [jnp.exp(s - m) for s in s_list]
    denom = jnp.exp(extra_logit - m)
    for p in p_list:
        denom = denom + jnp.sum(p, axis=-1, keepdims=True)
    return p_list, 1.0 / denom


ATT_Q_BLOCKS = 4


def _win_attn_kernel(sink_ref, q_ref, kp_ref, kc_ref, kn_ref, vp_ref, vc_ref, vn_ref,
                     kx_ref, vx_ref, o_ref, *, seq_len):
    i = pl.program_id(1)
    blk = ATT_BLOCK
    qb = q_ref.shape[1] // blk
    scale = HEAD_DIM ** -0.5
    g = ATT_HEADS // ATT_KV_HEADS
    kw = jnp.concatenate([kp_ref[0], kc_ref[0], kn_ref[0]], axis=0)
    vw = jnp.concatenate([vp_ref[0], vc_ref[0], vn_ref[0]], axis=0).astype(BF16)
    kwt = kw.T.astype(BF16)
    kxt = kx_ref[0].T.astype(BF16)
    vx = vx_ref[0].astype(BF16)
    row = lax.broadcasted_iota(jnp.int32, (g * blk, 3 * blk), 0) % blk
    col = lax.broadcasted_iota(jnp.int32, (g * blk, 3 * blk), 1)
    band_bias = jnp.where(jnp.abs(col - blk - row) <= WINDOW, 0.0, NEG_INF)
    col1 = lax.broadcasted_iota(jnp.int32, (1, 3 * blk), 1)
    for j in range(qb):
        q_blk = i * qb + j
        k_pos = (q_blk - 1) * blk + col1
        edge_bias = jnp.where(k_pos >= 0, jnp.where(k_pos < seq_len, 0.0, NEG_INF), NEG_INF)
        bias = band_bias + edge_bias
        outs = []
        for kv in range(ATT_KV_HEADS):
            ksl = slice(kv * HEAD_DIM, (kv + 1) * HEAD_DIM)
            heads = range(kv * g, (kv + 1) * g)
            qs = (jnp.concatenate([q_ref[0, j * blk:(j + 1) * blk, h * HEAD_DIM:(h + 1) * HEAD_DIM]
                                   for h in heads], axis=0) * scale).astype(BF16)
            sink = jnp.concatenate([jnp.full((blk, 1), sink_ref[h], F32) for h in heads], axis=0)
            s_win = jnp.dot(qs, kwt[ksl, j * blk:(j + 3) * blk], preferred_element_type=F32) + bias
            s_ctx = jnp.dot(qs, kxt[ksl, :], preferred_element_type=F32)
            m = jnp.maximum(jnp.maximum(jnp.max(s_win, axis=-1, keepdims=True),
                                        jnp.max(s_ctx, axis=-1, keepdims=True)), sink)
            p_win = jnp.exp(s_win - m)
            p_ctx = jnp.exp(s_ctx - m)
            denom = (jnp.exp(sink - m) + jnp.sum(p_win, axis=-1, keepdims=True)
                     + jnp.sum(p_ctx, axis=-1, keepdims=True))
            o = (jnp.dot(p_win.astype(BF16), vw[j * blk:(j + 3) * blk, ksl], preferred_element_type=F32)
                 + jnp.dot(p_ctx.astype(BF16), vx[:, ksl], preferred_element_type=F32)) * (1.0 / denom)
            outs.extend([o[k * blk:(k + 1) * blk] for k in range(g)])
        o_ref[0, j * blk:(j + 1) * blk, :] = jnp.concatenate(outs, axis=-1)


def window_attention(at_x, at_c, sink):
    B, S, _ = at_x.shape
    C = at_c.shape[1]
    blk = ATT_BLOCK
    qb = ATT_Q_BLOCKS
    nb = S // blk
    kcol = QK_COLS // 128 - 1
    vcol = kcol + 1

    def edge_spec(col, off):
        return pl.BlockSpec((1, blk, 128), lambda b, i, s: (b, jnp.clip(i * qb + off, 0, nb - 1), col))

    def mid_spec(col):
        return pl.BlockSpec((1, qb * blk, 128), lambda b, i, s: (b, i, col))

    grid_spec = pltpu.PrefetchScalarGridSpec(
        num_scalar_prefetch=1,
        grid=(B, nb // qb),
        in_specs=[pl.BlockSpec((1, qb * blk, ATT_HEADS * HEAD_DIM), lambda b, i, s: (b, i, 0)),
                  edge_spec(kcol, -1), mid_spec(kcol), edge_spec(kcol, qb),
                  edge_spec(vcol, -1), mid_spec(vcol), edge_spec(vcol, qb),
                  pl.BlockSpec((1, C, 128), lambda b, i, s: (b, 0, kcol)),
                  pl.BlockSpec((1, C, 128), lambda b, i, s: (b, 0, vcol))],
        out_specs=pl.BlockSpec((1, qb * blk, ATT_HEADS * HEAD_DIM), lambda b, i, s: (b, i, 0)),
    )
    return pl.pallas_call(
        functools.partial(_win_attn_kernel, seq_len=S),
        grid_spec=grid_spec,
        out_shape=jax.ShapeDtypeStruct((B, S, ATT_HEADS * HEAD_DIM), F32),
        compiler_params=_cparams("parallel", "parallel"),
        name="window_attention",
    )(sink.astype(F32), at_x, at_x, at_x, at_x, at_x, at_x, at_x, at_c, at_c)


def _ctx_attn_kernel(sink_ref, q_ref, kx_ref, vx_ref, o_ref):
    q = q_ref[0].astype(BF16)
    kx = kx_ref[0].astype(BF16)
    vx = vx_ref[0].astype(BF16)
    g = ATT_HEADS // ATT_KV_HEADS
    outs = []
    for h in range(ATT_HEADS):
        kv = h // g
        qs = q[:, h * HEAD_DIM:(h + 1) * HEAD_DIM]
        ksl = slice(kv * HEAD_DIM, (kv + 1) * HEAD_DIM)
        (p_ctx,), inv = _softmax_parts(qs, [(kx[:, ksl], None)], sink_ref[h])
        outs.append(jnp.dot(p_ctx.astype(BF16), vx[:, ksl], preferred_element_type=F32) * inv)
    o_ref[0] = jnp.concatenate(outs, axis=-1)


def context_attention(at_c, sink):
    B, C, _ = at_c.shape
    kcol = QK_COLS // 128 - 1
    grid_spec = pltpu.PrefetchScalarGridSpec(
        num_scalar_prefetch=1,
        grid=(B,),
        in_specs=[pl.BlockSpec((1, C, ATT_HEADS * HEAD_DIM), lambda b, s: (b, 0, 0)),
                  pl.BlockSpec((1, C, 128), lambda b, s: (b, 0, kcol)),
                  pl.BlockSpec((1, C, 128), lambda b, s: (b, 0, kcol + 1))],
        out_specs=pl.BlockSpec((1, C, ATT_HEADS * HEAD_DIM), lambda b, s: (b, 0, 0)),
    )
    return pl.pallas_call(
        _ctx_attn_kernel,
        grid_spec=grid_spec,
        out_shape=jax.ShapeDtypeStruct((B, C, ATT_HEADS * HEAD_DIM), F32),
        compiler_params=_cparams("parallel"),
        name="context_attention",
    )(sink.astype(F32), at_c, at_c, at_c)


def _out_proj_kernel(y0_ref, y1_ref, y2_ref, y3_ref, h_ref, g1_ref, gng_ref, w_ref,
                     n2g_ref, sh_ref, sc_ref, wr_ref, br_ref, ho_ref, lg_ref):
    parts = []
    for k, y_ref in enumerate((y0_ref, y1_ref, y2_ref, y3_ref)):
        y = y_ref[0]
        ms = jnp.mean(y * y, axis=-1, keepdims=True)
        yn = y * lax.rsqrt(ms + EPS) * gng_ref[:, k * D_GROUP:(k + 1) * D_GROUP]
        parts.append(yn.astype(BF16))
    yn = jnp.concatenate(parts, axis=-1)
    proj = jnp.dot(yn, w_ref[...], preferred_element_type=F32)
    h = h_ref[0] + g1_ref[0] * proj
    ho_ref[0] = h
    ms = jnp.mean(h * h, axis=-1, keepdims=True)
    n = h * lax.rsqrt(ms + EPS) * n2g_ref[...]
    n = n * (1.0 + sc_ref[0]) + sh_ref[0]
    lg_ref[0] = jnp.dot(n.astype(BF16), wr_ref[...], preferred_element_type=F32) + br_ref[...]


def out_proj(ys, h, g1, gng, w_out, n2g, sh2, sc2, w_router, b_router, tm):
    B, L, D = h.shape
    row3 = lambda n: pl.BlockSpec((1, tm, n), lambda b, i: (b, i, 0))
    mod = pl.BlockSpec((1, 1, D), lambda b, i: (b, 0, 0))
    full = lambda r, c: pl.BlockSpec((r, c), lambda b, i: (0, 0))
    return pl.pallas_call(
        _out_proj_kernel,
        grid=(B, L // tm),
        in_specs=[row3(D_GROUP)] * 4 + [row3(D), mod, full(1, D), full(D, D), full(1, D), mod, mod,
                                        full(D, ROUTER_COLS), full(1, ROUTER_COLS)],
        out_specs=[row3(D), row3(ROUTER_COLS)],
        out_shape=[jax.ShapeDtypeStruct((B, L, D), F32), jax.ShapeDtypeStruct((B, L, ROUTER_COLS), F32)],
        compiler_params=_cparams("parallel", "parallel"),
        name="out_proj",
    )(*ys, h, g1, gng.reshape(1, D), w_out, n2g.reshape(1, D), sh2, sc2, w_router, b_router)


N_PAIRS = EXP_PER_GROUP * (EXP_PER_GROUP - 1) // 2
N_CLASSES = N_GROUPS * N_PAIRS
ROUTE_TOKENS = 512
INFO_CLASS, INFO_RANK, INFO_WA, INFO_WB = 0, 1, 2, 3


SUBLANES = 8


def _route_kernel(lg_ref, info_ref, cnt_ref, ids_ref, run):
    i = pl.program_id(0)
    tb = lg_ref.shape[0]

    @pl.when(i == 0)
    def _():
        run[...] = jnp.zeros_like(run)

    lg = lg_ref[...]
    li = lax.broadcasted_iota(jnp.int32, lg.shape, 1)
    big = jnp.int32(ROUTER_COLS)

    def first_argmax(vals):
        m = jnp.max(vals, axis=-1, keepdims=True)
        return m, jnp.min(jnp.where(vals == m, li, big), axis=-1, keepdims=True)

    gl = jnp.where(li < N_GROUPS, lg, NEG_INF)
    gmax, g_idx = first_argmax(gl)
    g_prob = 1.0 / jnp.sum(jnp.exp(gl - gmax), axis=-1, keepdims=True)
    lo = N_GROUPS + EXP_PER_GROUP * g_idx
    el = jnp.where((li >= lo) & (li < lo + EXP_PER_GROUP), lg, NEG_INF)
    m1, i1 = first_argmax(el)
    m2, i2 = first_argmax(jnp.where(li == i1, NEG_INF, el))
    e2 = jnp.exp(m2 - m1)
    w1 = g_prob / (1.0 + e2)
    w2 = g_prob * e2 / (1.0 + e2)
    j1 = i1 - lo
    j2 = i2 - lo
    a = jnp.minimum(j1, j2)
    b = jnp.maximum(j1, j2)
    cls = g_idx * N_PAIRS + ((a * (2 * EXP_PER_GROUP - 1 - a)) >> 1) + (b - a - 1)
    w_a = jnp.where(j1 < j2, w1, w2)
    w_b = jnp.where(j1 < j2, w2, w1)

    hit = li == cls
    onehot = jnp.where(hit, 1.0, 0.0)
    r_i = lax.broadcasted_iota(jnp.int32, (tb, tb), 0)
    c_i = lax.broadcasted_iota(jnp.int32, (tb, tb), 1)
    below = jnp.where(c_i < r_i, 1.0, 0.0).astype(BF16)
    before = jnp.dot(below, onehot.astype(BF16), preferred_element_type=F32)
    rank = jnp.sum(jnp.where(hit, before + run[...], 0.0), axis=-1, keepdims=True)
    run[...] = run[...] + jnp.sum(onehot, axis=0, keepdims=True)
    cnt_ref[...] = run[...]
    info = jnp.where(li == INFO_CLASS, cls.astype(F32), 0.0)
    info = jnp.where(li == INFO_RANK, rank, info)
    info = jnp.where(li == INFO_WA, w_a, info)
    info = jnp.where(li == INFO_WB, w_b, info)
    info_ref[...] = info
    ids_ref[0] = info.T[:SUBLANES].astype(jnp.int32)


def route_tokens(logits):
    T = logits.shape[0]
    tb = ROUTE_TOKENS
    return pl.pallas_call(
        _route_kernel,
        grid=(T // tb,),
        in_specs=[pl.BlockSpec((tb, ROUTER_COLS), lambda i: (i, 0))],
        out_specs=[pl.BlockSpec((tb, ROUTER_COLS), lambda i: (i, 0)),
                   pl.BlockSpec((1, ROUTER_COLS), lambda i: (0, 0)),
                   pl.BlockSpec((1, SUBLANES, tb), lambda i: (i, 0, 0))],
        out_shape=[jax.ShapeDtypeStruct((T, ROUTER_COLS), F32), jax.ShapeDtypeStruct((1, ROUTER_COLS), F32),
                   jax.ShapeDtypeStruct((T // tb, SUBLANES, tb), jnp.int32)],
        scratch_shapes=[pltpu.VMEM((1, ROUTER_COLS), F32)],
        compiler_params=_cparams("arbitrary"),
        name="moe_route",
    )(logits)


def _pair_tables():
    a_tab, b_tab = [], []
    for g in range(N_GROUPS):
        for a in range(EXP_PER_GROUP):
            for b in range(a + 1, EXP_PER_GROUP):
                a_tab.append(g * EXP_PER_GROUP + a)
                b_tab.append(g * EXP_PER_GROUP + b)
    return jnp.array(a_tab, jnp.int32), jnp.array(b_tab, jnp.int32)


def _slot_kernel(ids_ref, start_ref, dest_ref):
    cls = ids_ref[0, INFO_CLASS:INFO_CLASS + 1, :]
    rank = ids_ref[0, INFO_RANK:INFO_RANK + 1, :]
    ci = lax.broadcasted_iota(jnp.int32, (ROUTER_COLS, cls.shape[1]), 0)
    start = jnp.sum(jnp.where(ci == cls, start_ref[...], 0), axis=0, keepdims=True)
    dest_ref[0] = jnp.broadcast_to(start + rank, dest_ref.shape[1:])


def slot_plan(ids, counts, n_blocks):
    nt, _, tb = ids.shape
    cnt = counts[0, :N_CLASSES].astype(jnp.int32)
    padded = (cnt + MOE_BLOCK - 1) // MOE_BLOCK * MOE_BLOCK
    pad_end = jnp.cumsum(padded)
    class_start = jnp.zeros((ROUTER_COLS, 1), jnp.int32).at[:N_CLASSES, 0].set(pad_end - padded)
    dest = pl.pallas_call(
        _slot_kernel,
        grid=(nt,),
        in_specs=[pl.BlockSpec((1, SUBLANES, tb), lambda i: (i, 0, 0)),
                  pl.BlockSpec((ROUTER_COLS, 1), lambda i: (0, 0))],
        out_specs=pl.BlockSpec((1, SUBLANES, tb), lambda i: (i, 0, 0)),
        out_shape=jax.ShapeDtypeStruct((nt, SUBLANES, tb), jnp.int32),
        compiler_params=_cparams("parallel"),
        name="moe_slots",
    )(ids, class_start)[:, 0, :].reshape(nt * tb)
    n_used = (pad_end[-1] // MOE_BLOCK).astype(jnp.int32).reshape(1)
    blk_cls = jnp.minimum(jnp.searchsorted(pad_end, jnp.arange(n_blocks) * MOE_BLOCK, side="right"),
                          N_CLASSES - 1)
    a_tab, b_tab = _pair_tables()
    return dest, a_tab[blk_cls], b_tab[blk_cls], n_used


DISPATCH_TOKENS = 256


def _wait_rows(buf, sem):
    pltpu.make_async_copy(buf, buf, sem).wait()


DMA_UNROLL = 8
TOKEN_TILE_ROWS = 8


def _store_token_tiles(tiles_ref, offset, pitch, x):
    n = x.shape[0]
    for j in range(x.shape[1] // LANES):
        tiles_ref[pl.ds(offset + j, n, stride=pitch), :] = x[:, j * LANES:(j + 1) * LANES]


def _load_token_tiles(tiles_ref, offset, pitch, n, width):
    return jnp.concatenate([tiles_ref[pl.ds(offset + j, n, stride=pitch), :] for j in range(width // LANES)],
                           axis=-1)


def _dispatch_kernel(dest_ref, hx_ref, hc_ref, g_ref, shx_ref, scx_ref, shc_ref, scc_ref, zeros_hbm,
                     xs_hbm, rows, sems, *, n_latent_blocks):
    del zeros_hbm
    i = pl.program_id(0)
    n = pl.num_programs(0)
    slot = i % 2
    tb = hx_ref.shape[0]

    @pl.when(i >= 2)
    def _():
        _wait_rows(rows.at[slot], sems.at[slot])

    def normed(h_ref, sh_ref, sc_ref):
        h = h_ref[...]
        ms = jnp.mean(h * h, axis=-1, keepdims=True)
        return h * lax.rsqrt(ms + EPS) * g_ref[...] * (1.0 + sc_ref[0]) + sh_ref[0]

    @pl.when(i < n_latent_blocks)
    def _():
        _store_token_tiles(rows.at[slot], 0, TOKEN_TILE_ROWS, normed(hx_ref, shx_ref, scx_ref))

    @pl.when(i >= n_latent_blocks)
    def _():
        _store_token_tiles(rows.at[slot], 0, TOKEN_TILE_ROWS, normed(hc_ref, shc_ref, scc_ref))

    def body(g, carry):
        for u in range(DMA_UNROLL):
            r = g * DMA_UNROLL + u
            dst = pl.multiple_of(dest_ref[0, 0, r] * TOKEN_TILE_ROWS, TOKEN_TILE_ROWS)
            pltpu.make_async_copy(rows.at[slot, pl.ds(r * TOKEN_TILE_ROWS, TOKEN_TILE_ROWS)],
                                  xs_hbm.at[pl.ds(dst, TOKEN_TILE_ROWS)], sems.at[slot]).start(priority=u % 2)
        return carry
    lax.fori_loop(0, tb // DMA_UNROLL, body, 0)

    @pl.when(i == n - 1)
    def _():
        _wait_rows(rows.at[slot], sems.at[slot])

        @pl.when(n >= 2)
        def _():
            _wait_rows(rows.at[1 - slot], sems.at[1 - slot])


def moe_dispatch(h_x, h_c, dest, n_blocks, n2g, mod_x, mod_c, tokens_per_batch):
    Tx, D = h_x.shape
    tb = DISPATCH_TOKENS
    nxb = Tx // tb
    if h_c is None:
        h_c, mod_c, ncb = h_x, mod_x, 0
    else:
        ncb = h_c.shape[0] // tb
    per_b = tokens_per_batch // tb
    P = n_blocks * MOE_BLOCK
    tile_rows = D // LANES
    assert tile_rows == TOKEN_TILE_ROWS
    xi = lambda i: jnp.minimum(i, nxb - 1)
    ci = lambda i: jnp.maximum(i - nxb, 0)
    modx = pl.BlockSpec((1, 1, D), lambda i: (xi(i) // per_b, 0, 0))
    modc = pl.BlockSpec((1, 1, D), lambda i: (0, 0, 0))
    return pl.pallas_call(
        functools.partial(_dispatch_kernel, n_latent_blocks=nxb),
        grid=(nxb + ncb,),
        in_specs=[pl.BlockSpec((1, 1, tb), lambda i: (i, 0, 0), memory_space=pltpu.SMEM),
                  pl.BlockSpec((tb, D), lambda i: (xi(i), 0)),
                  pl.BlockSpec((tb, D), lambda i: (ci(i), 0)),
                  pl.BlockSpec((1, D), lambda i: (0, 0)),
                  modx, modx, modc, modc,
                  pl.BlockSpec(memory_space=pl.ANY)],
        out_specs=pl.BlockSpec(memory_space=pl.ANY),
        out_shape=jax.ShapeDtypeStruct((P * tile_rows, LANES), F32),
        scratch_shapes=[pltpu.VMEM((2, tb * tile_rows, LANES), F32), pltpu.SemaphoreType.DMA((2,))],
        input_output_aliases={8: 0},
        compiler_params=_cparams("arbitrary"),
        name="moe_dispatch",
    )(dest.reshape(-1, 1, tb), h_x, h_c, n2g.reshape(1, D), mod_x[0], mod_x[1], mod_c[0], mod_c[1],
      jnp.zeros((P * tile_rows, LANES), F32))


def _expert_pair_kernel(ea_ref, eb_ref, nused_ref, xs_ref, wga_ref, wua_ref, wda_ref, wgb_ref, wub_ref, wdb_ref,
                        o_ref):
    del ea_ref, eb_ref
    i = pl.program_id(0)
    D = wga_ref.shape[1]

    @pl.when(i < nused_ref[0])
    def _():
        xb = _load_token_tiles(xs_ref, 0, TOKEN_TILE_ROWS, MOE_BLOCK, D).astype(BF16)
        halves = []
        for wg_ref, wu_ref, wd_ref in ((wga_ref, wua_ref, wda_ref), (wgb_ref, wub_ref, wdb_ref)):
            gate = jnp.dot(xb, wg_ref[0], preferred_element_type=F32)
            up = jnp.dot(xb, wu_ref[0], preferred_element_type=F32)
            hid = (gate * jax.nn.sigmoid(gate) * up).astype(BF16)
            out = jnp.dot(hid, wd_ref[0], preferred_element_type=F32)
            halves.append(lax.bitcast_convert_type(out.astype(BF16).astype(F32), jnp.uint32))
        _store_token_tiles(o_ref, 0, TOKEN_TILE_ROWS, halves[0] | (halves[1] >> 16))

    @pl.when(i >= nused_ref[0])
    def _():
        o_ref[...] = jnp.zeros_like(o_ref)


def _unpack_pair(words):
    hi = lax.bitcast_convert_type(words & jnp.uint32(0xFFFF0000), F32)
    lo = lax.bitcast_convert_type(words << 16, F32)
    return hi, lo


def expert_pairs(xs, blk_a, blk_b, n_used, w_gate, w_up, w_down):
    D, DE = w_gate.shape[1:]
    P = xs.shape[0] // TOKEN_TILE_ROWS
    n_blocks = P // MOE_BLOCK
    wspec = lambda shape, which: pl.BlockSpec(shape, lambda i, ea, eb, nu: ((ea, eb)[which][i], 0, 0))
    grid_spec = pltpu.PrefetchScalarGridSpec(
        num_scalar_prefetch=3,
        grid=(n_blocks,),
        in_specs=[pl.BlockSpec((MOE_BLOCK * TOKEN_TILE_ROWS, LANES), lambda i, ea, eb, nu: (i, 0)),
                  wspec((1, D, DE), 0), wspec((1, D, DE), 0), wspec((1, DE, D), 0),
                  wspec((1, D, DE), 1), wspec((1, D, DE), 1), wspec((1, DE, D), 1)],
        out_specs=pl.BlockSpec((MOE_BLOCK * TOKEN_TILE_ROWS, LANES), lambda i, ea, eb, nu: (i, 0)),
    )
    return pl.pallas_call(
        _expert_pair_kernel,
        grid_spec=grid_spec,
        out_shape=jax.ShapeDtypeStruct((P * TOKEN_TILE_ROWS, LANES), jnp.uint32),
        compiler_params=_cparams("arbitrary"),
        name="moe_experts",
    )(blk_a, blk_b, n_used, xs, w_gate, w_up, w_down, w_gate, w_up, w_down)


def _gather_pairs(idx_ref, src_hbm, buf, sem, n_tokens):
    def body(g, carry):
        for u in range(DMA_UNROLL):
            r = g * DMA_UNROLL + u
            src = pl.multiple_of(idx_ref[0, 0, r] * TOKEN_TILE_ROWS, TOKEN_TILE_ROWS)
            pltpu.make_async_copy(src_hbm.at[pl.ds(src, TOKEN_TILE_ROWS)],
                                  buf.at[pl.ds(r * TOKEN_TILE_ROWS, TOKEN_TILE_ROWS)], sem).start(priority=u % 2)
        return carry
    lax.fori_loop(0, n_tokens // DMA_UNROLL, body, 0)


def _collect_kernel(dest_ref, dest_next_ref, o_hbm, info_ref, h_ref, g2_ref, fg_ref, out_ref, obuf, sems, *,
                    final_norm):
    i = pl.program_id(0)
    n = pl.num_programs(0)
    slot = i % 2
    tb, D = h_ref.shape

    @pl.when(i == 0)
    def _():
        _gather_pairs(dest_ref, o_hbm, obuf.at[0], sems.at[0], tb)

    @pl.when(i + 1 < n)
    def _():
        _gather_pairs(dest_next_ref, o_hbm, obuf.at[1 - slot], sems.at[1 - slot], tb)

    _wait_rows(obuf.at[slot], sems.at[slot])
    e_a, e_b = _unpack_pair(_load_token_tiles(obuf.at[slot], 0, TOKEN_TILE_ROWS, tb, D))
    m = info_ref[:, INFO_WA:INFO_WA + 1] * e_a + info_ref[:, INFO_WB:INFO_WB + 1] * e_b
    h = h_ref[...] + g2_ref[0] * m
    if final_norm:
        ms = jnp.mean(h * h, axis=-1, keepdims=True)
        h = h * lax.rsqrt(ms + EPS) * fg_ref[...]
    out_ref[...] = h


def moe_collect(o_sorted, dest, info, block_offset, h_tokens, g2, tokens_per_batch, final_g, final_norm):
    T, D = h_tokens.shape
    tb = DISPATCH_TOKENS
    nt = T // tb
    per_b = tokens_per_batch // tb
    last = block_offset + nt - 1
    dest3 = dest.reshape(-1, 1, tb)
    return pl.pallas_call(
        functools.partial(_collect_kernel, final_norm=final_norm),
        grid=(nt,),
        in_specs=[pl.BlockSpec((1, 1, tb), lambda i: (block_offset + i, 0, 0), memory_space=pltpu.SMEM),
                  pl.BlockSpec((1, 1, tb), lambda i: (jnp.minimum(block_offset + i + 1, last), 0, 0),
                               memory_space=pltpu.SMEM),
                  pl.BlockSpec(memory_space=pl.ANY),
                  pl.BlockSpec((tb, ROUTER_COLS), lambda i: (block_offset + i, 0)),
                  pl.BlockSpec((tb, D), lambda i: (i, 0)),
                  pl.BlockSpec((1, 1, D), lambda i: (i // per_b, 0, 0)),
                  pl.BlockSpec((1, D), lambda i: (0, 0))],
        out_specs=pl.BlockSpec((tb, D), lambda i: (i, 0)),
        out_shape=jax.ShapeDtypeStruct((T, D), F32),
        scratch_shapes=[pltpu.VMEM((2, tb * TOKEN_TILE_ROWS, LANES), jnp.uint32), pltpu.SemaphoreType.DMA((2,))],
        compiler_params=_cparams("arbitrary"),
        name="moe_collect",
    )(dest3, dest3, o_sorted, info, h_tokens, g2, final_g.reshape(1, D))


CONV_MARGIN = 16


def _time_chunk(L):
    return min(L, 256)


LANES = 128


def _zero_margins(pad_ref, L):
    zeros = jnp.zeros((CONV_MARGIN, LANES), F32)
    for s in range(pad_ref.shape[0]):
        pad_ref[s, pl.ds(0, CONV_MARGIN), :] = zeros
        pad_ref[s, pl.ds(CONV_MARGIN + L, CONV_MARGIN), :] = zeros


def _dw_conv_slab(pad_ref, s, base, T, w_ref, b_ref, col, taps, pad_left):
    acc = jnp.broadcast_to(b_ref[:, col:col + LANES], (T, LANES))
    for k in range(taps):
        acc = acc + w_ref[k:k + 1, col:col + LANES] * pad_ref[s, pl.ds(base + (CONV_MARGIN - pad_left + k), T), :]
    return acc


def _conformer_kernel(u_ref, w_ref, b_ref, g_ref, beta_ref, o_ref, ypad):
    L = o_ref.shape[1]
    T = _time_chunk(L)
    C = D_GROUP
    n_slabs = C // LANES
    pad = (CONF_KERNEL - 1) // 2
    _zero_margins(ypad, L)

    def glu(j, carry):
        base = pl.multiple_of(j * T, T)
        for s in range(n_slabs):
            a = u_ref[0, pl.ds(base, T), s * LANES:(s + 1) * LANES]
            gate = u_ref[0, pl.ds(base, T), C + s * LANES:C + (s + 1) * LANES]
            ypad[s, pl.ds(CONV_MARGIN + base, T), :] = a * jax.nn.sigmoid(gate)
        return carry
    lax.fori_loop(0, L // T, glu, 0)

    def conv(j, carry):
        base = pl.multiple_of(j * T, T)
        acc = jnp.concatenate([_dw_conv_slab(ypad, s, base, T, w_ref, b_ref, s * LANES, CONF_KERNEL, pad)
                               for s in range(n_slabs)], axis=-1)
        mu = jnp.mean(acc, axis=-1, keepdims=True)
        cen = acc - mu
        var = jnp.mean(cen * cen, axis=-1, keepdims=True)
        y = cen * lax.rsqrt(var + EPS) * g_ref[...] + beta_ref[...]
        o_ref[0, pl.ds(base, T), :] = y * jax.nn.sigmoid(y)
        return carry
    lax.fori_loop(0, L // T, conv, 0)


def conformer_conv(u, w, b, ln_g, ln_b):
    B, L, _ = u.shape
    C = D_GROUP
    vec = pl.BlockSpec((1, C), lambda i: (0, 0))
    return pl.pallas_call(
        _conformer_kernel,
        grid=(B,),
        in_specs=[pl.BlockSpec((1, L, 2 * C), lambda i: (i, 0, 0)),
                  pl.BlockSpec((CONF_KERNEL, C), lambda i: (0, 0)), vec, vec, vec],
        out_specs=pl.BlockSpec((1, L, C), lambda i: (i, 0, 0)),
        out_shape=jax.ShapeDtypeStruct((B, L, C), F32),
        scratch_shapes=[pltpu.VMEM((C // LANES, L + 2 * CONV_MARGIN, LANES), F32)],
        compiler_params=_cparams("parallel"),
        name="conformer_conv",
    )(u, w, b.reshape(1, C), ln_g.reshape(1, C), ln_b.reshape(1, C))


def _gelu_tanh(x):
    return 0.5 * x * (1.0 + jnp.tanh(math.sqrt(2.0 / math.pi) * (x + 0.044715 * (x * x * x))))


def _lru_kernel(uc_ref, ux_ref, cw_ref, cb_ref, wcat_ref, bcat_ref, lam_ref, *rest, need_ctx):
    if need_ctx:
        oc_ref, ox_ref, cpad, xpad, a_s, b_s, yx, yc = rest
    else:
        ox_ref, cpad, xpad, a_s, b_s, yx = rest
        oc_ref = yc = None
    C = D_GROUP
    n_slabs = C // LANES
    Lc = uc_ref.shape[1]
    Lx = ux_ref.shape[1]
    pad_l = (LRU_CONV - 1) // 2

    def fill(pad_ref, u_ref, L):
        T = _time_chunk(L)
        _zero_margins(pad_ref, L)

        def body(j, carry):
            base = pl.multiple_of(j * T, T)
            for s in range(n_slabs):
                pad_ref[s, pl.ds(CONV_MARGIN + base, T), :] = u_ref[0, pl.ds(base, T),
                                                                    C + s * LANES:C + (s + 1) * LANES]
            return carry
        lax.fori_loop(0, L // T, body, 0)

    fill(cpad, uc_ref, Lc)
    fill(xpad, ux_ref, Lx)

    def coeffs(pad_ref, base, T, d):
        x = jnp.concatenate([_dw_conv_slab(pad_ref, s, base, T, cw_ref, cb_ref, s * LANES, LRU_CONV, pad_l)
                             for s in range(n_slabs)], axis=-1)
        t = jnp.tanh(jnp.dot(x.astype(BF16), wcat_ref[:, 2 * d * C:2 * (d + 1) * C],
                             preferred_element_type=F32) + bcat_ref[:, 2 * d * C:2 * (d + 1) * C])
        i = 0.5 * t[:, C:] + 0.5
        z = -lam_ref[d:d + 1, :]
        softplus = jnp.maximum(z, 0.0) + jnp.log(1.0 + jnp.exp(-jnp.abs(z)))
        half_rate = (-0.5 * LRU_C) * softplus
        a = jnp.exp(half_rate * t[:, :C] + half_rate)
        b = jnp.sqrt(1.0 - a * a) * (i * x)
        for s in range(n_slabs):
            a_s[d * n_slabs + s, pl.ds(0, T), :] = a[:, s * LANES:(s + 1) * LANES]
            b_s[d * n_slabs + s, pl.ds(0, T), :] = b[:, s * LANES:(s + 1) * LANES]

    def run(pad_ref, L, h, y_ref):
        T = _time_chunk(L)
        n = L // T

        def chunk(j, h):
            base_f = pl.multiple_of(j * T, T)
            base_b = pl.multiple_of((n - 1 - j) * T, T)
            coeffs(pad_ref, base_f, T, 0)
            coeffs(pad_ref, base_b, T, 1)

            def step(t, h):
                new = []
                for d, (base, row) in enumerate(((base_f, t), (base_b, T - 1 - t))):
                    for s in range(n_slabs):
                        k = d * n_slabs + s
                        hs = a_s[k, pl.ds(row, 1), :] * h[k] + b_s[k, pl.ds(row, 1), :]
                        if y_ref is not None:
                            y_ref[k, pl.ds(base + row, 1), :] = hs
                        new.append(hs)
                return tuple(new)
            return lax.fori_loop(0, T, step, h, unroll=8)
        return lax.fori_loop(0, n, chunk, h)

    h = tuple(jnp.zeros((1, LANES), F32) for _ in range(2 * n_slabs))
    h = run(cpad, Lc, h, yc)
    run(xpad, Lx, h, yx)

    def finish(u_ref, y_ref, o_ref, L):
        T = _time_chunk(L)

        def body(j, carry):
            base = pl.multiple_of(j * T, T)
            y = jnp.concatenate([y_ref[s, pl.ds(base, T), :] + y_ref[n_slabs + s, pl.ds(base, T), :]
                                 for s in range(n_slabs)], axis=-1)
            o_ref[0, pl.ds(base, T), :] = _gelu_tanh(u_ref[0, pl.ds(base, T), :C]) * y
            return carry
        lax.fori_loop(0, L // T, body, 0)

    finish(ux_ref, yx, ox_ref, Lx)
    if need_ctx:
        finish(uc_ref, yc, oc_ref, Lc)


def _block_diag(w):
    H, n, _ = w.shape
    eye = jnp.eye(H, dtype=w.dtype)
    return (eye[:, None, :, None] * w[:, :, None, :]).reshape(H * n, H * n)


def rglru_mixer(uc, ux, lp, need_ctx):
    B, Lc, _ = uc.shape
    Lx = ux.shape[1]
    C = D_GROUP
    wcat = (0.5 * jnp.concatenate([_block_diag(lp["lru_wa"][0]), _block_diag(lp["lru_wx"][0]),
                                   _block_diag(lp["lru_wa"][1]), _block_diag(lp["lru_wx"][1])], axis=1)).astype(BF16)
    bcat = 0.5 * jnp.concatenate([lp["lru_ba"][0], lp["lru_bx"][0], lp["lru_ba"][1], lp["lru_bx"][1]]).reshape(1, 4 * C)
    full = lambda r, c: pl.BlockSpec((r, c), lambda i: (0, 0))
    seq = lambda L, n: pl.BlockSpec((1, L, n), lambda i: (i, 0, 0))
    out_specs = [seq(Lx, C)]
    out_shape = [jax.ShapeDtypeStruct((B, Lx, C), F32)]
    if need_ctx:
        out_specs = [seq(Lc, C)] + out_specs
        out_shape = [jax.ShapeDtypeStruct((B, Lc, C), F32)] + out_shape
    T = _time_chunk(Lx)
    slab = lambda rows, n=1: pltpu.VMEM((n * C // LANES, rows, LANES), F32)
    scratch = [slab(Lc + 2 * CONV_MARGIN), slab(Lx + 2 * CONV_MARGIN), slab(T, 2), slab(T, 2), slab(Lx, 2)]
    if need_ctx:
        scratch.append(slab(Lc, 2))
    res = pl.pallas_call(
        functools.partial(_lru_kernel, need_ctx=need_ctx),
        grid=(B,),
        in_specs=[seq(Lc, 2 * C), seq(Lx, 2 * C), full(LRU_CONV, C), full(1, C), full(C, 4 * C),
                  full(1, 4 * C), full(2, C)],
        out_specs=out_specs,
        out_shape=out_shape,
        scratch_shapes=scratch,
        compiler_params=_cparams("parallel"),
        name="rglru",
    )(uc, ux, lp["lru_conv_w"], lp["lru_conv_b"].reshape(1, C), wcat, bcat, lp["lru_lambda"])
    if need_ctx:
        return res[0], res[1]
    return None, res[0]


HY_SHORT = 3


def _short_conv(pad_ref, base, T, w_ref, b_ref, c0, c1):
    return jnp.concatenate([_dw_conv_slab(pad_ref, col // LANES, base, T, w_ref, b_ref, col, HY_SHORT, 1)
                            for col in range(c0, c1, LANES)], axis=-1)


def _fill_padded(pad_ref, u_ref, L, T):
    _zero_margins(pad_ref, L)

    def body(j, carry):
        base = pl.multiple_of(j * T, T)
        for s in range(pad_ref.shape[0]):
            pad_ref[s, pl.ds(CONV_MARGIN + base, T), :] = u_ref[0, pl.ds(base, T), s * LANES:(s + 1) * LANES]
        return carry
    lax.fori_loop(0, L // T, body, 0)


def _hyena_pre_kernel(u_ref, w_ref, b_ref, z_ref, upad):
    L = u_ref.shape[1]
    T = _time_chunk(L)
    C = D_GROUP
    _fill_padded(upad, u_ref, L, T)

    def body(j, carry):
        base = pl.multiple_of(j * T, T)
        x1 = _short_conv(upad, base, T, w_ref, b_ref, C, 2 * C)
        v = _short_conv(upad, base, T, w_ref, b_ref, 2 * C, 3 * C)
        z_ref[pl.ds(base, T), :] = (x1 * v).astype(BF16)
        return carry
    lax.fori_loop(0, L // T, body, 0)


def _hyena_post_kernel(u_ref, y_ref, w_ref, b_ref, bias_ref, o_ref, upad):
    L = u_ref.shape[1]
    T = _time_chunk(L)
    C = D_GROUP
    _fill_padded(upad, u_ref, L, T)

    def body(j, carry):
        base = pl.multiple_of(j * T, T)
        x0 = _short_conv(upad, base, T, w_ref, b_ref, 0, C)
        x1 = _short_conv(upad, base, T, w_ref, b_ref, C, 2 * C)
        v = _short_conv(upad, base, T, w_ref, b_ref, 2 * C, 3 * C)
        o_ref[0, pl.ds(base, T), :] = x0 * (y_ref[pl.ds(base, T), :] + (x1 * v) * bias_ref[...])
        return carry
    lax.fori_loop(0, L // T, body, 0)


def _spectrum_kernel(f_ref, z_ref, ha_ref, hb_ref, hc_ref, y_ref):
    tf = ha_ref.shape[0]
    acc = jnp.dot(f_ref[...], z_ref[...], preferred_element_type=F32)
    zr = acc[:tf]
    zi = acc[tf:]
    y_ref[:tf, :] = (zr * ha_ref[...] - zi * hb_ref[...]).astype(BF16)
    y_ref[tf:, :] = (zr * hb_ref[...] + zi * hc_ref[...]).astype(BF16)


def _idft_kernel(f_ref, y_ref, o_ref):
    o_ref[...] = jnp.dot(f_ref[...], y_ref[...], preferred_element_type=F32)


def dft_tables(L):
    N = 2 * L
    tf = min(256, L)
    k = jnp.arange(L, dtype=jnp.int32)
    n = jnp.arange(L, dtype=jnp.int32)
    ang = (2.0 * math.pi / N) * ((k[:, None] * n[None, :]) % N).astype(F32)
    cos = jnp.cos(ang)
    sin = jnp.sin(ang)
    nyq = jnp.where(n % 2 == 0, 1.0, -1.0).astype(F32)
    f_re = cos
    f_im = (-sin).at[0].set(nyq)
    fwd = jnp.stack([f_re.reshape(L // tf, tf, L), f_im.reshape(L // tf, tf, L)], axis=1).reshape(N, L)
    ck = jnp.where(k == 0, 1.0, 2.0).astype(F32)[:, None] / N
    i_re = cos * ck
    i_im = (-sin * ck).at[0].set(nyq / N)
    inv = jnp.stack([i_re.reshape(L // tf, tf, L), i_im.reshape(L // tf, tf, L)], axis=1).reshape(N, L).T
    return fwd.astype(BF16), inv.astype(BF16)


def filter_spectrum(h_fwd, h_bwd):
    L, C = h_fwd.shape
    k = jnp.concatenate([h_fwd, jnp.zeros((1, C), F32), h_bwd[1:][::-1]], axis=0)
    hf = jnp.fft.rfft(k, axis=0)
    hr = jnp.real(hf)
    hi = jnp.imag(hf)
    a = hr[:L]
    b = hi[:L].at[0].set(0.0)
    c = hr[:L].at[0].set(hr[L])
    return a, b, c


def hyena_mixer(u, lp, tables):
    B, L, _ = u.shape
    C = D_GROUP
    N = 2 * L
    fwd, inv = tables
    tf = min(256, L)
    T = _time_chunk(L)
    w, bsh = lp["hy_short_w"], lp["hy_short_b"].reshape(1, 3 * C)
    z2 = pl.pallas_call(
        _hyena_pre_kernel,
        grid=(B,),
        in_specs=[pl.BlockSpec((1, L, 3 * C), lambda b: (b, 0, 0)),
                  pl.BlockSpec((HY_SHORT, 3 * C), lambda b: (0, 0)),
                  pl.BlockSpec((1, 3 * C), lambda b: (0, 0))],
        out_specs=pl.BlockSpec((L, C), lambda b: (0, b)),
        out_shape=jax.ShapeDtypeStruct((L, B * C), BF16),
        scratch_shapes=[pltpu.VMEM((3 * C // LANES, L + 2 * CONV_MARGIN, LANES), F32)],
        compiler_params=_cparams("parallel"),
        name="hyena_pre",
    )(u, w, bsh)

    h_fwd, h_bwd = _hyena_filters(L, lp)
    tn = 2 * C
    ha, hb, hc = [jnp.tile(t, (1, tn // C)) for t in filter_spectrum(h_fwd, h_bwd)]
    hspec = pl.BlockSpec((tf, tn), lambda i, j: (i, 0))
    y2 = pl.pallas_call(
        _spectrum_kernel,
        grid=(L // tf, B * C // tn),
        in_specs=[pl.BlockSpec((2 * tf, L), lambda i, j: (i, 0)),
                  pl.BlockSpec((L, tn), lambda i, j: (0, j)), hspec, hspec, hspec],
        out_specs=pl.BlockSpec((2 * tf, tn), lambda i, j: (i, j)),
        out_shape=jax.ShapeDtypeStruct((N, B * C), BF16),
        compiler_params=_cparams("parallel", "parallel"),
        name="hyena_spectrum",
    )(fwd, z2, ha, hb, hc)

    tl = min(256, L)
    yt = pl.pallas_call(
        _idft_kernel,
        grid=(L // tl, B * C // tn),
        in_specs=[pl.BlockSpec((tl, N), lambda i, j: (i, 0)),
                  pl.BlockSpec((N, tn), lambda i, j: (0, j))],
        out_specs=pl.BlockSpec((tl, tn), lambda i, j: (i, j)),
        out_shape=jax.ShapeDtypeStruct((L, B * C), F32),
        compiler_params=_cparams("parallel", "parallel"),
        name="hyena_idft",
    )(inv, y2)

    return pl.pallas_call(
        _hyena_post_kernel,
        grid=(B,),
        in_specs=[pl.BlockSpec((1, L, 3 * C), lambda b: (b, 0, 0)),
                  pl.BlockSpec((L, C), lambda b: (0, b)),
                  pl.BlockSpec((HY_SHORT, 3 * C), lambda b: (0, 0)),
                  pl.BlockSpec((1, 3 * C), lambda b: (0, 0)),
                  pl.BlockSpec((1, C), lambda b: (0, 0))],
        out_specs=pl.BlockSpec((1, L, C), lambda b: (b, 0, 0)),
        out_shape=jax.ShapeDtypeStruct((B, L, C), F32),
        scratch_shapes=[pltpu.VMEM((3 * C // LANES, L + 2 * CONV_MARGIN, LANES), F32)],
        compiler_params=_cparams("parallel"),
        name="hyena_post",
    )(u, yt, w, bsh, lp["hy_bias"].reshape(1, C))


FFT_N2 = 128
FFT_UNROLL = 8


class _FftPlan:
    def __init__(self, L):
        self.L = L
        self.N = 2 * L
        self.N1 = self.N // FFT_N2
        self.KH = self.N1 // 2 + 1
        self.KP = -(-self.KH // 8) * 8
        self.PA = 2 * self.KP + 4


def fft_tables(L):
    p = _FftPlan(L)
    N, N1, KH, KP = p.N, p.N1, p.KH, p.KP
    n2 = jnp.arange(FFT_N2, dtype=jnp.int32)
    k1 = jnp.arange(KP, dtype=jnp.int32)
    n1 = jnp.arange(N1, dtype=jnp.int32)
    n = FFT_N2 * n1[None, None, :] + n2[:, None, None]
    ang = (2.0 * math.pi / N) * ((k1[None, :, None] * n) % N).astype(F32)
    keep = (k1 < KH)[None, :, None]
    g_re = jnp.where(keep, jnp.cos(ang), 0.0)
    g_im = jnp.where(keep, -jnp.sin(ang), 0.0)
    ga_full = jnp.concatenate([g_re, g_im], axis=1)
    ck = jnp.where((k1 == 0) | (k1 == N1 // 2), 1.0, 2.0) / N
    ga_inv = jnp.swapaxes(ga_full[:, :, :N1 // 2] * jnp.tile(ck, 2)[None, :, None], 1, 2)
    kk = jnp.arange(FFT_N2, dtype=jnp.int32)
    ang2 = (2.0 * math.pi / FFT_N2) * ((kk[:, None] * kk[None, :]) % FFT_N2).astype(F32)
    fr, fi = jnp.cos(ang2), -jnp.sin(ang2)
    fb = jnp.block([[fr, -fi], [fi, fr]])
    fb_inv = jnp.block([[fr, fi], [-fi, fr]])
    return dict(ga_half=ga_full[:, :, :N1 // 2].astype(BF16), ga_full=ga_full.astype(BF16),
                ga_inv=ga_inv.astype(BF16), fb=fb.astype(BF16), fb_inv=fb_inv.astype(BF16))


def _fft_stage_a(x_ref, ga_ref, s_ref, plan, n1_count):
    n_slabs = x_ref.shape[0]

    def body(n2, carry):
        xs = jnp.concatenate([x_ref[s, pl.ds(n2, n1_count, stride=FFT_N2), :] for s in range(n_slabs)], axis=-1)
        a = jnp.dot(ga_ref[n2], xs.astype(BF16), preferred_element_type=F32)
        for s in range(n_slabs):
            s_ref[s, pl.ds(n2 * plan.PA, 2 * plan.KP), :] = a[:, s * LANES:(s + 1) * LANES]
        return carry
    lax.fori_loop(0, FFT_N2, body, 0, unroll=FFT_UNROLL)


def _fft_load_k1(s_ref, k1, plan):
    n_slabs = s_ref.shape[0]
    re = jnp.concatenate([s_ref[s, pl.ds(k1, FFT_N2, stride=plan.PA), :] for s in range(n_slabs)], axis=-1)
    im = jnp.concatenate([s_ref[s, pl.ds(plan.KP + k1, FFT_N2, stride=plan.PA), :] for s in range(n_slabs)], axis=-1)
    return jnp.concatenate([re, im], axis=0).astype(BF16)


def _fft_filter_kernel(k_ref, ga_ref, fb_ref, h_ref, s_ref, *, plan):
    _fft_stage_a(k_ref, ga_ref, s_ref, plan, plan.N1)

    def body(k1, carry):
        h_ref[k1] = jnp.dot(fb_ref[...], _fft_load_k1(s_ref, k1, plan), preferred_element_type=F32).astype(BF16)
        return carry
    lax.fori_loop(0, plan.KH, body, 0)


def _fft_conv_kernel(z_ref, ga_ref, gi_ref, fb_ref, fbi_ref, h_ref, y_ref, s_ref, *, plan):
    zs = z_ref.at[0]
    ys = y_ref.at[0]
    n_slabs = zs.shape[0]
    half = FFT_N2
    _fft_stage_a(zs, ga_ref, s_ref, plan, plan.N1 // 2)

    def body_b(k1, carry):
        x = jnp.dot(fb_ref[...], _fft_load_k1(s_ref, k1, plan), preferred_element_type=F32)
        h = h_ref[k1].astype(F32)
        xr, xi, hr, hi = x[:half], x[half:], h[:half], h[half:]
        y = jnp.concatenate([xr * hr - xi * hi, xr * hi + xi * hr], axis=0).astype(BF16)
        b = jnp.dot(fbi_ref[...], y, preferred_element_type=F32)
        for s in range(n_slabs):
            s_ref[s, pl.ds(k1, FFT_N2, stride=plan.PA), :] = b[:half, s * LANES:(s + 1) * LANES]
            s_ref[s, pl.ds(plan.KP + k1, FFT_N2, stride=plan.PA), :] = b[half:, s * LANES:(s + 1) * LANES]
        return carry
    lax.fori_loop(0, plan.KH, body_b, 0, unroll=3)

    def body_a(n2, carry):
        b = jnp.concatenate([s_ref[s, pl.ds(n2 * plan.PA, 2 * plan.KP), :] for s in range(n_slabs)], axis=-1)
        y = jnp.dot(gi_ref[n2], b.astype(BF16), preferred_element_type=F32)
        for s in range(n_slabs):
            ys[s, pl.ds(n2, plan.N1 // 2, stride=FFT_N2), :] = y[:, s * LANES:(s + 1) * LANES]
        return carry
    lax.fori_loop(0, FFT_N2, body_a, 0, unroll=FFT_UNROLL)


def fft_filter_spectrum(h_fwd, h_bwd, tabs):
    L, C = h_fwd.shape
    plan = _FftPlan(L)
    n_slabs = C // LANES
    k = jnp.concatenate([h_fwd, jnp.zeros((1, C), F32), h_bwd[1:][::-1]], axis=0)
    k = k.reshape(plan.N, n_slabs, LANES).transpose(1, 0, 2)
    full = lambda shape: pl.BlockSpec(shape, lambda i: (0,) * len(shape))
    return pl.pallas_call(
        functools.partial(_fft_filter_kernel, plan=plan),
        grid=(1,),
        in_specs=[full((n_slabs, plan.N, LANES)), full((FFT_N2, 2 * plan.KP, plan.N1)),
                  full((2 * FFT_N2, 2 * FFT_N2))],
        out_specs=full((plan.KH, 2 * FFT_N2, C)),
        out_shape=jax.ShapeDtypeStruct((plan.KH, 2 * FFT_N2, C), BF16),
        scratch_shapes=[pltpu.VMEM((n_slabs, FFT_N2 * plan.PA, LANES), F32)],
        compiler_params=_cparams("arbitrary"),
        name="hyena_filter_fft",
    )(k, tabs["ga_full"], tabs["fb"])


def fft_long_conv(z, h_spec, tabs):
    B, n_slabs, L, _ = z.shape
    plan = _FftPlan(L)
    C = n_slabs * LANES
    full = lambda shape: pl.BlockSpec(shape, lambda b: (0,) * len(shape))
    seq = pl.BlockSpec((1, n_slabs, L, LANES), lambda b: (b, 0, 0, 0))
    return pl.pallas_call(
        functools.partial(_fft_conv_kernel, plan=plan),
        grid=(B,),
        in_specs=[seq, full((FFT_N2, 2 * plan.KP, plan.N1 // 2)), full((FFT_N2, plan.N1 // 2, 2 * plan.KP)),
                  full((2 * FFT_N2, 2 * FFT_N2)), full((2 * FFT_N2, 2 * FFT_N2)),
                  full((plan.KH, 2 * FFT_N2, C))],
        out_specs=seq,
        out_shape=jax.ShapeDtypeStruct((B, n_slabs, L, LANES), F32),
        scratch_shapes=[pltpu.VMEM((n_slabs, FFT_N2 * plan.PA, LANES), F32)],
        compiler_params=_cparams("parallel"),
        name="hyena_fft_conv",
    )(z, tabs["ga_half"], tabs["ga_inv"], tabs["fb"], tabs["fb_inv"], h_spec)


def _hyena_pre_slab_kernel(u_ref, w_ref, b_ref, z_ref, upad):
    L = u_ref.shape[1]
    T = _time_chunk(L)
    C = D_GROUP
    _fill_padded(upad, u_ref, L, T)

    def body(j, carry):
        base = pl.multiple_of(j * T, T)
        for s in range(C // LANES):
            x1 = _dw_conv_slab(upad, C // LANES + s, base, T, w_ref, b_ref, C + s * LANES, HY_SHORT, 1)
            v = _dw_conv_slab(upad, 2 * C // LANES + s, base, T, w_ref, b_ref, 2 * C + s * LANES, HY_SHORT, 1)
            z_ref[0, s, pl.ds(base, T), :] = x1 * v
        return carry
    lax.fori_loop(0, L // T, body, 0)


def _hyena_post_slab_kernel(u0_ref, z_ref, y_ref, w_ref, b_ref, bias_ref, o_ref, upad):
    L = u0_ref.shape[1]
    T = _time_chunk(L)
    C = D_GROUP
    _fill_padded(upad, u0_ref, L, T)

    def body(j, carry):
        base = pl.multiple_of(j * T, T)
        x0 = _short_conv(upad, base, T, w_ref, b_ref, 0, C)
        z = jnp.concatenate([z_ref[0, s, pl.ds(base, T), :] for s in range(C // LANES)], axis=-1)
        y = jnp.concatenate([y_ref[0, s, pl.ds(base, T), :] for s in range(C // LANES)], axis=-1)
        o_ref[0, pl.ds(base, T), :] = x0 * (y + z * bias_ref[...])
        return carry
    lax.fori_loop(0, L // T, body, 0)


def hyena_mixer_fft(u, lp, tabs):
    B, L, _ = u.shape
    C = D_GROUP
    n_slabs = C // LANES
    w, bsh = lp["hy_short_w"], lp["hy_short_b"].reshape(1, 3 * C)
    useq = pl.BlockSpec((1, L, 3 * C), lambda b: (b, 0, 0))
    slabs = pl.BlockSpec((1, n_slabs, L, LANES), lambda b: (b, 0, 0, 0))
    wspec = pl.BlockSpec((HY_SHORT, 3 * C), lambda b: (0, 0))
    bspec = pl.BlockSpec((1, 3 * C), lambda b: (0, 0))
    pad_scratch = pltpu.VMEM((3 * C // LANES, L + 2 * CONV_MARGIN, LANES), F32)
    z = pl.pallas_call(
        _hyena_pre_slab_kernel,
        grid=(B,),
        in_specs=[useq, wspec, bspec],
        out_specs=slabs,
        out_shape=jax.ShapeDtypeStruct((B, n_slabs, L, LANES), F32),
        scratch_shapes=[pad_scratch],
        compiler_params=_cparams("parallel"),
        name="hyena_pre",
    )(u, w, bsh)
    h_fwd, h_bwd = _hyena_filters(L, lp)
    y = fft_long_conv(z, fft_filter_spectrum(h_fwd, h_bwd, tabs), tabs)
    return pl.pallas_call(
        _hyena_post_slab_kernel,
        grid=(B,),
        in_specs=[pl.BlockSpec((1, L, C), lambda b: (b, 0, 0)), slabs, slabs, wspec, bspec,
                  pl.BlockSpec((1, C), lambda b: (0, 0))],
        out_specs=pl.BlockSpec((1, L, C), lambda b: (b, 0, 0)),
        out_shape=jax.ShapeDtypeStruct((B, L, C), F32),
        scratch_shapes=[pltpu.VMEM((n_slabs, L + 2 * CONV_MARGIN, LANES), F32)],
        compiler_params=_cparams("parallel"),
        name="hyena_post",
    )(u, z, y, w, bsh, lp["hy_bias"].reshape(1, C))


def _hyena_filters(L, lp):
    t = jnp.linspace(0.0, 1.0, L, dtype=F32)[:, None]
    bands = (HY_EMB - 1) // 2
    w = 2.0 * math.pi * jnp.arange(L, dtype=F32)[:, None] / L
    f = jnp.linspace(1e-4, bands - 1, bands, dtype=F32)[None]
    z = jnp.concatenate([t, jnp.cos(f * w), -jnp.sin(f * w)], axis=-1)
    hdn = jnp.sin(z @ lp["hy_ffn_w1"] + lp["hy_ffn_b1"])
    hdn = jnp.sin(hdn @ lp["hy_ffn_w2"] + lp["hy_ffn_b2"])
    h = (hdn @ lp["hy_ffn_w3"]).reshape(L, 2, D_GROUP)
    max_decay = math.log(HY_TARGET) / HY_FAST_DECAY
    min_decay = math.log(HY_TARGET) / HY_SLOW_DECAY
    deltas = jnp.linspace(min_decay, max_decay, D_GROUP, dtype=F32)
    h = h * jnp.exp(-t * jnp.abs(deltas))[:, None, :]
    h = h / (jnp.sum(jnp.abs(h), axis=(0, 1), keepdims=True) + EPS)
    return h[:, 0], h[:, 1]


def _layer(hc, hx, c_silu_all, lp, need_ctx, final_g, final_norm, tables_x, tables_c):
    B, S, D = hx.shape
    C = hc.shape[1]
    mod = small_linear(c_silu_all, lp["ada_w"], lp["ada_b"])
    mod_x = mod[:B].reshape(B, 6, 1, D)
    mod_c = jnp.broadcast_to(mod[B].reshape(1, 6, 1, D), (B, 6, 1, D))
    w_ext = extend_w_in(lp["w_in"])
    cos_x, sin_x = rope_tables(S, True)
    cos_c, sin_c = rope_tables(C, False)
    hy_x, cf_x, at_x, lr_x = in_proj(hx, mod_x[:, 0], mod_x[:, 1], lp["norm1_g"], w_ext, cos_x, sin_x, tm=512)
    hy_c, cf_c, at_c, lr_c = in_proj(hc, mod_c[:, 0], mod_c[:, 1], lp["norm1_g"], w_ext, cos_c, sin_c, tm=256)

    yd_c, yd_x = rglru_mixer(lr_c, lr_x, lp, need_ctx)
    conf = lambda u: conformer_conv(u, lp["conf_dw_w"], lp["conf_dw_b"], lp["conf_ln_g"], lp["conf_ln_b"])
    ys_x = [hyena_mixer_fft(hy_x, lp, tables_x), conf(cf_x),
            window_attention(at_x, at_c, lp["attn_sink"]), yd_x]

    w_out = lp["w_out"].astype(BF16)
    w_router = jnp.zeros((D, ROUTER_COLS), F32)
    w_router = w_router.at[:, :N_GROUPS].set(lp["router_g_w"]).at[:, N_GROUPS:N_GROUPS + N_EXPERTS].set(lp["router_e_w"])
    w_router = w_router.astype(BF16)
    b_router = jnp.zeros((1, ROUTER_COLS), F32)
    b_router = b_router.at[0, :N_GROUPS].set(lp["router_g_b"]).at[0, N_GROUPS:N_GROUPS + N_EXPERTS].set(lp["router_e_b"])

    hx1, lg_x = out_proj(ys_x, hx, mod_x[:, 2], lp["group_norm_g"], w_out, lp["norm2_g"],
                         mod_x[:, 3], mod_x[:, 4], w_router, b_router, tm=512)
    h_tok = hx1.reshape(B * S, D)
    hc_tok = None
    lg = lg_x.reshape(B * S, ROUTER_COLS)
    if need_ctx:
        ys_c = [hyena_mixer(hy_c, lp, tables_c), conf(cf_c),
                context_attention(at_c, lp["attn_sink"]), yd_c]
        hc1, lg_c = out_proj(ys_c, hc, mod_c[:, 2], lp["group_norm_g"], w_out, lp["norm2_g"],
                             mod_c[:, 3], mod_c[:, 4], w_router, b_router, tm=256)
        hc_tok = hc1.reshape(B * C, D)
        lg = jnp.concatenate([lg, lg_c.reshape(B * C, ROUTER_COLS)], axis=0)

    T = lg.shape[0]
    n_blocks = -(-T // MOE_BLOCK) + N_CLASSES
    info, counts, ids = route_tokens(lg)
    dest, blk_a, blk_b, n_used = slot_plan(ids, counts, n_blocks)
    xs = moe_dispatch(h_tok, hc_tok, dest, n_blocks, lp["norm2_g"], (mod_x[:, 3], mod_x[:, 4]),
                      (mod_c[:, 3], mod_c[:, 4]), S)
    o_sorted = expert_pairs(xs, blk_a, blk_b, n_used, lp["exp_w_gate"].astype(BF16),
                            lp["exp_w_up"].astype(BF16), lp["exp_w_down"].astype(BF16))
    hx2 = moe_collect(o_sorted, dest, info, 0, h_tok, mod_x[:, 5], S, final_g, final_norm)
    hx2 = hx2.reshape(B, S, D)
    if need_ctx:
        hc2 = moe_collect(o_sorted, dest, info, B * S // DISPATCH_TOKENS, hc_tok, mod_c[:, 5], C,
                          final_g, False).reshape(B, C, D)
    else:
        hc2 = hc
    return hc2, hx2


def kernel(x, c, ctx, c_ctx, norm1_g, norm2_g, ada_w, ada_b, w_in, hy_short_w, hy_short_b, hy_ffn_w1, hy_ffn_b1, hy_ffn_w2, hy_ffn_b2, hy_ffn_w3, hy_bias, conf_dw_w, conf_dw_b, conf_ln_g, conf_ln_b, attn_sink, lru_conv_w, lru_conv_b, lru_wa, lru_ba, lru_wx, lru_bx, lru_lambda, group_norm_g, w_out, router_g_w, router_g_b, router_e_w, router_e_b, exp_w_gate, exp_w_up, exp_w_down, final_norm_g):
    stacked = dict(norm1_g=norm1_g, norm2_g=norm2_g, ada_w=ada_w, ada_b=ada_b, w_in=w_in,
                   hy_short_w=hy_short_w, hy_short_b=hy_short_b, hy_ffn_w1=hy_ffn_w1, hy_ffn_b1=hy_ffn_b1,
                   hy_ffn_w2=hy_ffn_w2, hy_ffn_b2=hy_ffn_b2, hy_ffn_w3=hy_ffn_w3, hy_bias=hy_bias,
                   conf_dw_w=conf_dw_w, conf_dw_b=conf_dw_b, conf_ln_g=conf_ln_g, conf_ln_b=conf_ln_b,
                   attn_sink=attn_sink, lru_conv_w=lru_conv_w, lru_conv_b=lru_conv_b, lru_wa=lru_wa,
                   lru_ba=lru_ba, lru_wx=lru_wx, lru_bx=lru_bx, lru_lambda=lru_lambda,
                   group_norm_g=group_norm_g, w_out=w_out, router_g_w=router_g_w, router_g_b=router_g_b,
                   router_e_w=router_e_w, router_e_b=router_e_b, exp_w_gate=exp_w_gate,
                   exp_w_up=exp_w_up, exp_w_down=exp_w_down)
    depth = norm1_g.shape[0]
    B = x.shape[0]
    cs = jnp.concatenate([jax.nn.silu(c), jnp.broadcast_to(jax.nn.silu(c_ctx)[None], (8, c.shape[1]))], axis=0)
    hc, hx = ctx, x
    tables_x = fft_tables(x.shape[1])
    tables_c = dft_tables(ctx.shape[1])
    for l in range(depth):
        lp = {k: v[l] for k, v in stacked.items()}
        hc, hx = _layer(hc, hx, cs, lp, need_ctx=(l < depth - 1), final_g=final_norm_g,
                        final_norm=(l == depth - 1), tables_x=tables_x, tables_c=tables_c)
    return hx
```

```python
import functools
import math

import jax
import jax.numpy as jnp
from jax import lax
from jax.experimental import pallas as pl
from jax.experimental.pallas import tpu as pltpu

F32 = jnp.float32
BF16 = jnp.bfloat16

EPS = 1e-6
NEG_INF = -1e30
GRID_W = 64
N_MIXERS = 4
D_GROUP = 256
HY_COLS = 3 * D_GROUP
CONF_COLS = 2 * D_GROUP
ATT_HEADS = 4
ATT_KV_HEADS = 2
HEAD_DIM = 64
ATT_COLS = (ATT_HEADS + 2 * ATT_KV_HEADS) * HEAD_DIM
LRU_COLS = 2 * D_GROUP
QK_COLS = (ATT_HEADS + ATT_KV_HEADS) * HEAD_DIM
WINDOW = 128
ATT_BLOCK = 128
ROPE_BASE = 10000.0
HY_EMB = 33
HY_FAST_DECAY = 0.3
HY_SLOW_DECAY = 1.5
HY_TARGET = 1e-2
CONF_KERNEL = 31
LRU_HEADS = 4
LRU_CONV = 4
LRU_C = 8.0
N_GROUPS = 4
EXP_PER_GROUP = 8
N_EXPERTS = N_GROUPS * EXP_PER_GROUP
TOP_K = 2
MOE_BLOCK = 256
ROUTER_COLS = 128

VMEM_LIMIT_BYTES = 56 * 1024 * 1024


def _cparams(*sem):
    return pltpu.CompilerParams(dimension_semantics=sem, vmem_limit_bytes=VMEM_LIMIT_BYTES)


def _linear_kernel(x_ref, w_ref, b_ref, o_ref):
    o_ref[...] = jnp.dot(x_ref[...], w_ref[...], preferred_element_type=F32,
                         precision=lax.Precision.HIGHEST) + b_ref[...]


def small_linear(x, w, b, tn=1024):
    M, K = x.shape
    N = w.shape[1]
    return pl.pallas_call(
        _linear_kernel,
        grid=(N // tn,),
        in_specs=[pl.BlockSpec((M, K), lambda j: (0, 0)),
                  pl.BlockSpec((K, tn), lambda j: (0, j)),
                  pl.BlockSpec((1, tn), lambda j: (0, j))],
        out_specs=pl.BlockSpec((M, tn), lambda j: (0, j)),
        out_shape=jax.ShapeDtypeStruct((M, N), F32),
        compiler_params=_cparams("parallel"),
        name="ada_linear",
    )(x, w, b.reshape(1, N))


def _in_proj_kernel(x_ref, sh_ref, sc_ref, g_ref, w_ref, cos_ref, sin_ref,
                    hy_ref, cf_ref, at_ref, lr_ref):
    x = x_ref[0]
    ms = jnp.mean(x * x, axis=-1, keepdims=True)
    y = x * lax.rsqrt(ms + EPS) * g_ref[...]
    y = y * (1.0 + sc_ref[0]) + sh_ref[0]
    u = jnp.dot(y.astype(BF16), w_ref[...], preferred_element_type=F32)
    c0 = HY_COLS
    c1 = c0 + CONF_COLS
    c2 = c1 + ATT_COLS
    c3 = c2 + LRU_COLS
    hy_ref[0] = u[:, :c0]
    cf_ref[0] = u[:, c0:c1]
    lr_ref[0] = u[:, c2:c3]
    qk = u[:, c1:c1 + QK_COLS]
    qk_rot = u[:, c3:c3 + QK_COLS]
    at_ref[0, :, :QK_COLS] = qk * cos_ref[...] + qk_rot * sin_ref[...]
    at_ref[0, :, QK_COLS:] = u[:, c1 + QK_COLS:c2]


def in_proj(h, shift, scale, g, w_ext, cos_t, sin_t, tm):
    B, L, D = h.shape
    NW = w_ext.shape[1]
    outs = [HY_COLS, CONF_COLS, ATT_COLS, LRU_COLS]
    return pl.pallas_call(
        _in_proj_kernel,
        grid=(B, L // tm),
        in_specs=[pl.BlockSpec((1, tm, D), lambda b, i: (b, i, 0)),
                  pl.BlockSpec((1, 1, D), lambda b, i: (b, 0, 0)),
                  pl.BlockSpec((1, 1, D), lambda b, i: (b, 0, 0)),
                  pl.BlockSpec((1, D), lambda b, i: (0, 0)),
                  pl.BlockSpec((D, NW), lambda b, i: (0, 0)),
                  pl.BlockSpec((tm, QK_COLS), lambda b, i: (i, 0)),
                  pl.BlockSpec((tm, QK_COLS), lambda b, i: (i, 0))],
        out_specs=[pl.BlockSpec((1, tm, n), lambda b, i: (b, i, 0)) for n in outs],
        out_shape=[jax.ShapeDtypeStruct((B, L, n), F32) for n in outs],
        compiler_params=_cparams("parallel", "parallel"),
        name="in_proj",
    )(h, shift, scale, g.reshape(1, D), w_ext, cos_t, sin_t)


def rope_tables(L, rotary):
    n_heads = ATT_HEADS + ATT_KV_HEADS
    if not rotary:
        return jnp.ones((L, QK_COLS), F32), jnp.zeros((L, QK_COLS), F32)
    pos = jnp.arange(L)
    row = (pos // GRID_W).astype(F32)
    col = (pos % GRID_W).astype(F32)
    half = HEAD_DIM // 2
    inv_freq = ROPE_BASE ** (-jnp.arange(0, half, 2, dtype=F32) / half)
    ang_r = row[:, None] * inv_freq[None]
    ang_c = col[:, None] * inv_freq[None]
    cos_h = jnp.concatenate([jnp.cos(ang_r)] * 2 + [jnp.cos(ang_c)] * 2, axis=-1)
    sin_h = jnp.concatenate([jnp.sin(ang_r)] * 2 + [jnp.sin(ang_c)] * 2, axis=-1)
    return jnp.tile(cos_h, (1, n_heads)), jnp.tile(sin_h, (1, n_heads))


def extend_w_in(w_in):
    c1 = HY_COLS + CONF_COLS
    wqk = w_in[:, c1:c1 + QK_COLS]
    D = w_in.shape[0]
    w4 = wqk.reshape(D, QK_COLS // 32, 2, 16)
    wrot = jnp.stack([-w4[:, :, 1], w4[:, :, 0]], axis=2).reshape(D, QK_COLS)
    return jnp.concatenate([w_in, wrot], axis=1).astype(BF16)


def _softmax_parts(q, k_list, extra_logit):
    scale = HEAD_DIM ** -0.5
    s_list = []
    for k, mask in k_list:
        s = lax.dot_general(q, k, (((1,), (1,)), ((), ())), preferred_element_type=F32) * scale
        if mask is not None:
            s = jnp.where(mask, s, NEG_INF)
        s_list.append(s)
    m = extra_logit
    for s in s_list:
        m = jnp.maximum(m, jnp.max(s, axis=-1, keepdims=True))
    p_list = [jnp.exp(s - m) for s in s_list]
    denom = jnp.exp(extra_logit - m)
    for p in p_list:
        denom = denom + jnp.sum(p, axis=-1, keepdims=True)
    return p_list, 1.0 / denom


ATT_Q_BLOCKS = 4


def _win_attn_kernel(sink_ref, q_ref, kp_ref, kc_ref, kn_ref, vp_ref, vc_ref, vn_ref,
                     kx_ref, vx_ref, o_ref, *, seq_len):
    i = pl.program_id(1)
    blk = ATT_BLOCK
    qb = q_ref.shape[1] // blk
    scale = HEAD_DIM ** -0.5
    g = ATT_HEADS // ATT_KV_HEADS
    kw = jnp.concatenate([kp_ref[0], kc_ref[0], kn_ref[0]], axis=0)
    vw = jnp.concatenate([vp_ref[0], vc_ref[0], vn_ref[0]], axis=0).astype(BF16)
    kwt = kw.T.astype(BF16)
    kxt = kx_ref[0].T.astype(BF16)
    vx = vx_ref[0].astype(BF16)
    row = lax.broadcasted_iota(jnp.int32, (g * blk, 3 * blk), 0) % blk
    col = lax.broadcasted_iota(jnp.int32, (g * blk, 3 * blk), 1)
    band_bias = jnp.where(jnp.abs(col - blk - row) <= WINDOW, 0.0, NEG_INF)
    col1 = lax.broadcasted_iota(jnp.int32, (1, 3 * blk), 1)
    for j in range(qb):
        q_blk = i * qb + j
        k_pos = (q_blk - 1) * blk + col1
        edge_bias = jnp.where(k_pos >= 0, jnp.where(k_pos < seq_len, 0.0, NEG_INF), NEG_INF)
        bias = band_bias + edge_bias
        outs = []
        for kv in range(ATT_KV_HEADS):
            ksl = slice(kv * HEAD_DIM, (kv + 1) * HEAD_DIM)
            heads = range(kv * g, (kv + 1) * g)
            qs = (jnp.concatenate([q_ref[0, j * blk:(j + 1) * blk, h * HEAD_DIM:(h + 1) * HEAD_DIM]
                                   for h in heads], axis=0) * scale).astype(BF16)
            sink = jnp.concatenate([jnp.full((blk, 1), sink_ref[h], F32) for h in heads], axis=0)
            s_win = jnp.dot(qs, kwt[ksl, j * blk:(j + 3) * blk], preferred_element_type=F32) + bias
            s_ctx = jnp.dot(qs, kxt[ksl, :], preferred_element_type=F32)
            m = jnp.maximum(jnp.maximum(jnp.max(s_win, axis=-1, keepdims=True),
                                        jnp.max(s_ctx, axis=-1, keepdims=True)), sink)
            p_win = jnp.exp(s_win - m)
            p_ctx = jnp.exp(s_ctx - m)
            denom = (jnp.exp(sink - m) + jnp.sum(p_win, axis=-1, keepdims=True)
                     + jnp.sum(p_ctx, axis=-1, keepdims=True))
            o = (jnp.dot(p_win.astype(BF16), vw[j * blk:(j + 3) * blk, ksl], preferred_element_type=F32)
                 + jnp.dot(p_ctx.astype(BF16), vx[:, ksl], preferred_element_type=F32)) * (1.0 / denom)
            outs.extend([o[k * blk:(k + 1) * blk] for k in range(g)])
        o_ref[0, j * blk:(j + 1) * blk, :] = jnp.concatenate(outs, axis=-1)


def window_attention(at_x, at_c, sink):
    B, S, _ = at_x.shape
    C = at_c.shape[1]
    blk = ATT_BLOCK
    qb = ATT_Q_BLOCKS
    nb = S // blk
    kcol = QK_COLS // 128 - 1
    vcol = kcol + 1

    def edge_spec(col, off):
        return pl.BlockSpec((1, blk, 128), lambda b, i, s: (b, jnp.clip(i * qb + off, 0, nb - 1), col))

    def mid_spec(col):
        return pl.BlockSpec((1, qb * blk, 128), lambda b, i, s: (b, i, col))

    grid_spec = pltpu.PrefetchScalarGridSpec(
        num_scalar_prefetch=1,
        grid=(B, nb // qb),
        in_specs=[pl.BlockSpec((1, qb * blk, ATT_HEADS * HEAD_DIM), lambda b, i, s: (b, i, 0)),
                  edge_spec(kcol, -1), mid_spec(kcol), edge_spec(kcol, qb),
                  edge_spec(vcol, -1), mid_spec(vcol), edge_spec(vcol, qb),
                  pl.BlockSpec((1, C, 128), lambda b, i, s: (b, 0, kcol)),
                  pl.BlockSpec((1, C, 128), lambda b, i, s: (b, 0, vcol))],
        out_specs=pl.BlockSpec((1, qb * blk, ATT_HEADS * HEAD_DIM), lambda b, i, s: (b, i, 0)),
    )
    return pl.pallas_call(
        functools.partial(_win_attn_kernel, seq_len=S),
        grid_spec=grid_spec,
        out_shape=jax.ShapeDtypeStruct((B, S, ATT_HEADS * HEAD_DIM), F32),
        compiler_params=_cparams("parallel", "parallel"),
        name="window_attention",
    )(sink.astype(F32), at_x, at_x, at_x, at_x, at_x, at_x, at_x, at_c, at_c)


def _ctx_attn_kernel(sink_ref, q_ref, kx_ref, vx_ref, o_ref):
    q = q_ref[0].astype(BF16)
    kx = kx_ref[0].astype(BF16)
    vx = vx_ref[0].astype(BF16)
    g = ATT_HEADS // ATT_KV_HEADS
    outs = []
    for h in range(ATT_HEADS):
        kv = h // g
        qs = q[:, h * HEAD_DIM:(h + 1) * HEAD_DIM]
        ksl = slice(kv * HEAD_DIM, (kv + 1) * HEAD_DIM)
        (p_ctx,), inv = _softmax_parts(qs, [(kx[:, ksl], None)], sink_ref[h])
        outs.append(jnp.dot(p_ctx.astype(BF16), vx[:, ksl], preferred_element_type=F32) * inv)
    o_ref[0] = jnp.concatenate(outs, axis=-1)


def context_attention(at_c, sink):
    B, C, _ = at_c.shape
    kcol = QK_COLS // 128 - 1
    grid_spec = pltpu.PrefetchScalarGridSpec(
        num_scalar_prefetch=1,
        grid=(B,),
        in_specs=[pl.BlockSpec((1, C, ATT_HEADS * HEAD_DIM), lambda b, s: (b, 0, 0)),
                  pl.BlockSpec((1, C, 128), lambda b, s: (b, 0, kcol)),
                  pl.BlockSpec((1, C, 128), lambda b, s: (b, 0, kcol + 1))],
        out_specs=pl.BlockSpec((1, C, ATT_HEADS * HEAD_DIM), lambda b, s: (b, 0, 0)),
    )
    return pl.pallas_call(
        _ctx_attn_kernel,
        grid_spec=grid_spec,
        out_shape=jax.ShapeDtypeStruct((B, C, ATT_HEADS * HEAD_DIM), F32),
        compiler_params=_cparams("parallel"),
        name="context_attention",
    )(sink.astype(F32), at_c, at_c, at_c)


def _out_proj_kernel(y0_ref, y1_ref, y2_ref, y3_ref, h_ref, g1_ref, gng_ref, w_ref,
                     n2g_ref, sh_ref, sc_ref, wr_ref, br_ref, ho_ref, lg_ref):
    parts = []
    for k, y_ref in enumerate((y0_ref, y1_ref, y2_ref, y3_ref)):
        y = y_ref[0]
        ms = jnp.mean(y * y, axis=-1, keepdims=True)
        yn = y * lax.rsqrt(ms + EPS) * gng_ref[:, k * D_GROUP:(k + 1) * D_GROUP]
        parts.append(yn.astype(BF16))
    yn = jnp.concatenate(parts, axis=-1)
    proj = jnp.dot(yn, w_ref[...], preferred_element_type=F32)
    h = h_ref[0] + g1_ref[0] * proj
    ho_ref[0] = h
    ms = jnp.mean(h * h, axis=-1, keepdims=True)
    n = h * lax.rsqrt(ms + EPS) * n2g_ref[...]
    n = n * (1.0 + sc_ref[0]) + sh_ref[0]
    lg_ref[0] = jnp.dot(n.astype(BF16), wr_ref[...], preferred_element_type=F32) + br_ref[...]


def out_proj(ys, h, g1, gng, w_out, n2g, sh2, sc2, w_router, b_router, tm):
    B, L, D = h.shape
    row3 = lambda n: pl.BlockSpec((1, tm, n), lambda b, i: (b, i, 0))
    mod = pl.BlockSpec((1, 1, D), lambda b, i: (b, 0, 0))
    full = lambda r, c: pl.BlockSpec((r, c), lambda b, i: (0, 0))
    return pl.pallas_call(
        _out_proj_kernel,
        grid=(B, L // tm),
        in_specs=[row3(D_GROUP)] * 4 + [row3(D), mod, full(1, D), full(D, D), full(1, D), mod, mod,
                                        full(D, ROUTER_COLS), full(1, ROUTER_COLS)],
        out_specs=[row3(D), row3(ROUTER_COLS)],
        out_shape=[jax.ShapeDtypeStruct((B, L, D), F32), jax.ShapeDtypeStruct((B, L, ROUTER_COLS), F32)],
        compiler_params=_cparams("parallel", "parallel"),
        name="out_proj",
    )(*ys, h, g1, gng.reshape(1, D), w_out, n2g.reshape(1, D), sh2, sc2, w_router, b_router)


N_PAIRS = EXP_PER_GROUP * (EXP_PER_GROUP - 1) // 2
N_CLASSES = N_GROUPS * N_PAIRS
ROUTE_TOKENS = 512
INFO_CLASS, INFO_RANK, INFO_WA, INFO_WB = 0, 1, 2, 3


SUBLANES = 8


def _route_kernel(lg_ref, info_ref, cnt_ref, ids_ref, run):
    i = pl.program_id(0)
    tb = lg_ref.shape[0]

    @pl.when(i == 0)
    def _():
        run[...] = jnp.zeros_like(run)

    lg = lg_ref[...]
    li = lax.broadcasted_iota(jnp.int32, lg.shape, 1)
    big = jnp.int32(ROUTER_COLS)

    def first_argmax(vals):
        m = jnp.max(vals, axis=-1, keepdims=True)
        return m, jnp.min(jnp.where(vals == m, li, big), axis=-1, keepdims=True)

    gl = jnp.where(li < N_GROUPS, lg, NEG_INF)
    gmax, g_idx = first_argmax(gl)
    g_prob = 1.0 / jnp.sum(jnp.exp(gl - gmax), axis=-1, keepdims=True)
    lo = N_GROUPS + EXP_PER_GROUP * g_idx
    el = jnp.where((li >= lo) & (li < lo + EXP_PER_GROUP), lg, NEG_INF)
    m1, i1 = first_argmax(el)
    m2, i2 = first_argmax(jnp.where(li == i1, NEG_INF, el))
    e2 = jnp.exp(m2 - m1)
    w1 = g_prob / (1.0 + e2)
    w2 = g_prob * e2 / (1.0 + e2)
    j1 = i1 - lo
    j2 = i2 - lo
    a = jnp.minimum(j1, j2)
    b = jnp.maximum(j1, j2)
    cls = g_idx * N_PAIRS + ((a * (2 * EXP_PER_GROUP - 1 - a)) >> 1) + (b - a - 1)
    w_a = jnp.where(j1 < j2, w1, w2)
    w_b = jnp.where(j1 < j2, w2, w1)

    hit = li == cls
    onehot = jnp.where(hit, 1.0, 0.0)
    r_i = lax.broadcasted_iota(jnp.int32, (tb, tb), 0)
    c_i = lax.broadcasted_iota(jnp.int32, (tb, tb), 1)
    below = jnp.where(c_i < r_i, 1.0, 0.0).astype(BF16)
    before = jnp.dot(below, onehot.astype(BF16), preferred_element_type=F32)
    rank = jnp.sum(jnp.where(hit, before + run[...], 0.0), axis=-1, keepdims=True)
    run[...] = run[...] + jnp.sum(onehot, axis=0, keepdims=True)
    cnt_ref[...] = run[...]
    info = jnp.where(li == INFO_CLASS, cls.astype(F32), 0.0)
    info = jnp.where(li == INFO_RANK, rank, info)
    info = jnp.where(li == INFO_WA, w_a, info)
    info = jnp.where(li == INFO_WB, w_b, info)
    info_ref[...] = info
    ids_ref[0] = info.T[:SUBLANES].astype(jnp.int32)


def route_tokens(logits):
    T = logits.shape[0]
    tb = ROUTE_TOKENS
    return pl.pallas_call(
        _route_kernel,
        grid=(T // tb,),
        in_specs=[pl.BlockSpec((tb, ROUTER_COLS), lambda i: (i, 0))],
        out_specs=[pl.BlockSpec((tb, ROUTER_COLS), lambda i: (i, 0)),
                   pl.BlockSpec((1, ROUTER_COLS), lambda i: (0, 0)),
                   pl.BlockSpec((1, SUBLANES, tb), lambda i: (i, 0, 0))],
        out_shape=[jax.ShapeDtypeStruct((T, ROUTER_COLS), F32), jax.ShapeDtypeStruct((1, ROUTER_COLS), F32),
                   jax.ShapeDtypeStruct((T // tb, SUBLANES, tb), jnp.int32)],
        scratch_shapes=[pltpu.VMEM((1, ROUTER_COLS), F32)],
        compiler_params=_cparams("arbitrary"),
        name="moe_route",
    )(logits)


def _pair_tables():
    a_tab, b_tab = [], []
    for g in range(N_GROUPS):
        for a in range(EXP_PER_GROUP):
            for b in range(a + 1, EXP_PER_GROUP):
                a_tab.append(g * EXP_PER_GROUP + a)
                b_tab.append(g * EXP_PER_GROUP + b)
    return jnp.array(a_tab, jnp.int32), jnp.array(b_tab, jnp.int32)


def _slot_kernel(ids_ref, start_ref, dest_ref):
    cls = ids_ref[0, INFO_CLASS:INFO_CLASS + 1, :]
    rank = ids_ref[0, INFO_RANK:INFO_RANK + 1, :]
    ci = lax.broadcasted_iota(jnp.int32, (ROUTER_COLS, cls.shape[1]), 0)
    start = jnp.sum(jnp.where(ci == cls, start_ref[...], 0), axis=0, keepdims=True)
    dest_ref[0] = jnp.broadcast_to(start + rank, dest_ref.shape[1:])


def slot_plan(ids, counts, n_blocks):
    nt, _, tb = ids.shape
    cnt = counts[0, :N_CLASSES].astype(jnp.int32)
    padded = (cnt + MOE_BLOCK - 1) // MOE_BLOCK * MOE_BLOCK
    upto = jnp.arange(N_CLASSES)[None, :] <= jnp.arange(N_CLASSES)[:, None]
    pad_end = jnp.sum(jnp.where(upto, padded[None, :], 0), axis=1)
    class_start = jnp.zeros((ROUTER_COLS, 1), jnp.int32).at[:N_CLASSES, 0].set(pad_end - padded)
    dest = pl.pallas_call(
        _slot_kernel,
        grid=(nt,),
        in_specs=[pl.BlockSpec((1, SUBLANES, tb), lambda i: (i, 0, 0)),
                  pl.BlockSpec((ROUTER_COLS, 1), lambda i: (0, 0))],
        out_specs=pl.BlockSpec((1, SUBLANES, tb), lambda i: (i, 0, 0)),
        out_shape=jax.ShapeDtypeStruct((nt, SUBLANES, tb), jnp.int32),
        compiler_params=_cparams("parallel"),
        name="moe_slots",
    )(ids, class_start)[:, 0, :].reshape(nt * tb)
    n_used = (pad_end[-1] // MOE_BLOCK).astype(jnp.int32).reshape(1)
    blk_first = jnp.arange(n_blocks, dtype=jnp.int32) * MOE_BLOCK
    blk_cls = jnp.minimum(jnp.sum((pad_end[None, :] <= blk_first[:, None]).astype(jnp.int32), axis=1),
                          N_CLASSES - 1)
    a_tab, b_tab = _pair_tables()
    hit = blk_cls[:, None] == jnp.arange(N_CLASSES)[None, :]
    pick = lambda tab: jnp.sum(jnp.where(hit, tab[None, :], 0), axis=1).astype(jnp.int32)
    return dest, pick(a_tab), pick(b_tab), n_used


DISPATCH_TOKENS = 256


def _wait_rows(buf, sem):
    pltpu.make_async_copy(buf, buf, sem).wait()


DMA_UNROLL = 8
TOKEN_TILE_ROWS = 8


def _store_token_tiles(tiles_ref, offset, pitch, x):
    n = x.shape[0]
    for j in range(x.shape[1] // LANES):
        tiles_ref[pl.ds(offset + j, n, stride=pitch), :] = x[:, j * LANES:(j + 1) * LANES]


def _load_token_tiles(tiles_ref, offset, pitch, n, width):
    return jnp.concatenate([tiles_ref[pl.ds(offset + j, n, stride=pitch), :] for j in range(width // LANES)],
                           axis=-1)


def _dispatch_kernel(dest_ref, hx_ref, hc_ref, g_ref, shx_ref, scx_ref, shc_ref, scc_ref, zeros_hbm,
                     xs_hbm, rows, sems, *, n_latent_blocks):
    del zeros_hbm
    i = pl.program_id(0)
    n = pl.num_programs(0)
    slot = i % 2
    tb = hx_ref.shape[0]

    @pl.when(i >= 2)
    def _():
        _wait_rows(rows.at[slot], sems.at[slot])

    def normed(h_ref, sh_ref, sc_ref):
        h = h_ref[...]
        ms = jnp.mean(h * h, axis=-1, keepdims=True)
        return h * lax.rsqrt(ms + EPS) * g_ref[...] * (1.0 + sc_ref[0]) + sh_ref[0]

    @pl.when(i < n_latent_blocks)
    def _():
        _store_token_tiles(rows.at[slot], 0, TOKEN_TILE_ROWS, normed(hx_ref, shx_ref, scx_ref))

    @pl.when(i >= n_latent_blocks)
    def _():
        _store_token_tiles(rows.at[slot], 0, TOKEN_TILE_ROWS, normed(hc_ref, shc_ref, scc_ref))

    def body(g, carry):
        for u in range(DMA_UNROLL):
            r = g * DMA_UNROLL + u
            dst = pl.multiple_of(dest_ref[0, 0, r] * TOKEN_TILE_ROWS, TOKEN_TILE_ROWS)
            pltpu.make_async_copy(rows.at[slot, pl.ds(r * TOKEN_TILE_ROWS, TOKEN_TILE_ROWS)],
                                  xs_hbm.at[pl.ds(dst, TOKEN_TILE_ROWS)], sems.at[slot]).start(priority=u % 2)
        return carry
    lax.fori_loop(0, tb // DMA_UNROLL, body, 0)

    @pl.when(i == n - 1)
    def _():
        _wait_rows(rows.at[slot], sems.at[slot])

        @pl.when(n >= 2)
        def _():
            _wait_rows(rows.at[1 - slot], sems.at[1 - slot])


def moe_dispatch(h_x, h_c, dest, n_blocks, n2g, mod_x, mod_c, tokens_per_batch):
    Tx, D = h_x.shape
    tb = DISPATCH_TOKENS
    nxb = Tx // tb
    if h_c is None:
        h_c, mod_c, ncb = h_x, mod_x, 0
    else:
        ncb = h_c.shape[0] // tb
    per_b = tokens_per_batch // tb
    P = n_blocks * MOE_BLOCK
    tile_rows = D // LANES
    assert tile_rows == TOKEN_TILE_ROWS
    xi = lambda i: jnp.minimum(i, nxb - 1)
    ci = lambda i: jnp.maximum(i - nxb, 0)
    modx = pl.BlockSpec((1, 1, D), lambda i: (xi(i) // per_b, 0, 0))
    modc = pl.BlockSpec((1, 1, D), lambda i: (0, 0, 0))
    return pl.pallas_call(
        functools.partial(_dispatch_kernel, n_latent_blocks=nxb),
        grid=(nxb + ncb,),
        in_specs=[pl.BlockSpec((1, 1, tb), lambda i: (i, 0, 0), memory_space=pltpu.SMEM),
                  pl.BlockSpec((tb, D), lambda i: (xi(i), 0)),
                  pl.BlockSpec((tb, D), lambda i: (ci(i), 0)),
                  pl.BlockSpec((1, D), lambda i: (0, 0)),
                  modx, modx, modc, modc,
                  pl.BlockSpec(memory_space=pl.ANY)],
        out_specs=pl.BlockSpec(memory_space=pl.ANY),
        out_shape=jax.ShapeDtypeStruct((P * tile_rows, LANES), F32),
        scratch_shapes=[pltpu.VMEM((2, tb * tile_rows, LANES), F32), pltpu.SemaphoreType.DMA((2,))],
        input_output_aliases={8: 0},
        compiler_params=_cparams("arbitrary"),
        name="moe_dispatch",
    )(dest.reshape(-1, 1, tb), h_x, h_c, n2g.reshape(1, D), mod_x[0], mod_x[1], mod_c[0], mod_c[1],
      jnp.zeros((P * tile_rows, LANES), F32))


def _expert_pair_kernel(ea_ref, eb_ref, nused_ref, xs_ref, wga_ref, wua_ref, wda_ref, wgb_ref, wub_ref, wdb_ref,
                        o_ref):
    del ea_ref, eb_ref
    i = pl.program_id(0)
    D = wga_ref.shape[2]

    @pl.when(i < nused_ref[0])
    def _():
        xb = _load_token_tiles(xs_ref, 0, TOKEN_TILE_ROWS, MOE_BLOCK, D).astype(BF16)
        halves = []
        for wg_ref, wu_ref, wd_ref in ((wga_ref, wua_ref, wda_ref), (wgb_ref, wub_ref, wdb_ref)):
            gate = jnp.dot(xb, wg_ref[0, 0], preferred_element_type=F32)
            up = jnp.dot(xb, wu_ref[0, 0], preferred_element_type=F32)
            hid = (gate * jax.nn.sigmoid(gate) * up).astype(BF16)
            out = jnp.dot(hid, wd_ref[0, 0], preferred_element_type=F32)
            halves.append(lax.bitcast_convert_type(out.astype(BF16).astype(F32), jnp.uint32))
        _store_token_tiles(o_ref, 0, TOKEN_TILE_ROWS, halves[0] | (halves[1] >> 16))

    @pl.when(i >= nused_ref[0])
    def _():
        o_ref[...] = jnp.zeros_like(o_ref)


def _unpack_pair(words):
    hi = lax.bitcast_convert_type(words & jnp.uint32(0xFFFF0000), F32)
    lo = lax.bitcast_convert_type(words << 16, F32)
    return hi, lo


def expert_pairs(xs, blk_a, blk_b, n_used, w_gate, w_up, w_down, layer):
    D, DE = w_gate.shape[2:]
    P = xs.shape[0] // TOKEN_TILE_ROWS
    n_blocks = P // MOE_BLOCK
    wspec = lambda shape, which: pl.BlockSpec(shape, lambda i, ea, eb, nu: (layer, (ea, eb)[which][i], 0, 0))
    grid_spec = pltpu.PrefetchScalarGridSpec(
        num_scalar_prefetch=3,
        grid=(n_blocks,),
        in_specs=[pl.BlockSpec((MOE_BLOCK * TOKEN_TILE_ROWS, LANES), lambda i, ea, eb, nu: (i, 0)),
                  wspec((1, 1, D, DE), 0), wspec((1, 1, D, DE), 0), wspec((1, 1, DE, D), 0),
                  wspec((1, 1, D, DE), 1), wspec((1, 1, D, DE), 1), wspec((1, 1, DE, D), 1)],
        out_specs=pl.BlockSpec((MOE_BLOCK * TOKEN_TILE_ROWS, LANES), lambda i, ea, eb, nu: (i, 0)),
    )
    return pl.pallas_call(
        _expert_pair_kernel,
        grid_spec=grid_spec,
        out_shape=jax.ShapeDtypeStruct((P * TOKEN_TILE_ROWS, LANES), jnp.uint32),
        compiler_params=_cparams("arbitrary"),
        name="moe_experts",
    )(blk_a, blk_b, n_used, xs, w_gate, w_up, w_down, w_gate, w_up, w_down)


def _gather_pairs(idx_ref, src_hbm, buf, sem, n_tokens):
    def body(g, carry):
        for u in range(DMA_UNROLL):
            r = g * DMA_UNROLL + u
            src = pl.multiple_of(idx_ref[0, 0, r] * TOKEN_TILE_ROWS, TOKEN_TILE_ROWS)
            pltpu.make_async_copy(src_hbm.at[pl.ds(src, TOKEN_TILE_ROWS)],
                                  buf.at[pl.ds(r * TOKEN_TILE_ROWS, TOKEN_TILE_ROWS)], sem).start(priority=u % 2)
        return carry
    lax.fori_loop(0, n_tokens // DMA_UNROLL, body, 0)


def _collect_kernel(dest_ref, dest_next_ref, o_hbm, info_ref, h_ref, g2_ref, fg_ref, out_ref, obuf, sems, *,
                    final_norm):
    i = pl.program_id(0)
    n = pl.num_programs(0)
    slot = i % 2
    tb, D = h_ref.shape

    @pl.when(i == 0)
    def _():
        _gather_pairs(dest_ref, o_hbm, obuf.at[0], sems.at[0], tb)

    @pl.when(i + 1 < n)
    def _():
        _gather_pairs(dest_next_ref, o_hbm, obuf.at[1 - slot], sems.at[1 - slot], tb)

    _wait_rows(obuf.at[slot], sems.at[slot])
    e_a, e_b = _unpack_pair(_load_token_tiles(obuf.at[slot], 0, TOKEN_TILE_ROWS, tb, D))
    m = info_ref[:, INFO_WA:INFO_WA + 1] * e_a + info_ref[:, INFO_WB:INFO_WB + 1] * e_b
    h = h_ref[...] + g2_ref[0] * m
    if final_norm:
        ms = jnp.mean(h * h, axis=-1, keepdims=True)
        h = h * lax.rsqrt(ms + EPS) * fg_ref[...]
    out_ref[...] = h


def moe_collect(o_sorted, dest, info, block_offset, h_tokens, g2, tokens_per_batch, final_g, final_norm):
    T, D = h_tokens.shape
    tb = DISPATCH_TOKENS
    nt = T // tb
    per_b = tokens_per_batch // tb
    last = block_offset + nt - 1
    dest3 = dest.reshape(-1, 1, tb)
    return pl.pallas_call(
        functools.partial(_collect_kernel, final_norm=final_norm),
        grid=(nt,),
        in_specs=[pl.BlockSpec((1, 1, tb), lambda i: (block_offset + i, 0, 0), memory_space=pltpu.SMEM),
                  pl.BlockSpec((1, 1, tb), lambda i: (jnp.minimum(block_offset + i + 1, last), 0, 0),
                               memory_space=pltpu.SMEM),
                  pl.BlockSpec(memory_space=pl.ANY),
                  pl.BlockSpec((tb, ROUTER_COLS), lambda i: (block_offset + i, 0)),
                  pl.BlockSpec((tb, D), lambda i: (i, 0)),
                  pl.BlockSpec((1, 1, D), lambda i: (i // per_b, 0, 0)),
                  pl.BlockSpec((1, D), lambda i: (0, 0))],
        out_specs=pl.BlockSpec((tb, D), lambda i: (i, 0)),
        out_shape=jax.ShapeDtypeStruct((T, D), F32),
        scratch_shapes=[pltpu.VMEM((2, tb * TOKEN_TILE_ROWS, LANES), jnp.uint32), pltpu.SemaphoreType.DMA((2,))],
        compiler_params=_cparams("arbitrary"),
        name="moe_collect",
    )(dest3, dest3, o_sorted, info, h_tokens, g2, final_g.reshape(1, D))


CONV_MARGIN = 16


def _time_chunk(L):
    return min(L, 256)


LANES = 128


def _zero_margins(pad_ref, L):
    zeros = jnp.zeros((CONV_MARGIN, LANES), F32)
    for s in range(pad_ref.shape[0]):
        pad_ref[s, pl.ds(0, CONV_MARGIN), :] = zeros
        pad_ref[s, pl.ds(CONV_MARGIN + L, CONV_MARGIN), :] = zeros


def _dw_conv_slab(pad_ref, s, base, T, w_ref, b_ref, col, taps, pad_left):
    acc = jnp.broadcast_to(b_ref[:, col:col + LANES], (T, LANES))
    for k in range(taps):
        acc = acc + w_ref[k:k + 1, col:col + LANES] * pad_ref[s, pl.ds(base + (CONV_MARGIN - pad_left + k), T), :]
    return acc


def _conformer_kernel(u_ref, w_ref, b_ref, g_ref, beta_ref, o_ref, ypad):
    L = o_ref.shape[1]
    T = _time_chunk(L)
    C = D_GROUP
    n_slabs = C // LANES
    pad = (CONF_KERNEL - 1) // 2
    _zero_margins(ypad, L)

    def glu(j, carry):
        base = pl.multiple_of(j * T, T)
        for s in range(n_slabs):
            a = u_ref[0, pl.ds(base, T), s * LANES:(s + 1) * LANES]
            gate = u_ref[0, pl.ds(base, T), C + s * LANES:C + (s + 1) * LANES]
            ypad[s, pl.ds(CONV_MARGIN + base, T), :] = a * jax.nn.sigmoid(gate)
        return carry
    lax.fori_loop(0, L // T, glu, 0)

    def conv(j, carry):
        base = pl.multiple_of(j * T, T)
        acc = jnp.concatenate([_dw_conv_slab(ypad, s, base, T, w_ref, b_ref, s * LANES, CONF_KERNEL, pad)
                               for s in range(n_slabs)], axis=-1)
        mu = jnp.mean(acc, axis=-1, keepdims=True)
        cen = acc - mu
        var = jnp.mean(cen * cen, axis=-1, keepdims=True)
        y = cen * lax.rsqrt(var + EPS) * g_ref[...] + beta_ref[...]
        o_ref[0, pl.ds(base, T), :] = y * jax.nn.sigmoid(y)
        return carry
    lax.fori_loop(0, L // T, conv, 0)


def conformer_conv(u, w, b, ln_g, ln_b):
    B, L, _ = u.shape
    C = D_GROUP
    vec = pl.BlockSpec((1, C), lambda i: (0, 0))
    return pl.pallas_call(
        _conformer_kernel,
        grid=(B,),
        in_specs=[pl.BlockSpec((1, L, 2 * C), lambda i: (i, 0, 0)),
                  pl.BlockSpec((CONF_KERNEL, C), lambda i: (0, 0)), vec, vec, vec],
        out_specs=pl.BlockSpec((1, L, C), lambda i: (i, 0, 0)),
        out_shape=jax.ShapeDtypeStruct((B, L, C), F32),
        scratch_shapes=[pltpu.VMEM((C // LANES, L + 2 * CONV_MARGIN, LANES), F32)],
        compiler_params=_cparams("parallel"),
        name="conformer_conv",
    )(u, w, b.reshape(1, C), ln_g.reshape(1, C), ln_b.reshape(1, C))


def _gelu_tanh(x):
    return 0.5 * x * (1.0 + jnp.tanh(math.sqrt(2.0 / math.pi) * (x + 0.044715 * (x * x * x))))


def _lru_kernel(uc_ref, ux_ref, cw_ref, cb_ref, wcat_ref, bcat_ref, lam_ref, *rest, need_ctx):
    if need_ctx:
        oc_ref, ox_ref, cpad, xpad, a_s, b_s, yx, yc = rest
    else:
        ox_ref, cpad, xpad, a_s, b_s, yx = rest
        oc_ref = yc = None
    C = D_GROUP
    n_slabs = C // LANES
    Lc = uc_ref.shape[1]
    Lx = ux_ref.shape[1]
    pad_l = (LRU_CONV - 1) // 2

    def fill(pad_ref, u_ref, L):
        T = _time_chunk(L)
        _zero_margins(pad_ref, L)

        def body(j, carry):
            base = pl.multiple_of(j * T, T)
            for s in range(n_slabs):
                pad_ref[s, pl.ds(CONV_MARGIN + base, T), :] = u_ref[0, pl.ds(base, T),
                                                                    C + s * LANES:C + (s + 1) * LANES]
            return carry
        lax.fori_loop(0, L // T, body, 0)

    fill(cpad, uc_ref, Lc)
    fill(xpad, ux_ref, Lx)

    def coeffs(pad_ref, base, T, d):
        x = jnp.concatenate([_dw_conv_slab(pad_ref, s, base, T, cw_ref, cb_ref, s * LANES, LRU_CONV, pad_l)
                             for s in range(n_slabs)], axis=-1)
        t = jnp.tanh(jnp.dot(x.astype(BF16), wcat_ref[:, 2 * d * C:2 * (d + 1) * C],
                             preferred_element_type=F32) + bcat_ref[:, 2 * d * C:2 * (d + 1) * C])
        i = 0.5 * t[:, C:] + 0.5
        z = -lam_ref[d:d + 1, :]
        softplus = jnp.maximum(z, 0.0) + jnp.log(1.0 + jnp.exp(-jnp.abs(z)))
        half_rate = (-0.5 * LRU_C) * softplus
        a = jnp.exp(half_rate * t[:, :C] + half_rate)
        b = jnp.sqrt(1.0 - a * a) * (i * x)
        for s in range(n_slabs):
            a_s[d * n_slabs + s, pl.ds(0, T), :] = a[:, s * LANES:(s + 1) * LANES]
            b_s[d * n_slabs + s, pl.ds(0, T), :] = b[:, s * LANES:(s + 1) * LANES]

    def run(pad_ref, L, h, y_ref):
        T = _time_chunk(L)
        n = L // T

        def chunk(j, h):
            base_f = pl.multiple_of(j * T, T)
            base_b = pl.multiple_of((n - 1 - j) * T, T)
            coeffs(pad_ref, base_f, T, 0)
            coeffs(pad_ref, base_b, T, 1)

            def step(t, h):
                new = []
                for d, (base, row) in enumerate(((base_f, t), (base_b, T - 1 - t))):
                    for s in range(n_slabs):
                        k = d * n_slabs + s
                        hs = a_s[k, pl.ds(row, 1), :] * h[k] + b_s[k, pl.ds(row, 1), :]
                        if y_ref is not None:
                            y_ref[k, pl.ds(base + row, 1), :] = hs
                        new.append(hs)
                return tuple(new)
            return lax.fori_loop(0, T, step, h, unroll=8)
        return lax.fori_loop(0, n, chunk, h)

    h = tuple(jnp.zeros((1, LANES), F32) for _ in range(2 * n_slabs))
    h = run(cpad, Lc, h, yc)
    run(xpad, Lx, h, yx)

    def finish(u_ref, y_ref, o_ref, L):
        T = _time_chunk(L)

        def body(j, carry):
            base = pl.multiple_of(j * T, T)
            y = jnp.concatenate([y_ref[s, pl.ds(base, T), :] + y_ref[n_slabs + s, pl.ds(base, T), :]
                                 for s in range(n_slabs)], axis=-1)
            o_ref[0, pl.ds(base, T), :] = _gelu_tanh(u_ref[0, pl.ds(base, T), :C]) * y
            return carry
        lax.fori_loop(0, L // T, body, 0)

    finish(ux_ref, yx, ox_ref, Lx)
    if need_ctx:
        finish(uc_ref, yc, oc_ref, Lc)


def _block_diag(w):
    H, n, _ = w.shape
    eye = jnp.eye(H, dtype=w.dtype)
    return (eye[:, None, :, None] * w[:, :, None, :]).reshape(H * n, H * n)


def rglru_mixer(uc, ux, lp, need_ctx):
    B, Lc, _ = uc.shape
    Lx = ux.shape[1]
    C = D_GROUP
    wcat = (0.5 * jnp.concatenate([_block_diag(lp["lru_wa"][0]), _block_diag(lp["lru_wx"][0]),
                                   _block_diag(lp["lru_wa"][1]), _block_diag(lp["lru_wx"][1])], axis=1)).astype(BF16)
    bcat = 0.5 * jnp.concatenate([lp["lru_ba"][0], lp["lru_bx"][0], lp["lru_ba"][1], lp["lru_bx"][1]]).reshape(1, 4 * C)
    full = lambda r, c: pl.BlockSpec((r, c), lambda i: (0, 0))
    seq = lambda L, n: pl.BlockSpec((1, L, n), lambda i: (i, 0, 0))
    out_specs = [seq(Lx, C)]
    out_shape = [jax.ShapeDtypeStruct((B, Lx, C), F32)]
    if need_ctx:
        out_specs = [seq(Lc, C)] + out_specs
        out_shape = [jax.ShapeDtypeStruct((B, Lc, C), F32)] + out_shape
    T = _time_chunk(Lx)
    slab = lambda rows, n=1: pltpu.VMEM((n * C // LANES, rows, LANES), F32)
    scratch = [slab(Lc + 2 * CONV_MARGIN), slab(Lx + 2 * CONV_MARGIN), slab(T, 2), slab(T, 2), slab(Lx, 2)]
    if need_ctx:
        scratch.append(slab(Lc, 2))
    res = pl.pallas_call(
        functools.partial(_lru_kernel, need_ctx=need_ctx),
        grid=(B,),
        in_specs=[seq(Lc, 2 * C), seq(Lx, 2 * C), full(LRU_CONV, C), full(1, C), full(C, 4 * C),
                  full(1, 4 * C), full(2, C)],
        out_specs=out_specs,
        out_shape=out_shape,
        scratch_shapes=scratch,
        compiler_params=_cparams("parallel"),
        name="rglru",
    )(uc, ux, lp["lru_conv_w"], lp["lru_conv_b"].reshape(1, C), wcat, bcat, lp["lru_lambda"])
    if need_ctx:
        return res[0], res[1]
    return None, res[0]


HY_SHORT = 3


def _short_conv(pad_ref, base, T, w_ref, b_ref, c0, c1):
    return jnp.concatenate([_dw_conv_slab(pad_ref, col // LANES, base, T, w_ref, b_ref, col, HY_SHORT, 1)
                            for col in range(c0, c1, LANES)], axis=-1)


def _fill_padded(pad_ref, u_ref, L, T):
    _zero_margins(pad_ref, L)

    def body(j, carry):
        base = pl.multiple_of(j * T, T)
        for s in range(pad_ref.shape[0]):
            pad_ref[s, pl.ds(CONV_MARGIN + base, T), :] = u_ref[0, pl.ds(base, T), s * LANES:(s + 1) * LANES]
        return carry
    lax.fori_loop(0, L // T, body, 0)


def _hyena_pre_kernel(u_ref, w_ref, b_ref, z_ref, upad):
    L = u_ref.shape[1]
    T = _time_chunk(L)
    C = D_GROUP
    _fill_padded(upad, u_ref, L, T)

    def body(j, carry):
        base = pl.multiple_of(j * T, T)
        x1 = _short_conv(upad, base, T, w_ref, b_ref, C, 2 * C)
        v = _short_conv(upad, base, T, w_ref, b_ref, 2 * C, 3 * C)
        z_ref[pl.ds(base, T), :] = (x1 * v).astype(BF16)
        return carry
    lax.fori_loop(0, L // T, body, 0)


def _hyena_post_kernel(u_ref, y_ref, w_ref, b_ref, bias_ref, o_ref, upad):
    L = u_ref.shape[1]
    T = _time_chunk(L)
    C = D_GROUP
    _fill_padded(upad, u_ref, L, T)

    def body(j, carry):
        base = pl.multiple_of(j * T, T)
        x0 = _short_conv(upad, base, T, w_ref, b_ref, 0, C)
        x1 = _short_conv(upad, base, T, w_ref, b_ref, C, 2 * C)
        v = _short_conv(upad, base, T, w_ref, b_ref, 2 * C, 3 * C)
        o_ref[0, pl.ds(base, T), :] = x0 * (y_ref[pl.ds(base, T), :] + (x1 * v) * bias_ref[...])
        return carry
    lax.fori_loop(0, L // T, body, 0)


def _spectrum_kernel(f_ref, z_ref, ha_ref, hb_ref, hc_ref, y_ref):
    tf = ha_ref.shape[0]
    acc = jnp.dot(f_ref[...], z_ref[...], preferred_element_type=F32)
    zr = acc[:tf]
    zi = acc[tf:]
    y_ref[:tf, :] = (zr * ha_ref[...] - zi * hb_ref[...]).astype(BF16)
    y_ref[tf:, :] = (zr * hb_ref[...] + zi * hc_ref[...]).astype(BF16)


def _idft_kernel(f_ref, y_ref, o_ref):
    o_ref[...] = jnp.dot(f_ref[...], y_ref[...], preferred_element_type=F32)


def dft_tables(L):
    N = 2 * L
    tf = min(256, L)
    k = jnp.arange(L, dtype=jnp.int32)
    n = jnp.arange(L, dtype=jnp.int32)
    ang = (2.0 * math.pi / N) * ((k[:, None] * n[None, :]) % N).astype(F32)
    cos = jnp.cos(ang)
    sin = jnp.sin(ang)
    nyq = jnp.where(n % 2 == 0, 1.0, -1.0).astype(F32)
    f_re = cos
    f_im = (-sin).at[0].set(nyq)
    fwd = jnp.stack([f_re.reshape(L // tf, tf, L), f_im.reshape(L // tf, tf, L)], axis=1).reshape(N, L)
    ck = jnp.where(k == 0, 1.0, 2.0).astype(F32)[:, None] / N
    i_re = cos * ck
    i_im = (-sin * ck).at[0].set(nyq / N)
    inv = jnp.stack([i_re.reshape(L // tf, tf, L), i_im.reshape(L // tf, tf, L)], axis=1).reshape(N, L).T
    return fwd.astype(BF16), inv.astype(BF16)


def filter_spectrum(h_fwd, h_bwd):
    L, C = h_fwd.shape
    k = jnp.concatenate([h_fwd, jnp.zeros((1, C), F32), h_bwd[1:][::-1]], axis=0)
    hf = jnp.fft.rfft(k, axis=0)
    hr = jnp.real(hf)
    hi = jnp.imag(hf)
    a = hr[:L]
    b = hi[:L].at[0].set(0.0)
    c = hr[:L].at[0].set(hr[L])
    return a, b, c


def hyena_mixer(u, lp, tables):
    B, L, _ = u.shape
    C = D_GROUP
    N = 2 * L
    fwd, inv = tables
    tf = min(256, L)
    T = _time_chunk(L)
    w, bsh = lp["hy_short_w"], lp["hy_short_b"].reshape(1, 3 * C)
    z2 = pl.pallas_call(
        _hyena_pre_kernel,
        grid=(B,),
        in_specs=[pl.BlockSpec((1, L, 3 * C), lambda b: (b, 0, 0)),
                  pl.BlockSpec((HY_SHORT, 3 * C), lambda b: (0, 0)),
                  pl.BlockSpec((1, 3 * C), lambda b: (0, 0))],
        out_specs=pl.BlockSpec((L, C), lambda b: (0, b)),
        out_shape=jax.ShapeDtypeStruct((L, B * C), BF16),
        scratch_shapes=[pltpu.VMEM((3 * C // LANES, L + 2 * CONV_MARGIN, LANES), F32)],
        compiler_params=_cparams("parallel"),
        name="hyena_pre",
    )(u, w, bsh)

    h_fwd, h_bwd = _hyena_filters(L, lp)
    tn = 2 * C
    ha, hb, hc = [jnp.tile(t, (1, tn // C)) for t in filter_spectrum(h_fwd, h_bwd)]
    hspec = pl.BlockSpec((tf, tn), lambda i, j: (i, 0))
    y2 = pl.pallas_call(
        _spectrum_kernel,
        grid=(L // tf, B * C // tn),
        in_specs=[pl.BlockSpec((2 * tf, L), lambda i, j: (i, 0)),
                  pl.BlockSpec((L, tn), lambda i, j: (0, j)), hspec, hspec, hspec],
        out_specs=pl.BlockSpec((2 * tf, tn), lambda i, j: (i, j)),
        out_shape=jax.ShapeDtypeStruct((N, B * C), BF16),
        compiler_params=_cparams("parallel", "parallel"),
        name="hyena_spectrum",
    )(fwd, z2, ha, hb, hc)

    tl = min(256, L)
    yt = pl.pallas_call(
        _idft_kernel,
        grid=(L // tl, B * C // tn),
        in_specs=[pl.BlockSpec((tl, N), lambda i, j: (i, 0)),
                  pl.BlockSpec((N, tn), lambda i, j: (0, j))],
        out_specs=pl.BlockSpec((tl, tn), lambda i, j: (i, j)),
        out_shape=jax.ShapeDtypeStruct((L, B * C), F32),
        compiler_params=_cparams("parallel", "parallel"),
        name="hyena_idft",
    )(inv, y2)

    return pl.pallas_call(
        _hyena_post_kernel,
        grid=(B,),
        in_specs=[pl.BlockSpec((1, L, 3 * C), lambda b: (b, 0, 0)),
                  pl.BlockSpec((L, C), lambda b: (0, b)),
                  pl.BlockSpec((HY_SHORT, 3 * C), lambda b: (0, 0)),
                  pl.BlockSpec((1, 3 * C), lambda b: (0, 0)),
                  pl.BlockSpec((1, C), lambda b: (0, 0))],
        out_specs=pl.BlockSpec((1, L, C), lambda b: (b, 0, 0)),
        out_shape=jax.ShapeDtypeStruct((B, L, C), F32),
        scratch_shapes=[pltpu.VMEM((3 * C // LANES, L + 2 * CONV_MARGIN, LANES), F32)],
        compiler_params=_cparams("parallel"),
        name="hyena_post",
    )(u, yt, w, bsh, lp["hy_bias"].reshape(1, C))


FFT_N2 = 128
FFT_UNROLL = 8


class _FftPlan:
    def __init__(self, L):
        self.L = L
        self.N = 2 * L
        self.N1 = self.N // FFT_N2
        self.KH = self.N1 // 2 + 1
        self.KP = -(-self.KH // 8) * 8
        self.PA = 2 * self.KP + 4


def fft_tables(L):
    p = _FftPlan(L)
    N, N1, KH, KP = p.N, p.N1, p.KH, p.KP
    n2 = jnp.arange(FFT_N2, dtype=jnp.int32)
    k1 = jnp.arange(KP, dtype=jnp.int32)
    n1 = jnp.arange(N1, dtype=jnp.int32)
    n = FFT_N2 * n1[None, None, :] + n2[:, None, None]
    ang = (2.0 * math.pi / N) * ((k1[None, :, None] * n) % N).astype(F32)
    keep = (k1 < KH)[None, :, None]
    g_re = jnp.where(keep, jnp.cos(ang), 0.0)
    g_im = jnp.where(keep, -jnp.sin(ang), 0.0)
    ga_full = jnp.concatenate([g_re, g_im], axis=1)
    ck = jnp.where((k1 == 0) | (k1 == N1 // 2), 1.0, 2.0) / N
    ga_inv = jnp.swapaxes(ga_full[:, :, :N1 // 2] * jnp.tile(ck, 2)[None, :, None], 1, 2)
    kk = jnp.arange(FFT_N2, dtype=jnp.int32)
    ang2 = (2.0 * math.pi / FFT_N2) * ((kk[:, None] * kk[None, :]) % FFT_N2).astype(F32)
    fr, fi = jnp.cos(ang2), -jnp.sin(ang2)
    fb = jnp.block([[fr, -fi], [fi, fr]])
    fb_inv = jnp.block([[fr, fi], [-fi, fr]])
    return dict(ga_half=ga_full[:, :, :N1 // 2].astype(BF16), ga_full=ga_full.astype(BF16),
                ga_inv=ga_inv.astype(BF16), fb=fb.astype(BF16), fb_inv=fb_inv.astype(BF16))


def _fft_stage_a(x_ref, ga_ref, s_ref, plan, n1_count):
    n_slabs = x_ref.shape[0]

    def body(n2, carry):
        xs = jnp.concatenate([x_ref[s, pl.ds(n2, n1_count, stride=FFT_N2), :] for s in range(n_slabs)], axis=-1)
        a = jnp.dot(ga_ref[n2], xs.astype(BF16), preferred_element_type=F32)
        for s in range(n_slabs):
            s_ref[s, pl.ds(n2 * plan.PA, 2 * plan.KP), :] = a[:, s * LANES:(s + 1) * LANES]
        return carry
    lax.fori_loop(0, FFT_N2, body, 0, unroll=FFT_UNROLL)


def _fft_load_k1(s_ref, k1, plan):
    n_slabs = s_ref.shape[0]
    re = jnp.concatenate([s_ref[s, pl.ds(k1, FFT_N2, stride=plan.PA), :] for s in range(n_slabs)], axis=-1)
    im = jnp.concatenate([s_ref[s, pl.ds(plan.KP + k1, FFT_N2, stride=plan.PA), :] for s in range(n_slabs)], axis=-1)
    return jnp.concatenate([re, im], axis=0).astype(BF16)


def _fft_filter_kernel(k_ref, ga_ref, fb_ref, h_ref, s_ref, *, plan):
    _fft_stage_a(k_ref, ga_ref, s_ref, plan, plan.N1)

    def body(k1, carry):
        h_ref[k1] = jnp.dot(fb_ref[...], _fft_load_k1(s_ref, k1, plan), preferred_element_type=F32).astype(BF16)
        return carry
    lax.fori_loop(0, plan.KH, body, 0)


def _fft_conv_kernel(z_ref, ga_ref, gi_ref, fb_ref, fbi_ref, h_ref, y_ref, s_ref, *, plan):
    zs = z_ref.at[0]
    ys = y_ref.at[0]
    n_slabs = zs.shape[0]
    half = FFT_N2
    _fft_stage_a(zs, ga_ref, s_ref, plan, plan.N1 // 2)

    def body_b(k1, carry):
        x = jnp.dot(fb_ref[...], _fft_load_k1(s_ref, k1, plan), preferred_element_type=F32)
        h = h_ref[k1].astype(F32)
        xr, xi, hr, hi = x[:half], x[half:], h[:half], h[half:]
        y = jnp.concatenate([xr * hr - xi * hi, xr * hi + xi * hr], axis=0).astype(BF16)
        b = jnp.dot(fbi_ref[...], y, preferred_element_type=F32)
        for s in range(n_slabs):
            s_ref[s, pl.ds(k1, FFT_N2, stride=plan.PA), :] = b[:half, s * LANES:(s + 1) * LANES]
            s_ref[s, pl.ds(plan.KP + k1, FFT_N2, stride=plan.PA), :] = b[half:, s * LANES:(s + 1) * LANES]
        return carry
    lax.fori_loop(0, plan.KH, body_b, 0, unroll=3)

    def body_a(n2, carry):
        b = jnp.concatenate([s_ref[s, pl.ds(n2 * plan.PA, 2 * plan.KP), :] for s in range(n_slabs)], axis=-1)
        y = jnp.dot(gi_ref[n2], b.astype(BF16), preferred_element_type=F32)
        for s in range(n_slabs):
            ys[s, pl.ds(n2, plan.N1 // 2, stride=FFT_N2), :] = y[:, s * LANES:(s + 1) * LANES]
        return carry
    lax.fori_loop(0, FFT_N2, body_a, 0, unroll=FFT_UNROLL)


def fft_filter_spectrum(h_fwd, h_bwd, tabs):
    L, C = h_fwd.shape
    plan = _FftPlan(L)
    n_slabs = C // LANES
    k = jnp.concatenate([h_fwd, jnp.zeros((1, C), F32), h_bwd[1:][::-1]], axis=0)
    k = k.reshape(plan.N, n_slabs, LANES).transpose(1, 0, 2)
    full = lambda shape: pl.BlockSpec(shape, lambda i: (0,) * len(shape))
    return pl.pallas_call(
        functools.partial(_fft_filter_kernel, plan=plan),
        grid=(1,),
        in_specs=[full((n_slabs, plan.N, LANES)), full((FFT_N2, 2 * plan.KP, plan.N1)),
                  full((2 * FFT_N2, 2 * FFT_N2))],
        out_specs=full((plan.KH, 2 * FFT_N2, C)),
        out_shape=jax.ShapeDtypeStruct((plan.KH, 2 * FFT_N2, C), BF16),
        scratch_shapes=[pltpu.VMEM((n_slabs, FFT_N2 * plan.PA, LANES), F32)],
        compiler_params=_cparams("arbitrary"),
        name="hyena_filter_fft",
    )(k, tabs["ga_full"], tabs["fb"])


def fft_long_conv(z, h_spec, tabs):
    B, n_slabs, L, _ = z.shape
    plan = _FftPlan(L)
    C = n_slabs * LANES
    full = lambda shape: pl.BlockSpec(shape, lambda b: (0,) * len(shape))
    seq = pl.BlockSpec((1, n_slabs, L, LANES), lambda b: (b, 0, 0, 0))
    return pl.pallas_call(
        functools.partial(_fft_conv_kernel, plan=plan),
        grid=(B,),
        in_specs=[seq, full((FFT_N2, 2 * plan.KP, plan.N1 // 2)), full((FFT_N2, plan.N1 // 2, 2 * plan.KP)),
                  full((2 * FFT_N2, 2 * FFT_N2)), full((2 * FFT_N2, 2 * FFT_N2)),
                  full((plan.KH, 2 * FFT_N2, C))],
        out_specs=seq,
        out_shape=jax.ShapeDtypeStruct((B, n_slabs, L, LANES), F32),
        scratch_shapes=[pltpu.VMEM((n_slabs, FFT_N2 * plan.PA, LANES), F32)],
        compiler_params=_cparams("parallel"),
        name="hyena_fft_conv",
    )(z, tabs["ga_half"], tabs["ga_inv"], tabs["fb"], tabs["fb_inv"], h_spec)


def _hyena_pre_slab_kernel(u_ref, w_ref, b_ref, z_ref, upad):
    L = u_ref.shape[1]
    T = _time_chunk(L)
    C = D_GROUP
    _fill_padded(upad, u_ref, L, T)

    def body(j, carry):
        base = pl.multiple_of(j * T, T)
        for s in range(C // LANES):
            x1 = _dw_conv_slab(upad, C // LANES + s, base, T, w_ref, b_ref, C + s * LANES, HY_SHORT, 1)
            v = _dw_conv_slab(upad, 2 * C // LANES + s, base, T, w_ref, b_ref, 2 * C + s * LANES, HY_SHORT, 1)
            z_ref[0, s, pl.ds(base, T), :] = x1 * v
        return carry
    lax.fori_loop(0, L // T, body, 0)


def _hyena_post_slab_kernel(u0_ref, z_ref, y_ref, w_ref, b_ref, bias_ref, o_ref, upad):
    L = u0_ref.shape[1]
    T = _time_chunk(L)
    C = D_GROUP
    _fill_padded(upad, u0_ref, L, T)

    def body(j, carry):
        base = pl.multiple_of(j * T, T)
        x0 = _short_conv(upad, base, T, w_ref, b_ref, 0, C)
        z = jnp.concatenate([z_ref[0, s, pl.ds(base, T), :] for s in range(C // LANES)], axis=-1)
        y = jnp.concatenate([y_ref[0, s, pl.ds(base, T), :] for s in range(C // LANES)], axis=-1)
        o_ref[0, pl.ds(base, T), :] = x0 * (y + z * bias_ref[...])
        return carry
    lax.fori_loop(0, L // T, body, 0)


def hyena_mixer_fft(u, lp, tabs):
    B, L, _ = u.shape
    C = D_GROUP
    n_slabs = C // LANES
    w, bsh = lp["hy_short_w"], lp["hy_short_b"].reshape(1, 3 * C)
    useq = pl.BlockSpec((1, L, 3 * C), lambda b: (b, 0, 0))
    slabs = pl.BlockSpec((1, n_slabs, L, LANES), lambda b: (b, 0, 0, 0))
    wspec = pl.BlockSpec((HY_SHORT, 3 * C), lambda b: (0, 0))
    bspec = pl.BlockSpec((1, 3 * C), lambda b: (0, 0))
    pad_scratch = pltpu.VMEM((3 * C // LANES, L + 2 * CONV_MARGIN, LANES), F32)
    z = pl.pallas_call(
        _hyena_pre_slab_kernel,
        grid=(B,),
        in_specs=[useq, wspec, bspec],
        out_specs=slabs,
        out_shape=jax.ShapeDtypeStruct((B, n_slabs, L, LANES), F32),
        scratch_shapes=[pad_scratch],
        compiler_params=_cparams("parallel"),
        name="hyena_pre",
    )(u, w, bsh)
    h_fwd, h_bwd = _hyena_filters(L, lp)
    y = fft_long_conv(z, fft_filter_spectrum(h_fwd, h_bwd, tabs), tabs)
    return pl.pallas_call(
        _hyena_post_slab_kernel,
        grid=(B,),
        in_specs=[pl.BlockSpec((1, L, C), lambda b: (b, 0, 0)), slabs, slabs, wspec, bspec,
                  pl.BlockSpec((1, C), lambda b: (0, 0))],
        out_specs=pl.BlockSpec((1, L, C), lambda b: (b, 0, 0)),
        out_shape=jax.ShapeDtypeStruct((B, L, C), F32),
        scratch_shapes=[pltpu.VMEM((n_slabs, L + 2 * CONV_MARGIN, LANES), F32)],
        compiler_params=_cparams("parallel"),
        name="hyena_post",
    )(u, z, y, w, bsh, lp["hy_bias"].reshape(1, C))


def _hyena_filters(L, lp):
    t = jnp.linspace(0.0, 1.0, L, dtype=F32)[:, None]
    bands = (HY_EMB - 1) // 2
    w = 2.0 * math.pi * jnp.arange(L, dtype=F32)[:, None] / L
    f = jnp.linspace(1e-4, bands - 1, bands, dtype=F32)[None]
    z = jnp.concatenate([t, jnp.cos(f * w), -jnp.sin(f * w)], axis=-1)
    hdn = jnp.sin(z @ lp["hy_ffn_w1"] + lp["hy_ffn_b1"])
    hdn = jnp.sin(hdn @ lp["hy_ffn_w2"] + lp["hy_ffn_b2"])
    h = (hdn @ lp["hy_ffn_w3"]).reshape(L, 2, D_GROUP)
    max_decay = math.log(HY_TARGET) / HY_FAST_DECAY
    min_decay = math.log(HY_TARGET) / HY_SLOW_DECAY
    deltas = jnp.linspace(min_decay, max_decay, D_GROUP, dtype=F32)
    h = h * jnp.exp(-t * jnp.abs(deltas))[:, None, :]
    h = h / (jnp.sum(jnp.abs(h), axis=(0, 1), keepdims=True) + EPS)
    return h[:, 0], h[:, 1]


def _layer(hc, hx, c_silu_all, lp, need_ctx, final_g, final_norm, tables_x, tables_c, experts, layer):
    B, S, D = hx.shape
    C = hc.shape[1]
    mod = small_linear(c_silu_all, lp["ada_w"], lp["ada_b"])
    mod_x = mod[:B].reshape(B, 6, 1, D)
    mod_c = jnp.broadcast_to(mod[B].reshape(1, 6, 1, D), (B, 6, 1, D))
    w_ext = extend_w_in(lp["w_in"])
    cos_x, sin_x = rope_tables(S, True)
    cos_c, sin_c = rope_tables(C, False)
    hy_x, cf_x, at_x, lr_x = in_proj(hx, mod_x[:, 0], mod_x[:, 1], lp["norm1_g"], w_ext, cos_x, sin_x, tm=512)
    hy_c, cf_c, at_c, lr_c = in_proj(hc, mod_c[:, 0], mod_c[:, 1], lp["norm1_g"], w_ext, cos_c, sin_c, tm=256)

    yd_c, yd_x = rglru_mixer(lr_c, lr_x, lp, need_ctx)
    conf = lambda u: conformer_conv(u, lp["conf_dw_w"], lp["conf_dw_b"], lp["conf_ln_g"], lp["conf_ln_b"])
    ys_x = [hyena_mixer_fft(hy_x, lp, tables_x), conf(cf_x),
            window_attention(at_x, at_c, lp["attn_sink"]), yd_x]

    w_out = lp["w_out"].astype(BF16)
    w_router = jnp.zeros((D, ROUTER_COLS), F32)
    w_router = w_router.at[:, :N_GROUPS].set(lp["router_g_w"]).at[:, N_GROUPS:N_GROUPS + N_EXPERTS].set(lp["router_e_w"])
    w_router = w_router.astype(BF16)
    b_router = jnp.zeros((1, ROUTER_COLS), F32)
    b_router = b_router.at[0, :N_GROUPS].set(lp["router_g_b"]).at[0, N_GROUPS:N_GROUPS + N_EXPERTS].set(lp["router_e_b"])

    hx1, lg_x = out_proj(ys_x, hx, mod_x[:, 2], lp["group_norm_g"], w_out, lp["norm2_g"],
                         mod_x[:, 3], mod_x[:, 4], w_router, b_router, tm=512)
    h_tok = hx1.reshape(B * S, D)
    hc_tok = None
    lg = lg_x.reshape(B * S, ROUTER_COLS)
    if need_ctx:
        ys_c = [hyena_mixer(hy_c, lp, tables_c), conf(cf_c),
                context_attention(at_c, lp["attn_sink"]), yd_c]
        hc1, lg_c = out_proj(ys_c, hc, mod_c[:, 2], lp["group_norm_g"], w_out, lp["norm2_g"],
                             mod_c[:, 3], mod_c[:, 4], w_router, b_router, tm=256)
        hc_tok = hc1.reshape(B * C, D)
        lg = jnp.concatenate([lg, lg_c.reshape(B * C, ROUTER_COLS)], axis=0)

    T = lg.shape[0]
    n_blocks = -(-T // MOE_BLOCK) + N_CLASSES
    info, counts, ids = route_tokens(lg)
    dest, blk_a, blk_b, n_used = slot_plan(ids, counts, n_blocks)
    xs = moe_dispatch(h_tok, hc_tok, dest, n_blocks, lp["norm2_g"], (mod_x[:, 3], mod_x[:, 4]),
                      (mod_c[:, 3], mod_c[:, 4]), S)
    o_sorted = expert_pairs(xs, blk_a, blk_b, n_used, *experts, layer)
    hx2 = moe_collect(o_sorted, dest, info, 0, h_tok, mod_x[:, 5], S, final_g, final_norm)
    hx2 = hx2.reshape(B, S, D)
    if need_ctx:
        hc2 = moe_collect(o_sorted, dest, info, B * S // DISPATCH_TOKENS, hc_tok, mod_c[:, 5], C,
                          final_g, False).reshape(B, C, D)
    else:
        hc2 = hc
    return hc2, hx2


def kernel(x, c, ctx, c_ctx, norm1_g, norm2_g, ada_w, ada_b, w_in, hy_short_w, hy_short_b, hy_ffn_w1, hy_ffn_b1, hy_ffn_w2, hy_ffn_b2, hy_ffn_w3, hy_bias, conf_dw_w, conf_dw_b, conf_ln_g, conf_ln_b, attn_sink, lru_conv_w, lru_conv_b, lru_wa, lru_ba, lru_wx, lru_bx, lru_lambda, group_norm_g, w_out, router_g_w, router_g_b, router_e_w, router_e_b, exp_w_gate, exp_w_up, exp_w_down, final_norm_g):
    stacked = dict(norm1_g=norm1_g, norm2_g=norm2_g, ada_w=ada_w, ada_b=ada_b, w_in=w_in,
                   hy_short_w=hy_short_w, hy_short_b=hy_short_b, hy_ffn_w1=hy_ffn_w1, hy_ffn_b1=hy_ffn_b1,
                   hy_ffn_w2=hy_ffn_w2, hy_ffn_b2=hy_ffn_b2, hy_ffn_w3=hy_ffn_w3, hy_bias=hy_bias,
                   conf_dw_w=conf_dw_w, conf_dw_b=conf_dw_b, conf_ln_g=conf_ln_g, conf_ln_b=conf_ln_b,
                   attn_sink=attn_sink, lru_conv_w=lru_conv_w, lru_conv_b=lru_conv_b, lru_wa=lru_wa,
                   lru_ba=lru_ba, lru_wx=lru_wx, lru_bx=lru_bx, lru_lambda=lru_lambda,
                   group_norm_g=group_norm_g, w_out=w_out, router_g_w=router_g_w, router_g_b=router_g_b,
                   router_e_w=router_e_w, router_e_b=router_e_b)
    experts = (exp_w_gate.astype(BF16), exp_w_up.astype(BF16), exp_w_down.astype(BF16))
    depth = norm1_g.shape[0]
    B = x.shape[0]
    cs = jnp.concatenate([jax.nn.silu(c), jnp.broadcast_to(jax.nn.silu(c_ctx)[None], (8, c.shape[1]))], axis=0)
    hc, hx = ctx, x
    tables_x = fft_tables(x.shape[1])
    tables_c = dft_tables(ctx.shape[1])
    for l in range(depth):
        lp = {k: v[l] for k, v in stacked.items()}
        hc, hx = _layer(hc, hx, cs, lp, need_ctx=(l < depth - 1), final_g=final_norm_g,
                        final_norm=(l == depth - 1), tables_x=tables_x, tables_c=tables_c,
                        experts=experts, layer=l)
    return hx
```

```python
import functools
import math

import jax
import jax.numpy as jnp
from jax import lax
from jax.experimental import pallas as pl
from jax.experimental.pallas import tpu as pltpu

F32 = jnp.float32
BF16 = jnp.bfloat16

EPS = 1e-6
NEG_INF = -1e30
GRID_W = 64
N_MIXERS = 4
D_GROUP = 256
HY_COLS = 3 * D_GROUP
CONF_COLS = 2 * D_GROUP
ATT_HEADS = 4
ATT_KV_HEADS = 2
HEAD_DIM = 64
ATT_COLS = (ATT_HEADS + 2 * ATT_KV_HEADS) * HEAD_DIM
LRU_COLS = 2 * D_GROUP
QK_COLS = (ATT_HEADS + ATT_KV_HEADS) * HEAD_DIM
WINDOW = 128
ATT_BLOCK = 128
ROPE_BASE = 10000.0
HY_EMB = 33
HY_FAST_DECAY = 0.3
HY_SLOW_DECAY = 1.5
HY_TARGET = 1e-2
CONF_KERNEL = 31
LRU_HEADS = 4
LRU_CONV = 4
LRU_C = 8.0
N_GROUPS = 4
EXP_PER_GROUP = 8
N_EXPERTS = N_GROUPS * EXP_PER_GROUP
TOP_K = 2
MOE_BLOCK = 256
ROUTER_COLS = 128

VMEM_LIMIT_BYTES = 56 * 1024 * 1024


def _cparams(*sem):
    return pltpu.CompilerParams(dimension_semantics=sem, vmem_limit_bytes=VMEM_LIMIT_BYTES)


def _linear_kernel(x_ref, w_ref, b_ref, o_ref):
    o_ref[...] = jnp.dot(x_ref[...], w_ref[...], preferred_element_type=F32,
                         precision=lax.Precision.HIGHEST) + b_ref[...]


def small_linear(x, w, b, tn=1024):
    M, K = x.shape
    N = w.shape[1]
    return pl.pallas_call(
        _linear_kernel,
        grid=(N // tn,),
        in_specs=[pl.BlockSpec((M, K), lambda j: (0, 0)),
                  pl.BlockSpec((K, tn), lambda j: (0, j)),
                  pl.BlockSpec((1, tn), lambda j: (0, j))],
        out_specs=pl.BlockSpec((M, tn), lambda j: (0, j)),
        out_shape=jax.ShapeDtypeStruct((M, N), F32),
        compiler_params=_cparams("parallel"),
        name="ada_linear",
    )(x, w, b.reshape(1, N))


def _in_proj_kernel(x_ref, sh_ref, sc_ref, g_ref, w_ref, cos_ref, sin_ref,
                    hy_ref, cf_ref, at_ref, lr_ref):
    x = x_ref[0]
    ms = jnp.mean(x * x, axis=-1, keepdims=True)
    y = x * lax.rsqrt(ms + EPS) * g_ref[...]
    y = y * (1.0 + sc_ref[0]) + sh_ref[0]
    u = jnp.dot(y.astype(BF16), w_ref[...], preferred_element_type=F32)
    c0 = HY_COLS
    c1 = c0 + CONF_COLS
    c2 = c1 + ATT_COLS
    c3 = c2 + LRU_COLS
    hy_ref[0] = u[:, :c0]
    cf_ref[0] = u[:, c0:c1]
    lr_ref[0] = u[:, c2:c3]
    qk = u[:, c1:c1 + QK_COLS]
    qk_rot = u[:, c3:c3 + QK_COLS]
    at_ref[0, :, :QK_COLS] = qk * cos_ref[...] + qk_rot * sin_ref[...]
    at_ref[0, :, QK_COLS:] = u[:, c1 + QK_COLS:c2]


def in_proj(h, shift, scale, g, w_ext, cos_t, sin_t, tm):
    B, L, D = h.shape
    NW = w_ext.shape[1]
    outs = [HY_COLS, CONF_COLS, ATT_COLS, LRU_COLS]
    return pl.pallas_call(
        _in_proj_kernel,
        grid=(B, L // tm),
        in_specs=[pl.BlockSpec((1, tm, D), lambda b, i: (b, i, 0)),
                  pl.BlockSpec((1, 1, D), lambda b, i: (b, 0, 0)),
                  pl.BlockSpec((1, 1, D), lambda b, i: (b, 0, 0)),
                  pl.BlockSpec((1, D), lambda b, i: (0, 0)),
                  pl.BlockSpec((D, NW), lambda b, i: (0, 0)),
                  pl.BlockSpec((tm, QK_COLS), lambda b, i: (i, 0)),
                  pl.BlockSpec((tm, QK_COLS), lambda b, i: (i, 0))],
        out_specs=[pl.BlockSpec((1, tm, n), lambda b, i: (b, i, 0)) for n in outs],
        out_shape=[jax.ShapeDtypeStruct((B, L, n), F32) for n in outs],
        compiler_params=_cparams("parallel", "parallel"),
        name="in_proj",
    )(h, shift, scale, g.reshape(1, D), w_ext, cos_t, sin_t)


def rope_tables(L, rotary):
    n_heads = ATT_HEADS + ATT_KV_HEADS
    if not rotary:
        return jnp.ones((L, QK_COLS), F32), jnp.zeros((L, QK_COLS), F32)
    pos = jnp.arange(L)
    row = (pos // GRID_W).astype(F32)
    col = (pos % GRID_W).astype(F32)
    half = HEAD_DIM // 2
    inv_freq = ROPE_BASE ** (-jnp.arange(0, half, 2, dtype=F32) / half)
    ang_r = row[:, None] * inv_freq[None]
    ang_c = col[:, None] * inv_freq[None]
    cos_h = jnp.concatenate([jnp.cos(ang_r)] * 2 + [jnp.cos(ang_c)] * 2, axis=-1)
    sin_h = jnp.concatenate([jnp.sin(ang_r)] * 2 + [jnp.sin(ang_c)] * 2, axis=-1)
    return jnp.tile(cos_h, (1, n_heads)), jnp.tile(sin_h, (1, n_heads))


def extend_w_in(w_in):
    c1 = HY_COLS + CONF_COLS
    wqk = w_in[:, c1:c1 + QK_COLS]
    D = w_in.shape[0]
    w4 = wqk.reshape(D, QK_COLS // 32, 2, 16)
    wrot = jnp.stack([-w4[:, :, 1], w4[:, :, 0]], axis=2).reshape(D, QK_COLS)
    return jnp.concatenate([w_in, wrot], axis=1).astype(BF16)


def _softmax_parts(q, k_list, extra_logit):
    scale = HEAD_DIM ** -0.5
    s_list = []
    for k, mask in k_list:
        s = lax.dot_general(q, k, (((1,), (1,)), ((), ())), preferred_element_type=F32) * scale
        if mask is not None:
            s = jnp.where(mask, s, NEG_INF)
        s_list.append(s)
    m = extra_logit
    for s in s_list:
        m = jnp.maximum(m, jnp.max(s, axis=-1, keepdims=True))
    p_list = [jnp.exp(s - m) for s in s_list]
    denom = jnp.exp(extra_logit - m)
    for p in p_list:
        denom = denom + jnp.sum(p, axis=-1, keepdims=True)
    return p_list, 1.0 / denom


ATT_Q_BLOCKS = 4


def _win_attn_kernel(sink_ref, q_ref, kp_ref, kc_ref, kn_ref, vp_ref, vc_ref, vn_ref,
                     kx_ref, vx_ref, o_ref, *, seq_len):
    i = pl.program_id(1)
    blk = ATT_BLOCK
    qb = q_ref.shape[1] // blk
    scale = HEAD_DIM ** -0.5
    g = ATT_HEADS // ATT_KV_HEADS
    kw = jnp.concatenate([kp_ref[0], kc_ref[0], kn_ref[0]], axis=0)
    vw = jnp.concatenate([vp_ref[0], vc_ref[0], vn_ref[0]], axis=0).astype(BF16)
    kwt = kw.T.astype(BF16)
    kxt = kx_ref[0].T.astype(BF16)
    vx = vx_ref[0].astype(BF16)
    row = lax.broadcasted_iota(jnp.int32, (g * blk, 3 * blk), 0) % blk
    col = lax.broadcasted_iota(jnp.int32, (g * blk, 3 * blk), 1)
    band_bias = jnp.where(jnp.abs(col - blk - row) <= WINDOW, 0.0, NEG_INF)
    col1 = lax.broadcasted_iota(jnp.int32, (1, 3 * blk), 1)
    for j in range(qb):
        q_blk = i * qb + j
        k_pos = (q_blk - 1) * blk + col1
        edge_bias = jnp.where(k_pos >= 0, jnp.where(k_pos < seq_len, 0.0, NEG_INF), NEG_INF)
        bias = band_bias + edge_bias
        outs = []
        for kv in range(ATT_KV_HEADS):
            ksl = slice(kv * HEAD_DIM, (kv + 1) * HEAD_DIM)
            heads = range(kv * g, (kv + 1) * g)
            qs = (jnp.concatenate([q_ref[0, j * blk:(j + 1) * blk, h * HEAD_DIM:(h + 1) * HEAD_DIM]
                                   for h in heads], axis=0) * scale).astype(BF16)
            sink = jnp.concatenate([jnp.full((blk, 1), sink_ref[h], F32) for h in heads], axis=0)
            s_win = jnp.dot(qs, kwt[ksl, j * blk:(j + 3) * blk], preferred_element_type=F32) + bias
            s_ctx = jnp.dot(qs, kxt[ksl, :], preferred_element_type=F32)
            m = jnp.maximum(jnp.maximum(jnp.max(s_win, axis=-1, keepdims=True),
                                        jnp.max(s_ctx, axis=-1, keepdims=True)), sink)
            p_win = jnp.exp(s_win - m)
            p_ctx = jnp.exp(s_ctx - m)
            denom = (jnp.exp(sink - m) + jnp.sum(p_win, axis=-1, keepdims=True)
                     + jnp.sum(p_ctx, axis=-1, keepdims=True))
            o = (jnp.dot(p_win.astype(BF16), vw[j * blk:(j + 3) * blk, ksl], preferred_element_type=F32)
                 + jnp.dot(p_ctx.astype(BF16), vx[:, ksl], preferred_element_type=F32)) * (1.0 / denom)
            outs.extend([o[k * blk:(k + 1) * blk] for k in range(g)])
        o_ref[0, j * blk:(j + 1) * blk, :] = jnp.concatenate(outs, axis=-1)


def window_attention(at_x, at_c, sink):
    B, S, _ = at_x.shape
    C = at_c.shape[1]
    blk = ATT_BLOCK
    qb = ATT_Q_BLOCKS
    nb = S // blk
    kcol = QK_COLS // 128 - 1
    vcol = kcol + 1

    def edge_spec(col, off):
        return pl.BlockSpec((1, blk, 128), lambda b, i, s: (b, jnp.clip(i * qb + off, 0, nb - 1), col))

    def mid_spec(col):
        return pl.BlockSpec((1, qb * blk, 128), lambda b, i, s: (b, i, col))

    grid_spec = pltpu.PrefetchScalarGridSpec(
        num_scalar_prefetch=1,
        grid=(B, nb // qb),
        in_specs=[pl.BlockSpec((1, qb * blk, ATT_HEADS * HEAD_DIM), lambda b, i, s: (b, i, 0)),
                  edge_spec(kcol, -1), mid_spec(kcol), edge_spec(kcol, qb),
                  edge_spec(vcol, -1), mid_spec(vcol), edge_spec(vcol, qb),
                  pl.BlockSpec((1, C, 128), lambda b, i, s: (b, 0, kcol)),
                  pl.BlockSpec((1, C, 128), lambda b, i, s: (b, 0, vcol))],
        out_specs=pl.BlockSpec((1, qb * blk, ATT_HEADS * HEAD_DIM), lambda b, i, s: (b, i, 0)),
    )
    return pl.pallas_call(
        functools.partial(_win_attn_kernel, seq_len=S),
        grid_spec=grid_spec,
        out_shape=jax.ShapeDtypeStruct((B, S, ATT_HEADS * HEAD_DIM), F32),
        compiler_params=_cparams("parallel", "parallel"),
        name="window_attention",
    )(sink.astype(F32), at_x, at_x, at_x, at_x, at_x, at_x, at_x, at_c, at_c)


def _ctx_attn_kernel(sink_ref, q_ref, kx_ref, vx_ref, o_ref):
    q = q_ref[0].astype(BF16)
    kx = kx_ref[0].astype(BF16)
    vx = vx_ref[0].astype(BF16)
    g = ATT_HEADS // ATT_KV_HEADS
    outs = []
    for h in range(ATT_HEADS):
        kv = h // g
        qs = q[:, h * HEAD_DIM:(h + 1) * HEAD_DIM]
        ksl = slice(kv * HEAD_DIM, (kv + 1) * HEAD_DIM)
        (p_ctx,), inv = _softmax_parts(qs, [(kx[:, ksl], None)], sink_ref[h])
        outs.append(jnp.dot(p_ctx.astype(BF16), vx[:, ksl], preferred_element_type=F32) * inv)
    o_ref[0] = jnp.concatenate(outs, axis=-1)


def context_attention(at_c, sink):
    B, C, _ = at_c.shape
    kcol = QK_COLS // 128 - 1
    grid_spec = pltpu.PrefetchScalarGridSpec(
        num_scalar_prefetch=1,
        grid=(B,),
        in_specs=[pl.BlockSpec((1, C, ATT_HEADS * HEAD_DIM), lambda b, s: (b, 0, 0)),
                  pl.BlockSpec((1, C, 128), lambda b, s: (b, 0, kcol)),
                  pl.BlockSpec((1, C, 128), lambda b, s: (b, 0, kcol + 1))],
        out_specs=pl.BlockSpec((1, C, ATT_HEADS * HEAD_DIM), lambda b, s: (b, 0, 0)),
    )
    return pl.pallas_call(
        _ctx_attn_kernel,
        grid_spec=grid_spec,
        out_shape=jax.ShapeDtypeStruct((B, C, ATT_HEADS * HEAD_DIM), F32),
        compiler_params=_cparams("parallel"),
        name="context_attention",
    )(sink.astype(F32), at_c, at_c, at_c)


def _out_proj_kernel(y0_ref, y1_ref, y2_ref, y3_ref, h_ref, g1_ref, gng_ref, w_ref,
                     n2g_ref, sh_ref, sc_ref, wr_ref, br_ref, ho_ref, lg_ref):
    parts = []
    for k, y_ref in enumerate((y0_ref, y1_ref, y2_ref, y3_ref)):
        y = y_ref[0]
        ms = jnp.mean(y * y, axis=-1, keepdims=True)
        yn = y * lax.rsqrt(ms + EPS) * gng_ref[:, k * D_GROUP:(k + 1) * D_GROUP]
        parts.append(yn.astype(BF16))
    yn = jnp.concatenate(parts, axis=-1)
    proj = jnp.dot(yn, w_ref[...], preferred_element_type=F32)
    h = h_ref[0] + g1_ref[0] * proj
    ho_ref[0] = h
    ms = jnp.mean(h * h, axis=-1, keepdims=True)
    n = h * lax.rsqrt(ms + EPS) * n2g_ref[...]
    n = n * (1.0 + sc_ref[0]) + sh_ref[0]
    lg_ref[0] = jnp.dot(n.astype(BF16), wr_ref[...], preferred_element_type=F32) + br_ref[...]


def out_proj(ys, h, g1, gng, w_out, n2g, sh2, sc2, w_router, b_router, tm):
    B, L, D = h.shape
    row3 = lambda n: pl.BlockSpec((1, tm, n), lambda b, i: (b, i, 0))
    mod = pl.BlockSpec((1, 1, D), lambda b, i: (b, 0, 0))
    full = lambda r, c: pl.BlockSpec((r, c), lambda b, i: (0, 0))
    return pl.pallas_call(
        _out_proj_kernel,
        grid=(B, L // tm),
        in_specs=[row3(D_GROUP)] * 4 + [row3(D), mod, full(1, D), full(D, D), full(1, D), mod, mod,
                                        full(D, ROUTER_COLS), full(1, ROUTER_COLS)],
        out_specs=[row3(D), row3(ROUTER_COLS)],
        out_shape=[jax.ShapeDtypeStruct((B, L, D), F32), jax.ShapeDtypeStruct((B, L, ROUTER_COLS), F32)],
        compiler_params=_cparams("parallel", "parallel"),
        name="out_proj",
    )(*ys, h, g1, gng.reshape(1, D), w_out, n2g.reshape(1, D), sh2, sc2, w_router, b_router)


N_PAIRS = EXP_PER_GROUP * (EXP_PER_GROUP - 1) // 2
N_CLASSES = N_GROUPS * N_PAIRS
ROUTE_TOKENS = 512
INFO_CLASS, INFO_RANK, INFO_WA, INFO_WB = 0, 1, 2, 3


SUBLANES = 8


def _route_kernel(lg_ref, below_ref, info_ref, cnt_ref, ids_ref, run):
    i = pl.program_id(0)

    @pl.when(i == 0)
    def _():
        run[...] = jnp.zeros_like(run)

    lg = lg_ref[...]
    li = lax.broadcasted_iota(jnp.int32, lg.shape, 1).astype(F32)
    big = float(ROUTER_COLS)

    def first_argmax(vals):
        m = jnp.max(vals, axis=-1, keepdims=True)
        return m, jnp.min(jnp.where(vals == m, li, big), axis=-1, keepdims=True)

    gl = jnp.where(li < N_GROUPS, lg, NEG_INF)
    gmax, g_idx = first_argmax(gl)
    g_prob = 1.0 / jnp.sum(jnp.exp(gl - gmax), axis=-1, keepdims=True)
    lo = N_GROUPS + EXP_PER_GROUP * g_idx
    el = jnp.where(li >= lo, jnp.where(li < lo + EXP_PER_GROUP, lg, NEG_INF), NEG_INF)
    m1, i1 = first_argmax(el)
    m2, i2 = first_argmax(jnp.where(li == i1, NEG_INF, el))
    e2 = jnp.exp(m2 - m1)
    w1 = g_prob / (1.0 + e2)
    w2 = g_prob * e2 / (1.0 + e2)
    j1 = i1 - lo
    j2 = i2 - lo
    a = jnp.minimum(j1, j2)
    b = jnp.maximum(j1, j2)
    cls = g_idx * N_PAIRS + (a * (2 * EXP_PER_GROUP - 1 - a)) * 0.5 + (b - a - 1.0)
    w_a = jnp.where(j1 < j2, w1, w2)
    w_b = jnp.where(j1 < j2, w2, w1)

    hit = li == cls
    onehot = jnp.where(hit, 1.0, 0.0)
    before = jnp.dot(below_ref[...], onehot.astype(BF16), preferred_element_type=F32)
    rank = jnp.sum(jnp.where(hit, before + run[...], 0.0), axis=-1, keepdims=True)
    run[...] = run[...] + jnp.sum(onehot, axis=0, keepdims=True)
    cnt_ref[...] = run[...]
    info = jnp.where(li == INFO_CLASS, cls, 0.0)
    info = jnp.where(li == INFO_RANK, rank, info)
    info = jnp.where(li == INFO_WA, w_a, info)
    info = jnp.where(li == INFO_WB, w_b, info)
    info_ref[...] = info
    ids_ref[0] = info.T[:SUBLANES].astype(jnp.int32)


def route_tokens(logits):
    T = logits.shape[0]
    tb = ROUTE_TOKENS
    below = (jnp.arange(tb)[None, :] < jnp.arange(tb)[:, None]).astype(BF16)
    return pl.pallas_call(
        _route_kernel,
        grid=(T // tb,),
        in_specs=[pl.BlockSpec((tb, ROUTER_COLS), lambda i: (i, 0)),
                  pl.BlockSpec((tb, tb), lambda i: (0, 0))],
        out_specs=[pl.BlockSpec((tb, ROUTER_COLS), lambda i: (i, 0)),
                   pl.BlockSpec((1, ROUTER_COLS), lambda i: (0, 0)),
                   pl.BlockSpec((1, SUBLANES, tb), lambda i: (i, 0, 0))],
        out_shape=[jax.ShapeDtypeStruct((T, ROUTER_COLS), F32), jax.ShapeDtypeStruct((1, ROUTER_COLS), F32),
                   jax.ShapeDtypeStruct((T // tb, SUBLANES, tb), jnp.int32)],
        scratch_shapes=[pltpu.VMEM((1, ROUTER_COLS), F32)],
        compiler_params=_cparams("arbitrary"),
        name="moe_route",
    )(logits, below)


def _pair_tables():
    a_tab, b_tab = [], []
    for g in range(N_GROUPS):
        for a in range(EXP_PER_GROUP):
            for b in range(a + 1, EXP_PER_GROUP):
                a_tab.append(g * EXP_PER_GROUP + a)
                b_tab.append(g * EXP_PER_GROUP + b)
    return jnp.array(a_tab, jnp.int32), jnp.array(b_tab, jnp.int32)


def _slot_kernel(ids_ref, start_ref, dest_ref):
    cls = ids_ref[0, INFO_CLASS:INFO_CLASS + 1, :]
    rank = ids_ref[0, INFO_RANK:INFO_RANK + 1, :]
    ci = lax.broadcasted_iota(jnp.int32, (ROUTER_COLS, cls.shape[1]), 0)
    start = jnp.sum(jnp.where(ci == cls, start_ref[...], 0), axis=0, keepdims=True)
    dest_ref[0] = jnp.broadcast_to(start + rank, dest_ref.shape[1:])


def slot_plan(ids, counts, n_blocks):
    nt, _, tb = ids.shape
    cnt = counts[0, :N_CLASSES].astype(jnp.int32)
    padded = (cnt + MOE_BLOCK - 1) // MOE_BLOCK * MOE_BLOCK
    upto = jnp.arange(N_CLASSES)[None, :] <= jnp.arange(N_CLASSES)[:, None]
    pad_end = jnp.sum(jnp.where(upto, padded[None, :], 0), axis=1)
    class_start = jnp.zeros((ROUTER_COLS, 1), jnp.int32).at[:N_CLASSES, 0].set(pad_end - padded)
    dest = pl.pallas_call(
        _slot_kernel,
        grid=(nt,),
        in_specs=[pl.BlockSpec((1, SUBLANES, tb), lambda i: (i, 0, 0)),
                  pl.BlockSpec((ROUTER_COLS, 1), lambda i: (0, 0))],
        out_specs=pl.BlockSpec((1, SUBLANES, tb), lambda i: (i, 0, 0)),
        out_shape=jax.ShapeDtypeStruct((nt, SUBLANES, tb), jnp.int32),
        compiler_params=_cparams("parallel"),
        name="moe_slots",
    )(ids, class_start)[:, 0, :].reshape(nt * tb)
    n_used = (pad_end[-1] // MOE_BLOCK).astype(jnp.int32).reshape(1)
    blk_first = jnp.arange(n_blocks, dtype=jnp.int32) * MOE_BLOCK
    blk_cls = jnp.minimum(jnp.sum((pad_end[None, :] <= blk_first[:, None]).astype(jnp.int32), axis=1),
                          N_CLASSES - 1)
    a_tab, b_tab = _pair_tables()
    hit = blk_cls[:, None] == jnp.arange(N_CLASSES)[None, :]
    pick = lambda tab: jnp.sum(jnp.where(hit, tab[None, :], 0), axis=1).astype(jnp.int32)
    return dest, pick(a_tab), pick(b_tab), n_used


DISPATCH_TOKENS = 256


def _wait_rows(buf, sem):
    pltpu.make_async_copy(buf, buf, sem).wait()


DMA_UNROLL = 8
TOKEN_TILE_ROWS = 8


def _store_token_tiles(tiles_ref, offset, pitch, x):
    n = x.shape[0]
    for j in range(x.shape[1] // LANES):
        tiles_ref[pl.ds(offset + j, n, stride=pitch), :] = x[:, j * LANES:(j + 1) * LANES]


def _load_token_tiles(tiles_ref, offset, pitch, n, width):
    return jnp.concatenate([tiles_ref[pl.ds(offset + j, n, stride=pitch), :] for j in range(width // LANES)],
                           axis=-1)


def _dispatch_kernel(dest_ref, hx_ref, hc_ref, g_ref, shx_ref, scx_ref, shc_ref, scc_ref, zeros_hbm,
                     xs_hbm, rows, sems, *, n_latent_blocks):
    del zeros_hbm
    i = pl.program_id(0)
    n = pl.num_programs(0)
    slot = i % 2
    tb = hx_ref.shape[0]

    @pl.when(i >= 2)
    def _():
        _wait_rows(rows.at[slot], sems.at[slot])

    def normed(h_ref, sh_ref, sc_ref):
        h = h_ref[...]
        ms = jnp.mean(h * h, axis=-1, keepdims=True)
        return h * lax.rsqrt(ms + EPS) * g_ref[...] * (1.0 + sc_ref[0]) + sh_ref[0]

    @pl.when(i < n_latent_blocks)
    def _():
        _store_token_tiles(rows.at[slot], 0, TOKEN_TILE_ROWS, normed(hx_ref, shx_ref, scx_ref))

    @pl.when(i >= n_latent_blocks)
    def _():
        _store_token_tiles(rows.at[slot], 0, TOKEN_TILE_ROWS, normed(hc_ref, shc_ref, scc_ref))

    def body(g, carry):
        for u in range(DMA_UNROLL):
            r = g * DMA_UNROLL + u
            dst = pl.multiple_of(dest_ref[0, 0, r] * TOKEN_TILE_ROWS, TOKEN_TILE_ROWS)
            pltpu.make_async_copy(rows.at[slot, pl.ds(r * TOKEN_TILE_ROWS, TOKEN_TILE_ROWS)],
                                  xs_hbm.at[pl.ds(dst, TOKEN_TILE_ROWS)], sems.at[slot]).start(priority=u % 2)
        return carry
    lax.fori_loop(0, tb // DMA_UNROLL, body, 0)

    @pl.when(i == n - 1)
    def _():
        _wait_rows(rows.at[slot], sems.at[slot])

        @pl.when(n >= 2)
        def _():
            _wait_rows(rows.at[1 - slot], sems.at[1 - slot])


def moe_dispatch(h_x, h_c, dest, n_blocks, n2g, mod_x, mod_c, tokens_per_batch):
    Tx, D = h_x.shape
    tb = DISPATCH_TOKENS
    nxb = Tx // tb
    if h_c is None:
        h_c, mod_c, ncb = h_x, mod_x, 0
    else:
        ncb = h_c.shape[0] // tb
    per_b = tokens_per_batch // tb
    P = n_blocks * MOE_BLOCK
    tile_rows = D // LANES
    assert tile_rows == TOKEN_TILE_ROWS
    xi = lambda i: jnp.minimum(i, nxb - 1)
    ci = lambda i: jnp.maximum(i - nxb, 0)
    modx = pl.BlockSpec((1, 1, D), lambda i: (xi(i) // per_b, 0, 0))
    modc = pl.BlockSpec((1, 1, D), lambda i: (0, 0, 0))
    return pl.pallas_call(
        functools.partial(_dispatch_kernel, n_latent_blocks=nxb),
        grid=(nxb + ncb,),
        in_specs=[pl.BlockSpec((1, 1, tb), lambda i: (i, 0, 0), memory_space=pltpu.SMEM),
                  pl.BlockSpec((tb, D), lambda i: (xi(i), 0)),
                  pl.BlockSpec((tb, D), lambda i: (ci(i), 0)),
                  pl.BlockSpec((1, D), lambda i: (0, 0)),
                  modx, modx, modc, modc,
                  pl.BlockSpec(memory_space=pl.ANY)],
        out_specs=pl.BlockSpec(memory_space=pl.ANY),
        out_shape=jax.ShapeDtypeStruct((P * tile_rows, LANES), F32),
        scratch_shapes=[pltpu.VMEM((2, tb * tile_rows, LANES), F32), pltpu.SemaphoreType.DMA((2,))],
        input_output_aliases={8: 0},
        compiler_params=_cparams("arbitrary"),
        name="moe_dispatch",
    )(dest.reshape(-1, 1, tb), h_x, h_c, n2g.reshape(1, D), mod_x[0], mod_x[1], mod_c[0], mod_c[1],
      jnp.zeros((P * tile_rows, LANES), F32))


def _expert_pair_kernel(ea_ref, eb_ref, nused_ref, xs_ref, wga_ref, wua_ref, wda_ref, wgb_ref, wub_ref, wdb_ref,
                        o_ref):
    del ea_ref, eb_ref
    i = pl.program_id(0)
    D = wga_ref.shape[2]

    @pl.when(i < nused_ref[0])
    def _():
        xb = _load_token_tiles(xs_ref, 0, TOKEN_TILE_ROWS, MOE_BLOCK, D).astype(BF16)
        halves = []
        for wg_ref, wu_ref, wd_ref in ((wga_ref, wua_ref, wda_ref), (wgb_ref, wub_ref, wdb_ref)):
            gate = jnp.dot(xb, wg_ref[0, 0], preferred_element_type=F32)
            up = jnp.dot(xb, wu_ref[0, 0], preferred_element_type=F32)
            hid = (gate * jax.nn.sigmoid(gate) * up).astype(BF16)
            out = jnp.dot(hid, wd_ref[0, 0], preferred_element_type=F32)
            halves.append(lax.bitcast_convert_type(out.astype(BF16).astype(F32), jnp.uint32))
        _store_token_tiles(o_ref, 0, TOKEN_TILE_ROWS, halves[0] | (halves[1] >> 16))

    @pl.when(i >= nused_ref[0])
    def _():
        o_ref[...] = jnp.zeros_like(o_ref)


def _unpack_pair(words):
    hi = lax.bitcast_convert_type(words & jnp.uint32(0xFFFF0000), F32)
    lo = lax.bitcast_convert_type(words << 16, F32)
    return hi, lo


def expert_pairs(xs, blk_a, blk_b, n_used, w_gate, w_up, w_down, layer):
    D, DE = w_gate.shape[2:]
    P = xs.shape[0] // TOKEN_TILE_ROWS
    n_blocks = P // MOE_BLOCK
    wspec = lambda shape, which: pl.BlockSpec(shape, lambda i, ea, eb, nu: (layer, (ea, eb)[which][i], 0, 0))
    grid_spec = pltpu.PrefetchScalarGridSpec(
        num_scalar_prefetch=3,
        grid=(n_blocks,),
        in_specs=[pl.BlockSpec((MOE_BLOCK * TOKEN_TILE_ROWS, LANES), lambda i, ea, eb, nu: (i, 0)),
                  wspec((1, 1, D, DE), 0), wspec((1, 1, D, DE), 0), wspec((1, 1, DE, D), 0),
                  wspec((1, 1, D, DE), 1), wspec((1, 1, D, DE), 1), wspec((1, 1, DE, D), 1)],
        out_specs=pl.BlockSpec((MOE_BLOCK * TOKEN_TILE_ROWS, LANES), lambda i, ea, eb, nu: (i, 0)),
    )
    return pl.pallas_call(
        _expert_pair_kernel,
        grid_spec=grid_spec,
        out_shape=jax.ShapeDtypeStruct((P * TOKEN_TILE_ROWS, LANES), jnp.uint32),
        compiler_params=_cparams("arbitrary"),
        name="moe_experts",
    )(blk_a, blk_b, n_used, xs, w_gate, w_up, w_down, w_gate, w_up, w_down)


def _gather_pairs(idx_ref, src_hbm, buf, sem, n_tokens):
    def body(g, carry):
        for u in range(DMA_UNROLL):
            r = g * DMA_UNROLL + u
            src = pl.multiple_of(idx_ref[0, 0, r] * TOKEN_TILE_ROWS, TOKEN_TILE_ROWS)
            pltpu.make_async_copy(src_hbm.at[pl.ds(src, TOKEN_TILE_ROWS)],
                                  buf.at[pl.ds(r * TOKEN_TILE_ROWS, TOKEN_TILE_ROWS)], sem).start(priority=u % 2)
        return carry
    lax.fori_loop(0, n_tokens // DMA_UNROLL, body, 0)


def _collect_kernel(dest_ref, dest_next_ref, o_hbm, info_ref, h_ref, g2_ref, fg_ref, out_ref, obuf, sems, *,
                    final_norm):
    i = pl.program_id(0)
    n = pl.num_programs(0)
    slot = i % 2
    tb, D = h_ref.shape

    @pl.when(i == 0)
    def _():
        _gather_pairs(dest_ref, o_hbm, obuf.at[0], sems.at[0], tb)

    @pl.when(i + 1 < n)
    def _():
        _gather_pairs(dest_next_ref, o_hbm, obuf.at[1 - slot], sems.at[1 - slot], tb)

    _wait_rows(obuf.at[slot], sems.at[slot])
    e_a, e_b = _unpack_pair(_load_token_tiles(obuf.at[slot], 0, TOKEN_TILE_ROWS, tb, D))
    m = info_ref[:, INFO_WA:INFO_WA + 1] * e_a + info_ref[:, INFO_WB:INFO_WB + 1] * e_b
    h = h_ref[...] + g2_ref[0] * m
    if final_norm:
        ms = jnp.mean(h * h, axis=-1, keepdims=True)
        h = h * lax.rsqrt(ms + EPS) * fg_ref[...]
    out_ref[...] = h


def moe_collect(o_sorted, dest, info, block_offset, h_tokens, g2, tokens_per_batch, final_g, final_norm):
    T, D = h_tokens.shape
    tb = DISPATCH_TOKENS
    nt = T // tb
    per_b = tokens_per_batch // tb
    last = block_offset + nt - 1
    dest3 = dest.reshape(-1, 1, tb)
    return pl.pallas_call(
        functools.partial(_collect_kernel, final_norm=final_norm),
        grid=(nt,),
        in_specs=[pl.BlockSpec((1, 1, tb), lambda i: (block_offset + i, 0, 0), memory_space=pltpu.SMEM),
                  pl.BlockSpec((1, 1, tb), lambda i: (jnp.minimum(block_offset + i + 1, last), 0, 0),
                               memory_space=pltpu.SMEM),
                  pl.BlockSpec(memory_space=pl.ANY),
                  pl.BlockSpec((tb, ROUTER_COLS), lambda i: (block_offset + i, 0)),
                  pl.BlockSpec((tb, D), lambda i: (i, 0)),
                  pl.BlockSpec((1, 1, D), lambda i: (i // per_b, 0, 0)),
                  pl.BlockSpec((1, D), lambda i: (0, 0))],
        out_specs=pl.BlockSpec((tb, D), lambda i: (i, 0)),
        out_shape=jax.ShapeDtypeStruct((T, D), F32),
        scratch_shapes=[pltpu.VMEM((2, tb * TOKEN_TILE_ROWS, LANES), jnp.uint32), pltpu.SemaphoreType.DMA((2,))],
        compiler_params=_cparams("arbitrary"),
        name="moe_collect",
    )(dest3, dest3, o_sorted, info, h_tokens, g2, final_g.reshape(1, D))


CONV_MARGIN = 16


def _time_chunk(L):
    return min(L, 256)


LANES = 128


def _zero_margins(pad_ref, L):
    zeros = jnp.zeros((CONV_MARGIN, LANES), F32)
    for s in range(pad_ref.shape[0]):
        pad_ref[s, pl.ds(0, CONV_MARGIN), :] = zeros
        pad_ref[s, pl.ds(CONV_MARGIN + L, CONV_MARGIN), :] = zeros


def _dw_conv_slab(pad_ref, s, base, T, w_ref, b_ref, col, taps, pad_left):
    acc = jnp.broadcast_to(b_ref[:, col:col + LANES], (T, LANES))
    for k in range(taps):
        acc = acc + w_ref[k:k + 1, col:col + LANES] * pad_ref[s, pl.ds(base + (CONV_MARGIN - pad_left + k), T), :]
    return acc


def _conformer_kernel(u_ref, w_ref, b_ref, g_ref, beta_ref, o_ref, ypad):
    L = o_ref.shape[1]
    T = _time_chunk(L)
    C = D_GROUP
    n_slabs = C // LANES
    pad = (CONF_KERNEL - 1) // 2
    _zero_margins(ypad, L)

    def glu(j, carry):
        base = pl.multiple_of(j * T, T)
        for s in range(n_slabs):
            a = u_ref[0, pl.ds(base, T), s * LANES:(s + 1) * LANES]
            gate = u_ref[0, pl.ds(base, T), C + s * LANES:C + (s + 1) * LANES]
            ypad[s, pl.ds(CONV_MARGIN + base, T), :] = a * jax.nn.sigmoid(gate)
        return carry
    lax.fori_loop(0, L // T, glu, 0)

    def conv(j, carry):
        base = pl.multiple_of(j * T, T)
        acc = jnp.concatenate([_dw_conv_slab(ypad, s, base, T, w_ref, b_ref, s * LANES, CONF_KERNEL, pad)
                               for s in range(n_slabs)], axis=-1)
        mu = jnp.mean(acc, axis=-1, keepdims=True)
        cen = acc - mu
        var = jnp.mean(cen * cen, axis=-1, keepdims=True)
        y = cen * lax.rsqrt(var + EPS) * g_ref[...] + beta_ref[...]
        o_ref[0, pl.ds(base, T), :] = y * jax.nn.sigmoid(y)
        return carry
    lax.fori_loop(0, L // T, conv, 0)


def conformer_conv(u, w, b, ln_g, ln_b):
    B, L, _ = u.shape
    C = D_GROUP
    vec = pl.BlockSpec((1, C), lambda i: (0, 0))
    return pl.pallas_call(
        _conformer_kernel,
        grid=(B,),
        in_specs=[pl.BlockSpec((1, L, 2 * C), lambda i: (i, 0, 0)),
                  pl.BlockSpec((CONF_KERNEL, C), lambda i: (0, 0)), vec, vec, vec],
        out_specs=pl.BlockSpec((1, L, C), lambda i: (i, 0, 0)),
        out_shape=jax.ShapeDtypeStruct((B, L, C), F32),
        scratch_shapes=[pltpu.VMEM((C // LANES, L + 2 * CONV_MARGIN, LANES), F32)],
        compiler_params=_cparams("parallel"),
        name="conformer_conv",
    )(u, w, b.reshape(1, C), ln_g.reshape(1, C), ln_b.reshape(1, C))


def _gelu_tanh(x):
    return 0.5 * x * (1.0 + jnp.tanh(math.sqrt(2.0 / math.pi) * (x + 0.044715 * (x * x * x))))


def _lru_kernel(uc_ref, ux_ref, cw_ref, cb_ref, wcat_ref, bcat_ref, lam_ref, *rest, need_ctx):
    if need_ctx:
        oc_ref, ox_ref, cpad, xpad, a_s, b_s, yx, yc = rest
    else:
        ox_ref, cpad, xpad, a_s, b_s, yx = rest
        oc_ref = yc = None
    C = D_GROUP
    n_slabs = C // LANES
    Lc = uc_ref.shape[1]
    Lx = ux_ref.shape[1]
    pad_l = (LRU_CONV - 1) // 2

    def fill(pad_ref, u_ref, L):
        T = _time_chunk(L)
        _zero_margins(pad_ref, L)

        def body(j, carry):
            base = pl.multiple_of(j * T, T)
            for s in range(n_slabs):
                pad_ref[s, pl.ds(CONV_MARGIN + base, T), :] = u_ref[0, pl.ds(base, T),
                                                                    C + s * LANES:C + (s + 1) * LANES]
            return carry
        lax.fori_loop(0, L // T, body, 0)

    fill(cpad, uc_ref, Lc)
    fill(xpad, ux_ref, Lx)

    def coeffs(pad_ref, base, T, d):
        x = jnp.concatenate([_dw_conv_slab(pad_ref, s, base, T, cw_ref, cb_ref, s * LANES, LRU_CONV, pad_l)
                             for s in range(n_slabs)], axis=-1)
        t = jnp.tanh(jnp.dot(x.astype(BF16), wcat_ref[:, 2 * d * C:2 * (d + 1) * C],
                             preferred_element_type=F32) + bcat_ref[:, 2 * d * C:2 * (d + 1) * C])
        i = 0.5 * t[:, C:] + 0.5
        z = -lam_ref[d:d + 1, :]
        softplus = jnp.maximum(z, 0.0) + jnp.log(1.0 + jnp.exp(-jnp.abs(z)))
        half_rate = (-0.5 * LRU_C) * softplus
        a = jnp.exp(half_rate * t[:, :C] + half_rate)
        b = jnp.sqrt(1.0 - a * a) * (i * x)
        for s in range(n_slabs):
            a_s[d * n_slabs + s, pl.ds(0, T), :] = a[:, s * LANES:(s + 1) * LANES]
            b_s[d * n_slabs + s, pl.ds(0, T), :] = b[:, s * LANES:(s + 1) * LANES]

    def run(pad_ref, L, h, y_ref):
        T = _time_chunk(L)
        n = L // T

        def chunk(j, h):
            base_f = pl.multiple_of(j * T, T)
            base_b = pl.multiple_of((n - 1 - j) * T, T)
            coeffs(pad_ref, base_f, T, 0)
            coeffs(pad_ref, base_b, T, 1)

            def step(t, h):
                new = []
                for d, (base, row) in enumerate(((base_f, t), (base_b, T - 1 - t))):
                    for s in range(n_slabs):
                        k = d * n_slabs + s
                        hs = a_s[k, pl.ds(row, 1), :] * h[k] + b_s[k, pl.ds(row, 1), :]
                        if y_ref is not None:
                            y_ref[k, pl.ds(base + row, 1), :] = hs
                        new.append(hs)
                return tuple(new)
            return lax.fori_loop(0, T, step, h, unroll=8)
        return lax.fori_loop(0, n, chunk, h)

    h = tuple(jnp.zeros((1, LANES), F32) for _ in range(2 * n_slabs))
    h = run(cpad, Lc, h, yc)
    run(xpad, Lx, h, yx)

    def finish(u_ref, y_ref, o_ref, L):
        T = _time_chunk(L)

        def body(j, carry):
            base = pl.multiple_of(j * T, T)
            y = jnp.concatenate([y_ref[s, pl.ds(base, T), :] + y_ref[n_slabs + s, pl.ds(base, T), :]
                                 for s in range(n_slabs)], axis=-1)
            o_ref[0, pl.ds(base, T), :] = _gelu_tanh(u_ref[0, pl.ds(base, T), :C]) * y
            return carry
        lax.fori_loop(0, L // T, body, 0)

    finish(ux_ref, yx, ox_ref, Lx)
    if need_ctx:
        finish(uc_ref, yc, oc_ref, Lc)


def _block_diag(w):
    H, n, _ = w.shape
    eye = jnp.eye(H, dtype=w.dtype)
    return (eye[:, None, :, None] * w[:, :, None, :]).reshape(H * n, H * n)


def rglru_mixer(uc, ux, lp, need_ctx):
    B, Lc, _ = uc.shape
    Lx = ux.shape[1]
    C = D_GROUP
    wcat = (0.5 * jnp.concatenate([_block_diag(lp["lru_wa"][0]), _block_diag(lp["lru_wx"][0]),
                                   _block_diag(lp["lru_wa"][1]), _block_diag(lp["lru_wx"][1])], axis=1)).astype(BF16)
    bcat = 0.5 * jnp.concatenate([lp["lru_ba"][0], lp["lru_bx"][0], lp["lru_ba"][1], lp["lru_bx"][1]]).reshape(1, 4 * C)
    full = lambda r, c: pl.BlockSpec((r, c), lambda i: (0, 0))
    seq = lambda L, n: pl.BlockSpec((1, L, n), lambda i: (i, 0, 0))
    out_specs = [seq(Lx, C)]
    out_shape = [jax.ShapeDtypeStruct((B, Lx, C), F32)]
    if need_ctx:
        out_specs = [seq(Lc, C)] + out_specs
        out_shape = [jax.ShapeDtypeStruct((B, Lc, C), F32)] + out_shape
    T = _time_chunk(Lx)
    slab = lambda rows, n=1: pltpu.VMEM((n * C // LANES, rows, LANES), F32)
    scratch = [slab(Lc + 2 * CONV_MARGIN), slab(Lx + 2 * CONV_MARGIN), slab(T, 2), slab(T, 2), slab(Lx, 2)]
    if need_ctx:
        scratch.append(slab(Lc, 2))
    res = pl.pallas_call(
        functools.partial(_lru_kernel, need_ctx=need_ctx),
        grid=(B,),
        in_specs=[seq(Lc, 2 * C), seq(Lx, 2 * C), full(LRU_CONV, C), full(1, C), full(C, 4 * C),
                  full(1, 4 * C), full(2, C)],
        out_specs=out_specs,
        out_shape=out_shape,
        scratch_shapes=scratch,
        compiler_params=_cparams("parallel"),
        name="rglru",
    )(uc, ux, lp["lru_conv_w"], lp["lru_conv_b"].reshape(1, C), wcat, bcat, lp["lru_lambda"])
    if need_ctx:
        return res[0], res[1]
    return None, res[0]


HY_SHORT = 3


def _short_conv(pad_ref, base, T, w_ref, b_ref, c0, c1):
    return jnp.concatenate([_dw_conv_slab(pad_ref, col // LANES, base, T, w_ref, b_ref, col, HY_SHORT, 1)
                            for col in range(c0, c1, LANES)], axis=-1)


def _fill_padded(pad_ref, u_ref, L, T):
    _zero_margins(pad_ref, L)

    def body(j, carry):
        base = pl.multiple_of(j * T, T)
        for s in range(pad_ref.shape[0]):
            pad_ref[s, pl.ds(CONV_MARGIN + base, T), :] = u_ref[0, pl.ds(base, T), s * LANES:(s + 1) * LANES]
        return carry
    lax.fori_loop(0, L // T, body, 0)


def _hyena_pre_kernel(u_ref, w_ref, b_ref, z_ref, upad):
    L = u_ref.shape[1]
    T = _time_chunk(L)
    C = D_GROUP
    _fill_padded(upad, u_ref, L, T)

    def body(j, carry):
        base = pl.multiple_of(j * T, T)
        x1 = _short_conv(upad, base, T, w_ref, b_ref, C, 2 * C)
        v = _short_conv(upad, base, T, w_ref, b_ref, 2 * C, 3 * C)
        z_ref[pl.ds(base, T), :] = (x1 * v).astype(BF16)
        return carry
    lax.fori_loop(0, L // T, body, 0)


def _hyena_post_kernel(u_ref, y_ref, w_ref, b_ref, bias_ref, o_ref, upad):
    L = u_ref.shape[1]
    T = _time_chunk(L)
    C = D_GROUP
    _fill_padded(upad, u_ref, L, T)

    def body(j, carry):
        base = pl.multiple_of(j * T, T)
        x0 = _short_conv(upad, base, T, w_ref, b_ref, 0, C)
        x1 = _short_conv(upad, base, T, w_ref, b_ref, C, 2 * C)
        v = _short_conv(upad, base, T, w_ref, b_ref, 2 * C, 3 * C)
        o_ref[0, pl.ds(base, T), :] = x0 * (y_ref[pl.ds(base, T), :] + (x1 * v) * bias_ref[...])
        return carry
    lax.fori_loop(0, L // T, body, 0)


def _spectrum_kernel(f_ref, z_ref, ha_ref, hb_ref, hc_ref, y_ref):
    tf = ha_ref.shape[0]
    acc = jnp.dot(f_ref[...], z_ref[...], preferred_element_type=F32)
    zr = acc[:tf]
    zi = acc[tf:]
    y_ref[:tf, :] = (zr * ha_ref[...] - zi * hb_ref[...]).astype(BF16)
    y_ref[tf:, :] = (zr * hb_ref[...] + zi * hc_ref[...]).astype(BF16)


def _idft_kernel(f_ref, y_ref, o_ref):
    o_ref[...] = jnp.dot(f_ref[...], y_ref[...], preferred_element_type=F32)


def dft_tables(L):
    N = 2 * L
    tf = min(256, L)
    k = jnp.arange(L, dtype=jnp.int32)
    n = jnp.arange(L, dtype=jnp.int32)
    ang = (2.0 * math.pi / N) * ((k[:, None] * n[None, :]) % N).astype(F32)
    cos = jnp.cos(ang)
    sin = jnp.sin(ang)
    nyq = jnp.where(n % 2 == 0, 1.0, -1.0).astype(F32)
    f_re = cos
    f_im = (-sin).at[0].set(nyq)
    fwd = jnp.stack([f_re.reshape(L // tf, tf, L), f_im.reshape(L // tf, tf, L)], axis=1).reshape(N, L)
    ck = jnp.where(k == 0, 1.0, 2.0).astype(F32)[:, None] / N
    i_re = cos * ck
    i_im = (-sin * ck).at[0].set(nyq / N)
    inv = jnp.stack([i_re.reshape(L // tf, tf, L), i_im.reshape(L // tf, tf, L)], axis=1).reshape(N, L).T
    return fwd.astype(BF16), inv.astype(BF16)


def filter_spectrum(h_fwd, h_bwd):
    L, C = h_fwd.shape
    k = jnp.concatenate([h_fwd, jnp.zeros((1, C), F32), h_bwd[1:][::-1]], axis=0)
    hf = jnp.fft.rfft(k, axis=0)
    hr = jnp.real(hf)
    hi = jnp.imag(hf)
    a = hr[:L]
    b = hi[:L].at[0].set(0.0)
    c = hr[:L].at[0].set(hr[L])
    return a, b, c


def hyena_mixer(u, lp, tables):
    B, L, _ = u.shape
    C = D_GROUP
    N = 2 * L
    fwd, inv = tables
    tf = min(256, L)
    T = _time_chunk(L)
    w, bsh = lp["hy_short_w"], lp["hy_short_b"].reshape(1, 3 * C)
    z2 = pl.pallas_call(
        _hyena_pre_kernel,
        grid=(B,),
        in_specs=[pl.BlockSpec((1, L, 3 * C), lambda b: (b, 0, 0)),
                  pl.BlockSpec((HY_SHORT, 3 * C), lambda b: (0, 0)),
                  pl.BlockSpec((1, 3 * C), lambda b: (0, 0))],
        out_specs=pl.BlockSpec((L, C), lambda b: (0, b)),
        out_shape=jax.ShapeDtypeStruct((L, B * C), BF16),
        scratch_shapes=[pltpu.VMEM((3 * C // LANES, L + 2 * CONV_MARGIN, LANES), F32)],
        compiler_params=_cparams("parallel"),
        name="hyena_pre",
    )(u, w, bsh)

    h_fwd, h_bwd = _hyena_filters(L, lp)
    tn = 2 * C
    ha, hb, hc = [jnp.tile(t, (1, tn // C)) for t in filter_spectrum(h_fwd, h_bwd)]
    hspec = pl.BlockSpec((tf, tn), lambda i, j: (i, 0))
    y2 = pl.pallas_call(
        _spectrum_kernel,
        grid=(L // tf, B * C // tn),
        in_specs=[pl.BlockSpec((2 * tf, L), lambda i, j: (i, 0)),
                  pl.BlockSpec((L, tn), lambda i, j: (0, j)), hspec, hspec, hspec],
        out_specs=pl.BlockSpec((2 * tf, tn), lambda i, j: (i, j)),
        out_shape=jax.ShapeDtypeStruct((N, B * C), BF16),
        compiler_params=_cparams("parallel", "parallel"),
        name="hyena_spectrum",
    )(fwd, z2, ha, hb, hc)

    tl = min(256, L)
    yt = pl.pallas_call(
        _idft_kernel,
        grid=(L // tl, B * C // tn),
        in_specs=[pl.BlockSpec((tl, N), lambda i, j: (i, 0)),
                  pl.BlockSpec((N, tn), lambda i, j: (0, j))],
        out_specs=pl.BlockSpec((tl, tn), lambda i, j: (i, j)),
        out_shape=jax.ShapeDtypeStruct((L, B * C), F32),
        compiler_params=_cparams("parallel", "parallel"),
        name="hyena_idft",
    )(inv, y2)

    return pl.pallas_call(
        _hyena_post_kernel,
        grid=(B,),
        in_specs=[pl.BlockSpec((1, L, 3 * C), lambda b: (b, 0, 0)),
                  pl.BlockSpec((L, C), lambda b: (0, b)),
                  pl.BlockSpec((HY_SHORT, 3 * C), lambda b: (0, 0)),
                  pl.BlockSpec((1, 3 * C), lambda b: (0, 0)),
                  pl.BlockSpec((1, C), lambda b: (0, 0))],
        out_specs=pl.BlockSpec((1, L, C), lambda b: (b, 0, 0)),
        out_shape=jax.ShapeDtypeStruct((B, L, C), F32),
        scratch_shapes=[pltpu.VMEM((3 * C // LANES, L + 2 * CONV_MARGIN, LANES), F32)],
        compiler_params=_cparams("parallel"),
        name="hyena_post",
    )(u, yt, w, bsh, lp["hy_bias"].reshape(1, C))


FFT_N2 = 128
FFT_UNROLL = 8


class _FftPlan:
    def __init__(self, L):
        self.L = L
        self.N = 2 * L
        self.N1 = self.N // FFT_N2
        self.KH = self.N1 // 2 + 1
        self.KP = -(-self.KH // 8) * 8
        self.PA = 2 * self.KP + 4


def fft_tables(L):
    p = _FftPlan(L)
    N, N1, KH, KP = p.N, p.N1, p.KH, p.KP
    n2 = jnp.arange(FFT_N2, dtype=jnp.int32)
    k1 = jnp.arange(KP, dtype=jnp.int32)
    n1 = jnp.arange(N1, dtype=jnp.int32)
    n = FFT_N2 * n1[None, None, :] + n2[:, None, None]
    ang = (2.0 * math.pi / N) * ((k1[None, :, None] * n) % N).astype(F32)
    keep = (k1 < KH)[None, :, None]
    g_re = jnp.where(keep, jnp.cos(ang), 0.0)
    g_im = jnp.where(keep, -jnp.sin(ang), 0.0)
    ga_full = jnp.concatenate([g_re, g_im], axis=1)
    ck = jnp.where((k1 == 0) | (k1 == N1 // 2), 1.0, 2.0) / N
    ga_inv = jnp.swapaxes(ga_full[:, :, :N1 // 2] * jnp.tile(ck, 2)[None, :, None], 1, 2)
    kk = jnp.arange(FFT_N2, dtype=jnp.int32)
    ang2 = (2.0 * math.pi / FFT_N2) * ((kk[:, None] * kk[None, :]) % FFT_N2).astype(F32)
    fr, fi = jnp.cos(ang2), -jnp.sin(ang2)
    fb = jnp.block([[fr, -fi], [fi, fr]])
    fb_inv = jnp.block([[fr, fi], [-fi, fr]])
    return dict(ga_half=ga_full[:, :, :N1 // 2].astype(BF16), ga_full=ga_full.astype(BF16),
                ga_inv=ga_inv.astype(BF16), fb=fb.astype(BF16), fb_inv=fb_inv.astype(BF16))


def _fft_stage_a(x_ref, ga_ref, s_ref, plan, n1_count):
    n_slabs = x_ref.shape[0]

    def body(n2, carry):
        xs = jnp.concatenate([x_ref[s, pl.ds(n2, n1_count, stride=FFT_N2), :] for s in range(n_slabs)], axis=-1)
        a = jnp.dot(ga_ref[n2], xs.astype(BF16), preferred_element_type=F32)
        for s in range(n_slabs):
            s_ref[s, pl.ds(n2 * plan.PA, 2 * plan.KP), :] = a[:, s * LANES:(s + 1) * LANES]
        return carry
    lax.fori_loop(0, FFT_N2, body, 0, unroll=FFT_UNROLL)


def _fft_load_k1(s_ref, k1, plan):
    n_slabs = s_ref.shape[0]
    re = jnp.concatenate([s_ref[s, pl.ds(k1, FFT_N2, stride=plan.PA), :] for s in range(n_slabs)], axis=-1)
    im = jnp.concatenate([s_ref[s, pl.ds(plan.KP + k1, FFT_N2, stride=plan.PA), :] for s in range(n_slabs)], axis=-1)
    return jnp.concatenate([re, im], axis=0).astype(BF16)


def _fft_filter_kernel(k_ref, ga_ref, fb_ref, h_ref, s_ref, *, plan):
    _fft_stage_a(k_ref, ga_ref, s_ref, plan, plan.N1)

    def body(k1, carry):
        h_ref[k1] = jnp.dot(fb_ref[...], _fft_load_k1(s_ref, k1, plan), preferred_element_type=F32).astype(BF16)
        return carry
    lax.fori_loop(0, plan.KH, body, 0)


def _fft_conv_kernel(z_ref, ga_ref, gi_ref, fb_ref, fbi_ref, h_ref, y_ref, s_ref, *, plan):
    zs = z_ref.at[0]
    ys = y_ref.at[0]
    n_slabs = zs.shape[0]
    half = FFT_N2
    _fft_stage_a(zs, ga_ref, s_ref, plan, plan.N1 // 2)

    def body_b(k1, carry):
        x = jnp.dot(fb_ref[...], _fft_load_k1(s_ref, k1, plan), preferred_element_type=F32)
        h = h_ref[k1].astype(F32)
        xr, xi, hr, hi = x[:half], x[half:], h[:half], h[half:]
        y = jnp.concatenate([xr * hr - xi * hi, xr * hi + xi * hr], axis=0).astype(BF16)
        b = jnp.dot(fbi_ref[...], y, preferred_element_type=F32)
        for s in range(n_slabs):
            s_ref[s, pl.ds(k1, FFT_N2, stride=plan.PA), :] = b[:half, s * LANES:(s + 1) * LANES]
            s_ref[s, pl.ds(plan.KP + k1, FFT_N2, stride=plan.PA), :] = b[half:, s * LANES:(s + 1) * LANES]
        return carry
    lax.fori_loop(0, plan.KH, body_b, 0, unroll=3)

    def body_a(n2, carry):
        b = jnp.concatenate([s_ref[s, pl.ds(n2 * plan.PA, 2 * plan.KP), :] for s in range(n_slabs)], axis=-1)
        y = jnp.dot(gi_ref[n2], b.astype(BF16), preferred_element_type=F32)
        for s in range(n_slabs):
            ys[s, pl.ds(n2, plan.N1 // 2, stride=FFT_N2), :] = y[:, s * LANES:(s + 1) * LANES]
        return carry
    lax.fori_loop(0, FFT_N2, body_a, 0, unroll=FFT_UNROLL)


def fft_filter_spectrum(h_fwd, h_bwd, tabs):
    L, C = h_fwd.shape
    plan = _FftPlan(L)
    n_slabs = C // LANES
    k = jnp.concatenate([h_fwd, jnp.zeros((1, C), F32), h_bwd[1:][::-1]], axis=0)
    k = k.reshape(plan.N, n_slabs, LANES).transpose(1, 0, 2)
    full = lambda shape: pl.BlockSpec(shape, lambda i: (0,) * len(shape))
    return pl.pallas_call(
        functools.partial(_fft_filter_kernel, plan=plan),
        grid=(1,),
        in_specs=[full((n_slabs, plan.N, LANES)), full((FFT_N2, 2 * plan.KP, plan.N1)),
                  full((2 * FFT_N2, 2 * FFT_N2))],
        out_specs=full((plan.KH, 2 * FFT_N2, C)),
        out_shape=jax.ShapeDtypeStruct((plan.KH, 2 * FFT_N2, C), BF16),
        scratch_shapes=[pltpu.VMEM((n_slabs, FFT_N2 * plan.PA, LANES), F32)],
        compiler_params=_cparams("arbitrary"),
        name="hyena_filter_fft",
    )(k, tabs["ga_full"], tabs["fb"])


def fft_long_conv(z, h_spec, tabs):
    B, n_slabs, L, _ = z.shape
    plan = _FftPlan(L)
    C = n_slabs * LANES
    full = lambda shape: pl.BlockSpec(shape, lambda b: (0,) * len(shape))
    seq = pl.BlockSpec((1, n_slabs, L, LANES), lambda b: (b, 0, 0, 0))
    return pl.pallas_call(
        functools.partial(_fft_conv_kernel, plan=plan),
        grid=(B,),
        in_specs=[seq, full((FFT_N2, 2 * plan.KP, plan.N1 // 2)), full((FFT_N2, plan.N1 // 2, 2 * plan.KP)),
                  full((2 * FFT_N2, 2 * FFT_N2)), full((2 * FFT_N2, 2 * FFT_N2)),
                  full((plan.KH, 2 * FFT_N2, C))],
        out_specs=seq,
        out_shape=jax.ShapeDtypeStruct((B, n_slabs, L, LANES), F32),
        scratch_shapes=[pltpu.VMEM((n_slabs, FFT_N2 * plan.PA, LANES), F32)],
        compiler_params=_cparams("parallel"),
        name="hyena_fft_conv",
    )(z, tabs["ga_half"], tabs["ga_inv"], tabs["fb"], tabs["fb_inv"], h_spec)


def _hyena_pre_slab_kernel(u_ref, w_ref, b_ref, z_ref, upad):
    L = u_ref.shape[1]
    T = _time_chunk(L)
    C = D_GROUP
    _fill_padded(upad, u_ref, L, T)

    def body(j, carry):
        base = pl.multiple_of(j * T, T)
        for s in range(C // LANES):
            x1 = _dw_conv_slab(upad, C // LANES + s, base, T, w_ref, b_ref, C + s * LANES, HY_SHORT, 1)
            v = _dw_conv_slab(upad, 2 * C // LANES + s, base, T, w_ref, b_ref, 2 * C + s * LANES, HY_SHORT, 1)
            z_ref[0, s, pl.ds(base, T), :] = x1 * v
        return carry
    lax.fori_loop(0, L // T, body, 0)


def _hyena_post_slab_kernel(u0_ref, z_ref, y_ref, w_ref, b_ref, bias_ref, o_ref, upad):
    L = u0_ref.shape[1]
    T = _time_chunk(L)
    C = D_GROUP
    _fill_padded(upad, u0_ref, L, T)

    def body(j, carry):
        base = pl.multiple_of(j * T, T)
        x0 = _short_conv(upad, base, T, w_ref, b_ref, 0, C)
        z = jnp.concatenate([z_ref[0, s, pl.ds(base, T), :] for s in range(C // LANES)], axis=-1)
        y = jnp.concatenate([y_ref[0, s, pl.ds(base, T), :] for s in range(C // LANES)], axis=-1)
        o_ref[0, pl.ds(base, T), :] = x0 * (y + z * bias_ref[...])
        return carry
    lax.fori_loop(0, L // T, body, 0)


def hyena_mixer_fft(u, lp, tabs):
    B, L, _ = u.shape
    C = D_GROUP
    n_slabs = C // LANES
    w, bsh = lp["hy_short_w"], lp["hy_short_b"].reshape(1, 3 * C)
    useq = pl.BlockSpec((1, L, 3 * C), lambda b: (b, 0, 0))
    slabs = pl.BlockSpec((1, n_slabs, L, LANES), lambda b: (b, 0, 0, 0))
    wspec = pl.BlockSpec((HY_SHORT, 3 * C), lambda b: (0, 0))
    bspec = pl.BlockSpec((1, 3 * C), lambda b: (0, 0))
    pad_scratch = pltpu.VMEM((3 * C // LANES, L + 2 * CONV_MARGIN, LANES), F32)
    z = pl.pallas_call(
        _hyena_pre_slab_kernel,
        grid=(B,),
        in_specs=[useq, wspec, bspec],
        out_specs=slabs,
        out_shape=jax.ShapeDtypeStruct((B, n_slabs, L, LANES), F32),
        scratch_shapes=[pad_scratch],
        compiler_params=_cparams("parallel"),
        name="hyena_pre",
    )(u, w, bsh)
    h_fwd, h_bwd = _hyena_filters(L, lp)
    y = fft_long_conv(z, fft_filter_spectrum(h_fwd, h_bwd, tabs), tabs)
    return pl.pallas_call(
        _hyena_post_slab_kernel,
        grid=(B,),
        in_specs=[pl.BlockSpec((1, L, C), lambda b: (b, 0, 0)), slabs, slabs, wspec, bspec,
                  pl.BlockSpec((1, C), lambda b: (0, 0))],
        out_specs=pl.BlockSpec((1, L, C), lambda b: (b, 0, 0)),
        out_shape=jax.ShapeDtypeStruct((B, L, C), F32),
        scratch_shapes=[pltpu.VMEM((n_slabs, L + 2 * CONV_MARGIN, LANES), F32)],
        compiler_params=_cparams("parallel"),
        name="hyena_post",
    )(u, z, y, w, bsh, lp["hy_bias"].reshape(1, C))


def _hyena_filters(L, lp):
    t = jnp.linspace(0.0, 1.0, L, dtype=F32)[:, None]
    bands = (HY_EMB - 1) // 2
    w = 2.0 * math.pi * jnp.arange(L, dtype=F32)[:, None] / L
    f = jnp.linspace(1e-4, bands - 1, bands, dtype=F32)[None]
    z = jnp.concatenate([t, jnp.cos(f * w), -jnp.sin(f * w)], axis=-1)
    hdn = jnp.sin(z @ lp["hy_ffn_w1"] + lp["hy_ffn_b1"])
    hdn = jnp.sin(hdn @ lp["hy_ffn_w2"] + lp["hy_ffn_b2"])
    h = (hdn @ lp["hy_ffn_w3"]).reshape(L, 2, D_GROUP)
    max_decay = math.log(HY_TARGET) / HY_FAST_DECAY
    min_decay = math.log(HY_TARGET) / HY_SLOW_DECAY
    deltas = jnp.linspace(min_decay, max_decay, D_GROUP, dtype=F32)
    h = h * jnp.exp(-t * jnp.abs(deltas))[:, None, :]
    h = h / (jnp.sum(jnp.abs(h), axis=(0, 1), keepdims=True) + EPS)
    return h[:, 0], h[:, 1]


def _layer(hc, hx, c_silu_all, lp, need_ctx, final_g, final_norm, tables_x, tables_c, experts, layer):
    B, S, D = hx.shape
    C = hc.shape[1]
    mod = small_linear(c_silu_all, lp["ada_w"], lp["ada_b"])
    mod_x = mod[:B].reshape(B, 6, 1, D)
    mod_c = jnp.broadcast_to(mod[B].reshape(1, 6, 1, D), (B, 6, 1, D))
    w_ext = extend_w_in(lp["w_in"])
    cos_x, sin_x = rope_tables(S, True)
    cos_c, sin_c = rope_tables(C, False)
    hy_x, cf_x, at_x, lr_x = in_proj(hx, mod_x[:, 0], mod_x[:, 1], lp["norm1_g"], w_ext, cos_x, sin_x, tm=512)
    hy_c, cf_c, at_c, lr_c = in_proj(hc, mod_c[:, 0], mod_c[:, 1], lp["norm1_g"], w_ext, cos_c, sin_c, tm=256)

    yd_c, yd_x = rglru_mixer(lr_c, lr_x, lp, need_ctx)
    conf = lambda u: conformer_conv(u, lp["conf_dw_w"], lp["conf_dw_b"], lp["conf_ln_g"], lp["conf_ln_b"])
    ys_x = [hyena_mixer_fft(hy_x, lp, tables_x), conf(cf_x),
            window_attention(at_x, at_c, lp["attn_sink"]), yd_x]

    w_out = lp["w_out"].astype(BF16)
    w_router = jnp.zeros((D, ROUTER_COLS), F32)
    w_router = w_router.at[:, :N_GROUPS].set(lp["router_g_w"]).at[:, N_GROUPS:N_GROUPS + N_EXPERTS].set(lp["router_e_w"])
    w_router = w_router.astype(BF16)
    b_router = jnp.zeros((1, ROUTER_COLS), F32)
    b_router = b_router.at[0, :N_GROUPS].set(lp["router_g_b"]).at[0, N_GROUPS:N_GROUPS + N_EXPERTS].set(lp["router_e_b"])

    hx1, lg_x = out_proj(ys_x, hx, mod_x[:, 2], lp["group_norm_g"], w_out, lp["norm2_g"],
                         mod_x[:, 3], mod_x[:, 4], w_router, b_router, tm=512)
    h_tok = hx1.reshape(B * S, D)
    hc_tok = None
    lg = lg_x.reshape(B * S, ROUTER_COLS)
    if need_ctx:
        ys_c = [hyena_mixer(hy_c, lp, tables_c), conf(cf_c),
                context_attention(at_c, lp["attn_sink"]), yd_c]
        hc1, lg_c = out_proj(ys_c, hc, mod_c[:, 2], lp["group_norm_g"], w_out, lp["norm2_g"],
                             mod_c[:, 3], mod_c[:, 4], w_router, b_router, tm=256)
        hc_tok = hc1.reshape(B * C, D)
        lg = jnp.concatenate([lg, lg_c.reshape(B * C, ROUTER_COLS)], axis=0)

    T = lg.shape[0]
    n_blocks = -(-T // MOE_BLOCK) + N_CLASSES
    info, counts, ids = route_tokens(lg)
    dest, blk_a, blk_b, n_used = slot_plan(ids, counts, n_blocks)
    xs = moe_dispatch(h_tok, hc_tok, dest, n_blocks, lp["norm2_g"], (mod_x[:, 3], mod_x[:, 4]),
                      (mod_c[:, 3], mod_c[:, 4]), S)
    o_sorted = expert_pairs(xs, blk_a, blk_b, n_used, *experts, layer)
    hx2 = moe_collect(o_sorted, dest, info, 0, h_tok, mod_x[:, 5], S, final_g, final_norm)
    hx2 = hx2.reshape(B, S, D)
    if need_ctx:
        hc2 = moe_collect(o_sorted, dest, info, B * S // DISPATCH_TOKENS, hc_tok, mod_c[:, 5], C,
                          final_g, False).reshape(B, C, D)
    else:
        hc2 = hc
    return hc2, hx2


def kernel(x, c, ctx, c_ctx, norm1_g, norm2_g, ada_w, ada_b, w_in, hy_short_w, hy_short_b, hy_ffn_w1, hy_ffn_b1, hy_ffn_w2, hy_ffn_b2, hy_ffn_w3, hy_bias, conf_dw_w, conf_dw_b, conf_ln_g, conf_ln_b, attn_sink, lru_conv_w, lru_conv_b, lru_wa, lru_ba, lru_wx, lru_bx, lru_lambda, group_norm_g, w_out, router_g_w, router_g_b, router_e_w, router_e_b, exp_w_gate, exp_w_up, exp_w_down, final_norm_g):
    stacked = dict(norm1_g=norm1_g, norm2_g=norm2_g, ada_w=ada_w, ada_b=ada_b, w_in=w_in,
                   hy_short_w=hy_short_w, hy_short_b=hy_short_b, hy_ffn_w1=hy_ffn_w1, hy_ffn_b1=hy_ffn_b1,
                   hy_ffn_w2=hy_ffn_w2, hy_ffn_b2=hy_ffn_b2, hy_ffn_w3=hy_ffn_w3, hy_bias=hy_bias,
                   conf_dw_w=conf_dw_w, conf_dw_b=conf_dw_b, conf_ln_g=conf_ln_g, conf_ln_b=conf_ln_b,
                   attn_sink=attn_sink, lru_conv_w=lru_conv_w, lru_conv_b=lru_conv_b, lru_wa=lru_wa,
                   lru_ba=lru_ba, lru_wx=lru_wx, lru_bx=lru_bx, lru_lambda=lru_lambda,
                   group_norm_g=group_norm_g, w_out=w_out, router_g_w=router_g_w, router_g_b=router_g_b,
                   router_e_w=router_e_w, router_e_b=router_e_b)
    experts = (exp_w_gate.astype(BF16), exp_w_up.astype(BF16), exp_w_down.astype(BF16))
    depth = norm1_g.shape[0]
    B = x.shape[0]
    cs = jnp.concatenate([jax.nn.silu(c), jnp.broadcast_to(jax.nn.silu(c_ctx)[None], (8, c.shape[1]))], axis=0)
    hc, hx = ctx, x
    tables_x = fft_tables(x.shape[1])
    tables_c = dft_tables(ctx.shape[1])
    for l in range(depth):
        lp = {k: v[l] for k, v in stacked.items()}
        hc, hx = _layer(hc, hx, cs, lp, need_ctx=(l < depth - 1), final_g=final_norm_g,
                        final_norm=(l == depth - 1), tables_x=tables_x, tables_c=tables_c,
                        experts=experts, layer=l)
    return hx
```

```python
import functools
import math

import jax
import jax.numpy as jnp
from jax import lax
from jax.experimental import pallas as pl
from jax.experimental.pallas import tpu as pltpu

F32 = jnp.float32
BF16 = jnp.bfloat16

EPS = 1e-6
NEG_INF = -1e30
GRID_W = 64
N_MIXERS = 4
D_GROUP = 256
HY_COLS = 3 * D_GROUP
CONF_COLS = 2 * D_GROUP
ATT_HEADS = 4
ATT_KV_HEADS = 2
HEAD_DIM = 64
ATT_COLS = (ATT_HEADS + 2 * ATT_KV_HEADS) * HEAD_DIM
LRU_COLS = 2 * D_GROUP
QK_COLS = (ATT_HEADS + ATT_KV_HEADS) * HEAD_DIM
WINDOW = 128
ATT_BLOCK = 128
ROPE_BASE = 10000.0
HY_EMB = 33
HY_FAST_DECAY = 0.3
HY_SLOW_DECAY = 1.5
HY_TARGET = 1e-2
CONF_KERNEL = 31
LRU_HEADS = 4
LRU_CONV = 4
LRU_C = 8.0
N_GROUPS = 4
EXP_PER_GROUP = 8
N_EXPERTS = N_GROUPS * EXP_PER_GROUP
TOP_K = 2
MOE_BLOCK = 256
ROUTER_COLS = 128

VMEM_LIMIT_BYTES = 56 * 1024 * 1024


def _cparams(*sem):
    return pltpu.CompilerParams(dimension_semantics=sem, vmem_limit_bytes=VMEM_LIMIT_BYTES)


def _linear_kernel(x_ref, w_ref, b_ref, o_ref):
    o_ref[...] = jnp.dot(x_ref[...], w_ref[...], preferred_element_type=F32,
                         precision=lax.Precision.HIGHEST) + b_ref[...]


def small_linear(x, w, b, tn=1024):
    M, K = x.shape
    N = w.shape[1]
    return pl.pallas_call(
        _linear_kernel,
        grid=(N // tn,),
        in_specs=[pl.BlockSpec((M, K), lambda j: (0, 0)),
                  pl.BlockSpec((K, tn), lambda j: (0, j)),
                  pl.BlockSpec((1, tn), lambda j: (0, j))],
        out_specs=pl.BlockSpec((M, tn), lambda j: (0, j)),
        out_shape=jax.ShapeDtypeStruct((M, N), F32),
        compiler_params=_cparams("parallel"),
        name="ada_linear",
    )(x, w, b.reshape(1, N))


def _in_proj_kernel(x_ref, sh_ref, sc_ref, g_ref, w_ref, cos_ref, sin_ref,
                    hy_ref, cf_ref, at_ref, lr_ref):
    x = x_ref[0]
    ms = jnp.mean(x * x, axis=-1, keepdims=True)
    y = x * lax.rsqrt(ms + EPS) * g_ref[...]
    y = y * (1.0 + sc_ref[0]) + sh_ref[0]
    u = jnp.dot(y.astype(BF16), w_ref[...], preferred_element_type=F32)
    c0 = HY_COLS
    c1 = c0 + CONF_COLS
    c2 = c1 + ATT_COLS
    c3 = c2 + LRU_COLS
    hy_ref[0] = u[:, :c0]
    cf_ref[0] = u[:, c0:c1]
    lr_ref[0] = u[:, c2:c3]
    qk = u[:, c1:c1 + QK_COLS]
    qk_rot = u[:, c3:c3 + QK_COLS]
    at_ref[0, :, :QK_COLS] = qk * cos_ref[...] + qk_rot * sin_ref[...]
    at_ref[0, :, QK_COLS:] = u[:, c1 + QK_COLS:c2]


def in_proj(h, shift, scale, g, w_ext, cos_t, sin_t, tm):
    B, L, D = h.shape
    NW = w_ext.shape[1]
    outs = [HY_COLS, CONF_COLS, ATT_COLS, LRU_COLS]
    return pl.pallas_call(
        _in_proj_kernel,
        grid=(B, L // tm),
        in_specs=[pl.BlockSpec((1, tm, D), lambda b, i: (b, i, 0)),
                  pl.BlockSpec((1, 1, D), lambda b, i: (b, 0, 0)),
                  pl.BlockSpec((1, 1, D), lambda b, i: (b, 0, 0)),
                  pl.BlockSpec((1, D), lambda b, i: (0, 0)),
                  pl.BlockSpec((D, NW), lambda b, i: (0, 0)),
                  pl.BlockSpec((tm, QK_COLS), lambda b, i: (i, 0)),
                  pl.BlockSpec((tm, QK_COLS), lambda b, i: (i, 0))],
        out_specs=[pl.BlockSpec((1, tm, n), lambda b, i: (b, i, 0)) for n in outs],
        out_shape=[jax.ShapeDtypeStruct((B, L, n), F32) for n in outs],
        compiler_params=_cparams("parallel", "parallel"),
        name="in_proj",
    )(h, shift, scale, g.reshape(1, D), w_ext, cos_t, sin_t)


def rope_tables(L, rotary):
    n_heads = ATT_HEADS + ATT_KV_HEADS
    if not rotary:
        return jnp.ones((L, QK_COLS), F32), jnp.zeros((L, QK_COLS), F32)
    pos = jnp.arange(L)
    row = (pos // GRID_W).astype(F32)
    col = (pos % GRID_W).astype(F32)
    half = HEAD_DIM // 2
    inv_freq = ROPE_BASE ** (-jnp.arange(0, half, 2, dtype=F32) / half)
    ang_r = row[:, None] * inv_freq[None]
    ang_c = col[:, None] * inv_freq[None]
    cos_h = jnp.concatenate([jnp.cos(ang_r)] * 2 + [jnp.cos(ang_c)] * 2, axis=-1)
    sin_h = jnp.concatenate([jnp.sin(ang_r)] * 2 + [jnp.sin(ang_c)] * 2, axis=-1)
    return jnp.tile(cos_h, (1, n_heads)), jnp.tile(sin_h, (1, n_heads))


def extend_w_in(w_in):
    c1 = HY_COLS + CONF_COLS
    wqk = w_in[:, c1:c1 + QK_COLS]
    D = w_in.shape[0]
    w4 = wqk.reshape(D, QK_COLS // 32, 2, 16)
    wrot = jnp.stack([-w4[:, :, 1], w4[:, :, 0]], axis=2).reshape(D, QK_COLS)
    return jnp.concatenate([w_in, wrot], axis=1).astype(BF16)


def _softmax_parts(q, k_list, extra_logit):
    scale = HEAD_DIM ** -0.5
    s_list = []
    for k, mask in k_list:
        s = lax.dot_general(q, k, (((1,), (1,)), ((), ())), preferred_element_type=F32) * scale
        if mask is not None:
            s = jnp.where(mask, s, NEG_INF)
        s_list.append(s)
    m = extra_logit
    for s in s_list:
        m = jnp.maximum(m, jnp.max(s, axis=-1, keepdims=True))
    p_list = [jnp.exp(s - m) for s in s_list]
    denom = jnp.exp(extra_logit - m)
    for p in p_list:
        denom = denom + jnp.sum(p, axis=-1, keepdims=True)
    return p_list, 1.0 / denom


ATT_Q_BLOCKS = 8


def _win_attn_kernel(sink_ref, q_ref, kp_ref, kc_ref, kn_ref, vp_ref, vc_ref, vn_ref,
                     kx_ref, vx_ref, o_ref, *, seq_len):
    i = pl.program_id(1)
    blk = ATT_BLOCK
    qb = q_ref.shape[1] // blk
    scale = HEAD_DIM ** -0.5
    g = ATT_HEADS // ATT_KV_HEADS
    kw = jnp.concatenate([kp_ref[0], kc_ref[0], kn_ref[0]], axis=0)
    vw = jnp.concatenate([vp_ref[0], vc_ref[0], vn_ref[0]], axis=0).astype(BF16)
    kwt = kw.T.astype(BF16)
    kxt = kx_ref[0].T.astype(BF16)
    vx = vx_ref[0].astype(BF16)
    row = lax.broadcasted_iota(jnp.int32, (g * blk, 3 * blk), 0) % blk
    col = lax.broadcasted_iota(jnp.int32, (g * blk, 3 * blk), 1)
    band_bias = jnp.where(jnp.abs(col - blk - row) <= WINDOW, 0.0, NEG_INF)
    col1 = lax.broadcasted_iota(jnp.int32, (1, 3 * blk), 1)
    for j in range(qb):
        q_blk = i * qb + j
        k_pos = (q_blk - 1) * blk + col1
        edge_bias = jnp.where(k_pos >= 0, jnp.where(k_pos < seq_len, 0.0, NEG_INF), NEG_INF)
        bias = band_bias + edge_bias
        outs = []
        for kv in range(ATT_KV_HEADS):
            ksl = slice(kv * HEAD_DIM, (kv + 1) * HEAD_DIM)
            heads = range(kv * g, (kv + 1) * g)
            qs = (jnp.concatenate([q_ref[0, j * blk:(j + 1) * blk, h * HEAD_DIM:(h + 1) * HEAD_DIM]
                                   for h in heads], axis=0) * scale).astype(BF16)
            sink = jnp.concatenate([jnp.full((blk, 1), sink_ref[h], F32) for h in heads], axis=0)
            s_win = jnp.dot(qs, kwt[ksl, j * blk:(j + 3) * blk], preferred_element_type=F32) + bias
            s_ctx = jnp.dot(qs, kxt[ksl, :], preferred_element_type=F32)
            m = jnp.maximum(jnp.maximum(jnp.max(s_win, axis=-1, keepdims=True),
                                        jnp.max(s_ctx, axis=-1, keepdims=True)), sink)
            p_win = jnp.exp(s_win - m)
            p_ctx = jnp.exp(s_ctx - m)
            denom = (jnp.exp(sink - m) + jnp.sum(p_win, axis=-1, keepdims=True)
                     + jnp.sum(p_ctx, axis=-1, keepdims=True))
            o = (jnp.dot(p_win.astype(BF16), vw[j * blk:(j + 3) * blk, ksl], preferred_element_type=F32)
                 + jnp.dot(p_ctx.astype(BF16), vx[:, ksl], preferred_element_type=F32)) * (1.0 / denom)
            outs.extend([o[k * blk:(k + 1) * blk] for k in range(g)])
        o_ref[0, j * blk:(j + 1) * blk, :] = jnp.concatenate(outs, axis=-1)


def window_attention(at_x, at_c, sink):
    B, S, _ = at_x.shape
    C = at_c.shape[1]
    blk = ATT_BLOCK
    qb = ATT_Q_BLOCKS
    nb = S // blk
    kcol = QK_COLS // 128 - 1
    vcol = kcol + 1

    def edge_spec(col, off):
        return pl.BlockSpec((1, blk, 128), lambda b, i, s: (b, jnp.clip(i * qb + off, 0, nb - 1), col))

    def mid_spec(col):
        return pl.BlockSpec((1, qb * blk, 128), lambda b, i, s: (b, i, col))

    grid_spec = pltpu.PrefetchScalarGridSpec(
        num_scalar_prefetch=1,
        grid=(B, nb // qb),
        in_specs=[pl.BlockSpec((1, qb * blk, ATT_HEADS * HEAD_DIM), lambda b, i, s: (b, i, 0)),
                  edge_spec(kcol, -1), mid_spec(kcol), edge_spec(kcol, qb),
                  edge_spec(vcol, -1), mid_spec(vcol), edge_spec(vcol, qb),
                  pl.BlockSpec((1, C, 128), lambda b, i, s: (b, 0, kcol)),
                  pl.BlockSpec((1, C, 128), lambda b, i, s: (b, 0, vcol))],
        out_specs=pl.BlockSpec((1, qb * blk, ATT_HEADS * HEAD_DIM), lambda b, i, s: (b, i, 0)),
    )
    return pl.pallas_call(
        functools.partial(_win_attn_kernel, seq_len=S),
        grid_spec=grid_spec,
        out_shape=jax.ShapeDtypeStruct((B, S, ATT_HEADS * HEAD_DIM), F32),
        compiler_params=_cparams("parallel", "parallel"),
        name="window_attention",
    )(sink.astype(F32), at_x, at_x, at_x, at_x, at_x, at_x, at_x, at_c, at_c)


def _ctx_attn_kernel(sink_ref, q_ref, kx_ref, vx_ref, o_ref):
    q = q_ref[0].astype(BF16)
    kx = kx_ref[0].astype(BF16)
    vx = vx_ref[0].astype(BF16)
    g = ATT_HEADS // ATT_KV_HEADS
    outs = []
    for h in range(ATT_HEADS):
        kv = h // g
        qs = q[:, h * HEAD_DIM:(h + 1) * HEAD_DIM]
        ksl = slice(kv * HEAD_DIM, (kv + 1) * HEAD_DIM)
        (p_ctx,), inv = _softmax_parts(qs, [(kx[:, ksl], None)], sink_ref[h])
        outs.append(jnp.dot(p_ctx.astype(BF16), vx[:, ksl], preferred_element_type=F32) * inv)
    o_ref[0] = jnp.concatenate(outs, axis=-1)


def context_attention(at_c, sink):
    B, C, _ = at_c.shape
    kcol = QK_COLS // 128 - 1
    grid_spec = pltpu.PrefetchScalarGridSpec(
        num_scalar_prefetch=1,
        grid=(B,),
        in_specs=[pl.BlockSpec((1, C, ATT_HEADS * HEAD_DIM), lambda b, s: (b, 0, 0)),
                  pl.BlockSpec((1, C, 128), lambda b, s: (b, 0, kcol)),
                  pl.BlockSpec((1, C, 128), lambda b, s: (b, 0, kcol + 1))],
        out_specs=pl.BlockSpec((1, C, ATT_HEADS * HEAD_DIM), lambda b, s: (b, 0, 0)),
    )
    return pl.pallas_call(
        _ctx_attn_kernel,
        grid_spec=grid_spec,
        out_shape=jax.ShapeDtypeStruct((B, C, ATT_HEADS * HEAD_DIM), F32),
        compiler_params=_cparams("parallel"),
        name="context_attention",
    )(sink.astype(F32), at_c, at_c, at_c)


def _out_proj_kernel(y0_ref, y1_ref, y2_ref, y3_ref, h_ref, g1_ref, gng_ref, w_ref,
                     n2g_ref, sh_ref, sc_ref, wr_ref, br_ref, ho_ref, lg_ref):
    parts = []
    for k, y_ref in enumerate((y0_ref, y1_ref, y2_ref, y3_ref)):
        y = y_ref[0]
        ms = jnp.mean(y * y, axis=-1, keepdims=True)
        yn = y * lax.rsqrt(ms + EPS) * gng_ref[:, k * D_GROUP:(k + 1) * D_GROUP]
        parts.append(yn.astype(BF16))
    yn = jnp.concatenate(parts, axis=-1)
    proj = jnp.dot(yn, w_ref[...], preferred_element_type=F32)
    h = h_ref[0] + g1_ref[0] * proj
    ho_ref[0] = h
    ms = jnp.mean(h * h, axis=-1, keepdims=True)
    n = h * lax.rsqrt(ms + EPS) * n2g_ref[...]
    n = n * (1.0 + sc_ref[0]) + sh_ref[0]
    lg_ref[0] = jnp.dot(n.astype(BF16), wr_ref[...], preferred_element_type=F32) + br_ref[...]


def out_proj(ys, h, g1, gng, w_out, n2g, sh2, sc2, w_router, b_router, tm):
    B, L, D = h.shape
    row3 = lambda n: pl.BlockSpec((1, tm, n), lambda b, i: (b, i, 0))
    mod = pl.BlockSpec((1, 1, D), lambda b, i: (b, 0, 0))
    full = lambda r, c: pl.BlockSpec((r, c), lambda b, i: (0, 0))
    return pl.pallas_call(
        _out_proj_kernel,
        grid=(B, L // tm),
        in_specs=[row3(D_GROUP)] * 4 + [row3(D), mod, full(1, D), full(D, D), full(1, D), mod, mod,
                                        full(D, ROUTER_COLS), full(1, ROUTER_COLS)],
        out_specs=[row3(D), row3(ROUTER_COLS)],
        out_shape=[jax.ShapeDtypeStruct((B, L, D), F32), jax.ShapeDtypeStruct((B, L, ROUTER_COLS), F32)],
        compiler_params=_cparams("parallel", "parallel"),
        name="out_proj",
    )(*ys, h, g1, gng.reshape(1, D), w_out, n2g.reshape(1, D), sh2, sc2, w_router, b_router)


N_PAIRS = EXP_PER_GROUP * (EXP_PER_GROUP - 1) // 2
N_CLASSES = N_GROUPS * N_PAIRS
ROUTE_TOKENS = 512
INFO_CLASS, INFO_RANK, INFO_WA, INFO_WB = 0, 1, 2, 3


SUBLANES = 8


def _route_kernel(lg_ref, below_ref, info_ref, cnt_ref, ids_ref, run):
    i = pl.program_id(0)

    @pl.when(i == 0)
    def _():
        run[...] = jnp.zeros_like(run)

    lg = lg_ref[...]
    li = lax.broadcasted_iota(jnp.int32, lg.shape, 1).astype(F32)
    big = float(ROUTER_COLS)

    def first_argmax(vals):
        m = jnp.max(vals, axis=-1, keepdims=True)
        return m, jnp.min(jnp.where(vals == m, li, big), axis=-1, keepdims=True)

    gl = jnp.where(li < N_GROUPS, lg, NEG_INF)
    gmax, g_idx = first_argmax(gl)
    g_prob = 1.0 / jnp.sum(jnp.exp(gl - gmax), axis=-1, keepdims=True)
    lo = N_GROUPS + EXP_PER_GROUP * g_idx
    el = jnp.where(li >= lo, jnp.where(li < lo + EXP_PER_GROUP, lg, NEG_INF), NEG_INF)
    m1, i1 = first_argmax(el)
    m2, i2 = first_argmax(jnp.where(li == i1, NEG_INF, el))
    e2 = jnp.exp(m2 - m1)
    w1 = g_prob / (1.0 + e2)
    w2 = g_prob * e2 / (1.0 + e2)
    j1 = i1 - lo
    j2 = i2 - lo
    a = jnp.minimum(j1, j2)
    b = jnp.maximum(j1, j2)
    cls = g_idx * N_PAIRS + (a * (2 * EXP_PER_GROUP - 1 - a)) * 0.5 + (b - a - 1.0)
    w_a = jnp.where(j1 < j2, w1, w2)
    w_b = jnp.where(j1 < j2, w2, w1)

    hit = li == cls
    onehot = jnp.where(hit, 1.0, 0.0)
    before = jnp.dot(below_ref[...], onehot.astype(BF16), preferred_element_type=F32)
    rank = jnp.sum(jnp.where(hit, before + run[...], 0.0), axis=-1, keepdims=True)
    run[...] = run[...] + jnp.sum(onehot, axis=0, keepdims=True)
    cnt_ref[...] = run[...]
    info = jnp.where(li == INFO_CLASS, cls, 0.0)
    info = jnp.where(li == INFO_RANK, rank, info)
    info = jnp.where(li == INFO_WA, w_a, info)
    info = jnp.where(li == INFO_WB, w_b, info)
    info_ref[...] = info
    ids_ref[0] = info.T[:SUBLANES].astype(jnp.int32)


def route_tokens(logits):
    T = logits.shape[0]
    tb = ROUTE_TOKENS
    below = (jnp.arange(tb)[None, :] < jnp.arange(tb)[:, None]).astype(BF16)
    return pl.pallas_call(
        _route_kernel,
        grid=(T // tb,),
        in_specs=[pl.BlockSpec((tb, ROUTER_COLS), lambda i: (i, 0)),
                  pl.BlockSpec((tb, tb), lambda i: (0, 0))],
        out_specs=[pl.BlockSpec((tb, ROUTER_COLS), lambda i: (i, 0)),
                   pl.BlockSpec((1, ROUTER_COLS), lambda i: (0, 0)),
                   pl.BlockSpec((1, SUBLANES, tb), lambda i: (i, 0, 0))],
        out_shape=[jax.ShapeDtypeStruct((T, ROUTER_COLS), F32), jax.ShapeDtypeStruct((1, ROUTER_COLS), F32),
                   jax.ShapeDtypeStruct((T // tb, SUBLANES, tb), jnp.int32)],
        scratch_shapes=[pltpu.VMEM((1, ROUTER_COLS), F32)],
        compiler_params=_cparams("arbitrary"),
        name="moe_route",
    )(logits, below)


def _pair_tables():
    a_tab, b_tab = [], []
    for g in range(N_GROUPS):
        for a in range(EXP_PER_GROUP):
            for b in range(a + 1, EXP_PER_GROUP):
                a_tab.append(g * EXP_PER_GROUP + a)
                b_tab.append(g * EXP_PER_GROUP + b)
    return jnp.array(a_tab, jnp.int32), jnp.array(b_tab, jnp.int32)


def _slot_kernel(ids_ref, start_ref, dest_ref):
    cls = ids_ref[0, INFO_CLASS:INFO_CLASS + 1, :]
    rank = ids_ref[0, INFO_RANK:INFO_RANK + 1, :]
    ci = lax.broadcasted_iota(jnp.int32, (ROUTER_COLS, cls.shape[1]), 0)
    start = jnp.sum(jnp.where(ci == cls, start_ref[...], 0), axis=0, keepdims=True)
    dest_ref[0] = jnp.broadcast_to(start + rank, dest_ref.shape[1:])


def slot_plan(ids, counts, n_blocks):
    nt, _, tb = ids.shape
    cnt = counts[0, :N_CLASSES].astype(jnp.int32)
    padded = (cnt + MOE_BLOCK - 1) // MOE_BLOCK * MOE_BLOCK
    upto = jnp.arange(N_CLASSES)[None, :] <= jnp.arange(N_CLASSES)[:, None]
    pad_end = jnp.sum(jnp.where(upto, padded[None, :], 0), axis=1)
    class_start = jnp.zeros((ROUTER_COLS, 1), jnp.int32).at[:N_CLASSES, 0].set(pad_end - padded)
    dest = pl.pallas_call(
        _slot_kernel,
        grid=(nt,),
        in_specs=[pl.BlockSpec((1, SUBLANES, tb), lambda i: (i, 0, 0)),
                  pl.BlockSpec((ROUTER_COLS, 1), lambda i: (0, 0))],
        out_specs=pl.BlockSpec((1, SUBLANES, tb), lambda i: (i, 0, 0)),
        out_shape=jax.ShapeDtypeStruct((nt, SUBLANES, tb), jnp.int32),
        compiler_params=_cparams("parallel"),
        name="moe_slots",
    )(ids, class_start)[:, 0, :].reshape(nt * tb)
    n_used = (pad_end[-1] // MOE_BLOCK).astype(jnp.int32).reshape(1)
    blk_first = jnp.arange(n_blocks, dtype=jnp.int32) * MOE_BLOCK
    blk_cls = jnp.minimum(jnp.sum((pad_end[None, :] <= blk_first[:, None]).astype(jnp.int32), axis=1),
                          N_CLASSES - 1)
    a_tab, b_tab = _pair_tables()
    hit = blk_cls[:, None] == jnp.arange(N_CLASSES)[None, :]
    pick = lambda tab: jnp.sum(jnp.where(hit, tab[None, :], 0), axis=1).astype(jnp.int32)
    return dest, pick(a_tab), pick(b_tab), n_used


DISPATCH_TOKENS = 512


def _wait_rows(buf, sem):
    pltpu.make_async_copy(buf, buf, sem).wait()


DMA_UNROLL = 8
TOKEN_TILE_ROWS = 8


def _store_token_tiles(tiles_ref, offset, pitch, x):
    n = x.shape[0]
    for j in range(x.shape[1] // LANES):
        tiles_ref[pl.ds(offset + j, n, stride=pitch), :] = x[:, j * LANES:(j + 1) * LANES]


def _load_token_tiles(tiles_ref, offset, pitch, n, width):
    return jnp.concatenate([tiles_ref[pl.ds(offset + j, n, stride=pitch), :] for j in range(width // LANES)],
                           axis=-1)


def _dispatch_kernel(dest_ref, hx_ref, hc_ref, g_ref, shx_ref, scx_ref, shc_ref, scc_ref, zeros_hbm,
                     xs_hbm, rows, sems, *, n_latent_blocks):
    del zeros_hbm
    i = pl.program_id(0)
    n = pl.num_programs(0)
    slot = i % 2
    tb = hx_ref.shape[0]

    @pl.when(i >= 2)
    def _():
        _wait_rows(rows.at[slot], sems.at[slot])

    def normed(h_ref, sh_ref, sc_ref):
        h = h_ref[...]
        ms = jnp.mean(h * h, axis=-1, keepdims=True)
        return h * lax.rsqrt(ms + EPS) * g_ref[...] * (1.0 + sc_ref[0]) + sh_ref[0]

    @pl.when(i < n_latent_blocks)
    def _():
        _store_token_tiles(rows.at[slot], 0, TOKEN_TILE_ROWS, normed(hx_ref, shx_ref, scx_ref))

    @pl.when(i >= n_latent_blocks)
    def _():
        _store_token_tiles(rows.at[slot], 0, TOKEN_TILE_ROWS, normed(hc_ref, shc_ref, scc_ref))

    def body(g, carry):
        for u in range(DMA_UNROLL):
            r = g * DMA_UNROLL + u
            dst = pl.multiple_of(dest_ref[0, 0, r] * TOKEN_TILE_ROWS, TOKEN_TILE_ROWS)
            pltpu.make_async_copy(rows.at[slot, pl.ds(r * TOKEN_TILE_ROWS, TOKEN_TILE_ROWS)],
                                  xs_hbm.at[pl.ds(dst, TOKEN_TILE_ROWS)], sems.at[slot]).start(priority=u % 2)
        return carry
    lax.fori_loop(0, tb // DMA_UNROLL, body, 0)

    @pl.when(i == n - 1)
    def _():
        _wait_rows(rows.at[slot], sems.at[slot])

        @pl.when(n >= 2)
        def _():
            _wait_rows(rows.at[1 - slot], sems.at[1 - slot])


def moe_dispatch(h_x, h_c, dest, n_blocks, n2g, mod_x, mod_c, tokens_per_batch):
    Tx, D = h_x.shape
    tb = DISPATCH_TOKENS
    nxb = Tx // tb
    if h_c is None:
        h_c, mod_c, ncb = h_x, mod_x, 0
    else:
        ncb = h_c.shape[0] // tb
    per_b = tokens_per_batch // tb
    P = n_blocks * MOE_BLOCK
    tile_rows = D // LANES
    assert tile_rows == TOKEN_TILE_ROWS
    xi = lambda i: jnp.minimum(i, nxb - 1)
    ci = lambda i: jnp.maximum(i - nxb, 0)
    modx = pl.BlockSpec((1, 1, D), lambda i: (xi(i) // per_b, 0, 0))
    modc = pl.BlockSpec((1, 1, D), lambda i: (0, 0, 0))
    return pl.pallas_call(
        functools.partial(_dispatch_kernel, n_latent_blocks=nxb),
        grid=(nxb + ncb,),
        in_specs=[pl.BlockSpec((1, 1, tb), lambda i: (i, 0, 0), memory_space=pltpu.SMEM),
                  pl.BlockSpec((tb, D), lambda i: (xi(i), 0)),
                  pl.BlockSpec((tb, D), lambda i: (ci(i), 0)),
                  pl.BlockSpec((1, D), lambda i: (0, 0)),
                  modx, modx, modc, modc,
                  pl.BlockSpec(memory_space=pl.ANY)],
        out_specs=pl.BlockSpec(memory_space=pl.ANY),
        out_shape=jax.ShapeDtypeStruct((P * tile_rows, LANES), F32),
        scratch_shapes=[pltpu.VMEM((2, tb * tile_rows, LANES), F32), pltpu.SemaphoreType.DMA((2,))],
        input_output_aliases={8: 0},
        compiler_params=_cparams("arbitrary"),
        name="moe_dispatch",
    )(dest.reshape(-1, 1, tb), h_x, h_c, n2g.reshape(1, D), mod_x[0], mod_x[1], mod_c[0], mod_c[1],
      jnp.zeros((P * tile_rows, LANES), F32))


def _expert_pair_kernel(ea_ref, eb_ref, nused_ref, xs_ref, wga_ref, wua_ref, wda_ref, wgb_ref, wub_ref, wdb_ref,
                        o_ref):
    del ea_ref, eb_ref
    i = pl.program_id(0)
    D = wga_ref.shape[2]

    @pl.when(i < nused_ref[0])
    def _():
        xb = _load_token_tiles(xs_ref, 0, TOKEN_TILE_ROWS, MOE_BLOCK, D).astype(BF16)
        halves = []
        for wg_ref, wu_ref, wd_ref in ((wga_ref, wua_ref, wda_ref), (wgb_ref, wub_ref, wdb_ref)):
            gate = jnp.dot(xb, wg_ref[0, 0], preferred_element_type=F32)
            up = jnp.dot(xb, wu_ref[0, 0], preferred_element_type=F32)
            hid = (gate * jax.nn.sigmoid(gate) * up).astype(BF16)
            out = jnp.dot(hid, wd_ref[0, 0], preferred_element_type=F32)
            halves.append(lax.bitcast_convert_type(out.astype(BF16).astype(F32), jnp.uint32))
        _store_token_tiles(o_ref, 0, TOKEN_TILE_ROWS, halves[0] | (halves[1] >> 16))

    @pl.when(i >= nused_ref[0])
    def _():
        o_ref[...] = jnp.zeros_like(o_ref)


def _unpack_pair(words):
    hi = lax.bitcast_convert_type(words & jnp.uint32(0xFFFF0000), F32)
    lo = lax.bitcast_convert_type(words << 16, F32)
    return hi, lo


def expert_pairs(xs, blk_a, blk_b, n_used, w_gate, w_up, w_down, layer):
    D, DE = w_gate.shape[2:]
    P = xs.shape[0] // TOKEN_TILE_ROWS
    n_blocks = P // MOE_BLOCK
    wspec = lambda shape, which: pl.BlockSpec(shape, lambda i, ea, eb, nu: (layer, (ea, eb)[which][i], 0, 0))
    grid_spec = pltpu.PrefetchScalarGridSpec(
        num_scalar_prefetch=3,
        grid=(n_blocks,),
        in_specs=[pl.BlockSpec((MOE_BLOCK * TOKEN_TILE_ROWS, LANES), lambda i, ea, eb, nu: (i, 0)),
                  wspec((1, 1, D, DE), 0), wspec((1, 1, D, DE), 0), wspec((1, 1, DE, D), 0),
                  wspec((1, 1, D, DE), 1), wspec((1, 1, D, DE), 1), wspec((1, 1, DE, D), 1)],
        out_specs=pl.BlockSpec((MOE_BLOCK * TOKEN_TILE_ROWS, LANES), lambda i, ea, eb, nu: (i, 0)),
    )
    return pl.pallas_call(
        _expert_pair_kernel,
        grid_spec=grid_spec,
        out_shape=jax.ShapeDtypeStruct((P * TOKEN_TILE_ROWS, LANES), jnp.uint32),
        compiler_params=_cparams("arbitrary"),
        name="moe_experts",
    )(blk_a, blk_b, n_used, xs, w_gate, w_up, w_down, w_gate, w_up, w_down)


def _gather_pairs(idx_ref, src_hbm, buf, sem, n_tokens):
    def body(g, carry):
        for u in range(DMA_UNROLL):
            r = g * DMA_UNROLL + u
            src = pl.multiple_of(idx_ref[0, 0, r] * TOKEN_TILE_ROWS, TOKEN_TILE_ROWS)
            pltpu.make_async_copy(src_hbm.at[pl.ds(src, TOKEN_TILE_ROWS)],
                                  buf.at[pl.ds(r * TOKEN_TILE_ROWS, TOKEN_TILE_ROWS)], sem).start(priority=u % 2)
        return carry
    lax.fori_loop(0, n_tokens // DMA_UNROLL, body, 0)


def _collect_kernel(dest_ref, dest_next_ref, o_hbm, info_ref, h_ref, g2_ref, fg_ref, out_ref, obuf, sems, *,
                    final_norm):
    i = pl.program_id(0)
    n = pl.num_programs(0)
    slot = i % 2
    tb, D = h_ref.shape

    @pl.when(i == 0)
    def _():
        _gather_pairs(dest_ref, o_hbm, obuf.at[0], sems.at[0], tb)

    @pl.when(i + 1 < n)
    def _():
        _gather_pairs(dest_next_ref, o_hbm, obuf.at[1 - slot], sems.at[1 - slot], tb)

    _wait_rows(obuf.at[slot], sems.at[slot])
    e_a, e_b = _unpack_pair(_load_token_tiles(obuf.at[slot], 0, TOKEN_TILE_ROWS, tb, D))
    m = info_ref[:, INFO_WA:INFO_WA + 1] * e_a + info_ref[:, INFO_WB:INFO_WB + 1] * e_b
    h = h_ref[...] + g2_ref[0] * m
    if final_norm:
        ms = jnp.mean(h * h, axis=-1, keepdims=True)
        h = h * lax.rsqrt(ms + EPS) * fg_ref[...]
    out_ref[...] = h


def moe_collect(o_sorted, dest, info, block_offset, h_tokens, g2, tokens_per_batch, final_g, final_norm):
    T, D = h_tokens.shape
    tb = DISPATCH_TOKENS
    nt = T // tb
    if g2.shape[0] == 1:
        g2_index = lambda i: 0
    else:
        assert tokens_per_batch % tb == 0
        g2_index = lambda i: i // (tokens_per_batch // tb)
    last = block_offset + nt - 1
    dest3 = dest.reshape(-1, 1, tb)
    return pl.pallas_call(
        functools.partial(_collect_kernel, final_norm=final_norm),
        grid=(nt,),
        in_specs=[pl.BlockSpec((1, 1, tb), lambda i: (block_offset + i, 0, 0), memory_space=pltpu.SMEM),
                  pl.BlockSpec((1, 1, tb), lambda i: (jnp.minimum(block_offset + i + 1, last), 0, 0),
                               memory_space=pltpu.SMEM),
                  pl.BlockSpec(memory_space=pl.ANY),
                  pl.BlockSpec((tb, ROUTER_COLS), lambda i: (block_offset + i, 0)),
                  pl.BlockSpec((tb, D), lambda i: (i, 0)),
                  pl.BlockSpec((1, 1, D), lambda i: (g2_index(i), 0, 0)),
                  pl.BlockSpec((1, D), lambda i: (0, 0))],
        out_specs=pl.BlockSpec((tb, D), lambda i: (i, 0)),
        out_shape=jax.ShapeDtypeStruct((T, D), F32),
        scratch_shapes=[pltpu.VMEM((2, tb * TOKEN_TILE_ROWS, LANES), jnp.uint32), pltpu.SemaphoreType.DMA((2,))],
        compiler_params=_cparams("arbitrary"),
        name="moe_collect",
    )(dest3, dest3, o_sorted, info, h_tokens, g2, final_g.reshape(1, D))


CONV_MARGIN = 16


def _time_chunk(L):
    return min(L, 256)


LANES = 128


def _zero_margins(pad_ref, L):
    zeros = jnp.zeros((CONV_MARGIN, LANES), F32)
    for s in range(pad_ref.shape[0]):
        pad_ref[s, pl.ds(0, CONV_MARGIN), :] = zeros
        pad_ref[s, pl.ds(CONV_MARGIN + L, CONV_MARGIN), :] = zeros


def _dw_conv_slab(pad_ref, s, base, T, w_ref, b_ref, col, taps, pad_left):
    acc = jnp.broadcast_to(b_ref[:, col:col + LANES], (T, LANES))
    for k in range(taps):
        acc = acc + w_ref[k:k + 1, col:col + LANES] * pad_ref[s, pl.ds(base + (CONV_MARGIN - pad_left + k), T), :]
    return acc


def _conformer_kernel(u_ref, w_ref, b_ref, g_ref, beta_ref, o_ref, ypad):
    L = o_ref.shape[1]
    T = _time_chunk(L)
    C = D_GROUP
    n_slabs = C // LANES
    pad = (CONF_KERNEL - 1) // 2
    _zero_margins(ypad, L)

    def glu(j, carry):
        base = pl.multiple_of(j * T, T)
        for s in range(n_slabs):
            a = u_ref[0, pl.ds(base, T), s * LANES:(s + 1) * LANES]
            gate = u_ref[0, pl.ds(base, T), C + s * LANES:C + (s + 1) * LANES]
            ypad[s, pl.ds(CONV_MARGIN + base, T), :] = a * jax.nn.sigmoid(gate)
        return carry
    lax.fori_loop(0, L // T, glu, 0)

    def conv(j, carry):
        base = pl.multiple_of(j * T, T)
        acc = jnp.concatenate([_dw_conv_slab(ypad, s, base, T, w_ref, b_ref, s * LANES, CONF_KERNEL, pad)
                               for s in range(n_slabs)], axis=-1)
        mu = jnp.mean(acc, axis=-1, keepdims=True)
        cen = acc - mu
        var = jnp.mean(cen * cen, axis=-1, keepdims=True)
        y = cen * lax.rsqrt(var + EPS) * g_ref[...] + beta_ref[...]
        o_ref[0, pl.ds(base, T), :] = y * jax.nn.sigmoid(y)
        return carry
    lax.fori_loop(0, L // T, conv, 0)


def conformer_conv(u, w, b, ln_g, ln_b):
    B, L, _ = u.shape
    C = D_GROUP
    vec = pl.BlockSpec((1, C), lambda i: (0, 0))
    return pl.pallas_call(
        _conformer_kernel,
        grid=(B,),
        in_specs=[pl.BlockSpec((1, L, 2 * C), lambda i: (i, 0, 0)),
                  pl.BlockSpec((CONF_KERNEL, C), lambda i: (0, 0)), vec, vec, vec],
        out_specs=pl.BlockSpec((1, L, C), lambda i: (i, 0, 0)),
        out_shape=jax.ShapeDtypeStruct((B, L, C), F32),
        scratch_shapes=[pltpu.VMEM((C // LANES, L + 2 * CONV_MARGIN, LANES), F32)],
        compiler_params=_cparams("parallel"),
        name="conformer_conv",
    )(u, w, b.reshape(1, C), ln_g.reshape(1, C), ln_b.reshape(1, C))


def _gelu_tanh(x):
    return 0.5 * x * (1.0 + jnp.tanh(math.sqrt(2.0 / math.pi) * (x + 0.044715 * (x * x * x))))


def _lru_kernel(uc_ref, ux_ref, cw_ref, cb_ref, wcat_ref, bcat_ref, lam_ref, *rest, need_ctx):
    if need_ctx:
        oc_ref, ox_ref, cpad, xpad, a_s, b_s, yx, yc = rest
    else:
        ox_ref, cpad, xpad, a_s, b_s, yx = rest
        oc_ref = yc = None
    C = D_GROUP
    n_slabs = C // LANES
    Lc = uc_ref.shape[1]
    Lx = ux_ref.shape[1]
    pad_l = (LRU_CONV - 1) // 2

    def fill(pad_ref, u_ref, L):
        T = _time_chunk(L)
        _zero_margins(pad_ref, L)

        def body(j, carry):
            base = pl.multiple_of(j * T, T)
            for s in range(n_slabs):
                pad_ref[s, pl.ds(CONV_MARGIN + base, T), :] = u_ref[0, pl.ds(base, T),
                                                                    C + s * LANES:C + (s + 1) * LANES]
            return carry
        lax.fori_loop(0, L // T, body, 0)

    fill(cpad, uc_ref, Lc)
    fill(xpad, ux_ref, Lx)

    def coeffs(pad_ref, base, T, d):
        x = jnp.concatenate([_dw_conv_slab(pad_ref, s, base, T, cw_ref, cb_ref, s * LANES, LRU_CONV, pad_l)
                             for s in range(n_slabs)], axis=-1)
        t = jnp.tanh(jnp.dot(x.astype(BF16), wcat_ref[:, 2 * d * C:2 * (d + 1) * C],
                             preferred_element_type=F32) + bcat_ref[:, 2 * d * C:2 * (d + 1) * C])
        i = 0.5 * t[:, C:] + 0.5
        z = -lam_ref[d:d + 1, :]
        softplus = jnp.maximum(z, 0.0) + jnp.log(1.0 + jnp.exp(-jnp.abs(z)))
        half_rate = (-0.5 * LRU_C) * softplus
        a = jnp.exp(half_rate * t[:, :C] + half_rate)
        b = jnp.sqrt(1.0 - a * a) * (i * x)
        for s in range(n_slabs):
            a_s[d * n_slabs + s, pl.ds(0, T), :] = a[:, s * LANES:(s + 1) * LANES]
            b_s[d * n_slabs + s, pl.ds(0, T), :] = b[:, s * LANES:(s + 1) * LANES]

    def run(pad_ref, L, h, y_ref):
        T = _time_chunk(L)
        n = L // T

        def chunk(j, h):
            base_f = pl.multiple_of(j * T, T)
            base_b = pl.multiple_of((n - 1 - j) * T, T)
            coeffs(pad_ref, base_f, T, 0)
            coeffs(pad_ref, base_b, T, 1)

            def step(t, h):
                new = []
                for d, (base, row) in enumerate(((base_f, t), (base_b, T - 1 - t))):
                    for s in range(n_slabs):
                        k = d * n_slabs + s
                        hs = a_s[k, pl.ds(row, 1), :] * h[k] + b_s[k, pl.ds(row, 1), :]
                        if y_ref is not None:
                            y_ref[k, pl.ds(base + row, 1), :] = hs
                        new.append(hs)
                return tuple(new)
            return lax.fori_loop(0, T, step, h, unroll=8)
        return lax.fori_loop(0, n, chunk, h)

    h = tuple(jnp.zeros((1, LANES), F32) for _ in range(2 * n_slabs))
    h = run(cpad, Lc, h, yc)
    run(xpad, Lx, h, yx)

    def finish(u_ref, y_ref, o_ref, L):
        T = _time_chunk(L)

        def body(j, carry):
            base = pl.multiple_of(j * T, T)
            y = jnp.concatenate([y_ref[s, pl.ds(base, T), :] + y_ref[n_slabs + s, pl.ds(base, T), :]
                                 for s in range(n_slabs)], axis=-1)
            o_ref[0, pl.ds(base, T), :] = _gelu_tanh(u_ref[0, pl.ds(base, T), :C]) * y
            return carry
        lax.fori_loop(0, L // T, body, 0)

    finish(ux_ref, yx, ox_ref, Lx)
    if need_ctx:
        finish(uc_ref, yc, oc_ref, Lc)


def _block_diag(w):
    H, n, _ = w.shape
    eye = jnp.eye(H, dtype=w.dtype)
    return (eye[:, None, :, None] * w[:, :, None, :]).reshape(H * n, H * n)


def rglru_mixer(uc, ux, lp, need_ctx):
    B, Lc, _ = uc.shape
    Lx = ux.shape[1]
    C = D_GROUP
    wcat = (0.5 * jnp.concatenate([_block_diag(lp["lru_wa"][0]), _block_diag(lp["lru_wx"][0]),
                                   _block_diag(lp["lru_wa"][1]), _block_diag(lp["lru_wx"][1])], axis=1)).astype(BF16)
    bcat = 0.5 * jnp.concatenate([lp["lru_ba"][0], lp["lru_bx"][0], lp["lru_ba"][1], lp["lru_bx"][1]]).reshape(1, 4 * C)
    full = lambda r, c: pl.BlockSpec((r, c), lambda i: (0, 0))
    seq = lambda L, n: pl.BlockSpec((1, L, n), lambda i: (i, 0, 0))
    out_specs = [seq(Lx, C)]
    out_shape = [jax.ShapeDtypeStruct((B, Lx, C), F32)]
    if need_ctx:
        out_specs = [seq(Lc, C)] + out_specs
        out_shape = [jax.ShapeDtypeStruct((B, Lc, C), F32)] + out_shape
    T = _time_chunk(Lx)
    slab = lambda rows, n=1: pltpu.VMEM((n * C // LANES, rows, LANES), F32)
    scratch = [slab(Lc + 2 * CONV_MARGIN), slab(Lx + 2 * CONV_MARGIN), slab(T, 2), slab(T, 2), slab(Lx, 2)]
    if need_ctx:
        scratch.append(slab(Lc, 2))
    res = pl.pallas_call(
        functools.partial(_lru_kernel, need_ctx=need_ctx),
        grid=(B,),
        in_specs=[seq(Lc, 2 * C), seq(Lx, 2 * C), full(LRU_CONV, C), full(1, C), full(C, 4 * C),
                  full(1, 4 * C), full(2, C)],
        out_specs=out_specs,
        out_shape=out_shape,
        scratch_shapes=scratch,
        compiler_params=_cparams("parallel"),
        name="rglru",
    )(uc, ux, lp["lru_conv_w"], lp["lru_conv_b"].reshape(1, C), wcat, bcat, lp["lru_lambda"])
    if need_ctx:
        return res[0], res[1]
    return None, res[0]


HY_SHORT = 3


def _short_conv(pad_ref, base, T, w_ref, b_ref, c0, c1):
    return jnp.concatenate([_dw_conv_slab(pad_ref, col // LANES, base, T, w_ref, b_ref, col, HY_SHORT, 1)
                            for col in range(c0, c1, LANES)], axis=-1)


def _fill_padded(pad_ref, u_ref, L, T):
    _zero_margins(pad_ref, L)

    def body(j, carry):
        base = pl.multiple_of(j * T, T)
        for s in range(pad_ref.shape[0]):
            pad_ref[s, pl.ds(CONV_MARGIN + base, T), :] = u_ref[0, pl.ds(base, T), s * LANES:(s + 1) * LANES]
        return carry
    lax.fori_loop(0, L // T, body, 0)


def _hyena_pre_kernel(u_ref, w_ref, b_ref, z_ref, upad):
    L = u_ref.shape[1]
    T = _time_chunk(L)
    C = D_GROUP
    _fill_padded(upad, u_ref, L, T)

    def body(j, carry):
        base = pl.multiple_of(j * T, T)
        x1 = _short_conv(upad, base, T, w_ref, b_ref, C, 2 * C)
        v = _short_conv(upad, base, T, w_ref, b_ref, 2 * C, 3 * C)
        z_ref[pl.ds(base, T), :] = (x1 * v).astype(BF16)
        return carry
    lax.fori_loop(0, L // T, body, 0)


def _hyena_post_kernel(u_ref, y_ref, w_ref, b_ref, bias_ref, o_ref, upad):
    L = u_ref.shape[1]
    T = _time_chunk(L)
    C = D_GROUP
    _fill_padded(upad, u_ref, L, T)

    def body(j, carry):
        base = pl.multiple_of(j * T, T)
        x0 = _short_conv(upad, base, T, w_ref, b_ref, 0, C)
        x1 = _short_conv(upad, base, T, w_ref, b_ref, C, 2 * C)
        v = _short_conv(upad, base, T, w_ref, b_ref, 2 * C, 3 * C)
        o_ref[0, pl.ds(base, T), :] = x0 * (y_ref[pl.ds(base, T), :] + (x1 * v) * bias_ref[...])
        return carry
    lax.fori_loop(0, L // T, body, 0)


def _spectrum_kernel(f_ref, z_ref, ha_ref, hb_ref, hc_ref, y_ref):
    tf = ha_ref.shape[0]
    acc = jnp.dot(f_ref[...], z_ref[...], preferred_element_type=F32)
    zr = acc[:tf]
    zi = acc[tf:]
    y_ref[:tf, :] = (zr * ha_ref[...] - zi * hb_ref[...]).astype(BF16)
    y_ref[tf:, :] = (zr * hb_ref[...] + zi * hc_ref[...]).astype(BF16)


def _idft_kernel(f_ref, y_ref, o_ref):
    o_ref[...] = jnp.dot(f_ref[...], y_ref[...], preferred_element_type=F32)


def dft_tables(L):
    N = 2 * L
    tf = min(256, L)
    k = jnp.arange(L, dtype=jnp.int32)
    n = jnp.arange(L, dtype=jnp.int32)
    ang = (2.0 * math.pi / N) * ((k[:, None] * n[None, :]) % N).astype(F32)
    cos = jnp.cos(ang)
    sin = jnp.sin(ang)
    nyq = jnp.where(n % 2 == 0, 1.0, -1.0).astype(F32)
    f_re = cos
    f_im = (-sin).at[0].set(nyq)
    fwd = jnp.stack([f_re.reshape(L // tf, tf, L), f_im.reshape(L // tf, tf, L)], axis=1).reshape(N, L)
    ck = jnp.where(k == 0, 1.0, 2.0).astype(F32)[:, None] / N
    i_re = cos * ck
    i_im = (-sin * ck).at[0].set(nyq / N)
    inv = jnp.stack([i_re.reshape(L // tf, tf, L), i_im.reshape(L // tf, tf, L)], axis=1).reshape(N, L).T
    return fwd.astype(BF16), inv.astype(BF16)


def filter_spectrum(h_fwd, h_bwd):
    L, C = h_fwd.shape
    k = jnp.concatenate([h_fwd, jnp.zeros((1, C), F32), h_bwd[1:][::-1]], axis=0)
    hf = jnp.fft.rfft(k, axis=0)
    hr = jnp.real(hf)
    hi = jnp.imag(hf)
    a = hr[:L]
    b = hi[:L].at[0].set(0.0)
    c = hr[:L].at[0].set(hr[L])
    return a, b, c


def hyena_mixer(u, lp, tables):
    B, L, _ = u.shape
    C = D_GROUP
    N = 2 * L
    fwd, inv = tables
    tf = min(256, L)
    T = _time_chunk(L)
    w, bsh = lp["hy_short_w"], lp["hy_short_b"].reshape(1, 3 * C)
    z2 = pl.pallas_call(
        _hyena_pre_kernel,
        grid=(B,),
        in_specs=[pl.BlockSpec((1, L, 3 * C), lambda b: (b, 0, 0)),
                  pl.BlockSpec((HY_SHORT, 3 * C), lambda b: (0, 0)),
                  pl.BlockSpec((1, 3 * C), lambda b: (0, 0))],
        out_specs=pl.BlockSpec((L, C), lambda b: (0, b)),
        out_shape=jax.ShapeDtypeStruct((L, B * C), BF16),
        scratch_shapes=[pltpu.VMEM((3 * C // LANES, L + 2 * CONV_MARGIN, LANES), F32)],
        compiler_params=_cparams("parallel"),
        name="hyena_pre",
    )(u, w, bsh)

    h_fwd, h_bwd = _hyena_filters(L, lp)
    tn = 2 * C
    ha, hb, hc = [jnp.tile(t, (1, tn // C)) for t in filter_spectrum(h_fwd, h_bwd)]
    hspec = pl.BlockSpec((tf, tn), lambda i, j: (i, 0))
    y2 = pl.pallas_call(
        _spectrum_kernel,
        grid=(L // tf, B * C // tn),
        in_specs=[pl.BlockSpec((2 * tf, L), lambda i, j: (i, 0)),
                  pl.BlockSpec((L, tn), lambda i, j: (0, j)), hspec, hspec, hspec],
        out_specs=pl.BlockSpec((2 * tf, tn), lambda i, j: (i, j)),
        out_shape=jax.ShapeDtypeStruct((N, B * C), BF16),
        compiler_params=_cparams("parallel", "parallel"),
        name="hyena_spectrum",
    )(fwd, z2, ha, hb, hc)

    tl = min(256, L)
    yt = pl.pallas_call(
        _idft_kernel,
        grid=(L // tl, B * C // tn),
        in_specs=[pl.BlockSpec((tl, N), lambda i, j: (i, 0)),
                  pl.BlockSpec((N, tn), lambda i, j: (0, j))],
        out_specs=pl.BlockSpec((tl, tn), lambda i, j: (i, j)),
        out_shape=jax.ShapeDtypeStruct((L, B * C), F32),
        compiler_params=_cparams("parallel", "parallel"),
        name="hyena_idft",
    )(inv, y2)

    return pl.pallas_call(
        _hyena_post_kernel,
        grid=(B,),
        in_specs=[pl.BlockSpec((1, L, 3 * C), lambda b: (b, 0, 0)),
                  pl.BlockSpec((L, C), lambda b: (0, b)),
                  pl.BlockSpec((HY_SHORT, 3 * C), lambda b: (0, 0)),
                  pl.BlockSpec((1, 3 * C), lambda b: (0, 0)),
                  pl.BlockSpec((1, C), lambda b: (0, 0))],
        out_specs=pl.BlockSpec((1, L, C), lambda b: (b, 0, 0)),
        out_shape=jax.ShapeDtypeStruct((B, L, C), F32),
        scratch_shapes=[pltpu.VMEM((3 * C // LANES, L + 2 * CONV_MARGIN, LANES), F32)],
        compiler_params=_cparams("parallel"),
        name="hyena_post",
    )(u, yt, w, bsh, lp["hy_bias"].reshape(1, C))


FFT_N2 = 128
FFT_UNROLL = 8


class _FftPlan:
    def __init__(self, L):
        self.L = L
        self.N = 2 * L
        self.N1 = self.N // FFT_N2
        self.KH = self.N1 // 2 + 1
        self.KP = -(-self.KH // 8) * 8
        self.PA = 2 * self.KP + 4


def fft_tables(L):
    p = _FftPlan(L)
    N, N1, KH, KP = p.N, p.N1, p.KH, p.KP
    n2 = jnp.arange(FFT_N2, dtype=jnp.int32)
    k1 = jnp.arange(KP, dtype=jnp.int32)
    n1 = jnp.arange(N1, dtype=jnp.int32)
    n = FFT_N2 * n1[None, None, :] + n2[:, None, None]
    ang = (2.0 * math.pi / N) * ((k1[None, :, None] * n) % N).astype(F32)
    keep = (k1 < KH)[None, :, None]
    g_re = jnp.where(keep, jnp.cos(ang), 0.0)
    g_im = jnp.where(keep, -jnp.sin(ang), 0.0)
    ga_full = jnp.concatenate([g_re, g_im], axis=1)
    ck = jnp.where((k1 == 0) | (k1 == N1 // 2), 1.0, 2.0) / N
    ga_inv = jnp.swapaxes(ga_full[:, :, :N1 // 2] * jnp.tile(ck, 2)[None, :, None], 1, 2)
    kk = jnp.arange(FFT_N2, dtype=jnp.int32)
    ang2 = (2.0 * math.pi / FFT_N2) * ((kk[:, None] * kk[None, :]) % FFT_N2).astype(F32)
    fr, fi = jnp.cos(ang2), -jnp.sin(ang2)
    fb = jnp.block([[fr, -fi], [fi, fr]])
    fb_inv = jnp.block([[fr, fi], [-fi, fr]])
    return dict(ga_half=ga_full[:, :, :N1 // 2].astype(BF16), ga_full=ga_full.astype(BF16),
                ga_inv=ga_inv.astype(BF16), fb=fb.astype(BF16), fb_inv=fb_inv.astype(BF16))


def _fft_stage_a(x_ref, ga_ref, s_ref, plan, n1_count):
    n_slabs = x_ref.shape[0]

    def body(n2, carry):
        xs = jnp.concatenate([x_ref[s, pl.ds(n2, n1_count, stride=FFT_N2), :] for s in range(n_slabs)], axis=-1)
        a = jnp.dot(ga_ref[n2], xs.astype(BF16), preferred_element_type=F32)
        for s in range(n_slabs):
            s_ref[s, pl.ds(n2 * plan.PA, 2 * plan.KP), :] = a[:, s * LANES:(s + 1) * LANES]
        return carry
    lax.fori_loop(0, FFT_N2, body, 0, unroll=FFT_UNROLL)


def _fft_load_k1(s_ref, k1, plan):
    n_slabs = s_ref.shape[0]
    re = jnp.concatenate([s_ref[s, pl.ds(k1, FFT_N2, stride=plan.PA), :] for s in range(n_slabs)], axis=-1)
    im = jnp.concatenate([s_ref[s, pl.ds(plan.KP + k1, FFT_N2, stride=plan.PA), :] for s in range(n_slabs)], axis=-1)
    return jnp.concatenate([re, im], axis=0).astype(BF16)


def _fft_filter_kernel(k_ref, ga_ref, fb_ref, h_ref, s_ref, *, plan):
    _fft_stage_a(k_ref, ga_ref, s_ref, plan, plan.N1)

    def body(k1, carry):
        h_ref[k1] = jnp.dot(fb_ref[...], _fft_load_k1(s_ref, k1, plan), preferred_element_type=F32).astype(BF16)
        return carry
    lax.fori_loop(0, plan.KH, body, 0)


def _fft_conv_kernel(z_ref, ga_ref, gi_ref, fb_ref, fbi_ref, h_ref, y_ref, s_ref, *, plan):
    zs = z_ref.at[0]
    ys = y_ref.at[0]
    n_slabs = zs.shape[0]
    half = FFT_N2
    _fft_stage_a(zs, ga_ref, s_ref, plan, plan.N1 // 2)

    def body_b(k1, carry):
        x = jnp.dot(fb_ref[...], _fft_load_k1(s_ref, k1, plan), preferred_element_type=F32)
        h = h_ref[k1].astype(F32)
        xr, xi, hr, hi = x[:half], x[half:], h[:half], h[half:]
        y = jnp.concatenate([xr * hr - xi * hi, xr * hi + xi * hr], axis=0).astype(BF16)
        b = jnp.dot(fbi_ref[...], y, preferred_element_type=F32)
        for s in range(n_slabs):
            s_ref[s, pl.ds(k1, FFT_N2, stride=plan.PA), :] = b[:half, s * LANES:(s + 1) * LANES]
            s_ref[s, pl.ds(plan.KP + k1, FFT_N2, stride=plan.PA), :] = b[half:, s * LANES:(s + 1) * LANES]
        return carry
    lax.fori_loop(0, plan.KH, body_b, 0, unroll=3)

    def body_a(n2, carry):
        b = jnp.concatenate([s_ref[s, pl.ds(n2 * plan.PA, 2 * plan.KP), :] for s in range(n_slabs)], axis=-1)
        y = jnp.dot(gi_ref[n2], b.astype(BF16), preferred_element_type=F32)
        for s in range(n_slabs):
            ys[s, pl.ds(n2, plan.N1 // 2, stride=FFT_N2), :] = y[:, s * LANES:(s + 1) * LANES]
        return carry
    lax.fori_loop(0, FFT_N2, body_a, 0, unroll=FFT_UNROLL)


def fft_filter_spectrum(h_fwd, h_bwd, tabs):
    L, C = h_fwd.shape
    plan = _FftPlan(L)
    n_slabs = C // LANES
    k = jnp.concatenate([h_fwd, jnp.zeros((1, C), F32), h_bwd[1:][::-1]], axis=0)
    k = k.reshape(plan.N, n_slabs, LANES).transpose(1, 0, 2)
    full = lambda shape: pl.BlockSpec(shape, lambda i: (0,) * len(shape))
    return pl.pallas_call(
        functools.partial(_fft_filter_kernel, plan=plan),
        grid=(1,),
        in_specs=[full((n_slabs, plan.N, LANES)), full((FFT_N2, 2 * plan.KP, plan.N1)),
                  full((2 * FFT_N2, 2 * FFT_N2))],
        out_specs=full((plan.KH, 2 * FFT_N2, C)),
        out_shape=jax.ShapeDtypeStruct((plan.KH, 2 * FFT_N2, C), BF16),
        scratch_shapes=[pltpu.VMEM((n_slabs, FFT_N2 * plan.PA, LANES), F32)],
        compiler_params=_cparams("arbitrary"),
        name="hyena_filter_fft",
    )(k, tabs["ga_full"], tabs["fb"])


def fft_long_conv(z, h_spec, tabs):
    B, n_slabs, L, _ = z.shape
    plan = _FftPlan(L)
    C = n_slabs * LANES
    full = lambda shape: pl.BlockSpec(shape, lambda b: (0,) * len(shape))
    seq = pl.BlockSpec((1, n_slabs, L, LANES), lambda b: (b, 0, 0, 0))
    return pl.pallas_call(
        functools.partial(_fft_conv_kernel, plan=plan),
        grid=(B,),
        in_specs=[seq, full((FFT_N2, 2 * plan.KP, plan.N1 // 2)), full((FFT_N2, plan.N1 // 2, 2 * plan.KP)),
                  full((2 * FFT_N2, 2 * FFT_N2)), full((2 * FFT_N2, 2 * FFT_N2)),
                  full((plan.KH, 2 * FFT_N2, C))],
        out_specs=seq,
        out_shape=jax.ShapeDtypeStruct((B, n_slabs, L, LANES), F32),
        scratch_shapes=[pltpu.VMEM((n_slabs, FFT_N2 * plan.PA, LANES), F32)],
        compiler_params=_cparams("parallel"),
        name="hyena_fft_conv",
    )(z, tabs["ga_half"], tabs["ga_inv"], tabs["fb"], tabs["fb_inv"], h_spec)


def _hyena_pre_slab_kernel(u_ref, w_ref, b_ref, z_ref, upad):
    L = u_ref.shape[1]
    T = _time_chunk(L)
    C = D_GROUP
    _fill_padded(upad, u_ref, L, T)

    def body(j, carry):
        base = pl.multiple_of(j * T, T)
        for s in range(C // LANES):
            x1 = _dw_conv_slab(upad, C // LANES + s, base, T, w_ref, b_ref, C + s * LANES, HY_SHORT, 1)
            v = _dw_conv_slab(upad, 2 * C // LANES + s, base, T, w_ref, b_ref, 2 * C + s * LANES, HY_SHORT, 1)
            z_ref[0, s, pl.ds(base, T), :] = x1 * v
        return carry
    lax.fori_loop(0, L // T, body, 0)


def _hyena_post_slab_kernel(u0_ref, z_ref, y_ref, w_ref, b_ref, bias_ref, o_ref, upad):
    L = u0_ref.shape[1]
    T = _time_chunk(L)
    C = D_GROUP
    _fill_padded(upad, u0_ref, L, T)

    def body(j, carry):
        base = pl.multiple_of(j * T, T)
        x0 = _short_conv(upad, base, T, w_ref, b_ref, 0, C)
        z = jnp.concatenate([z_ref[0, s, pl.ds(base, T), :] for s in range(C // LANES)], axis=-1)
        y = jnp.concatenate([y_ref[0, s, pl.ds(base, T), :] for s in range(C // LANES)], axis=-1)
        o_ref[0, pl.ds(base, T), :] = x0 * (y + z * bias_ref[...])
        return carry
    lax.fori_loop(0, L // T, body, 0)


def hyena_mixer_fft(u, lp, tabs):
    B, L, _ = u.shape
    C = D_GROUP
    n_slabs = C // LANES
    w, bsh = lp["hy_short_w"], lp["hy_short_b"].reshape(1, 3 * C)
    useq = pl.BlockSpec((1, L, 3 * C), lambda b: (b, 0, 0))
    slabs = pl.BlockSpec((1, n_slabs, L, LANES), lambda b: (b, 0, 0, 0))
    wspec = pl.BlockSpec((HY_SHORT, 3 * C), lambda b: (0, 0))
    bspec = pl.BlockSpec((1, 3 * C), lambda b: (0, 0))
    pad_scratch = pltpu.VMEM((3 * C // LANES, L + 2 * CONV_MARGIN, LANES), F32)
    z = pl.pallas_call(
        _hyena_pre_slab_kernel,
        grid=(B,),
        in_specs=[useq, wspec, bspec],
        out_specs=slabs,
        out_shape=jax.ShapeDtypeStruct((B, n_slabs, L, LANES), F32),
        scratch_shapes=[pad_scratch],
        compiler_params=_cparams("parallel"),
        name="hyena_pre",
    )(u, w, bsh)
    h_fwd, h_bwd = _hyena_filters(L, lp)
    y = fft_long_conv(z, fft_filter_spectrum(h_fwd, h_bwd, tabs), tabs)
    return pl.pallas_call(
        _hyena_post_slab_kernel,
        grid=(B,),
        in_specs=[pl.BlockSpec((1, L, C), lambda b: (b, 0, 0)), slabs, slabs, wspec, bspec,
                  pl.BlockSpec((1, C), lambda b: (0, 0))],
        out_specs=pl.BlockSpec((1, L, C), lambda b: (b, 0, 0)),
        out_shape=jax.ShapeDtypeStruct((B, L, C), F32),
        scratch_shapes=[pltpu.VMEM((n_slabs, L + 2 * CONV_MARGIN, LANES), F32)],
        compiler_params=_cparams("parallel"),
        name="hyena_post",
    )(u, z, y, w, bsh, lp["hy_bias"].reshape(1, C))


def _hyena_filters(L, lp):
    t = jnp.linspace(0.0, 1.0, L, dtype=F32)[:, None]
    bands = (HY_EMB - 1) // 2
    w = 2.0 * math.pi * jnp.arange(L, dtype=F32)[:, None] / L
    f = jnp.linspace(1e-4, bands - 1, bands, dtype=F32)[None]
    z = jnp.concatenate([t, jnp.cos(f * w), -jnp.sin(f * w)], axis=-1)
    hdn = jnp.sin(z @ lp["hy_ffn_w1"] + lp["hy_ffn_b1"])
    hdn = jnp.sin(hdn @ lp["hy_ffn_w2"] + lp["hy_ffn_b2"])
    h = (hdn @ lp["hy_ffn_w3"]).reshape(L, 2, D_GROUP)
    max_decay = math.log(HY_TARGET) / HY_FAST_DECAY
    min_decay = math.log(HY_TARGET) / HY_SLOW_DECAY
    deltas = jnp.linspace(min_decay, max_decay, D_GROUP, dtype=F32)
    h = h * jnp.exp(-t * jnp.abs(deltas))[:, None, :]
    h = h / (jnp.sum(jnp.abs(h), axis=(0, 1), keepdims=True) + EPS)
    return h[:, 0], h[:, 1]


def _layer(hc, hx, c_silu_all, lp, need_ctx, final_g, final_norm, tables_x, tables_c, experts, layer):
    B, S, D = hx.shape
    C = hc.shape[1]
    mod = small_linear(c_silu_all, lp["ada_w"], lp["ada_b"])
    mod_x = mod[:B].reshape(B, 6, 1, D)
    mod_c = jnp.broadcast_to(mod[B].reshape(1, 6, 1, D), (B, 6, 1, D))
    w_ext = extend_w_in(lp["w_in"])
    cos_x, sin_x = rope_tables(S, True)
    cos_c, sin_c = rope_tables(C, False)
    hy_x, cf_x, at_x, lr_x = in_proj(hx, mod_x[:, 0], mod_x[:, 1], lp["norm1_g"], w_ext, cos_x, sin_x, tm=512)
    hy_c, cf_c, at_c, lr_c = in_proj(hc, mod_c[:, 0], mod_c[:, 1], lp["norm1_g"], w_ext, cos_c, sin_c, tm=256)

    yd_c, yd_x = rglru_mixer(lr_c, lr_x, lp, need_ctx)
    conf = lambda u: conformer_conv(u, lp["conf_dw_w"], lp["conf_dw_b"], lp["conf_ln_g"], lp["conf_ln_b"])
    ys_x = [hyena_mixer_fft(hy_x, lp, tables_x), conf(cf_x),
            window_attention(at_x, at_c, lp["attn_sink"]), yd_x]

    w_out = lp["w_out"].astype(BF16)
    w_router = jnp.zeros((D, ROUTER_COLS), F32)
    w_router = w_router.at[:, :N_GROUPS].set(lp["router_g_w"]).at[:, N_GROUPS:N_GROUPS + N_EXPERTS].set(lp["router_e_w"])
    w_router = w_router.astype(BF16)
    b_router = jnp.zeros((1, ROUTER_COLS), F32)
    b_router = b_router.at[0, :N_GROUPS].set(lp["router_g_b"]).at[0, N_GROUPS:N_GROUPS + N_EXPERTS].set(lp["router_e_b"])

    hx1, lg_x = out_proj(ys_x, hx, mod_x[:, 2], lp["group_norm_g"], w_out, lp["norm2_g"],
                         mod_x[:, 3], mod_x[:, 4], w_router, b_router, tm=512)
    h_tok = hx1.reshape(B * S, D)
    hc_tok = None
    lg = lg_x.reshape(B * S, ROUTER_COLS)
    if need_ctx:
        ys_c = [hyena_mixer(hy_c, lp, tables_c), conf(cf_c),
                context_attention(at_c, lp["attn_sink"]), yd_c]
        hc1, lg_c = out_proj(ys_c, hc, mod_c[:, 2], lp["group_norm_g"], w_out, lp["norm2_g"],
                             mod_c[:, 3], mod_c[:, 4], w_router, b_router, tm=256)
        hc_tok = hc1.reshape(B * C, D)
        lg = jnp.concatenate([lg, lg_c.reshape(B * C, ROUTER_COLS)], axis=0)

    T = lg.shape[0]
    n_blocks = -(-T // MOE_BLOCK) + N_CLASSES
    info, counts, ids = route_tokens(lg)
    dest, blk_a, blk_b, n_used = slot_plan(ids, counts, n_blocks)
    xs = moe_dispatch(h_tok, hc_tok, dest, n_blocks, lp["norm2_g"], (mod_x[:, 3], mod_x[:, 4]),
                      (mod_c[:, 3], mod_c[:, 4]), S)
    o_sorted = expert_pairs(xs, blk_a, blk_b, n_used, *experts, layer)
    hx2 = moe_collect(o_sorted, dest, info, 0, h_tok, mod_x[:, 5], S, final_g, final_norm)
    hx2 = hx2.reshape(B, S, D)
    if need_ctx:
        hc2 = moe_collect(o_sorted, dest, info, B * S // DISPATCH_TOKENS, hc_tok, mod_c[:1, 5], C,
                          final_g, False).reshape(B, C, D)
    else:
        hc2 = hc
    return hc2, hx2


def kernel(x, c, ctx, c_ctx, norm1_g, norm2_g, ada_w, ada_b, w_in, hy_short_w, hy_short_b, hy_ffn_w1, hy_ffn_b1, hy_ffn_w2, hy_ffn_b2, hy_ffn_w3, hy_bias, conf_dw_w, conf_dw_b, conf_ln_g, conf_ln_b, attn_sink, lru_conv_w, lru_conv_b, lru_wa, lru_ba, lru_wx, lru_bx, lru_lambda, group_norm_g, w_out, router_g_w, router_g_b, router_e_w, router_e_b, exp_w_gate, exp_w_up, exp_w_down, final_norm_g):
    stacked = dict(norm1_g=norm1_g, norm2_g=norm2_g, ada_w=ada_w, ada_b=ada_b, w_in=w_in,
                   hy_short_w=hy_short_w, hy_short_b=hy_short_b, hy_ffn_w1=hy_ffn_w1, hy_ffn_b1=hy_ffn_b1,
                   hy_ffn_w2=hy_ffn_w2, hy_ffn_b2=hy_ffn_b2, hy_ffn_w3=hy_ffn_w3, hy_bias=hy_bias,
                   conf_dw_w=conf_dw_w, conf_dw_b=conf_dw_b, conf_ln_g=conf_ln_g, conf_ln_b=conf_ln_b,
                   attn_sink=attn_sink, lru_conv_w=lru_conv_w, lru_conv_b=lru_conv_b, lru_wa=lru_wa,
                   lru_ba=lru_ba, lru_wx=lru_wx, lru_bx=lru_bx, lru_lambda=lru_lambda,
                   group_norm_g=group_norm_g, w_out=w_out, router_g_w=router_g_w, router_g_b=router_g_b,
                   router_e_w=router_e_w, router_e_b=router_e_b)
    experts = (exp_w_gate.astype(BF16), exp_w_up.astype(BF16), exp_w_down.astype(BF16))
    depth = norm1_g.shape[0]
    B = x.shape[0]
    cs = jnp.concatenate([jax.nn.silu(c), jnp.broadcast_to(jax.nn.silu(c_ctx)[None], (8, c.shape[1]))], axis=0)
    hc, hx = ctx, x
    tables_x = fft_tables(x.shape[1])
    tables_c = dft_tables(ctx.shape[1])
    for l in range(depth):
        lp = {k: v[l] for k, v in stacked.items()}
        hc, hx = _layer(hc, hx, cs, lp, need_ctx=(l < depth - 1), final_g=final_norm_g,
                        final_norm=(l == depth - 1), tables_x=tables_x, tables_c=tables_c,
                        experts=experts, layer=l)
    return hx
```

```python
import functools
import math

import jax
import jax.numpy as jnp
from jax import lax
from jax.experimental import pallas as pl
from jax.experimental.pallas import tpu as pltpu

F32 = jnp.float32
BF16 = jnp.bfloat16

EPS = 1e-6
NEG_INF = -1e30
GRID_W = 64
N_MIXERS = 4
D_GROUP = 256
HY_COLS = 3 * D_GROUP
CONF_COLS = 2 * D_GROUP
ATT_HEADS = 4
ATT_KV_HEADS = 2
HEAD_DIM = 64
ATT_COLS = (ATT_HEADS + 2 * ATT_KV_HEADS) * HEAD_DIM
LRU_COLS = 2 * D_GROUP
QK_COLS = (ATT_HEADS + ATT_KV_HEADS) * HEAD_DIM
WINDOW = 128
ATT_BLOCK = 128
ROPE_BASE = 10000.0
HY_EMB = 33
HY_FAST_DECAY = 0.3
HY_SLOW_DECAY = 1.5
HY_TARGET = 1e-2
CONF_KERNEL = 31
LRU_HEADS = 4
LRU_CONV = 4
LRU_C = 8.0
N_GROUPS = 4
EXP_PER_GROUP = 8
N_EXPERTS = N_GROUPS * EXP_PER_GROUP
TOP_K = 2
MOE_BLOCK = 256
ROUTER_COLS = 128

VMEM_LIMIT_BYTES = 56 * 1024 * 1024


def _cparams(*sem):
    return pltpu.CompilerParams(dimension_semantics=sem, vmem_limit_bytes=VMEM_LIMIT_BYTES)


def _linear_kernel(x_ref, w_ref, b_ref, o_ref):
    o_ref[...] = jnp.dot(x_ref[...], w_ref[...], preferred_element_type=F32,
                         precision=lax.Precision.HIGHEST) + b_ref[...]


def small_linear(x, w, b, tn=1024):
    M, K = x.shape
    N = w.shape[1]
    return pl.pallas_call(
        _linear_kernel,
        grid=(N // tn,),
        in_specs=[pl.BlockSpec((M, K), lambda j: (0, 0)),
                  pl.BlockSpec((K, tn), lambda j: (0, j)),
                  pl.BlockSpec((1, tn), lambda j: (0, j))],
        out_specs=pl.BlockSpec((M, tn), lambda j: (0, j)),
        out_shape=jax.ShapeDtypeStruct((M, N), F32),
        compiler_params=_cparams("parallel"),
        name="ada_linear",
    )(x, w, b.reshape(1, N))


def _in_proj_kernel(x_ref, sh_ref, sc_ref, g_ref, w_ref, cos_ref, sin_ref,
                    hy_ref, cf_ref, at_ref, lr_ref):
    x = x_ref[0]
    ms = jnp.mean(x * x, axis=-1, keepdims=True)
    y = x * lax.rsqrt(ms + EPS) * g_ref[...]
    y = y * (1.0 + sc_ref[0]) + sh_ref[0]
    u = jnp.dot(y.astype(BF16), w_ref[...], preferred_element_type=F32)
    c0 = HY_COLS
    c1 = c0 + CONF_COLS
    c2 = c1 + ATT_COLS
    c3 = c2 + LRU_COLS
    hy_ref[0] = u[:, :c0]
    cf_ref[0] = u[:, c0:c1]
    lr_ref[0] = u[:, c2:c3]
    qk = u[:, c1:c1 + QK_COLS]
    qk_rot = u[:, c3:c3 + QK_COLS]
    at_ref[0, :, :QK_COLS] = qk * cos_ref[...] + qk_rot * sin_ref[...]
    at_ref[0, :, QK_COLS:] = u[:, c1 + QK_COLS:c2]


def in_proj(h, shift, scale, g, w_ext, cos_t, sin_t, tm):
    B, L, D = h.shape
    NW = w_ext.shape[1]
    outs = [HY_COLS, CONF_COLS, ATT_COLS, LRU_COLS]
    return pl.pallas_call(
        _in_proj_kernel,
        grid=(B, L // tm),
        in_specs=[pl.BlockSpec((1, tm, D), lambda b, i: (b, i, 0)),
                  pl.BlockSpec((1, 1, D), lambda b, i: (b, 0, 0)),
                  pl.BlockSpec((1, 1, D), lambda b, i: (b, 0, 0)),
                  pl.BlockSpec((1, D), lambda b, i: (0, 0)),
                  pl.BlockSpec((D, NW), lambda b, i: (0, 0)),
                  pl.BlockSpec((tm, QK_COLS), lambda b, i: (i, 0)),
                  pl.BlockSpec((tm, QK_COLS), lambda b, i: (i, 0))],
        out_specs=[pl.BlockSpec((1, tm, n), lambda b, i: (b, i, 0)) for n in outs],
        out_shape=[jax.ShapeDtypeStruct((B, L, n), F32) for n in outs],
        compiler_params=_cparams("parallel", "parallel"),
        name="in_proj",
    )(h, shift, scale, g.reshape(1, D), w_ext, cos_t, sin_t)


def rope_tables(L, rotary):
    n_heads = ATT_HEADS + ATT_KV_HEADS
    if not rotary:
        return jnp.ones((L, QK_COLS), F32), jnp.zeros((L, QK_COLS), F32)
    pos = jnp.arange(L)
    row = (pos // GRID_W).astype(F32)
    col = (pos % GRID_W).astype(F32)
    half = HEAD_DIM // 2
    inv_freq = ROPE_BASE ** (-jnp.arange(0, half, 2, dtype=F32) / half)
    ang_r = row[:, None] * inv_freq[None]
    ang_c = col[:, None] * inv_freq[None]
    cos_h = jnp.concatenate([jnp.cos(ang_r)] * 2 + [jnp.cos(ang_c)] * 2, axis=-1)
    sin_h = jnp.concatenate([jnp.sin(ang_r)] * 2 + [jnp.sin(ang_c)] * 2, axis=-1)
    return jnp.tile(cos_h, (1, n_heads)), jnp.tile(sin_h, (1, n_heads))


def extend_w_in(w_in):
    c1 = HY_COLS + CONF_COLS
    wqk = w_in[:, c1:c1 + QK_COLS]
    D = w_in.shape[0]
    w4 = wqk.reshape(D, QK_COLS // 32, 2, 16)
    wrot = jnp.stack([-w4[:, :, 1], w4[:, :, 0]], axis=2).reshape(D, QK_COLS)
    return jnp.concatenate([w_in, wrot], axis=1).astype(BF16)


def _softmax_parts(q, k_list, extra_logit):
    scale = HEAD_DIM ** -0.5
    s_list = []
    for k, mask in k_list:
        s = lax.dot_general(q, k, (((1,), (1,)), ((), ())), preferred_element_type=F32) * scale
        if mask is not None:
            s = jnp.where(mask, s, NEG_INF)
        s_list.append(s)
    m = extra_logit
    for s in s_list:
        m = jnp.maximum(m, jnp.max(s, axis=-1, keepdims=True))
    p_list = [jnp.exp(s - m) for s in s_list]
    denom = jnp.exp(extra_logit - m)
    for p in p_list:
        denom = denom + jnp.sum(p, axis=-1, keepdims=True)
    return p_list, 1.0 / denom


ATT_Q_BLOCKS = 16


def _win_attn_kernel(sink_ref, q_ref, kp_ref, kc_ref, kn_ref, vp_ref, vc_ref, vn_ref,
                     kx_ref, vx_ref, o_ref, *, seq_len):
    i = pl.program_id(1)
    blk = ATT_BLOCK
    qb = q_ref.shape[1] // blk
    scale = HEAD_DIM ** -0.5
    g = ATT_HEADS // ATT_KV_HEADS
    kw = jnp.concatenate([kp_ref[0], kc_ref[0], kn_ref[0]], axis=0)
    vw = jnp.concatenate([vp_ref[0], vc_ref[0], vn_ref[0]], axis=0).astype(BF16)
    kwt = kw.T.astype(BF16)
    kxt = kx_ref[0].T.astype(BF16)
    vx = vx_ref[0].astype(BF16)
    row = lax.broadcasted_iota(jnp.int32, (g * blk, 3 * blk), 0) % blk
    col = lax.broadcasted_iota(jnp.int32, (g * blk, 3 * blk), 1)
    band_bias = jnp.where(jnp.abs(col - blk - row) <= WINDOW, 0.0, NEG_INF)
    col1 = lax.broadcasted_iota(jnp.int32, (1, 3 * blk), 1)
    for j in range(qb):
        q_blk = i * qb + j
        k_pos = (q_blk - 1) * blk + col1
        edge_bias = jnp.where(k_pos >= 0, jnp.where(k_pos < seq_len, 0.0, NEG_INF), NEG_INF)
        bias = band_bias + edge_bias
        outs = []
        for kv in range(ATT_KV_HEADS):
            ksl = slice(kv * HEAD_DIM, (kv + 1) * HEAD_DIM)
            heads = range(kv * g, (kv + 1) * g)
            qs = (jnp.concatenate([q_ref[0, j * blk:(j + 1) * blk, h * HEAD_DIM:(h + 1) * HEAD_DIM]
                                   for h in heads], axis=0) * scale).astype(BF16)
            sink = jnp.concatenate([jnp.full((blk, 1), sink_ref[h], F32) for h in heads], axis=0)
            s_win = jnp.dot(qs, kwt[ksl, j * blk:(j + 3) * blk], preferred_element_type=F32) + bias
            s_ctx = jnp.dot(qs, kxt[ksl, :], preferred_element_type=F32)
            m = jnp.maximum(jnp.maximum(jnp.max(s_win, axis=-1, keepdims=True),
                                        jnp.max(s_ctx, axis=-1, keepdims=True)), sink)
            p_win = jnp.exp(s_win - m)
            p_ctx = jnp.exp(s_ctx - m)
            denom = (jnp.exp(sink - m) + jnp.sum(p_win, axis=-1, keepdims=True)
                     + jnp.sum(p_ctx, axis=-1, keepdims=True))
            o = (jnp.dot(p_win.astype(BF16), vw[j * blk:(j + 3) * blk, ksl], preferred_element_type=F32)
                 + jnp.dot(p_ctx.astype(BF16), vx[:, ksl], preferred_element_type=F32)) * (1.0 / denom)
            outs.extend([o[k * blk:(k + 1) * blk] for k in range(g)])
        o_ref[0, j * blk:(j + 1) * blk, :] = jnp.concatenate(outs, axis=-1)


def window_attention(at_x, at_c, sink):
    B, S, _ = at_x.shape
    C = at_c.shape[1]
    blk = ATT_BLOCK
    qb = ATT_Q_BLOCKS
    nb = S // blk
    kcol = QK_COLS // 128 - 1
    vcol = kcol + 1

    def edge_spec(col, off):
        return pl.BlockSpec((1, blk, 128), lambda b, i, s: (b, jnp.clip(i * qb + off, 0, nb - 1), col))

    def mid_spec(col):
        return pl.BlockSpec((1, qb * blk, 128), lambda b, i, s: (b, i, col))

    grid_spec = pltpu.PrefetchScalarGridSpec(
        num_scalar_prefetch=1,
        grid=(B, nb // qb),
        in_specs=[pl.BlockSpec((1, qb * blk, ATT_HEADS * HEAD_DIM), lambda b, i, s: (b, i, 0)),
                  edge_spec(kcol, -1), mid_spec(kcol), edge_spec(kcol, qb),
                  edge_spec(vcol, -1), mid_spec(vcol), edge_spec(vcol, qb),
                  pl.BlockSpec((1, C, 128), lambda b, i, s: (b, 0, kcol)),
                  pl.BlockSpec((1, C, 128), lambda b, i, s: (b, 0, vcol))],
        out_specs=pl.BlockSpec((1, qb * blk, ATT_HEADS * HEAD_DIM), lambda b, i, s: (b, i, 0)),
    )
    return pl.pallas_call(
        functools.partial(_win_attn_kernel, seq_len=S),
        grid_spec=grid_spec,
        out_shape=jax.ShapeDtypeStruct((B, S, ATT_HEADS * HEAD_DIM), F32),
        compiler_params=_cparams("parallel", "parallel"),
        name="window_attention",
    )(sink.astype(F32), at_x, at_x, at_x, at_x, at_x, at_x, at_x, at_c, at_c)


def _ctx_attn_kernel(sink_ref, q_ref, kx_ref, vx_ref, o_ref):
    q = q_ref[0].astype(BF16)
    kx = kx_ref[0].astype(BF16)
    vx = vx_ref[0].astype(BF16)
    g = ATT_HEADS // ATT_KV_HEADS
    outs = []
    for h in range(ATT_HEADS):
        kv = h // g
        qs = q[:, h * HEAD_DIM:(h + 1) * HEAD_DIM]
        ksl = slice(kv * HEAD_DIM, (kv + 1) * HEAD_DIM)
        (p_ctx,), inv = _softmax_parts(qs, [(kx[:, ksl], None)], sink_ref[h])
        outs.append(jnp.dot(p_ctx.astype(BF16), vx[:, ksl], preferred_element_type=F32) * inv)
    o_ref[0] = jnp.concatenate(outs, axis=-1)


def context_attention(at_c, sink):
    B, C, _ = at_c.shape
    kcol = QK_COLS // 128 - 1
    grid_spec = pltpu.PrefetchScalarGridSpec(
        num_scalar_prefetch=1,
        grid=(B,),
        in_specs=[pl.BlockSpec((1, C, ATT_HEADS * HEAD_DIM), lambda b, s: (b, 0, 0)),
                  pl.BlockSpec((1, C, 128), lambda b, s: (b, 0, kcol)),
                  pl.BlockSpec((1, C, 128), lambda b, s: (b, 0, kcol + 1))],
        out_specs=pl.BlockSpec((1, C, ATT_HEADS * HEAD_DIM), lambda b, s: (b, 0, 0)),
    )
    return pl.pallas_call(
        _ctx_attn_kernel,
        grid_spec=grid_spec,
        out_shape=jax.ShapeDtypeStruct((B, C, ATT_HEADS * HEAD_DIM), F32),
        compiler_params=_cparams("parallel"),
        name="context_attention",
    )(sink.astype(F32), at_c, at_c, at_c)


def _out_proj_kernel(y0_ref, y1_ref, y2_ref, y3_ref, h_ref, g1_ref, gng_ref, w_ref,
                     n2g_ref, sh_ref, sc_ref, wr_ref, br_ref, ho_ref, lg_ref):
    parts = []
    for k, y_ref in enumerate((y0_ref, y1_ref, y2_ref, y3_ref)):
        y = y_ref[0]
        ms = jnp.mean(y * y, axis=-1, keepdims=True)
        yn = y * lax.rsqrt(ms + EPS) * gng_ref[:, k * D_GROUP:(k + 1) * D_GROUP]
        parts.append(yn.astype(BF16))
    yn = jnp.concatenate(parts, axis=-1)
    proj = jnp.dot(yn, w_ref[...], preferred_element_type=F32)
    h = h_ref[0] + g1_ref[0] * proj
    ho_ref[0] = h
    ms = jnp.mean(h * h, axis=-1, keepdims=True)
    n = h * lax.rsqrt(ms + EPS) * n2g_ref[...]
    n = n * (1.0 + sc_ref[0]) + sh_ref[0]
    lg_ref[0] = jnp.dot(n.astype(BF16), wr_ref[...], preferred_element_type=F32) + br_ref[...]


def out_proj(ys, h, g1, gng, w_out, n2g, sh2, sc2, w_router, b_router, tm):
    B, L, D = h.shape
    row3 = lambda n: pl.BlockSpec((1, tm, n), lambda b, i: (b, i, 0))
    mod = pl.BlockSpec((1, 1, D), lambda b, i: (b, 0, 0))
    full = lambda r, c: pl.BlockSpec((r, c), lambda b, i: (0, 0))
    return pl.pallas_call(
        _out_proj_kernel,
        grid=(B, L // tm),
        in_specs=[row3(D_GROUP)] * 4 + [row3(D), mod, full(1, D), full(D, D), full(1, D), mod, mod,
                                        full(D, ROUTER_COLS), full(1, ROUTER_COLS)],
        out_specs=[row3(D), row3(ROUTER_COLS)],
        out_shape=[jax.ShapeDtypeStruct((B, L, D), F32), jax.ShapeDtypeStruct((B, L, ROUTER_COLS), F32)],
        compiler_params=_cparams("parallel", "parallel"),
        name="out_proj",
    )(*ys, h, g1, gng.reshape(1, D), w_out, n2g.reshape(1, D), sh2, sc2, w_router, b_router)


N_PAIRS = EXP_PER_GROUP * (EXP_PER_GROUP - 1) // 2
N_CLASSES = N_GROUPS * N_PAIRS
ROUTE_TOKENS = 512
INFO_CLASS, INFO_RANK, INFO_WA, INFO_WB = 0, 1, 2, 3


SUBLANES = 8


def _route_kernel(lg_ref, below_ref, info_ref, cnt_ref, ids_ref, run):
    i = pl.program_id(0)

    @pl.when(i == 0)
    def _():
        run[...] = jnp.zeros_like(run)

    lg = lg_ref[...]
    li = lax.broadcasted_iota(jnp.int32, lg.shape, 1).astype(F32)
    big = float(ROUTER_COLS)

    def first_argmax(vals):
        m = jnp.max(vals, axis=-1, keepdims=True)
        return m, jnp.min(jnp.where(vals == m, li, big), axis=-1, keepdims=True)

    gl = jnp.where(li < N_GROUPS, lg, NEG_INF)
    gmax, g_idx = first_argmax(gl)
    g_prob = 1.0 / jnp.sum(jnp.exp(gl - gmax), axis=-1, keepdims=True)
    lo = N_GROUPS + EXP_PER_GROUP * g_idx
    el = jnp.where(li >= lo, jnp.where(li < lo + EXP_PER_GROUP, lg, NEG_INF), NEG_INF)
    m1, i1 = first_argmax(el)
    m2, i2 = first_argmax(jnp.where(li == i1, NEG_INF, el))
    e2 = jnp.exp(m2 - m1)
    w1 = g_prob / (1.0 + e2)
    w2 = g_prob * e2 / (1.0 + e2)
    j1 = i1 - lo
    j2 = i2 - lo
    a = jnp.minimum(j1, j2)
    b = jnp.maximum(j1, j2)
    cls = g_idx * N_PAIRS + (a * (2 * EXP_PER_GROUP - 1 - a)) * 0.5 + (b - a - 1.0)
    w_a = jnp.where(j1 < j2, w1, w2)
    w_b = jnp.where(j1 < j2, w2, w1)

    hit = li == cls
    onehot = jnp.where(hit, 1.0, 0.0)
    before = jnp.dot(below_ref[...], onehot.astype(BF16), preferred_element_type=F32)
    rank = jnp.sum(jnp.where(hit, before + run[...], 0.0), axis=-1, keepdims=True)
    run[...] = run[...] + jnp.sum(onehot, axis=0, keepdims=True)
    cnt_ref[...] = run[...]
    info = jnp.where(li == INFO_CLASS, cls, 0.0)
    info = jnp.where(li == INFO_RANK, rank, info)
    info = jnp.where(li == INFO_WA, w_a, info)
    info = jnp.where(li == INFO_WB, w_b, info)
    info_ref[...] = info
    ids_ref[0] = info.T[:SUBLANES].astype(jnp.int32)


def route_tokens(logits):
    T = logits.shape[0]
    tb = ROUTE_TOKENS
    below = (jnp.arange(tb)[None, :] < jnp.arange(tb)[:, None]).astype(BF16)
    return pl.pallas_call(
        _route_kernel,
        grid=(T // tb,),
        in_specs=[pl.BlockSpec((tb, ROUTER_COLS), lambda i: (i, 0)),
                  pl.BlockSpec((tb, tb), lambda i: (0, 0))],
        out_specs=[pl.BlockSpec((tb, ROUTER_COLS), lambda i: (i, 0)),
                   pl.BlockSpec((1, ROUTER_COLS), lambda i: (0, 0)),
                   pl.BlockSpec((1, SUBLANES, tb), lambda i: (i, 0, 0))],
        out_shape=[jax.ShapeDtypeStruct((T, ROUTER_COLS), F32), jax.ShapeDtypeStruct((1, ROUTER_COLS), F32),
                   jax.ShapeDtypeStruct((T // tb, SUBLANES, tb), jnp.int32)],
        scratch_shapes=[pltpu.VMEM((1, ROUTER_COLS), F32)],
        compiler_params=_cparams("arbitrary"),
        name="moe_route",
    )(logits, below)


def _pair_tables():
    a_tab, b_tab = [], []
    for g in range(N_GROUPS):
        for a in range(EXP_PER_GROUP):
            for b in range(a + 1, EXP_PER_GROUP):
                a_tab.append(g * EXP_PER_GROUP + a)
                b_tab.append(g * EXP_PER_GROUP + b)
    return jnp.array(a_tab, jnp.int32), jnp.array(b_tab, jnp.int32)


def _slot_kernel(ids_ref, start_ref, dest_ref):
    cls = ids_ref[0, INFO_CLASS:INFO_CLASS + 1, :]
    rank = ids_ref[0, INFO_RANK:INFO_RANK + 1, :]
    ci = lax.broadcasted_iota(jnp.int32, (ROUTER_COLS, cls.shape[1]), 0)
    start = jnp.sum(jnp.where(ci == cls, start_ref[...], 0), axis=0, keepdims=True)
    dest_ref[0] = jnp.broadcast_to(start + rank, dest_ref.shape[1:])


def slot_plan(ids, counts, n_blocks):
    nt, _, tb = ids.shape
    cnt = counts[0, :N_CLASSES].astype(jnp.int32)
    padded = (cnt + MOE_BLOCK - 1) // MOE_BLOCK * MOE_BLOCK
    upto = jnp.arange(N_CLASSES)[None, :] <= jnp.arange(N_CLASSES)[:, None]
    pad_end = jnp.sum(jnp.where(upto, padded[None, :], 0), axis=1)
    class_start = jnp.zeros((ROUTER_COLS, 1), jnp.int32).at[:N_CLASSES, 0].set(pad_end - padded)
    dest = pl.pallas_call(
        _slot_kernel,
        grid=(nt,),
        in_specs=[pl.BlockSpec((1, SUBLANES, tb), lambda i: (i, 0, 0)),
                  pl.BlockSpec((ROUTER_COLS, 1), lambda i: (0, 0))],
        out_specs=pl.BlockSpec((1, SUBLANES, tb), lambda i: (i, 0, 0)),
        out_shape=jax.ShapeDtypeStruct((nt, SUBLANES, tb), jnp.int32),
        compiler_params=_cparams("parallel"),
        name="moe_slots",
    )(ids, class_start)[:, 0, :].reshape(nt * tb)
    n_used = (pad_end[-1] // MOE_BLOCK).astype(jnp.int32).reshape(1)
    blk_first = jnp.arange(n_blocks, dtype=jnp.int32) * MOE_BLOCK
    blk_cls = jnp.minimum(jnp.sum((pad_end[None, :] <= blk_first[:, None]).astype(jnp.int32), axis=1),
                          N_CLASSES - 1)
    a_tab, b_tab = _pair_tables()
    hit = blk_cls[:, None] == jnp.arange(N_CLASSES)[None, :]
    pick = lambda tab: jnp.sum(jnp.where(hit, tab[None, :], 0), axis=1).astype(jnp.int32)
    return dest, pick(a_tab), pick(b_tab), n_used


DISPATCH_TOKENS = 1024


def _wait_rows(buf, sem):
    pltpu.make_async_copy(buf, buf, sem).wait()


DMA_UNROLL = 8
TOKEN_TILE_ROWS = 8


def _store_token_tiles(tiles_ref, offset, pitch, x):
    n = x.shape[0]
    for j in range(x.shape[1] // LANES):
        tiles_ref[pl.ds(offset + j, n, stride=pitch), :] = x[:, j * LANES:(j + 1) * LANES]


def _load_token_tiles(tiles_ref, offset, pitch, n, width):
    return jnp.concatenate([tiles_ref[pl.ds(offset + j, n, stride=pitch), :] for j in range(width // LANES)],
                           axis=-1)


def _dispatch_kernel(dest_ref, hx_ref, hc_ref, g_ref, shx_ref, scx_ref, shc_ref, scc_ref, zeros_hbm,
                     xs_hbm, rows, sems, *, n_latent_blocks):
    del zeros_hbm
    i = pl.program_id(0)
    n = pl.num_programs(0)
    slot = i % 2
    tb = hx_ref.shape[0]

    @pl.when(i >= 2)
    def _():
        _wait_rows(rows.at[slot], sems.at[slot])

    def normed(h_ref, sh_ref, sc_ref):
        h = h_ref[...]
        ms = jnp.mean(h * h, axis=-1, keepdims=True)
        return h * lax.rsqrt(ms + EPS) * g_ref[...] * (1.0 + sc_ref[0]) + sh_ref[0]

    @pl.when(i < n_latent_blocks)
    def _():
        _store_token_tiles(rows.at[slot], 0, TOKEN_TILE_ROWS, normed(hx_ref, shx_ref, scx_ref))

    @pl.when(i >= n_latent_blocks)
    def _():
        _store_token_tiles(rows.at[slot], 0, TOKEN_TILE_ROWS, normed(hc_ref, shc_ref, scc_ref))

    def body(g, carry):
        for u in range(DMA_UNROLL):
            r = g * DMA_UNROLL + u
            dst = pl.multiple_of(dest_ref[0, 0, r] * TOKEN_TILE_ROWS, TOKEN_TILE_ROWS)
            pltpu.make_async_copy(rows.at[slot, pl.ds(r * TOKEN_TILE_ROWS, TOKEN_TILE_ROWS)],
                                  xs_hbm.at[pl.ds(dst, TOKEN_TILE_ROWS)], sems.at[slot]).start(priority=u % 2)
        return carry
    lax.fori_loop(0, tb // DMA_UNROLL, body, 0)

    @pl.when(i == n - 1)
    def _():
        _wait_rows(rows.at[slot], sems.at[slot])

        @pl.when(n >= 2)
        def _():
            _wait_rows(rows.at[1 - slot], sems.at[1 - slot])


def moe_dispatch(h_x, h_c, dest, n_blocks, n2g, mod_x, mod_c, tokens_per_batch):
    Tx, D = h_x.shape
    tb = DISPATCH_TOKENS
    nxb = Tx // tb
    if h_c is None:
        h_c, mod_c, ncb = h_x, mod_x, 0
    else:
        ncb = h_c.shape[0] // tb
    per_b = tokens_per_batch // tb
    P = n_blocks * MOE_BLOCK
    tile_rows = D // LANES
    assert tile_rows == TOKEN_TILE_ROWS
    xi = lambda i: jnp.minimum(i, nxb - 1)
    ci = lambda i: jnp.maximum(i - nxb, 0)
    modx = pl.BlockSpec((1, 1, D), lambda i: (xi(i) // per_b, 0, 0))
    modc = pl.BlockSpec((1, 1, D), lambda i: (0, 0, 0))
    return pl.pallas_call(
        functools.partial(_dispatch_kernel, n_latent_blocks=nxb),
        grid=(nxb + ncb,),
        in_specs=[pl.BlockSpec((1, 1, tb), lambda i: (i, 0, 0), memory_space=pltpu.SMEM),
                  pl.BlockSpec((tb, D), lambda i: (xi(i), 0)),
                  pl.BlockSpec((tb, D), lambda i: (ci(i), 0)),
                  pl.BlockSpec((1, D), lambda i: (0, 0)),
                  modx, modx, modc, modc,
                  pl.BlockSpec(memory_space=pl.ANY)],
        out_specs=pl.BlockSpec(memory_space=pl.ANY),
        out_shape=jax.ShapeDtypeStruct((P * tile_rows, LANES), F32),
        scratch_shapes=[pltpu.VMEM((2, tb * tile_rows, LANES), F32), pltpu.SemaphoreType.DMA((2,))],
        input_output_aliases={8: 0},
        compiler_params=_cparams("arbitrary"),
        name="moe_dispatch",
    )(dest.reshape(-1, 1, tb), h_x, h_c, n2g.reshape(1, D), mod_x[0], mod_x[1], mod_c[0], mod_c[1],
      jnp.zeros((P * tile_rows, LANES), F32))


def _expert_pair_kernel(ea_ref, eb_ref, nused_ref, xs_ref, wga_ref, wua_ref, wda_ref, wgb_ref, wub_ref, wdb_ref,
                        o_ref):
    del ea_ref, eb_ref
    i = pl.program_id(0)
    D = wga_ref.shape[2]

    @pl.when(i < nused_ref[0])
    def _():
        xb = _load_token_tiles(xs_ref, 0, TOKEN_TILE_ROWS, MOE_BLOCK, D).astype(BF16)
        halves = []
        for wg_ref, wu_ref, wd_ref in ((wga_ref, wua_ref, wda_ref), (wgb_ref, wub_ref, wdb_ref)):
            gate = jnp.dot(xb, wg_ref[0, 0], preferred_element_type=F32)
            up = jnp.dot(xb, wu_ref[0, 0], preferred_element_type=F32)
            hid = (gate * jax.nn.sigmoid(gate) * up).astype(BF16)
            out = jnp.dot(hid, wd_ref[0, 0], preferred_element_type=F32)
            halves.append(lax.bitcast_convert_type(out.astype(BF16).astype(F32), jnp.uint32))
        _store_token_tiles(o_ref, 0, TOKEN_TILE_ROWS, halves[0] | (halves[1] >> 16))

    @pl.when(i >= nused_ref[0])
    def _():
        o_ref[...] = jnp.zeros_like(o_ref)


def _unpack_pair(words):
    hi = lax.bitcast_convert_type(words & jnp.uint32(0xFFFF0000), F32)
    lo = lax.bitcast_convert_type(words << 16, F32)
    return hi, lo


def expert_pairs(xs, blk_a, blk_b, n_used, w_gate, w_up, w_down, layer):
    D, DE = w_gate.shape[2:]
    P = xs.shape[0] // TOKEN_TILE_ROWS
    n_blocks = P // MOE_BLOCK
    wspec = lambda shape, which: pl.BlockSpec(shape, lambda i, ea, eb, nu: (layer, (ea, eb)[which][i], 0, 0))
    grid_spec = pltpu.PrefetchScalarGridSpec(
        num_scalar_prefetch=3,
        grid=(n_blocks,),
        in_specs=[pl.BlockSpec((MOE_BLOCK * TOKEN_TILE_ROWS, LANES), lambda i, ea, eb, nu: (i, 0)),
                  wspec((1, 1, D, DE), 0), wspec((1, 1, D, DE), 0), wspec((1, 1, DE, D), 0),
                  wspec((1, 1, D, DE), 1), wspec((1, 1, D, DE), 1), wspec((1, 1, DE, D), 1)],
        out_specs=pl.BlockSpec((MOE_BLOCK * TOKEN_TILE_ROWS, LANES), lambda i, ea, eb, nu: (i, 0)),
    )
    return pl.pallas_call(
        _expert_pair_kernel,
        grid_spec=grid_spec,
        out_shape=jax.ShapeDtypeStruct((P * TOKEN_TILE_ROWS, LANES), jnp.uint32),
        compiler_params=_cparams("arbitrary"),
        name="moe_experts",
    )(blk_a, blk_b, n_used, xs, w_gate, w_up, w_down, w_gate, w_up, w_down)


def _gather_pairs(idx_ref, src_hbm, buf, sem, n_tokens):
    def body(g, carry):
        for u in range(DMA_UNROLL):
            r = g * DMA_UNROLL + u
            src = pl.multiple_of(idx_ref[0, 0, r] * TOKEN_TILE_ROWS, TOKEN_TILE_ROWS)
            pltpu.make_async_copy(src_hbm.at[pl.ds(src, TOKEN_TILE_ROWS)],
                                  buf.at[pl.ds(r * TOKEN_TILE_ROWS, TOKEN_TILE_ROWS)], sem).start(priority=u % 2)
        return carry
    lax.fori_loop(0, n_tokens // DMA_UNROLL, body, 0)


def _collect_kernel(dest_ref, dest_next_ref, o_hbm, info_ref, h_ref, g2_ref, fg_ref, out_ref, obuf, sems, *,
                    final_norm):
    i = pl.program_id(0)
    n = pl.num_programs(0)
    slot = i % 2
    tb, D = h_ref.shape

    @pl.when(i == 0)
    def _():
        _gather_pairs(dest_ref, o_hbm, obuf.at[0], sems.at[0], tb)

    @pl.when(i + 1 < n)
    def _():
        _gather_pairs(dest_next_ref, o_hbm, obuf.at[1 - slot], sems.at[1 - slot], tb)

    _wait_rows(obuf.at[slot], sems.at[slot])
    e_a, e_b = _unpack_pair(_load_token_tiles(obuf.at[slot], 0, TOKEN_TILE_ROWS, tb, D))
    m = info_ref[:, INFO_WA:INFO_WA + 1] * e_a + info_ref[:, INFO_WB:INFO_WB + 1] * e_b
    h = h_ref[...] + g2_ref[0] * m
    if final_norm:
        ms = jnp.mean(h * h, axis=-1, keepdims=True)
        h = h * lax.rsqrt(ms + EPS) * fg_ref[...]
    out_ref[...] = h


def moe_collect(o_sorted, dest, info, block_offset, h_tokens, g2, tokens_per_batch, final_g, final_norm):
    T, D = h_tokens.shape
    tb = DISPATCH_TOKENS
    nt = T // tb
    if g2.shape[0] == 1:
        g2_index = lambda i: 0
    else:
        assert tokens_per_batch % tb == 0
        g2_index = lambda i: i // (tokens_per_batch // tb)
    last = block_offset + nt - 1
    dest3 = dest.reshape(-1, 1, tb)
    return pl.pallas_call(
        functools.partial(_collect_kernel, final_norm=final_norm),
        grid=(nt,),
        in_specs=[pl.BlockSpec((1, 1, tb), lambda i: (block_offset + i, 0, 0), memory_space=pltpu.SMEM),
                  pl.BlockSpec((1, 1, tb), lambda i: (jnp.minimum(block_offset + i + 1, last), 0, 0),
                               memory_space=pltpu.SMEM),
                  pl.BlockSpec(memory_space=pl.ANY),
                  pl.BlockSpec((tb, ROUTER_COLS), lambda i: (block_offset + i, 0)),
                  pl.BlockSpec((tb, D), lambda i: (i, 0)),
                  pl.BlockSpec((1, 1, D), lambda i: (g2_index(i), 0, 0)),
                  pl.BlockSpec((1, D), lambda i: (0, 0))],
        out_specs=pl.BlockSpec((tb, D), lambda i: (i, 0)),
        out_shape=jax.ShapeDtypeStruct((T, D), F32),
        scratch_shapes=[pltpu.VMEM((2, tb * TOKEN_TILE_ROWS, LANES), jnp.uint32), pltpu.SemaphoreType.DMA((2,))],
        compiler_params=_cparams("arbitrary"),
        name="moe_collect",
    )(dest3, dest3, o_sorted, info, h_tokens, g2, final_g.reshape(1, D))


CONV_MARGIN = 16


def _time_chunk(L):
    return min(L, 256)


LANES = 128


def _zero_margins(pad_ref, L):
    zeros = jnp.zeros((CONV_MARGIN, LANES), F32)
    for s in range(pad_ref.shape[0]):
        pad_ref[s, pl.ds(0, CONV_MARGIN), :] = zeros
        pad_ref[s, pl.ds(CONV_MARGIN + L, CONV_MARGIN), :] = zeros


def _dw_conv_slab(pad_ref, s, base, T, w_ref, b_ref, col, taps, pad_left):
    acc = jnp.broadcast_to(b_ref[:, col:col + LANES], (T, LANES))
    for k in range(taps):
        acc = acc + w_ref[k:k + 1, col:col + LANES] * pad_ref[s, pl.ds(base + (CONV_MARGIN - pad_left + k), T), :]
    return acc


def _conformer_kernel(u_ref, w_ref, b_ref, g_ref, beta_ref, o_ref, ypad):
    L = o_ref.shape[1]
    T = _time_chunk(L)
    C = D_GROUP
    n_slabs = C // LANES
    pad = (CONF_KERNEL - 1) // 2
    _zero_margins(ypad, L)

    def glu(j, carry):
        base = pl.multiple_of(j * T, T)
        for s in range(n_slabs):
            a = u_ref[0, pl.ds(base, T), s * LANES:(s + 1) * LANES]
            gate = u_ref[0, pl.ds(base, T), C + s * LANES:C + (s + 1) * LANES]
            ypad[s, pl.ds(CONV_MARGIN + base, T), :] = a * jax.nn.sigmoid(gate)
        return carry
    lax.fori_loop(0, L // T, glu, 0)

    def conv(j, carry):
        base = pl.multiple_of(j * T, T)
        acc = jnp.concatenate([_dw_conv_slab(ypad, s, base, T, w_ref, b_ref, s * LANES, CONF_KERNEL, pad)
                               for s in range(n_slabs)], axis=-1)
        mu = jnp.mean(acc, axis=-1, keepdims=True)
        cen = acc - mu
        var = jnp.mean(cen * cen, axis=-1, keepdims=True)
        y = cen * lax.rsqrt(var + EPS) * g_ref[...] + beta_ref[...]
        o_ref[0, pl.ds(base, T), :] = y * jax.nn.sigmoid(y)
        return carry
    lax.fori_loop(0, L // T, conv, 0)


def conformer_conv(u, w, b, ln_g, ln_b):
    B, L, _ = u.shape
    C = D_GROUP
    vec = pl.BlockSpec((1, C), lambda i: (0, 0))
    return pl.pallas_call(
        _conformer_kernel,
        grid=(B,),
        in_specs=[pl.BlockSpec((1, L, 2 * C), lambda i: (i, 0, 0)),
                  pl.BlockSpec((CONF_KERNEL, C), lambda i: (0, 0)), vec, vec, vec],
        out_specs=pl.BlockSpec((1, L, C), lambda i: (i, 0, 0)),
        out_shape=jax.ShapeDtypeStruct((B, L, C), F32),
        scratch_shapes=[pltpu.VMEM((C // LANES, L + 2 * CONV_MARGIN, LANES), F32)],
        compiler_params=_cparams("parallel"),
        name="conformer_conv",
    )(u, w, b.reshape(1, C), ln_g.reshape(1, C), ln_b.reshape(1, C))


def _gelu_tanh(x):
    return 0.5 * x * (1.0 + jnp.tanh(math.sqrt(2.0 / math.pi) * (x + 0.044715 * (x * x * x))))


def _lru_kernel(uc_ref, ux_ref, cw_ref, cb_ref, wcat_ref, bcat_ref, lam_ref, *rest, need_ctx):
    if need_ctx:
        oc_ref, ox_ref, cpad, xpad, a_s, b_s, yx, yc = rest
    else:
        ox_ref, cpad, xpad, a_s, b_s, yx = rest
        oc_ref = yc = None
    C = D_GROUP
    n_slabs = C // LANES
    Lc = uc_ref.shape[1]
    Lx = ux_ref.shape[1]
    pad_l = (LRU_CONV - 1) // 2

    def fill(pad_ref, u_ref, L):
        T = _time_chunk(L)
        _zero_margins(pad_ref, L)

        def body(j, carry):
            base = pl.multiple_of(j * T, T)
            for s in range(n_slabs):
                pad_ref[s, pl.ds(CONV_MARGIN + base, T), :] = u_ref[0, pl.ds(base, T),
                                                                    C + s * LANES:C + (s + 1) * LANES]
            return carry
        lax.fori_loop(0, L // T, body, 0)

    fill(cpad, uc_ref, Lc)
    fill(xpad, ux_ref, Lx)

    def coeffs(pad_ref, base, T, d):
        x = jnp.concatenate([_dw_conv_slab(pad_ref, s, base, T, cw_ref, cb_ref, s * LANES, LRU_CONV, pad_l)
                             for s in range(n_slabs)], axis=-1)
        t = jnp.tanh(jnp.dot(x.astype(BF16), wcat_ref[:, 2 * d * C:2 * (d + 1) * C],
                             preferred_element_type=F32) + bcat_ref[:, 2 * d * C:2 * (d + 1) * C])
        i = 0.5 * t[:, C:] + 0.5
        z = -lam_ref[d:d + 1, :]
        softplus = jnp.maximum(z, 0.0) + jnp.log(1.0 + jnp.exp(-jnp.abs(z)))
        half_rate = (-0.5 * LRU_C) * softplus
        a = jnp.exp(half_rate * t[:, :C] + half_rate)
        b = jnp.sqrt(1.0 - a * a) * (i * x)
        for s in range(n_slabs):
            a_s[d * n_slabs + s, pl.ds(0, T), :] = a[:, s * LANES:(s + 1) * LANES]
            b_s[d * n_slabs + s, pl.ds(0, T), :] = b[:, s * LANES:(s + 1) * LANES]

    def run(pad_ref, L, h, y_ref):
        T = _time_chunk(L)
        n = L // T

        def chunk(j, h):
            base_f = pl.multiple_of(j * T, T)
            base_b = pl.multiple_of((n - 1 - j) * T, T)
            coeffs(pad_ref, base_f, T, 0)
            coeffs(pad_ref, base_b, T, 1)

            def step(t, h):
                new = []
                for d, (base, row) in enumerate(((base_f, t), (base_b, T - 1 - t))):
                    for s in range(n_slabs):
                        k = d * n_slabs + s
                        hs = a_s[k, pl.ds(row, 1), :] * h[k] + b_s[k, pl.ds(row, 1), :]
                        if y_ref is not None:
                            y_ref[k, pl.ds(base + row, 1), :] = hs
                        new.append(hs)
                return tuple(new)
            return lax.fori_loop(0, T, step, h, unroll=8)
        return lax.fori_loop(0, n, chunk, h)

    h = tuple(jnp.zeros((1, LANES), F32) for _ in range(2 * n_slabs))
    h = run(cpad, Lc, h, yc)
    run(xpad, Lx, h, yx)

    def finish(u_ref, y_ref, o_ref, L):
        T = _time_chunk(L)

        def body(j, carry):
            base = pl.multiple_of(j * T, T)
            y = jnp.concatenate([y_ref[s, pl.ds(base, T), :] + y_ref[n_slabs + s, pl.ds(base, T), :]
                                 for s in range(n_slabs)], axis=-1)
            o_ref[0, pl.ds(base, T), :] = _gelu_tanh(u_ref[0, pl.ds(base, T), :C]) * y
            return carry
        lax.fori_loop(0, L // T, body, 0)

    finish(ux_ref, yx, ox_ref, Lx)
    if need_ctx:
        finish(uc_ref, yc, oc_ref, Lc)


def _block_diag(w):
    H, n, _ = w.shape
    eye = jnp.eye(H, dtype=w.dtype)
    return (eye[:, None, :, None] * w[:, :, None, :]).reshape(H * n, H * n)


def rglru_mixer(uc, ux, lp, need_ctx):
    B, Lc, _ = uc.shape
    Lx = ux.shape[1]
    C = D_GROUP
    wcat = (0.5 * jnp.concatenate([_block_diag(lp["lru_wa"][0]), _block_diag(lp["lru_wx"][0]),
                                   _block_diag(lp["lru_wa"][1]), _block_diag(lp["lru_wx"][1])], axis=1)).astype(BF16)
    bcat = 0.5 * jnp.concatenate([lp["lru_ba"][0], lp["lru_bx"][0], lp["lru_ba"][1], lp["lru_bx"][1]]).reshape(1, 4 * C)
    full = lambda r, c: pl.BlockSpec((r, c), lambda i: (0, 0))
    seq = lambda L, n: pl.BlockSpec((1, L, n), lambda i: (i, 0, 0))
    out_specs = [seq(Lx, C)]
    out_shape = [jax.ShapeDtypeStruct((B, Lx, C), F32)]
    if need_ctx:
        out_specs = [seq(Lc, C)] + out_specs
        out_shape = [jax.ShapeDtypeStruct((B, Lc, C), F32)] + out_shape
    T = _time_chunk(Lx)
    slab = lambda rows, n=1: pltpu.VMEM((n * C // LANES, rows, LANES), F32)
    scratch = [slab(Lc + 2 * CONV_MARGIN), slab(Lx + 2 * CONV_MARGIN), slab(T, 2), slab(T, 2), slab(Lx, 2)]
    if need_ctx:
        scratch.append(slab(Lc, 2))
    res = pl.pallas_call(
        functools.partial(_lru_kernel, need_ctx=need_ctx),
        grid=(B,),
        in_specs=[seq(Lc, 2 * C), seq(Lx, 2 * C), full(LRU_CONV, C), full(1, C), full(C, 4 * C),
                  full(1, 4 * C), full(2, C)],
        out_specs=out_specs,
        out_shape=out_shape,
        scratch_shapes=scratch,
        compiler_params=_cparams("parallel"),
        name="rglru",
    )(uc, ux, lp["lru_conv_w"], lp["lru_conv_b"].reshape(1, C), wcat, bcat, lp["lru_lambda"])
    if need_ctx:
        return res[0], res[1]
    return None, res[0]


HY_SHORT = 3


def _short_conv(pad_ref, base, T, w_ref, b_ref, c0, c1):
    return jnp.concatenate([_dw_conv_slab(pad_ref, col // LANES, base, T, w_ref, b_ref, col, HY_SHORT, 1)
                            for col in range(c0, c1, LANES)], axis=-1)


def _fill_padded(pad_ref, u_ref, L, T):
    _zero_margins(pad_ref, L)

    def body(j, carry):
        base = pl.multiple_of(j * T, T)
        for s in range(pad_ref.shape[0]):
            pad_ref[s, pl.ds(CONV_MARGIN + base, T), :] = u_ref[0, pl.ds(base, T), s * LANES:(s + 1) * LANES]
        return carry
    lax.fori_loop(0, L // T, body, 0)


def _hyena_pre_kernel(u_ref, w_ref, b_ref, z_ref, upad):
    L = u_ref.shape[1]
    T = _time_chunk(L)
    C = D_GROUP
    _fill_padded(upad, u_ref, L, T)

    def body(j, carry):
        base = pl.multiple_of(j * T, T)
        x1 = _short_conv(upad, base, T, w_ref, b_ref, C, 2 * C)
        v = _short_conv(upad, base, T, w_ref, b_ref, 2 * C, 3 * C)
        z_ref[pl.ds(base, T), :] = (x1 * v).astype(BF16)
        return carry
    lax.fori_loop(0, L // T, body, 0)


def _hyena_post_kernel(u_ref, y_ref, w_ref, b_ref, bias_ref, o_ref, upad):
    L = u_ref.shape[1]
    T = _time_chunk(L)
    C = D_GROUP
    _fill_padded(upad, u_ref, L, T)

    def body(j, carry):
        base = pl.multiple_of(j * T, T)
        x0 = _short_conv(upad, base, T, w_ref, b_ref, 0, C)
        x1 = _short_conv(upad, base, T, w_ref, b_ref, C, 2 * C)
        v = _short_conv(upad, base, T, w_ref, b_ref, 2 * C, 3 * C)
        o_ref[0, pl.ds(base, T), :] = x0 * (y_ref[pl.ds(base, T), :] + (x1 * v) * bias_ref[...])
        return carry
    lax.fori_loop(0, L // T, body, 0)


def _spectrum_kernel(f_ref, z_ref, ha_ref, hb_ref, hc_ref, y_ref):
    tf = ha_ref.shape[0]
    acc = jnp.dot(f_ref[...], z_ref[...], preferred_element_type=F32)
    zr = acc[:tf]
    zi = acc[tf:]
    y_ref[:tf, :] = (zr * ha_ref[...] - zi * hb_ref[...]).astype(BF16)
    y_ref[tf:, :] = (zr * hb_ref[...] + zi * hc_ref[...]).astype(BF16)


def _idft_kernel(f_ref, y_ref, o_ref):
    o_ref[...] = jnp.dot(f_ref[...], y_ref[...], preferred_element_type=F32)


def dft_tables(L):
    N = 2 * L
    tf = min(256, L)
    k = jnp.arange(L, dtype=jnp.int32)
    n = jnp.arange(L, dtype=jnp.int32)
    ang = (2.0 * math.pi / N) * ((k[:, None] * n[None, :]) % N).astype(F32)
    cos = jnp.cos(ang)
    sin = jnp.sin(ang)
    nyq = jnp.where(n % 2 == 0, 1.0, -1.0).astype(F32)
    f_re = cos
    f_im = (-sin).at[0].set(nyq)
    fwd = jnp.stack([f_re.reshape(L // tf, tf, L), f_im.reshape(L // tf, tf, L)], axis=1).reshape(N, L)
    ck = jnp.where(k == 0, 1.0, 2.0).astype(F32)[:, None] / N
    i_re = cos * ck
    i_im = (-sin * ck).at[0].set(nyq / N)
    inv = jnp.stack([i_re.reshape(L // tf, tf, L), i_im.reshape(L // tf, tf, L)], axis=1).reshape(N, L).T
    return fwd.astype(BF16), inv.astype(BF16)


def filter_spectrum(h_fwd, h_bwd):
    L, C = h_fwd.shape
    k = jnp.concatenate([h_fwd, jnp.zeros((1, C), F32), h_bwd[1:][::-1]], axis=0)
    hf = jnp.fft.rfft(k, axis=0)
    hr = jnp.real(hf)
    hi = jnp.imag(hf)
    a = hr[:L]
    b = hi[:L].at[0].set(0.0)
    c = hr[:L].at[0].set(hr[L])
    return a, b, c


def hyena_mixer(u, lp, tables):
    B, L, _ = u.shape
    C = D_GROUP
    N = 2 * L
    fwd, inv = tables
    tf = min(256, L)
    T = _time_chunk(L)
    w, bsh = lp["hy_short_w"], lp["hy_short_b"].reshape(1, 3 * C)
    z2 = pl.pallas_call(
        _hyena_pre_kernel,
        grid=(B,),
        in_specs=[pl.BlockSpec((1, L, 3 * C), lambda b: (b, 0, 0)),
                  pl.BlockSpec((HY_SHORT, 3 * C), lambda b: (0, 0)),
                  pl.BlockSpec((1, 3 * C), lambda b: (0, 0))],
        out_specs=pl.BlockSpec((L, C), lambda b: (0, b)),
        out_shape=jax.ShapeDtypeStruct((L, B * C), BF16),
        scratch_shapes=[pltpu.VMEM((3 * C // LANES, L + 2 * CONV_MARGIN, LANES), F32)],
        compiler_params=_cparams("parallel"),
        name="hyena_pre",
    )(u, w, bsh)

    h_fwd, h_bwd = _hyena_filters(L, lp)
    tn = 2 * C
    ha, hb, hc = [jnp.tile(t, (1, tn // C)) for t in filter_spectrum(h_fwd, h_bwd)]
    hspec = pl.BlockSpec((tf, tn), lambda i, j: (i, 0))
    y2 = pl.pallas_call(
        _spectrum_kernel,
        grid=(L // tf, B * C // tn),
        in_specs=[pl.BlockSpec((2 * tf, L), lambda i, j: (i, 0)),
                  pl.BlockSpec((L, tn), lambda i, j: (0, j)), hspec, hspec, hspec],
        out_specs=pl.BlockSpec((2 * tf, tn), lambda i, j: (i, j)),
        out_shape=jax.ShapeDtypeStruct((N, B * C), BF16),
        compiler_params=_cparams("parallel", "parallel"),
        name="hyena_spectrum",
    )(fwd, z2, ha, hb, hc)

    tl = min(256, L)
    yt = pl.pallas_call(
        _idft_kernel,
        grid=(L // tl, B * C // tn),
        in_specs=[pl.BlockSpec((tl, N), lambda i, j: (i, 0)),
                  pl.BlockSpec((N, tn), lambda i, j: (0, j))],
        out_specs=pl.BlockSpec((tl, tn), lambda i, j: (i, j)),
        out_shape=jax.ShapeDtypeStruct((L, B * C), F32),
        compiler_params=_cparams("parallel", "parallel"),
        name="hyena_idft",
    )(inv, y2)

    return pl.pallas_call(
        _hyena_post_kernel,
        grid=(B,),
        in_specs=[pl.BlockSpec((1, L, 3 * C), lambda b: (b, 0, 0)),
                  pl.BlockSpec((L, C), lambda b: (0, b)),
                  pl.BlockSpec((HY_SHORT, 3 * C), lambda b: (0, 0)),
                  pl.BlockSpec((1, 3 * C), lambda b: (0, 0)),
                  pl.BlockSpec((1, C), lambda b: (0, 0))],
        out_specs=pl.BlockSpec((1, L, C), lambda b: (b, 0, 0)),
        out_shape=jax.ShapeDtypeStruct((B, L, C), F32),
        scratch_shapes=[pltpu.VMEM((3 * C // LANES, L + 2 * CONV_MARGIN, LANES), F32)],
        compiler_params=_cparams("parallel"),
        name="hyena_post",
    )(u, yt, w, bsh, lp["hy_bias"].reshape(1, C))


FFT_N2 = 128
FFT_UNROLL = 8


class _FftPlan:
    def __init__(self, L):
        self.L = L
        self.N = 2 * L
        self.N1 = self.N // FFT_N2
        self.KH = self.N1 // 2 + 1
        self.KP = -(-self.KH // 8) * 8
        self.PA = 2 * self.KP + 4


def fft_tables(L):
    p = _FftPlan(L)
    N, N1, KH, KP = p.N, p.N1, p.KH, p.KP
    n2 = jnp.arange(FFT_N2, dtype=jnp.int32)
    k1 = jnp.arange(KP, dtype=jnp.int32)
    n1 = jnp.arange(N1, dtype=jnp.int32)
    n = FFT_N2 * n1[None, None, :] + n2[:, None, None]
    ang = (2.0 * math.pi / N) * ((k1[None, :, None] * n) % N).astype(F32)
    keep = (k1 < KH)[None, :, None]
    g_re = jnp.where(keep, jnp.cos(ang), 0.0)
    g_im = jnp.where(keep, -jnp.sin(ang), 0.0)
    ga_full = jnp.concatenate([g_re, g_im], axis=1)
    ck = jnp.where((k1 == 0) | (k1 == N1 // 2), 1.0, 2.0) / N
    ga_inv = jnp.swapaxes(ga_full[:, :, :N1 // 2] * jnp.tile(ck, 2)[None, :, None], 1, 2)
    kk = jnp.arange(FFT_N2, dtype=jnp.int32)
    ang2 = (2.0 * math.pi / FFT_N2) * ((kk[:, None] * kk[None, :]) % FFT_N2).astype(F32)
    fr, fi = jnp.cos(ang2), -jnp.sin(ang2)
    fb = jnp.block([[fr, -fi], [fi, fr]])
    fb_inv = jnp.block([[fr, fi], [-fi, fr]])
    return dict(ga_half=ga_full[:, :, :N1 // 2].astype(BF16), ga_full=ga_full.astype(BF16),
                ga_inv=ga_inv.astype(BF16), fb=fb.astype(BF16), fb_inv=fb_inv.astype(BF16))


def _fft_stage_a(x_ref, ga_ref, s_ref, plan, n1_count):
    n_slabs = x_ref.shape[0]

    def body(n2, carry):
        xs = jnp.concatenate([x_ref[s, pl.ds(n2, n1_count, stride=FFT_N2), :] for s in range(n_slabs)], axis=-1)
        a = jnp.dot(ga_ref[n2], xs.astype(BF16), preferred_element_type=F32)
        for s in range(n_slabs):
            s_ref[s, pl.ds(n2 * plan.PA, 2 * plan.KP), :] = a[:, s * LANES:(s + 1) * LANES]
        return carry
    lax.fori_loop(0, FFT_N2, body, 0, unroll=FFT_UNROLL)


def _fft_load_k1(s_ref, k1, plan):
    n_slabs = s_ref.shape[0]
    re = jnp.concatenate([s_ref[s, pl.ds(k1, FFT_N2, stride=plan.PA), :] for s in range(n_slabs)], axis=-1)
    im = jnp.concatenate([s_ref[s, pl.ds(plan.KP + k1, FFT_N2, stride=plan.PA), :] for s in range(n_slabs)], axis=-1)
    return jnp.concatenate([re, im], axis=0).astype(BF16)


def _fft_filter_kernel(k_ref, ga_ref, fb_ref, h_ref, s_ref, *, plan):
    _fft_stage_a(k_ref, ga_ref, s_ref, plan, plan.N1)

    def body(k1, carry):
        h_ref[k1] = jnp.dot(fb_ref[...], _fft_load_k1(s_ref, k1, plan), preferred_element_type=F32).astype(BF16)
        return carry
    lax.fori_loop(0, plan.KH, body, 0)


def _fft_conv_kernel(z_ref, ga_ref, gi_ref, fb_ref, fbi_ref, h_ref, y_ref, s_ref, *, plan):
    zs = z_ref.at[0]
    ys = y_ref.at[0]
    n_slabs = zs.shape[0]
    half = FFT_N2
    _fft_stage_a(zs, ga_ref, s_ref, plan, plan.N1 // 2)

    def body_b(k1, carry):
        x = jnp.dot(fb_ref[...], _fft_load_k1(s_ref, k1, plan), preferred_element_type=F32)
        h = h_ref[k1].astype(F32)
        xr, xi, hr, hi = x[:half], x[half:], h[:half], h[half:]
        y = jnp.concatenate([xr * hr - xi * hi, xr * hi + xi * hr], axis=0).astype(BF16)
        b = jnp.dot(fbi_ref[...], y, preferred_element_type=F32)
        for s in range(n_slabs):
            s_ref[s, pl.ds(k1, FFT_N2, stride=plan.PA), :] = b[:half, s * LANES:(s + 1) * LANES]
            s_ref[s, pl.ds(plan.KP + k1, FFT_N2, stride=plan.PA), :] = b[half:, s * LANES:(s + 1) * LANES]
        return carry
    lax.fori_loop(0, plan.KH, body_b, 0, unroll=3)

    def body_a(n2, carry):
        b = jnp.concatenate([s_ref[s, pl.ds(n2 * plan.PA, 2 * plan.KP), :] for s in range(n_slabs)], axis=-1)
        y = jnp.dot(gi_ref[n2], b.astype(BF16), preferred_element_type=F32)
        for s in range(n_slabs):
            ys[s, pl.ds(n2, plan.N1 // 2, stride=FFT_N2), :] = y[:, s * LANES:(s + 1) * LANES]
        return carry
    lax.fori_loop(0, FFT_N2, body_a, 0, unroll=FFT_UNROLL)


def fft_filter_spectrum(h_fwd, h_bwd, tabs):
    L, C = h_fwd.shape
    plan = _FftPlan(L)
    n_slabs = C // LANES
    k = jnp.concatenate([h_fwd, jnp.zeros((1, C), F32), h_bwd[1:][::-1]], axis=0)
    k = k.reshape(plan.N, n_slabs, LANES).transpose(1, 0, 2)
    full = lambda shape: pl.BlockSpec(shape, lambda i: (0,) * len(shape))
    return pl.pallas_call(
        functools.partial(_fft_filter_kernel, plan=plan),
        grid=(1,),
        in_specs=[full((n_slabs, plan.N, LANES)), full((FFT_N2, 2 * plan.KP, plan.N1)),
                  full((2 * FFT_N2, 2 * FFT_N2))],
        out_specs=full((plan.KH, 2 * FFT_N2, C)),
        out_shape=jax.ShapeDtypeStruct((plan.KH, 2 * FFT_N2, C), BF16),
        scratch_shapes=[pltpu.VMEM((n_slabs, FFT_N2 * plan.PA, LANES), F32)],
        compiler_params=_cparams("arbitrary"),
        name="hyena_filter_fft",
    )(k, tabs["ga_full"], tabs["fb"])


def fft_long_conv(z, h_spec, tabs):
    B, n_slabs, L, _ = z.shape
    plan = _FftPlan(L)
    C = n_slabs * LANES
    full = lambda shape: pl.BlockSpec(shape, lambda b: (0,) * len(shape))
    seq = pl.BlockSpec((1, n_slabs, L, LANES), lambda b: (b, 0, 0, 0))
    return pl.pallas_call(
        functools.partial(_fft_conv_kernel, plan=plan),
        grid=(B,),
        in_specs=[seq, full((FFT_N2, 2 * plan.KP, plan.N1 // 2)), full((FFT_N2, plan.N1 // 2, 2 * plan.KP)),
                  full((2 * FFT_N2, 2 * FFT_N2)), full((2 * FFT_N2, 2 * FFT_N2)),
                  full((plan.KH, 2 * FFT_N2, C))],
        out_specs=seq,
        out_shape=jax.ShapeDtypeStruct((B, n_slabs, L, LANES), F32),
        scratch_shapes=[pltpu.VMEM((n_slabs, FFT_N2 * plan.PA, LANES), F32)],
        compiler_params=_cparams("parallel"),
        name="hyena_fft_conv",
    )(z, tabs["ga_half"], tabs["ga_inv"], tabs["fb"], tabs["fb_inv"], h_spec)


def _hyena_pre_slab_kernel(u_ref, w_ref, b_ref, z_ref, upad):
    L = u_ref.shape[1]
    T = _time_chunk(L)
    C = D_GROUP
    _fill_padded(upad, u_ref, L, T)

    def body(j, carry):
        base = pl.multiple_of(j * T, T)
        for s in range(C // LANES):
            x1 = _dw_conv_slab(upad, C // LANES + s, base, T, w_ref, b_ref, C + s * LANES, HY_SHORT, 1)
            v = _dw_conv_slab(upad, 2 * C // LANES + s, base, T, w_ref, b_ref, 2 * C + s * LANES, HY_SHORT, 1)
            z_ref[0, s, pl.ds(base, T), :] = x1 * v
        return carry
    lax.fori_loop(0, L // T, body, 0)


def _hyena_post_slab_kernel(u0_ref, z_ref, y_ref, w_ref, b_ref, bias_ref, o_ref, upad):
    L = u0_ref.shape[1]
    T = _time_chunk(L)
    C = D_GROUP
    _fill_padded(upad, u0_ref, L, T)

    def body(j, carry):
        base = pl.multiple_of(j * T, T)
        x0 = _short_conv(upad, base, T, w_ref, b_ref, 0, C)
        z = jnp.concatenate([z_ref[0, s, pl.ds(base, T), :] for s in range(C // LANES)], axis=-1)
        y = jnp.concatenate([y_ref[0, s, pl.ds(base, T), :] for s in range(C // LANES)], axis=-1)
        o_ref[0, pl.ds(base, T), :] = x0 * (y + z * bias_ref[...])
        return carry
    lax.fori_loop(0, L // T, body, 0)


def hyena_mixer_fft(u, lp, tabs):
    B, L, _ = u.shape
    C = D_GROUP
    n_slabs = C // LANES
    w, bsh = lp["hy_short_w"], lp["hy_short_b"].reshape(1, 3 * C)
    useq = pl.BlockSpec((1, L, 3 * C), lambda b: (b, 0, 0))
    slabs = pl.BlockSpec((1, n_slabs, L, LANES), lambda b: (b, 0, 0, 0))
    wspec = pl.BlockSpec((HY_SHORT, 3 * C), lambda b: (0, 0))
    bspec = pl.BlockSpec((1, 3 * C), lambda b: (0, 0))
    pad_scratch = pltpu.VMEM((3 * C // LANES, L + 2 * CONV_MARGIN, LANES), F32)
    z = pl.pallas_call(
        _hyena_pre_slab_kernel,
        grid=(B,),
        in_specs=[useq, wspec, bspec],
        out_specs=slabs,
        out_shape=jax.ShapeDtypeStruct((B, n_slabs, L, LANES), F32),
        scratch_shapes=[pad_scratch],
        compiler_params=_cparams("parallel"),
        name="hyena_pre",
    )(u, w, bsh)
    h_fwd, h_bwd = _hyena_filters(L, lp)
    y = fft_long_conv(z, fft_filter_spectrum(h_fwd, h_bwd, tabs), tabs)
    return pl.pallas_call(
        _hyena_post_slab_kernel,
        grid=(B,),
        in_specs=[pl.BlockSpec((1, L, C), lambda b: (b, 0, 0)), slabs, slabs, wspec, bspec,
                  pl.BlockSpec((1, C), lambda b: (0, 0))],
        out_specs=pl.BlockSpec((1, L, C), lambda b: (b, 0, 0)),
        out_shape=jax.ShapeDtypeStruct((B, L, C), F32),
        scratch_shapes=[pltpu.VMEM((n_slabs, L + 2 * CONV_MARGIN, LANES), F32)],
        compiler_params=_cparams("parallel"),
        name="hyena_post",
    )(u, z, y, w, bsh, lp["hy_bias"].reshape(1, C))


def _hyena_filters(L, lp):
    t = jnp.linspace(0.0, 1.0, L, dtype=F32)[:, None]
    bands = (HY_EMB - 1) // 2
    w = 2.0 * math.pi * jnp.arange(L, dtype=F32)[:, None] / L
    f = jnp.linspace(1e-4, bands - 1, bands, dtype=F32)[None]
    z = jnp.concatenate([t, jnp.cos(f * w), -jnp.sin(f * w)], axis=-1)
    hdn = jnp.sin(z @ lp["hy_ffn_w1"] + lp["hy_ffn_b1"])
    hdn = jnp.sin(hdn @ lp["hy_ffn_w2"] + lp["hy_ffn_b2"])
    h = (hdn @ lp["hy_ffn_w3"]).reshape(L, 2, D_GROUP)
    max_decay = math.log(HY_TARGET) / HY_FAST_DECAY
    min_decay = math.log(HY_TARGET) / HY_SLOW_DECAY
    deltas = jnp.linspace(min_decay, max_decay, D_GROUP, dtype=F32)
    h = h * jnp.exp(-t * jnp.abs(deltas))[:, None, :]
    h = h / (jnp.sum(jnp.abs(h), axis=(0, 1), keepdims=True) + EPS)
    return h[:, 0], h[:, 1]


def _layer(hc, hx, c_silu_all, lp, need_ctx, final_g, final_norm, tables_x, tables_c, experts, layer):
    B, S, D = hx.shape
    C = hc.shape[1]
    mod = small_linear(c_silu_all, lp["ada_w"], lp["ada_b"])
    mod_x = mod[:B].reshape(B, 6, 1, D)
    mod_c = jnp.broadcast_to(mod[B].reshape(1, 6, 1, D), (B, 6, 1, D))
    w_ext = extend_w_in(lp["w_in"])
    cos_x, sin_x = rope_tables(S, True)
    cos_c, sin_c = rope_tables(C, False)
    hy_x, cf_x, at_x, lr_x = in_proj(hx, mod_x[:, 0], mod_x[:, 1], lp["norm1_g"], w_ext, cos_x, sin_x, tm=512)
    hy_c, cf_c, at_c, lr_c = in_proj(hc, mod_c[:, 0], mod_c[:, 1], lp["norm1_g"], w_ext, cos_c, sin_c, tm=256)

    yd_c, yd_x = rglru_mixer(lr_c, lr_x, lp, need_ctx)
    conf = lambda u: conformer_conv(u, lp["conf_dw_w"], lp["conf_dw_b"], lp["conf_ln_g"], lp["conf_ln_b"])
    ys_x = [hyena_mixer_fft(hy_x, lp, tables_x), conf(cf_x),
            window_attention(at_x, at_c, lp["attn_sink"]), yd_x]

    w_out = lp["w_out"].astype(BF16)
    w_router = jnp.zeros((D, ROUTER_COLS), F32)
    w_router = w_router.at[:, :N_GROUPS].set(lp["router_g_w"]).at[:, N_GROUPS:N_GROUPS + N_EXPERTS].set(lp["router_e_w"])
    w_router = w_router.astype(BF16)
    b_router = jnp.zeros((1, ROUTER_COLS), F32)
    b_router = b_router.at[0, :N_GROUPS].set(lp["router_g_b"]).at[0, N_GROUPS:N_GROUPS + N_EXPERTS].set(lp["router_e_b"])

    hx1, lg_x = out_proj(ys_x, hx, mod_x[:, 2], lp["group_norm_g"], w_out, lp["norm2_g"],
                         mod_x[:, 3], mod_x[:, 4], w_router, b_router, tm=512)
    h_tok = hx1.reshape(B * S, D)
    hc_tok = None
    lg = lg_x.reshape(B * S, ROUTER_COLS)
    if need_ctx:
        ys_c = [hyena_mixer(hy_c, lp, tables_c), conf(cf_c),
                context_attention(at_c, lp["attn_sink"]), yd_c]
        hc1, lg_c = out_proj(ys_c, hc, mod_c[:, 2], lp["group_norm_g"], w_out, lp["norm2_g"],
                             mod_c[:, 3], mod_c[:, 4], w_router, b_router, tm=256)
        hc_tok = hc1.reshape(B * C, D)
        lg = jnp.concatenate([lg, lg_c.reshape(B * C, ROUTER_COLS)], axis=0)

    T = lg.shape[0]
    n_blocks = -(-T // MOE_BLOCK) + N_CLASSES
    info, counts, ids = route_tokens(lg)
    dest, blk_a, blk_b, n_used = slot_plan(ids, counts, n_blocks)
    xs = moe_dispatch(h_tok, hc_tok, dest, n_blocks, lp["norm2_g"], (mod_x[:, 3], mod_x[:, 4]),
                      (mod_c[:, 3], mod_c[:, 4]), S)
    o_sorted = expert_pairs(xs, blk_a, blk_b, n_used, *experts, layer)
    hx2 = moe_collect(o_sorted, dest, info, 0, h_tok, mod_x[:, 5], S, final_g, final_norm)
    hx2 = hx2.reshape(B, S, D)
    if need_ctx:
        hc2 = moe_collect(o_sorted, dest, info, B * S // DISPATCH_TOKENS, hc_tok, mod_c[:1, 5], C,
                          final_g, False).reshape(B, C, D)
    else:
        hc2 = hc
    return hc2, hx2


def kernel(x, c, ctx, c_ctx, norm1_g, norm2_g, ada_w, ada_b, w_in, hy_short_w, hy_short_b, hy_ffn_w1, hy_ffn_b1, hy_ffn_w2, hy_ffn_b2, hy_ffn_w3, hy_bias, conf_dw_w, conf_dw_b, conf_ln_g, conf_ln_b, attn_sink, lru_conv_w, lru_conv_b, lru_wa, lru_ba, lru_wx, lru_bx, lru_lambda, group_norm_g, w_out, router_g_w, router_g_b, router_e_w, router_e_b, exp_w_gate, exp_w_up, exp_w_down, final_norm_g):
    stacked = dict(norm1_g=norm1_g, norm2_g=norm2_g, ada_w=ada_w, ada_b=ada_b, w_in=w_in,
                   hy_short_w=hy_short_w, hy_short_b=hy_short_b, hy_ffn_w1=hy_ffn_w1, hy_ffn_b1=hy_ffn_b1,
                   hy_ffn_w2=hy_ffn_w2, hy_ffn_b2=hy_ffn_b2, hy_ffn_w3=hy_ffn_w3, hy_bias=hy_bias,
                   conf_dw_w=conf_dw_w, conf_dw_b=conf_dw_b, conf_ln_g=conf_ln_g, conf_ln_b=conf_ln_b,
                   attn_sink=attn_sink, lru_conv_w=lru_conv_w, lru_conv_b=lru_conv_b, lru_wa=lru_wa,
                   lru_ba=lru_ba, lru_wx=lru_wx, lru_bx=lru_bx, lru_lambda=lru_lambda,
                   group_norm_g=group_norm_g, w_out=w_out, router_g_w=router_g_w, router_g_b=router_g_b,
                   router_e_w=router_e_w, router_e_b=router_e_b)
    experts = (exp_w_gate.astype(BF16), exp_w_up.astype(BF16), exp_w_down.astype(BF16))
    depth = norm1_g.shape[0]
    B = x.shape[0]
    cs = jnp.concatenate([jax.nn.silu(c), jnp.broadcast_to(jax.nn.silu(c_ctx)[None], (8, c.shape[1]))], axis=0)
    hc, hx = ctx, x
    tables_x = fft_tables(x.shape[1])
    tables_c = dft_tables(ctx.shape[1])
    for l in range(depth):
        lp = {k: v[l] for k, v in stacked.items()}
        hc, hx = _layer(hc, hx, cs, lp, need_ctx=(l < depth - 1), final_g=final_norm_g,
                        final_norm=(l == depth - 1), tables_x=tables_x, tables_c=tables_c,
                        experts=experts, layer=l)
    return hx
```

```python
import functools
import math

import jax
import jax.numpy as jnp
from jax import lax
from jax.experimental import pallas as pl
from jax.experimental.pallas import tpu as pltpu

F32 = jnp.float32
BF16 = jnp.bfloat16

EPS = 1e-6
NEG_INF = -1e30
GRID_W = 64
N_MIXERS = 4
D_GROUP = 256
HY_COLS = 3 * D_GROUP
CONF_COLS = 2 * D_GROUP
ATT_HEADS = 4
ATT_KV_HEADS = 2
HEAD_DIM = 64
ATT_COLS = (ATT_HEADS + 2 * ATT_KV_HEADS) * HEAD_DIM
LRU_COLS = 2 * D_GROUP
QK_COLS = (ATT_HEADS + ATT_KV_HEADS) * HEAD_DIM
WINDOW = 128
ATT_BLOCK = 128
ROPE_BASE = 10000.0
HY_EMB = 33
HY_FAST_DECAY = 0.3
HY_SLOW_DECAY = 1.5
HY_TARGET = 1e-2
CONF_KERNEL = 31
LRU_HEADS = 4
LRU_CONV = 4
LRU_C = 8.0
N_GROUPS = 4
EXP_PER_GROUP = 8
N_EXPERTS = N_GROUPS * EXP_PER_GROUP
TOP_K = 2
MOE_BLOCK = 256
ROUTER_COLS = 128

VMEM_LIMIT_BYTES = 56 * 1024 * 1024


def _cparams(*sem):
    return pltpu.CompilerParams(dimension_semantics=sem, vmem_limit_bytes=VMEM_LIMIT_BYTES)


def _linear_kernel(x_ref, w_ref, b_ref, o_ref):
    o_ref[...] = jnp.dot(x_ref[...], w_ref[...], preferred_element_type=F32,
                         precision=lax.Precision.HIGHEST) + b_ref[...]


def small_linear(x, w, b, tn=1024):
    M, K = x.shape
    N = w.shape[1]
    return pl.pallas_call(
        _linear_kernel,
        grid=(N // tn,),
        in_specs=[pl.BlockSpec((M, K), lambda j: (0, 0)),
                  pl.BlockSpec((K, tn), lambda j: (0, j)),
                  pl.BlockSpec((1, tn), lambda j: (0, j))],
        out_specs=pl.BlockSpec((M, tn), lambda j: (0, j)),
        out_shape=jax.ShapeDtypeStruct((M, N), F32),
        compiler_params=_cparams("parallel"),
        name="ada_linear",
    )(x, w, b.reshape(1, N))


def _in_proj_kernel(x_ref, sh_ref, sc_ref, g_ref, w_ref, cos_ref, sin_ref,
                    hy_ref, cf_ref, at_ref, lr_ref):
    x = x_ref[0]
    ms = jnp.mean(x * x, axis=-1, keepdims=True)
    y = x * lax.rsqrt(ms + EPS) * g_ref[...]
    y = y * (1.0 + sc_ref[0]) + sh_ref[0]
    u = jnp.dot(y.astype(BF16), w_ref[...], preferred_element_type=F32)
    c0 = HY_COLS
    c1 = c0 + CONF_COLS
    c2 = c1 + ATT_COLS
    c3 = c2 + LRU_COLS
    hy_ref[0] = u[:, :c0]
    cf_ref[0] = u[:, c0:c1]
    lr_ref[0] = u[:, c2:c3]
    qk = u[:, c1:c1 + QK_COLS]
    qk_rot = u[:, c3:c3 + QK_COLS]
    at_ref[0, :, :QK_COLS] = qk * cos_ref[...] + qk_rot * sin_ref[...]
    at_ref[0, :, QK_COLS:] = u[:, c1 + QK_COLS:c2]


def in_proj(h, shift, scale, g, w_ext, cos_t, sin_t, tm):
    B, L, D = h.shape
    NW = w_ext.shape[1]
    outs = [HY_COLS, CONF_COLS, ATT_COLS, LRU_COLS]
    return pl.pallas_call(
        _in_proj_kernel,
        grid=(B, L // tm),
        in_specs=[pl.BlockSpec((1, tm, D), lambda b, i: (b, i, 0)),
                  pl.BlockSpec((1, 1, D), lambda b, i: (b, 0, 0)),
                  pl.BlockSpec((1, 1, D), lambda b, i: (b, 0, 0)),
                  pl.BlockSpec((1, D), lambda b, i: (0, 0)),
                  pl.BlockSpec((D, NW), lambda b, i: (0, 0)),
                  pl.BlockSpec((tm, QK_COLS), lambda b, i: (i, 0)),
                  pl.BlockSpec((tm, QK_COLS), lambda b, i: (i, 0))],
        out_specs=[pl.BlockSpec((1, tm, n), lambda b, i: (b, i, 0)) for n in outs],
        out_shape=[jax.ShapeDtypeStruct((B, L, n), F32) for n in outs],
        compiler_params=_cparams("parallel", "parallel"),
        name="in_proj",
    )(h, shift, scale, g.reshape(1, D), w_ext, cos_t, sin_t)


def rope_tables(L, rotary):
    n_heads = ATT_HEADS + ATT_KV_HEADS
    if not rotary:
        return jnp.ones((L, QK_COLS), F32), jnp.zeros((L, QK_COLS), F32)
    pos = jnp.arange(L)
    row = (pos // GRID_W).astype(F32)
    col = (pos % GRID_W).astype(F32)
    half = HEAD_DIM // 2
    inv_freq = ROPE_BASE ** (-jnp.arange(0, half, 2, dtype=F32) / half)
    ang_r = row[:, None] * inv_freq[None]
    ang_c = col[:, None] * inv_freq[None]
    cos_h = jnp.concatenate([jnp.cos(ang_r)] * 2 + [jnp.cos(ang_c)] * 2, axis=-1)
    sin_h = jnp.concatenate([jnp.sin(ang_r)] * 2 + [jnp.sin(ang_c)] * 2, axis=-1)
    return jnp.tile(cos_h, (1, n_heads)), jnp.tile(sin_h, (1, n_heads))


def extend_w_in(w_in):
    c1 = HY_COLS + CONF_COLS
    wqk = w_in[:, c1:c1 + QK_COLS]
    D = w_in.shape[0]
    w4 = wqk.reshape(D, QK_COLS // 32, 2, 16)
    wrot = jnp.stack([-w4[:, :, 1], w4[:, :, 0]], axis=2).reshape(D, QK_COLS)
    return jnp.concatenate([w_in, wrot], axis=1).astype(BF16)


def _softmax_parts(q, k_list, extra_logit):
    scale = HEAD_DIM ** -0.5
    s_list = []
    for k, mask in k_list:
        s = lax.dot_general(q, k, (((1,), (1,)), ((), ())), preferred_element_type=F32) * scale
        if mask is not None:
            s = jnp.where(mask, s, NEG_INF)
        s_list.append(s)
    m = extra_logit
    for s in s_list:
        m = jnp.maximum(m, jnp.max(s, axis=-1, keepdims=True))
    p_list = [jnp.exp(s - m) for s in s_list]
    denom = jnp.exp(extra_logit - m)
    for p in p_list:
        denom = denom + jnp.sum(p, axis=-1, keepdims=True)
    return p_list, 1.0 / denom


ATT_Q_BLOCKS = 16


def _win_attn_kernel(sink_ref, q_ref, kp_ref, kc_ref, kn_ref, vp_ref, vc_ref, vn_ref,
                     kx_ref, vx_ref, o_ref, *, seq_len):
    i = pl.program_id(1)
    blk = ATT_BLOCK
    qb = q_ref.shape[1] // blk
    scale = HEAD_DIM ** -0.5
    g = ATT_HEADS // ATT_KV_HEADS
    kw = jnp.concatenate([kp_ref[0], kc_ref[0], kn_ref[0]], axis=0)
    vw = jnp.concatenate([vp_ref[0], vc_ref[0], vn_ref[0]], axis=0).astype(BF16)
    kwt = kw.T.astype(BF16)
    kxt = kx_ref[0].T.astype(BF16)
    vx = vx_ref[0].astype(BF16)
    row = lax.broadcasted_iota(jnp.int32, (g * blk, 3 * blk), 0) % blk
    col = lax.broadcasted_iota(jnp.int32, (g * blk, 3 * blk), 1)
    band_bias = jnp.where(jnp.abs(col - blk - row) <= WINDOW, 0.0, NEG_INF)
    col1 = lax.broadcasted_iota(jnp.int32, (1, 3 * blk), 1)
    for j in range(qb):
        q_blk = i * qb + j
        k_pos = (q_blk - 1) * blk + col1
        edge_bias = jnp.where(k_pos >= 0, jnp.where(k_pos < seq_len, 0.0, NEG_INF), NEG_INF)
        bias = band_bias + edge_bias
        outs = []
        for kv in range(ATT_KV_HEADS):
            ksl = slice(kv * HEAD_DIM, (kv + 1) * HEAD_DIM)
            heads = range(kv * g, (kv + 1) * g)
            qs = (jnp.concatenate([q_ref[0, j * blk:(j + 1) * blk, h * HEAD_DIM:(h + 1) * HEAD_DIM]
                                   for h in heads], axis=0) * scale).astype(BF16)
            sink = jnp.concatenate([jnp.full((blk, 1), sink_ref[h], F32) for h in heads], axis=0)
            s_win = jnp.dot(qs, kwt[ksl, j * blk:(j + 3) * blk], preferred_element_type=F32) + bias
            s_ctx = jnp.dot(qs, kxt[ksl, :], preferred_element_type=F32)
            m = jnp.maximum(jnp.maximum(jnp.max(s_win, axis=-1, keepdims=True),
                                        jnp.max(s_ctx, axis=-1, keepdims=True)), sink)
            p_win = jnp.exp(s_win - m)
            p_ctx = jnp.exp(s_ctx - m)
            denom = (jnp.exp(sink - m) + jnp.sum(p_win, axis=-1, keepdims=True)
                     + jnp.sum(p_ctx, axis=-1, keepdims=True))
            o = (jnp.dot(p_win.astype(BF16), vw[j * blk:(j + 3) * blk, ksl], preferred_element_type=F32)
                 + jnp.dot(p_ctx.astype(BF16), vx[:, ksl], preferred_element_type=F32)) * (1.0 / denom)
            outs.extend([o[k * blk:(k + 1) * blk] for k in range(g)])
        o_ref[0, j * blk:(j + 1) * blk, :] = jnp.concatenate(outs, axis=-1)


def window_attention(at_x, at_c, sink):
    B, S, _ = at_x.shape
    C = at_c.shape[1]
    blk = ATT_BLOCK
    qb = ATT_Q_BLOCKS
    nb = S // blk
    kcol = QK_COLS // 128 - 1
    vcol = kcol + 1

    def edge_spec(col, off):
        return pl.BlockSpec((1, blk, 128), lambda b, i, s: (b, jnp.clip(i * qb + off, 0, nb - 1), col))

    def mid_spec(col):
        return pl.BlockSpec((1, qb * blk, 128), lambda b, i, s: (b, i, col))

    grid_spec = pltpu.PrefetchScalarGridSpec(
        num_scalar_prefetch=1,
        grid=(B, nb // qb),
        in_specs=[pl.BlockSpec((1, qb * blk, ATT_HEADS * HEAD_DIM), lambda b, i, s: (b, i, 0)),
                  edge_spec(kcol, -1), mid_spec(kcol), edge_spec(kcol, qb),
                  edge_spec(vcol, -1), mid_spec(vcol), edge_spec(vcol, qb),
                  pl.BlockSpec((1, C, 128), lambda b, i, s: (b, 0, kcol)),
                  pl.BlockSpec((1, C, 128), lambda b, i, s: (b, 0, vcol))],
        out_specs=pl.BlockSpec((1, qb * blk, ATT_HEADS * HEAD_DIM), lambda b, i, s: (b, i, 0)),
    )
    return pl.pallas_call(
        functools.partial(_win_attn_kernel, seq_len=S),
        grid_spec=grid_spec,
        out_shape=jax.ShapeDtypeStruct((B, S, ATT_HEADS * HEAD_DIM), F32),
        compiler_params=_cparams("parallel", "parallel"),
        name="window_attention",
    )(sink.astype(F32), at_x, at_x, at_x, at_x, at_x, at_x, at_x, at_c, at_c)


def _ctx_attn_kernel(sink_ref, q_ref, kx_ref, vx_ref, o_ref):
    q = q_ref[0].astype(BF16)
    kx = kx_ref[0].astype(BF16)
    vx = vx_ref[0].astype(BF16)
    g = ATT_HEADS // ATT_KV_HEADS
    outs = []
    for h in range(ATT_HEADS):
        kv = h // g
        qs = q[:, h * HEAD_DIM:(h + 1) * HEAD_DIM]
        ksl = slice(kv * HEAD_DIM, (kv + 1) * HEAD_DIM)
        (p_ctx,), inv = _softmax_parts(qs, [(kx[:, ksl], None)], sink_ref[h])
        outs.append(jnp.dot(p_ctx.astype(BF16), vx[:, ksl], preferred_element_type=F32) * inv)
    o_ref[0] = jnp.concatenate(outs, axis=-1)


def context_attention(at_c, sink):
    B, C, _ = at_c.shape
    kcol = QK_COLS // 128 - 1
    grid_spec = pltpu.PrefetchScalarGridSpec(
        num_scalar_prefetch=1,
        grid=(B,),
        in_specs=[pl.BlockSpec((1, C, ATT_HEADS * HEAD_DIM), lambda b, s: (b, 0, 0)),
                  pl.BlockSpec((1, C, 128), lambda b, s: (b, 0, kcol)),
                  pl.BlockSpec((1, C, 128), lambda b, s: (b, 0, kcol + 1))],
        out_specs=pl.BlockSpec((1, C, ATT_HEADS * HEAD_DIM), lambda b, s: (b, 0, 0)),
    )
    return pl.pallas_call(
        _ctx_attn_kernel,
        grid_spec=grid_spec,
        out_shape=jax.ShapeDtypeStruct((B, C, ATT_HEADS * HEAD_DIM), F32),
        compiler_params=_cparams("parallel"),
        name="context_attention",
    )(sink.astype(F32), at_c, at_c, at_c)


def _out_proj_kernel(y0_ref, y1_ref, y2_ref, y3_ref, h_ref, g1_ref, gng_ref, w_ref,
                     n2g_ref, sh_ref, sc_ref, wr_ref, br_ref, ho_ref, lg_ref):
    parts = []
    for k, y_ref in enumerate((y0_ref, y1_ref, y2_ref, y3_ref)):
        y = y_ref[0]
        ms = jnp.mean(y * y, axis=-1, keepdims=True)
        yn = y * lax.rsqrt(ms + EPS) * gng_ref[:, k * D_GROUP:(k + 1) * D_GROUP]
        parts.append(yn.astype(BF16))
    yn = jnp.concatenate(parts, axis=-1)
    proj = jnp.dot(yn, w_ref[...], preferred_element_type=F32)
    h = h_ref[0] + g1_ref[0] * proj
    ho_ref[0] = h
    ms = jnp.mean(h * h, axis=-1, keepdims=True)
    n = h * lax.rsqrt(ms + EPS) * n2g_ref[...]
    n = n * (1.0 + sc_ref[0]) + sh_ref[0]
    lg_ref[0] = jnp.dot(n.astype(BF16), wr_ref[...], preferred_element_type=F32) + br_ref[...]


def out_proj(ys, h, g1, gng, w_out, n2g, sh2, sc2, w_router, b_router, tm):
    B, L, D = h.shape
    row3 = lambda n: pl.BlockSpec((1, tm, n), lambda b, i: (b, i, 0))
    mod = pl.BlockSpec((1, 1, D), lambda b, i: (b, 0, 0))
    full = lambda r, c: pl.BlockSpec((r, c), lambda b, i: (0, 0))
    return pl.pallas_call(
        _out_proj_kernel,
        grid=(B, L // tm),
        in_specs=[row3(D_GROUP)] * 4 + [row3(D), mod, full(1, D), full(D, D), full(1, D), mod, mod,
                                        full(D, ROUTER_COLS), full(1, ROUTER_COLS)],
        out_specs=[row3(D), row3(ROUTER_COLS)],
        out_shape=[jax.ShapeDtypeStruct((B, L, D), F32), jax.ShapeDtypeStruct((B, L, ROUTER_COLS), F32)],
        compiler_params=_cparams("parallel", "parallel"),
        name="out_proj",
    )(*ys, h, g1, gng.reshape(1, D), w_out, n2g.reshape(1, D), sh2, sc2, w_router, b_router)


N_PAIRS = EXP_PER_GROUP * (EXP_PER_GROUP - 1) // 2
N_CLASSES = N_GROUPS * N_PAIRS
ROUTE_TOKENS = 512
INFO_CLASS, INFO_RANK, INFO_WA, INFO_WB = 0, 1, 2, 3


SUBLANES = 8


def _route_kernel(lg_ref, below_ref, info_ref, cnt_ref, ids_ref, run):
    i = pl.program_id(0)

    @pl.when(i == 0)
    def _():
        run[...] = jnp.zeros_like(run)

    lg = lg_ref[...]
    li = lax.broadcasted_iota(jnp.int32, lg.shape, 1).astype(F32)
    big = float(ROUTER_COLS)

    def first_argmax(vals):
        m = jnp.max(vals, axis=-1, keepdims=True)
        return m, jnp.min(jnp.where(vals == m, li, big), axis=-1, keepdims=True)

    gl = jnp.where(li < N_GROUPS, lg, NEG_INF)
    gmax, g_idx = first_argmax(gl)
    g_prob = 1.0 / jnp.sum(jnp.exp(gl - gmax), axis=-1, keepdims=True)
    lo = N_GROUPS + EXP_PER_GROUP * g_idx
    el = jnp.where(li >= lo, jnp.where(li < lo + EXP_PER_GROUP, lg, NEG_INF), NEG_INF)
    m1, i1 = first_argmax(el)
    m2, i2 = first_argmax(jnp.where(li == i1, NEG_INF, el))
    e2 = jnp.exp(m2 - m1)
    w1 = g_prob / (1.0 + e2)
    w2 = g_prob * e2 / (1.0 + e2)
    j1 = i1 - lo
    j2 = i2 - lo
    a = jnp.minimum(j1, j2)
    b = jnp.maximum(j1, j2)
    cls = g_idx * N_PAIRS + (a * (2 * EXP_PER_GROUP - 1 - a)) * 0.5 + (b - a - 1.0)
    w_a = jnp.where(j1 < j2, w1, w2)
    w_b = jnp.where(j1 < j2, w2, w1)

    hit = li == cls
    onehot = jnp.where(hit, 1.0, 0.0)
    before = jnp.dot(below_ref[...], onehot.astype(BF16), preferred_element_type=F32)
    rank = jnp.sum(jnp.where(hit, before + run[...], 0.0), axis=-1, keepdims=True)
    run[...] = run[...] + jnp.sum(onehot, axis=0, keepdims=True)
    cnt_ref[...] = run[...]
    info = jnp.where(li == INFO_CLASS, cls, 0.0)
    info = jnp.where(li == INFO_RANK, rank, info)
    info = jnp.where(li == INFO_WA, w_a, info)
    info = jnp.where(li == INFO_WB, w_b, info)
    info_ref[...] = info
    ids_ref[0] = info.T[:SUBLANES].astype(jnp.int32)


def route_tokens(logits):
    T = logits.shape[0]
    tb = ROUTE_TOKENS
    below = (jnp.arange(tb)[None, :] < jnp.arange(tb)[:, None]).astype(BF16)
    return pl.pallas_call(
        _route_kernel,
        grid=(T // tb,),
        in_specs=[pl.BlockSpec((tb, ROUTER_COLS), lambda i: (i, 0)),
                  pl.BlockSpec((tb, tb), lambda i: (0, 0))],
        out_specs=[pl.BlockSpec((tb, ROUTER_COLS), lambda i: (i, 0)),
                   pl.BlockSpec((1, ROUTER_COLS), lambda i: (0, 0)),
                   pl.BlockSpec((1, SUBLANES, tb), lambda i: (i, 0, 0))],
        out_shape=[jax.ShapeDtypeStruct((T, ROUTER_COLS), F32), jax.ShapeDtypeStruct((1, ROUTER_COLS), F32),
                   jax.ShapeDtypeStruct((T // tb, SUBLANES, tb), jnp.int32)],
        scratch_shapes=[pltpu.VMEM((1, ROUTER_COLS), F32)],
        compiler_params=_cparams("arbitrary"),
        name="moe_route",
    )(logits, below)


def _pair_tables():
    a_tab, b_tab = [], []
    for g in range(N_GROUPS):
        for a in range(EXP_PER_GROUP):
            for b in range(a + 1, EXP_PER_GROUP):
                a_tab.append(g * EXP_PER_GROUP + a)
                b_tab.append(g * EXP_PER_GROUP + b)
    return jnp.array(a_tab, jnp.int32), jnp.array(b_tab, jnp.int32)


def _slot_kernel(ids_ref, start_ref, dest_ref):
    cls = ids_ref[0, INFO_CLASS:INFO_CLASS + 1, :]
    rank = ids_ref[0, INFO_RANK:INFO_RANK + 1, :]
    ci = lax.broadcasted_iota(jnp.int32, (ROUTER_COLS, cls.shape[1]), 0)
    start = jnp.sum(jnp.where(ci == cls, start_ref[...], 0), axis=0, keepdims=True)
    dest_ref[0] = jnp.broadcast_to(start + rank, dest_ref.shape[1:])


def slot_plan(ids, counts, n_blocks):
    nt, _, tb = ids.shape
    cnt = counts[0, :N_CLASSES].astype(jnp.int32)
    padded = (cnt + MOE_BLOCK - 1) // MOE_BLOCK * MOE_BLOCK
    upto = jnp.arange(N_CLASSES)[None, :] <= jnp.arange(N_CLASSES)[:, None]
    pad_end = jnp.sum(jnp.where(upto, padded[None, :], 0), axis=1)
    class_start = jnp.zeros((ROUTER_COLS, 1), jnp.int32).at[:N_CLASSES, 0].set(pad_end - padded)
    dest = pl.pallas_call(
        _slot_kernel,
        grid=(nt,),
        in_specs=[pl.BlockSpec((1, SUBLANES, tb), lambda i: (i, 0, 0)),
                  pl.BlockSpec((ROUTER_COLS, 1), lambda i: (0, 0))],
        out_specs=pl.BlockSpec((1, SUBLANES, tb), lambda i: (i, 0, 0)),
        out_shape=jax.ShapeDtypeStruct((nt, SUBLANES, tb), jnp.int32),
        compiler_params=_cparams("parallel"),
        name="moe_slots",
    )(ids, class_start)[:, 0, :].reshape(nt * tb)
    n_used = (pad_end[-1] // MOE_BLOCK).astype(jnp.int32).reshape(1)
    blk_first = jnp.arange(n_blocks, dtype=jnp.int32) * MOE_BLOCK
    blk_cls = jnp.minimum(jnp.sum((pad_end[None, :] <= blk_first[:, None]).astype(jnp.int32), axis=1),
                          N_CLASSES - 1)
    a_tab, b_tab = _pair_tables()
    hit = blk_cls[:, None] == jnp.arange(N_CLASSES)[None, :]
    pick = lambda tab: jnp.sum(jnp.where(hit, tab[None, :], 0), axis=1).astype(jnp.int32)
    return dest, pick(a_tab), pick(b_tab), n_used


DISPATCH_TOKENS = 1024
COLLECT_TOKENS = 512


def _wait_rows(buf, sem):
    pltpu.make_async_copy(buf, buf, sem).wait()


DMA_UNROLL = 8
TOKEN_TILE_ROWS = 8


def _store_token_tiles(tiles_ref, offset, pitch, x):
    n = x.shape[0]
    for j in range(x.shape[1] // LANES):
        tiles_ref[pl.ds(offset + j, n, stride=pitch), :] = x[:, j * LANES:(j + 1) * LANES]


def _load_token_tiles(tiles_ref, offset, pitch, n, width):
    return jnp.concatenate([tiles_ref[pl.ds(offset + j, n, stride=pitch), :] for j in range(width // LANES)],
                           axis=-1)


def _dispatch_kernel(dest_ref, hx_ref, hc_ref, g_ref, shx_ref, scx_ref, shc_ref, scc_ref, zeros_hbm,
                     xs_hbm, rows, sems, *, n_latent_blocks):
    del zeros_hbm
    i = pl.program_id(0)
    n = pl.num_programs(0)
    slot = i % 2
    tb = hx_ref.shape[0]

    @pl.when(i >= 2)
    def _():
        _wait_rows(rows.at[slot], sems.at[slot])

    def normed(h_ref, sh_ref, sc_ref):
        h = h_ref[...]
        ms = jnp.mean(h * h, axis=-1, keepdims=True)
        return h * lax.rsqrt(ms + EPS) * g_ref[...] * (1.0 + sc_ref[0]) + sh_ref[0]

    @pl.when(i < n_latent_blocks)
    def _():
        _store_token_tiles(rows.at[slot], 0, TOKEN_TILE_ROWS, normed(hx_ref, shx_ref, scx_ref))

    @pl.when(i >= n_latent_blocks)
    def _():
        _store_token_tiles(rows.at[slot], 0, TOKEN_TILE_ROWS, normed(hc_ref, shc_ref, scc_ref))

    def body(g, carry):
        for u in range(DMA_UNROLL):
            r = g * DMA_UNROLL + u
            dst = pl.multiple_of(dest_ref[0, 0, r] * TOKEN_TILE_ROWS, TOKEN_TILE_ROWS)
            pltpu.make_async_copy(rows.at[slot, pl.ds(r * TOKEN_TILE_ROWS, TOKEN_TILE_ROWS)],
                                  xs_hbm.at[pl.ds(dst, TOKEN_TILE_ROWS)], sems.at[slot]).start(priority=u % 2)
        return carry
    lax.fori_loop(0, tb // DMA_UNROLL, body, 0)

    @pl.when(i == n - 1)
    def _():
        _wait_rows(rows.at[slot], sems.at[slot])

        @pl.when(n >= 2)
        def _():
            _wait_rows(rows.at[1 - slot], sems.at[1 - slot])


def moe_dispatch(h_x, h_c, dest, n_blocks, n2g, mod_x, mod_c, tokens_per_batch):
    Tx, D = h_x.shape
    tb = DISPATCH_TOKENS
    nxb = Tx // tb
    if h_c is None:
        h_c, mod_c, ncb = h_x, mod_x, 0
    else:
        ncb = h_c.shape[0] // tb
    per_b = tokens_per_batch // tb
    P = n_blocks * MOE_BLOCK
    tile_rows = D // LANES
    assert tile_rows == TOKEN_TILE_ROWS
    xi = lambda i: jnp.minimum(i, nxb - 1)
    ci = lambda i: jnp.maximum(i - nxb, 0)
    modx = pl.BlockSpec((1, 1, D), lambda i: (xi(i) // per_b, 0, 0))
    modc = pl.BlockSpec((1, 1, D), lambda i: (0, 0, 0))
    return pl.pallas_call(
        functools.partial(_dispatch_kernel, n_latent_blocks=nxb),
        grid=(nxb + ncb,),
        in_specs=[pl.BlockSpec((1, 1, tb), lambda i: (i, 0, 0), memory_space=pltpu.SMEM),
                  pl.BlockSpec((tb, D), lambda i: (xi(i), 0)),
                  pl.BlockSpec((tb, D), lambda i: (ci(i), 0)),
                  pl.BlockSpec((1, D), lambda i: (0, 0)),
                  modx, modx, modc, modc,
                  pl.BlockSpec(memory_space=pl.ANY)],
        out_specs=pl.BlockSpec(memory_space=pl.ANY),
        out_shape=jax.ShapeDtypeStruct((P * tile_rows, LANES), F32),
        scratch_shapes=[pltpu.VMEM((2, tb * tile_rows, LANES), F32), pltpu.SemaphoreType.DMA((2,))],
        input_output_aliases={8: 0},
        compiler_params=_cparams("arbitrary"),
        name="moe_dispatch",
    )(dest.reshape(-1, 1, tb), h_x, h_c, n2g.reshape(1, D), mod_x[0], mod_x[1], mod_c[0], mod_c[1],
      jnp.zeros((P * tile_rows, LANES), F32))


def _expert_pair_kernel(ea_ref, eb_ref, nused_ref, xs_ref, wga_ref, wua_ref, wda_ref, wgb_ref, wub_ref, wdb_ref,
                        o_ref):
    del ea_ref, eb_ref
    i = pl.program_id(0)
    D = wga_ref.shape[2]

    @pl.when(i < nused_ref[0])
    def _():
        xb = _load_token_tiles(xs_ref, 0, TOKEN_TILE_ROWS, MOE_BLOCK, D).astype(BF16)
        halves = []
        for wg_ref, wu_ref, wd_ref in ((wga_ref, wua_ref, wda_ref), (wgb_ref, wub_ref, wdb_ref)):
            gate = jnp.dot(xb, wg_ref[0, 0], preferred_element_type=F32)
            up = jnp.dot(xb, wu_ref[0, 0], preferred_element_type=F32)
            hid = (gate * jax.nn.sigmoid(gate) * up).astype(BF16)
            out = jnp.dot(hid, wd_ref[0, 0], preferred_element_type=F32)
            halves.append(lax.bitcast_convert_type(out.astype(BF16).astype(F32), jnp.uint32))
        _store_token_tiles(o_ref, 0, TOKEN_TILE_ROWS, halves[0] | (halves[1] >> 16))

    @pl.when(i >= nused_ref[0])
    def _():
        o_ref[...] = jnp.zeros_like(o_ref)


def _unpack_pair(words):
    hi = lax.bitcast_convert_type(words & jnp.uint32(0xFFFF0000), F32)
    lo = lax.bitcast_convert_type(words << 16, F32)
    return hi, lo


def expert_pairs(xs, blk_a, blk_b, n_used, w_gate, w_up, w_down, layer):
    D, DE = w_gate.shape[2:]
    P = xs.shape[0] // TOKEN_TILE_ROWS
    n_blocks = P // MOE_BLOCK
    wspec = lambda shape, which: pl.BlockSpec(shape, lambda i, ea, eb, nu: (layer, (ea, eb)[which][i], 0, 0))
    grid_spec = pltpu.PrefetchScalarGridSpec(
        num_scalar_prefetch=3,
        grid=(n_blocks,),
        in_specs=[pl.BlockSpec((MOE_BLOCK * TOKEN_TILE_ROWS, LANES), lambda i, ea, eb, nu: (i, 0)),
                  wspec((1, 1, D, DE), 0), wspec((1, 1, D, DE), 0), wspec((1, 1, DE, D), 0),
                  wspec((1, 1, D, DE), 1), wspec((1, 1, D, DE), 1), wspec((1, 1, DE, D), 1)],
        out_specs=pl.BlockSpec((MOE_BLOCK * TOKEN_TILE_ROWS, LANES), lambda i, ea, eb, nu: (i, 0)),
    )
    return pl.pallas_call(
        _expert_pair_kernel,
        grid_spec=grid_spec,
        out_shape=jax.ShapeDtypeStruct((P * TOKEN_TILE_ROWS, LANES), jnp.uint32),
        compiler_params=_cparams("arbitrary"),
        name="moe_experts",
    )(blk_a, blk_b, n_used, xs, w_gate, w_up, w_down, w_gate, w_up, w_down)


def _gather_pairs(idx_ref, src_hbm, buf, sem, n_tokens):
    def body(g, carry):
        for u in range(DMA_UNROLL):
            r = g * DMA_UNROLL + u
            src = pl.multiple_of(idx_ref[0, 0, r] * TOKEN_TILE_ROWS, TOKEN_TILE_ROWS)
            pltpu.make_async_copy(src_hbm.at[pl.ds(src, TOKEN_TILE_ROWS)],
                                  buf.at[pl.ds(r * TOKEN_TILE_ROWS, TOKEN_TILE_ROWS)], sem).start(priority=u % 2)
        return carry
    lax.fori_loop(0, n_tokens // DMA_UNROLL, body, 0)


def _collect_kernel(dest_ref, dest_next_ref, o_hbm, info_ref, h_ref, g2_ref, fg_ref, out_ref, obuf, sems, *,
                    final_norm):
    i = pl.program_id(0)
    n = pl.num_programs(0)
    slot = i % 2
    tb, D = h_ref.shape

    @pl.when(i == 0)
    def _():
        _gather_pairs(dest_ref, o_hbm, obuf.at[0], sems.at[0], tb)

    @pl.when(i + 1 < n)
    def _():
        _gather_pairs(dest_next_ref, o_hbm, obuf.at[1 - slot], sems.at[1 - slot], tb)

    _wait_rows(obuf.at[slot], sems.at[slot])
    e_a, e_b = _unpack_pair(_load_token_tiles(obuf.at[slot], 0, TOKEN_TILE_ROWS, tb, D))
    m = info_ref[:, INFO_WA:INFO_WA + 1] * e_a + info_ref[:, INFO_WB:INFO_WB + 1] * e_b
    h = h_ref[...] + g2_ref[0] * m
    if final_norm:
        ms = jnp.mean(h * h, axis=-1, keepdims=True)
        h = h * lax.rsqrt(ms + EPS) * fg_ref[...]
    out_ref[...] = h


def moe_collect(o_sorted, dest, info, block_offset, h_tokens, g2, tokens_per_batch, final_g, final_norm):
    T, D = h_tokens.shape
    tb = COLLECT_TOKENS
    nt = T // tb
    if g2.shape[0] == 1:
        g2_index = lambda i: 0
    else:
        assert tokens_per_batch % tb == 0
        g2_index = lambda i: i // (tokens_per_batch // tb)
    last = block_offset + nt - 1
    dest3 = dest.reshape(-1, 1, tb)
    return pl.pallas_call(
        functools.partial(_collect_kernel, final_norm=final_norm),
        grid=(nt,),
        in_specs=[pl.BlockSpec((1, 1, tb), lambda i: (block_offset + i, 0, 0), memory_space=pltpu.SMEM),
                  pl.BlockSpec((1, 1, tb), lambda i: (jnp.minimum(block_offset + i + 1, last), 0, 0),
                               memory_space=pltpu.SMEM),
                  pl.BlockSpec(memory_space=pl.ANY),
                  pl.BlockSpec((tb, ROUTER_COLS), lambda i: (block_offset + i, 0)),
                  pl.BlockSpec((tb, D), lambda i: (i, 0)),
                  pl.BlockSpec((1, 1, D), lambda i: (g2_index(i), 0, 0)),
                  pl.BlockSpec((1, D), lambda i: (0, 0))],
        out_specs=pl.BlockSpec((tb, D), lambda i: (i, 0)),
        out_shape=jax.ShapeDtypeStruct((T, D), F32),
        scratch_shapes=[pltpu.VMEM((2, tb * TOKEN_TILE_ROWS, LANES), jnp.uint32), pltpu.SemaphoreType.DMA((2,))],
        compiler_params=_cparams("arbitrary"),
        name="moe_collect",
    )(dest3, dest3, o_sorted, info, h_tokens, g2, final_g.reshape(1, D))


CONV_MARGIN = 16


def _time_chunk(L):
    return min(L, 256)


LANES = 128


def _zero_margins(pad_ref, L):
    zeros = jnp.zeros((CONV_MARGIN, LANES), F32)
    for s in range(pad_ref.shape[0]):
        pad_ref[s, pl.ds(0, CONV_MARGIN), :] = zeros
        pad_ref[s, pl.ds(CONV_MARGIN + L, CONV_MARGIN), :] = zeros


def _dw_conv_slab(pad_ref, s, base, T, w_ref, b_ref, col, taps, pad_left):
    acc = jnp.broadcast_to(b_ref[:, col:col + LANES], (T, LANES))
    for k in range(taps):
        acc = acc + w_ref[k:k + 1, col:col + LANES] * pad_ref[s, pl.ds(base + (CONV_MARGIN - pad_left + k), T), :]
    return acc


def _conformer_kernel(u_ref, w_ref, b_ref, g_ref, beta_ref, o_ref, ypad):
    L = o_ref.shape[1]
    T = _time_chunk(L)
    C = D_GROUP
    n_slabs = C // LANES
    pad = (CONF_KERNEL - 1) // 2
    _zero_margins(ypad, L)

    def glu(j, carry):
        base = pl.multiple_of(j * T, T)
        for s in range(n_slabs):
            a = u_ref[0, pl.ds(base, T), s * LANES:(s + 1) * LANES]
            gate = u_ref[0, pl.ds(base, T), C + s * LANES:C + (s + 1) * LANES]
            ypad[s, pl.ds(CONV_MARGIN + base, T), :] = a * jax.nn.sigmoid(gate)
        return carry
    lax.fori_loop(0, L // T, glu, 0)

    def conv(j, carry):
        base = pl.multiple_of(j * T, T)
        acc = jnp.concatenate([_dw_conv_slab(ypad, s, base, T, w_ref, b_ref, s * LANES, CONF_KERNEL, pad)
                               for s in range(n_slabs)], axis=-1)
        mu = jnp.mean(acc, axis=-1, keepdims=True)
        cen = acc - mu
        var = jnp.mean(cen * cen, axis=-1, keepdims=True)
        y = cen * lax.rsqrt(var + EPS) * g_ref[...] + beta_ref[...]
        o_ref[0, pl.ds(base, T), :] = y * jax.nn.sigmoid(y)
        return carry
    lax.fori_loop(0, L // T, conv, 0)


def conformer_conv(u, w, b, ln_g, ln_b):
    B, L, _ = u.shape
    C = D_GROUP
    vec = pl.BlockSpec((1, C), lambda i: (0, 0))
    return pl.pallas_call(
        _conformer_kernel,
        grid=(B,),
        in_specs=[pl.BlockSpec((1, L, 2 * C), lambda i: (i, 0, 0)),
                  pl.BlockSpec((CONF_KERNEL, C), lambda i: (0, 0)), vec, vec, vec],
        out_specs=pl.BlockSpec((1, L, C), lambda i: (i, 0, 0)),
        out_shape=jax.ShapeDtypeStruct((B, L, C), F32),
        scratch_shapes=[pltpu.VMEM((C // LANES, L + 2 * CONV_MARGIN, LANES), F32)],
        compiler_params=_cparams("parallel"),
        name="conformer_conv",
    )(u, w, b.reshape(1, C), ln_g.reshape(1, C), ln_b.reshape(1, C))


def _gelu_tanh(x):
    return 0.5 * x * (1.0 + jnp.tanh(math.sqrt(2.0 / math.pi) * (x + 0.044715 * (x * x * x))))


def _lru_kernel(uc_ref, ux_ref, cw_ref, cb_ref, wcat_ref, bcat_ref, lam_ref, *rest, need_ctx):
    if need_ctx:
        oc_ref, ox_ref, cpad, xpad, a_s, b_s, yx, yc = rest
    else:
        ox_ref, cpad, xpad, a_s, b_s, yx = rest
        oc_ref = yc = None
    C = D_GROUP
    n_slabs = C // LANES
    Lc = uc_ref.shape[1]
    Lx = ux_ref.shape[1]
    pad_l = (LRU_CONV - 1) // 2

    def fill(pad_ref, u_ref, L):
        T = _time_chunk(L)
        _zero_margins(pad_ref, L)

        def body(j, carry):
            base = pl.multiple_of(j * T, T)
            for s in range(n_slabs):
                pad_ref[s, pl.ds(CONV_MARGIN + base, T), :] = u_ref[0, pl.ds(base, T),
                                                                    C + s * LANES:C + (s + 1) * LANES]
            return carry
        lax.fori_loop(0, L // T, body, 0)

    fill(cpad, uc_ref, Lc)
    fill(xpad, ux_ref, Lx)

    def coeffs(pad_ref, base, T, d):
        x = jnp.concatenate([_dw_conv_slab(pad_ref, s, base, T, cw_ref, cb_ref, s * LANES, LRU_CONV, pad_l)
                             for s in range(n_slabs)], axis=-1)
        t = jnp.tanh(jnp.dot(x.astype(BF16), wcat_ref[:, 2 * d * C:2 * (d + 1) * C],
                             preferred_element_type=F32) + bcat_ref[:, 2 * d * C:2 * (d + 1) * C])
        i = 0.5 * t[:, C:] + 0.5
        z = -lam_ref[d:d + 1, :]
        softplus = jnp.maximum(z, 0.0) + jnp.log(1.0 + jnp.exp(-jnp.abs(z)))
        half_rate = (-0.5 * LRU_C) * softplus
        a = jnp.exp(half_rate * t[:, :C] + half_rate)
        b = jnp.sqrt(1.0 - a * a) * (i * x)
        for s in range(n_slabs):
            a_s[d * n_slabs + s, pl.ds(0, T), :] = a[:, s * LANES:(s + 1) * LANES]
            b_s[d * n_slabs + s, pl.ds(0, T), :] = b[:, s * LANES:(s + 1) * LANES]

    def run(pad_ref, L, h, y_ref):
        T = _time_chunk(L)
        n = L // T

        def chunk(j, h):
            base_f = pl.multiple_of(j * T, T)
            base_b = pl.multiple_of((n - 1 - j) * T, T)
            coeffs(pad_ref, base_f, T, 0)
            coeffs(pad_ref, base_b, T, 1)

            def step(t, h):
                new = []
                for d, (base, row) in enumerate(((base_f, t), (base_b, T - 1 - t))):
                    for s in range(n_slabs):
                        k = d * n_slabs + s
                        hs = a_s[k, pl.ds(row, 1), :] * h[k] + b_s[k, pl.ds(row, 1), :]
                        if y_ref is not None:
                            y_ref[k, pl.ds(base + row, 1), :] = hs
                        new.append(hs)
                return tuple(new)
            return lax.fori_loop(0, T, step, h, unroll=8)
        return lax.fori_loop(0, n, chunk, h)

    h = tuple(jnp.zeros((1, LANES), F32) for _ in range(2 * n_slabs))
    h = run(cpad, Lc, h, yc)
    run(xpad, Lx, h, yx)

    def finish(u_ref, y_ref, o_ref, L):
        T = _time_chunk(L)

        def body(j, carry):
            base = pl.multiple_of(j * T, T)
            y = jnp.concatenate([y_ref[s, pl.ds(base, T), :] + y_ref[n_slabs + s, pl.ds(base, T), :]
                                 for s in range(n_slabs)], axis=-1)
            o_ref[0, pl.ds(base, T), :] = _gelu_tanh(u_ref[0, pl.ds(base, T), :C]) * y
            return carry
        lax.fori_loop(0, L // T, body, 0)

    finish(ux_ref, yx, ox_ref, Lx)
    if need_ctx:
        finish(uc_ref, yc, oc_ref, Lc)


def _block_diag(w):
    H, n, _ = w.shape
    eye = jnp.eye(H, dtype=w.dtype)
    return (eye[:, None, :, None] * w[:, :, None, :]).reshape(H * n, H * n)


def rglru_mixer(uc, ux, lp, need_ctx):
    B, Lc, _ = uc.shape
    Lx = ux.shape[1]
    C = D_GROUP
    wcat = (0.5 * jnp.concatenate([_block_diag(lp["lru_wa"][0]), _block_diag(lp["lru_wx"][0]),
                                   _block_diag(lp["lru_wa"][1]), _block_diag(lp["lru_wx"][1])], axis=1)).astype(BF16)
    bcat = 0.5 * jnp.concatenate([lp["lru_ba"][0], lp["lru_bx"][0], lp["lru_ba"][1], lp["lru_bx"][1]]).reshape(1, 4 * C)
    full = lambda r, c: pl.BlockSpec((r, c), lambda i: (0, 0))
    seq = lambda L, n: pl.BlockSpec((1, L, n), lambda i: (i, 0, 0))
    out_specs = [seq(Lx, C)]
    out_shape = [jax.ShapeDtypeStruct((B, Lx, C), F32)]
    if need_ctx:
        out_specs = [seq(Lc, C)] + out_specs
        out_shape = [jax.ShapeDtypeStruct((B, Lc, C), F32)] + out_shape
    T = _time_chunk(Lx)
    slab = lambda rows, n=1: pltpu.VMEM((n * C // LANES, rows, LANES), F32)
    scratch = [slab(Lc + 2 * CONV_MARGIN), slab(Lx + 2 * CONV_MARGIN), slab(T, 2), slab(T, 2), slab(Lx, 2)]
    if need_ctx:
        scratch.append(slab(Lc, 2))
    res = pl.pallas_call(
        functools.partial(_lru_kernel, need_ctx=need_ctx),
        grid=(B,),
        in_specs=[seq(Lc, 2 * C), seq(Lx, 2 * C), full(LRU_CONV, C), full(1, C), full(C, 4 * C),
                  full(1, 4 * C), full(2, C)],
        out_specs=out_specs,
        out_shape=out_shape,
        scratch_shapes=scratch,
        compiler_params=_cparams("parallel"),
        name="rglru",
    )(uc, ux, lp["lru_conv_w"], lp["lru_conv_b"].reshape(1, C), wcat, bcat, lp["lru_lambda"])
    if need_ctx:
        return res[0], res[1]
    return None, res[0]


HY_SHORT = 3


def _short_conv(pad_ref, base, T, w_ref, b_ref, c0, c1):
    return jnp.concatenate([_dw_conv_slab(pad_ref, col // LANES, base, T, w_ref, b_ref, col, HY_SHORT, 1)
                            for col in range(c0, c1, LANES)], axis=-1)


def _fill_padded(pad_ref, u_ref, L, T):
    _zero_margins(pad_ref, L)

    def body(j, carry):
        base = pl.multiple_of(j * T, T)
        for s in range(pad_ref.shape[0]):
            pad_ref[s, pl.ds(CONV_MARGIN + base, T), :] = u_ref[0, pl.ds(base, T), s * LANES:(s + 1) * LANES]
        return carry
    lax.fori_loop(0, L // T, body, 0)


def _hyena_pre_kernel(u_ref, w_ref, b_ref, z_ref, upad):
    L = u_ref.shape[1]
    T = _time_chunk(L)
    C = D_GROUP
    _fill_padded(upad, u_ref, L, T)

    def body(j, carry):
        base = pl.multiple_of(j * T, T)
        x1 = _short_conv(upad, base, T, w_ref, b_ref, C, 2 * C)
        v = _short_conv(upad, base, T, w_ref, b_ref, 2 * C, 3 * C)
        z_ref[pl.ds(base, T), :] = (x1 * v).astype(BF16)
        return carry
    lax.fori_loop(0, L // T, body, 0)


def _hyena_post_kernel(u_ref, y_ref, w_ref, b_ref, bias_ref, o_ref, upad):
    L = u_ref.shape[1]
    T = _time_chunk(L)
    C = D_GROUP
    _fill_padded(upad, u_ref, L, T)

    def body(j, carry):
        base = pl.multiple_of(j * T, T)
        x0 = _short_conv(upad, base, T, w_ref, b_ref, 0, C)
        x1 = _short_conv(upad, base, T, w_ref, b_ref, C, 2 * C)
        v = _short_conv(upad, base, T, w_ref, b_ref, 2 * C, 3 * C)
        o_ref[0, pl.ds(base, T), :] = x0 * (y_ref[pl.ds(base, T), :] + (x1 * v) * bias_ref[...])
        return carry
    lax.fori_loop(0, L // T, body, 0)


def _spectrum_kernel(f_ref, z_ref, ha_ref, hb_ref, hc_ref, y_ref):
    tf = ha_ref.shape[0]
    acc = jnp.dot(f_ref[...], z_ref[...], preferred_element_type=F32)
    zr = acc[:tf]
    zi = acc[tf:]
    y_ref[:tf, :] = (zr * ha_ref[...] - zi * hb_ref[...]).astype(BF16)
    y_ref[tf:, :] = (zr * hb_ref[...] + zi * hc_ref[...]).astype(BF16)


def _idft_kernel(f_ref, y_ref, o_ref):
    o_ref[...] = jnp.dot(f_ref[...], y_ref[...], preferred_element_type=F32)


def dft_tables(L):
    N = 2 * L
    tf = min(256, L)
    k = jnp.arange(L, dtype=jnp.int32)
    n = jnp.arange(L, dtype=jnp.int32)
    ang = (2.0 * math.pi / N) * ((k[:, None] * n[None, :]) % N).astype(F32)
    cos = jnp.cos(ang)
    sin = jnp.sin(ang)
    nyq = jnp.where(n % 2 == 0, 1.0, -1.0).astype(F32)
    f_re = cos
    f_im = (-sin).at[0].set(nyq)
    fwd = jnp.stack([f_re.reshape(L // tf, tf, L), f_im.reshape(L // tf, tf, L)], axis=1).reshape(N, L)
    ck = jnp.where(k == 0, 1.0, 2.0).astype(F32)[:, None] / N
    i_re = cos * ck
    i_im = (-sin * ck).at[0].set(nyq / N)
    inv = jnp.stack([i_re.reshape(L // tf, tf, L), i_im.reshape(L // tf, tf, L)], axis=1).reshape(N, L).T
    return fwd.astype(BF16), inv.astype(BF16)


def filter_spectrum(h_fwd, h_bwd):
    L, C = h_fwd.shape
    k = jnp.concatenate([h_fwd, jnp.zeros((1, C), F32), h_bwd[1:][::-1]], axis=0)
    hf = jnp.fft.rfft(k, axis=0)
    hr = jnp.real(hf)
    hi = jnp.imag(hf)
    a = hr[:L]
    b = hi[:L].at[0].set(0.0)
    c = hr[:L].at[0].set(hr[L])
    return a, b, c


def hyena_mixer(u, lp, tables):
    B, L, _ = u.shape
    C = D_GROUP
    N = 2 * L
    fwd, inv = tables
    tf = min(256, L)
    T = _time_chunk(L)
    w, bsh = lp["hy_short_w"], lp["hy_short_b"].reshape(1, 3 * C)
    z2 = pl.pallas_call(
        _hyena_pre_kernel,
        grid=(B,),
        in_specs=[pl.BlockSpec((1, L, 3 * C), lambda b: (b, 0, 0)),
                  pl.BlockSpec((HY_SHORT, 3 * C), lambda b: (0, 0)),
                  pl.BlockSpec((1, 3 * C), lambda b: (0, 0))],
        out_specs=pl.BlockSpec((L, C), lambda b: (0, b)),
        out_shape=jax.ShapeDtypeStruct((L, B * C), BF16),
        scratch_shapes=[pltpu.VMEM((3 * C // LANES, L + 2 * CONV_MARGIN, LANES), F32)],
        compiler_params=_cparams("parallel"),
        name="hyena_pre",
    )(u, w, bsh)

    h_fwd, h_bwd = _hyena_filters(L, lp)
    tn = 2 * C
    ha, hb, hc = [jnp.tile(t, (1, tn // C)) for t in filter_spectrum(h_fwd, h_bwd)]
    hspec = pl.BlockSpec((tf, tn), lambda i, j: (i, 0))
    y2 = pl.pallas_call(
        _spectrum_kernel,
        grid=(L // tf, B * C // tn),
        in_specs=[pl.BlockSpec((2 * tf, L), lambda i, j: (i, 0)),
                  pl.BlockSpec((L, tn), lambda i, j: (0, j)), hspec, hspec, hspec],
        out_specs=pl.BlockSpec((2 * tf, tn), lambda i, j: (i, j)),
        out_shape=jax.ShapeDtypeStruct((N, B * C), BF16),
        compiler_params=_cparams("parallel", "parallel"),
        name="hyena_spectrum",
    )(fwd, z2, ha, hb, hc)

    tl = min(256, L)
    yt = pl.pallas_call(
        _idft_kernel,
        grid=(L // tl, B * C // tn),
        in_specs=[pl.BlockSpec((tl, N), lambda i, j: (i, 0)),
                  pl.BlockSpec((N, tn), lambda i, j: (0, j))],
        out_specs=pl.BlockSpec((tl, tn), lambda i, j: (i, j)),
        out_shape=jax.ShapeDtypeStruct((L, B * C), F32),
        compiler_params=_cparams("parallel", "parallel"),
        name="hyena_idft",
    )(inv, y2)

    return pl.pallas_call(
        _hyena_post_kernel,
        grid=(B,),
        in_specs=[pl.BlockSpec((1, L, 3 * C), lambda b: (b, 0, 0)),
                  pl.BlockSpec((L, C), lambda b: (0, b)),
                  pl.BlockSpec((HY_SHORT, 3 * C), lambda b: (0, 0)),
                  pl.BlockSpec((1, 3 * C), lambda b: (0, 0)),
                  pl.BlockSpec((1, C), lambda b: (0, 0))],
        out_specs=pl.BlockSpec((1, L, C), lambda b: (b, 0, 0)),
        out_shape=jax.ShapeDtypeStruct((B, L, C), F32),
        scratch_shapes=[pltpu.VMEM((3 * C // LANES, L + 2 * CONV_MARGIN, LANES), F32)],
        compiler_params=_cparams("parallel"),
        name="hyena_post",
    )(u, yt, w, bsh, lp["hy_bias"].reshape(1, C))


FFT_N2 = 128
FFT_UNROLL = 8


class _FftPlan:
    def __init__(self, L):
        self.L = L
        self.N = 2 * L
        self.N1 = self.N // FFT_N2
        self.KH = self.N1 // 2 + 1
        self.KP = -(-self.KH // 8) * 8
        self.PA = 2 * self.KP + 4


def fft_tables(L):
    p = _FftPlan(L)
    N, N1, KH, KP = p.N, p.N1, p.KH, p.KP
    n2 = jnp.arange(FFT_N2, dtype=jnp.int32)
    k1 = jnp.arange(KP, dtype=jnp.int32)
    n1 = jnp.arange(N1, dtype=jnp.int32)
    n = FFT_N2 * n1[None, None, :] + n2[:, None, None]
    ang = (2.0 * math.pi / N) * ((k1[None, :, None] * n) % N).astype(F32)
    keep = (k1 < KH)[None, :, None]
    g_re = jnp.where(keep, jnp.cos(ang), 0.0)
    g_im = jnp.where(keep, -jnp.sin(ang), 0.0)
    ga_full = jnp.concatenate([g_re, g_im], axis=1)
    ck = jnp.where((k1 == 0) | (k1 == N1 // 2), 1.0, 2.0) / N
    ga_inv = jnp.swapaxes(ga_full[:, :, :N1 // 2] * jnp.tile(ck, 2)[None, :, None], 1, 2)
    kk = jnp.arange(FFT_N2, dtype=jnp.int32)
    ang2 = (2.0 * math.pi / FFT_N2) * ((kk[:, None] * kk[None, :]) % FFT_N2).astype(F32)
    fr, fi = jnp.cos(ang2), -jnp.sin(ang2)
    fb = jnp.block([[fr, -fi], [fi, fr]])
    fb_inv = jnp.block([[fr, fi], [-fi, fr]])
    return dict(ga_half=ga_full[:, :, :N1 // 2].astype(BF16), ga_full=ga_full.astype(BF16),
                ga_inv=ga_inv.astype(BF16), fb=fb.astype(BF16), fb_inv=fb_inv.astype(BF16))


def _fft_stage_a(x_ref, ga_ref, s_ref, plan, n1_count):
    n_slabs = x_ref.shape[0]

    def body(n2, carry):
        xs = jnp.concatenate([x_ref[s, pl.ds(n2, n1_count, stride=FFT_N2), :] for s in range(n_slabs)], axis=-1)
        a = jnp.dot(ga_ref[n2], xs.astype(BF16), preferred_element_type=F32)
        for s in range(n_slabs):
            s_ref[s, pl.ds(n2 * plan.PA, 2 * plan.KP), :] = a[:, s * LANES:(s + 1) * LANES]
        return carry
    lax.fori_loop(0, FFT_N2, body, 0, unroll=FFT_UNROLL)


def _fft_load_k1(s_ref, k1, plan):
    n_slabs = s_ref.shape[0]
    re = jnp.concatenate([s_ref[s, pl.ds(k1, FFT_N2, stride=plan.PA), :] for s in range(n_slabs)], axis=-1)
    im = jnp.concatenate([s_ref[s, pl.ds(plan.KP + k1, FFT_N2, stride=plan.PA), :] for s in range(n_slabs)], axis=-1)
    return jnp.concatenate([re, im], axis=0).astype(BF16)


def _fft_filter_kernel(k_ref, ga_ref, fb_ref, h_ref, s_ref, *, plan):
    _fft_stage_a(k_ref, ga_ref, s_ref, plan, plan.N1)

    def body(k1, carry):
        h_ref[k1] = jnp.dot(fb_ref[...], _fft_load_k1(s_ref, k1, plan), preferred_element_type=F32).astype(BF16)
        return carry
    lax.fori_loop(0, plan.KH, body, 0)


def _fft_conv_kernel(z_ref, ga_ref, gi_ref, fb_ref, fbi_ref, h_ref, y_ref, s_ref, *, plan):
    zs = z_ref.at[0]
    ys = y_ref.at[0]
    n_slabs = zs.shape[0]
    half = FFT_N2
    _fft_stage_a(zs, ga_ref, s_ref, plan, plan.N1 // 2)

    def body_b(k1, carry):
        x = jnp.dot(fb_ref[...], _fft_load_k1(s_ref, k1, plan), preferred_element_type=F32)
        h = h_ref[k1].astype(F32)
        xr, xi, hr, hi = x[:half], x[half:], h[:half], h[half:]
        y = jnp.concatenate([xr * hr - xi * hi, xr * hi + xi * hr], axis=0).astype(BF16)
        b = jnp.dot(fbi_ref[...], y, preferred_element_type=F32)
        for s in range(n_slabs):
            s_ref[s, pl.ds(k1, FFT_N2, stride=plan.PA), :] = b[:half, s * LANES:(s + 1) * LANES]
            s_ref[s, pl.ds(plan.KP + k1, FFT_N2, stride=plan.PA), :] = b[half:, s * LANES:(s + 1) * LANES]
        return carry
    lax.fori_loop(0, plan.KH, body_b, 0, unroll=3)

    def body_a(n2, carry):
        b = jnp.concatenate([s_ref[s, pl.ds(n2 * plan.PA, 2 * plan.KP), :] for s in range(n_slabs)], axis=-1)
        y = jnp.dot(gi_ref[n2], b.astype(BF16), preferred_element_type=F32)
        for s in range(n_slabs):
            ys[s, pl.ds(n2, plan.N1 // 2, stride=FFT_N2), :] = y[:, s * LANES:(s + 1) * LANES]
        return carry
    lax.fori_loop(0, FFT_N2, body_a, 0, unroll=FFT_UNROLL)


def fft_filter_spectrum(h_fwd, h_bwd, tabs):
    L, C = h_fwd.shape
    plan = _FftPlan(L)
    n_slabs = C // LANES
    k = jnp.concatenate([h_fwd, jnp.zeros((1, C), F32), h_bwd[1:][::-1]], axis=0)
    k = k.reshape(plan.N, n_slabs, LANES).transpose(1, 0, 2)
    full = lambda shape: pl.BlockSpec(shape, lambda i: (0,) * len(shape))
    return pl.pallas_call(
        functools.partial(_fft_filter_kernel, plan=plan),
        grid=(1,),
        in_specs=[full((n_slabs, plan.N, LANES)), full((FFT_N2, 2 * plan.KP, plan.N1)),
                  full((2 * FFT_N2, 2 * FFT_N2))],
        out_specs=full((plan.KH, 2 * FFT_N2, C)),
        out_shape=jax.ShapeDtypeStruct((plan.KH, 2 * FFT_N2, C), BF16),
        scratch_shapes=[pltpu.VMEM((n_slabs, FFT_N2 * plan.PA, LANES), F32)],
        compiler_params=_cparams("arbitrary"),
        name="hyena_filter_fft",
    )(k, tabs["ga_full"], tabs["fb"])


def fft_long_conv(z, h_spec, tabs):
    B, n_slabs, L, _ = z.shape
    plan = _FftPlan(L)
    C = n_slabs * LANES
    full = lambda shape: pl.BlockSpec(shape, lambda b: (0,) * len(shape))
    seq = pl.BlockSpec((1, n_slabs, L, LANES), lambda b: (b, 0, 0, 0))
    return pl.pallas_call(
        functools.partial(_fft_conv_kernel, plan=plan),
        grid=(B,),
        in_specs=[seq, full((FFT_N2, 2 * plan.KP, plan.N1 // 2)), full((FFT_N2, plan.N1 // 2, 2 * plan.KP)),
                  full((2 * FFT_N2, 2 * FFT_N2)), full((2 * FFT_N2, 2 * FFT_N2)),
                  full((plan.KH, 2 * FFT_N2, C))],
        out_specs=seq,
        out_shape=jax.ShapeDtypeStruct((B, n_slabs, L, LANES), F32),
        scratch_shapes=[pltpu.VMEM((n_slabs, FFT_N2 * plan.PA, LANES), F32)],
        compiler_params=_cparams("parallel"),
        name="hyena_fft_conv",
    )(z, tabs["ga_half"], tabs["ga_inv"], tabs["fb"], tabs["fb_inv"], h_spec)


def _hyena_pre_slab_kernel(u_ref, w_ref, b_ref, z_ref, upad):
    L = u_ref.shape[1]
    T = _time_chunk(L)
    C = D_GROUP
    _fill_padded(upad, u_ref, L, T)

    def body(j, carry):
        base = pl.multiple_of(j * T, T)
        for s in range(C // LANES):
            x1 = _dw_conv_slab(upad, C // LANES + s, base, T, w_ref, b_ref, C + s * LANES, HY_SHORT, 1)
            v = _dw_conv_slab(upad, 2 * C // LANES + s, base, T, w_ref, b_ref, 2 * C + s * LANES, HY_SHORT, 1)
            z_ref[0, s, pl.ds(base, T), :] = x1 * v
        return carry
    lax.fori_loop(0, L // T, body, 0)


def _hyena_post_slab_kernel(u0_ref, z_ref, y_ref, w_ref, b_ref, bias_ref, o_ref, upad):
    L = u0_ref.shape[1]
    T = _time_chunk(L)
    C = D_GROUP
    _fill_padded(upad, u0_ref, L, T)

    def body(j, carry):
        base = pl.multiple_of(j * T, T)
        x0 = _short_conv(upad, base, T, w_ref, b_ref, 0, C)
        z = jnp.concatenate([z_ref[0, s, pl.ds(base, T), :] for s in range(C // LANES)], axis=-1)
        y = jnp.concatenate([y_ref[0, s, pl.ds(base, T), :] for s in range(C // LANES)], axis=-1)
        o_ref[0, pl.ds(base, T), :] = x0 * (y + z * bias_ref[...])
        return carry
    lax.fori_loop(0, L // T, body, 0)


def hyena_mixer_fft(u, lp, tabs):
    B, L, _ = u.shape
    C = D_GROUP
    n_slabs = C // LANES
    w, bsh = lp["hy_short_w"], lp["hy_short_b"].reshape(1, 3 * C)
    useq = pl.BlockSpec((1, L, 3 * C), lambda b: (b, 0, 0))
    slabs = pl.BlockSpec((1, n_slabs, L, LANES), lambda b: (b, 0, 0, 0))
    wspec = pl.BlockSpec((HY_SHORT, 3 * C), lambda b: (0, 0))
    bspec = pl.BlockSpec((1, 3 * C), lambda b: (0, 0))
    pad_scratch = pltpu.VMEM((3 * C // LANES, L + 2 * CONV_MARGIN, LANES), F32)
    z = pl.pallas_call(
        _hyena_pre_slab_kernel,
        grid=(B,),
        in_specs=[useq, wspec, bspec],
        out_specs=slabs,
        out_shape=jax.ShapeDtypeStruct((B, n_slabs, L, LANES), F32),
        scratch_shapes=[pad_scratch],
        compiler_params=_cparams("parallel"),
        name="hyena_pre",
    )(u, w, bsh)
    h_fwd, h_bwd = _hyena_filters(L, lp)
    y = fft_long_conv(z, fft_filter_spectrum(h_fwd, h_bwd, tabs), tabs)
    return pl.pallas_call(
        _hyena_post_slab_kernel,
        grid=(B,),
        in_specs=[pl.BlockSpec((1, L, C), lambda b: (b, 0, 0)), slabs, slabs, wspec, bspec,
                  pl.BlockSpec((1, C), lambda b: (0, 0))],
        out_specs=pl.BlockSpec((1, L, C), lambda b: (b, 0, 0)),
        out_shape=jax.ShapeDtypeStruct((B, L, C), F32),
        scratch_shapes=[pltpu.VMEM((n_slabs, L + 2 * CONV_MARGIN, LANES), F32)],
        compiler_params=_cparams("parallel"),
        name="hyena_post",
    )(u, z, y, w, bsh, lp["hy_bias"].reshape(1, C))


def _hyena_filters(L, lp):
    t = jnp.linspace(0.0, 1.0, L, dtype=F32)[:, None]
    bands = (HY_EMB - 1) // 2
    w = 2.0 * math.pi * jnp.arange(L, dtype=F32)[:, None] / L
    f = jnp.linspace(1e-4, bands - 1, bands, dtype=F32)[None]
    z = jnp.concatenate([t, jnp.cos(f * w), -jnp.sin(f * w)], axis=-1)
    hdn = jnp.sin(z @ lp["hy_ffn_w1"] + lp["hy_ffn_b1"])
    hdn = jnp.sin(hdn @ lp["hy_ffn_w2"] + lp["hy_ffn_b2"])
    h = (hdn @ lp["hy_ffn_w3"]).reshape(L, 2, D_GROUP)
    max_decay = math.log(HY_TARGET) / HY_FAST_DECAY
    min_decay = math.log(HY_TARGET) / HY_SLOW_DECAY
    deltas = jnp.linspace(min_decay, max_decay, D_GROUP, dtype=F32)
    h = h * jnp.exp(-t * jnp.abs(deltas))[:, None, :]
    h = h / (jnp.sum(jnp.abs(h), axis=(0, 1), keepdims=True) + EPS)
    return h[:, 0], h[:, 1]


def _layer(hc, hx, c_silu_all, lp, need_ctx, final_g, final_norm, tables_x, tables_c, experts, layer):
    B, S, D = hx.shape
    C = hc.shape[1]
    mod = small_linear(c_silu_all, lp["ada_w"], lp["ada_b"])
    mod_x = mod[:B].reshape(B, 6, 1, D)
    mod_c = jnp.broadcast_to(mod[B].reshape(1, 6, 1, D), (B, 6, 1, D))
    w_ext = extend_w_in(lp["w_in"])
    cos_x, sin_x = rope_tables(S, True)
    cos_c, sin_c = rope_tables(C, False)
    hy_x, cf_x, at_x, lr_x = in_proj(hx, mod_x[:, 0], mod_x[:, 1], lp["norm1_g"], w_ext, cos_x, sin_x, tm=512)
    hy_c, cf_c, at_c, lr_c = in_proj(hc, mod_c[:, 0], mod_c[:, 1], lp["norm1_g"], w_ext, cos_c, sin_c, tm=256)

    yd_c, yd_x = rglru_mixer(lr_c, lr_x, lp, need_ctx)
    conf = lambda u: conformer_conv(u, lp["conf_dw_w"], lp["conf_dw_b"], lp["conf_ln_g"], lp["conf_ln_b"])
    ys_x = [hyena_mixer_fft(hy_x, lp, tables_x), conf(cf_x),
            window_attention(at_x, at_c, lp["attn_sink"]), yd_x]

    w_out = lp["w_out"].astype(BF16)
    w_router = jnp.zeros((D, ROUTER_COLS), F32)
    w_router = w_router.at[:, :N_GROUPS].set(lp["router_g_w"]).at[:, N_GROUPS:N_GROUPS + N_EXPERTS].set(lp["router_e_w"])
    w_router = w_router.astype(BF16)
    b_router = jnp.zeros((1, ROUTER_COLS), F32)
    b_router = b_router.at[0, :N_GROUPS].set(lp["router_g_b"]).at[0, N_GROUPS:N_GROUPS + N_EXPERTS].set(lp["router_e_b"])

    hx1, lg_x = out_proj(ys_x, hx, mod_x[:, 2], lp["group_norm_g"], w_out, lp["norm2_g"],
                         mod_x[:, 3], mod_x[:, 4], w_router, b_router, tm=512)
    h_tok = hx1.reshape(B * S, D)
    hc_tok = None
    lg = lg_x.reshape(B * S, ROUTER_COLS)
    if need_ctx:
        ys_c = [hyena_mixer(hy_c, lp, tables_c), conf(cf_c),
                context_attention(at_c, lp["attn_sink"]), yd_c]
        hc1, lg_c = out_proj(ys_c, hc, mod_c[:, 2], lp["group_norm_g"], w_out, lp["norm2_g"],
                             mod_c[:, 3], mod_c[:, 4], w_router, b_router, tm=256)
        hc_tok = hc1.reshape(B * C, D)
        lg = jnp.concatenate([lg, lg_c.reshape(B * C, ROUTER_COLS)], axis=0)

    T = lg.shape[0]
    n_blocks = -(-T // MOE_BLOCK) + N_CLASSES
    info, counts, ids = route_tokens(lg)
    dest, blk_a, blk_b, n_used = slot_plan(ids, counts, n_blocks)
    xs = moe_dispatch(h_tok, hc_tok, dest, n_blocks, lp["norm2_g"], (mod_x[:, 3], mod_x[:, 4]),
                      (mod_c[:, 3], mod_c[:, 4]), S)
    o_sorted = expert_pairs(xs, blk_a, blk_b, n_used, *experts, layer)
    hx2 = moe_collect(o_sorted, dest, info, 0, h_tok, mod_x[:, 5], S, final_g, final_norm)
    hx2 = hx2.reshape(B, S, D)
    if need_ctx:
        hc2 = moe_collect(o_sorted, dest, info, B * S // COLLECT_TOKENS, hc_tok, mod_c[:1, 5], C,
                          final_g, False).reshape(B, C, D)
    else:
        hc2 = hc
    return hc2, hx2


def kernel(x, c, ctx, c_ctx, norm1_g, norm2_g, ada_w, ada_b, w_in, hy_short_w, hy_short_b, hy_ffn_w1, hy_ffn_b1, hy_ffn_w2, hy_ffn_b2, hy_ffn_w3, hy_bias, conf_dw_w, conf_dw_b, conf_ln_g, conf_ln_b, attn_sink, lru_conv_w, lru_conv_b, lru_wa, lru_ba, lru_wx, lru_bx, lru_lambda, group_norm_g, w_out, router_g_w, router_g_b, router_e_w, router_e_b, exp_w_gate, exp_w_up, exp_w_down, final_norm_g):
    stacked = dict(norm1_g=norm1_g, norm2_g=norm2_g, ada_w=ada_w, ada_b=ada_b, w_in=w_in,
                   hy_short_w=hy_short_w, hy_short_b=hy_short_b, hy_ffn_w1=hy_ffn_w1, hy_ffn_b1=hy_ffn_b1,
                   hy_ffn_w2=hy_ffn_w2, hy_ffn_b2=hy_ffn_b2, hy_ffn_w3=hy_ffn_w3, hy_bias=hy_bias,
                   conf_dw_w=conf_dw_w, conf_dw_b=conf_dw_b, conf_ln_g=conf_ln_g, conf_ln_b=conf_ln_b,
                   attn_sink=attn_sink, lru_conv_w=lru_conv_w, lru_conv_b=lru_conv_b, lru_wa=lru_wa,
                   lru_ba=lru_ba, lru_wx=lru_wx, lru_bx=lru_bx, lru_lambda=lru_lambda,
                   group_norm_g=group_norm_g, w_out=w_out, router_g_w=router_g_w, router_g_b=router_g_b,
                   router_e_w=router_e_w, router_e_b=router_e_b)
    experts = (exp_w_gate.astype(BF16), exp_w_up.astype(BF16), exp_w_down.astype(BF16))
    depth = norm1_g.shape[0]
    B = x.shape[0]
    cs = jnp.concatenate([jax.nn.silu(c), jnp.broadcast_to(jax.nn.silu(c_ctx)[None], (8, c.shape[1]))], axis=0)
    hc, hx = ctx, x
    tables_x = fft_tables(x.shape[1])
    tables_c = dft_tables(ctx.shape[1])
    for l in range(depth):
        lp = {k: v[l] for k, v in stacked.items()}
        hc, hx = _layer(hc, hx, cs, lp, need_ctx=(l < depth - 1), final_g=final_norm_g,
                        final_norm=(l == depth - 1), tables_x=tables_x, tables_c=tables_c,
                        experts=experts, layer=l)
    return hx
```

```python
import functools
import math

import jax
import jax.numpy as jnp
from jax import lax
from jax.experimental import pallas as pl
from jax.experimental.pallas import tpu as pltpu

F32 = jnp.float32
BF16 = jnp.bfloat16

EPS = 1e-6
NEG_INF = -1e30
GRID_W = 64
N_MIXERS = 4
D_GROUP = 256
HY_COLS = 3 * D_GROUP
CONF_COLS = 2 * D_GROUP
ATT_HEADS = 4
ATT_KV_HEADS = 2
HEAD_DIM = 64
ATT_COLS = (ATT_HEADS + 2 * ATT_KV_HEADS) * HEAD_DIM
LRU_COLS = 2 * D_GROUP
QK_COLS = (ATT_HEADS + ATT_KV_HEADS) * HEAD_DIM
WINDOW = 128
ATT_BLOCK = 128
ROPE_BASE = 10000.0
HY_EMB = 33
HY_FAST_DECAY = 0.3
HY_SLOW_DECAY = 1.5
HY_TARGET = 1e-2
CONF_KERNEL = 31
LRU_HEADS = 4
LRU_CONV = 4
LRU_C = 8.0
N_GROUPS = 4
EXP_PER_GROUP = 8
N_EXPERTS = N_GROUPS * EXP_PER_GROUP
TOP_K = 2
MOE_BLOCK = 256
ROUTER_COLS = 128

VMEM_LIMIT_BYTES = 56 * 1024 * 1024


def _cparams(*sem):
    return pltpu.CompilerParams(dimension_semantics=sem, vmem_limit_bytes=VMEM_LIMIT_BYTES)


def _linear_kernel(x_ref, w_ref, b_ref, o_ref):
    o_ref[...] = jnp.dot(x_ref[...], w_ref[...], preferred_element_type=F32,
                         precision=lax.Precision.HIGHEST) + b_ref[...]


def small_linear(x, w, b, tn=1024):
    M, K = x.shape
    N = w.shape[1]
    return pl.pallas_call(
        _linear_kernel,
        grid=(N // tn,),
        in_specs=[pl.BlockSpec((M, K), lambda j: (0, 0)),
                  pl.BlockSpec((K, tn), lambda j: (0, j)),
                  pl.BlockSpec((1, tn), lambda j: (0, j))],
        out_specs=pl.BlockSpec((M, tn), lambda j: (0, j)),
        out_shape=jax.ShapeDtypeStruct((M, N), F32),
        compiler_params=_cparams("parallel"),
        name="ada_linear",
    )(x, w, b.reshape(1, N))


def _in_proj_kernel(x_ref, sh_ref, sc_ref, g_ref, w_ref, cos_ref, sin_ref,
                    hy_ref, cf_ref, at_ref, lr_ref):
    x = x_ref[0]
    ms = jnp.mean(x * x, axis=-1, keepdims=True)
    y = x * lax.rsqrt(ms + EPS) * g_ref[...]
    y = y * (1.0 + sc_ref[0]) + sh_ref[0]
    u = jnp.dot(y.astype(BF16), w_ref[...], preferred_element_type=F32)
    c0 = HY_COLS
    c1 = c0 + CONF_COLS
    c2 = c1 + ATT_COLS
    c3 = c2 + LRU_COLS
    hy_ref[0] = u[:, :c0]
    cf_ref[0] = u[:, c0:c1]
    lr_ref[0] = u[:, c2:c3]
    qk = u[:, c1:c1 + QK_COLS]
    qk_rot = u[:, c3:c3 + QK_COLS]
    at_ref[0, :, :QK_COLS] = qk * cos_ref[...] + qk_rot * sin_ref[...]
    at_ref[0, :, QK_COLS:] = u[:, c1 + QK_COLS:c2]


def in_proj(h, shift, scale, g, w_ext, cos_t, sin_t, tm):
    B, L, D = h.shape
    NW = w_ext.shape[1]
    outs = [HY_COLS, CONF_COLS, ATT_COLS, LRU_COLS]
    return pl.pallas_call(
        _in_proj_kernel,
        grid=(B, L // tm),
        in_specs=[pl.BlockSpec((1, tm, D), lambda b, i: (b, i, 0)),
                  pl.BlockSpec((1, 1, D), lambda b, i: (b, 0, 0)),
                  pl.BlockSpec((1, 1, D), lambda b, i: (b, 0, 0)),
                  pl.BlockSpec((1, D), lambda b, i: (0, 0)),
                  pl.BlockSpec((D, NW), lambda b, i: (0, 0)),
                  pl.BlockSpec((tm, QK_COLS), lambda b, i: (i, 0)),
                  pl.BlockSpec((tm, QK_COLS), lambda b, i: (i, 0))],
        out_specs=[pl.BlockSpec((1, tm, n), lambda b, i: (b, i, 0)) for n in outs],
        out_shape=[jax.ShapeDtypeStruct((B, L, n), F32) for n in outs],
        compiler_params=_cparams("parallel", "parallel"),
        name="in_proj",
    )(h, shift, scale, g.reshape(1, D), w_ext, cos_t, sin_t)


def rope_tables(L, rotary):
    n_heads = ATT_HEADS + ATT_KV_HEADS
    if not rotary:
        return jnp.ones((L, QK_COLS), F32), jnp.zeros((L, QK_COLS), F32)
    pos = jnp.arange(L)
    row = (pos // GRID_W).astype(F32)
    col = (pos % GRID_W).astype(F32)
    half = HEAD_DIM // 2
    inv_freq = ROPE_BASE ** (-jnp.arange(0, half, 2, dtype=F32) / half)
    ang_r = row[:, None] * inv_freq[None]
    ang_c = col[:, None] * inv_freq[None]
    cos_h = jnp.concatenate([jnp.cos(ang_r)] * 2 + [jnp.cos(ang_c)] * 2, axis=-1)
    sin_h = jnp.concatenate([jnp.sin(ang_r)] * 2 + [jnp.sin(ang_c)] * 2, axis=-1)
    return jnp.tile(cos_h, (1, n_heads)), jnp.tile(sin_h, (1, n_heads))


def extend_w_in(w_in):
    c1 = HY_COLS + CONF_COLS
    wqk = w_in[:, c1:c1 + QK_COLS]
    D = w_in.shape[0]
    w4 = wqk.reshape(D, QK_COLS // 32, 2, 16)
    wrot = jnp.stack([-w4[:, :, 1], w4[:, :, 0]], axis=2).reshape(D, QK_COLS)
    return jnp.concatenate([w_in, wrot], axis=1).astype(BF16)


def _softmax_parts(q, k_list, extra_logit):
    scale = HEAD_DIM ** -0.5
    s_list = []
    for k, mask in k_list:
        s = lax.dot_general(q, k, (((1,), (1,)), ((), ())), preferred_element_type=F32) * scale
        if mask is not None:
            s = jnp.where(mask, s, NEG_INF)
        s_list.append(s)
    m = extra_logit
    for s in s_list:
        m = jnp.maximum(m, jnp.max(s, axis=-1, keepdims=True))
    p_list = [jnp.exp(s - m) for s in s_list]
    denom = jnp.exp(extra_logit - m)
    for p in p_list:
        denom = denom + jnp.sum(p, axis=-1, keepdims=True)
    return p_list, 1.0 / denom


ATT_Q_BLOCKS = 16


def _win_attn_kernel(sink_ref, q_ref, kp_ref, kc_ref, kn_ref, vp_ref, vc_ref, vn_ref,
                     kx_ref, vx_ref, o_ref, *, seq_len):
    i = pl.program_id(1)
    blk = ATT_BLOCK
    qb = q_ref.shape[1] // blk
    scale = HEAD_DIM ** -0.5
    g = ATT_HEADS // ATT_KV_HEADS
    kw = jnp.concatenate([kp_ref[0], kc_ref[0], kn_ref[0]], axis=0)
    vw = jnp.concatenate([vp_ref[0], vc_ref[0], vn_ref[0]], axis=0).astype(BF16)
    kwt = kw.T.astype(BF16)
    kxt = kx_ref[0].T.astype(BF16)
    vx = vx_ref[0].astype(BF16)
    row = lax.broadcasted_iota(jnp.int32, (g * blk, 3 * blk), 0) % blk
    col = lax.broadcasted_iota(jnp.int32, (g * blk, 3 * blk), 1)
    band_bias = jnp.where(jnp.abs(col - blk - row) <= WINDOW, 0.0, NEG_INF)
    col1 = lax.broadcasted_iota(jnp.int32, (1, 3 * blk), 1)
    for j in range(qb):
        q_blk = i * qb + j
        k_pos = (q_blk - 1) * blk + col1
        edge_bias = jnp.where(k_pos >= 0, jnp.where(k_pos < seq_len, 0.0, NEG_INF), NEG_INF)
        bias = band_bias + edge_bias
        outs = []
        for kv in range(ATT_KV_HEADS):
            ksl = slice(kv * HEAD_DIM, (kv + 1) * HEAD_DIM)
            heads = range(kv * g, (kv + 1) * g)
            qs = (jnp.concatenate([q_ref[0, j * blk:(j + 1) * blk, h * HEAD_DIM:(h + 1) * HEAD_DIM]
                                   for h in heads], axis=0) * scale).astype(BF16)
            sink = jnp.concatenate([jnp.full((blk, 1), sink_ref[h], F32) for h in heads], axis=0)
            s_win = jnp.dot(qs, kwt[ksl, j * blk:(j + 3) * blk], preferred_element_type=F32) + bias
            s_ctx = jnp.dot(qs, kxt[ksl, :], preferred_element_type=F32)
            m = jnp.maximum(jnp.maximum(jnp.max(s_win, axis=-1, keepdims=True),
                                        jnp.max(s_ctx, axis=-1, keepdims=True)), sink)
            p_win = jnp.exp(s_win - m)
            p_ctx = jnp.exp(s_ctx - m)
            denom = (jnp.exp(sink - m) + jnp.sum(p_win, axis=-1, keepdims=True)
                     + jnp.sum(p_ctx, axis=-1, keepdims=True))
            o = (jnp.dot(p_win.astype(BF16), vw[j * blk:(j + 3) * blk, ksl], preferred_element_type=F32)
                 + jnp.dot(p_ctx.astype(BF16), vx[:, ksl], preferred_element_type=F32)) * (1.0 / denom)
            outs.extend([o[k * blk:(k + 1) * blk] for k in range(g)])
        o_ref[0, j * blk:(j + 1) * blk, :] = jnp.concatenate(outs, axis=-1)


def window_attention(at_x, at_c, sink):
    B, S, _ = at_x.shape
    C = at_c.shape[1]
    blk = ATT_BLOCK
    qb = ATT_Q_BLOCKS
    nb = S // blk
    kcol = QK_COLS // 128 - 1
    vcol = kcol + 1

    def edge_spec(col, off):
        return pl.BlockSpec((1, blk, 128), lambda b, i, s: (b, jnp.clip(i * qb + off, 0, nb - 1), col))

    def mid_spec(col):
        return pl.BlockSpec((1, qb * blk, 128), lambda b, i, s: (b, i, col))

    grid_spec = pltpu.PrefetchScalarGridSpec(
        num_scalar_prefetch=1,
        grid=(B, nb // qb),
        in_specs=[pl.BlockSpec((1, qb * blk, ATT_HEADS * HEAD_DIM), lambda b, i, s: (b, i, 0)),
                  edge_spec(kcol, -1), mid_spec(kcol), edge_spec(kcol, qb),
                  edge_spec(vcol, -1), mid_spec(vcol), edge_spec(vcol, qb),
                  pl.BlockSpec((1, C, 128), lambda b, i, s: (b, 0, kcol)),
                  pl.BlockSpec((1, C, 128), lambda b, i, s: (b, 0, vcol))],
        out_specs=pl.BlockSpec((1, qb * blk, ATT_HEADS * HEAD_DIM), lambda b, i, s: (b, i, 0)),
    )
    return pl.pallas_call(
        functools.partial(_win_attn_kernel, seq_len=S),
        grid_spec=grid_spec,
        out_shape=jax.ShapeDtypeStruct((B, S, ATT_HEADS * HEAD_DIM), F32),
        compiler_params=_cparams("parallel", "parallel"),
        name="window_attention",
    )(sink.astype(F32), at_x, at_x, at_x, at_x, at_x, at_x, at_x, at_c, at_c)


def _ctx_attn_kernel(sink_ref, q_ref, kx_ref, vx_ref, o_ref):
    q = q_ref[0].astype(BF16)
    kx = kx_ref[0].astype(BF16)
    vx = vx_ref[0].astype(BF16)
    g = ATT_HEADS // ATT_KV_HEADS
    outs = []
    for h in range(ATT_HEADS):
        kv = h // g
        qs = q[:, h * HEAD_DIM:(h + 1) * HEAD_DIM]
        ksl = slice(kv * HEAD_DIM, (kv + 1) * HEAD_DIM)
        (p_ctx,), inv = _softmax_parts(qs, [(kx[:, ksl], None)], sink_ref[h])
        outs.append(jnp.dot(p_ctx.astype(BF16), vx[:, ksl], preferred_element_type=F32) * inv)
    o_ref[0] = jnp.concatenate(outs, axis=-1)


def context_attention(at_c, sink):
    B, C, _ = at_c.shape
    kcol = QK_COLS // 128 - 1
    grid_spec = pltpu.PrefetchScalarGridSpec(
        num_scalar_prefetch=1,
        grid=(B,),
        in_specs=[pl.BlockSpec((1, C, ATT_HEADS * HEAD_DIM), lambda b, s: (b, 0, 0)),
                  pl.BlockSpec((1, C, 128), lambda b, s: (b, 0, kcol)),
                  pl.BlockSpec((1, C, 128), lambda b, s: (b, 0, kcol + 1))],
        out_specs=pl.BlockSpec((1, C, ATT_HEADS * HEAD_DIM), lambda b, s: (b, 0, 0)),
    )
    return pl.pallas_call(
        _ctx_attn_kernel,
        grid_spec=grid_spec,
        out_shape=jax.ShapeDtypeStruct((B, C, ATT_HEADS * HEAD_DIM), F32),
        compiler_params=_cparams("parallel"),
        name="context_attention",
    )(sink.astype(F32), at_c, at_c, at_c)


def _out_proj_kernel(y0_ref, y1_ref, y2_ref, y3_ref, h_ref, g1_ref, gng_ref, w_ref,
                     n2g_ref, sh_ref, sc_ref, wr_ref, br_ref, ho_ref, lg_ref):
    parts = []
    for k, y_ref in enumerate((y0_ref, y1_ref, y2_ref, y3_ref)):
        y = y_ref[0]
        ms = jnp.mean(y * y, axis=-1, keepdims=True)
        yn = y * lax.rsqrt(ms + EPS) * gng_ref[:, k * D_GROUP:(k + 1) * D_GROUP]
        parts.append(yn.astype(BF16))
    yn = jnp.concatenate(parts, axis=-1)
    proj = jnp.dot(yn, w_ref[...], preferred_element_type=F32)
    h = h_ref[0] + g1_ref[0] * proj
    ho_ref[0] = h
    ms = jnp.mean(h * h, axis=-1, keepdims=True)
    n = h * lax.rsqrt(ms + EPS) * n2g_ref[...]
    n = n * (1.0 + sc_ref[0]) + sh_ref[0]
    lg_ref[0] = jnp.dot(n.astype(BF16), wr_ref[...], preferred_element_type=F32) + br_ref[...]


def out_proj(ys, h, g1, gng, w_out, n2g, sh2, sc2, w_router, b_router, tm):
    B, L, D = h.shape
    row3 = lambda n: pl.BlockSpec((1, tm, n), lambda b, i: (b, i, 0))
    mod = pl.BlockSpec((1, 1, D), lambda b, i: (b, 0, 0))
    full = lambda r, c: pl.BlockSpec((r, c), lambda b, i: (0, 0))
    return pl.pallas_call(
        _out_proj_kernel,
        grid=(B, L // tm),
        in_specs=[row3(D_GROUP)] * 4 + [row3(D), mod, full(1, D), full(D, D), full(1, D), mod, mod,
                                        full(D, ROUTER_COLS), full(1, ROUTER_COLS)],
        out_specs=[row3(D), row3(ROUTER_COLS)],
        out_shape=[jax.ShapeDtypeStruct((B, L, D), F32), jax.ShapeDtypeStruct((B, L, ROUTER_COLS), F32)],
        compiler_params=_cparams("parallel", "parallel"),
        name="out_proj",
    )(*ys, h, g1, gng.reshape(1, D), w_out, n2g.reshape(1, D), sh2, sc2, w_router, b_router)


N_PAIRS = EXP_PER_GROUP * (EXP_PER_GROUP - 1) // 2
N_CLASSES = N_GROUPS * N_PAIRS
ROUTE_TOKENS = 512
INFO_CLASS, INFO_RANK, INFO_WA, INFO_WB = 0, 1, 2, 3


SUBLANES = 8


def _route_kernel(lg_ref, below_ref, info_ref, cnt_ref, ids_ref, run):
    i = pl.program_id(0)

    @pl.when(i == 0)
    def _():
        run[...] = jnp.zeros_like(run)

    lg = lg_ref[...]
    li = lax.broadcasted_iota(jnp.int32, lg.shape, 1).astype(F32)
    big = float(ROUTER_COLS)

    def first_argmax(vals):
        m = jnp.max(vals, axis=-1, keepdims=True)
        return m, jnp.min(jnp.where(vals == m, li, big), axis=-1, keepdims=True)

    gl = jnp.where(li < N_GROUPS, lg, NEG_INF)
    gmax, g_idx = first_argmax(gl)
    g_prob = 1.0 / jnp.sum(jnp.exp(gl - gmax), axis=-1, keepdims=True)
    lo = N_GROUPS + EXP_PER_GROUP * g_idx
    el = jnp.where(li >= lo, jnp.where(li < lo + EXP_PER_GROUP, lg, NEG_INF), NEG_INF)
    m1, i1 = first_argmax(el)
    m2, i2 = first_argmax(jnp.where(li == i1, NEG_INF, el))
    e2 = jnp.exp(m2 - m1)
    w1 = g_prob / (1.0 + e2)
    w2 = g_prob * e2 / (1.0 + e2)
    j1 = i1 - lo
    j2 = i2 - lo
    a = jnp.minimum(j1, j2)
    b = jnp.maximum(j1, j2)
    cls = g_idx * N_PAIRS + (a * (2 * EXP_PER_GROUP - 1 - a)) * 0.5 + (b - a - 1.0)
    w_a = jnp.where(j1 < j2, w1, w2)
    w_b = jnp.where(j1 < j2, w2, w1)

    hit = li == cls
    onehot = jnp.where(hit, 1.0, 0.0)
    before = jnp.dot(below_ref[...], onehot.astype(BF16), preferred_element_type=F32)
    rank = jnp.sum(jnp.where(hit, before + run[...], 0.0), axis=-1, keepdims=True)
    run[...] = run[...] + jnp.sum(onehot, axis=0, keepdims=True)
    cnt_ref[...] = run[...]
    info = jnp.where(li == INFO_CLASS, cls, 0.0)
    info = jnp.where(li == INFO_RANK, rank, info)
    info = jnp.where(li == INFO_WA, w_a, info)
    info = jnp.where(li == INFO_WB, w_b, info)
    info_ref[...] = info
    ids_ref[0] = info.T[:SUBLANES].astype(jnp.int32)


def route_tokens(logits):
    T = logits.shape[0]
    tb = ROUTE_TOKENS
    below = (jnp.arange(tb)[None, :] < jnp.arange(tb)[:, None]).astype(BF16)
    return pl.pallas_call(
        _route_kernel,
        grid=(T // tb,),
        in_specs=[pl.BlockSpec((tb, ROUTER_COLS), lambda i: (i, 0)),
                  pl.BlockSpec((tb, tb), lambda i: (0, 0))],
        out_specs=[pl.BlockSpec((tb, ROUTER_COLS), lambda i: (i, 0)),
                   pl.BlockSpec((1, ROUTER_COLS), lambda i: (0, 0)),
                   pl.BlockSpec((1, SUBLANES, tb), lambda i: (i, 0, 0))],
        out_shape=[jax.ShapeDtypeStruct((T, ROUTER_COLS), F32), jax.ShapeDtypeStruct((1, ROUTER_COLS), F32),
                   jax.ShapeDtypeStruct((T // tb, SUBLANES, tb), jnp.int32)],
        scratch_shapes=[pltpu.VMEM((1, ROUTER_COLS), F32)],
        compiler_params=_cparams("arbitrary"),
        name="moe_route",
    )(logits, below)


def _pair_tables():
    a_tab, b_tab = [], []
    for g in range(N_GROUPS):
        for a in range(EXP_PER_GROUP):
            for b in range(a + 1, EXP_PER_GROUP):
                a_tab.append(g * EXP_PER_GROUP + a)
                b_tab.append(g * EXP_PER_GROUP + b)
    return jnp.array(a_tab, jnp.int32), jnp.array(b_tab, jnp.int32)


def _slot_kernel(ids_ref, start_ref, dest_ref):
    cls = ids_ref[0, INFO_CLASS:INFO_CLASS + 1, :]
    rank = ids_ref[0, INFO_RANK:INFO_RANK + 1, :]
    ci = lax.broadcasted_iota(jnp.int32, (ROUTER_COLS, cls.shape[1]), 0)
    start = jnp.sum(jnp.where(ci == cls, start_ref[...], 0), axis=0, keepdims=True)
    dest_ref[0] = jnp.broadcast_to(start + rank, dest_ref.shape[1:])


def slot_plan(ids, counts, n_blocks):
    nt, _, tb = ids.shape
    cnt = counts[0, :N_CLASSES].astype(jnp.int32)
    padded = (cnt + MOE_BLOCK - 1) // MOE_BLOCK * MOE_BLOCK
    upto = jnp.arange(N_CLASSES)[None, :] <= jnp.arange(N_CLASSES)[:, None]
    pad_end = jnp.sum(jnp.where(upto, padded[None, :], 0), axis=1)
    class_start = jnp.zeros((ROUTER_COLS, 1), jnp.int32).at[:N_CLASSES, 0].set(pad_end - padded)
    dest = pl.pallas_call(
        _slot_kernel,
        grid=(nt,),
        in_specs=[pl.BlockSpec((1, SUBLANES, tb), lambda i: (i, 0, 0)),
                  pl.BlockSpec((ROUTER_COLS, 1), lambda i: (0, 0))],
        out_specs=pl.BlockSpec((1, SUBLANES, tb), lambda i: (i, 0, 0)),
        out_shape=jax.ShapeDtypeStruct((nt, SUBLANES, tb), jnp.int32),
        compiler_params=_cparams("parallel"),
        name="moe_slots",
    )(ids, class_start)[:, 0, :].reshape(nt * tb)
    n_used = (pad_end[-1] // MOE_BLOCK).astype(jnp.int32).reshape(1)
    blk_first = jnp.arange(n_blocks, dtype=jnp.int32) * MOE_BLOCK
    blk_cls = jnp.minimum(jnp.sum((pad_end[None, :] <= blk_first[:, None]).astype(jnp.int32), axis=1),
                          N_CLASSES - 1)
    a_tab, b_tab = _pair_tables()
    hit = blk_cls[:, None] == jnp.arange(N_CLASSES)[None, :]
    pick = lambda tab: jnp.sum(jnp.where(hit, tab[None, :], 0), axis=1).astype(jnp.int32)
    return dest, pick(a_tab), pick(b_tab), n_used


DISPATCH_TOKENS = 1024
COLLECT_TOKENS = 512


def _wait_rows(buf, sem):
    pltpu.make_async_copy(buf, buf, sem).wait()


DMA_UNROLL = 8
TOKEN_TILE_ROWS = 8


def _store_token_tiles(tiles_ref, offset, pitch, x):
    n = x.shape[0]
    for j in range(x.shape[1] // LANES):
        tiles_ref[pl.ds(offset + j, n, stride=pitch), :] = x[:, j * LANES:(j + 1) * LANES]


def _load_token_tiles(tiles_ref, offset, pitch, n, width):
    return jnp.concatenate([tiles_ref[pl.ds(offset + j, n, stride=pitch), :] for j in range(width // LANES)],
                           axis=-1)


def _dispatch_kernel(dest_ref, hx_ref, hc_ref, g_ref, shx_ref, scx_ref, shc_ref, scc_ref, zeros_hbm,
                     xs_hbm, rows, sems, *, n_latent_blocks):
    del zeros_hbm
    i = pl.program_id(0)
    n = pl.num_programs(0)
    slot = i % 2
    tb = hx_ref.shape[0]

    @pl.when(i >= 2)
    def _():
        _wait_rows(rows.at[slot], sems.at[slot])

    def normed(h_ref, sh_ref, sc_ref):
        h = h_ref[...]
        ms = jnp.mean(h * h, axis=-1, keepdims=True)
        return h * lax.rsqrt(ms + EPS) * g_ref[...] * (1.0 + sc_ref[0]) + sh_ref[0]

    @pl.when(i < n_latent_blocks)
    def _():
        _store_token_tiles(rows.at[slot], 0, TOKEN_TILE_ROWS, normed(hx_ref, shx_ref, scx_ref))

    @pl.when(i >= n_latent_blocks)
    def _():
        _store_token_tiles(rows.at[slot], 0, TOKEN_TILE_ROWS, normed(hc_ref, shc_ref, scc_ref))

    def body(g, carry):
        for u in range(DMA_UNROLL):
            r = g * DMA_UNROLL + u
            dst = pl.multiple_of(dest_ref[0, 0, r] * TOKEN_TILE_ROWS, TOKEN_TILE_ROWS)
            pltpu.make_async_copy(rows.at[slot, pl.ds(r * TOKEN_TILE_ROWS, TOKEN_TILE_ROWS)],
                                  xs_hbm.at[pl.ds(dst, TOKEN_TILE_ROWS)], sems.at[slot]).start(priority=u % 2)
        return carry
    lax.fori_loop(0, tb // DMA_UNROLL, body, 0)

    @pl.when(i == n - 1)
    def _():
        _wait_rows(rows.at[slot], sems.at[slot])

        @pl.when(n >= 2)
        def _():
            _wait_rows(rows.at[1 - slot], sems.at[1 - slot])


def moe_dispatch(h_x, h_c, dest, n_blocks, n2g, mod_x, mod_c, tokens_per_batch):
    Tx, D = h_x.shape
    tb = DISPATCH_TOKENS
    nxb = Tx // tb
    if h_c is None:
        h_c, mod_c, ncb = h_x, mod_x, 0
    else:
        ncb = h_c.shape[0] // tb
    per_b = tokens_per_batch // tb
    P = n_blocks * MOE_BLOCK
    tile_rows = D // LANES
    assert tile_rows == TOKEN_TILE_ROWS
    xi = lambda i: jnp.minimum(i, nxb - 1)
    ci = lambda i: jnp.maximum(i - nxb, 0)
    modx = pl.BlockSpec((1, 1, D), lambda i: (xi(i) // per_b, 0, 0))
    modc = pl.BlockSpec((1, 1, D), lambda i: (0, 0, 0))
    return pl.pallas_call(
        functools.partial(_dispatch_kernel, n_latent_blocks=nxb),
        grid=(nxb + ncb,),
        in_specs=[pl.BlockSpec((1, 1, tb), lambda i: (i, 0, 0), memory_space=pltpu.SMEM),
                  pl.BlockSpec((tb, D), lambda i: (xi(i), 0)),
                  pl.BlockSpec((tb, D), lambda i: (ci(i), 0)),
                  pl.BlockSpec((1, D), lambda i: (0, 0)),
                  modx, modx, modc, modc,
                  pl.BlockSpec(memory_space=pl.ANY)],
        out_specs=pl.BlockSpec(memory_space=pl.ANY),
        out_shape=jax.ShapeDtypeStruct((P * tile_rows, LANES), F32),
        scratch_shapes=[pltpu.VMEM((2, tb * tile_rows, LANES), F32), pltpu.SemaphoreType.DMA((2,))],
        input_output_aliases={8: 0},
        compiler_params=_cparams("arbitrary"),
        name="moe_dispatch",
    )(dest.reshape(-1, 1, tb), h_x, h_c, n2g.reshape(1, D), mod_x[0], mod_x[1], mod_c[0], mod_c[1],
      jnp.zeros((P * tile_rows, LANES), F32))


def _expert_pair_kernel(ea_ref, eb_ref, nused_ref, xs_ref, wga_ref, wua_ref, wda_ref, wgb_ref, wub_ref, wdb_ref,
                        o_ref):
    del ea_ref, eb_ref
    i = pl.program_id(0)
    D = wga_ref.shape[2]

    @pl.when(i < nused_ref[0])
    def _():
        xb = _load_token_tiles(xs_ref, 0, TOKEN_TILE_ROWS, MOE_BLOCK, D).astype(BF16)
        halves = []
        for wg_ref, wu_ref, wd_ref in ((wga_ref, wua_ref, wda_ref), (wgb_ref, wub_ref, wdb_ref)):
            gate = jnp.dot(xb, wg_ref[0, 0], preferred_element_type=F32)
            up = jnp.dot(xb, wu_ref[0, 0], preferred_element_type=F32)
            hid = (gate * jax.nn.sigmoid(gate) * up).astype(BF16)
            out = jnp.dot(hid, wd_ref[0, 0], preferred_element_type=F32)
            halves.append(lax.bitcast_convert_type(out.astype(BF16).astype(F32), jnp.uint32))
        _store_token_tiles(o_ref, 0, TOKEN_TILE_ROWS, halves[0] | (halves[1] >> 16))

    @pl.when(i >= nused_ref[0])
    def _():
        o_ref[...] = jnp.zeros_like(o_ref)


def _unpack_pair(words):
    hi = lax.bitcast_convert_type(words & jnp.uint32(0xFFFF0000), F32)
    lo = lax.bitcast_convert_type(words << 16, F32)
    return hi, lo


def expert_pairs(xs, blk_a, blk_b, n_used, w_gate, w_up, w_down, layer):
    D, DE = w_gate.shape[2:]
    P = xs.shape[0] // TOKEN_TILE_ROWS
    n_blocks = P // MOE_BLOCK
    wspec = lambda shape, which: pl.BlockSpec(shape, lambda i, ea, eb, nu: (layer, (ea, eb)[which][i], 0, 0))
    grid_spec = pltpu.PrefetchScalarGridSpec(
        num_scalar_prefetch=3,
        grid=(n_blocks,),
        in_specs=[pl.BlockSpec((MOE_BLOCK * TOKEN_TILE_ROWS, LANES), lambda i, ea, eb, nu: (i, 0)),
                  wspec((1, 1, D, DE), 0), wspec((1, 1, D, DE), 0), wspec((1, 1, DE, D), 0),
                  wspec((1, 1, D, DE), 1), wspec((1, 1, D, DE), 1), wspec((1, 1, DE, D), 1)],
        out_specs=pl.BlockSpec((MOE_BLOCK * TOKEN_TILE_ROWS, LANES), lambda i, ea, eb, nu: (i, 0)),
    )
    return pl.pallas_call(
        _expert_pair_kernel,
        grid_spec=grid_spec,
        out_shape=jax.ShapeDtypeStruct((P * TOKEN_TILE_ROWS, LANES), jnp.uint32),
        compiler_params=_cparams("arbitrary"),
        name="moe_experts",
    )(blk_a, blk_b, n_used, xs, w_gate, w_up, w_down, w_gate, w_up, w_down)


def _gather_pairs(idx_ref, src_hbm, buf, sem, n_tokens):
    def body(g, carry):
        for u in range(DMA_UNROLL):
            r = g * DMA_UNROLL + u
            src = pl.multiple_of(idx_ref[0, 0, r] * TOKEN_TILE_ROWS, TOKEN_TILE_ROWS)
            pltpu.make_async_copy(src_hbm.at[pl.ds(src, TOKEN_TILE_ROWS)],
                                  buf.at[pl.ds(r * TOKEN_TILE_ROWS, TOKEN_TILE_ROWS)], sem).start(priority=u % 2)
        return carry
    lax.fori_loop(0, n_tokens // DMA_UNROLL, body, 0)


def _collect_kernel(dest_ref, dest_next_ref, o_hbm, info_ref, h_ref, g2_ref, fg_ref, out_ref, obuf, sems, *,
                    final_norm):
    i = pl.program_id(0)
    n = pl.num_programs(0)
    slot = i % 2
    tb, D = h_ref.shape

    @pl.when(i == 0)
    def _():
        _gather_pairs(dest_ref, o_hbm, obuf.at[0], sems.at[0], tb)

    @pl.when(i + 1 < n)
    def _():
        _gather_pairs(dest_next_ref, o_hbm, obuf.at[1 - slot], sems.at[1 - slot], tb)

    _wait_rows(obuf.at[slot], sems.at[slot])
    e_a, e_b = _unpack_pair(_load_token_tiles(obuf.at[slot], 0, TOKEN_TILE_ROWS, tb, D))
    m = info_ref[:, INFO_WA:INFO_WA + 1] * e_a + info_ref[:, INFO_WB:INFO_WB + 1] * e_b
    h = h_ref[...] + g2_ref[0] * m
    if final_norm:
        ms = jnp.mean(h * h, axis=-1, keepdims=True)
        h = h * lax.rsqrt(ms + EPS) * fg_ref[...]
    out_ref[...] = h


def moe_collect(o_sorted, dest, info, block_offset, h_tokens, g2, tokens_per_batch, final_g, final_norm):
    T, D = h_tokens.shape
    tb = COLLECT_TOKENS
    nt = T // tb
    if g2.shape[0] == 1:
        g2_index = lambda i: 0
    else:
        assert tokens_per_batch % tb == 0
        g2_index = lambda i: i // (tokens_per_batch // tb)
    last = block_offset + nt - 1
    dest3 = dest.reshape(-1, 1, tb)
    return pl.pallas_call(
        functools.partial(_collect_kernel, final_norm=final_norm),
        grid=(nt,),
        in_specs=[pl.BlockSpec((1, 1, tb), lambda i: (block_offset + i, 0, 0), memory_space=pltpu.SMEM),
                  pl.BlockSpec((1, 1, tb), lambda i: (jnp.minimum(block_offset + i + 1, last), 0, 0),
                               memory_space=pltpu.SMEM),
                  pl.BlockSpec(memory_space=pl.ANY),
                  pl.BlockSpec((tb, ROUTER_COLS), lambda i: (block_offset + i, 0)),
                  pl.BlockSpec((tb, D), lambda i: (i, 0)),
                  pl.BlockSpec((1, 1, D), lambda i: (g2_index(i), 0, 0)),
                  pl.BlockSpec((1, D), lambda i: (0, 0))],
        out_specs=pl.BlockSpec((tb, D), lambda i: (i, 0)),
        out_shape=jax.ShapeDtypeStruct((T, D), F32),
        scratch_shapes=[pltpu.VMEM((2, tb * TOKEN_TILE_ROWS, LANES), jnp.uint32), pltpu.SemaphoreType.DMA((2,))],
        compiler_params=_cparams("arbitrary"),
        name="moe_collect",
    )(dest3, dest3, o_sorted, info, h_tokens, g2, final_g.reshape(1, D))


CONV_MARGIN = 16


def _time_chunk(L):
    return min(L, 256)


LANES = 128


def _zero_margins(pad_ref, L):
    zeros = jnp.zeros((CONV_MARGIN, LANES), F32)
    for s in range(pad_ref.shape[0]):
        pad_ref[s, pl.ds(0, CONV_MARGIN), :] = zeros
        pad_ref[s, pl.ds(CONV_MARGIN + L, CONV_MARGIN), :] = zeros


def _dw_conv_slab(pad_ref, s, base, T, w_ref, b_ref, col, taps, pad_left):
    acc = jnp.broadcast_to(b_ref[:, col:col + LANES], (T, LANES))
    for k in range(taps):
        acc = acc + w_ref[k:k + 1, col:col + LANES] * pad_ref[s, pl.ds(base + (CONV_MARGIN - pad_left + k), T), :]
    return acc


def _conformer_kernel(u_ref, w_ref, b_ref, g_ref, beta_ref, o_ref, ypad):
    L = o_ref.shape[1]
    T = _time_chunk(L)
    C = D_GROUP
    n_slabs = C // LANES
    pad = (CONF_KERNEL - 1) // 2
    _zero_margins(ypad, L)

    def glu(j, carry):
        base = pl.multiple_of(j * T, T)
        for s in range(n_slabs):
            a = u_ref[0, pl.ds(base, T), s * LANES:(s + 1) * LANES]
            gate = u_ref[0, pl.ds(base, T), C + s * LANES:C + (s + 1) * LANES]
            ypad[s, pl.ds(CONV_MARGIN + base, T), :] = a * jax.nn.sigmoid(gate)
        return carry
    lax.fori_loop(0, L // T, glu, 0)

    def conv(j, carry):
        base = pl.multiple_of(j * T, T)
        acc = jnp.concatenate([_dw_conv_slab(ypad, s, base, T, w_ref, b_ref, s * LANES, CONF_KERNEL, pad)
                               for s in range(n_slabs)], axis=-1)
        mu = jnp.mean(acc, axis=-1, keepdims=True)
        cen = acc - mu
        var = jnp.mean(cen * cen, axis=-1, keepdims=True)
        y = cen * lax.rsqrt(var + EPS) * g_ref[...] + beta_ref[...]
        o_ref[0, pl.ds(base, T), :] = y * jax.nn.sigmoid(y)
        return carry
    lax.fori_loop(0, L // T, conv, 0)


def conformer_conv(u, w, b, ln_g, ln_b):
    B, L, _ = u.shape
    C = D_GROUP
    vec = pl.BlockSpec((1, C), lambda i: (0, 0))
    return pl.pallas_call(
        _conformer_kernel,
        grid=(B,),
        in_specs=[pl.BlockSpec((1, L, 2 * C), lambda i: (i, 0, 0)),
                  pl.BlockSpec((CONF_KERNEL, C), lambda i: (0, 0)), vec, vec, vec],
        out_specs=pl.BlockSpec((1, L, C), lambda i: (i, 0, 0)),
        out_shape=jax.ShapeDtypeStruct((B, L, C), F32),
        scratch_shapes=[pltpu.VMEM((C // LANES, L + 2 * CONV_MARGIN, LANES), F32)],
        compiler_params=_cparams("parallel"),
        name="conformer_conv",
    )(u, w, b.reshape(1, C), ln_g.reshape(1, C), ln_b.reshape(1, C))


def _gelu_tanh(x):
    return 0.5 * x * (1.0 + jnp.tanh(math.sqrt(2.0 / math.pi) * (x + 0.044715 * (x * x * x))))


def _lru_kernel(uc_ref, ux_ref, cw_ref, cb_ref, wcat_ref, bcat_ref, lam_ref, *rest, need_ctx):
    if need_ctx:
        oc_ref, ox_ref, cpad, xpad, a_s, b_s, yx, yc = rest
    else:
        ox_ref, cpad, xpad, a_s, b_s, yx = rest
        oc_ref = yc = None
    C = D_GROUP
    n_slabs = C // LANES
    Lc = uc_ref.shape[1]
    Lx = ux_ref.shape[1]
    pad_l = (LRU_CONV - 1) // 2

    def fill(pad_ref, u_ref, L):
        T = _time_chunk(L)
        _zero_margins(pad_ref, L)

        def body(j, carry):
            base = pl.multiple_of(j * T, T)
            for s in range(n_slabs):
                pad_ref[s, pl.ds(CONV_MARGIN + base, T), :] = u_ref[0, pl.ds(base, T),
                                                                    C + s * LANES:C + (s + 1) * LANES]
            return carry
        lax.fori_loop(0, L // T, body, 0)

    fill(cpad, uc_ref, Lc)
    fill(xpad, ux_ref, Lx)

    def coeffs(pad_ref, base, T, d):
        x = jnp.concatenate([_dw_conv_slab(pad_ref, s, base, T, cw_ref, cb_ref, s * LANES, LRU_CONV, pad_l)
                             for s in range(n_slabs)], axis=-1)
        t = jnp.tanh(jnp.dot(x.astype(BF16), wcat_ref[:, 2 * d * C:2 * (d + 1) * C],
                             preferred_element_type=F32) + bcat_ref[:, 2 * d * C:2 * (d + 1) * C])
        i = 0.5 * t[:, C:] + 0.5
        z = -lam_ref[d:d + 1, :]
        softplus = jnp.maximum(z, 0.0) + jnp.log(1.0 + jnp.exp(-jnp.abs(z)))
        half_rate = (-0.5 * LRU_C) * softplus
        a = jnp.exp(half_rate * t[:, :C] + half_rate)
        b = jnp.sqrt(1.0 - a * a) * (i * x)
        for s in range(n_slabs):
            a_s[d * n_slabs + s, pl.ds(0, T), :] = a[:, s * LANES:(s + 1) * LANES]
            b_s[d * n_slabs + s, pl.ds(0, T), :] = b[:, s * LANES:(s + 1) * LANES]

    def run(pad_ref, L, h, y_ref):
        T = _time_chunk(L)
        n = L // T

        def chunk(j, h):
            base_f = pl.multiple_of(j * T, T)
            base_b = pl.multiple_of((n - 1 - j) * T, T)
            coeffs(pad_ref, base_f, T, 0)
            coeffs(pad_ref, base_b, T, 1)

            def step(t, h):
                new = []
                for d, (base, row) in enumerate(((base_f, t), (base_b, T - 1 - t))):
                    for s in range(n_slabs):
                        k = d * n_slabs + s
                        hs = a_s[k, pl.ds(row, 1), :] * h[k] + b_s[k, pl.ds(row, 1), :]
                        if y_ref is not None:
                            y_ref[k, pl.ds(base + row, 1), :] = hs
                        new.append(hs)
                return tuple(new)
            return lax.fori_loop(0, T, step, h, unroll=8)
        return lax.fori_loop(0, n, chunk, h)

    h = tuple(jnp.zeros((1, LANES), F32) for _ in range(2 * n_slabs))
    h = run(cpad, Lc, h, yc)
    run(xpad, Lx, h, yx)

    def finish(u_ref, y_ref, o_ref, L):
        T = _time_chunk(L)

        def body(j, carry):
            base = pl.multiple_of(j * T, T)
            y = jnp.concatenate([y_ref[s, pl.ds(base, T), :] + y_ref[n_slabs + s, pl.ds(base, T), :]
                                 for s in range(n_slabs)], axis=-1)
            o_ref[0, pl.ds(base, T), :] = _gelu_tanh(u_ref[0, pl.ds(base, T), :C]) * y
            return carry
        lax.fori_loop(0, L // T, body, 0)

    finish(ux_ref, yx, ox_ref, Lx)
    if need_ctx:
        finish(uc_ref, yc, oc_ref, Lc)


def _block_diag(w):
    H, n, _ = w.shape
    eye = jnp.eye(H, dtype=w.dtype)
    return (eye[:, None, :, None] * w[:, :, None, :]).reshape(H * n, H * n)


def rglru_mixer(uc, ux, lp, need_ctx):
    B, Lc, _ = uc.shape
    Lx = ux.shape[1]
    C = D_GROUP
    wcat = (0.5 * jnp.concatenate([_block_diag(lp["lru_wa"][0]), _block_diag(lp["lru_wx"][0]),
                                   _block_diag(lp["lru_wa"][1]), _block_diag(lp["lru_wx"][1])], axis=1)).astype(BF16)
    bcat = 0.5 * jnp.concatenate([lp["lru_ba"][0], lp["lru_bx"][0], lp["lru_ba"][1], lp["lru_bx"][1]]).reshape(1, 4 * C)
    full = lambda r, c: pl.BlockSpec((r, c), lambda i: (0, 0))
    seq = lambda L, n: pl.BlockSpec((1, L, n), lambda i: (i, 0, 0))
    out_specs = [seq(Lx, C)]
    out_shape = [jax.ShapeDtypeStruct((B, Lx, C), F32)]
    if need_ctx:
        out_specs = [seq(Lc, C)] + out_specs
        out_shape = [jax.ShapeDtypeStruct((B, Lc, C), F32)] + out_shape
    T = _time_chunk(Lx)
    slab = lambda rows, n=1: pltpu.VMEM((n * C // LANES, rows, LANES), F32)
    scratch = [slab(Lc + 2 * CONV_MARGIN), slab(Lx + 2 * CONV_MARGIN), slab(T, 2), slab(T, 2), slab(Lx, 2)]
    if need_ctx:
        scratch.append(slab(Lc, 2))
    res = pl.pallas_call(
        functools.partial(_lru_kernel, need_ctx=need_ctx),
        grid=(B,),
        in_specs=[seq(Lc, 2 * C), seq(Lx, 2 * C), full(LRU_CONV, C), full(1, C), full(C, 4 * C),
                  full(1, 4 * C), full(2, C)],
        out_specs=out_specs,
        out_shape=out_shape,
        scratch_shapes=scratch,
        compiler_params=_cparams("parallel"),
        name="rglru",
    )(uc, ux, lp["lru_conv_w"], lp["lru_conv_b"].reshape(1, C), wcat, bcat, lp["lru_lambda"])
    if need_ctx:
        return res[0], res[1]
    return None, res[0]


HY_SHORT = 3


def _short_conv(pad_ref, base, T, w_ref, b_ref, c0, c1):
    return jnp.concatenate([_dw_conv_slab(pad_ref, col // LANES, base, T, w_ref, b_ref, col, HY_SHORT, 1)
                            for col in range(c0, c1, LANES)], axis=-1)


def _fill_padded(pad_ref, u_ref, L, T):
    _zero_margins(pad_ref, L)

    def body(j, carry):
        base = pl.multiple_of(j * T, T)
        for s in range(pad_ref.shape[0]):
            pad_ref[s, pl.ds(CONV_MARGIN + base, T), :] = u_ref[0, pl.ds(base, T), s * LANES:(s + 1) * LANES]
        return carry
    lax.fori_loop(0, L // T, body, 0)


def _hyena_pre_kernel(u_ref, w_ref, b_ref, z_ref, upad):
    L = u_ref.shape[1]
    T = _time_chunk(L)
    C = D_GROUP
    _fill_padded(upad, u_ref, L, T)

    def body(j, carry):
        base = pl.multiple_of(j * T, T)
        x1 = _short_conv(upad, base, T, w_ref, b_ref, C, 2 * C)
        v = _short_conv(upad, base, T, w_ref, b_ref, 2 * C, 3 * C)
        z_ref[pl.ds(base, T), :] = (x1 * v).astype(BF16)
        return carry
    lax.fori_loop(0, L // T, body, 0)


def _hyena_post_kernel(u_ref, y_ref, w_ref, b_ref, bias_ref, o_ref, upad):
    L = u_ref.shape[1]
    T = _time_chunk(L)
    C = D_GROUP
    _fill_padded(upad, u_ref, L, T)

    def body(j, carry):
        base = pl.multiple_of(j * T, T)
        x0 = _short_conv(upad, base, T, w_ref, b_ref, 0, C)
        x1 = _short_conv(upad, base, T, w_ref, b_ref, C, 2 * C)
        v = _short_conv(upad, base, T, w_ref, b_ref, 2 * C, 3 * C)
        o_ref[0, pl.ds(base, T), :] = x0 * (y_ref[pl.ds(base, T), :] + (x1 * v) * bias_ref[...])
        return carry
    lax.fori_loop(0, L // T, body, 0)


def _spectrum_kernel(f_ref, z_ref, ha_ref, hb_ref, hc_ref, y_ref):
    tf = ha_ref.shape[0]
    acc = jnp.dot(f_ref[...], z_ref[...], preferred_element_type=F32)
    zr = acc[:tf]
    zi = acc[tf:]
    y_ref[:tf, :] = (zr * ha_ref[...] - zi * hb_ref[...]).astype(BF16)
    y_ref[tf:, :] = (zr * hb_ref[...] + zi * hc_ref[...]).astype(BF16)


def _idft_kernel(f_ref, y_ref, o_ref):
    o_ref[...] = jnp.dot(f_ref[...], y_ref[...], preferred_element_type=F32)


def dft_tables(L):
    N = 2 * L
    tf = min(256, L)
    k = jnp.arange(L, dtype=jnp.int32)
    n = jnp.arange(L, dtype=jnp.int32)
    ang = (2.0 * math.pi / N) * ((k[:, None] * n[None, :]) % N).astype(F32)
    cos = jnp.cos(ang)
    sin = jnp.sin(ang)
    nyq = jnp.where(n % 2 == 0, 1.0, -1.0).astype(F32)
    f_re = cos
    f_im = (-sin).at[0].set(nyq)
    fwd = jnp.stack([f_re.reshape(L // tf, tf, L), f_im.reshape(L // tf, tf, L)], axis=1).reshape(N, L)
    ck = jnp.where(k == 0, 1.0, 2.0).astype(F32)[:, None] / N
    i_re = cos * ck
    i_im = (-sin * ck).at[0].set(nyq / N)
    inv = jnp.stack([i_re.reshape(L // tf, tf, L), i_im.reshape(L // tf, tf, L)], axis=1).reshape(N, L).T
    return fwd.astype(BF16), inv.astype(BF16)


def filter_spectrum(k_slabs, fwd):
    n_slabs, N, _ = k_slabs.shape
    L = N // 2
    C = n_slabs * LANES
    tf = min(256, L)
    halves = jnp.concatenate([k_slabs[s, h * L:(h + 1) * L] for h in range(2) for s in range(n_slabs)], axis=1)
    r = pl.pallas_call(
        _idft_kernel,
        grid=(N // (2 * tf), 1),
        in_specs=[pl.BlockSpec((2 * tf, L), lambda i, j: (i, 0)),
                  pl.BlockSpec((L, 2 * C), lambda i, j: (0, 0))],
        out_specs=pl.BlockSpec((2 * tf, 2 * C), lambda i, j: (i, 0)),
        out_shape=jax.ShapeDtypeStruct((N, 2 * C), F32),
        compiler_params=_cparams("parallel", "parallel"),
        name="hyena_filter_dft",
    )(fwd, halves.astype(BF16)).reshape(L // tf, 2, tf, 2 * C)
    sign = jnp.where(jnp.arange(L) % 2 == 0, 1.0, -1.0).astype(F32)[:, None]
    re = r[:, 0].reshape(L, 2 * C)
    im = r[:, 1].reshape(L, 2 * C)
    hr = re[:, :C] + sign * re[:, C:]
    hi = im[:, :C] + sign * im[:, C:]
    return hr, hi.at[0].set(0.0), hr.at[0].set(hi[0])


def hyena_mixer(u, lp, tables):
    B, L, _ = u.shape
    C = D_GROUP
    N = 2 * L
    fwd, inv = tables
    tf = min(256, L)
    T = _time_chunk(L)
    w, bsh = lp["hy_short_w"], lp["hy_short_b"].reshape(1, 3 * C)
    z2 = pl.pallas_call(
        _hyena_pre_kernel,
        grid=(B,),
        in_specs=[pl.BlockSpec((1, L, 3 * C), lambda b: (b, 0, 0)),
                  pl.BlockSpec((HY_SHORT, 3 * C), lambda b: (0, 0)),
                  pl.BlockSpec((1, 3 * C), lambda b: (0, 0))],
        out_specs=pl.BlockSpec((L, C), lambda b: (0, b)),
        out_shape=jax.ShapeDtypeStruct((L, B * C), BF16),
        scratch_shapes=[pltpu.VMEM((3 * C // LANES, L + 2 * CONV_MARGIN, LANES), F32)],
        compiler_params=_cparams("parallel"),
        name="hyena_pre",
    )(u, w, bsh)

    tn = 2 * C
    ha, hb, hc = [jnp.tile(t, (1, tn // C)) for t in filter_spectrum(hyena_filter(L, lp), fwd)]
    hspec = pl.BlockSpec((tf, tn), lambda i, j: (i, 0))
    y2 = pl.pallas_call(
        _spectrum_kernel,
        grid=(L // tf, B * C // tn),
        in_specs=[pl.BlockSpec((2 * tf, L), lambda i, j: (i, 0)),
                  pl.BlockSpec((L, tn), lambda i, j: (0, j)), hspec, hspec, hspec],
        out_specs=pl.BlockSpec((2 * tf, tn), lambda i, j: (i, j)),
        out_shape=jax.ShapeDtypeStruct((N, B * C), BF16),
        compiler_params=_cparams("parallel", "parallel"),
        name="hyena_spectrum",
    )(fwd, z2, ha, hb, hc)

    tl = min(256, L)
    yt = pl.pallas_call(
        _idft_kernel,
        grid=(L // tl, B * C // tn),
        in_specs=[pl.BlockSpec((tl, N), lambda i, j: (i, 0)),
                  pl.BlockSpec((N, tn), lambda i, j: (0, j))],
        out_specs=pl.BlockSpec((tl, tn), lambda i, j: (i, j)),
        out_shape=jax.ShapeDtypeStruct((L, B * C), F32),
        compiler_params=_cparams("parallel", "parallel"),
        name="hyena_idft",
    )(inv, y2)

    return pl.pallas_call(
        _hyena_post_kernel,
        grid=(B,),
        in_specs=[pl.BlockSpec((1, L, 3 * C), lambda b: (b, 0, 0)),
                  pl.BlockSpec((L, C), lambda b: (0, b)),
                  pl.BlockSpec((HY_SHORT, 3 * C), lambda b: (0, 0)),
                  pl.BlockSpec((1, 3 * C), lambda b: (0, 0)),
                  pl.BlockSpec((1, C), lambda b: (0, 0))],
        out_specs=pl.BlockSpec((1, L, C), lambda b: (b, 0, 0)),
        out_shape=jax.ShapeDtypeStruct((B, L, C), F32),
        scratch_shapes=[pltpu.VMEM((3 * C // LANES, L + 2 * CONV_MARGIN, LANES), F32)],
        compiler_params=_cparams("parallel"),
        name="hyena_post",
    )(u, yt, w, bsh, lp["hy_bias"].reshape(1, C))


FFT_N2 = 128
FFT_UNROLL = 8


class _FftPlan:
    def __init__(self, L):
        self.L = L
        self.N = 2 * L
        self.N1 = self.N // FFT_N2
        self.KH = self.N1 // 2 + 1
        self.KP = -(-self.KH // 8) * 8
        self.PA = 2 * self.KP + 4


def fft_tables(L):
    p = _FftPlan(L)
    N, N1, KH, KP = p.N, p.N1, p.KH, p.KP
    n2 = jnp.arange(FFT_N2, dtype=jnp.int32)
    k1 = jnp.arange(KP, dtype=jnp.int32)
    n1 = jnp.arange(N1, dtype=jnp.int32)
    n = FFT_N2 * n1[None, None, :] + n2[:, None, None]
    ang = (2.0 * math.pi / N) * ((k1[None, :, None] * n) % N).astype(F32)
    keep = (k1 < KH)[None, :, None]
    g_re = jnp.where(keep, jnp.cos(ang), 0.0)
    g_im = jnp.where(keep, -jnp.sin(ang), 0.0)
    ga_full = jnp.concatenate([g_re, g_im], axis=1)
    ck = jnp.where((k1 == 0) | (k1 == N1 // 2), 1.0, 2.0) / N
    ga_inv = jnp.swapaxes(ga_full[:, :, :N1 // 2] * jnp.tile(ck, 2)[None, :, None], 1, 2)
    kk = jnp.arange(FFT_N2, dtype=jnp.int32)
    ang2 = (2.0 * math.pi / FFT_N2) * ((kk[:, None] * kk[None, :]) % FFT_N2).astype(F32)
    fr, fi = jnp.cos(ang2), -jnp.sin(ang2)
    fb = jnp.block([[fr, -fi], [fi, fr]])
    fb_inv = jnp.block([[fr, fi], [-fi, fr]])
    return dict(ga_half=ga_full[:, :, :N1 // 2].astype(BF16), ga_full=ga_full.astype(BF16),
                ga_inv=ga_inv.astype(BF16), fb=fb.astype(BF16), fb_inv=fb_inv.astype(BF16))


def _fft_stage_a(x_ref, ga_ref, s_ref, plan, n1_count):
    n_slabs = x_ref.shape[0]

    def body(n2, carry):
        xs = jnp.concatenate([x_ref[s, pl.ds(n2, n1_count, stride=FFT_N2), :] for s in range(n_slabs)], axis=-1)
        a = jnp.dot(ga_ref[n2], xs.astype(BF16), preferred_element_type=F32)
        for s in range(n_slabs):
            s_ref[s, pl.ds(n2 * plan.PA, 2 * plan.KP), :] = a[:, s * LANES:(s + 1) * LANES]
        return carry
    lax.fori_loop(0, FFT_N2, body, 0, unroll=FFT_UNROLL)


def _fft_load_k1(s_ref, k1, plan):
    n_slabs = s_ref.shape[0]
    re = jnp.concatenate([s_ref[s, pl.ds(k1, FFT_N2, stride=plan.PA), :] for s in range(n_slabs)], axis=-1)
    im = jnp.concatenate([s_ref[s, pl.ds(plan.KP + k1, FFT_N2, stride=plan.PA), :] for s in range(n_slabs)], axis=-1)
    return jnp.concatenate([re, im], axis=0).astype(BF16)


def _fft_filter_kernel(k_ref, ga_ref, fb_ref, h_ref, s_ref, *, plan):
    _fft_stage_a(k_ref, ga_ref, s_ref, plan, plan.N1)

    def body(k1, carry):
        h_ref[k1] = jnp.dot(fb_ref[...], _fft_load_k1(s_ref, k1, plan), preferred_element_type=F32).astype(BF16)
        return carry
    lax.fori_loop(0, plan.KH, body, 0)


def _fft_conv_kernel(z_ref, ga_ref, gi_ref, fb_ref, fbi_ref, h_ref, y_ref, s_ref, *, plan):
    zs = z_ref.at[0]
    ys = y_ref.at[0]
    n_slabs = zs.shape[0]
    half = FFT_N2
    _fft_stage_a(zs, ga_ref, s_ref, plan, plan.N1 // 2)

    def body_b(k1, carry):
        x = jnp.dot(fb_ref[...], _fft_load_k1(s_ref, k1, plan), preferred_element_type=F32)
        h = h_ref[k1].astype(F32)
        xr, xi, hr, hi = x[:half], x[half:], h[:half], h[half:]
        y = jnp.concatenate([xr * hr - xi * hi, xr * hi + xi * hr], axis=0).astype(BF16)
        b = jnp.dot(fbi_ref[...], y, preferred_element_type=F32)
        for s in range(n_slabs):
            s_ref[s, pl.ds(k1, FFT_N2, stride=plan.PA), :] = b[:half, s * LANES:(s + 1) * LANES]
            s_ref[s, pl.ds(plan.KP + k1, FFT_N2, stride=plan.PA), :] = b[half:, s * LANES:(s + 1) * LANES]
        return carry
    lax.fori_loop(0, plan.KH, body_b, 0, unroll=3)

    def body_a(n2, carry):
        b = jnp.concatenate([s_ref[s, pl.ds(n2 * plan.PA, 2 * plan.KP), :] for s in range(n_slabs)], axis=-1)
        y = jnp.dot(gi_ref[n2], b.astype(BF16), preferred_element_type=F32)
        for s in range(n_slabs):
            ys[s, pl.ds(n2, plan.N1 // 2, stride=FFT_N2), :] = y[:, s * LANES:(s + 1) * LANES]
        return carry
    lax.fori_loop(0, FFT_N2, body_a, 0, unroll=FFT_UNROLL)


def fft_filter_spectrum(k, tabs):
    n_slabs, N, _ = k.shape
    plan = _FftPlan(N // 2)
    C = n_slabs * LANES
    full = lambda shape: pl.BlockSpec(shape, lambda i: (0,) * len(shape))
    return pl.pallas_call(
        functools.partial(_fft_filter_kernel, plan=plan),
        grid=(1,),
        in_specs=[full((n_slabs, plan.N, LANES)), full((FFT_N2, 2 * plan.KP, plan.N1)),
                  full((2 * FFT_N2, 2 * FFT_N2))],
        out_specs=full((plan.KH, 2 * FFT_N2, C)),
        out_shape=jax.ShapeDtypeStruct((plan.KH, 2 * FFT_N2, C), BF16),
        scratch_shapes=[pltpu.VMEM((n_slabs, FFT_N2 * plan.PA, LANES), F32)],
        compiler_params=_cparams("arbitrary"),
        name="hyena_filter_fft",
    )(k, tabs["ga_full"], tabs["fb"])


def fft_long_conv(z, h_spec, tabs):
    B, n_slabs, L, _ = z.shape
    plan = _FftPlan(L)
    C = n_slabs * LANES
    full = lambda shape: pl.BlockSpec(shape, lambda b: (0,) * len(shape))
    seq = pl.BlockSpec((1, n_slabs, L, LANES), lambda b: (b, 0, 0, 0))
    return pl.pallas_call(
        functools.partial(_fft_conv_kernel, plan=plan),
        grid=(B,),
        in_specs=[seq, full((FFT_N2, 2 * plan.KP, plan.N1 // 2)), full((FFT_N2, plan.N1 // 2, 2 * plan.KP)),
                  full((2 * FFT_N2, 2 * FFT_N2)), full((2 * FFT_N2, 2 * FFT_N2)),
                  full((plan.KH, 2 * FFT_N2, C))],
        out_specs=seq,
        out_shape=jax.ShapeDtypeStruct((B, n_slabs, L, LANES), F32),
        scratch_shapes=[pltpu.VMEM((n_slabs, FFT_N2 * plan.PA, LANES), F32)],
        compiler_params=_cparams("parallel"),
        name="hyena_fft_conv",
    )(z, tabs["ga_half"], tabs["ga_inv"], tabs["fb"], tabs["fb_inv"], h_spec)


def _hyena_pre_slab_kernel(u_ref, w_ref, b_ref, z_ref, upad):
    L = u_ref.shape[1]
    T = _time_chunk(L)
    C = D_GROUP
    _fill_padded(upad, u_ref, L, T)

    def body(j, carry):
        base = pl.multiple_of(j * T, T)
        for s in range(C // LANES):
            x1 = _dw_conv_slab(upad, C // LANES + s, base, T, w_ref, b_ref, C + s * LANES, HY_SHORT, 1)
            v = _dw_conv_slab(upad, 2 * C // LANES + s, base, T, w_ref, b_ref, 2 * C + s * LANES, HY_SHORT, 1)
            z_ref[0, s, pl.ds(base, T), :] = x1 * v
        return carry
    lax.fori_loop(0, L // T, body, 0)


def _hyena_post_slab_kernel(u0_ref, z_ref, y_ref, w_ref, b_ref, bias_ref, o_ref, upad):
    L = u0_ref.shape[1]
    T = _time_chunk(L)
    C = D_GROUP
    _fill_padded(upad, u0_ref, L, T)

    def body(j, carry):
        base = pl.multiple_of(j * T, T)
        x0 = _short_conv(upad, base, T, w_ref, b_ref, 0, C)
        z = jnp.concatenate([z_ref[0, s, pl.ds(base, T), :] for s in range(C // LANES)], axis=-1)
        y = jnp.concatenate([y_ref[0, s, pl.ds(base, T), :] for s in range(C // LANES)], axis=-1)
        o_ref[0, pl.ds(base, T), :] = x0 * (y + z * bias_ref[...])
        return carry
    lax.fori_loop(0, L // T, body, 0)


def hyena_mixer_fft(u, lp, tabs):
    B, L, _ = u.shape
    C = D_GROUP
    n_slabs = C // LANES
    w, bsh = lp["hy_short_w"], lp["hy_short_b"].reshape(1, 3 * C)
    useq = pl.BlockSpec((1, L, 3 * C), lambda b: (b, 0, 0))
    slabs = pl.BlockSpec((1, n_slabs, L, LANES), lambda b: (b, 0, 0, 0))
    wspec = pl.BlockSpec((HY_SHORT, 3 * C), lambda b: (0, 0))
    bspec = pl.BlockSpec((1, 3 * C), lambda b: (0, 0))
    pad_scratch = pltpu.VMEM((3 * C // LANES, L + 2 * CONV_MARGIN, LANES), F32)
    z = pl.pallas_call(
        _hyena_pre_slab_kernel,
        grid=(B,),
        in_specs=[useq, wspec, bspec],
        out_specs=slabs,
        out_shape=jax.ShapeDtypeStruct((B, n_slabs, L, LANES), F32),
        scratch_shapes=[pad_scratch],
        compiler_params=_cparams("parallel"),
        name="hyena_pre",
    )(u, w, bsh)
    y = fft_long_conv(z, fft_filter_spectrum(hyena_filter(L, lp), tabs), tabs)
    return pl.pallas_call(
        _hyena_post_slab_kernel,
        grid=(B,),
        in_specs=[pl.BlockSpec((1, L, C), lambda b: (b, 0, 0)), slabs, slabs, wspec, bspec,
                  pl.BlockSpec((1, C), lambda b: (0, 0))],
        out_specs=pl.BlockSpec((1, L, C), lambda b: (b, 0, 0)),
        out_shape=jax.ShapeDtypeStruct((B, L, C), F32),
        scratch_shapes=[pltpu.VMEM((n_slabs, L + 2 * CONV_MARGIN, LANES), F32)],
        compiler_params=_cparams("parallel"),
        name="hyena_post",
    )(u, z, y, w, bsh, lp["hy_bias"].reshape(1, C))


def _filter_gen_kernel(z_ref, zr_ref, w1_ref, b1_ref, w2_ref, b2_ref, w3_ref, dl_ref, k_ref, nrm):
    L = z_ref.shape[0]
    T = min(L, 512)
    C = D_GROUP
    hi = lax.Precision.HIGHEST

    def decayed(zz):
        h1 = jnp.sin(jnp.dot(zz, w1_ref[...], preferred_element_type=F32, precision=hi) + b1_ref[...])
        h2 = jnp.sin(jnp.dot(h1, w2_ref[...], preferred_element_type=F32, precision=hi) + b2_ref[...])
        h = jnp.dot(h2, w3_ref[...], preferred_element_type=F32, precision=hi)
        decay = jnp.exp(-zz[:, 0:1] * dl_ref[...])
        return h[:, :C] * decay, h[:, C:] * decay

    nrm[...] = jnp.zeros_like(nrm)

    def accumulate(j, carry):
        base = pl.multiple_of(j * T, T)
        fwd, bwd = decayed(z_ref[pl.ds(base, T), :])
        nrm[...] = nrm[...] + jnp.sum(jnp.abs(fwd) + jnp.abs(bwd), axis=0, keepdims=True)
        return carry
    lax.fori_loop(0, L // T, accumulate, 0)

    inv = 1.0 / (nrm[...] + EPS)

    def emit(j, carry):
        base = pl.multiple_of(j * T, T)
        fwd, _ = decayed(z_ref[pl.ds(base, T), :])
        _, bwd_rev = decayed(zr_ref[pl.ds(base, T), :])
        row = base + lax.broadcasted_iota(jnp.int32, (T, 1), 0)
        k1 = fwd * inv
        k2 = jnp.where(row == 0, 0.0, bwd_rev * inv)
        for s in range(C // LANES):
            k_ref[s, pl.ds(base, T), :] = k1[:, s * LANES:(s + 1) * LANES]
            k_ref[s, pl.ds(L + base, T), :] = k2[:, s * LANES:(s + 1) * LANES]
        return carry
    lax.fori_loop(0, L // T, emit, 0)


def hyena_filter(L, lp):
    C = D_GROUP
    n_slabs = C // LANES
    t = jnp.linspace(0.0, 1.0, L, dtype=F32)[:, None]
    bands = (HY_EMB - 1) // 2
    w = 2.0 * math.pi * jnp.arange(L, dtype=F32)[:, None] / L
    f = jnp.linspace(1e-4, bands - 1, bands, dtype=F32)[None]
    z = jnp.concatenate([t, jnp.cos(f * w), -jnp.sin(f * w)], axis=-1)
    z = jnp.pad(z, ((0, 0), (0, LANES - HY_EMB)))
    z_rev = jnp.concatenate([z[:1], z[1:][::-1]], axis=0)
    w1 = jnp.pad(lp["hy_ffn_w1"], ((0, LANES - HY_EMB), (0, 0)))
    H = w1.shape[1]
    max_decay = math.log(HY_TARGET) / HY_FAST_DECAY
    min_decay = math.log(HY_TARGET) / HY_SLOW_DECAY
    abs_deltas = jnp.abs(jnp.linspace(min_decay, max_decay, C, dtype=F32)).reshape(1, C)
    full = lambda *shape: pl.BlockSpec(shape, lambda i: (0,) * len(shape))
    return pl.pallas_call(
        _filter_gen_kernel,
        grid=(1,),
        in_specs=[full(L, LANES), full(L, LANES), full(LANES, H), full(1, H), full(H, H), full(1, H),
                  full(H, 2 * C), full(1, C)],
        out_specs=full(n_slabs, 2 * L, LANES),
        out_shape=jax.ShapeDtypeStruct((n_slabs, 2 * L, LANES), F32),
        scratch_shapes=[pltpu.VMEM((1, C), F32)],
        compiler_params=_cparams("arbitrary"),
        name="hyena_filter_gen",
    )(z, z_rev, w1, lp["hy_ffn_b1"].reshape(1, H), lp["hy_ffn_w2"], lp["hy_ffn_b2"].reshape(1, H),
      lp["hy_ffn_w3"], abs_deltas)


def _layer(hc, hx, c_silu_all, lp, need_ctx, final_g, final_norm, tables_x, tables_c, experts, layer):
    B, S, D = hx.shape
    C = hc.shape[1]
    mod = small_linear(c_silu_all, lp["ada_w"], lp["ada_b"])
    mod_x = mod[:B].reshape(B, 6, 1, D)
    mod_c = jnp.broadcast_to(mod[B].reshape(1, 6, 1, D), (B, 6, 1, D))
    w_ext = extend_w_in(lp["w_in"])
    cos_x, sin_x = rope_tables(S, True)
    cos_c, sin_c = rope_tables(C, False)
    hy_x, cf_x, at_x, lr_x = in_proj(hx, mod_x[:, 0], mod_x[:, 1], lp["norm1_g"], w_ext, cos_x, sin_x, tm=512)
    hy_c, cf_c, at_c, lr_c = in_proj(hc, mod_c[:, 0], mod_c[:, 1], lp["norm1_g"], w_ext, cos_c, sin_c, tm=256)

    yd_c, yd_x = rglru_mixer(lr_c, lr_x, lp, need_ctx)
    conf = lambda u: conformer_conv(u, lp["conf_dw_w"], lp["conf_dw_b"], lp["conf_ln_g"], lp["conf_ln_b"])
    ys_x = [hyena_mixer_fft(hy_x, lp, tables_x), conf(cf_x),
            window_attention(at_x, at_c, lp["attn_sink"]), yd_x]

    w_out = lp["w_out"].astype(BF16)
    w_router = jnp.zeros((D, ROUTER_COLS), F32)
    w_router = w_router.at[:, :N_GROUPS].set(lp["router_g_w"]).at[:, N_GROUPS:N_GROUPS + N_EXPERTS].set(lp["router_e_w"])
    w_router = w_router.astype(BF16)
    b_router = jnp.zeros((1, ROUTER_COLS), F32)
    b_router = b_router.at[0, :N_GROUPS].set(lp["router_g_b"]).at[0, N_GROUPS:N_GROUPS + N_EXPERTS].set(lp["router_e_b"])

    hx1, lg_x = out_proj(ys_x, hx, mod_x[:, 2], lp["group_norm_g"], w_out, lp["norm2_g"],
                         mod_x[:, 3], mod_x[:, 4], w_router, b_router, tm=512)
    h_tok = hx1.reshape(B * S, D)
    hc_tok = None
    lg = lg_x.reshape(B * S, ROUTER_COLS)
    if need_ctx:
        ys_c = [hyena_mixer(hy_c, lp, tables_c), conf(cf_c),
                context_attention(at_c, lp["attn_sink"]), yd_c]
        hc1, lg_c = out_proj(ys_c, hc, mod_c[:, 2], lp["group_norm_g"], w_out, lp["norm2_g"],
                             mod_c[:, 3], mod_c[:, 4], w_router, b_router, tm=256)
        hc_tok = hc1.reshape(B * C, D)
        lg = jnp.concatenate([lg, lg_c.reshape(B * C, ROUTER_COLS)], axis=0)

    T = lg.shape[0]
    n_blocks = -(-T // MOE_BLOCK) + N_CLASSES
    info, counts, ids = route_tokens(lg)
    dest, blk_a, blk_b, n_used = slot_plan(ids, counts, n_blocks)
    xs = moe_dispatch(h_tok, hc_tok, dest, n_blocks, lp["norm2_g"], (mod_x[:, 3], mod_x[:, 4]),
                      (mod_c[:, 3], mod_c[:, 4]), S)
    o_sorted = expert_pairs(xs, blk_a, blk_b, n_used, *experts, layer)
    hx2 = moe_collect(o_sorted, dest, info, 0, h_tok, mod_x[:, 5], S, final_g, final_norm)
    hx2 = hx2.reshape(B, S, D)
    if need_ctx:
        hc2 = moe_collect(o_sorted, dest, info, B * S // COLLECT_TOKENS, hc_tok, mod_c[:1, 5], C,
                          final_g, False).reshape(B, C, D)
    else:
        hc2 = hc
    return hc2, hx2


def kernel(x, c, ctx, c_ctx, norm1_g, norm2_g, ada_w, ada_b, w_in, hy_short_w, hy_short_b, hy_ffn_w1, hy_ffn_b1, hy_ffn_w2, hy_ffn_b2, hy_ffn_w3, hy_bias, conf_dw_w, conf_dw_b, conf_ln_g, conf_ln_b, attn_sink, lru_conv_w, lru_conv_b, lru_wa, lru_ba, lru_wx, lru_bx, lru_lambda, group_norm_g, w_out, router_g_w, router_g_b, router_e_w, router_e_b, exp_w_gate, exp_w_up, exp_w_down, final_norm_g):
    stacked = dict(norm1_g=norm1_g, norm2_g=norm2_g, ada_w=ada_w, ada_b=ada_b, w_in=w_in,
                   hy_short_w=hy_short_w, hy_short_b=hy_short_b, hy_ffn_w1=hy_ffn_w1, hy_ffn_b1=hy_ffn_b1,
                   hy_ffn_w2=hy_ffn_w2, hy_ffn_b2=hy_ffn_b2, hy_ffn_w3=hy_ffn_w3, hy_bias=hy_bias,
                   conf_dw_w=conf_dw_w, conf_dw_b=conf_dw_b, conf_ln_g=conf_ln_g, conf_ln_b=conf_ln_b,
                   attn_sink=attn_sink, lru_conv_w=lru_conv_w, lru_conv_b=lru_conv_b, lru_wa=lru_wa,
                   lru_ba=lru_ba, lru_wx=lru_wx, lru_bx=lru_bx, lru_lambda=lru_lambda,
                   group_norm_g=group_norm_g, w_out=w_out, router_g_w=router_g_w, router_g_b=router_g_b,
                   router_e_w=router_e_w, router_e_b=router_e_b)
    experts = (exp_w_gate.astype(BF16), exp_w_up.astype(BF16), exp_w_down.astype(BF16))
    depth = norm1_g.shape[0]
    B = x.shape[0]
    cs = jnp.concatenate([jax.nn.silu(c), jnp.broadcast_to(jax.nn.silu(c_ctx)[None], (8, c.shape[1]))], axis=0)
    hc, hx = ctx, x
    tables_x = fft_tables(x.shape[1])
    tables_c = dft_tables(ctx.shape[1])
    for l in range(depth):
        lp = {k: v[l] for k, v in stacked.items()}
        hc, hx = _layer(hc, hx, cs, lp, need_ctx=(l < depth - 1), final_g=final_norm_g,
                        final_norm=(l == depth - 1), tables_x=tables_x, tables_c=tables_c,
                        experts=experts, layer=l)
    return hx
```

```python
import functools
import math

import jax
import jax.numpy as jnp
from jax import lax
from jax.experimental import pallas as pl
from jax.experimental.pallas import tpu as pltpu

F32 = jnp.float32
BF16 = jnp.bfloat16

EPS = 1e-6
NEG_INF = -1e30
GRID_W = 64
N_MIXERS = 4
D_GROUP = 256
HY_COLS = 3 * D_GROUP
CONF_COLS = 2 * D_GROUP
ATT_HEADS = 4
ATT_KV_HEADS = 2
HEAD_DIM = 64
ATT_COLS = (ATT_HEADS + 2 * ATT_KV_HEADS) * HEAD_DIM
LRU_COLS = 2 * D_GROUP
QK_COLS = (ATT_HEADS + ATT_KV_HEADS) * HEAD_DIM
WINDOW = 128
ATT_BLOCK = 128
ROPE_BASE = 10000.0
HY_EMB = 33
HY_FAST_DECAY = 0.3
HY_SLOW_DECAY = 1.5
HY_TARGET = 1e-2
CONF_KERNEL = 31
LRU_HEADS = 4
LRU_CONV = 4
LRU_C = 8.0
N_GROUPS = 4
EXP_PER_GROUP = 8
N_EXPERTS = N_GROUPS * EXP_PER_GROUP
TOP_K = 2
MOE_BLOCK = 256
ROUTER_COLS = 128

VMEM_LIMIT_BYTES = 56 * 1024 * 1024


def _cparams(*sem):
    return pltpu.CompilerParams(dimension_semantics=sem, vmem_limit_bytes=VMEM_LIMIT_BYTES)


def _linear_kernel(x_ref, w_ref, b_ref, o_ref):
    o_ref[...] = jnp.dot(x_ref[...], w_ref[...], preferred_element_type=F32,
                         precision=lax.Precision.HIGHEST) + b_ref[...]


def small_linear(x, w, b, tn=1024):
    M, K = x.shape
    N = w.shape[1]
    return pl.pallas_call(
        _linear_kernel,
        grid=(N // tn,),
        in_specs=[pl.BlockSpec((M, K), lambda j: (0, 0)),
                  pl.BlockSpec((K, tn), lambda j: (0, j)),
                  pl.BlockSpec((1, tn), lambda j: (0, j))],
        out_specs=pl.BlockSpec((M, tn), lambda j: (0, j)),
        out_shape=jax.ShapeDtypeStruct((M, N), F32),
        compiler_params=_cparams("parallel"),
        name="ada_linear",
    )(x, w, b.reshape(1, N))


def _in_proj_kernel(x_ref, sh_ref, sc_ref, g_ref, w_ref, cos_ref, sin_ref,
                    hy_ref, cf_ref, at_ref, lr_ref):
    x = x_ref[0]
    ms = jnp.mean(x * x, axis=-1, keepdims=True)
    y = x * lax.rsqrt(ms + EPS) * g_ref[...]
    y = y * (1.0 + sc_ref[0]) + sh_ref[0]
    u = jnp.dot(y.astype(BF16), w_ref[...], preferred_element_type=F32)
    c0 = HY_COLS
    c1 = c0 + CONF_COLS
    c2 = c1 + ATT_COLS
    c3 = c2 + LRU_COLS
    hy_ref[0] = u[:, :c0]
    cf_ref[0] = u[:, c0:c1]
    lr_ref[0] = u[:, c2:c3]
    qk = u[:, c1:c1 + QK_COLS]
    qk_rot = u[:, c3:c3 + QK_COLS]
    at_ref[0, :, :QK_COLS] = qk * cos_ref[...] + qk_rot * sin_ref[...]
    at_ref[0, :, QK_COLS:] = u[:, c1 + QK_COLS:c2]


def in_proj(h, shift, scale, g, w_ext, cos_t, sin_t, tm):
    B, L, D = h.shape
    NW = w_ext.shape[1]
    outs = [HY_COLS, CONF_COLS, ATT_COLS, LRU_COLS]
    return pl.pallas_call(
        _in_proj_kernel,
        grid=(B, L // tm),
        in_specs=[pl.BlockSpec((1, tm, D), lambda b, i: (b, i, 0)),
                  pl.BlockSpec((1, 1, D), lambda b, i: (b, 0, 0)),
                  pl.BlockSpec((1, 1, D), lambda b, i: (b, 0, 0)),
                  pl.BlockSpec((1, D), lambda b, i: (0, 0)),
                  pl.BlockSpec((D, NW), lambda b, i: (0, 0)),
                  pl.BlockSpec((tm, QK_COLS), lambda b, i: (i, 0)),
                  pl.BlockSpec((tm, QK_COLS), lambda b, i: (i, 0))],
        out_specs=[pl.BlockSpec((1, tm, n), lambda b, i: (b, i, 0)) for n in outs],
        out_shape=[jax.ShapeDtypeStruct((B, L, n), F32) for n in outs],
        compiler_params=_cparams("parallel", "parallel"),
        name="in_proj",
    )(h, shift, scale, g.reshape(1, D), w_ext, cos_t, sin_t)


def rope_tables(L, rotary):
    n_heads = ATT_HEADS + ATT_KV_HEADS
    if not rotary:
        return jnp.ones((L, QK_COLS), F32), jnp.zeros((L, QK_COLS), F32)
    pos = jnp.arange(L)
    row = (pos // GRID_W).astype(F32)
    col = (pos % GRID_W).astype(F32)
    half = HEAD_DIM // 2
    inv_freq = ROPE_BASE ** (-jnp.arange(0, half, 2, dtype=F32) / half)
    ang_r = row[:, None] * inv_freq[None]
    ang_c = col[:, None] * inv_freq[None]
    cos_h = jnp.concatenate([jnp.cos(ang_r)] * 2 + [jnp.cos(ang_c)] * 2, axis=-1)
    sin_h = jnp.concatenate([jnp.sin(ang_r)] * 2 + [jnp.sin(ang_c)] * 2, axis=-1)
    return jnp.tile(cos_h, (1, n_heads)), jnp.tile(sin_h, (1, n_heads))


def extend_w_in(w_in):
    c1 = HY_COLS + CONF_COLS
    wqk = w_in[:, c1:c1 + QK_COLS]
    D = w_in.shape[0]
    w4 = wqk.reshape(D, QK_COLS // 32, 2, 16)
    wrot = jnp.stack([-w4[:, :, 1], w4[:, :, 0]], axis=2).reshape(D, QK_COLS)
    return jnp.concatenate([w_in, wrot], axis=1).astype(BF16)


def _softmax_parts(q, k_list, extra_logit):
    scale = HEAD_DIM ** -0.5
    s_list = []
    for k, mask in k_list:
        s = lax.dot_general(q, k, (((1,), (1,)), ((), ())), preferred_element_type=F32) * scale
        if mask is not None:
            s = jnp.where(mask, s, NEG_INF)
        s_list.append(s)
    m = extra_logit
    for s in s_list:
        m = jnp.maximum(m, jnp.max(s, axis=-1, keepdims=True))
    p_list = [jnp.exp(s - m) for s in s_list]
    denom = jnp.exp(extra_logit - m)
    for p in p_list:
        denom = denom + jnp.sum(p, axis=-1, keepdims=True)
    return p_list, 1.0 / denom


ATT_Q_BLOCKS = 16


def _win_attn_kernel(sink_ref, q_ref, kp_ref, kc_ref, kn_ref, vp_ref, vc_ref, vn_ref,
                     kx_ref, vx_ref, o_ref, *, seq_len):
    i = pl.program_id(1)
    blk = ATT_BLOCK
    qb = q_ref.shape[1] // blk
    scale = HEAD_DIM ** -0.5
    g = ATT_HEADS // ATT_KV_HEADS
    kw = jnp.concatenate([kp_ref[0], kc_ref[0], kn_ref[0]], axis=0)
    vw = jnp.concatenate([vp_ref[0], vc_ref[0], vn_ref[0]], axis=0).astype(BF16)
    kwt = kw.T.astype(BF16)
    kxt = kx_ref[0].T.astype(BF16)
    vx = vx_ref[0].astype(BF16)
    row = lax.broadcasted_iota(jnp.int32, (g * blk, 3 * blk), 0) % blk
    col = lax.broadcasted_iota(jnp.int32, (g * blk, 3 * blk), 1)
    band_bias = jnp.where(jnp.abs(col - blk - row) <= WINDOW, 0.0, NEG_INF)
    col1 = lax.broadcasted_iota(jnp.int32, (1, 3 * blk), 1)
    for j in range(qb):
        q_blk = i * qb + j
        k_pos = (q_blk - 1) * blk + col1
        edge_bias = jnp.where(k_pos >= 0, jnp.where(k_pos < seq_len, 0.0, NEG_INF), NEG_INF)
        bias = band_bias + edge_bias
        outs = []
        for kv in range(ATT_KV_HEADS):
            ksl = slice(kv * HEAD_DIM, (kv + 1) * HEAD_DIM)
            heads = range(kv * g, (kv + 1) * g)
            qs = (jnp.concatenate([q_ref[0, j * blk:(j + 1) * blk, h * HEAD_DIM:(h + 1) * HEAD_DIM]
                                   for h in heads], axis=0) * scale).astype(BF16)
            sink = jnp.concatenate([jnp.full((blk, 1), sink_ref[h], F32) for h in heads], axis=0)
            s_win = jnp.dot(qs, kwt[ksl, j * blk:(j + 3) * blk], preferred_element_type=F32) + bias
            s_ctx = jnp.dot(qs, kxt[ksl, :], preferred_element_type=F32)
            m = jnp.maximum(jnp.maximum(jnp.max(s_win, axis=-1, keepdims=True),
                                        jnp.max(s_ctx, axis=-1, keepdims=True)), sink)
            p_win = jnp.exp(s_win - m)
            p_ctx = jnp.exp(s_ctx - m)
            denom = (jnp.exp(sink - m) + jnp.sum(p_win, axis=-1, keepdims=True)
                     + jnp.sum(p_ctx, axis=-1, keepdims=True))
            o = (jnp.dot(p_win.astype(BF16), vw[j * blk:(j + 3) * blk, ksl], preferred_element_type=F32)
                 + jnp.dot(p_ctx.astype(BF16), vx[:, ksl], preferred_element_type=F32)) * (1.0 / denom)
            outs.extend([o[k * blk:(k + 1) * blk] for k in range(g)])
        o_ref[0, j * blk:(j + 1) * blk, :] = jnp.concatenate(outs, axis=-1)


def window_attention(at_x, at_c, sink):
    B, S, _ = at_x.shape
    C = at_c.shape[1]
    blk = ATT_BLOCK
    qb = ATT_Q_BLOCKS
    nb = S // blk
    kcol = QK_COLS // 128 - 1
    vcol = kcol + 1

    def edge_spec(col, off):
        return pl.BlockSpec((1, blk, 128), lambda b, i, s: (b, jnp.clip(i * qb + off, 0, nb - 1), col))

    def mid_spec(col):
        return pl.BlockSpec((1, qb * blk, 128), lambda b, i, s: (b, i, col))

    grid_spec = pltpu.PrefetchScalarGridSpec(
        num_scalar_prefetch=1,
        grid=(B, nb // qb),
        in_specs=[pl.BlockSpec((1, qb * blk, ATT_HEADS * HEAD_DIM), lambda b, i, s: (b, i, 0)),
                  edge_spec(kcol, -1), mid_spec(kcol), edge_spec(kcol, qb),
                  edge_spec(vcol, -1), mid_spec(vcol), edge_spec(vcol, qb),
                  pl.BlockSpec((1, C, 128), lambda b, i, s: (b, 0, kcol)),
                  pl.BlockSpec((1, C, 128), lambda b, i, s: (b, 0, vcol))],
        out_specs=pl.BlockSpec((1, qb * blk, ATT_HEADS * HEAD_DIM), lambda b, i, s: (b, i, 0)),
    )
    return pl.pallas_call(
        functools.partial(_win_attn_kernel, seq_len=S),
        grid_spec=grid_spec,
        out_shape=jax.ShapeDtypeStruct((B, S, ATT_HEADS * HEAD_DIM), F32),
        compiler_params=_cparams("parallel", "parallel"),
        name="window_attention",
    )(sink.astype(F32), at_x, at_x, at_x, at_x, at_x, at_x, at_x, at_c, at_c)


def _ctx_attn_kernel(sink_ref, q_ref, kx_ref, vx_ref, o_ref):
    q = q_ref[0].astype(BF16)
    kx = kx_ref[0].astype(BF16)
    vx = vx_ref[0].astype(BF16)
    g = ATT_HEADS // ATT_KV_HEADS
    outs = []
    for h in range(ATT_HEADS):
        kv = h // g
        qs = q[:, h * HEAD_DIM:(h + 1) * HEAD_DIM]
        ksl = slice(kv * HEAD_DIM, (kv + 1) * HEAD_DIM)
        (p_ctx,), inv = _softmax_parts(qs, [(kx[:, ksl], None)], sink_ref[h])
        outs.append(jnp.dot(p_ctx.astype(BF16), vx[:, ksl], preferred_element_type=F32) * inv)
    o_ref[0] = jnp.concatenate(outs, axis=-1)


def context_attention(at_c, sink):
    B, C, _ = at_c.shape
    kcol = QK_COLS // 128 - 1
    grid_spec = pltpu.PrefetchScalarGridSpec(
        num_scalar_prefetch=1,
        grid=(B,),
        in_specs=[pl.BlockSpec((1, C, ATT_HEADS * HEAD_DIM), lambda b, s: (b, 0, 0)),
                  pl.BlockSpec((1, C, 128), lambda b, s: (b, 0, kcol)),
                  pl.BlockSpec((1, C, 128), lambda b, s: (b, 0, kcol + 1))],
        out_specs=pl.BlockSpec((1, C, ATT_HEADS * HEAD_DIM), lambda b, s: (b, 0, 0)),
    )
    return pl.pallas_call(
        _ctx_attn_kernel,
        grid_spec=grid_spec,
        out_shape=jax.ShapeDtypeStruct((B, C, ATT_HEADS * HEAD_DIM), F32),
        compiler_params=_cparams("parallel"),
        name="context_attention",
    )(sink.astype(F32), at_c, at_c, at_c)


def _out_proj_kernel(y0_ref, y1_ref, y2_ref, y3_ref, h_ref, g1_ref, gng_ref, w_ref,
                     n2g_ref, sh_ref, sc_ref, wr_ref, br_ref, ho_ref, lg_ref):
    parts = []
    for k, y_ref in enumerate((y0_ref, y1_ref, y2_ref, y3_ref)):
        y = y_ref[0]
        ms = jnp.mean(y * y, axis=-1, keepdims=True)
        yn = y * lax.rsqrt(ms + EPS) * gng_ref[:, k * D_GROUP:(k + 1) * D_GROUP]
        parts.append(yn.astype(BF16))
    yn = jnp.concatenate(parts, axis=-1)
    proj = jnp.dot(yn, w_ref[...], preferred_element_type=F32)
    h = h_ref[0] + g1_ref[0] * proj
    ho_ref[0] = h
    ms = jnp.mean(h * h, axis=-1, keepdims=True)
    n = h * lax.rsqrt(ms + EPS) * n2g_ref[...]
    n = n * (1.0 + sc_ref[0]) + sh_ref[0]
    lg_ref[0] = jnp.dot(n.astype(BF16), wr_ref[...], preferred_element_type=F32) + br_ref[...]


def out_proj(ys, h, g1, gng, w_out, n2g, sh2, sc2, w_router, b_router, tm):
    B, L, D = h.shape
    row3 = lambda n: pl.BlockSpec((1, tm, n), lambda b, i: (b, i, 0))
    mod = pl.BlockSpec((1, 1, D), lambda b, i: (b, 0, 0))
    full = lambda r, c: pl.BlockSpec((r, c), lambda b, i: (0, 0))
    return pl.pallas_call(
        _out_proj_kernel,
        grid=(B, L // tm),
        in_specs=[row3(D_GROUP)] * 4 + [row3(D), mod, full(1, D), full(D, D), full(1, D), mod, mod,
                                        full(D, ROUTER_COLS), full(1, ROUTER_COLS)],
        out_specs=[row3(D), row3(ROUTER_COLS)],
        out_shape=[jax.ShapeDtypeStruct((B, L, D), F32), jax.ShapeDtypeStruct((B, L, ROUTER_COLS), F32)],
        compiler_params=_cparams("parallel", "parallel"),
        name="out_proj",
    )(*ys, h, g1, gng.reshape(1, D), w_out, n2g.reshape(1, D), sh2, sc2, w_router, b_router)


N_PAIRS = EXP_PER_GROUP * (EXP_PER_GROUP - 1) // 2
N_CLASSES = N_GROUPS * N_PAIRS
ROUTE_TOKENS = 512
INFO_CLASS, INFO_RANK, INFO_WA, INFO_WB = 0, 1, 2, 3


SUBLANES = 8


def _route_kernel(lg_ref, below_ref, info_ref, cnt_ref, ids_ref, run):
    i = pl.program_id(0)

    @pl.when(i == 0)
    def _():
        run[...] = jnp.zeros_like(run)

    lg = lg_ref[...]
    li = lax.broadcasted_iota(jnp.int32, lg.shape, 1).astype(F32)
    big = float(ROUTER_COLS)

    def first_argmax(vals):
        m = jnp.max(vals, axis=-1, keepdims=True)
        return m, jnp.min(jnp.where(vals == m, li, big), axis=-1, keepdims=True)

    gl = jnp.where(li < N_GROUPS, lg, NEG_INF)
    gmax, g_idx = first_argmax(gl)
    g_prob = 1.0 / jnp.sum(jnp.exp(gl - gmax), axis=-1, keepdims=True)
    lo = N_GROUPS + EXP_PER_GROUP * g_idx
    el = jnp.where(li >= lo, jnp.where(li < lo + EXP_PER_GROUP, lg, NEG_INF), NEG_INF)
    m1, i1 = first_argmax(el)
    m2, i2 = first_argmax(jnp.where(li == i1, NEG_INF, el))
    e2 = jnp.exp(m2 - m1)
    w1 = g_prob / (1.0 + e2)
    w2 = g_prob * e2 / (1.0 + e2)
    j1 = i1 - lo
    j2 = i2 - lo
    a = jnp.minimum(j1, j2)
    b = jnp.maximum(j1, j2)
    cls = g_idx * N_PAIRS + (a * (2 * EXP_PER_GROUP - 1 - a)) * 0.5 + (b - a - 1.0)
    w_a = jnp.where(j1 < j2, w1, w2)
    w_b = jnp.where(j1 < j2, w2, w1)

    hit = li == cls
    onehot = jnp.where(hit, 1.0, 0.0)
    before = jnp.dot(below_ref[...], onehot.astype(BF16), preferred_element_type=F32)
    rank = jnp.sum(jnp.where(hit, before + run[...], 0.0), axis=-1, keepdims=True)
    run[...] = run[...] + jnp.sum(onehot, axis=0, keepdims=True)
    cnt_ref[...] = run[...]
    info = jnp.where(li == INFO_CLASS, cls, 0.0)
    info = jnp.where(li == INFO_RANK, rank, info)
    info = jnp.where(li == INFO_WA, w_a, info)
    info = jnp.where(li == INFO_WB, w_b, info)
    info_ref[...] = info
    ids_ref[0] = info.T[:SUBLANES].astype(jnp.int32)


def route_tokens(logits):
    T = logits.shape[0]
    tb = ROUTE_TOKENS
    below = (jnp.arange(tb)[None, :] < jnp.arange(tb)[:, None]).astype(BF16)
    return pl.pallas_call(
        _route_kernel,
        grid=(T // tb,),
        in_specs=[pl.BlockSpec((tb, ROUTER_COLS), lambda i: (i, 0)),
                  pl.BlockSpec((tb, tb), lambda i: (0, 0))],
        out_specs=[pl.BlockSpec((tb, ROUTER_COLS), lambda i: (i, 0)),
                   pl.BlockSpec((1, ROUTER_COLS), lambda i: (0, 0)),
                   pl.BlockSpec((1, SUBLANES, tb), lambda i: (i, 0, 0))],
        out_shape=[jax.ShapeDtypeStruct((T, ROUTER_COLS), F32), jax.ShapeDtypeStruct((1, ROUTER_COLS), F32),
                   jax.ShapeDtypeStruct((T // tb, SUBLANES, tb), jnp.int32)],
        scratch_shapes=[pltpu.VMEM((1, ROUTER_COLS), F32)],
        compiler_params=_cparams("arbitrary"),
        name="moe_route",
    )(logits, below)


def _pair_tables():
    a_tab, b_tab = [], []
    for g in range(N_GROUPS):
        for a in range(EXP_PER_GROUP):
            for b in range(a + 1, EXP_PER_GROUP):
                a_tab.append(g * EXP_PER_GROUP + a)
                b_tab.append(g * EXP_PER_GROUP + b)
    return jnp.array(a_tab, jnp.int32), jnp.array(b_tab, jnp.int32)


def _slot_kernel(ids_ref, start_ref, dest_ref):
    cls = ids_ref[0, INFO_CLASS:INFO_CLASS + 1, :]
    rank = ids_ref[0, INFO_RANK:INFO_RANK + 1, :]
    ci = lax.broadcasted_iota(jnp.int32, (ROUTER_COLS, cls.shape[1]), 0)
    start = jnp.sum(jnp.where(ci == cls, start_ref[...], 0), axis=0, keepdims=True)
    dest_ref[0] = jnp.broadcast_to(start + rank, dest_ref.shape[1:])


def slot_plan(ids, counts, n_blocks):
    nt, _, tb = ids.shape
    cnt = counts[0, :N_CLASSES].astype(jnp.int32)
    padded = (cnt + MOE_BLOCK - 1) // MOE_BLOCK * MOE_BLOCK
    upto = jnp.arange(N_CLASSES)[None, :] <= jnp.arange(N_CLASSES)[:, None]
    pad_end = jnp.sum(jnp.where(upto, padded[None, :], 0), axis=1)
    class_start = jnp.zeros((ROUTER_COLS, 1), jnp.int32).at[:N_CLASSES, 0].set(pad_end - padded)
    dest = pl.pallas_call(
        _slot_kernel,
        grid=(nt,),
        in_specs=[pl.BlockSpec((1, SUBLANES, tb), lambda i: (i, 0, 0)),
                  pl.BlockSpec((ROUTER_COLS, 1), lambda i: (0, 0))],
        out_specs=pl.BlockSpec((1, SUBLANES, tb), lambda i: (i, 0, 0)),
        out_shape=jax.ShapeDtypeStruct((nt, SUBLANES, tb), jnp.int32),
        compiler_params=_cparams("parallel"),
        name="moe_slots",
    )(ids, class_start)[:, 0, :].reshape(nt * tb)
    n_used = (pad_end[-1] // MOE_BLOCK).astype(jnp.int32).reshape(1)
    blk_first = jnp.arange(n_blocks, dtype=jnp.int32) * MOE_BLOCK
    blk_cls = jnp.minimum(jnp.sum((pad_end[None, :] <= blk_first[:, None]).astype(jnp.int32), axis=1),
                          N_CLASSES - 1)
    a_tab, b_tab = _pair_tables()
    hit = blk_cls[:, None] == jnp.arange(N_CLASSES)[None, :]
    pick = lambda tab: jnp.sum(jnp.where(hit, tab[None, :], 0), axis=1).astype(jnp.int32)
    return dest, pick(a_tab), pick(b_tab), n_used


DISPATCH_TOKENS = 1024
COLLECT_TOKENS = 512


def _wait_rows(buf, sem):
    pltpu.make_async_copy(buf, buf, sem).wait()


DMA_UNROLL = 8
TOKEN_TILE_ROWS = 8


def _store_token_tiles(tiles_ref, offset, pitch, x):
    n = x.shape[0]
    for j in range(x.shape[1] // LANES):
        tiles_ref[pl.ds(offset + j, n, stride=pitch), :] = x[:, j * LANES:(j + 1) * LANES]


def _load_token_tiles(tiles_ref, offset, pitch, n, width):
    return jnp.concatenate([tiles_ref[pl.ds(offset + j, n, stride=pitch), :] for j in range(width // LANES)],
                           axis=-1)


def _dispatch_kernel(dest_ref, hx_ref, hc_ref, g_ref, shx_ref, scx_ref, shc_ref, scc_ref, zeros_hbm,
                     xs_hbm, rows, sems, *, n_latent_blocks):
    del zeros_hbm
    i = pl.program_id(0)
    n = pl.num_programs(0)
    slot = i % 2
    tb = hx_ref.shape[0]

    @pl.when(i >= 2)
    def _():
        _wait_rows(rows.at[slot], sems.at[slot])

    def normed(h_ref, sh_ref, sc_ref):
        h = h_ref[...]
        ms = jnp.mean(h * h, axis=-1, keepdims=True)
        return h * lax.rsqrt(ms + EPS) * g_ref[...] * (1.0 + sc_ref[0]) + sh_ref[0]

    @pl.when(i < n_latent_blocks)
    def _():
        _store_token_tiles(rows.at[slot], 0, TOKEN_TILE_ROWS, normed(hx_ref, shx_ref, scx_ref))

    @pl.when(i >= n_latent_blocks)
    def _():
        _store_token_tiles(rows.at[slot], 0, TOKEN_TILE_ROWS, normed(hc_ref, shc_ref, scc_ref))

    def body(g, carry):
        for u in range(DMA_UNROLL):
            r = g * DMA_UNROLL + u
            dst = pl.multiple_of(dest_ref[0, 0, r] * TOKEN_TILE_ROWS, TOKEN_TILE_ROWS)
            pltpu.make_async_copy(rows.at[slot, pl.ds(r * TOKEN_TILE_ROWS, TOKEN_TILE_ROWS)],
                                  xs_hbm.at[pl.ds(dst, TOKEN_TILE_ROWS)], sems.at[slot]).start(priority=u % 2)
        return carry
    lax.fori_loop(0, tb // DMA_UNROLL, body, 0)

    @pl.when(i == n - 1)
    def _():
        _wait_rows(rows.at[slot], sems.at[slot])

        @pl.when(n >= 2)
        def _():
            _wait_rows(rows.at[1 - slot], sems.at[1 - slot])


def moe_dispatch(h_x, h_c, dest, n_blocks, n2g, mod_x, mod_c, tokens_per_batch):
    Tx, D = h_x.shape
    tb = DISPATCH_TOKENS
    nxb = Tx // tb
    if h_c is None:
        h_c, mod_c, ncb = h_x, mod_x, 0
    else:
        ncb = h_c.shape[0] // tb
    per_b = tokens_per_batch // tb
    P = n_blocks * MOE_BLOCK
    tile_rows = D // LANES
    assert tile_rows == TOKEN_TILE_ROWS
    xi = lambda i: jnp.minimum(i, nxb - 1)
    ci = lambda i: jnp.maximum(i - nxb, 0)
    modx = pl.BlockSpec((1, 1, D), lambda i: (xi(i) // per_b, 0, 0))
    modc = pl.BlockSpec((1, 1, D), lambda i: (0, 0, 0))
    return pl.pallas_call(
        functools.partial(_dispatch_kernel, n_latent_blocks=nxb),
        grid=(nxb + ncb,),
        in_specs=[pl.BlockSpec((1, 1, tb), lambda i: (i, 0, 0), memory_space=pltpu.SMEM),
                  pl.BlockSpec((tb, D), lambda i: (xi(i), 0)),
                  pl.BlockSpec((tb, D), lambda i: (ci(i), 0)),
                  pl.BlockSpec((1, D), lambda i: (0, 0)),
                  modx, modx, modc, modc,
                  pl.BlockSpec(memory_space=pl.ANY)],
        out_specs=pl.BlockSpec(memory_space=pl.ANY),
        out_shape=jax.ShapeDtypeStruct((P * tile_rows, LANES), F32),
        scratch_shapes=[pltpu.VMEM((2, tb * tile_rows, LANES), F32), pltpu.SemaphoreType.DMA((2,))],
        input_output_aliases={8: 0},
        compiler_params=_cparams("arbitrary"),
        name="moe_dispatch",
    )(dest.reshape(-1, 1, tb), h_x, h_c, n2g.reshape(1, D), mod_x[0], mod_x[1], mod_c[0], mod_c[1],
      jnp.zeros((P * tile_rows, LANES), F32))


def _expert_pair_kernel(ea_ref, eb_ref, nused_ref, xs_ref, wga_ref, wua_ref, wda_ref, wgb_ref, wub_ref, wdb_ref,
                        o_ref):
    del ea_ref, eb_ref
    i = pl.program_id(0)
    D = wga_ref.shape[2]

    @pl.when(i < nused_ref[0])
    def _():
        xb = _load_token_tiles(xs_ref, 0, TOKEN_TILE_ROWS, MOE_BLOCK, D).astype(BF16)
        halves = []
        for wg_ref, wu_ref, wd_ref in ((wga_ref, wua_ref, wda_ref), (wgb_ref, wub_ref, wdb_ref)):
            gate = jnp.dot(xb, wg_ref[0, 0], preferred_element_type=F32)
            up = jnp.dot(xb, wu_ref[0, 0], preferred_element_type=F32)
            hid = (gate * jax.nn.sigmoid(gate) * up).astype(BF16)
            out = jnp.dot(hid, wd_ref[0, 0], preferred_element_type=F32)
            halves.append(lax.bitcast_convert_type(out.astype(BF16).astype(F32), jnp.uint32))
        _store_token_tiles(o_ref, 0, TOKEN_TILE_ROWS, halves[0] | (halves[1] >> 16))

    @pl.when(i >= nused_ref[0])
    def _():
        o_ref[...] = jnp.zeros_like(o_ref)


def _unpack_pair(words):
    hi = lax.bitcast_convert_type(words & jnp.uint32(0xFFFF0000), F32)
    lo = lax.bitcast_convert_type(words << 16, F32)
    return hi, lo


def expert_pairs(xs, blk_a, blk_b, n_used, w_gate, w_up, w_down, layer):
    D, DE = w_gate.shape[2:]
    P = xs.shape[0] // TOKEN_TILE_ROWS
    n_blocks = P // MOE_BLOCK
    wspec = lambda shape, which: pl.BlockSpec(shape, lambda i, ea, eb, nu: (layer, (ea, eb)[which][i], 0, 0))
    grid_spec = pltpu.PrefetchScalarGridSpec(
        num_scalar_prefetch=3,
        grid=(n_blocks,),
        in_specs=[pl.BlockSpec((MOE_BLOCK * TOKEN_TILE_ROWS, LANES), lambda i, ea, eb, nu: (i, 0)),
                  wspec((1, 1, D, DE), 0), wspec((1, 1, D, DE), 0), wspec((1, 1, DE, D), 0),
                  wspec((1, 1, D, DE), 1), wspec((1, 1, D, DE), 1), wspec((1, 1, DE, D), 1)],
        out_specs=pl.BlockSpec((MOE_BLOCK * TOKEN_TILE_ROWS, LANES), lambda i, ea, eb, nu: (i, 0)),
    )
    return pl.pallas_call(
        _expert_pair_kernel,
        grid_spec=grid_spec,
        out_shape=jax.ShapeDtypeStruct((P * TOKEN_TILE_ROWS, LANES), jnp.uint32),
        compiler_params=_cparams("arbitrary"),
        name="moe_experts",
    )(blk_a, blk_b, n_used, xs, w_gate, w_up, w_down, w_gate, w_up, w_down)


def _gather_pairs(idx_ref, src_hbm, buf, sem, n_tokens):
    def body(g, carry):
        for u in range(DMA_UNROLL):
            r = g * DMA_UNROLL + u
            src = pl.multiple_of(idx_ref[0, 0, r] * TOKEN_TILE_ROWS, TOKEN_TILE_ROWS)
            pltpu.make_async_copy(src_hbm.at[pl.ds(src, TOKEN_TILE_ROWS)],
                                  buf.at[pl.ds(r * TOKEN_TILE_ROWS, TOKEN_TILE_ROWS)], sem).start(priority=u % 2)
        return carry
    lax.fori_loop(0, n_tokens // DMA_UNROLL, body, 0)


def _collect_kernel(dest_ref, dest_next_ref, o_hbm, info_ref, h_ref, g2_ref, fg_ref, out_ref, obuf, sems, *,
                    final_norm):
    i = pl.program_id(0)
    n = pl.num_programs(0)
    slot = i % 2
    tb, D = h_ref.shape

    @pl.when(i == 0)
    def _():
        _gather_pairs(dest_ref, o_hbm, obuf.at[0], sems.at[0], tb)

    @pl.when(i + 1 < n)
    def _():
        _gather_pairs(dest_next_ref, o_hbm, obuf.at[1 - slot], sems.at[1 - slot], tb)

    _wait_rows(obuf.at[slot], sems.at[slot])
    e_a, e_b = _unpack_pair(_load_token_tiles(obuf.at[slot], 0, TOKEN_TILE_ROWS, tb, D))
    m = info_ref[:, INFO_WA:INFO_WA + 1] * e_a + info_ref[:, INFO_WB:INFO_WB + 1] * e_b
    h = h_ref[...] + g2_ref[0] * m
    if final_norm:
        ms = jnp.mean(h * h, axis=-1, keepdims=True)
        h = h * lax.rsqrt(ms + EPS) * fg_ref[...]
    out_ref[...] = h


def moe_collect(o_sorted, dest, info, block_offset, h_tokens, g2, tokens_per_batch, final_g, final_norm):
    T, D = h_tokens.shape
    tb = COLLECT_TOKENS
    nt = T // tb
    if g2.shape[0] == 1:
        g2_index = lambda i: 0
    else:
        assert tokens_per_batch % tb == 0
        g2_index = lambda i: i // (tokens_per_batch // tb)
    last = block_offset + nt - 1
    dest3 = dest.reshape(-1, 1, tb)
    return pl.pallas_call(
        functools.partial(_collect_kernel, final_norm=final_norm),
        grid=(nt,),
        in_specs=[pl.BlockSpec((1, 1, tb), lambda i: (block_offset + i, 0, 0), memory_space=pltpu.SMEM),
                  pl.BlockSpec((1, 1, tb), lambda i: (jnp.minimum(block_offset + i + 1, last), 0, 0),
                               memory_space=pltpu.SMEM),
                  pl.BlockSpec(memory_space=pl.ANY),
                  pl.BlockSpec((tb, ROUTER_COLS), lambda i: (block_offset + i, 0)),
                  pl.BlockSpec((tb, D), lambda i: (i, 0)),
                  pl.BlockSpec((1, 1, D), lambda i: (g2_index(i), 0, 0)),
                  pl.BlockSpec((1, D), lambda i: (0, 0))],
        out_specs=pl.BlockSpec((tb, D), lambda i: (i, 0)),
        out_shape=jax.ShapeDtypeStruct((T, D), F32),
        scratch_shapes=[pltpu.VMEM((2, tb * TOKEN_TILE_ROWS, LANES), jnp.uint32), pltpu.SemaphoreType.DMA((2,))],
        compiler_params=_cparams("arbitrary"),
        name="moe_collect",
    )(dest3, dest3, o_sorted, info, h_tokens, g2, final_g.reshape(1, D))


CONV_MARGIN = 16


def _time_chunk(L):
    return min(L, 256)


LANES = 128


def _zero_margins(pad_ref, L):
    zeros = jnp.zeros((CONV_MARGIN, LANES), F32)
    for s in range(pad_ref.shape[0]):
        pad_ref[s, pl.ds(0, CONV_MARGIN), :] = zeros
        pad_ref[s, pl.ds(CONV_MARGIN + L, CONV_MARGIN), :] = zeros


def _dw_conv_slab(pad_ref, s, base, T, w_ref, b_ref, col, taps, pad_left):
    acc = jnp.broadcast_to(b_ref[:, col:col + LANES], (T, LANES))
    for k in range(taps):
        acc = acc + w_ref[k:k + 1, col:col + LANES] * pad_ref[s, pl.ds(base + (CONV_MARGIN - pad_left + k), T), :]
    return acc


def _conformer_kernel(u_ref, w_ref, b_ref, g_ref, beta_ref, o_ref, ypad):
    L = o_ref.shape[1]
    T = _time_chunk(L)
    C = D_GROUP
    n_slabs = C // LANES
    pad = (CONF_KERNEL - 1) // 2
    _zero_margins(ypad, L)

    def glu(j, carry):
        base = pl.multiple_of(j * T, T)
        for s in range(n_slabs):
            a = u_ref[0, pl.ds(base, T), s * LANES:(s + 1) * LANES]
            gate = u_ref[0, pl.ds(base, T), C + s * LANES:C + (s + 1) * LANES]
            ypad[s, pl.ds(CONV_MARGIN + base, T), :] = a * jax.nn.sigmoid(gate)
        return carry
    lax.fori_loop(0, L // T, glu, 0)

    def conv(j, carry):
        base = pl.multiple_of(j * T, T)
        acc = jnp.concatenate([_dw_conv_slab(ypad, s, base, T, w_ref, b_ref, s * LANES, CONF_KERNEL, pad)
                               for s in range(n_slabs)], axis=-1)
        mu = jnp.mean(acc, axis=-1, keepdims=True)
        cen = acc - mu
        var = jnp.mean(cen * cen, axis=-1, keepdims=True)
        y = cen * lax.rsqrt(var + EPS) * g_ref[...] + beta_ref[...]
        o_ref[0, pl.ds(base, T), :] = y * jax.nn.sigmoid(y)
        return carry
    lax.fori_loop(0, L // T, conv, 0)


def conformer_conv(u, w, b, ln_g, ln_b):
    B, L, _ = u.shape
    C = D_GROUP
    vec = pl.BlockSpec((1, C), lambda i: (0, 0))
    return pl.pallas_call(
        _conformer_kernel,
        grid=(B,),
        in_specs=[pl.BlockSpec((1, L, 2 * C), lambda i: (i, 0, 0)),
                  pl.BlockSpec((CONF_KERNEL, C), lambda i: (0, 0)), vec, vec, vec],
        out_specs=pl.BlockSpec((1, L, C), lambda i: (i, 0, 0)),
        out_shape=jax.ShapeDtypeStruct((B, L, C), F32),
        scratch_shapes=[pltpu.VMEM((C // LANES, L + 2 * CONV_MARGIN, LANES), F32)],
        compiler_params=_cparams("parallel"),
        name="conformer_conv",
    )(u, w, b.reshape(1, C), ln_g.reshape(1, C), ln_b.reshape(1, C))


def _gelu_tanh(x):
    return 0.5 * x * (1.0 + jnp.tanh(math.sqrt(2.0 / math.pi) * (x + 0.044715 * (x * x * x))))


def _lru_kernel(uc_ref, ux_ref, cw_ref, cb_ref, wcat_ref, bcat_ref, lam_ref, *rest, need_ctx):
    if need_ctx:
        oc_ref, ox_ref, cpad, xpad, a_s, b_s, yx, yc = rest
    else:
        ox_ref, cpad, xpad, a_s, b_s, yx = rest
        oc_ref = yc = None
    C = D_GROUP
    n_slabs = C // LANES
    Lc = uc_ref.shape[1]
    Lx = ux_ref.shape[1]
    pad_l = (LRU_CONV - 1) // 2

    def fill(pad_ref, u_ref, L):
        T = _time_chunk(L)
        _zero_margins(pad_ref, L)

        def body(j, carry):
            base = pl.multiple_of(j * T, T)
            for s in range(n_slabs):
                pad_ref[s, pl.ds(CONV_MARGIN + base, T), :] = u_ref[0, pl.ds(base, T),
                                                                    C + s * LANES:C + (s + 1) * LANES]
            return carry
        lax.fori_loop(0, L // T, body, 0)

    fill(cpad, uc_ref, Lc)
    fill(xpad, ux_ref, Lx)

    def coeffs(pad_ref, base, T, d):
        x = jnp.concatenate([_dw_conv_slab(pad_ref, s, base, T, cw_ref, cb_ref, s * LANES, LRU_CONV, pad_l)
                             for s in range(n_slabs)], axis=-1)
        t = jnp.tanh(jnp.dot(x.astype(BF16), wcat_ref[:, 2 * d * C:2 * (d + 1) * C],
                             preferred_element_type=F32) + bcat_ref[:, 2 * d * C:2 * (d + 1) * C])
        i = 0.5 * t[:, C:] + 0.5
        z = -lam_ref[d:d + 1, :]
        softplus = jnp.maximum(z, 0.0) + jnp.log(1.0 + jnp.exp(-jnp.abs(z)))
        half_rate = (-0.5 * LRU_C) * softplus
        a = jnp.exp(half_rate * t[:, :C] + half_rate)
        b = jnp.sqrt(1.0 - a * a) * (i * x)
        for s in range(n_slabs):
            a_s[d * n_slabs + s, pl.ds(0, T), :] = a[:, s * LANES:(s + 1) * LANES]
            b_s[d * n_slabs + s, pl.ds(0, T), :] = b[:, s * LANES:(s + 1) * LANES]

    def run(pad_ref, L, h, y_ref):
        T = _time_chunk(L)
        n = L // T

        def chunk(j, h):
            base_f = pl.multiple_of(j * T, T)
            base_b = pl.multiple_of((n - 1 - j) * T, T)
            coeffs(pad_ref, base_f, T, 0)
            coeffs(pad_ref, base_b, T, 1)

            def step(t, h):
                new = []
                for d, (base, row) in enumerate(((base_f, t), (base_b, T - 1 - t))):
                    for s in range(n_slabs):
                        k = d * n_slabs + s
                        hs = a_s[k, pl.ds(row, 1), :] * h[k] + b_s[k, pl.ds(row, 1), :]
                        if y_ref is not None:
                            y_ref[k, pl.ds(base + row, 1), :] = hs
                        new.append(hs)
                return tuple(new)
            return lax.fori_loop(0, T, step, h, unroll=8)
        return lax.fori_loop(0, n, chunk, h)

    h = tuple(jnp.zeros((1, LANES), F32) for _ in range(2 * n_slabs))
    h = run(cpad, Lc, h, yc)
    run(xpad, Lx, h, yx)

    def finish(u_ref, y_ref, o_ref, L):
        T = _time_chunk(L)

        def body(j, carry):
            base = pl.multiple_of(j * T, T)
            y = jnp.concatenate([y_ref[s, pl.ds(base, T), :] + y_ref[n_slabs + s, pl.ds(base, T), :]
                                 for s in range(n_slabs)], axis=-1)
            o_ref[0, pl.ds(base, T), :] = _gelu_tanh(u_ref[0, pl.ds(base, T), :C]) * y
            return carry
        lax.fori_loop(0, L // T, body, 0)

    finish(ux_ref, yx, ox_ref, Lx)
    if need_ctx:
        finish(uc_ref, yc, oc_ref, Lc)


def _block_diag(w):
    H, n, _ = w.shape
    eye = jnp.eye(H, dtype=w.dtype)
    return (eye[:, None, :, None] * w[:, :, None, :]).reshape(H * n, H * n)


def rglru_mixer(uc, ux, lp, need_ctx):
    B, Lc, _ = uc.shape
    Lx = ux.shape[1]
    C = D_GROUP
    wcat = (0.5 * jnp.concatenate([_block_diag(lp["lru_wa"][0]), _block_diag(lp["lru_wx"][0]),
                                   _block_diag(lp["lru_wa"][1]), _block_diag(lp["lru_wx"][1])], axis=1)).astype(BF16)
    bcat = 0.5 * jnp.concatenate([lp["lru_ba"][0], lp["lru_bx"][0], lp["lru_ba"][1], lp["lru_bx"][1]]).reshape(1, 4 * C)
    full = lambda r, c: pl.BlockSpec((r, c), lambda i: (0, 0))
    seq = lambda L, n: pl.BlockSpec((1, L, n), lambda i: (i, 0, 0))
    out_specs = [seq(Lx, C)]
    out_shape = [jax.ShapeDtypeStruct((B, Lx, C), F32)]
    if need_ctx:
        out_specs = [seq(Lc, C)] + out_specs
        out_shape = [jax.ShapeDtypeStruct((B, Lc, C), F32)] + out_shape
    T = _time_chunk(Lx)
    slab = lambda rows, n=1: pltpu.VMEM((n * C // LANES, rows, LANES), F32)
    scratch = [slab(Lc + 2 * CONV_MARGIN), slab(Lx + 2 * CONV_MARGIN), slab(T, 2), slab(T, 2), slab(Lx, 2)]
    if need_ctx:
        scratch.append(slab(Lc, 2))
    res = pl.pallas_call(
        functools.partial(_lru_kernel, need_ctx=need_ctx),
        grid=(B,),
        in_specs=[seq(Lc, 2 * C), seq(Lx, 2 * C), full(LRU_CONV, C), full(1, C), full(C, 4 * C),
                  full(1, 4 * C), full(2, C)],
        out_specs=out_specs,
        out_shape=out_shape,
        scratch_shapes=scratch,
        compiler_params=_cparams("parallel"),
        name="rglru",
    )(uc, ux, lp["lru_conv_w"], lp["lru_conv_b"].reshape(1, C), wcat, bcat, lp["lru_lambda"])
    if need_ctx:
        return res[0], res[1]
    return None, res[0]


HY_SHORT = 3


def _short_conv(pad_ref, base, T, w_ref, b_ref, c0, c1):
    return jnp.concatenate([_dw_conv_slab(pad_ref, col // LANES, base, T, w_ref, b_ref, col, HY_SHORT, 1)
                            for col in range(c0, c1, LANES)], axis=-1)


def _fill_padded(pad_ref, u_ref, L, T):
    _zero_margins(pad_ref, L)

    def body(j, carry):
        base = pl.multiple_of(j * T, T)
        for s in range(pad_ref.shape[0]):
            pad_ref[s, pl.ds(CONV_MARGIN + base, T), :] = u_ref[0, pl.ds(base, T), s * LANES:(s + 1) * LANES]
        return carry
    lax.fori_loop(0, L // T, body, 0)


def _hyena_pre_kernel(u_ref, w_ref, b_ref, z_ref, upad):
    L = u_ref.shape[1]
    T = _time_chunk(L)
    C = D_GROUP
    _fill_padded(upad, u_ref, L, T)

    def body(j, carry):
        base = pl.multiple_of(j * T, T)
        x1 = _short_conv(upad, base, T, w_ref, b_ref, C, 2 * C)
        v = _short_conv(upad, base, T, w_ref, b_ref, 2 * C, 3 * C)
        z_ref[pl.ds(base, T), :] = (x1 * v).astype(BF16)
        return carry
    lax.fori_loop(0, L // T, body, 0)


def _hyena_post_kernel(u_ref, y_ref, w_ref, b_ref, bias_ref, o_ref, upad):
    L = u_ref.shape[1]
    T = _time_chunk(L)
    C = D_GROUP
    _fill_padded(upad, u_ref, L, T)

    def body(j, carry):
        base = pl.multiple_of(j * T, T)
        x0 = _short_conv(upad, base, T, w_ref, b_ref, 0, C)
        x1 = _short_conv(upad, base, T, w_ref, b_ref, C, 2 * C)
        v = _short_conv(upad, base, T, w_ref, b_ref, 2 * C, 3 * C)
        o_ref[0, pl.ds(base, T), :] = x0 * (y_ref[pl.ds(base, T), :] + (x1 * v) * bias_ref[...])
        return carry
    lax.fori_loop(0, L // T, body, 0)


def _spectrum_kernel(f_ref, z_ref, ha_ref, hb_ref, hc_ref, y_ref):
    tf = ha_ref.shape[0]
    acc = jnp.dot(f_ref[...], z_ref[...], preferred_element_type=F32)
    zr = acc[:tf]
    zi = acc[tf:]
    y_ref[:tf, :] = (zr * ha_ref[...] - zi * hb_ref[...]).astype(BF16)
    y_ref[tf:, :] = (zr * hb_ref[...] + zi * hc_ref[...]).astype(BF16)


def _idft_kernel(f_ref, y_ref, o_ref):
    o_ref[...] = jnp.dot(f_ref[...], y_ref[...], preferred_element_type=F32)


def dft_tables(L):
    N = 2 * L
    tf = min(256, L)
    k = jnp.arange(L, dtype=jnp.int32)
    n = jnp.arange(L, dtype=jnp.int32)
    ang = (2.0 * math.pi / N) * ((k[:, None] * n[None, :]) % N).astype(F32)
    cos = jnp.cos(ang)
    sin = jnp.sin(ang)
    nyq = jnp.where(n % 2 == 0, 1.0, -1.0).astype(F32)
    f_re = cos
    f_im = (-sin).at[0].set(nyq)
    fwd = jnp.stack([f_re.reshape(L // tf, tf, L), f_im.reshape(L // tf, tf, L)], axis=1).reshape(N, L)
    ck = jnp.where(k == 0, 1.0, 2.0).astype(F32)[:, None] / N
    i_re = cos * ck
    i_im = (-sin * ck).at[0].set(nyq / N)
    inv = jnp.stack([i_re.reshape(L // tf, tf, L), i_im.reshape(L // tf, tf, L)], axis=1).reshape(N, L).T
    return fwd.astype(BF16), inv.astype(BF16)


def filter_spectrum(k_slabs, fwd):
    n_slabs, N, _ = k_slabs.shape
    L = N // 2
    C = n_slabs * LANES
    tf = min(256, L)
    halves = jnp.concatenate([k_slabs[s, h * L:(h + 1) * L] for h in range(2) for s in range(n_slabs)], axis=1)
    r = pl.pallas_call(
        _idft_kernel,
        grid=(N // (2 * tf), 1),
        in_specs=[pl.BlockSpec((2 * tf, L), lambda i, j: (i, 0)),
                  pl.BlockSpec((L, 2 * C), lambda i, j: (0, 0))],
        out_specs=pl.BlockSpec((2 * tf, 2 * C), lambda i, j: (i, 0)),
        out_shape=jax.ShapeDtypeStruct((N, 2 * C), F32),
        compiler_params=_cparams("parallel", "parallel"),
        name="hyena_filter_dft",
    )(fwd, halves.astype(BF16)).reshape(L // tf, 2, tf, 2 * C)
    sign = jnp.where(jnp.arange(L) % 2 == 0, 1.0, -1.0).astype(F32)[:, None]
    re = r[:, 0].reshape(L, 2 * C)
    im = r[:, 1].reshape(L, 2 * C)
    hr = re[:, :C] + sign * re[:, C:]
    hi = im[:, :C] + sign * im[:, C:]
    return hr, hi.at[0].set(0.0), hr.at[0].set(hi[0])


def hyena_mixer(u, lp, tables):
    B, L, _ = u.shape
    C = D_GROUP
    N = 2 * L
    fwd, inv = tables
    tf = min(256, L)
    T = _time_chunk(L)
    w, bsh = lp["hy_short_w"], lp["hy_short_b"].reshape(1, 3 * C)
    z2 = pl.pallas_call(
        _hyena_pre_kernel,
        grid=(B,),
        in_specs=[pl.BlockSpec((1, L, 3 * C), lambda b: (b, 0, 0)),
                  pl.BlockSpec((HY_SHORT, 3 * C), lambda b: (0, 0)),
                  pl.BlockSpec((1, 3 * C), lambda b: (0, 0))],
        out_specs=pl.BlockSpec((L, C), lambda b: (0, b)),
        out_shape=jax.ShapeDtypeStruct((L, B * C), BF16),
        scratch_shapes=[pltpu.VMEM((3 * C // LANES, L + 2 * CONV_MARGIN, LANES), F32)],
        compiler_params=_cparams("parallel"),
        name="hyena_pre",
    )(u, w, bsh)

    tn = 2 * C
    ha, hb, hc = [jnp.tile(t, (1, tn // C)) for t in filter_spectrum(hyena_filter(L, lp), fwd)]
    hspec = pl.BlockSpec((tf, tn), lambda i, j: (i, 0))
    y2 = pl.pallas_call(
        _spectrum_kernel,
        grid=(L // tf, B * C // tn),
        in_specs=[pl.BlockSpec((2 * tf, L), lambda i, j: (i, 0)),
                  pl.BlockSpec((L, tn), lambda i, j: (0, j)), hspec, hspec, hspec],
        out_specs=pl.BlockSpec((2 * tf, tn), lambda i, j: (i, j)),
        out_shape=jax.ShapeDtypeStruct((N, B * C), BF16),
        compiler_params=_cparams("parallel", "parallel"),
        name="hyena_spectrum",
    )(fwd, z2, ha, hb, hc)

    tl = min(256, L)
    yt = pl.pallas_call(
        _idft_kernel,
        grid=(L // tl, B * C // tn),
        in_specs=[pl.BlockSpec((tl, N), lambda i, j: (i, 0)),
                  pl.BlockSpec((N, tn), lambda i, j: (0, j))],
        out_specs=pl.BlockSpec((tl, tn), lambda i, j: (i, j)),
        out_shape=jax.ShapeDtypeStruct((L, B * C), F32),
        compiler_params=_cparams("parallel", "parallel"),
        name="hyena_idft",
    )(inv, y2)

    return pl.pallas_call(
        _hyena_post_kernel,
        grid=(B,),
        in_specs=[pl.BlockSpec((1, L, 3 * C), lambda b: (b, 0, 0)),
                  pl.BlockSpec((L, C), lambda b: (0, b)),
                  pl.BlockSpec((HY_SHORT, 3 * C), lambda b: (0, 0)),
                  pl.BlockSpec((1, 3 * C), lambda b: (0, 0)),
                  pl.BlockSpec((1, C), lambda b: (0, 0))],
        out_specs=pl.BlockSpec((1, L, C), lambda b: (b, 0, 0)),
        out_shape=jax.ShapeDtypeStruct((B, L, C), F32),
        scratch_shapes=[pltpu.VMEM((3 * C // LANES, L + 2 * CONV_MARGIN, LANES), F32)],
        compiler_params=_cparams("parallel"),
        name="hyena_post",
    )(u, yt, w, bsh, lp["hy_bias"].reshape(1, C))


FFT_N2 = 128
FFT_UNROLL = 8


class _FftPlan:
    def __init__(self, L):
        self.L = L
        self.N = 2 * L
        self.N1 = self.N // FFT_N2
        self.KH = self.N1 // 2 + 1
        self.KP = -(-self.KH // 8) * 8
        self.PA = 2 * self.KP + 4


def fft_tables(L):
    p = _FftPlan(L)
    N, N1, KH, KP = p.N, p.N1, p.KH, p.KP
    n2 = jnp.arange(FFT_N2, dtype=jnp.int32)
    k1 = jnp.arange(KP, dtype=jnp.int32)
    n1 = jnp.arange(N1, dtype=jnp.int32)
    n = FFT_N2 * n1[None, None, :] + n2[:, None, None]
    ang = (2.0 * math.pi / N) * ((k1[None, :, None] * n) % N).astype(F32)
    keep = (k1 < KH)[None, :, None]
    g_re = jnp.where(keep, jnp.cos(ang), 0.0)
    g_im = jnp.where(keep, -jnp.sin(ang), 0.0)
    ga_full = jnp.concatenate([g_re, g_im], axis=1)
    ck = jnp.where((k1 == 0) | (k1 == N1 // 2), 1.0, 2.0) / N
    ga_inv = jnp.swapaxes(ga_full[:, :, :N1 // 2] * jnp.tile(ck, 2)[None, :, None], 1, 2)
    kk = jnp.arange(FFT_N2, dtype=jnp.int32)
    ang2 = (2.0 * math.pi / FFT_N2) * ((kk[:, None] * kk[None, :]) % FFT_N2).astype(F32)
    fr, fi = jnp.cos(ang2), -jnp.sin(ang2)
    fb = jnp.block([[fr, -fi], [fi, fr]])
    fb_inv = jnp.block([[fr, fi], [-fi, fr]])
    return dict(ga_half=ga_full[:, :, :N1 // 2].astype(BF16), ga_full=ga_full.astype(BF16),
                ga_inv=ga_inv.astype(BF16), fb=fb.astype(BF16), fb_inv=fb_inv.astype(BF16))


def _fft_stage_a(x_ref, ga_ref, s_ref, plan, n1_count):
    n_slabs = x_ref.shape[0]

    def body(n2, carry):
        xs = jnp.concatenate([x_ref[s, pl.ds(n2, n1_count, stride=FFT_N2), :] for s in range(n_slabs)], axis=-1)
        a = jnp.dot(ga_ref[n2], xs.astype(BF16), preferred_element_type=F32)
        for s in range(n_slabs):
            s_ref[s, pl.ds(n2 * plan.PA, 2 * plan.KP), :] = a[:, s * LANES:(s + 1) * LANES]
        return carry
    lax.fori_loop(0, FFT_N2, body, 0, unroll=FFT_UNROLL)


def _fft_load_k1(s_ref, k1, plan):
    n_slabs = s_ref.shape[0]
    re = jnp.concatenate([s_ref[s, pl.ds(k1, FFT_N2, stride=plan.PA), :] for s in range(n_slabs)], axis=-1)
    im = jnp.concatenate([s_ref[s, pl.ds(plan.KP + k1, FFT_N2, stride=plan.PA), :] for s in range(n_slabs)], axis=-1)
    return jnp.concatenate([re, im], axis=0).astype(BF16)


def _fft_filter_kernel(k_ref, ga_ref, fb_ref, h_ref, s_ref, *, plan):
    _fft_stage_a(k_ref, ga_ref, s_ref, plan, plan.N1)

    def body(k1, carry):
        h_ref[k1] = jnp.dot(fb_ref[...], _fft_load_k1(s_ref, k1, plan), preferred_element_type=F32).astype(BF16)
        return carry
    lax.fori_loop(0, plan.KH, body, 0)


def _fft_conv_kernel(z_ref, ga_ref, gi_ref, fb_ref, fbi_ref, h_ref, y_ref, s_ref, *, plan):
    zs = z_ref.at[0]
    ys = y_ref.at[0]
    n_slabs = zs.shape[0]
    half = FFT_N2
    _fft_stage_a(zs, ga_ref, s_ref, plan, plan.N1 // 2)

    def body_b(k1, carry):
        x = jnp.dot(fb_ref[...], _fft_load_k1(s_ref, k1, plan), preferred_element_type=F32)
        h = h_ref[k1].astype(F32)
        xr, xi, hr, hi = x[:half], x[half:], h[:half], h[half:]
        y = jnp.concatenate([xr * hr - xi * hi, xr * hi + xi * hr], axis=0).astype(BF16)
        b = jnp.dot(fbi_ref[...], y, preferred_element_type=F32)
        for s in range(n_slabs):
            s_ref[s, pl.ds(k1, FFT_N2, stride=plan.PA), :] = b[:half, s * LANES:(s + 1) * LANES]
            s_ref[s, pl.ds(plan.KP + k1, FFT_N2, stride=plan.PA), :] = b[half:, s * LANES:(s + 1) * LANES]
        return carry
    lax.fori_loop(0, plan.KH, body_b, 0, unroll=3)

    def body_a(n2, carry):
        b = jnp.concatenate([s_ref[s, pl.ds(n2 * plan.PA, 2 * plan.KP), :] for s in range(n_slabs)], axis=-1)
        y = jnp.dot(gi_ref[n2], b.astype(BF16), preferred_element_type=F32)
        for s in range(n_slabs):
            ys[s, pl.ds(n2, plan.N1 // 2, stride=FFT_N2), :] = y[:, s * LANES:(s + 1) * LANES]
        return carry
    lax.fori_loop(0, FFT_N2, body_a, 0, unroll=FFT_UNROLL)


def fft_filter_spectrum(k, tabs):
    n_slabs, N, _ = k.shape
    plan = _FftPlan(N // 2)
    C = n_slabs * LANES
    full = lambda shape: pl.BlockSpec(shape, lambda i: (0,) * len(shape))
    return pl.pallas_call(
        functools.partial(_fft_filter_kernel, plan=plan),
        grid=(1,),
        in_specs=[full((n_slabs, plan.N, LANES)), full((FFT_N2, 2 * plan.KP, plan.N1)),
                  full((2 * FFT_N2, 2 * FFT_N2))],
        out_specs=full((plan.KH, 2 * FFT_N2, C)),
        out_shape=jax.ShapeDtypeStruct((plan.KH, 2 * FFT_N2, C), BF16),
        scratch_shapes=[pltpu.VMEM((n_slabs, FFT_N2 * plan.PA, LANES), F32)],
        compiler_params=_cparams("arbitrary"),
        name="hyena_filter_fft",
    )(k, tabs["ga_full"], tabs["fb"])


def fft_long_conv(z, h_spec, tabs):
    B, n_slabs, L, _ = z.shape
    plan = _FftPlan(L)
    C = n_slabs * LANES
    full = lambda shape: pl.BlockSpec(shape, lambda b: (0,) * len(shape))
    seq = pl.BlockSpec((1, n_slabs, L, LANES), lambda b: (b, 0, 0, 0))
    return pl.pallas_call(
        functools.partial(_fft_conv_kernel, plan=plan),
        grid=(B,),
        in_specs=[seq, full((FFT_N2, 2 * plan.KP, plan.N1 // 2)), full((FFT_N2, plan.N1 // 2, 2 * plan.KP)),
                  full((2 * FFT_N2, 2 * FFT_N2)), full((2 * FFT_N2, 2 * FFT_N2)),
                  full((plan.KH, 2 * FFT_N2, C))],
        out_specs=seq,
        out_shape=jax.ShapeDtypeStruct((B, n_slabs, L, LANES), F32),
        scratch_shapes=[pltpu.VMEM((n_slabs, FFT_N2 * plan.PA, LANES), F32)],
        compiler_params=_cparams("parallel"),
        name="hyena_fft_conv",
    )(z, tabs["ga_half"], tabs["ga_inv"], tabs["fb"], tabs["fb_inv"], h_spec)


def _hyena_pre_slab_kernel(u_ref, w_ref, b_ref, z_ref, upad):
    L = u_ref.shape[1]
    T = _time_chunk(L)
    C = D_GROUP
    _fill_padded(upad, u_ref, L, T)

    def body(j, carry):
        base = pl.multiple_of(j * T, T)
        for s in range(C // LANES):
            x1 = _dw_conv_slab(upad, C // LANES + s, base, T, w_ref, b_ref, C + s * LANES, HY_SHORT, 1)
            v = _dw_conv_slab(upad, 2 * C // LANES + s, base, T, w_ref, b_ref, 2 * C + s * LANES, HY_SHORT, 1)
            z_ref[0, s, pl.ds(base, T), :] = x1 * v
        return carry
    lax.fori_loop(0, L // T, body, 0)


def _hyena_post_slab_kernel(u0_ref, z_ref, y_ref, w_ref, b_ref, bias_ref, o_ref, upad):
    L = u0_ref.shape[1]
    T = _time_chunk(L)
    C = D_GROUP
    _fill_padded(upad, u0_ref, L, T)

    def body(j, carry):
        base = pl.multiple_of(j * T, T)
        x0 = _short_conv(upad, base, T, w_ref, b_ref, 0, C)
        z = jnp.concatenate([z_ref[0, s, pl.ds(base, T), :] for s in range(C // LANES)], axis=-1)
        y = jnp.concatenate([y_ref[0, s, pl.ds(base, T), :] for s in range(C // LANES)], axis=-1)
        o_ref[0, pl.ds(base, T), :] = x0 * (y + z * bias_ref[...])
        return carry
    lax.fori_loop(0, L // T, body, 0)


def hyena_mixer_fft(u, lp, tabs):
    B, L, _ = u.shape
    C = D_GROUP
    n_slabs = C // LANES
    w, bsh = lp["hy_short_w"], lp["hy_short_b"].reshape(1, 3 * C)
    useq = pl.BlockSpec((1, L, 3 * C), lambda b: (b, 0, 0))
    slabs = pl.BlockSpec((1, n_slabs, L, LANES), lambda b: (b, 0, 0, 0))
    wspec = pl.BlockSpec((HY_SHORT, 3 * C), lambda b: (0, 0))
    bspec = pl.BlockSpec((1, 3 * C), lambda b: (0, 0))
    pad_scratch = pltpu.VMEM((3 * C // LANES, L + 2 * CONV_MARGIN, LANES), F32)
    z = pl.pallas_call(
        _hyena_pre_slab_kernel,
        grid=(B,),
        in_specs=[useq, wspec, bspec],
        out_specs=slabs,
        out_shape=jax.ShapeDtypeStruct((B, n_slabs, L, LANES), F32),
        scratch_shapes=[pad_scratch],
        compiler_params=_cparams("parallel"),
        name="hyena_pre",
    )(u, w, bsh)
    y = fft_long_conv(z, fft_filter_spectrum(hyena_filter(L, lp), tabs), tabs)
    return pl.pallas_call(
        _hyena_post_slab_kernel,
        grid=(B,),
        in_specs=[pl.BlockSpec((1, L, C), lambda b: (b, 0, 0)), slabs, slabs, wspec, bspec,
                  pl.BlockSpec((1, C), lambda b: (0, 0))],
        out_specs=pl.BlockSpec((1, L, C), lambda b: (b, 0, 0)),
        out_shape=jax.ShapeDtypeStruct((B, L, C), F32),
        scratch_shapes=[pltpu.VMEM((n_slabs, L + 2 * CONV_MARGIN, LANES), F32)],
        compiler_params=_cparams("parallel"),
        name="hyena_post",
    )(u, z, y, w, bsh, lp["hy_bias"].reshape(1, C))


def _filter_gen_kernel(z_ref, zr_ref, w1_ref, b1_ref, w2_ref, b2_ref, w3_ref, dl_ref, k_ref, nrm):
    L = z_ref.shape[0]
    T = min(L, 512)
    C = D_GROUP
    hi = lax.Precision.HIGHEST

    def decayed(zz):
        h1 = jnp.sin(jnp.dot(zz, w1_ref[...], preferred_element_type=F32, precision=hi) + b1_ref[...])
        h2 = jnp.sin(jnp.dot(h1, w2_ref[...], preferred_element_type=F32, precision=hi) + b2_ref[...])
        h = jnp.dot(h2, w3_ref[...], preferred_element_type=F32, precision=hi)
        decay = jnp.exp(-zz[:, 0:1] * dl_ref[...])
        return h[:, :C] * decay, h[:, C:] * decay

    nrm[...] = jnp.zeros_like(nrm)

    def emit(j, carry):
        base = pl.multiple_of(j * T, T)
        fwd, _ = decayed(z_ref[pl.ds(base, T), :])
        _, bwd_rev = decayed(zr_ref[pl.ds(base, T), :])
        nrm[...] = nrm[...] + jnp.sum(jnp.abs(fwd) + jnp.abs(bwd_rev), axis=0, keepdims=True)
        for s in range(C // LANES):
            k_ref[s, pl.ds(base, T), :] = fwd[:, s * LANES:(s + 1) * LANES]
            k_ref[s, pl.ds(L + base, T), :] = bwd_rev[:, s * LANES:(s + 1) * LANES]
        return carry
    lax.fori_loop(0, L // T, emit, 0)

    inv = 1.0 / (nrm[...] + EPS)

    def scale(j, carry):
        base = pl.multiple_of(j * T, T)
        row = base + lax.broadcasted_iota(jnp.int32, (T, 1), 0)
        for s in range(C // LANES):
            inv_s = inv[:, s * LANES:(s + 1) * LANES]
            k_ref[s, pl.ds(base, T), :] = k_ref[s, pl.ds(base, T), :] * inv_s
            k_ref[s, pl.ds(L + base, T), :] = jnp.where(row == 0, 0.0, k_ref[s, pl.ds(L + base, T), :] * inv_s)
        return carry
    lax.fori_loop(0, L // T, scale, 0)


def hyena_filter(L, lp):
    C = D_GROUP
    n_slabs = C // LANES
    t = jnp.linspace(0.0, 1.0, L, dtype=F32)[:, None]
    bands = (HY_EMB - 1) // 2
    w = 2.0 * math.pi * jnp.arange(L, dtype=F32)[:, None] / L
    f = jnp.linspace(1e-4, bands - 1, bands, dtype=F32)[None]
    z = jnp.concatenate([t, jnp.cos(f * w), -jnp.sin(f * w)], axis=-1)
    z = jnp.pad(z, ((0, 0), (0, LANES - HY_EMB)))
    z_rev = jnp.concatenate([z[:1], z[1:][::-1]], axis=0)
    w1 = jnp.pad(lp["hy_ffn_w1"], ((0, LANES - HY_EMB), (0, 0)))
    H = w1.shape[1]
    max_decay = math.log(HY_TARGET) / HY_FAST_DECAY
    min_decay = math.log(HY_TARGET) / HY_SLOW_DECAY
    abs_deltas = jnp.abs(jnp.linspace(min_decay, max_decay, C, dtype=F32)).reshape(1, C)
    full = lambda *shape: pl.BlockSpec(shape, lambda i: (0,) * len(shape))
    return pl.pallas_call(
        _filter_gen_kernel,
        grid=(1,),
        in_specs=[full(L, LANES), full(L, LANES), full(LANES, H), full(1, H), full(H, H), full(1, H),
                  full(H, 2 * C), full(1, C)],
        out_specs=full(n_slabs, 2 * L, LANES),
        out_shape=jax.ShapeDtypeStruct((n_slabs, 2 * L, LANES), F32),
        scratch_shapes=[pltpu.VMEM((1, C), F32)],
        compiler_params=_cparams("arbitrary"),
        name="hyena_filter_gen",
    )(z, z_rev, w1, lp["hy_ffn_b1"].reshape(1, H), lp["hy_ffn_w2"], lp["hy_ffn_b2"].reshape(1, H),
      lp["hy_ffn_w3"], abs_deltas)


def _layer(hc, hx, c_silu_all, lp, need_ctx, final_g, final_norm, tables_x, tables_c, experts, layer):
    B, S, D = hx.shape
    C = hc.shape[1]
    mod = small_linear(c_silu_all, lp["ada_w"], lp["ada_b"])
    mod_x = mod[:B].reshape(B, 6, 1, D)
    mod_c = jnp.broadcast_to(mod[B].reshape(1, 6, 1, D), (B, 6, 1, D))
    w_ext = extend_w_in(lp["w_in"])
    cos_x, sin_x = rope_tables(S, True)
    cos_c, sin_c = rope_tables(C, False)
    hy_x, cf_x, at_x, lr_x = in_proj(hx, mod_x[:, 0], mod_x[:, 1], lp["norm1_g"], w_ext, cos_x, sin_x, tm=512)
    hy_c, cf_c, at_c, lr_c = in_proj(hc, mod_c[:, 0], mod_c[:, 1], lp["norm1_g"], w_ext, cos_c, sin_c, tm=256)

    yd_c, yd_x = rglru_mixer(lr_c, lr_x, lp, need_ctx)
    conf = lambda u: conformer_conv(u, lp["conf_dw_w"], lp["conf_dw_b"], lp["conf_ln_g"], lp["conf_ln_b"])
    ys_x = [hyena_mixer_fft(hy_x, lp, tables_x), conf(cf_x),
            window_attention(at_x, at_c, lp["attn_sink"]), yd_x]

    w_out = lp["w_out"].astype(BF16)
    w_router = jnp.zeros((D, ROUTER_COLS), F32)
    w_router = w_router.at[:, :N_GROUPS].set(lp["router_g_w"]).at[:, N_GROUPS:N_GROUPS + N_EXPERTS].set(lp["router_e_w"])
    w_router = w_router.astype(BF16)
    b_router = jnp.zeros((1, ROUTER_COLS), F32)
    b_router = b_router.at[0, :N_GROUPS].set(lp["router_g_b"]).at[0, N_GROUPS:N_GROUPS + N_EXPERTS].set(lp["router_e_b"])

    hx1, lg_x = out_proj(ys_x, hx, mod_x[:, 2], lp["group_norm_g"], w_out, lp["norm2_g"],
                         mod_x[:, 3], mod_x[:, 4], w_router, b_router, tm=512)
    h_tok = hx1.reshape(B * S, D)
    hc_tok = None
    lg = lg_x.reshape(B * S, ROUTER_COLS)
    if need_ctx:
        ys_c = [hyena_mixer(hy_c, lp, tables_c), conf(cf_c),
                context_attention(at_c, lp["attn_sink"]), yd_c]
        hc1, lg_c = out_proj(ys_c, hc, mod_c[:, 2], lp["group_norm_g"], w_out, lp["norm2_g"],
                             mod_c[:, 3], mod_c[:, 4], w_router, b_router, tm=256)
        hc_tok = hc1.reshape(B * C, D)
        lg = jnp.concatenate([lg, lg_c.reshape(B * C, ROUTER_COLS)], axis=0)

    T = lg.shape[0]
    n_blocks = -(-T // MOE_BLOCK) + N_CLASSES
    info, counts, ids = route_tokens(lg)
    dest, blk_a, blk_b, n_used = slot_plan(ids, counts, n_blocks)
    xs = moe_dispatch(h_tok, hc_tok, dest, n_blocks, lp["norm2_g"], (mod_x[:, 3], mod_x[:, 4]),
                      (mod_c[:, 3], mod_c[:, 4]), S)
    o_sorted = expert_pairs(xs, blk_a, blk_b, n_used, *experts, layer)
    hx2 = moe_collect(o_sorted, dest, info, 0, h_tok, mod_x[:, 5], S, final_g, final_norm)
    hx2 = hx2.reshape(B, S, D)
    if need_ctx:
        hc2 = moe_collect(o_sorted, dest, info, B * S // COLLECT_TOKENS, hc_tok, mod_c[:1, 5], C,
                          final_g, False).reshape(B, C, D)
    else:
        hc2 = hc
    return hc2, hx2


def kernel(x, c, ctx, c_ctx, norm1_g, norm2_g, ada_w, ada_b, w_in, hy_short_w, hy_short_b, hy_ffn_w1, hy_ffn_b1, hy_ffn_w2, hy_ffn_b2, hy_ffn_w3, hy_bias, conf_dw_w, conf_dw_b, conf_ln_g, conf_ln_b, attn_sink, lru_conv_w, lru_conv_b, lru_wa, lru_ba, lru_wx, lru_bx, lru_lambda, group_norm_g, w_out, router_g_w, router_g_b, router_e_w, router_e_b, exp_w_gate, exp_w_up, exp_w_down, final_norm_g):
    stacked = dict(norm1_g=norm1_g, norm2_g=norm2_g, ada_w=ada_w, ada_b=ada_b, w_in=w_in,
                   hy_short_w=hy_short_w, hy_short_b=hy_short_b, hy_ffn_w1=hy_ffn_w1, hy_ffn_b1=hy_ffn_b1,
                   hy_ffn_w2=hy_ffn_w2, hy_ffn_b2=hy_ffn_b2, hy_ffn_w3=hy_ffn_w3, hy_bias=hy_bias,
                   conf_dw_w=conf_dw_w, conf_dw_b=conf_dw_b, conf_ln_g=conf_ln_g, conf_ln_b=conf_ln_b,
                   attn_sink=attn_sink, lru_conv_w=lru_conv_w, lru_conv_b=lru_conv_b, lru_wa=lru_wa,
                   lru_ba=lru_ba, lru_wx=lru_wx, lru_bx=lru_bx, lru_lambda=lru_lambda,
                   group_norm_g=group_norm_g, w_out=w_out, router_g_w=router_g_w, router_g_b=router_g_b,
                   router_e_w=router_e_w, router_e_b=router_e_b)
    experts = (exp_w_gate.astype(BF16), exp_w_up.astype(BF16), exp_w_down.astype(BF16))
    depth = norm1_g.shape[0]
    B = x.shape[0]
    cs = jnp.concatenate([jax.nn.silu(c), jnp.broadcast_to(jax.nn.silu(c_ctx)[None], (8, c.shape[1]))], axis=0)
    hc, hx = ctx, x
    tables_x = fft_tables(x.shape[1])
    tables_c = dft_tables(ctx.shape[1])
    for l in range(depth):
        lp = {k: v[l] for k, v in stacked.items()}
        hc, hx = _layer(hc, hx, cs, lp, need_ctx=(l < depth - 1), final_g=final_norm_g,
                        final_norm=(l == depth - 1), tables_x=tables_x, tables_c=tables_c,
                        experts=experts, layer=l)
    return hx
```

```python
import functools
import math

import jax
import jax.numpy as jnp
from jax import lax
from jax.experimental import pallas as pl
from jax.experimental.pallas import tpu as pltpu

F32 = jnp.float32
BF16 = jnp.bfloat16

EPS = 1e-6
NEG_INF = -1e30
GRID_W = 64
D_GROUP = 256
HY_COLS = 3 * D_GROUP
CONF_COLS = 2 * D_GROUP
ATT_HEADS = 4
ATT_KV_HEADS = 2
HEAD_DIM = 64
ATT_COLS = (ATT_HEADS + 2 * ATT_KV_HEADS) * HEAD_DIM
LRU_COLS = 2 * D_GROUP
QK_COLS = (ATT_HEADS + ATT_KV_HEADS) * HEAD_DIM
WINDOW = 128
ATT_BLOCK = 128
ROPE_BASE = 10000.0
HY_EMB = 33
HY_FAST_DECAY = 0.3
HY_SLOW_DECAY = 1.5
HY_TARGET = 1e-2
CONF_KERNEL = 31
LRU_CONV = 4
LRU_C = 8.0
N_GROUPS = 4
EXP_PER_GROUP = 8
N_EXPERTS = N_GROUPS * EXP_PER_GROUP
MOE_BLOCK = 256
ROUTER_COLS = 128

VMEM_LIMIT_BYTES = 56 * 1024 * 1024


def _cparams(*sem):
    return pltpu.CompilerParams(dimension_semantics=sem, vmem_limit_bytes=VMEM_LIMIT_BYTES)


def _linear_kernel(x_ref, w_ref, b_ref, o_ref):
    o_ref[...] = jnp.dot(x_ref[...], w_ref[...], preferred_element_type=F32,
                         precision=lax.Precision.HIGHEST) + b_ref[...]


def small_linear(x, w, b, tn=1024):
    M, K = x.shape
    N = w.shape[1]
    return pl.pallas_call(
        _linear_kernel,
        grid=(N // tn,),
        in_specs=[pl.BlockSpec((M, K), lambda j: (0, 0)),
                  pl.BlockSpec((K, tn), lambda j: (0, j)),
                  pl.BlockSpec((1, tn), lambda j: (0, j))],
        out_specs=pl.BlockSpec((M, tn), lambda j: (0, j)),
        out_shape=jax.ShapeDtypeStruct((M, N), F32),
        compiler_params=_cparams("parallel"),
        name="ada_linear",
    )(x, w, b.reshape(1, N))


def _in_proj_kernel(x_ref, sh_ref, sc_ref, g_ref, w_ref, cos_ref, sin_ref,
                    hy_ref, cf_ref, at_ref, lr_ref):
    x = x_ref[0]
    ms = jnp.mean(x * x, axis=-1, keepdims=True)
    y = x * lax.rsqrt(ms + EPS) * g_ref[...]
    y = y * (1.0 + sc_ref[0]) + sh_ref[0]
    u = jnp.dot(y.astype(BF16), w_ref[...], preferred_element_type=F32)
    c0 = HY_COLS
    c1 = c0 + CONF_COLS
    c2 = c1 + ATT_COLS
    c3 = c2 + LRU_COLS
    hy_ref[0] = u[:, :c0]
    cf_ref[0] = u[:, c0:c1]
    lr_ref[0] = u[:, c2:c3]
    qk = u[:, c1:c1 + QK_COLS]
    qk_rot = u[:, c3:c3 + QK_COLS]
    at_ref[0, :, :QK_COLS] = qk * cos_ref[...] + qk_rot * sin_ref[...]
    at_ref[0, :, QK_COLS:] = u[:, c1 + QK_COLS:c2]


def in_proj(h, shift, scale, g, w_ext, cos_t, sin_t, tm):
    B, L, D = h.shape
    NW = w_ext.shape[1]
    outs = [HY_COLS, CONF_COLS, ATT_COLS, LRU_COLS]
    return pl.pallas_call(
        _in_proj_kernel,
        grid=(B, L // tm),
        in_specs=[pl.BlockSpec((1, tm, D), lambda b, i: (b, i, 0)),
                  pl.BlockSpec((1, 1, D), lambda b, i: (b, 0, 0)),
                  pl.BlockSpec((1, 1, D), lambda b, i: (b, 0, 0)),
                  pl.BlockSpec((1, D), lambda b, i: (0, 0)),
                  pl.BlockSpec((D, NW), lambda b, i: (0, 0)),
                  pl.BlockSpec((tm, QK_COLS), lambda b, i: (i, 0)),
                  pl.BlockSpec((tm, QK_COLS), lambda b, i: (i, 0))],
        out_specs=[pl.BlockSpec((1, tm, n), lambda b, i: (b, i, 0)) for n in outs],
        out_shape=[jax.ShapeDtypeStruct((B, L, n), F32) for n in outs],
        compiler_params=_cparams("parallel", "parallel"),
        name="in_proj",
    )(h, shift, scale, g.reshape(1, D), w_ext, cos_t, sin_t)


def rope_tables(L, rotary):
    n_heads = ATT_HEADS + ATT_KV_HEADS
    if not rotary:
        return jnp.ones((L, QK_COLS), F32), jnp.zeros((L, QK_COLS), F32)
    pos = jnp.arange(L)
    row = (pos // GRID_W).astype(F32)
    col = (pos % GRID_W).astype(F32)
    half = HEAD_DIM // 2
    inv_freq = ROPE_BASE ** (-jnp.arange(0, half, 2, dtype=F32) / half)
    ang_r = row[:, None] * inv_freq[None]
    ang_c = col[:, None] * inv_freq[None]
    cos_h = jnp.concatenate([jnp.cos(ang_r)] * 2 + [jnp.cos(ang_c)] * 2, axis=-1)
    sin_h = jnp.concatenate([jnp.sin(ang_r)] * 2 + [jnp.sin(ang_c)] * 2, axis=-1)
    return jnp.tile(cos_h, (1, n_heads)), jnp.tile(sin_h, (1, n_heads))


def extend_w_in(w_in):
    c1 = HY_COLS + CONF_COLS
    wqk = w_in[:, c1:c1 + QK_COLS]
    D = w_in.shape[0]
    w4 = wqk.reshape(D, QK_COLS // 32, 2, 16)
    wrot = jnp.stack([-w4[:, :, 1], w4[:, :, 0]], axis=2).reshape(D, QK_COLS)
    return jnp.concatenate([w_in, wrot], axis=1).astype(BF16)


def _softmax_parts(q, k_list, extra_logit):
    scale = HEAD_DIM ** -0.5
    s_list = []
    for k, mask in k_list:
        s = lax.dot_general(q, k, (((1,), (1,)), ((), ())), preferred_element_type=F32) * scale
        if mask is not None:
            s = jnp.where(mask, s, NEG_INF)
        s_list.append(s)
    m = extra_logit
    for s in s_list:
        m = jnp.maximum(m, jnp.max(s, axis=-1, keepdims=True))
    p_list = [jnp.exp(s - m) for s in s_list]
    denom = jnp.exp(extra_logit - m)
    for p in p_list:
        denom = denom + jnp.sum(p, axis=-1, keepdims=True)
    return p_list, 1.0 / denom


ATT_Q_BLOCKS = 16


def _win_attn_kernel(sink_ref, q_ref, kp_ref, kc_ref, kn_ref, vp_ref, vc_ref, vn_ref,
                     kx_ref, vx_ref, o_ref, *, seq_len):
    i = pl.program_id(1)
    blk = ATT_BLOCK
    qb = q_ref.shape[1] // blk
    scale = HEAD_DIM ** -0.5
    g = ATT_HEADS // ATT_KV_HEADS
    kw = jnp.concatenate([kp_ref[0], kc_ref[0], kn_ref[0]], axis=0)
    vw = jnp.concatenate([vp_ref[0], vc_ref[0], vn_ref[0]], axis=0).astype(BF16)
    kwt = kw.T.astype(BF16)
    kxt = kx_ref[0].T.astype(BF16)
    vx = vx_ref[0].astype(BF16)
    row = lax.broadcasted_iota(jnp.int32, (g * blk, 3 * blk), 0) % blk
    col = lax.broadcasted_iota(jnp.int32, (g * blk, 3 * blk), 1)
    band_bias = jnp.where(jnp.abs(col - blk - row) <= WINDOW, 0.0, NEG_INF)
    col1 = lax.broadcasted_iota(jnp.int32, (1, 3 * blk), 1)
    for j in range(qb):
        q_blk = i * qb + j
        k_pos = (q_blk - 1) * blk + col1
        edge_bias = jnp.where(k_pos >= 0, jnp.where(k_pos < seq_len, 0.0, NEG_INF), NEG_INF)
        bias = band_bias + edge_bias
        outs = []
        for kv in range(ATT_KV_HEADS):
            ksl = slice(kv * HEAD_DIM, (kv + 1) * HEAD_DIM)
            heads = range(kv * g, (kv + 1) * g)
            qs = (jnp.concatenate([q_ref[0, j * blk:(j + 1) * blk, h * HEAD_DIM:(h + 1) * HEAD_DIM]
                                   for h in heads], axis=0) * scale).astype(BF16)
            sink = jnp.concatenate([jnp.full((blk, 1), sink_ref[h], F32) for h in heads], axis=0)
            s_win = jnp.dot(qs, kwt[ksl, j * blk:(j + 3) * blk], preferred_element_type=F32) + bias
            s_ctx = jnp.dot(qs, kxt[ksl, :], preferred_element_type=F32)
            m = jnp.maximum(jnp.maximum(jnp.max(s_win, axis=-1, keepdims=True),
                                        jnp.max(s_ctx, axis=-1, keepdims=True)), sink)
            p_win = jnp.exp(s_win - m)
            p_ctx = jnp.exp(s_ctx - m)
            denom = (jnp.exp(sink - m) + jnp.sum(p_win, axis=-1, keepdims=True)
                     + jnp.sum(p_ctx, axis=-1, keepdims=True))
            o = (jnp.dot(p_win.astype(BF16), vw[j * blk:(j + 3) * blk, ksl], preferred_element_type=F32)
                 + jnp.dot(p_ctx.astype(BF16), vx[:, ksl], preferred_element_type=F32)) * (1.0 / denom)
            outs.extend([o[k * blk:(k + 1) * blk] for k in range(g)])
        o_ref[0, j * blk:(j + 1) * blk, :] = jnp.concatenate(outs, axis=-1)


def window_attention(at_x, at_c, sink):
    B, S, _ = at_x.shape
    C = at_c.shape[1]
    blk = ATT_BLOCK
    qb = ATT_Q_BLOCKS
    nb = S // blk
    kcol = QK_COLS // 128 - 1
    vcol = kcol + 1

    def edge_spec(col, off):
        return pl.BlockSpec((1, blk, 128), lambda b, i, s: (b, jnp.clip(i * qb + off, 0, nb - 1), col))

    def mid_spec(col):
        return pl.BlockSpec((1, qb * blk, 128), lambda b, i, s: (b, i, col))

    grid_spec = pltpu.PrefetchScalarGridSpec(
        num_scalar_prefetch=1,
        grid=(B, nb // qb),
        in_specs=[pl.BlockSpec((1, qb * blk, ATT_HEADS * HEAD_DIM), lambda b, i, s: (b, i, 0)),
                  edge_spec(kcol, -1), mid_spec(kcol), edge_spec(kcol, qb),
                  edge_spec(vcol, -1), mid_spec(vcol), edge_spec(vcol, qb),
                  pl.BlockSpec((1, C, 128), lambda b, i, s: (b, 0, kcol)),
                  pl.BlockSpec((1, C, 128), lambda b, i, s: (b, 0, vcol))],
        out_specs=pl.BlockSpec((1, qb * blk, ATT_HEADS * HEAD_DIM), lambda b, i, s: (b, i, 0)),
    )
    return pl.pallas_call(
        functools.partial(_win_attn_kernel, seq_len=S),
        grid_spec=grid_spec,
        out_shape=jax.ShapeDtypeStruct((B, S, ATT_HEADS * HEAD_DIM), F32),
        compiler_params=_cparams("parallel", "parallel"),
        name="window_attention",
    )(sink.astype(F32), at_x, at_x, at_x, at_x, at_x, at_x, at_x, at_c, at_c)


def _ctx_attn_kernel(sink_ref, q_ref, kx_ref, vx_ref, o_ref):
    q = q_ref[0].astype(BF16)
    kx = kx_ref[0].astype(BF16)
    vx = vx_ref[0].astype(BF16)
    g = ATT_HEADS // ATT_KV_HEADS
    outs = []
    for h in range(ATT_HEADS):
        kv = h // g
        qs = q[:, h * HEAD_DIM:(h + 1) * HEAD_DIM]
        ksl = slice(kv * HEAD_DIM, (kv + 1) * HEAD_DIM)
        (p_ctx,), inv = _softmax_parts(qs, [(kx[:, ksl], None)], sink_ref[h])
        outs.append(jnp.dot(p_ctx.astype(BF16), vx[:, ksl], preferred_element_type=F32) * inv)
    o_ref[0] = jnp.concatenate(outs, axis=-1)


def context_attention(at_c, sink):
    B, C, _ = at_c.shape
    kcol = QK_COLS // 128 - 1
    grid_spec = pltpu.PrefetchScalarGridSpec(
        num_scalar_prefetch=1,
        grid=(B,),
        in_specs=[pl.BlockSpec((1, C, ATT_HEADS * HEAD_DIM), lambda b, s: (b, 0, 0)),
                  pl.BlockSpec((1, C, 128), lambda b, s: (b, 0, kcol)),
                  pl.BlockSpec((1, C, 128), lambda b, s: (b, 0, kcol + 1))],
        out_specs=pl.BlockSpec((1, C, ATT_HEADS * HEAD_DIM), lambda b, s: (b, 0, 0)),
    )
    return pl.pallas_call(
        _ctx_attn_kernel,
        grid_spec=grid_spec,
        out_shape=jax.ShapeDtypeStruct((B, C, ATT_HEADS * HEAD_DIM), F32),
        compiler_params=_cparams("parallel"),
        name="context_attention",
    )(sink.astype(F32), at_c, at_c, at_c)


def _out_proj_kernel(y0_ref, y1_ref, y2_ref, y3_ref, h_ref, g1_ref, gng_ref, w_ref,
                     n2g_ref, sh_ref, sc_ref, wr_ref, br_ref, ho_ref, lg_ref):
    parts = []
    for k, y_ref in enumerate((y0_ref, y1_ref, y2_ref, y3_ref)):
        y = y_ref[0]
        ms = jnp.mean(y * y, axis=-1, keepdims=True)
        yn = y * lax.rsqrt(ms + EPS) * gng_ref[:, k * D_GROUP:(k + 1) * D_GROUP]
        parts.append(yn.astype(BF16))
    yn = jnp.concatenate(parts, axis=-1)
    proj = jnp.dot(yn, w_ref[...], preferred_element_type=F32)
    h = h_ref[0] + g1_ref[0] * proj
    ho_ref[0] = h
    ms = jnp.mean(h * h, axis=-1, keepdims=True)
    n = h * lax.rsqrt(ms + EPS) * n2g_ref[...]
    n = n * (1.0 + sc_ref[0]) + sh_ref[0]
    lg_ref[0] = jnp.dot(n.astype(BF16), wr_ref[...], preferred_element_type=F32) + br_ref[...]


def out_proj(ys, h, g1, gng, w_out, n2g, sh2, sc2, w_router, b_router, tm):
    B, L, D = h.shape
    row3 = lambda n: pl.BlockSpec((1, tm, n), lambda b, i: (b, i, 0))
    mod = pl.BlockSpec((1, 1, D), lambda b, i: (b, 0, 0))
    full = lambda r, c: pl.BlockSpec((r, c), lambda b, i: (0, 0))
    return pl.pallas_call(
        _out_proj_kernel,
        grid=(B, L // tm),
        in_specs=[row3(D_GROUP)] * 4 + [row3(D), mod, full(1, D), full(D, D), full(1, D), mod, mod,
                                        full(D, ROUTER_COLS), full(1, ROUTER_COLS)],
        out_specs=[row3(D), row3(ROUTER_COLS)],
        out_shape=[jax.ShapeDtypeStruct((B, L, D), F32), jax.ShapeDtypeStruct((B, L, ROUTER_COLS), F32)],
        compiler_params=_cparams("parallel", "parallel"),
        name="out_proj",
    )(*ys, h, g1, gng.reshape(1, D), w_out, n2g.reshape(1, D), sh2, sc2, w_router, b_router)


N_PAIRS = EXP_PER_GROUP * (EXP_PER_GROUP - 1) // 2
N_CLASSES = N_GROUPS * N_PAIRS
ROUTE_TOKENS = 512
INFO_CLASS, INFO_RANK, INFO_WA, INFO_WB = 0, 1, 2, 3


SUBLANES = 8


def _route_kernel(lg_ref, below_ref, info_ref, cnt_ref, ids_ref, slots_ref, run):
    i = pl.program_id(0)

    @pl.when(i == 0)
    def _():
        run[...] = jnp.zeros_like(run)

    lg = lg_ref[...]
    li = lax.broadcasted_iota(jnp.int32, lg.shape, 1).astype(F32)
    big = float(ROUTER_COLS)

    def first_argmax(vals):
        m = jnp.max(vals, axis=-1, keepdims=True)
        return m, jnp.min(jnp.where(vals == m, li, big), axis=-1, keepdims=True)

    gl = jnp.where(li < N_GROUPS, lg, NEG_INF)
    gmax, g_idx = first_argmax(gl)
    g_prob = 1.0 / jnp.sum(jnp.exp(gl - gmax), axis=-1, keepdims=True)
    lo = N_GROUPS + EXP_PER_GROUP * g_idx
    el = jnp.where(li >= lo, jnp.where(li < lo + EXP_PER_GROUP, lg, NEG_INF), NEG_INF)
    m1, i1 = first_argmax(el)
    m2, i2 = first_argmax(jnp.where(li == i1, NEG_INF, el))
    e2 = jnp.exp(m2 - m1)
    w1 = g_prob / (1.0 + e2)
    w2 = g_prob * e2 / (1.0 + e2)
    j1 = i1 - lo
    j2 = i2 - lo
    a = jnp.minimum(j1, j2)
    b = jnp.maximum(j1, j2)
    cls = g_idx * N_PAIRS + (a * (2 * EXP_PER_GROUP - 1 - a)) * 0.5 + (b - a - 1.0)
    w_a = jnp.where(j1 < j2, w1, w2)
    w_b = jnp.where(j1 < j2, w2, w1)

    hit = li == cls
    onehot = jnp.where(hit, 1.0, 0.0)
    before = jnp.dot(below_ref[...], onehot.astype(BF16), preferred_element_type=F32)
    rank = jnp.sum(jnp.where(hit, before + run[...], 0.0), axis=-1, keepdims=True)
    run[...] = run[...] + jnp.sum(onehot, axis=0, keepdims=True)
    cnt_ref[...] = run[...]
    info = jnp.where(li == INFO_CLASS, cls, 0.0)
    info = jnp.where(li == INFO_RANK, rank, info)
    info = jnp.where(li == INFO_WA, w_a, info)
    info = jnp.where(li == INFO_WB, w_b, info)
    info_ref[...] = info
    ids_ref[0] = info.T[:SUBLANES].astype(jnp.int32)
    slots_ref[...] = jnp.zeros_like(slots_ref)


def route_tokens(logits, slot_rows):
    T = logits.shape[0]
    tb = ROUTE_TOKENS
    steps = T // tb
    rows_per_step = -(-slot_rows // (steps * SUBLANES)) * SUBLANES
    below = (jnp.arange(tb)[None, :] < jnp.arange(tb)[:, None]).astype(BF16)
    return pl.pallas_call(
        _route_kernel,
        grid=(steps,),
        in_specs=[pl.BlockSpec((tb, ROUTER_COLS), lambda i: (i, 0)),
                  pl.BlockSpec((tb, tb), lambda i: (0, 0))],
        out_specs=[pl.BlockSpec((tb, ROUTER_COLS), lambda i: (i, 0)),
                   pl.BlockSpec((1, ROUTER_COLS), lambda i: (0, 0)),
                   pl.BlockSpec((1, SUBLANES, tb), lambda i: (i, 0, 0)),
                   pl.BlockSpec((rows_per_step, LANES), lambda i: (i, 0))],
        out_shape=[jax.ShapeDtypeStruct((T, ROUTER_COLS), F32), jax.ShapeDtypeStruct((1, ROUTER_COLS), F32),
                   jax.ShapeDtypeStruct((steps, SUBLANES, tb), jnp.int32),
                   jax.ShapeDtypeStruct((steps * rows_per_step, LANES), F32)],
        scratch_shapes=[pltpu.VMEM((1, ROUTER_COLS), F32)],
        compiler_params=_cparams("arbitrary"),
        name="moe_route",
    )(logits, below)


def _pair_tables():
    a_tab, b_tab = [], []
    for g in range(N_GROUPS):
        for a in range(EXP_PER_GROUP):
            for b in range(a + 1, EXP_PER_GROUP):
                a_tab.append(g * EXP_PER_GROUP + a)
                b_tab.append(g * EXP_PER_GROUP + b)
    return jnp.array(a_tab, jnp.int32), jnp.array(b_tab, jnp.int32)


def _slot_kernel(ids_ref, start_ref, dest_ref):
    cls = ids_ref[0, INFO_CLASS:INFO_CLASS + 1, :]
    rank = ids_ref[0, INFO_RANK:INFO_RANK + 1, :]
    ci = lax.broadcasted_iota(jnp.int32, (ROUTER_COLS, cls.shape[1]), 0)
    start = jnp.sum(jnp.where(ci == cls, start_ref[...], 0), axis=0, keepdims=True)
    dest_ref[0] = jnp.broadcast_to(start + rank, dest_ref.shape[1:])


def slot_plan(ids, counts, n_blocks):
    nt, _, tb = ids.shape
    cnt = counts[0, :N_CLASSES].astype(jnp.int32)
    padded = (cnt + MOE_BLOCK - 1) // MOE_BLOCK * MOE_BLOCK
    upto = jnp.arange(N_CLASSES)[None, :] <= jnp.arange(N_CLASSES)[:, None]
    pad_end = jnp.sum(jnp.where(upto, padded[None, :], 0), axis=1)
    class_start = jnp.zeros((ROUTER_COLS, 1), jnp.int32).at[:N_CLASSES, 0].set(pad_end - padded)
    dest = pl.pallas_call(
        _slot_kernel,
        grid=(nt,),
        in_specs=[pl.BlockSpec((1, SUBLANES, tb), lambda i: (i, 0, 0)),
                  pl.BlockSpec((ROUTER_COLS, 1), lambda i: (0, 0))],
        out_specs=pl.BlockSpec((1, SUBLANES, tb), lambda i: (i, 0, 0)),
        out_shape=jax.ShapeDtypeStruct((nt, SUBLANES, tb), jnp.int32),
        compiler_params=_cparams("parallel"),
        name="moe_slots",
    )(ids, class_start)[:, 0, :].reshape(nt * tb)
    n_used = (pad_end[-1] // MOE_BLOCK).astype(jnp.int32).reshape(1)
    blk_first = jnp.arange(n_blocks, dtype=jnp.int32) * MOE_BLOCK
    blk_cls = jnp.minimum(jnp.sum((pad_end[None, :] <= blk_first[:, None]).astype(jnp.int32), axis=1),
                          N_CLASSES - 1)
    a_tab, b_tab = _pair_tables()
    hit = blk_cls[:, None] == jnp.arange(N_CLASSES)[None, :]
    pick = lambda tab: jnp.sum(jnp.where(hit, tab[None, :], 0), axis=1).astype(jnp.int32)
    return dest, pick(a_tab), pick(b_tab), n_used


DISPATCH_TOKENS = 1024
COLLECT_TOKENS = 512


def _wait_rows(buf, sem):
    pltpu.make_async_copy(buf, buf, sem).wait()


DMA_UNROLL = 8
TOKEN_TILE_ROWS = 8


def _store_token_tiles(tiles_ref, offset, pitch, x):
    n = x.shape[0]
    for j in range(x.shape[1] // LANES):
        tiles_ref[pl.ds(offset + j, n, stride=pitch), :] = x[:, j * LANES:(j + 1) * LANES]


def _load_token_tiles(tiles_ref, offset, pitch, n, width):
    return jnp.concatenate([tiles_ref[pl.ds(offset + j, n, stride=pitch), :] for j in range(width // LANES)],
                           axis=-1)


def _dispatch_kernel(dest_ref, hx_ref, hc_ref, g_ref, shx_ref, scx_ref, shc_ref, scc_ref, zeros_hbm,
                     xs_hbm, rows, sems, *, n_latent_blocks):
    del zeros_hbm
    i = pl.program_id(0)
    n = pl.num_programs(0)
    slot = i % 2
    tb = hx_ref.shape[0]

    @pl.when(i >= 2)
    def _():
        _wait_rows(rows.at[slot], sems.at[slot])

    def normed(h_ref, sh_ref, sc_ref):
        h = h_ref[...]
        ms = jnp.mean(h * h, axis=-1, keepdims=True)
        return h * lax.rsqrt(ms + EPS) * g_ref[...] * (1.0 + sc_ref[0]) + sh_ref[0]

    @pl.when(i < n_latent_blocks)
    def _():
        _store_token_tiles(rows.at[slot], 0, TOKEN_TILE_ROWS, normed(hx_ref, shx_ref, scx_ref))

    @pl.when(i >= n_latent_blocks)
    def _():
        _store_token_tiles(rows.at[slot], 0, TOKEN_TILE_ROWS, normed(hc_ref, shc_ref, scc_ref))

    def body(g, carry):
        for u in range(DMA_UNROLL):
            r = g * DMA_UNROLL + u
            dst = pl.multiple_of(dest_ref[0, 0, r] * TOKEN_TILE_ROWS, TOKEN_TILE_ROWS)
            pltpu.make_async_copy(rows.at[slot, pl.ds(r * TOKEN_TILE_ROWS, TOKEN_TILE_ROWS)],
                                  xs_hbm.at[pl.ds(dst, TOKEN_TILE_ROWS)], sems.at[slot]).start(priority=u % 2)
        return carry
    lax.fori_loop(0, tb // DMA_UNROLL, body, 0)

    @pl.when(i == n - 1)
    def _():
        _wait_rows(rows.at[slot], sems.at[slot])

        @pl.when(n >= 2)
        def _():
            _wait_rows(rows.at[1 - slot], sems.at[1 - slot])


def moe_dispatch(h_x, h_c, dest, zero_slots, n2g, mod_x, mod_c, tokens_per_batch):
    Tx, D = h_x.shape
    tb = DISPATCH_TOKENS
    nxb = Tx // tb
    if h_c is None:
        h_c, mod_c, ncb = h_x, mod_x, 0
    else:
        ncb = h_c.shape[0] // tb
    per_b = tokens_per_batch // tb
    tile_rows = D // LANES
    assert tile_rows == TOKEN_TILE_ROWS
    xi = lambda i: jnp.minimum(i, nxb - 1)
    ci = lambda i: jnp.maximum(i - nxb, 0)
    modx = pl.BlockSpec((1, 1, D), lambda i: (xi(i) // per_b, 0, 0))
    modc = pl.BlockSpec((1, 1, D), lambda i: (0, 0, 0))
    return pl.pallas_call(
        functools.partial(_dispatch_kernel, n_latent_blocks=nxb),
        grid=(nxb + ncb,),
        in_specs=[pl.BlockSpec((1, 1, tb), lambda i: (i, 0, 0), memory_space=pltpu.SMEM),
                  pl.BlockSpec((tb, D), lambda i: (xi(i), 0)),
                  pl.BlockSpec((tb, D), lambda i: (ci(i), 0)),
                  pl.BlockSpec((1, D), lambda i: (0, 0)),
                  modx, modx, modc, modc,
                  pl.BlockSpec(memory_space=pl.ANY)],
        out_specs=pl.BlockSpec(memory_space=pl.ANY),
        out_shape=jax.ShapeDtypeStruct(zero_slots.shape, F32),
        scratch_shapes=[pltpu.VMEM((2, tb * tile_rows, LANES), F32), pltpu.SemaphoreType.DMA((2,))],
        input_output_aliases={8: 0},
        compiler_params=_cparams("arbitrary"),
        name="moe_dispatch",
    )(dest.reshape(-1, 1, tb), h_x, h_c, n2g.reshape(1, D), mod_x[0], mod_x[1], mod_c[0], mod_c[1], zero_slots)


def _expert_pair_kernel(ea_ref, eb_ref, nused_ref, xs_ref, wga_ref, wua_ref, wda_ref, wgb_ref, wub_ref, wdb_ref,
                        o_ref):
    del ea_ref, eb_ref
    i = pl.program_id(0)
    D = wga_ref.shape[2]

    @pl.when(i < nused_ref[0])
    def _():
        xb = _load_token_tiles(xs_ref, 0, TOKEN_TILE_ROWS, MOE_BLOCK, D).astype(BF16)
        halves = []
        for wg_ref, wu_ref, wd_ref in ((wga_ref, wua_ref, wda_ref), (wgb_ref, wub_ref, wdb_ref)):
            gate = jnp.dot(xb, wg_ref[0, 0], preferred_element_type=F32)
            up = jnp.dot(xb, wu_ref[0, 0], preferred_element_type=F32)
            hid = (gate * jax.nn.sigmoid(gate) * up).astype(BF16)
            out = jnp.dot(hid, wd_ref[0, 0], preferred_element_type=F32)
            halves.append(lax.bitcast_convert_type(out.astype(BF16).astype(F32), jnp.uint32))
        _store_token_tiles(o_ref, 0, TOKEN_TILE_ROWS, halves[0] | (halves[1] >> 16))

    @pl.when(i >= nused_ref[0])
    def _():
        o_ref[...] = jnp.zeros_like(o_ref)


def _unpack_pair(words):
    hi = lax.bitcast_convert_type(words & jnp.uint32(0xFFFF0000), F32)
    lo = lax.bitcast_convert_type(words << 16, F32)
    return hi, lo


def expert_pairs(xs, n_blocks, blk_a, blk_b, n_used, w_gate, w_up, w_down, layer):
    D, DE = w_gate.shape[2:]
    P = n_blocks * MOE_BLOCK
    wspec = lambda shape, which: pl.BlockSpec(shape, lambda i, ea, eb, nu: (layer, (ea, eb)[which][i], 0, 0))
    grid_spec = pltpu.PrefetchScalarGridSpec(
        num_scalar_prefetch=3,
        grid=(n_blocks,),
        in_specs=[pl.BlockSpec((MOE_BLOCK * TOKEN_TILE_ROWS, LANES), lambda i, ea, eb, nu: (i, 0)),
                  wspec((1, 1, D, DE), 0), wspec((1, 1, D, DE), 0), wspec((1, 1, DE, D), 0),
                  wspec((1, 1, D, DE), 1), wspec((1, 1, D, DE), 1), wspec((1, 1, DE, D), 1)],
        out_specs=pl.BlockSpec((MOE_BLOCK * TOKEN_TILE_ROWS, LANES), lambda i, ea, eb, nu: (i, 0)),
    )
    return pl.pallas_call(
        _expert_pair_kernel,
        grid_spec=grid_spec,
        out_shape=jax.ShapeDtypeStruct((P * TOKEN_TILE_ROWS, LANES), jnp.uint32),
        compiler_params=_cparams("arbitrary"),
        name="moe_experts",
    )(blk_a, blk_b, n_used, xs, w_gate, w_up, w_down, w_gate, w_up, w_down)


def _gather_pairs(idx_ref, src_hbm, buf, sem, n_tokens):
    def body(g, carry):
        for u in range(DMA_UNROLL):
            r = g * DMA_UNROLL + u
            src = pl.multiple_of(idx_ref[0, 0, r] * TOKEN_TILE_ROWS, TOKEN_TILE_ROWS)
            pltpu.make_async_copy(src_hbm.at[pl.ds(src, TOKEN_TILE_ROWS)],
                                  buf.at[pl.ds(r * TOKEN_TILE_ROWS, TOKEN_TILE_ROWS)], sem).start(priority=u % 2)
        return carry
    lax.fori_loop(0, n_tokens // DMA_UNROLL, body, 0)


def _collect_kernel(dest_ref, dest_next_ref, o_hbm, info_ref, h_ref, g2_ref, fg_ref, out_ref, obuf, sems, *,
                    final_norm):
    i = pl.program_id(0)
    n = pl.num_programs(0)
    slot = i % 2
    tb, D = h_ref.shape

    @pl.when(i == 0)
    def _():
        _gather_pairs(dest_ref, o_hbm, obuf.at[0], sems.at[0], tb)

    @pl.when(i + 1 < n)
    def _():
        _gather_pairs(dest_next_ref, o_hbm, obuf.at[1 - slot], sems.at[1 - slot], tb)

    _wait_rows(obuf.at[slot], sems.at[slot])
    e_a, e_b = _unpack_pair(_load_token_tiles(obuf.at[slot], 0, TOKEN_TILE_ROWS, tb, D))
    m = info_ref[:, INFO_WA:INFO_WA + 1] * e_a + info_ref[:, INFO_WB:INFO_WB + 1] * e_b
    h = h_ref[...] + g2_ref[0] * m
    if final_norm:
        ms = jnp.mean(h * h, axis=-1, keepdims=True)
        h = h * lax.rsqrt(ms + EPS) * fg_ref[...]
    out_ref[...] = h


def moe_collect(o_sorted, dest, info, block_offset, h_tokens, g2, tokens_per_batch, final_g, final_norm):
    T, D = h_tokens.shape
    tb = COLLECT_TOKENS
    nt = T // tb
    if g2.shape[0] == 1:
        g2_index = lambda i: 0
    else:
        assert tokens_per_batch % tb == 0
        g2_index = lambda i: i // (tokens_per_batch // tb)
    last = block_offset + nt - 1
    dest3 = dest.reshape(-1, 1, tb)
    return pl.pallas_call(
        functools.partial(_collect_kernel, final_norm=final_norm),
        grid=(nt,),
        in_specs=[pl.BlockSpec((1, 1, tb), lambda i: (block_offset + i, 0, 0), memory_space=pltpu.SMEM),
                  pl.BlockSpec((1, 1, tb), lambda i: (jnp.minimum(block_offset + i + 1, last), 0, 0),
                               memory_space=pltpu.SMEM),
                  pl.BlockSpec(memory_space=pl.ANY),
                  pl.BlockSpec((tb, ROUTER_COLS), lambda i: (block_offset + i, 0)),
                  pl.BlockSpec((tb, D), lambda i: (i, 0)),
                  pl.BlockSpec((1, 1, D), lambda i: (g2_index(i), 0, 0)),
                  pl.BlockSpec((1, D), lambda i: (0, 0))],
        out_specs=pl.BlockSpec((tb, D), lambda i: (i, 0)),
        out_shape=jax.ShapeDtypeStruct((T, D), F32),
        scratch_shapes=[pltpu.VMEM((2, tb * TOKEN_TILE_ROWS, LANES), jnp.uint32), pltpu.SemaphoreType.DMA((2,))],
        compiler_params=_cparams("arbitrary"),
        name="moe_collect",
    )(dest3, dest3, o_sorted, info, h_tokens, g2, final_g.reshape(1, D))


CONV_MARGIN = 16


def _time_chunk(L):
    return min(L, 256)


LANES = 128


def _zero_margins(pad_ref, L):
    zeros = jnp.zeros((CONV_MARGIN, LANES), F32)
    for s in range(pad_ref.shape[0]):
        pad_ref[s, pl.ds(0, CONV_MARGIN), :] = zeros
        pad_ref[s, pl.ds(CONV_MARGIN + L, CONV_MARGIN), :] = zeros


def _dw_conv_slab(pad_ref, s, base, T, w_ref, b_ref, col, taps, pad_left):
    acc = jnp.broadcast_to(b_ref[:, col:col + LANES], (T, LANES))
    for k in range(taps):
        acc = acc + w_ref[k:k + 1, col:col + LANES] * pad_ref[s, pl.ds(base + (CONV_MARGIN - pad_left + k), T), :]
    return acc


def _conformer_kernel(u_ref, w_ref, b_ref, g_ref, beta_ref, o_ref, ypad):
    L = o_ref.shape[1]
    T = _time_chunk(L)
    C = D_GROUP
    n_slabs = C // LANES
    pad = (CONF_KERNEL - 1) // 2
    _zero_margins(ypad, L)

    def glu(j, carry):
        base = pl.multiple_of(j * T, T)
        for s in range(n_slabs):
            a = u_ref[0, pl.ds(base, T), s * LANES:(s + 1) * LANES]
            gate = u_ref[0, pl.ds(base, T), C + s * LANES:C + (s + 1) * LANES]
            ypad[s, pl.ds(CONV_MARGIN + base, T), :] = a * jax.nn.sigmoid(gate)
        return carry
    lax.fori_loop(0, L // T, glu, 0)

    def conv(j, carry):
        base = pl.multiple_of(j * T, T)
        acc = jnp.concatenate([_dw_conv_slab(ypad, s, base, T, w_ref, b_ref, s * LANES, CONF_KERNEL, pad)
                               for s in range(n_slabs)], axis=-1)
        mu = jnp.mean(acc, axis=-1, keepdims=True)
        cen = acc - mu
        var = jnp.mean(cen * cen, axis=-1, keepdims=True)
        y = cen * lax.rsqrt(var + EPS) * g_ref[...] + beta_ref[...]
        o_ref[0, pl.ds(base, T), :] = y * jax.nn.sigmoid(y)
        return carry
    lax.fori_loop(0, L // T, conv, 0)


def conformer_conv(u, w, b, ln_g, ln_b):
    B, L, _ = u.shape
    C = D_GROUP
    vec = pl.BlockSpec((1, C), lambda i: (0, 0))
    return pl.pallas_call(
        _conformer_kernel,
        grid=(B,),
        in_specs=[pl.BlockSpec((1, L, 2 * C), lambda i: (i, 0, 0)),
                  pl.BlockSpec((CONF_KERNEL, C), lambda i: (0, 0)), vec, vec, vec],
        out_specs=pl.BlockSpec((1, L, C), lambda i: (i, 0, 0)),
        out_shape=jax.ShapeDtypeStruct((B, L, C), F32),
        scratch_shapes=[pltpu.VMEM((C // LANES, L + 2 * CONV_MARGIN, LANES), F32)],
        compiler_params=_cparams("parallel"),
        name="conformer_conv",
    )(u, w, b.reshape(1, C), ln_g.reshape(1, C), ln_b.reshape(1, C))


def _gelu_tanh(x):
    return 0.5 * x * (1.0 + jnp.tanh(math.sqrt(2.0 / math.pi) * (x + 0.044715 * (x * x * x))))


def _lru_kernel(uc_ref, ux_ref, cw_ref, cb_ref, wcat_ref, bcat_ref, lam_ref, *rest, need_ctx):
    if need_ctx:
        oc_ref, ox_ref, cpad, xpad, a_s, b_s, yx, yc = rest
    else:
        ox_ref, cpad, xpad, a_s, b_s, yx = rest
        oc_ref = yc = None
    C = D_GROUP
    n_slabs = C // LANES
    Lc = uc_ref.shape[1]
    Lx = ux_ref.shape[1]
    pad_l = (LRU_CONV - 1) // 2

    def fill(pad_ref, u_ref, L):
        T = _time_chunk(L)
        _zero_margins(pad_ref, L)

        def body(j, carry):
            base = pl.multiple_of(j * T, T)
            for s in range(n_slabs):
                pad_ref[s, pl.ds(CONV_MARGIN + base, T), :] = u_ref[0, pl.ds(base, T),
                                                                    C + s * LANES:C + (s + 1) * LANES]
            return carry
        lax.fori_loop(0, L // T, body, 0)

    fill(cpad, uc_ref, Lc)
    fill(xpad, ux_ref, Lx)

    def coeffs(pad_ref, base, T, d):
        x = jnp.concatenate([_dw_conv_slab(pad_ref, s, base, T, cw_ref, cb_ref, s * LANES, LRU_CONV, pad_l)
                             for s in range(n_slabs)], axis=-1)
        t = jnp.tanh(jnp.dot(x.astype(BF16), wcat_ref[:, 2 * d * C:2 * (d + 1) * C],
                             preferred_element_type=F32) + bcat_ref[:, 2 * d * C:2 * (d + 1) * C])
        i = 0.5 * t[:, C:] + 0.5
        z = -lam_ref[d:d + 1, :]
        softplus = jnp.maximum(z, 0.0) + jnp.log(1.0 + jnp.exp(-jnp.abs(z)))
        half_rate = (-0.5 * LRU_C) * softplus
        a = jnp.exp(half_rate * t[:, :C] + half_rate)
        b = jnp.sqrt(1.0 - a * a) * (i * x)
        for s in range(n_slabs):
            a_s[d * n_slabs + s, pl.ds(0, T), :] = a[:, s * LANES:(s + 1) * LANES]
            b_s[d * n_slabs + s, pl.ds(0, T), :] = b[:, s * LANES:(s + 1) * LANES]

    def run(pad_ref, L, h, y_ref):
        T = _time_chunk(L)
        n = L // T

        def chunk(j, h):
            base_f = pl.multiple_of(j * T, T)
            base_b = pl.multiple_of((n - 1 - j) * T, T)
            coeffs(pad_ref, base_f, T, 0)
            coeffs(pad_ref, base_b, T, 1)

            def step(t, h):
                new = []
                for d, (base, row) in enumerate(((base_f, t), (base_b, T - 1 - t))):
                    for s in range(n_slabs):
                        k = d * n_slabs + s
                        hs = a_s[k, pl.ds(row, 1), :] * h[k] + b_s[k, pl.ds(row, 1), :]
                        if y_ref is not None:
                            y_ref[k, pl.ds(base + row, 1), :] = hs
                        new.append(hs)
                return tuple(new)
            return lax.fori_loop(0, T, step, h, unroll=8)
        return lax.fori_loop(0, n, chunk, h)

    h = tuple(jnp.zeros((1, LANES), F32) for _ in range(2 * n_slabs))
    h = run(cpad, Lc, h, yc)
    run(xpad, Lx, h, yx)

    def finish(u_ref, y_ref, o_ref, L):
        T = _time_chunk(L)

        def body(j, carry):
            base = pl.multiple_of(j * T, T)
            y = jnp.concatenate([y_ref[s, pl.ds(base, T), :] + y_ref[n_slabs + s, pl.ds(base, T), :]
                                 for s in range(n_slabs)], axis=-1)
            o_ref[0, pl.ds(base, T), :] = _gelu_tanh(u_ref[0, pl.ds(base, T), :C]) * y
            return carry
        lax.fori_loop(0, L // T, body, 0)

    finish(ux_ref, yx, ox_ref, Lx)
    if need_ctx:
        finish(uc_ref, yc, oc_ref, Lc)


def _block_diag(w):
    H, n, _ = w.shape
    eye = jnp.eye(H, dtype=w.dtype)
    return (eye[:, None, :, None] * w[:, :, None, :]).reshape(H * n, H * n)


def rglru_mixer(uc, ux, lp, need_ctx):
    B, Lc, _ = uc.shape
    Lx = ux.shape[1]
    C = D_GROUP
    wcat = (0.5 * jnp.concatenate([_block_diag(lp["lru_wa"][0]), _block_diag(lp["lru_wx"][0]),
                                   _block_diag(lp["lru_wa"][1]), _block_diag(lp["lru_wx"][1])], axis=1)).astype(BF16)
    bcat = 0.5 * jnp.concatenate([lp["lru_ba"][0], lp["lru_bx"][0], lp["lru_ba"][1], lp["lru_bx"][1]]).reshape(1, 4 * C)
    full = lambda r, c: pl.BlockSpec((r, c), lambda i: (0, 0))
    seq = lambda L, n: pl.BlockSpec((1, L, n), lambda i: (i, 0, 0))
    out_specs = [seq(Lx, C)]
    out_shape = [jax.ShapeDtypeStruct((B, Lx, C), F32)]
    if need_ctx:
        out_specs = [seq(Lc, C)] + out_specs
        out_shape = [jax.ShapeDtypeStruct((B, Lc, C), F32)] + out_shape
    T = _time_chunk(Lx)
    slab = lambda rows, n=1: pltpu.VMEM((n * C // LANES, rows, LANES), F32)
    scratch = [slab(Lc + 2 * CONV_MARGIN), slab(Lx + 2 * CONV_MARGIN), slab(T, 2), slab(T, 2), slab(Lx, 2)]
    if need_ctx:
        scratch.append(slab(Lc, 2))
    res = pl.pallas_call(
        functools.partial(_lru_kernel, need_ctx=need_ctx),
        grid=(B,),
        in_specs=[seq(Lc, 2 * C), seq(Lx, 2 * C), full(LRU_CONV, C), full(1, C), full(C, 4 * C),
                  full(1, 4 * C), full(2, C)],
        out_specs=out_specs,
        out_shape=out_shape,
        scratch_shapes=scratch,
        compiler_params=_cparams("parallel"),
        name="rglru",
    )(uc, ux, lp["lru_conv_w"], lp["lru_conv_b"].reshape(1, C), wcat, bcat, lp["lru_lambda"])
    if need_ctx:
        return res[0], res[1]
    return None, res[0]


HY_SHORT = 3


def _short_conv(pad_ref, base, T, w_ref, b_ref, c0, c1):
    return jnp.concatenate([_dw_conv_slab(pad_ref, col // LANES, base, T, w_ref, b_ref, col, HY_SHORT, 1)
                            for col in range(c0, c1, LANES)], axis=-1)


def _fill_padded(pad_ref, u_ref, L, T):
    _zero_margins(pad_ref, L)

    def body(j, carry):
        base = pl.multiple_of(j * T, T)
        for s in range(pad_ref.shape[0]):
            pad_ref[s, pl.ds(CONV_MARGIN + base, T), :] = u_ref[0, pl.ds(base, T), s * LANES:(s + 1) * LANES]
        return carry
    lax.fori_loop(0, L // T, body, 0)


def _hyena_pre_kernel(u_ref, w_ref, b_ref, z_ref, upad):
    L = u_ref.shape[1]
    T = _time_chunk(L)
    C = D_GROUP
    _fill_padded(upad, u_ref, L, T)

    def body(j, carry):
        base = pl.multiple_of(j * T, T)
        x1 = _short_conv(upad, base, T, w_ref, b_ref, C, 2 * C)
        v = _short_conv(upad, base, T, w_ref, b_ref, 2 * C, 3 * C)
        z_ref[pl.ds(base, T), :] = (x1 * v).astype(BF16)
        return carry
    lax.fori_loop(0, L // T, body, 0)


def _hyena_post_kernel(u_ref, y_ref, w_ref, b_ref, bias_ref, o_ref, upad):
    L = u_ref.shape[1]
    T = _time_chunk(L)
    C = D_GROUP
    _fill_padded(upad, u_ref, L, T)

    def body(j, carry):
        base = pl.multiple_of(j * T, T)
        x0 = _short_conv(upad, base, T, w_ref, b_ref, 0, C)
        x1 = _short_conv(upad, base, T, w_ref, b_ref, C, 2 * C)
        v = _short_conv(upad, base, T, w_ref, b_ref, 2 * C, 3 * C)
        o_ref[0, pl.ds(base, T), :] = x0 * (y_ref[pl.ds(base, T), :] + (x1 * v) * bias_ref[...])
        return carry
    lax.fori_loop(0, L // T, body, 0)


def _spectrum_kernel(f_ref, z_ref, ha_ref, hb_ref, hc_ref, y_ref):
    tf = ha_ref.shape[0]
    acc = jnp.dot(f_ref[...], z_ref[...], preferred_element_type=F32)
    zr = acc[:tf]
    zi = acc[tf:]
    y_ref[:tf, :] = (zr * ha_ref[...] - zi * hb_ref[...]).astype(BF16)
    y_ref[tf:, :] = (zr * hb_ref[...] + zi * hc_ref[...]).astype(BF16)


def _idft_kernel(f_ref, y_ref, o_ref):
    o_ref[...] = jnp.dot(f_ref[...], y_ref[...], preferred_element_type=F32)


def dft_tables(L):
    N = 2 * L
    tf = min(256, L)
    k = jnp.arange(L, dtype=jnp.int32)
    n = jnp.arange(L, dtype=jnp.int32)
    ang = (2.0 * math.pi / N) * ((k[:, None] * n[None, :]) % N).astype(F32)
    cos = jnp.cos(ang)
    sin = jnp.sin(ang)
    nyq = jnp.where(n % 2 == 0, 1.0, -1.0).astype(F32)
    f_re = cos
    f_im = (-sin).at[0].set(nyq)
    fwd = jnp.stack([f_re.reshape(L // tf, tf, L), f_im.reshape(L // tf, tf, L)], axis=1).reshape(N, L)
    ck = jnp.where(k == 0, 1.0, 2.0).astype(F32)[:, None] / N
    i_re = cos * ck
    i_im = (-sin * ck).at[0].set(nyq / N)
    inv = jnp.stack([i_re.reshape(L // tf, tf, L), i_im.reshape(L // tf, tf, L)], axis=1).reshape(N, L).T
    return fwd.astype(BF16), inv.astype(BF16)


def filter_spectrum(k_slabs, fwd):
    n_slabs, N, _ = k_slabs.shape
    L = N // 2
    C = n_slabs * LANES
    tf = min(256, L)
    halves = jnp.concatenate([k_slabs[s, h * L:(h + 1) * L] for h in range(2) for s in range(n_slabs)], axis=1)
    r = pl.pallas_call(
        _idft_kernel,
        grid=(N // (2 * tf), 1),
        in_specs=[pl.BlockSpec((2 * tf, L), lambda i, j: (i, 0)),
                  pl.BlockSpec((L, 2 * C), lambda i, j: (0, 0))],
        out_specs=pl.BlockSpec((2 * tf, 2 * C), lambda i, j: (i, 0)),
        out_shape=jax.ShapeDtypeStruct((N, 2 * C), F32),
        compiler_params=_cparams("parallel", "parallel"),
        name="hyena_filter_dft",
    )(fwd, halves.astype(BF16)).reshape(L // tf, 2, tf, 2 * C)
    sign = jnp.where(jnp.arange(L) % 2 == 0, 1.0, -1.0).astype(F32)[:, None]
    re = r[:, 0].reshape(L, 2 * C)
    im = r[:, 1].reshape(L, 2 * C)
    hr = re[:, :C] + sign * re[:, C:]
    hi = im[:, :C] + sign * im[:, C:]
    return hr, hi.at[0].set(0.0), hr.at[0].set(hi[0])


def hyena_mixer(u, lp, tables):
    B, L, _ = u.shape
    C = D_GROUP
    N = 2 * L
    fwd, inv = tables
    tf = min(256, L)
    T = _time_chunk(L)
    w, bsh = lp["hy_short_w"], lp["hy_short_b"].reshape(1, 3 * C)
    z2 = pl.pallas_call(
        _hyena_pre_kernel,
        grid=(B,),
        in_specs=[pl.BlockSpec((1, L, 3 * C), lambda b: (b, 0, 0)),
                  pl.BlockSpec((HY_SHORT, 3 * C), lambda b: (0, 0)),
                  pl.BlockSpec((1, 3 * C), lambda b: (0, 0))],
        out_specs=pl.BlockSpec((L, C), lambda b: (0, b)),
        out_shape=jax.ShapeDtypeStruct((L, B * C), BF16),
        scratch_shapes=[pltpu.VMEM((3 * C // LANES, L + 2 * CONV_MARGIN, LANES), F32)],
        compiler_params=_cparams("parallel"),
        name="hyena_pre",
    )(u, w, bsh)

    tn = 2 * C
    ha, hb, hc = [jnp.tile(t, (1, tn // C)) for t in filter_spectrum(hyena_filter(L, lp), fwd)]
    hspec = pl.BlockSpec((tf, tn), lambda i, j: (i, 0))
    y2 = pl.pallas_call(
        _spectrum_kernel,
        grid=(L // tf, B * C // tn),
        in_specs=[pl.BlockSpec((2 * tf, L), lambda i, j: (i, 0)),
                  pl.BlockSpec((L, tn), lambda i, j: (0, j)), hspec, hspec, hspec],
        out_specs=pl.BlockSpec((2 * tf, tn), lambda i, j: (i, j)),
        out_shape=jax.ShapeDtypeStruct((N, B * C), BF16),
        compiler_params=_cparams("parallel", "parallel"),
        name="hyena_spectrum",
    )(fwd, z2, ha, hb, hc)

    tl = min(256, L)
    yt = pl.pallas_call(
        _idft_kernel,
        grid=(L // tl, B * C // tn),
        in_specs=[pl.BlockSpec((tl, N), lambda i, j: (i, 0)),
                  pl.BlockSpec((N, tn), lambda i, j: (0, j))],
        out_specs=pl.BlockSpec((tl, tn), lambda i, j: (i, j)),
        out_shape=jax.ShapeDtypeStruct((L, B * C), F32),
        compiler_params=_cparams("parallel", "parallel"),
        name="hyena_idft",
    )(inv, y2)

    return pl.pallas_call(
        _hyena_post_kernel,
        grid=(B,),
        in_specs=[pl.BlockSpec((1, L, 3 * C), lambda b: (b, 0, 0)),
                  pl.BlockSpec((L, C), lambda b: (0, b)),
                  pl.BlockSpec((HY_SHORT, 3 * C), lambda b: (0, 0)),
                  pl.BlockSpec((1, 3 * C), lambda b: (0, 0)),
                  pl.BlockSpec((1, C), lambda b: (0, 0))],
        out_specs=pl.BlockSpec((1, L, C), lambda b: (b, 0, 0)),
        out_shape=jax.ShapeDtypeStruct((B, L, C), F32),
        scratch_shapes=[pltpu.VMEM((3 * C // LANES, L + 2 * CONV_MARGIN, LANES), F32)],
        compiler_params=_cparams("parallel"),
        name="hyena_post",
    )(u, yt, w, bsh, lp["hy_bias"].reshape(1, C))


FFT_N2 = 128
FFT_UNROLL = 8


class _FftPlan:
    def __init__(self, L):
        self.L = L
        self.N = 2 * L
        self.N1 = self.N // FFT_N2
        self.KH = self.N1 // 2 + 1
        self.KP = -(-self.KH // 8) * 8
        self.PA = 2 * self.KP + 4


def fft_tables(L):
    p = _FftPlan(L)
    N, N1, KH, KP = p.N, p.N1, p.KH, p.KP
    n2 = jnp.arange(FFT_N2, dtype=jnp.int32)
    k1 = jnp.arange(KP, dtype=jnp.int32)
    n1 = jnp.arange(N1, dtype=jnp.int32)
    n = FFT_N2 * n1[None, None, :] + n2[:, None, None]
    ang = (2.0 * math.pi / N) * ((k1[None, :, None] * n) % N).astype(F32)
    keep = (k1 < KH)[None, :, None]
    g_re = jnp.where(keep, jnp.cos(ang), 0.0)
    g_im = jnp.where(keep, -jnp.sin(ang), 0.0)
    ga_full = jnp.concatenate([g_re, g_im], axis=1)
    ck = jnp.where((k1 == 0) | (k1 == N1 // 2), 1.0, 2.0) / N
    ga_inv = jnp.swapaxes(ga_full[:, :, :N1 // 2] * jnp.tile(ck, 2)[None, :, None], 1, 2)
    kk = jnp.arange(FFT_N2, dtype=jnp.int32)
    ang2 = (2.0 * math.pi / FFT_N2) * ((kk[:, None] * kk[None, :]) % FFT_N2).astype(F32)
    fr, fi = jnp.cos(ang2), -jnp.sin(ang2)
    fb = jnp.block([[fr, -fi], [fi, fr]])
    fb_inv = jnp.block([[fr, fi], [-fi, fr]])
    return dict(ga_half=ga_full[:, :, :N1 // 2].astype(BF16), ga_full=ga_full.astype(BF16),
                ga_inv=ga_inv.astype(BF16), fb=fb.astype(BF16), fb_inv=fb_inv.astype(BF16))


def _fft_stage_a(x_ref, ga_ref, s_ref, plan, n1_count):
    n_slabs = x_ref.shape[0]

    def body(n2, carry):
        xs = jnp.concatenate([x_ref[s, pl.ds(n2, n1_count, stride=FFT_N2), :] for s in range(n_slabs)], axis=-1)
        a = jnp.dot(ga_ref[n2], xs.astype(BF16), preferred_element_type=F32)
        for s in range(n_slabs):
            s_ref[s, pl.ds(n2 * plan.PA, 2 * plan.KP), :] = a[:, s * LANES:(s + 1) * LANES]
        return carry
    lax.fori_loop(0, FFT_N2, body, 0, unroll=FFT_UNROLL)


def _fft_load_k1(s_ref, k1, plan):
    n_slabs = s_ref.shape[0]
    re = jnp.concatenate([s_ref[s, pl.ds(k1, FFT_N2, stride=plan.PA), :] for s in range(n_slabs)], axis=-1)
    im = jnp.concatenate([s_ref[s, pl.ds(plan.KP + k1, FFT_N2, stride=plan.PA), :] for s in range(n_slabs)], axis=-1)
    return jnp.concatenate([re, im], axis=0).astype(BF16)


def _fft_filter_kernel(k_ref, ga_ref, fb_ref, h_ref, s_ref, *, plan):
    _fft_stage_a(k_ref, ga_ref, s_ref, plan, plan.N1)

    def body(k1, carry):
        h_ref[k1] = jnp.dot(fb_ref[...], _fft_load_k1(s_ref, k1, plan), preferred_element_type=F32).astype(BF16)
        return carry
    lax.fori_loop(0, plan.KH, body, 0)


def _fft_conv_kernel(z_ref, ga_ref, gi_ref, fb_ref, fbi_ref, h_ref, y_ref, s_ref, *, plan):
    zs = z_ref.at[0]
    ys = y_ref.at[0]
    n_slabs = zs.shape[0]
    half = FFT_N2
    _fft_stage_a(zs, ga_ref, s_ref, plan, plan.N1 // 2)

    def body_b(k1, carry):
        x = jnp.dot(fb_ref[...], _fft_load_k1(s_ref, k1, plan), preferred_element_type=F32)
        h = h_ref[k1].astype(F32)
        xr, xi, hr, hi = x[:half], x[half:], h[:half], h[half:]
        y = jnp.concatenate([xr * hr - xi * hi, xr * hi + xi * hr], axis=0).astype(BF16)
        b = jnp.dot(fbi_ref[...], y, preferred_element_type=F32)
        for s in range(n_slabs):
            s_ref[s, pl.ds(k1, FFT_N2, stride=plan.PA), :] = b[:half, s * LANES:(s + 1) * LANES]
            s_ref[s, pl.ds(plan.KP + k1, FFT_N2, stride=plan.PA), :] = b[half:, s * LANES:(s + 1) * LANES]
        return carry
    lax.fori_loop(0, plan.KH, body_b, 0, unroll=3)

    def body_a(n2, carry):
        b = jnp.concatenate([s_ref[s, pl.ds(n2 * plan.PA, 2 * plan.KP), :] for s in range(n_slabs)], axis=-1)
        y = jnp.dot(gi_ref[n2], b.astype(BF16), preferred_element_type=F32)
        for s in range(n_slabs):
            ys[s, pl.ds(n2, plan.N1 // 2, stride=FFT_N2), :] = y[:, s * LANES:(s + 1) * LANES]
        return carry
    lax.fori_loop(0, FFT_N2, body_a, 0, unroll=FFT_UNROLL)


def fft_filter_spectrum(k, tabs):
    n_slabs, N, _ = k.shape
    plan = _FftPlan(N // 2)
    C = n_slabs * LANES
    full = lambda shape: pl.BlockSpec(shape, lambda i: (0,) * len(shape))
    return pl.pallas_call(
        functools.partial(_fft_filter_kernel, plan=plan),
        grid=(1,),
        in_specs=[full((n_slabs, plan.N, LANES)), full((FFT_N2, 2 * plan.KP, plan.N1)),
                  full((2 * FFT_N2, 2 * FFT_N2))],
        out_specs=full((plan.KH, 2 * FFT_N2, C)),
        out_shape=jax.ShapeDtypeStruct((plan.KH, 2 * FFT_N2, C), BF16),
        scratch_shapes=[pltpu.VMEM((n_slabs, FFT_N2 * plan.PA, LANES), F32)],
        compiler_params=_cparams("arbitrary"),
        name="hyena_filter_fft",
    )(k, tabs["ga_full"], tabs["fb"])


def fft_long_conv(z, h_spec, tabs):
    B, n_slabs, L, _ = z.shape
    plan = _FftPlan(L)
    C = n_slabs * LANES
    full = lambda shape: pl.BlockSpec(shape, lambda b: (0,) * len(shape))
    seq = pl.BlockSpec((1, n_slabs, L, LANES), lambda b: (b, 0, 0, 0))
    return pl.pallas_call(
        functools.partial(_fft_conv_kernel, plan=plan),
        grid=(B,),
        in_specs=[seq, full((FFT_N2, 2 * plan.KP, plan.N1 // 2)), full((FFT_N2, plan.N1 // 2, 2 * plan.KP)),
                  full((2 * FFT_N2, 2 * FFT_N2)), full((2 * FFT_N2, 2 * FFT_N2)),
                  full((plan.KH, 2 * FFT_N2, C))],
        out_specs=seq,
        out_shape=jax.ShapeDtypeStruct((B, n_slabs, L, LANES), F32),
        scratch_shapes=[pltpu.VMEM((n_slabs, FFT_N2 * plan.PA, LANES), F32)],
        compiler_params=_cparams("parallel"),
        name="hyena_fft_conv",
    )(z, tabs["ga_half"], tabs["ga_inv"], tabs["fb"], tabs["fb_inv"], h_spec)


def _hyena_pre_slab_kernel(u_ref, w_ref, b_ref, z_ref, upad):
    L = u_ref.shape[1]
    T = _time_chunk(L)
    C = D_GROUP
    _fill_padded(upad, u_ref, L, T)

    def body(j, carry):
        base = pl.multiple_of(j * T, T)
        for s in range(C // LANES):
            x1 = _dw_conv_slab(upad, C // LANES + s, base, T, w_ref, b_ref, C + s * LANES, HY_SHORT, 1)
            v = _dw_conv_slab(upad, 2 * C // LANES + s, base, T, w_ref, b_ref, 2 * C + s * LANES, HY_SHORT, 1)
            z_ref[0, s, pl.ds(base, T), :] = x1 * v
        return carry
    lax.fori_loop(0, L // T, body, 0)


def _hyena_post_slab_kernel(u0_ref, z_ref, y_ref, w_ref, b_ref, bias_ref, o_ref, upad):
    L = u0_ref.shape[1]
    T = _time_chunk(L)
    C = D_GROUP
    _fill_padded(upad, u0_ref, L, T)

    def body(j, carry):
        base = pl.multiple_of(j * T, T)
        x0 = _short_conv(upad, base, T, w_ref, b_ref, 0, C)
        z = jnp.concatenate([z_ref[0, s, pl.ds(base, T), :] for s in range(C // LANES)], axis=-1)
        y = jnp.concatenate([y_ref[0, s, pl.ds(base, T), :] for s in range(C // LANES)], axis=-1)
        o_ref[0, pl.ds(base, T), :] = x0 * (y + z * bias_ref[...])
        return carry
    lax.fori_loop(0, L // T, body, 0)


def hyena_mixer_fft(u, lp, tabs):
    B, L, _ = u.shape
    C = D_GROUP
    n_slabs = C // LANES
    w, bsh = lp["hy_short_w"], lp["hy_short_b"].reshape(1, 3 * C)
    useq = pl.BlockSpec((1, L, 3 * C), lambda b: (b, 0, 0))
    slabs = pl.BlockSpec((1, n_slabs, L, LANES), lambda b: (b, 0, 0, 0))
    wspec = pl.BlockSpec((HY_SHORT, 3 * C), lambda b: (0, 0))
    bspec = pl.BlockSpec((1, 3 * C), lambda b: (0, 0))
    pad_scratch = pltpu.VMEM((3 * C // LANES, L + 2 * CONV_MARGIN, LANES), F32)
    z = pl.pallas_call(
        _hyena_pre_slab_kernel,
        grid=(B,),
        in_specs=[useq, wspec, bspec],
        out_specs=slabs,
        out_shape=jax.ShapeDtypeStruct((B, n_slabs, L, LANES), F32),
        scratch_shapes=[pad_scratch],
        compiler_params=_cparams("parallel"),
        name="hyena_pre",
    )(u, w, bsh)
    y = fft_long_conv(z, fft_filter_spectrum(hyena_filter(L, lp), tabs), tabs)
    return pl.pallas_call(
        _hyena_post_slab_kernel,
        grid=(B,),
        in_specs=[pl.BlockSpec((1, L, C), lambda b: (b, 0, 0)), slabs, slabs, wspec, bspec,
                  pl.BlockSpec((1, C), lambda b: (0, 0))],
        out_specs=pl.BlockSpec((1, L, C), lambda b: (b, 0, 0)),
        out_shape=jax.ShapeDtypeStruct((B, L, C), F32),
        scratch_shapes=[pltpu.VMEM((n_slabs, L + 2 * CONV_MARGIN, LANES), F32)],
        compiler_params=_cparams("parallel"),
        name="hyena_post",
    )(u, z, y, w, bsh, lp["hy_bias"].reshape(1, C))


def _filter_gen_kernel(z_ref, zr_ref, w1_ref, b1_ref, w2_ref, b2_ref, w3_ref, dl_ref, k_ref, nrm):
    L = z_ref.shape[0]
    T = min(L, 512)
    C = D_GROUP
    hi = lax.Precision.HIGHEST

    def decayed(zz):
        h1 = jnp.sin(jnp.dot(zz, w1_ref[...], preferred_element_type=F32, precision=hi) + b1_ref[...])
        h2 = jnp.sin(jnp.dot(h1, w2_ref[...], preferred_element_type=F32, precision=hi) + b2_ref[...])
        h = jnp.dot(h2, w3_ref[...], preferred_element_type=F32, precision=hi)
        decay = jnp.exp(-zz[:, 0:1] * dl_ref[...])
        return h[:, :C] * decay, h[:, C:] * decay

    nrm[...] = jnp.zeros_like(nrm)

    def emit(j, carry):
        base = pl.multiple_of(j * T, T)
        fwd, _ = decayed(z_ref[pl.ds(base, T), :])
        _, bwd_rev = decayed(zr_ref[pl.ds(base, T), :])
        nrm[...] = nrm[...] + jnp.sum(jnp.abs(fwd) + jnp.abs(bwd_rev), axis=0, keepdims=True)
        for s in range(C // LANES):
            k_ref[s, pl.ds(base, T), :] = fwd[:, s * LANES:(s + 1) * LANES]
            k_ref[s, pl.ds(L + base, T), :] = bwd_rev[:, s * LANES:(s + 1) * LANES]
        return carry
    lax.fori_loop(0, L // T, emit, 0)

    inv = 1.0 / (nrm[...] + EPS)

    def scale(j, carry):
        base = pl.multiple_of(j * T, T)
        row = base + lax.broadcasted_iota(jnp.int32, (T, 1), 0)
        for s in range(C // LANES):
            inv_s = inv[:, s * LANES:(s + 1) * LANES]
            k_ref[s, pl.ds(base, T), :] = k_ref[s, pl.ds(base, T), :] * inv_s
            k_ref[s, pl.ds(L + base, T), :] = jnp.where(row == 0, 0.0, k_ref[s, pl.ds(L + base, T), :] * inv_s)
        return carry
    lax.fori_loop(0, L // T, scale, 0)


def hyena_filter(L, lp):
    C = D_GROUP
    n_slabs = C // LANES
    t = jnp.linspace(0.0, 1.0, L, dtype=F32)[:, None]
    bands = (HY_EMB - 1) // 2
    w = 2.0 * math.pi * jnp.arange(L, dtype=F32)[:, None] / L
    f = jnp.linspace(1e-4, bands - 1, bands, dtype=F32)[None]
    z = jnp.concatenate([t, jnp.cos(f * w), -jnp.sin(f * w)], axis=-1)
    z = jnp.pad(z, ((0, 0), (0, LANES - HY_EMB)))
    z_rev = jnp.concatenate([z[:1], z[1:][::-1]], axis=0)
    w1 = jnp.pad(lp["hy_ffn_w1"], ((0, LANES - HY_EMB), (0, 0)))
    H = w1.shape[1]
    max_decay = math.log(HY_TARGET) / HY_FAST_DECAY
    min_decay = math.log(HY_TARGET) / HY_SLOW_DECAY
    abs_deltas = jnp.abs(jnp.linspace(min_decay, max_decay, C, dtype=F32)).reshape(1, C)
    full = lambda *shape: pl.BlockSpec(shape, lambda i: (0,) * len(shape))
    return pl.pallas_call(
        _filter_gen_kernel,
        grid=(1,),
        in_specs=[full(L, LANES), full(L, LANES), full(LANES, H), full(1, H), full(H, H), full(1, H),
                  full(H, 2 * C), full(1, C)],
        out_specs=full(n_slabs, 2 * L, LANES),
        out_shape=jax.ShapeDtypeStruct((n_slabs, 2 * L, LANES), F32),
        scratch_shapes=[pltpu.VMEM((1, C), F32)],
        compiler_params=_cparams("arbitrary"),
        name="hyena_filter_gen",
    )(z, z_rev, w1, lp["hy_ffn_b1"].reshape(1, H), lp["hy_ffn_w2"], lp["hy_ffn_b2"].reshape(1, H),
      lp["hy_ffn_w3"], abs_deltas)


def _layer(hc, hx, c_silu_all, lp, need_ctx, final_g, final_norm, tables_x, tables_c, experts, layer):
    B, S, D = hx.shape
    C = hc.shape[1]
    mod = small_linear(c_silu_all, lp["ada_w"], lp["ada_b"])
    mod_x = mod[:B].reshape(B, 6, 1, D)
    mod_c = jnp.broadcast_to(mod[B].reshape(1, 6, 1, D), (B, 6, 1, D))
    w_ext = extend_w_in(lp["w_in"])
    cos_x, sin_x = rope_tables(S, True)
    cos_c, sin_c = rope_tables(C, False)
    hy_x, cf_x, at_x, lr_x = in_proj(hx, mod_x[:, 0], mod_x[:, 1], lp["norm1_g"], w_ext, cos_x, sin_x, tm=512)
    hy_c, cf_c, at_c, lr_c = in_proj(hc, mod_c[:, 0], mod_c[:, 1], lp["norm1_g"], w_ext, cos_c, sin_c, tm=256)

    yd_c, yd_x = rglru_mixer(lr_c, lr_x, lp, need_ctx)
    conf = lambda u: conformer_conv(u, lp["conf_dw_w"], lp["conf_dw_b"], lp["conf_ln_g"], lp["conf_ln_b"])
    ys_x = [hyena_mixer_fft(hy_x, lp, tables_x), conf(cf_x),
            window_attention(at_x, at_c, lp["attn_sink"]), yd_x]

    w_out = lp["w_out"].astype(BF16)
    w_router = jnp.zeros((D, ROUTER_COLS), F32)
    w_router = w_router.at[:, :N_GROUPS].set(lp["router_g_w"]).at[:, N_GROUPS:N_GROUPS + N_EXPERTS].set(lp["router_e_w"])
    w_router = w_router.astype(BF16)
    b_router = jnp.zeros((1, ROUTER_COLS), F32)
    b_router = b_router.at[0, :N_GROUPS].set(lp["router_g_b"]).at[0, N_GROUPS:N_GROUPS + N_EXPERTS].set(lp["router_e_b"])

    hx1, lg_x = out_proj(ys_x, hx, mod_x[:, 2], lp["group_norm_g"], w_out, lp["norm2_g"],
                         mod_x[:, 3], mod_x[:, 4], w_router, b_router, tm=512)
    h_tok = hx1.reshape(B * S, D)
    hc_tok = None
    lg = lg_x.reshape(B * S, ROUTER_COLS)
    if need_ctx:
        ys_c = [hyena_mixer(hy_c, lp, tables_c), conf(cf_c),
                context_attention(at_c, lp["attn_sink"]), yd_c]
        hc1, lg_c = out_proj(ys_c, hc, mod_c[:, 2], lp["group_norm_g"], w_out, lp["norm2_g"],
                             mod_c[:, 3], mod_c[:, 4], w_router, b_router, tm=256)
        hc_tok = hc1.reshape(B * C, D)
        lg = jnp.concatenate([lg, lg_c.reshape(B * C, ROUTER_COLS)], axis=0)

    T = lg.shape[0]
    n_blocks = -(-T // MOE_BLOCK) + N_CLASSES
    info, counts, ids, zero_slots = route_tokens(lg, n_blocks * MOE_BLOCK * TOKEN_TILE_ROWS)
    dest, blk_a, blk_b, n_used = slot_plan(ids, counts, n_blocks)
    xs = moe_dispatch(h_tok, hc_tok, dest, zero_slots, lp["norm2_g"], (mod_x[:, 3], mod_x[:, 4]),
                      (mod_c[:, 3], mod_c[:, 4]), S)
    o_sorted = expert_pairs(xs, n_blocks, blk_a, blk_b, n_used, *experts, layer)
    hx2 = moe_collect(o_sorted, dest, info, 0, h_tok, mod_x[:, 5], S, final_g, final_norm)
    hx2 = hx2.reshape(B, S, D)
    if need_ctx:
        hc2 = moe_collect(o_sorted, dest, info, B * S // COLLECT_TOKENS, hc_tok, mod_c[:1, 5], C,
                          final_g, False).reshape(B, C, D)
    else:
        hc2 = hc
    return hc2, hx2


def kernel(x, c, ctx, c_ctx, norm1_g, norm2_g, ada_w, ada_b, w_in, hy_short_w, hy_short_b, hy_ffn_w1, hy_ffn_b1, hy_ffn_w2, hy_ffn_b2, hy_ffn_w3, hy_bias, conf_dw_w, conf_dw_b, conf_ln_g, conf_ln_b, attn_sink, lru_conv_w, lru_conv_b, lru_wa, lru_ba, lru_wx, lru_bx, lru_lambda, group_norm_g, w_out, router_g_w, router_g_b, router_e_w, router_e_b, exp_w_gate, exp_w_up, exp_w_down, final_norm_g):
    stacked = dict(norm1_g=norm1_g, norm2_g=norm2_g, ada_w=ada_w, ada_b=ada_b, w_in=w_in,
                   hy_short_w=hy_short_w, hy_short_b=hy_short_b, hy_ffn_w1=hy_ffn_w1, hy_ffn_b1=hy_ffn_b1,
                   hy_ffn_w2=hy_ffn_w2, hy_ffn_b2=hy_ffn_b2, hy_ffn_w3=hy_ffn_w3, hy_bias=hy_bias,
                   conf_dw_w=conf_dw_w, conf_dw_b=conf_dw_b, conf_ln_g=conf_ln_g, conf_ln_b=conf_ln_b,
                   attn_sink=attn_sink, lru_conv_w=lru_conv_w, lru_conv_b=lru_conv_b, lru_wa=lru_wa,
                   lru_ba=lru_ba, lru_wx=lru_wx, lru_bx=lru_bx, lru_lambda=lru_lambda,
                   group_norm_g=group_norm_g, w_out=w_out, router_g_w=router_g_w, router_g_b=router_g_b,
                   router_e_w=router_e_w, router_e_b=router_e_b)
    experts = (exp_w_gate.astype(BF16), exp_w_up.astype(BF16), exp_w_down.astype(BF16))
    depth = norm1_g.shape[0]
    B = x.shape[0]
    cs = jnp.concatenate([jax.nn.silu(c), jnp.broadcast_to(jax.nn.silu(c_ctx)[None], (8, c.shape[1]))], axis=0)
    hc, hx = ctx, x
    tables_x = fft_tables(x.shape[1])
    tables_c = dft_tables(ctx.shape[1])
    for l in range(depth):
        lp = {k: v[l] for k, v in stacked.items()}
        hc, hx = _layer(hc, hx, cs, lp, need_ctx=(l < depth - 1), final_g=final_norm_g,
                        final_norm=(l == depth - 1), tables_x=tables_x, tables_c=tables_c,
                        experts=experts, layer=l)
    return hx
```

```python
import functools
import math

import jax
import jax.numpy as jnp
from jax import lax
from jax.experimental import pallas as pl
from jax.experimental.pallas import tpu as pltpu

F32 = jnp.float32
BF16 = jnp.bfloat16

EPS = 1e-6
NEG_INF = -1e30
GRID_W = 64
D_GROUP = 256
HY_COLS = 3 * D_GROUP
CONF_COLS = 2 * D_GROUP
ATT_HEADS = 4
ATT_KV_HEADS = 2
HEAD_DIM = 64
ATT_COLS = (ATT_HEADS + 2 * ATT_KV_HEADS) * HEAD_DIM
LRU_COLS = 2 * D_GROUP
QK_COLS = (ATT_HEADS + ATT_KV_HEADS) * HEAD_DIM
WINDOW = 128
ATT_BLOCK = 128
ROPE_BASE = 10000.0
HY_EMB = 33
HY_FAST_DECAY = 0.3
HY_SLOW_DECAY = 1.5
HY_TARGET = 1e-2
CONF_KERNEL = 31
LRU_CONV = 4
LRU_C = 8.0
N_GROUPS = 4
EXP_PER_GROUP = 8
N_EXPERTS = N_GROUPS * EXP_PER_GROUP
MOE_BLOCK = 256
ROUTER_COLS = 128

VMEM_LIMIT_BYTES = 56 * 1024 * 1024


def _cparams(*sem):
    return pltpu.CompilerParams(dimension_semantics=sem, vmem_limit_bytes=VMEM_LIMIT_BYTES)


def _linear_kernel(x_ref, w_ref, b_ref, o_ref):
    o_ref[...] = jnp.dot(x_ref[...], w_ref[...], preferred_element_type=F32,
                         precision=lax.Precision.HIGHEST) + b_ref[...]


def small_linear(x, w, b, tn=1024):
    M, K = x.shape
    N = w.shape[1]
    return pl.pallas_call(
        _linear_kernel,
        grid=(N // tn,),
        in_specs=[pl.BlockSpec((M, K), lambda j: (0, 0)),
                  pl.BlockSpec((K, tn), lambda j: (0, j)),
                  pl.BlockSpec((1, tn), lambda j: (0, j))],
        out_specs=pl.BlockSpec((M, tn), lambda j: (0, j)),
        out_shape=jax.ShapeDtypeStruct((M, N), F32),
        compiler_params=_cparams("parallel"),
        name="ada_linear",
    )(x, w, b.reshape(1, N))


def _in_proj_kernel(x_ref, sh_ref, sc_ref, g_ref, w_ref, cos_ref, sin_ref,
                    hy_ref, cf_ref, at_ref, lr_ref, cast_refs=()):
    x = x_ref[0]
    ms = jnp.mean(x * x, axis=-1, keepdims=True)
    y = x * lax.rsqrt(ms + EPS) * g_ref[...]
    y = y * (1.0 + sc_ref[0]) + sh_ref[0]
    u = jnp.dot(y.astype(BF16), w_ref[...], preferred_element_type=F32)
    c0 = HY_COLS
    c1 = c0 + CONF_COLS
    c2 = c1 + ATT_COLS
    c3 = c2 + LRU_COLS
    hy_ref[0] = u[:, :c0]
    cf_ref[0] = u[:, c0:c1]
    lr_ref[0] = u[:, c2:c3]
    qk = u[:, c1:c1 + QK_COLS]
    qk_rot = u[:, c3:c3 + QK_COLS]
    at_ref[0, :, :QK_COLS] = qk * cos_ref[...] + qk_rot * sin_ref[...]
    at_ref[0, :, QK_COLS:] = u[:, c1 + QK_COLS:c2]
    for wf_ref, wb_ref in zip(cast_refs[:len(cast_refs) // 2], cast_refs[len(cast_refs) // 2:]):
        wb_ref[...] = wf_ref[...].astype(BF16)


def _in_proj_body(x_ref, sh_ref, sc_ref, g_ref, w_ref, cos_ref, sin_ref, *refs, n_cast):
    _in_proj_kernel(x_ref, sh_ref, sc_ref, g_ref, w_ref, cos_ref, sin_ref, *refs[n_cast:n_cast + 4],
                    cast_refs=refs[:n_cast] + refs[n_cast + 4:])


def can_fuse_weight_cast(n_steps, weights):
    for w in weights:
        E, R = w.shape[1:3]
        if n_steps % E or R % (n_steps // E) or (R // (n_steps // E)) % 16:
            return False
    return True


def in_proj(h, shift, scale, g, w_ext, cos_t, sin_t, tm, cast_weights=(), layer=0):
    B, L, D = h.shape
    NW = w_ext.shape[1]
    n_i = L // tm
    outs = [HY_COLS, CONF_COLS, ATT_COLS, LRU_COLS]
    cast_in, cast_out, cast_shape = [], [], []
    for w in cast_weights:
        E, R, Cw = w.shape[1:]
        per_e = B * n_i // E
        chunk = (1, 1, R // per_e, Cw)
        cast_in.append(pl.BlockSpec(chunk, lambda b, i, per_e=per_e: (layer, (b * n_i + i) // per_e,
                                                                        (b * n_i + i) % per_e, 0)))
        cast_out.append(pl.BlockSpec(chunk, lambda b, i, per_e=per_e: (0, (b * n_i + i) // per_e,
                                                                         (b * n_i + i) % per_e, 0)))
        cast_shape.append(jax.ShapeDtypeStruct((1, E, R, Cw), BF16))
    res = pl.pallas_call(
        functools.partial(_in_proj_body, n_cast=len(cast_weights)),
        grid=(B, n_i),
        in_specs=[pl.BlockSpec((1, tm, D), lambda b, i: (b, i, 0)),
                  pl.BlockSpec((1, 1, D), lambda b, i: (b, 0, 0)),
                  pl.BlockSpec((1, 1, D), lambda b, i: (b, 0, 0)),
                  pl.BlockSpec((1, D), lambda b, i: (0, 0)),
                  pl.BlockSpec((D, NW), lambda b, i: (0, 0)),
                  pl.BlockSpec((tm, QK_COLS), lambda b, i: (i, 0)),
                  pl.BlockSpec((tm, QK_COLS), lambda b, i: (i, 0))] + cast_in,
        out_specs=[pl.BlockSpec((1, tm, n), lambda b, i: (b, i, 0)) for n in outs] + cast_out,
        out_shape=[jax.ShapeDtypeStruct((B, L, n), F32) for n in outs] + cast_shape,
        compiler_params=_cparams("arbitrary", "arbitrary") if cast_weights else _cparams("parallel", "parallel"),
        name="in_proj",
    )(h, shift, scale, g.reshape(1, D), w_ext, cos_t, sin_t, *cast_weights)
    return res[:4], tuple(res[4:])


def rope_tables(L, rotary):
    n_heads = ATT_HEADS + ATT_KV_HEADS
    if not rotary:
        return jnp.ones((L, QK_COLS), F32), jnp.zeros((L, QK_COLS), F32)
    pos = jnp.arange(L)
    row = (pos // GRID_W).astype(F32)
    col = (pos % GRID_W).astype(F32)
    half = HEAD_DIM // 2
    inv_freq = ROPE_BASE ** (-jnp.arange(0, half, 2, dtype=F32) / half)
    ang_r = row[:, None] * inv_freq[None]
    ang_c = col[:, None] * inv_freq[None]
    cos_h = jnp.concatenate([jnp.cos(ang_r)] * 2 + [jnp.cos(ang_c)] * 2, axis=-1)
    sin_h = jnp.concatenate([jnp.sin(ang_r)] * 2 + [jnp.sin(ang_c)] * 2, axis=-1)
    return jnp.tile(cos_h, (1, n_heads)), jnp.tile(sin_h, (1, n_heads))


def extend_w_in(w_in):
    c1 = HY_COLS + CONF_COLS
    wqk = w_in[:, c1:c1 + QK_COLS]
    D = w_in.shape[0]
    w4 = wqk.reshape(D, QK_COLS // 32, 2, 16)
    wrot = jnp.stack([-w4[:, :, 1], w4[:, :, 0]], axis=2).reshape(D, QK_COLS)
    return jnp.concatenate([w_in, wrot], axis=1).astype(BF16)


def _softmax_parts(q, k_list, extra_logit):
    scale = HEAD_DIM ** -0.5
    s_list = []
    for k, mask in k_list:
        s = lax.dot_general(q, k, (((1,), (1,)), ((), ())), preferred_element_type=F32) * scale
        if mask is not None:
            s = jnp.where(mask, s, NEG_INF)
        s_list.append(s)
    m = extra_logit
    for s in s_list:
        m = jnp.maximum(m, jnp.max(s, axis=-1, keepdims=True))
    p_list = [jnp.exp(s - m) for s in s_list]
    denom = jnp.exp(extra_logit - m)
    for p in p_list:
        denom = denom + jnp.sum(p, axis=-1, keepdims=True)
    return p_list, 1.0 / denom


ATT_Q_BLOCKS = 16


def _win_attn_kernel(sink_ref, q_ref, kp_ref, kc_ref, kn_ref, vp_ref, vc_ref, vn_ref,
                     kx_ref, vx_ref, o_ref, *, seq_len):
    i = pl.program_id(1)
    blk = ATT_BLOCK
    qb = q_ref.shape[1] // blk
    scale = HEAD_DIM ** -0.5
    g = ATT_HEADS // ATT_KV_HEADS
    kw = jnp.concatenate([kp_ref[0], kc_ref[0], kn_ref[0]], axis=0)
    vw = jnp.concatenate([vp_ref[0], vc_ref[0], vn_ref[0]], axis=0).astype(BF16)
    kwt = kw.T.astype(BF16)
    kxt = kx_ref[0].T.astype(BF16)
    vx = vx_ref[0].astype(BF16)
    row = lax.broadcasted_iota(jnp.int32, (g * blk, 3 * blk), 0) % blk
    col = lax.broadcasted_iota(jnp.int32, (g * blk, 3 * blk), 1)
    band_bias = jnp.where(jnp.abs(col - blk - row) <= WINDOW, 0.0, NEG_INF)
    col1 = lax.broadcasted_iota(jnp.int32, (1, 3 * blk), 1)
    for j in range(qb):
        q_blk = i * qb + j
        k_pos = (q_blk - 1) * blk + col1
        edge_bias = jnp.where(k_pos >= 0, jnp.where(k_pos < seq_len, 0.0, NEG_INF), NEG_INF)
        bias = band_bias + edge_bias
        outs = []
        for kv in range(ATT_KV_HEADS):
            ksl = slice(kv * HEAD_DIM, (kv + 1) * HEAD_DIM)
            heads = range(kv * g, (kv + 1) * g)
            qs = (jnp.concatenate([q_ref[0, j * blk:(j + 1) * blk, h * HEAD_DIM:(h + 1) * HEAD_DIM]
                                   for h in heads], axis=0) * scale).astype(BF16)
            sink = jnp.concatenate([jnp.full((blk, 1), sink_ref[h], F32) for h in heads], axis=0)
            s_win = jnp.dot(qs, kwt[ksl, j * blk:(j + 3) * blk], preferred_element_type=F32) + bias
            s_ctx = jnp.dot(qs, kxt[ksl, :], preferred_element_type=F32)
            m = jnp.maximum(jnp.maximum(jnp.max(s_win, axis=-1, keepdims=True),
                                        jnp.max(s_ctx, axis=-1, keepdims=True)), sink)
            p_win = jnp.exp(s_win - m)
            p_ctx = jnp.exp(s_ctx - m)
            denom = (jnp.exp(sink - m) + jnp.sum(p_win, axis=-1, keepdims=True)
                     + jnp.sum(p_ctx, axis=-1, keepdims=True))
            o = (jnp.dot(p_win.astype(BF16), vw[j * blk:(j + 3) * blk, ksl], preferred_element_type=F32)
                 + jnp.dot(p_ctx.astype(BF16), vx[:, ksl], preferred_element_type=F32)) * (1.0 / denom)
            outs.extend([o[k * blk:(k + 1) * blk] for k in range(g)])
        o_ref[0, j * blk:(j + 1) * blk, :] = jnp.concatenate(outs, axis=-1)


def window_attention(at_x, at_c, sink):
    B, S, _ = at_x.shape
    C = at_c.shape[1]
    blk = ATT_BLOCK
    qb = ATT_Q_BLOCKS
    nb = S // blk
    kcol = QK_COLS // 128 - 1
    vcol = kcol + 1

    def edge_spec(col, off):
        return pl.BlockSpec((1, blk, 128), lambda b, i, s: (b, jnp.clip(i * qb + off, 0, nb - 1), col))

    def mid_spec(col):
        return pl.BlockSpec((1, qb * blk, 128), lambda b, i, s: (b, i, col))

    grid_spec = pltpu.PrefetchScalarGridSpec(
        num_scalar_prefetch=1,
        grid=(B, nb // qb),
        in_specs=[pl.BlockSpec((1, qb * blk, ATT_HEADS * HEAD_DIM), lambda b, i, s: (b, i, 0)),
                  edge_spec(kcol, -1), mid_spec(kcol), edge_spec(kcol, qb),
                  edge_spec(vcol, -1), mid_spec(vcol), edge_spec(vcol, qb),
                  pl.BlockSpec((1, C, 128), lambda b, i, s: (b, 0, kcol)),
                  pl.BlockSpec((1, C, 128), lambda b, i, s: (b, 0, vcol))],
        out_specs=pl.BlockSpec((1, qb * blk, ATT_HEADS * HEAD_DIM), lambda b, i, s: (b, i, 0)),
    )
    return pl.pallas_call(
        functools.partial(_win_attn_kernel, seq_len=S),
        grid_spec=grid_spec,
        out_shape=jax.ShapeDtypeStruct((B, S, ATT_HEADS * HEAD_DIM), F32),
        compiler_params=_cparams("parallel", "parallel"),
        name="window_attention",
    )(sink.astype(F32), at_x, at_x, at_x, at_x, at_x, at_x, at_x, at_c, at_c)


def _ctx_attn_kernel(sink_ref, q_ref, kx_ref, vx_ref, o_ref):
    q = q_ref[0].astype(BF16)
    kx = kx_ref[0].astype(BF16)
    vx = vx_ref[0].astype(BF16)
    g = ATT_HEADS // ATT_KV_HEADS
    outs = []
    for h in range(ATT_HEADS):
        kv = h // g
        qs = q[:, h * HEAD_DIM:(h + 1) * HEAD_DIM]
        ksl = slice(kv * HEAD_DIM, (kv + 1) * HEAD_DIM)
        (p_ctx,), inv = _softmax_parts(qs, [(kx[:, ksl], None)], sink_ref[h])
        outs.append(jnp.dot(p_ctx.astype(BF16), vx[:, ksl], preferred_element_type=F32) * inv)
    o_ref[0] = jnp.concatenate(outs, axis=-1)


def context_attention(at_c, sink):
    B, C, _ = at_c.shape
    kcol = QK_COLS // 128 - 1
    grid_spec = pltpu.PrefetchScalarGridSpec(
        num_scalar_prefetch=1,
        grid=(B,),
        in_specs=[pl.BlockSpec((1, C, ATT_HEADS * HEAD_DIM), lambda b, s: (b, 0, 0)),
                  pl.BlockSpec((1, C, 128), lambda b, s: (b, 0, kcol)),
                  pl.BlockSpec((1, C, 128), lambda b, s: (b, 0, kcol + 1))],
        out_specs=pl.BlockSpec((1, C, ATT_HEADS * HEAD_DIM), lambda b, s: (b, 0, 0)),
    )
    return pl.pallas_call(
        _ctx_attn_kernel,
        grid_spec=grid_spec,
        out_shape=jax.ShapeDtypeStruct((B, C, ATT_HEADS * HEAD_DIM), F32),
        compiler_params=_cparams("parallel"),
        name="context_attention",
    )(sink.astype(F32), at_c, at_c, at_c)


def _out_proj_kernel(y0_ref, y1_ref, y2_ref, y3_ref, h_ref, g1_ref, gng_ref, w_ref,
                     n2g_ref, sh_ref, sc_ref, wr_ref, br_ref, ho_ref, lg_ref):
    parts = []
    for k, y_ref in enumerate((y0_ref, y1_ref, y2_ref, y3_ref)):
        y = y_ref[0]
        ms = jnp.mean(y * y, axis=-1, keepdims=True)
        yn = y * lax.rsqrt(ms + EPS) * gng_ref[:, k * D_GROUP:(k + 1) * D_GROUP]
        parts.append(yn.astype(BF16))
    yn = jnp.concatenate(parts, axis=-1)
    proj = jnp.dot(yn, w_ref[...], preferred_element_type=F32)
    h = h_ref[0] + g1_ref[0] * proj
    ho_ref[0] = h
    ms = jnp.mean(h * h, axis=-1, keepdims=True)
    n = h * lax.rsqrt(ms + EPS) * n2g_ref[...]
    n = n * (1.0 + sc_ref[0]) + sh_ref[0]
    lg_ref[0] = jnp.dot(n.astype(BF16), wr_ref[...], preferred_element_type=F32) + br_ref[...]


def out_proj(ys, h, g1, gng, w_out, n2g, sh2, sc2, w_router, b_router, tm):
    B, L, D = h.shape
    row3 = lambda n: pl.BlockSpec((1, tm, n), lambda b, i: (b, i, 0))
    mod = pl.BlockSpec((1, 1, D), lambda b, i: (b, 0, 0))
    full = lambda r, c: pl.BlockSpec((r, c), lambda b, i: (0, 0))
    return pl.pallas_call(
        _out_proj_kernel,
        grid=(B, L // tm),
        in_specs=[row3(D_GROUP)] * 4 + [row3(D), mod, full(1, D), full(D, D), full(1, D), mod, mod,
                                        full(D, ROUTER_COLS), full(1, ROUTER_COLS)],
        out_specs=[row3(D), row3(ROUTER_COLS)],
        out_shape=[jax.ShapeDtypeStruct((B, L, D), F32), jax.ShapeDtypeStruct((B, L, ROUTER_COLS), F32)],
        compiler_params=_cparams("parallel", "parallel"),
        name="out_proj",
    )(*ys, h, g1, gng.reshape(1, D), w_out, n2g.reshape(1, D), sh2, sc2, w_router, b_router)


N_PAIRS = EXP_PER_GROUP * (EXP_PER_GROUP - 1) // 2
N_CLASSES = N_GROUPS * N_PAIRS
ROUTE_TOKENS = 512
INFO_CLASS, INFO_RANK, INFO_WA, INFO_WB = 0, 1, 2, 3


SUBLANES = 8


def _route_kernel(lg_ref, below_ref, info_ref, cnt_ref, ids_ref, slots_ref, run):
    i = pl.program_id(0)

    @pl.when(i == 0)
    def _():
        run[...] = jnp.zeros_like(run)

    lg = lg_ref[...]
    li = lax.broadcasted_iota(jnp.int32, lg.shape, 1).astype(F32)
    big = float(ROUTER_COLS)

    def first_argmax(vals):
        m = jnp.max(vals, axis=-1, keepdims=True)
        return m, jnp.min(jnp.where(vals == m, li, big), axis=-1, keepdims=True)

    gl = jnp.where(li < N_GROUPS, lg, NEG_INF)
    gmax, g_idx = first_argmax(gl)
    g_prob = 1.0 / jnp.sum(jnp.exp(gl - gmax), axis=-1, keepdims=True)
    lo = N_GROUPS + EXP_PER_GROUP * g_idx
    el = jnp.where(li >= lo, jnp.where(li < lo + EXP_PER_GROUP, lg, NEG_INF), NEG_INF)
    m1, i1 = first_argmax(el)
    m2, i2 = first_argmax(jnp.where(li == i1, NEG_INF, el))
    e2 = jnp.exp(m2 - m1)
    w1 = g_prob / (1.0 + e2)
    w2 = g_prob * e2 / (1.0 + e2)
    j1 = i1 - lo
    j2 = i2 - lo
    a = jnp.minimum(j1, j2)
    b = jnp.maximum(j1, j2)
    cls = g_idx * N_PAIRS + (a * (2 * EXP_PER_GROUP - 1 - a)) * 0.5 + (b - a - 1.0)
    w_a = jnp.where(j1 < j2, w1, w2)
    w_b = jnp.where(j1 < j2, w2, w1)

    hit = li == cls
    onehot = jnp.where(hit, 1.0, 0.0)
    before = jnp.dot(below_ref[...], onehot.astype(BF16), preferred_element_type=F32)
    rank = jnp.sum(jnp.where(hit, before + run[...], 0.0), axis=-1, keepdims=True)
    run[...] = run[...] + jnp.sum(onehot, axis=0, keepdims=True)
    cnt_ref[...] = run[...]
    info = jnp.where(li == INFO_CLASS, cls, 0.0)
    info = jnp.where(li == INFO_RANK, rank, info)
    info = jnp.where(li == INFO_WA, w_a, info)
    info = jnp.where(li == INFO_WB, w_b, info)
    info_ref[...] = info
    ids_ref[0] = info.T[:SUBLANES].astype(jnp.int32)
    slots_ref[...] = jnp.zeros_like(slots_ref)


def route_tokens(logits, slot_rows):
    T = logits.shape[0]
    tb = ROUTE_TOKENS
    steps = T // tb
    rows_per_step = -(-slot_rows // (steps * SUBLANES)) * SUBLANES
    below = (jnp.arange(tb)[None, :] < jnp.arange(tb)[:, None]).astype(BF16)
    return pl.pallas_call(
        _route_kernel,
        grid=(steps,),
        in_specs=[pl.BlockSpec((tb, ROUTER_COLS), lambda i: (i, 0)),
                  pl.BlockSpec((tb, tb), lambda i: (0, 0))],
        out_specs=[pl.BlockSpec((tb, ROUTER_COLS), lambda i: (i, 0)),
                   pl.BlockSpec((1, ROUTER_COLS), lambda i: (0, 0)),
                   pl.BlockSpec((1, SUBLANES, tb), lambda i: (i, 0, 0)),
                   pl.BlockSpec((rows_per_step, LANES), lambda i: (i, 0))],
        out_shape=[jax.ShapeDtypeStruct((T, ROUTER_COLS), F32), jax.ShapeDtypeStruct((1, ROUTER_COLS), F32),
                   jax.ShapeDtypeStruct((steps, SUBLANES, tb), jnp.int32),
                   jax.ShapeDtypeStruct((steps * rows_per_step, LANES), F32)],
        scratch_shapes=[pltpu.VMEM((1, ROUTER_COLS), F32)],
        compiler_params=_cparams("arbitrary"),
        name="moe_route",
    )(logits, below)


def _pair_tables():
    a_tab, b_tab = [], []
    for g in range(N_GROUPS):
        for a in range(EXP_PER_GROUP):
            for b in range(a + 1, EXP_PER_GROUP):
                a_tab.append(g * EXP_PER_GROUP + a)
                b_tab.append(g * EXP_PER_GROUP + b)
    return jnp.array(a_tab, jnp.int32), jnp.array(b_tab, jnp.int32)


def _slot_kernel(ids_ref, start_ref, dest_ref):
    cls = ids_ref[0, INFO_CLASS:INFO_CLASS + 1, :]
    rank = ids_ref[0, INFO_RANK:INFO_RANK + 1, :]
    ci = lax.broadcasted_iota(jnp.int32, (ROUTER_COLS, cls.shape[1]), 0)
    start = jnp.sum(jnp.where(ci == cls, start_ref[...], 0), axis=0, keepdims=True)
    dest_ref[0] = jnp.broadcast_to(start + rank, dest_ref.shape[1:])


def slot_plan(ids, counts, n_blocks):
    nt, _, tb = ids.shape
    cnt = counts[0, :N_CLASSES].astype(jnp.int32)
    padded = (cnt + MOE_BLOCK - 1) // MOE_BLOCK * MOE_BLOCK
    upto = jnp.arange(N_CLASSES)[None, :] <= jnp.arange(N_CLASSES)[:, None]
    pad_end = jnp.sum(jnp.where(upto, padded[None, :], 0), axis=1)
    class_start = jnp.zeros((ROUTER_COLS, 1), jnp.int32).at[:N_CLASSES, 0].set(pad_end - padded)
    dest = pl.pallas_call(
        _slot_kernel,
        grid=(nt,),
        in_specs=[pl.BlockSpec((1, SUBLANES, tb), lambda i: (i, 0, 0)),
                  pl.BlockSpec((ROUTER_COLS, 1), lambda i: (0, 0))],
        out_specs=pl.BlockSpec((1, SUBLANES, tb), lambda i: (i, 0, 0)),
        out_shape=jax.ShapeDtypeStruct((nt, SUBLANES, tb), jnp.int32),
        compiler_params=_cparams("parallel"),
        name="moe_slots",
    )(ids, class_start)[:, 0, :].reshape(nt * tb)
    n_used = (pad_end[-1] // MOE_BLOCK).astype(jnp.int32).reshape(1)
    blk_first = jnp.arange(n_blocks, dtype=jnp.int32) * MOE_BLOCK
    blk_cls = jnp.minimum(jnp.sum((pad_end[None, :] <= blk_first[:, None]).astype(jnp.int32), axis=1),
                          N_CLASSES - 1)
    a_tab, b_tab = _pair_tables()
    hit = blk_cls[:, None] == jnp.arange(N_CLASSES)[None, :]
    pick = lambda tab: jnp.sum(jnp.where(hit, tab[None, :], 0), axis=1).astype(jnp.int32)
    return dest, pick(a_tab), pick(b_tab), n_used


DISPATCH_TOKENS = 1024
COLLECT_TOKENS = 512


def _wait_rows(buf, sem):
    pltpu.make_async_copy(buf, buf, sem).wait()


DMA_UNROLL = 8
TOKEN_TILE_ROWS = 8


def _store_token_tiles(tiles_ref, offset, pitch, x):
    n = x.shape[0]
    for j in range(x.shape[1] // LANES):
        tiles_ref[pl.ds(offset + j, n, stride=pitch), :] = x[:, j * LANES:(j + 1) * LANES]


def _load_token_tiles(tiles_ref, offset, pitch, n, width):
    return jnp.concatenate([tiles_ref[pl.ds(offset + j, n, stride=pitch), :] for j in range(width // LANES)],
                           axis=-1)


def _dispatch_kernel(dest_ref, hx_ref, hc_ref, g_ref, shx_ref, scx_ref, shc_ref, scc_ref, zeros_hbm,
                     xs_hbm, rows, sems, *, n_latent_blocks):
    del zeros_hbm
    i = pl.program_id(0)
    n = pl.num_programs(0)
    slot = i % 2
    tb = hx_ref.shape[0]

    @pl.when(i >= 2)
    def _():
        _wait_rows(rows.at[slot], sems.at[slot])

    def normed(h_ref, sh_ref, sc_ref):
        h = h_ref[...]
        ms = jnp.mean(h * h, axis=-1, keepdims=True)
        return h * lax.rsqrt(ms + EPS) * g_ref[...] * (1.0 + sc_ref[0]) + sh_ref[0]

    @pl.when(i < n_latent_blocks)
    def _():
        _store_token_tiles(rows.at[slot], 0, TOKEN_TILE_ROWS, normed(hx_ref, shx_ref, scx_ref))

    @pl.when(i >= n_latent_blocks)
    def _():
        _store_token_tiles(rows.at[slot], 0, TOKEN_TILE_ROWS, normed(hc_ref, shc_ref, scc_ref))

    def body(g, carry):
        for u in range(DMA_UNROLL):
            r = g * DMA_UNROLL + u
            dst = pl.multiple_of(dest_ref[0, 0, r] * TOKEN_TILE_ROWS, TOKEN_TILE_ROWS)
            pltpu.make_async_copy(rows.at[slot, pl.ds(r * TOKEN_TILE_ROWS, TOKEN_TILE_ROWS)],
                                  xs_hbm.at[pl.ds(dst, TOKEN_TILE_ROWS)], sems.at[slot]).start(priority=u % 2)
        return carry
    lax.fori_loop(0, tb // DMA_UNROLL, body, 0)

    @pl.when(i == n - 1)
    def _():
        _wait_rows(rows.at[slot], sems.at[slot])

        @pl.when(n >= 2)
        def _():
            _wait_rows(rows.at[1 - slot], sems.at[1 - slot])


def moe_dispatch(h_x, h_c, dest, zero_slots, n2g, mod_x, mod_c, tokens_per_batch):
    Tx, D = h_x.shape
    tb = DISPATCH_TOKENS
    nxb = Tx // tb
    if h_c is None:
        h_c, mod_c, ncb = h_x, mod_x, 0
    else:
        ncb = h_c.shape[0] // tb
    per_b = tokens_per_batch // tb
    tile_rows = D // LANES
    assert tile_rows == TOKEN_TILE_ROWS
    xi = lambda i: jnp.minimum(i, nxb - 1)
    ci = lambda i: jnp.maximum(i - nxb, 0)
    modx = pl.BlockSpec((1, 1, D), lambda i: (xi(i) // per_b, 0, 0))
    modc = pl.BlockSpec((1, 1, D), lambda i: (0, 0, 0))
    return pl.pallas_call(
        functools.partial(_dispatch_kernel, n_latent_blocks=nxb),
        grid=(nxb + ncb,),
        in_specs=[pl.BlockSpec((1, 1, tb), lambda i: (i, 0, 0), memory_space=pltpu.SMEM),
                  pl.BlockSpec((tb, D), lambda i: (xi(i), 0)),
                  pl.BlockSpec((tb, D), lambda i: (ci(i), 0)),
                  pl.BlockSpec((1, D), lambda i: (0, 0)),
                  modx, modx, modc, modc,
                  pl.BlockSpec(memory_space=pl.ANY)],
        out_specs=pl.BlockSpec(memory_space=pl.ANY),
        out_shape=jax.ShapeDtypeStruct(zero_slots.shape, F32),
        scratch_shapes=[pltpu.VMEM((2, tb * tile_rows, LANES), F32), pltpu.SemaphoreType.DMA((2,))],
        input_output_aliases={8: 0},
        compiler_params=_cparams("arbitrary"),
        name="moe_dispatch",
    )(dest.reshape(-1, 1, tb), h_x, h_c, n2g.reshape(1, D), mod_x[0], mod_x[1], mod_c[0], mod_c[1], zero_slots)


def _expert_pair_kernel(ea_ref, eb_ref, nused_ref, xs_ref, wga_ref, wua_ref, wda_ref, wgb_ref, wub_ref, wdb_ref,
                        o_ref):
    del ea_ref, eb_ref
    i = pl.program_id(0)
    D = wga_ref.shape[2]

    @pl.when(i < nused_ref[0])
    def _():
        xb = _load_token_tiles(xs_ref, 0, TOKEN_TILE_ROWS, MOE_BLOCK, D).astype(BF16)
        halves = []
        for wg_ref, wu_ref, wd_ref in ((wga_ref, wua_ref, wda_ref), (wgb_ref, wub_ref, wdb_ref)):
            gate = jnp.dot(xb, wg_ref[0, 0], preferred_element_type=F32)
            up = jnp.dot(xb, wu_ref[0, 0], preferred_element_type=F32)
            hid = (gate * jax.nn.sigmoid(gate) * up).astype(BF16)
            out = jnp.dot(hid, wd_ref[0, 0], preferred_element_type=F32)
            halves.append(lax.bitcast_convert_type(out.astype(BF16).astype(F32), jnp.uint32))
        _store_token_tiles(o_ref, 0, TOKEN_TILE_ROWS, halves[0] | (halves[1] >> 16))

    @pl.when(i >= nused_ref[0])
    def _():
        o_ref[...] = jnp.zeros_like(o_ref)


def _unpack_pair(words):
    hi = lax.bitcast_convert_type(words & jnp.uint32(0xFFFF0000), F32)
    lo = lax.bitcast_convert_type(words << 16, F32)
    return hi, lo


def expert_pairs(xs, n_blocks, blk_a, blk_b, n_used, w_gate, w_up, w_down, layer):
    D, DE = w_gate.shape[2:]
    P = n_blocks * MOE_BLOCK
    wspec = lambda shape, which: pl.BlockSpec(shape, lambda i, ea, eb, nu: (layer, (ea, eb)[which][i], 0, 0))
    grid_spec = pltpu.PrefetchScalarGridSpec(
        num_scalar_prefetch=3,
        grid=(n_blocks,),
        in_specs=[pl.BlockSpec((MOE_BLOCK * TOKEN_TILE_ROWS, LANES), lambda i, ea, eb, nu: (i, 0)),
                  wspec((1, 1, D, DE), 0), wspec((1, 1, D, DE), 0), wspec((1, 1, DE, D), 0),
                  wspec((1, 1, D, DE), 1), wspec((1, 1, D, DE), 1), wspec((1, 1, DE, D), 1)],
        out_specs=pl.BlockSpec((MOE_BLOCK * TOKEN_TILE_ROWS, LANES), lambda i, ea, eb, nu: (i, 0)),
    )
    return pl.pallas_call(
        _expert_pair_kernel,
        grid_spec=grid_spec,
        out_shape=jax.ShapeDtypeStruct((P * TOKEN_TILE_ROWS, LANES), jnp.uint32),
        compiler_params=_cparams("arbitrary"),
        name="moe_experts",
    )(blk_a, blk_b, n_used, xs, w_gate, w_up, w_down, w_gate, w_up, w_down)


def _gather_pairs(idx_ref, src_hbm, buf, sem, n_tokens):
    def body(g, carry):
        for u in range(DMA_UNROLL):
            r = g * DMA_UNROLL + u
            src = pl.multiple_of(idx_ref[0, 0, r] * TOKEN_TILE_ROWS, TOKEN_TILE_ROWS)
            pltpu.make_async_copy(src_hbm.at[pl.ds(src, TOKEN_TILE_ROWS)],
                                  buf.at[pl.ds(r * TOKEN_TILE_ROWS, TOKEN_TILE_ROWS)], sem).start(priority=u % 2)
        return carry
    lax.fori_loop(0, n_tokens // DMA_UNROLL, body, 0)


def _collect_kernel(dest_ref, dest_next_ref, o_hbm, info_ref, h_ref, g2_ref, fg_ref, out_ref, obuf, sems, *,
                    final_norm):
    i = pl.program_id(0)
    n = pl.num_programs(0)
    slot = i % 2
    tb, D = h_ref.shape

    @pl.when(i == 0)
    def _():
        _gather_pairs(dest_ref, o_hbm, obuf.at[0], sems.at[0], tb)

    @pl.when(i + 1 < n)
    def _():
        _gather_pairs(dest_next_ref, o_hbm, obuf.at[1 - slot], sems.at[1 - slot], tb)

    _wait_rows(obuf.at[slot], sems.at[slot])
    e_a, e_b = _unpack_pair(_load_token_tiles(obuf.at[slot], 0, TOKEN_TILE_ROWS, tb, D))
    m = info_ref[:, INFO_WA:INFO_WA + 1] * e_a + info_ref[:, INFO_WB:INFO_WB + 1] * e_b
    h = h_ref[...] + g2_ref[0] * m
    if final_norm:
        ms = jnp.mean(h * h, axis=-1, keepdims=True)
        h = h * lax.rsqrt(ms + EPS) * fg_ref[...]
    out_ref[...] = h


def moe_collect(o_sorted, dest, info, block_offset, h_tokens, g2, tokens_per_batch, final_g, final_norm):
    T, D = h_tokens.shape
    tb = COLLECT_TOKENS
    nt = T // tb
    if g2.shape[0] == 1:
        g2_index = lambda i: 0
    else:
        assert tokens_per_batch % tb == 0
        g2_index = lambda i: i // (tokens_per_batch // tb)
    last = block_offset + nt - 1
    dest3 = dest.reshape(-1, 1, tb)
    return pl.pallas_call(
        functools.partial(_collect_kernel, final_norm=final_norm),
        grid=(nt,),
        in_specs=[pl.BlockSpec((1, 1, tb), lambda i: (block_offset + i, 0, 0), memory_space=pltpu.SMEM),
                  pl.BlockSpec((1, 1, tb), lambda i: (jnp.minimum(block_offset + i + 1, last), 0, 0),
                               memory_space=pltpu.SMEM),
                  pl.BlockSpec(memory_space=pl.ANY),
                  pl.BlockSpec((tb, ROUTER_COLS), lambda i: (block_offset + i, 0)),
                  pl.BlockSpec((tb, D), lambda i: (i, 0)),
                  pl.BlockSpec((1, 1, D), lambda i: (g2_index(i), 0, 0)),
                  pl.BlockSpec((1, D), lambda i: (0, 0))],
        out_specs=pl.BlockSpec((tb, D), lambda i: (i, 0)),
        out_shape=jax.ShapeDtypeStruct((T, D), F32),
        scratch_shapes=[pltpu.VMEM((2, tb * TOKEN_TILE_ROWS, LANES), jnp.uint32), pltpu.SemaphoreType.DMA((2,))],
        compiler_params=_cparams("arbitrary"),
        name="moe_collect",
    )(dest3, dest3, o_sorted, info, h_tokens, g2, final_g.reshape(1, D))


CONV_MARGIN = 16


def _time_chunk(L):
    return min(L, 256)


LANES = 128


def _zero_margins(pad_ref, L):
    zeros = jnp.zeros((CONV_MARGIN, LANES), F32)
    for s in range(pad_ref.shape[0]):
        pad_ref[s, pl.ds(0, CONV_MARGIN), :] = zeros
        pad_ref[s, pl.ds(CONV_MARGIN + L, CONV_MARGIN), :] = zeros


def _dw_conv_slab(pad_ref, s, base, T, w_ref, b_ref, col, taps, pad_left):
    acc = jnp.broadcast_to(b_ref[:, col:col + LANES], (T, LANES))
    for k in range(taps):
        acc = acc + w_ref[k:k + 1, col:col + LANES] * pad_ref[s, pl.ds(base + (CONV_MARGIN - pad_left + k), T), :]
    return acc


def _conformer_kernel(u_ref, w_ref, b_ref, g_ref, beta_ref, o_ref, ypad):
    L = o_ref.shape[1]
    T = _time_chunk(L)
    C = D_GROUP
    n_slabs = C // LANES
    pad = (CONF_KERNEL - 1) // 2
    _zero_margins(ypad, L)

    def glu(j, carry):
        base = pl.multiple_of(j * T, T)
        for s in range(n_slabs):
            a = u_ref[0, pl.ds(base, T), s * LANES:(s + 1) * LANES]
            gate = u_ref[0, pl.ds(base, T), C + s * LANES:C + (s + 1) * LANES]
            ypad[s, pl.ds(CONV_MARGIN + base, T), :] = a * jax.nn.sigmoid(gate)
        return carry
    lax.fori_loop(0, L // T, glu, 0)

    def conv(j, carry):
        base = pl.multiple_of(j * T, T)
        acc = jnp.concatenate([_dw_conv_slab(ypad, s, base, T, w_ref, b_ref, s * LANES, CONF_KERNEL, pad)
                               for s in range(n_slabs)], axis=-1)
        mu = jnp.mean(acc, axis=-1, keepdims=True)
        cen = acc - mu
        var = jnp.mean(cen * cen, axis=-1, keepdims=True)
        y = cen * lax.rsqrt(var + EPS) * g_ref[...] + beta_ref[...]
        o_ref[0, pl.ds(base, T), :] = y * jax.nn.sigmoid(y)
        return carry
    lax.fori_loop(0, L // T, conv, 0)


def conformer_conv(u, w, b, ln_g, ln_b):
    B, L, _ = u.shape
    C = D_GROUP
    vec = pl.BlockSpec((1, C), lambda i: (0, 0))
    return pl.pallas_call(
        _conformer_kernel,
        grid=(B,),
        in_specs=[pl.BlockSpec((1, L, 2 * C), lambda i: (i, 0, 0)),
                  pl.BlockSpec((CONF_KERNEL, C), lambda i: (0, 0)), vec, vec, vec],
        out_specs=pl.BlockSpec((1, L, C), lambda i: (i, 0, 0)),
        out_shape=jax.ShapeDtypeStruct((B, L, C), F32),
        scratch_shapes=[pltpu.VMEM((C // LANES, L + 2 * CONV_MARGIN, LANES), F32)],
        compiler_params=_cparams("parallel"),
        name="conformer_conv",
    )(u, w, b.reshape(1, C), ln_g.reshape(1, C), ln_b.reshape(1, C))


def _gelu_tanh(x):
    return 0.5 * x * (1.0 + jnp.tanh(math.sqrt(2.0 / math.pi) * (x + 0.044715 * (x * x * x))))


def _lru_kernel(uc_ref, ux_ref, cw_ref, cb_ref, wcat_ref, bcat_ref, lam_ref, *rest, need_ctx):
    if need_ctx:
        oc_ref, ox_ref, cpad, xpad, a_s, b_s, yx, yc = rest
    else:
        ox_ref, cpad, xpad, a_s, b_s, yx = rest
        oc_ref = yc = None
    C = D_GROUP
    n_slabs = C // LANES
    Lc = uc_ref.shape[1]
    Lx = ux_ref.shape[1]
    pad_l = (LRU_CONV - 1) // 2

    def fill(pad_ref, u_ref, L):
        T = _time_chunk(L)
        _zero_margins(pad_ref, L)

        def body(j, carry):
            base = pl.multiple_of(j * T, T)
            for s in range(n_slabs):
                pad_ref[s, pl.ds(CONV_MARGIN + base, T), :] = u_ref[0, pl.ds(base, T),
                                                                    C + s * LANES:C + (s + 1) * LANES]
            return carry
        lax.fori_loop(0, L // T, body, 0)

    fill(cpad, uc_ref, Lc)
    fill(xpad, ux_ref, Lx)

    def coeffs(pad_ref, base, T, d):
        x = jnp.concatenate([_dw_conv_slab(pad_ref, s, base, T, cw_ref, cb_ref, s * LANES, LRU_CONV, pad_l)
                             for s in range(n_slabs)], axis=-1)
        t = jnp.tanh(jnp.dot(x.astype(BF16), wcat_ref[:, 2 * d * C:2 * (d + 1) * C],
                             preferred_element_type=F32) + bcat_ref[:, 2 * d * C:2 * (d + 1) * C])
        i = 0.5 * t[:, C:] + 0.5
        z = -lam_ref[d:d + 1, :]
        softplus = jnp.maximum(z, 0.0) + jnp.log(1.0 + jnp.exp(-jnp.abs(z)))
        half_rate = (-0.5 * LRU_C) * softplus
        a = jnp.exp(half_rate * t[:, :C] + half_rate)
        b = jnp.sqrt(1.0 - a * a) * (i * x)
        for s in range(n_slabs):
            a_s[d * n_slabs + s, pl.ds(0, T), :] = a[:, s * LANES:(s + 1) * LANES]
            b_s[d * n_slabs + s, pl.ds(0, T), :] = b[:, s * LANES:(s + 1) * LANES]

    def run(pad_ref, L, h, y_ref):
        T = _time_chunk(L)
        n = L // T

        def chunk(j, h):
            base_f = pl.multiple_of(j * T, T)
            base_b = pl.multiple_of((n - 1 - j) * T, T)
            coeffs(pad_ref, base_f, T, 0)
            coeffs(pad_ref, base_b, T, 1)

            def step(t, h):
                new = []
                for d, (base, row) in enumerate(((base_f, t), (base_b, T - 1 - t))):
                    for s in range(n_slabs):
                        k = d * n_slabs + s
                        hs = a_s[k, pl.ds(row, 1), :] * h[k] + b_s[k, pl.ds(row, 1), :]
                        if y_ref is not None:
                            y_ref[k, pl.ds(base + row, 1), :] = hs
                        new.append(hs)
                return tuple(new)
            return lax.fori_loop(0, T, step, h, unroll=8)
        return lax.fori_loop(0, n, chunk, h)

    h = tuple(jnp.zeros((1, LANES), F32) for _ in range(2 * n_slabs))
    h = run(cpad, Lc, h, yc)
    run(xpad, Lx, h, yx)

    def finish(u_ref, y_ref, o_ref, L):
        T = _time_chunk(L)

        def body(j, carry):
            base = pl.multiple_of(j * T, T)
            y = jnp.concatenate([y_ref[s, pl.ds(base, T), :] + y_ref[n_slabs + s, pl.ds(base, T), :]
                                 for s in range(n_slabs)], axis=-1)
            o_ref[0, pl.ds(base, T), :] = _gelu_tanh(u_ref[0, pl.ds(base, T), :C]) * y
            return carry
        lax.fori_loop(0, L // T, body, 0)

    finish(ux_ref, yx, ox_ref, Lx)
    if need_ctx:
        finish(uc_ref, yc, oc_ref, Lc)


def _block_diag(w):
    H, n, _ = w.shape
    eye = jnp.eye(H, dtype=w.dtype)
    return (eye[:, None, :, None] * w[:, :, None, :]).reshape(H * n, H * n)


def rglru_mixer(uc, ux, lp, need_ctx):
    B, Lc, _ = uc.shape
    Lx = ux.shape[1]
    C = D_GROUP
    wcat = (0.5 * jnp.concatenate([_block_diag(lp["lru_wa"][0]), _block_diag(lp["lru_wx"][0]),
                                   _block_diag(lp["lru_wa"][1]), _block_diag(lp["lru_wx"][1])], axis=1)).astype(BF16)
    bcat = 0.5 * jnp.concatenate([lp["lru_ba"][0], lp["lru_bx"][0], lp["lru_ba"][1], lp["lru_bx"][1]]).reshape(1, 4 * C)
    full = lambda r, c: pl.BlockSpec((r, c), lambda i: (0, 0))
    seq = lambda L, n: pl.BlockSpec((1, L, n), lambda i: (i, 0, 0))
    out_specs = [seq(Lx, C)]
    out_shape = [jax.ShapeDtypeStruct((B, Lx, C), F32)]
    if need_ctx:
        out_specs = [seq(Lc, C)] + out_specs
        out_shape = [jax.ShapeDtypeStruct((B, Lc, C), F32)] + out_shape
    T = _time_chunk(Lx)
    slab = lambda rows, n=1: pltpu.VMEM((n * C // LANES, rows, LANES), F32)
    scratch = [slab(Lc + 2 * CONV_MARGIN), slab(Lx + 2 * CONV_MARGIN), slab(T, 2), slab(T, 2), slab(Lx, 2)]
    if need_ctx:
        scratch.append(slab(Lc, 2))
    res = pl.pallas_call(
        functools.partial(_lru_kernel, need_ctx=need_ctx),
        grid=(B,),
        in_specs=[seq(Lc, 2 * C), seq(Lx, 2 * C), full(LRU_CONV, C), full(1, C), full(C, 4 * C),
                  full(1, 4 * C), full(2, C)],
        out_specs=out_specs,
        out_shape=out_shape,
        scratch_shapes=scratch,
        compiler_params=_cparams("parallel"),
        name="rglru",
    )(uc, ux, lp["lru_conv_w"], lp["lru_conv_b"].reshape(1, C), wcat, bcat, lp["lru_lambda"])
    if need_ctx:
        return res[0], res[1]
    return None, res[0]


HY_SHORT = 3


def _short_conv(pad_ref, base, T, w_ref, b_ref, c0, c1):
    return jnp.concatenate([_dw_conv_slab(pad_ref, col // LANES, base, T, w_ref, b_ref, col, HY_SHORT, 1)
                            for col in range(c0, c1, LANES)], axis=-1)


def _fill_padded(pad_ref, u_ref, L, T):
    _zero_margins(pad_ref, L)

    def body(j, carry):
        base = pl.multiple_of(j * T, T)
        for s in range(pad_ref.shape[0]):
            pad_ref[s, pl.ds(CONV_MARGIN + base, T), :] = u_ref[0, pl.ds(base, T), s * LANES:(s + 1) * LANES]
        return carry
    lax.fori_loop(0, L // T, body, 0)


def _hyena_pre_kernel(u_ref, w_ref, b_ref, z_ref, upad):
    L = u_ref.shape[1]
    T = _time_chunk(L)
    C = D_GROUP
    _fill_padded(upad, u_ref, L, T)

    def body(j, carry):
        base = pl.multiple_of(j * T, T)
        x1 = _short_conv(upad, base, T, w_ref, b_ref, C, 2 * C)
        v = _short_conv(upad, base, T, w_ref, b_ref, 2 * C, 3 * C)
        z_ref[pl.ds(base, T), :] = (x1 * v).astype(BF16)
        return carry
    lax.fori_loop(0, L // T, body, 0)


def _hyena_post_kernel(u_ref, y_ref, w_ref, b_ref, bias_ref, o_ref, upad):
    L = u_ref.shape[1]
    T = _time_chunk(L)
    C = D_GROUP
    _fill_padded(upad, u_ref, L, T)

    def body(j, carry):
        base = pl.multiple_of(j * T, T)
        x0 = _short_conv(upad, base, T, w_ref, b_ref, 0, C)
        x1 = _short_conv(upad, base, T, w_ref, b_ref, C, 2 * C)
        v = _short_conv(upad, base, T, w_ref, b_ref, 2 * C, 3 * C)
        o_ref[0, pl.ds(base, T), :] = x0 * (y_ref[pl.ds(base, T), :] + (x1 * v) * bias_ref[...])
        return carry
    lax.fori_loop(0, L // T, body, 0)


def _spectrum_kernel(f_ref, z_ref, ha_ref, hb_ref, hc_ref, y_ref):
    tf = ha_ref.shape[0]
    acc = jnp.dot(f_ref[...], z_ref[...], preferred_element_type=F32)
    zr = acc[:tf]
    zi = acc[tf:]
    y_ref[:tf, :] = (zr * ha_ref[...] - zi * hb_ref[...]).astype(BF16)
    y_ref[tf:, :] = (zr * hb_ref[...] + zi * hc_ref[...]).astype(BF16)


def _idft_kernel(f_ref, y_ref, o_ref):
    o_ref[...] = jnp.dot(f_ref[...], y_ref[...], preferred_element_type=F32)


def dft_tables(L):
    N = 2 * L
    tf = min(256, L)
    k = jnp.arange(L, dtype=jnp.int32)
    n = jnp.arange(L, dtype=jnp.int32)
    ang = (2.0 * math.pi / N) * ((k[:, None] * n[None, :]) % N).astype(F32)
    cos = jnp.cos(ang)
    sin = jnp.sin(ang)
    nyq = jnp.where(n % 2 == 0, 1.0, -1.0).astype(F32)
    f_re = cos
    f_im = (-sin).at[0].set(nyq)
    fwd = jnp.stack([f_re.reshape(L // tf, tf, L), f_im.reshape(L // tf, tf, L)], axis=1).reshape(N, L)
    ck = jnp.where(k == 0, 1.0, 2.0).astype(F32)[:, None] / N
    i_re = cos * ck
    i_im = (-sin * ck).at[0].set(nyq / N)
    inv = jnp.stack([i_re.reshape(L // tf, tf, L), i_im.reshape(L // tf, tf, L)], axis=1).reshape(N, L).T
    return fwd.astype(BF16), inv.astype(BF16)


def filter_spectrum(k_slabs, fwd):
    n_slabs, N, _ = k_slabs.shape
    L = N // 2
    C = n_slabs * LANES
    tf = min(256, L)
    halves = jnp.concatenate([k_slabs[s, h * L:(h + 1) * L] for h in range(2) for s in range(n_slabs)], axis=1)
    r = pl.pallas_call(
        _idft_kernel,
        grid=(N // (2 * tf), 1),
        in_specs=[pl.BlockSpec((2 * tf, L), lambda i, j: (i, 0)),
                  pl.BlockSpec((L, 2 * C), lambda i, j: (0, 0))],
        out_specs=pl.BlockSpec((2 * tf, 2 * C), lambda i, j: (i, 0)),
        out_shape=jax.ShapeDtypeStruct((N, 2 * C), F32),
        compiler_params=_cparams("parallel", "parallel"),
        name="hyena_filter_dft",
    )(fwd, halves.astype(BF16)).reshape(L // tf, 2, tf, 2 * C)
    sign = jnp.where(jnp.arange(L) % 2 == 0, 1.0, -1.0).astype(F32)[:, None]
    re = r[:, 0].reshape(L, 2 * C)
    im = r[:, 1].reshape(L, 2 * C)
    hr = re[:, :C] + sign * re[:, C:]
    hi = im[:, :C] + sign * im[:, C:]
    return hr, hi.at[0].set(0.0), hr.at[0].set(hi[0])


def hyena_mixer(u, lp, tables):
    B, L, _ = u.shape
    C = D_GROUP
    N = 2 * L
    fwd, inv = tables
    tf = min(256, L)
    T = _time_chunk(L)
    w, bsh = lp["hy_short_w"], lp["hy_short_b"].reshape(1, 3 * C)
    z2 = pl.pallas_call(
        _hyena_pre_kernel,
        grid=(B,),
        in_specs=[pl.BlockSpec((1, L, 3 * C), lambda b: (b, 0, 0)),
                  pl.BlockSpec((HY_SHORT, 3 * C), lambda b: (0, 0)),
                  pl.BlockSpec((1, 3 * C), lambda b: (0, 0))],
        out_specs=pl.BlockSpec((L, C), lambda b: (0, b)),
        out_shape=jax.ShapeDtypeStruct((L, B * C), BF16),
        scratch_shapes=[pltpu.VMEM((3 * C // LANES, L + 2 * CONV_MARGIN, LANES), F32)],
        compiler_params=_cparams("parallel"),
        name="hyena_pre",
    )(u, w, bsh)

    tn = 2 * C
    ha, hb, hc = [jnp.tile(t, (1, tn // C)) for t in filter_spectrum(hyena_filter(L, lp), fwd)]
    hspec = pl.BlockSpec((tf, tn), lambda i, j: (i, 0))
    y2 = pl.pallas_call(
        _spectrum_kernel,
        grid=(L // tf, B * C // tn),
        in_specs=[pl.BlockSpec((2 * tf, L), lambda i, j: (i, 0)),
                  pl.BlockSpec((L, tn), lambda i, j: (0, j)), hspec, hspec, hspec],
        out_specs=pl.BlockSpec((2 * tf, tn), lambda i, j: (i, j)),
        out_shape=jax.ShapeDtypeStruct((N, B * C), BF16),
        compiler_params=_cparams("parallel", "parallel"),
        name="hyena_spectrum",
    )(fwd, z2, ha, hb, hc)

    tl = min(256, L)
    yt = pl.pallas_call(
        _idft_kernel,
        grid=(L // tl, B * C // tn),
        in_specs=[pl.BlockSpec((tl, N), lambda i, j: (i, 0)),
                  pl.BlockSpec((N, tn), lambda i, j: (0, j))],
        out_specs=pl.BlockSpec((tl, tn), lambda i, j: (i, j)),
        out_shape=jax.ShapeDtypeStruct((L, B * C), F32),
        compiler_params=_cparams("parallel", "parallel"),
        name="hyena_idft",
    )(inv, y2)

    return pl.pallas_call(
        _hyena_post_kernel,
        grid=(B,),
        in_specs=[pl.BlockSpec((1, L, 3 * C), lambda b: (b, 0, 0)),
                  pl.BlockSpec((L, C), lambda b: (0, b)),
                  pl.BlockSpec((HY_SHORT, 3 * C), lambda b: (0, 0)),
                  pl.BlockSpec((1, 3 * C), lambda b: (0, 0)),
                  pl.BlockSpec((1, C), lambda b: (0, 0))],
        out_specs=pl.BlockSpec((1, L, C), lambda b: (b, 0, 0)),
        out_shape=jax.ShapeDtypeStruct((B, L, C), F32),
        scratch_shapes=[pltpu.VMEM((3 * C // LANES, L + 2 * CONV_MARGIN, LANES), F32)],
        compiler_params=_cparams("parallel"),
        name="hyena_post",
    )(u, yt, w, bsh, lp["hy_bias"].reshape(1, C))


FFT_N2 = 128
FFT_UNROLL = 8


class _FftPlan:
    def __init__(self, L):
        self.L = L
        self.N = 2 * L
        self.N1 = self.N // FFT_N2
        self.KH = self.N1 // 2 + 1
        self.KP = -(-self.KH // 8) * 8
        self.PA = 2 * self.KP + 4


def fft_tables(L):
    p = _FftPlan(L)
    N, N1, KH, KP = p.N, p.N1, p.KH, p.KP
    n2 = jnp.arange(FFT_N2, dtype=jnp.int32)
    k1 = jnp.arange(KP, dtype=jnp.int32)
    n1 = jnp.arange(N1, dtype=jnp.int32)
    n = FFT_N2 * n1[None, None, :] + n2[:, None, None]
    ang = (2.0 * math.pi / N) * ((k1[None, :, None] * n) % N).astype(F32)
    keep = (k1 < KH)[None, :, None]
    g_re = jnp.where(keep, jnp.cos(ang), 0.0)
    g_im = jnp.where(keep, -jnp.sin(ang), 0.0)
    ga_full = jnp.concatenate([g_re, g_im], axis=1)
    ck = jnp.where((k1 == 0) | (k1 == N1 // 2), 1.0, 2.0) / N
    ga_inv = jnp.swapaxes(ga_full[:, :, :N1 // 2] * jnp.tile(ck, 2)[None, :, None], 1, 2)
    kk = jnp.arange(FFT_N2, dtype=jnp.int32)
    ang2 = (2.0 * math.pi / FFT_N2) * ((kk[:, None] * kk[None, :]) % FFT_N2).astype(F32)
    fr, fi = jnp.cos(ang2), -jnp.sin(ang2)
    fb = jnp.block([[fr, -fi], [fi, fr]])
    fb_inv = jnp.block([[fr, fi], [-fi, fr]])
    return dict(ga_half=ga_full[:, :, :N1 // 2].astype(BF16), ga_full=ga_full.astype(BF16),
                ga_inv=ga_inv.astype(BF16), fb=fb.astype(BF16), fb_inv=fb_inv.astype(BF16))


def _fft_stage_a(x_ref, ga_ref, s_ref, plan, n1_count):
    n_slabs = x_ref.shape[0]

    def body(n2, carry):
        xs = jnp.concatenate([x_ref[s, pl.ds(n2, n1_count, stride=FFT_N2), :] for s in range(n_slabs)], axis=-1)
        a = jnp.dot(ga_ref[n2], xs.astype(BF16), preferred_element_type=F32)
        for s in range(n_slabs):
            s_ref[s, pl.ds(n2 * plan.PA, 2 * plan.KP), :] = a[:, s * LANES:(s + 1) * LANES]
        return carry
    lax.fori_loop(0, FFT_N2, body, 0, unroll=FFT_UNROLL)


def _fft_load_k1(s_ref, k1, plan):
    n_slabs = s_ref.shape[0]
    re = jnp.concatenate([s_ref[s, pl.ds(k1, FFT_N2, stride=plan.PA), :] for s in range(n_slabs)], axis=-1)
    im = jnp.concatenate([s_ref[s, pl.ds(plan.KP + k1, FFT_N2, stride=plan.PA), :] for s in range(n_slabs)], axis=-1)
    return jnp.concatenate([re, im], axis=0).astype(BF16)


def _fft_filter_kernel(k_ref, ga_ref, fb_ref, h_ref, s_ref, *, plan):
    _fft_stage_a(k_ref, ga_ref, s_ref, plan, plan.N1)

    def body(k1, carry):
        h_ref[k1] = jnp.dot(fb_ref[...], _fft_load_k1(s_ref, k1, plan), preferred_element_type=F32).astype(BF16)
        return carry
    lax.fori_loop(0, plan.KH, body, 0)


def _fft_conv_kernel(z_ref, ga_ref, gi_ref, fb_ref, fbi_ref, h_ref, y_ref, s_ref, *, plan):
    zs = z_ref.at[0]
    ys = y_ref.at[0]
    n_slabs = zs.shape[0]
    half = FFT_N2
    _fft_stage_a(zs, ga_ref, s_ref, plan, plan.N1 // 2)

    def body_b(k1, carry):
        x = jnp.dot(fb_ref[...], _fft_load_k1(s_ref, k1, plan), preferred_element_type=F32)
        h = h_ref[k1].astype(F32)
        xr, xi, hr, hi = x[:half], x[half:], h[:half], h[half:]
        y = jnp.concatenate([xr * hr - xi * hi, xr * hi + xi * hr], axis=0).astype(BF16)
        b = jnp.dot(fbi_ref[...], y, preferred_element_type=F32)
        for s in range(n_slabs):
            s_ref[s, pl.ds(k1, FFT_N2, stride=plan.PA), :] = b[:half, s * LANES:(s + 1) * LANES]
            s_ref[s, pl.ds(plan.KP + k1, FFT_N2, stride=plan.PA), :] = b[half:, s * LANES:(s + 1) * LANES]
        return carry
    lax.fori_loop(0, plan.KH, body_b, 0, unroll=3)

    def body_a(n2, carry):
        b = jnp.concatenate([s_ref[s, pl.ds(n2 * plan.PA, 2 * plan.KP), :] for s in range(n_slabs)], axis=-1)
        y = jnp.dot(gi_ref[n2], b.astype(BF16), preferred_element_type=F32)
        for s in range(n_slabs):
            ys[s, pl.ds(n2, plan.N1 // 2, stride=FFT_N2), :] = y[:, s * LANES:(s + 1) * LANES]
        return carry
    lax.fori_loop(0, FFT_N2, body_a, 0, unroll=FFT_UNROLL)


def fft_filter_spectrum(k, tabs):
    n_slabs, N, _ = k.shape
    plan = _FftPlan(N // 2)
    C = n_slabs * LANES
    full = lambda shape: pl.BlockSpec(shape, lambda i: (0,) * len(shape))
    return pl.pallas_call(
        functools.partial(_fft_filter_kernel, plan=plan),
        grid=(1,),
        in_specs=[full((n_slabs, plan.N, LANES)), full((FFT_N2, 2 * plan.KP, plan.N1)),
                  full((2 * FFT_N2, 2 * FFT_N2))],
        out_specs=full((plan.KH, 2 * FFT_N2, C)),
        out_shape=jax.ShapeDtypeStruct((plan.KH, 2 * FFT_N2, C), BF16),
        scratch_shapes=[pltpu.VMEM((n_slabs, FFT_N2 * plan.PA, LANES), F32)],
        compiler_params=_cparams("arbitrary"),
        name="hyena_filter_fft",
    )(k, tabs["ga_full"], tabs["fb"])


def fft_long_conv(z, h_spec, tabs):
    B, n_slabs, L, _ = z.shape
    plan = _FftPlan(L)
    C = n_slabs * LANES
    full = lambda shape: pl.BlockSpec(shape, lambda b: (0,) * len(shape))
    seq = pl.BlockSpec((1, n_slabs, L, LANES), lambda b: (b, 0, 0, 0))
    return pl.pallas_call(
        functools.partial(_fft_conv_kernel, plan=plan),
        grid=(B,),
        in_specs=[seq, full((FFT_N2, 2 * plan.KP, plan.N1 // 2)), full((FFT_N2, plan.N1 // 2, 2 * plan.KP)),
                  full((2 * FFT_N2, 2 * FFT_N2)), full((2 * FFT_N2, 2 * FFT_N2)),
                  full((plan.KH, 2 * FFT_N2, C))],
        out_specs=seq,
        out_shape=jax.ShapeDtypeStruct((B, n_slabs, L, LANES), F32),
        scratch_shapes=[pltpu.VMEM((n_slabs, FFT_N2 * plan.PA, LANES), F32)],
        compiler_params=_cparams("parallel"),
        name="hyena_fft_conv",
    )(z, tabs["ga_half"], tabs["ga_inv"], tabs["fb"], tabs["fb_inv"], h_spec)


def _hyena_pre_slab_kernel(u_ref, w_ref, b_ref, z_ref, upad):
    L = u_ref.shape[1]
    T = _time_chunk(L)
    C = D_GROUP
    _fill_padded(upad, u_ref, L, T)

    def body(j, carry):
        base = pl.multiple_of(j * T, T)
        for s in range(C // LANES):
            x1 = _dw_conv_slab(upad, C // LANES + s, base, T, w_ref, b_ref, C + s * LANES, HY_SHORT, 1)
            v = _dw_conv_slab(upad, 2 * C // LANES + s, base, T, w_ref, b_ref, 2 * C + s * LANES, HY_SHORT, 1)
            z_ref[0, s, pl.ds(base, T), :] = x1 * v
        return carry
    lax.fori_loop(0, L // T, body, 0)


def _hyena_post_slab_kernel(u0_ref, z_ref, y_ref, w_ref, b_ref, bias_ref, o_ref, upad):
    L = u0_ref.shape[1]
    T = _time_chunk(L)
    C = D_GROUP
    _fill_padded(upad, u0_ref, L, T)

    def body(j, carry):
        base = pl.multiple_of(j * T, T)
        x0 = _short_conv(upad, base, T, w_ref, b_ref, 0, C)
        z = jnp.concatenate([z_ref[0, s, pl.ds(base, T), :] for s in range(C // LANES)], axis=-1)
        y = jnp.concatenate([y_ref[0, s, pl.ds(base, T), :] for s in range(C // LANES)], axis=-1)
        o_ref[0, pl.ds(base, T), :] = x0 * (y + z * bias_ref[...])
        return carry
    lax.fori_loop(0, L // T, body, 0)


def hyena_mixer_fft(u, lp, tabs):
    B, L, _ = u.shape
    C = D_GROUP
    n_slabs = C // LANES
    w, bsh = lp["hy_short_w"], lp["hy_short_b"].reshape(1, 3 * C)
    useq = pl.BlockSpec((1, L, 3 * C), lambda b: (b, 0, 0))
    slabs = pl.BlockSpec((1, n_slabs, L, LANES), lambda b: (b, 0, 0, 0))
    wspec = pl.BlockSpec((HY_SHORT, 3 * C), lambda b: (0, 0))
    bspec = pl.BlockSpec((1, 3 * C), lambda b: (0, 0))
    pad_scratch = pltpu.VMEM((3 * C // LANES, L + 2 * CONV_MARGIN, LANES), F32)
    z = pl.pallas_call(
        _hyena_pre_slab_kernel,
        grid=(B,),
        in_specs=[useq, wspec, bspec],
        out_specs=slabs,
        out_shape=jax.ShapeDtypeStruct((B, n_slabs, L, LANES), F32),
        scratch_shapes=[pad_scratch],
        compiler_params=_cparams("parallel"),
        name="hyena_pre",
    )(u, w, bsh)
    y = fft_long_conv(z, fft_filter_spectrum(hyena_filter(L, lp), tabs), tabs)
    return pl.pallas_call(
        _hyena_post_slab_kernel,
        grid=(B,),
        in_specs=[pl.BlockSpec((1, L, C), lambda b: (b, 0, 0)), slabs, slabs, wspec, bspec,
                  pl.BlockSpec((1, C), lambda b: (0, 0))],
        out_specs=pl.BlockSpec((1, L, C), lambda b: (b, 0, 0)),
        out_shape=jax.ShapeDtypeStruct((B, L, C), F32),
        scratch_shapes=[pltpu.VMEM((n_slabs, L + 2 * CONV_MARGIN, LANES), F32)],
        compiler_params=_cparams("parallel"),
        name="hyena_post",
    )(u, z, y, w, bsh, lp["hy_bias"].reshape(1, C))


def _filter_gen_kernel(z_ref, zr_ref, w1_ref, b1_ref, w2_ref, b2_ref, w3_ref, dl_ref, k_ref, nrm):
    L = z_ref.shape[0]
    T = min(L, 512)
    C = D_GROUP
    hi = lax.Precision.HIGHEST

    def decayed(zz):
        h1 = jnp.sin(jnp.dot(zz, w1_ref[...], preferred_element_type=F32, precision=hi) + b1_ref[...])
        h2 = jnp.sin(jnp.dot(h1, w2_ref[...], preferred_element_type=F32, precision=hi) + b2_ref[...])
        h = jnp.dot(h2, w3_ref[...], preferred_element_type=F32, precision=hi)
        decay = jnp.exp(-zz[:, 0:1] * dl_ref[...])
        return h[:, :C] * decay, h[:, C:] * decay

    nrm[...] = jnp.zeros_like(nrm)

    def emit(j, carry):
        base = pl.multiple_of(j * T, T)
        fwd, _ = decayed(z_ref[pl.ds(base, T), :])
        _, bwd_rev = decayed(zr_ref[pl.ds(base, T), :])
        nrm[...] = nrm[...] + jnp.sum(jnp.abs(fwd) + jnp.abs(bwd_rev), axis=0, keepdims=True)
        for s in range(C // LANES):
            k_ref[s, pl.ds(base, T), :] = fwd[:, s * LANES:(s + 1) * LANES]
            k_ref[s, pl.ds(L + base, T), :] = bwd_rev[:, s * LANES:(s + 1) * LANES]
        return carry
    lax.fori_loop(0, L // T, emit, 0)

    inv = 1.0 / (nrm[...] + EPS)

    def scale(j, carry):
        base = pl.multiple_of(j * T, T)
        row = base + lax.broadcasted_iota(jnp.int32, (T, 1), 0)
        for s in range(C // LANES):
            inv_s = inv[:, s * LANES:(s + 1) * LANES]
            k_ref[s, pl.ds(base, T), :] = k_ref[s, pl.ds(base, T), :] * inv_s
            k_ref[s, pl.ds(L + base, T), :] = jnp.where(row == 0, 0.0, k_ref[s, pl.ds(L + base, T), :] * inv_s)
        return carry
    lax.fori_loop(0, L // T, scale, 0)


def hyena_filter(L, lp):
    C = D_GROUP
    n_slabs = C // LANES
    t = jnp.linspace(0.0, 1.0, L, dtype=F32)[:, None]
    bands = (HY_EMB - 1) // 2
    w = 2.0 * math.pi * jnp.arange(L, dtype=F32)[:, None] / L
    f = jnp.linspace(1e-4, bands - 1, bands, dtype=F32)[None]
    z = jnp.concatenate([t, jnp.cos(f * w), -jnp.sin(f * w)], axis=-1)
    z = jnp.pad(z, ((0, 0), (0, LANES - HY_EMB)))
    z_rev = jnp.concatenate([z[:1], z[1:][::-1]], axis=0)
    w1 = jnp.pad(lp["hy_ffn_w1"], ((0, LANES - HY_EMB), (0, 0)))
    H = w1.shape[1]
    max_decay = math.log(HY_TARGET) / HY_FAST_DECAY
    min_decay = math.log(HY_TARGET) / HY_SLOW_DECAY
    abs_deltas = jnp.abs(jnp.linspace(min_decay, max_decay, C, dtype=F32)).reshape(1, C)
    full = lambda *shape: pl.BlockSpec(shape, lambda i: (0,) * len(shape))
    return pl.pallas_call(
        _filter_gen_kernel,
        grid=(1,),
        in_specs=[full(L, LANES), full(L, LANES), full(LANES, H), full(1, H), full(H, H), full(1, H),
                  full(H, 2 * C), full(1, C)],
        out_specs=full(n_slabs, 2 * L, LANES),
        out_shape=jax.ShapeDtypeStruct((n_slabs, 2 * L, LANES), F32),
        scratch_shapes=[pltpu.VMEM((1, C), F32)],
        compiler_params=_cparams("arbitrary"),
        name="hyena_filter_gen",
    )(z, z_rev, w1, lp["hy_ffn_b1"].reshape(1, H), lp["hy_ffn_w2"], lp["hy_ffn_b2"].reshape(1, H),
      lp["hy_ffn_w3"], abs_deltas)


def _layer(hc, hx, c_silu_all, lp, need_ctx, final_g, final_norm, tables_x, tables_c, experts, layer):
    B, S, D = hx.shape
    C = hc.shape[1]
    mod = small_linear(c_silu_all, lp["ada_w"], lp["ada_b"])
    mod_x = mod[:B].reshape(B, 6, 1, D)
    mod_c = jnp.broadcast_to(mod[B].reshape(1, 6, 1, D), (B, 6, 1, D))
    w_ext = extend_w_in(lp["w_in"])
    cos_x, sin_x = rope_tables(S, True)
    cos_c, sin_c = rope_tables(C, False)
    tm_x = 512
    if can_fuse_weight_cast(B * (S // tm_x), experts):
        (hy_x, cf_x, at_x, lr_x), experts_bf16 = in_proj(hx, mod_x[:, 0], mod_x[:, 1], lp["norm1_g"], w_ext,
                                                         cos_x, sin_x, tm=tm_x, cast_weights=experts, layer=layer)
    else:
        (hy_x, cf_x, at_x, lr_x), _ = in_proj(hx, mod_x[:, 0], mod_x[:, 1], lp["norm1_g"], w_ext, cos_x, sin_x,
                                              tm=tm_x)
        experts_bf16 = tuple(w[layer:layer + 1].astype(BF16) for w in experts)
    (hy_c, cf_c, at_c, lr_c), _ = in_proj(hc, mod_c[:, 0], mod_c[:, 1], lp["norm1_g"], w_ext, cos_c, sin_c, tm=256)

    yd_c, yd_x = rglru_mixer(lr_c, lr_x, lp, need_ctx)
    conf = lambda u: conformer_conv(u, lp["conf_dw_w"], lp["conf_dw_b"], lp["conf_ln_g"], lp["conf_ln_b"])
    ys_x = [hyena_mixer_fft(hy_x, lp, tables_x), conf(cf_x),
            window_attention(at_x, at_c, lp["attn_sink"]), yd_x]

    w_out = lp["w_out"].astype(BF16)
    w_router = jnp.zeros((D, ROUTER_COLS), F32)
    w_router = w_router.at[:, :N_GROUPS].set(lp["router_g_w"]).at[:, N_GROUPS:N_GROUPS + N_EXPERTS].set(lp["router_e_w"])
    w_router = w_router.astype(BF16)
    b_router = jnp.zeros((1, ROUTER_COLS), F32)
    b_router = b_router.at[0, :N_GROUPS].set(lp["router_g_b"]).at[0, N_GROUPS:N_GROUPS + N_EXPERTS].set(lp["router_e_b"])

    hx1, lg_x = out_proj(ys_x, hx, mod_x[:, 2], lp["group_norm_g"], w_out, lp["norm2_g"],
                         mod_x[:, 3], mod_x[:, 4], w_router, b_router, tm=512)
    h_tok = hx1.reshape(B * S, D)
    hc_tok = None
    lg = lg_x.reshape(B * S, ROUTER_COLS)
    if need_ctx:
        ys_c = [hyena_mixer(hy_c, lp, tables_c), conf(cf_c),
                context_attention(at_c, lp["attn_sink"]), yd_c]
        hc1, lg_c = out_proj(ys_c, hc, mod_c[:, 2], lp["group_norm_g"], w_out, lp["norm2_g"],
                             mod_c[:, 3], mod_c[:, 4], w_router, b_router, tm=256)
        hc_tok = hc1.reshape(B * C, D)
        lg = jnp.concatenate([lg, lg_c.reshape(B * C, ROUTER_COLS)], axis=0)

    T = lg.shape[0]
    n_blocks = -(-T // MOE_BLOCK) + N_CLASSES
    info, counts, ids, zero_slots = route_tokens(lg, n_blocks * MOE_BLOCK * TOKEN_TILE_ROWS)
    dest, blk_a, blk_b, n_used = slot_plan(ids, counts, n_blocks)
    xs = moe_dispatch(h_tok, hc_tok, dest, zero_slots, lp["norm2_g"], (mod_x[:, 3], mod_x[:, 4]),
                      (mod_c[:, 3], mod_c[:, 4]), S)
    o_sorted = expert_pairs(xs, n_blocks, blk_a, blk_b, n_used, *experts_bf16, 0)
    hx2 = moe_collect(o_sorted, dest, info, 0, h_tok, mod_x[:, 5], S, final_g, final_norm)
    hx2 = hx2.reshape(B, S, D)
    if need_ctx:
        hc2 = moe_collect(o_sorted, dest, info, B * S // COLLECT_TOKENS, hc_tok, mod_c[:1, 5], C,
                          final_g, False).reshape(B, C, D)
    else:
        hc2 = hc
    return hc2, hx2


def kernel(x, c, ctx, c_ctx, norm1_g, norm2_g, ada_w, ada_b, w_in, hy_short_w, hy_short_b, hy_ffn_w1, hy_ffn_b1, hy_ffn_w2, hy_ffn_b2, hy_ffn_w3, hy_bias, conf_dw_w, conf_dw_b, conf_ln_g, conf_ln_b, attn_sink, lru_conv_w, lru_conv_b, lru_wa, lru_ba, lru_wx, lru_bx, lru_lambda, group_norm_g, w_out, router_g_w, router_g_b, router_e_w, router_e_b, exp_w_gate, exp_w_up, exp_w_down, final_norm_g):
    stacked = dict(norm1_g=norm1_g, norm2_g=norm2_g, ada_w=ada_w, ada_b=ada_b, w_in=w_in,
                   hy_short_w=hy_short_w, hy_short_b=hy_short_b, hy_ffn_w1=hy_ffn_w1, hy_ffn_b1=hy_ffn_b1,
                   hy_ffn_w2=hy_ffn_w2, hy_ffn_b2=hy_ffn_b2, hy_ffn_w3=hy_ffn_w3, hy_bias=hy_bias,
                   conf_dw_w=conf_dw_w, conf_dw_b=conf_dw_b, conf_ln_g=conf_ln_g, conf_ln_b=conf_ln_b,
                   attn_sink=attn_sink, lru_conv_w=lru_conv_w, lru_conv_b=lru_conv_b, lru_wa=lru_wa,
                   lru_ba=lru_ba, lru_wx=lru_wx, lru_bx=lru_bx, lru_lambda=lru_lambda,
                   group_norm_g=group_norm_g, w_out=w_out, router_g_w=router_g_w, router_g_b=router_g_b,
                   router_e_w=router_e_w, router_e_b=router_e_b)
    experts = (exp_w_gate, exp_w_up, exp_w_down)
    depth = norm1_g.shape[0]
    B = x.shape[0]
    cs = jnp.concatenate([jax.nn.silu(c), jnp.broadcast_to(jax.nn.silu(c_ctx)[None], (8, c.shape[1]))], axis=0)
    hc, hx = ctx, x
    tables_x = fft_tables(x.shape[1])
    tables_c = dft_tables(ctx.shape[1])
    for l in range(depth):
        lp = {k: v[l] for k, v in stacked.items()}
        hc, hx = _layer(hc, hx, cs, lp, need_ctx=(l < depth - 1), final_g=final_norm_g,
                        final_norm=(l == depth - 1), tables_x=tables_x, tables_c=tables_c,
                        experts=experts, layer=l)
    return hx
```

```python
import functools
import math

import jax
import jax.numpy as jnp
from jax import lax
from jax.experimental import pallas as pl
from jax.experimental.pallas import tpu as pltpu

F32 = jnp.float32
BF16 = jnp.bfloat16

EPS = 1e-6
NEG_INF = -1e30
GRID_W = 64
D_GROUP = 256
HY_COLS = 3 * D_GROUP
CONF_COLS = 2 * D_GROUP
ATT_HEADS = 4
ATT_KV_HEADS = 2
HEAD_DIM = 64
ATT_COLS = (ATT_HEADS + 2 * ATT_KV_HEADS) * HEAD_DIM
LRU_COLS = 2 * D_GROUP
QK_COLS = (ATT_HEADS + ATT_KV_HEADS) * HEAD_DIM
WINDOW = 128
ATT_BLOCK = 128
ROPE_BASE = 10000.0
HY_EMB = 33
HY_FAST_DECAY = 0.3
HY_SLOW_DECAY = 1.5
HY_TARGET = 1e-2
CONF_KERNEL = 31
LRU_CONV = 4
LRU_C = 8.0
N_GROUPS = 4
EXP_PER_GROUP = 8
N_EXPERTS = N_GROUPS * EXP_PER_GROUP
MOE_BLOCK = 256
ROUTER_COLS = 128

VMEM_LIMIT_BYTES = 56 * 1024 * 1024


def _cparams(*sem):
    return pltpu.CompilerParams(dimension_semantics=sem, vmem_limit_bytes=VMEM_LIMIT_BYTES)


def _linear_kernel(x_ref, w_ref, b_ref, o_ref):
    o_ref[...] = jnp.dot(x_ref[...], w_ref[...], preferred_element_type=F32,
                         precision=lax.Precision.HIGHEST) + b_ref[...]


def small_linear(x, w, b, tn=1024):
    M, K = x.shape
    N = w.shape[1]
    return pl.pallas_call(
        _linear_kernel,
        grid=(N // tn,),
        in_specs=[pl.BlockSpec((M, K), lambda j: (0, 0)),
                  pl.BlockSpec((K, tn), lambda j: (0, j)),
                  pl.BlockSpec((1, tn), lambda j: (0, j))],
        out_specs=pl.BlockSpec((M, tn), lambda j: (0, j)),
        out_shape=jax.ShapeDtypeStruct((M, N), F32),
        compiler_params=_cparams("parallel"),
        name="ada_linear",
    )(x, w, b.reshape(1, N))


def _in_proj_kernel(x_ref, sh_ref, sc_ref, g_ref, w_ref, cos_ref, sin_ref,
                    hy_ref, cf_ref, at_ref, lr_ref, cast_refs=()):
    x = x_ref[0]
    ms = jnp.mean(x * x, axis=-1, keepdims=True)
    y = x * lax.rsqrt(ms + EPS) * g_ref[...]
    y = y * (1.0 + sc_ref[0]) + sh_ref[0]
    u = jnp.dot(y.astype(BF16), w_ref[...], preferred_element_type=F32)
    c0 = HY_COLS
    c1 = c0 + CONF_COLS
    c2 = c1 + ATT_COLS
    c3 = c2 + LRU_COLS
    hy_ref[0] = u[:, :c0]
    cf_ref[0] = u[:, c0:c1]
    lr_ref[0] = u[:, c2:c3]
    qk = u[:, c1:c1 + QK_COLS]
    qk_rot = u[:, c3:c3 + QK_COLS]
    at_ref[0, :, :QK_COLS] = qk * cos_ref[...] + qk_rot * sin_ref[...]
    at_ref[0, :, QK_COLS:] = u[:, c1 + QK_COLS:c2]
    for wf_ref, wb_ref in zip(cast_refs[:len(cast_refs) // 2], cast_refs[len(cast_refs) // 2:]):
        wb_ref[...] = wf_ref[...].astype(BF16)


def _in_proj_body(x_ref, sh_ref, sc_ref, g_ref, w_ref, cos_ref, sin_ref, *refs, n_cast):
    _in_proj_kernel(x_ref, sh_ref, sc_ref, g_ref, w_ref, cos_ref, sin_ref, *refs[n_cast:n_cast + 4],
                    cast_refs=refs[:n_cast] + refs[n_cast + 4:])


def can_fuse_weight_cast(n_steps, weights):
    for w in weights:
        E, R = w.shape[1:3]
        if n_steps % E or R % (n_steps // E) or (R // (n_steps // E)) % 16:
            return False
    return True


def in_proj(h, shift, scale, g, w_ext, cos_t, sin_t, tm, cast_weights=(), layer=0):
    B, L, D = h.shape
    NW = w_ext.shape[1]
    n_i = L // tm
    outs = [HY_COLS, CONF_COLS, ATT_COLS, LRU_COLS]
    cast_in, cast_out, cast_shape = [], [], []
    for w in cast_weights:
        E, R, Cw = w.shape[1:]
        per_e = B * n_i // E
        chunk = (1, 1, R // per_e, Cw)
        cast_in.append(pl.BlockSpec(chunk, lambda b, i, per_e=per_e: (layer, (b * n_i + i) // per_e,
                                                                        (b * n_i + i) % per_e, 0)))
        cast_out.append(pl.BlockSpec(chunk, lambda b, i, per_e=per_e: (0, (b * n_i + i) // per_e,
                                                                         (b * n_i + i) % per_e, 0)))
        cast_shape.append(jax.ShapeDtypeStruct((1, E, R, Cw), BF16))
    res = pl.pallas_call(
        functools.partial(_in_proj_body, n_cast=len(cast_weights)),
        grid=(B, n_i),
        in_specs=[pl.BlockSpec((1, tm, D), lambda b, i: (b, i, 0)),
                  pl.BlockSpec((1, 1, D), lambda b, i: (b, 0, 0)),
                  pl.BlockSpec((1, 1, D), lambda b, i: (b, 0, 0)),
                  pl.BlockSpec((1, D), lambda b, i: (0, 0)),
                  pl.BlockSpec((D, NW), lambda b, i: (0, 0)),
                  pl.BlockSpec((tm, QK_COLS), lambda b, i: (i, 0)),
                  pl.BlockSpec((tm, QK_COLS), lambda b, i: (i, 0))] + cast_in,
        out_specs=[pl.BlockSpec((1, tm, n), lambda b, i: (b, i, 0)) for n in outs] + cast_out,
        out_shape=[jax.ShapeDtypeStruct((B, L, n), F32) for n in outs] + cast_shape,
        compiler_params=_cparams("arbitrary", "arbitrary") if cast_weights else _cparams("parallel", "parallel"),
        name="in_proj",
    )(h, shift, scale, g.reshape(1, D), w_ext, cos_t, sin_t, *cast_weights)
    return res[:4], tuple(res[4:])


def rope_tables(L, rotary):
    n_heads = ATT_HEADS + ATT_KV_HEADS
    if not rotary:
        return jnp.ones((L, QK_COLS), F32), jnp.zeros((L, QK_COLS), F32)
    pos = jnp.arange(L)
    row = (pos // GRID_W).astype(F32)
    col = (pos % GRID_W).astype(F32)
    half = HEAD_DIM // 2
    inv_freq = ROPE_BASE ** (-jnp.arange(0, half, 2, dtype=F32) / half)
    ang_r = row[:, None] * inv_freq[None]
    ang_c = col[:, None] * inv_freq[None]
    cos_h = jnp.concatenate([jnp.cos(ang_r)] * 2 + [jnp.cos(ang_c)] * 2, axis=-1)
    sin_h = jnp.concatenate([jnp.sin(ang_r)] * 2 + [jnp.sin(ang_c)] * 2, axis=-1)
    return jnp.tile(cos_h, (1, n_heads)), jnp.tile(sin_h, (1, n_heads))


def extend_w_in(w_in):
    c1 = HY_COLS + CONF_COLS
    wqk = w_in[:, c1:c1 + QK_COLS]
    D = w_in.shape[0]
    w4 = wqk.reshape(D, QK_COLS // 32, 2, 16)
    wrot = jnp.stack([-w4[:, :, 1], w4[:, :, 0]], axis=2).reshape(D, QK_COLS)
    return jnp.concatenate([w_in, wrot], axis=1).astype(BF16)


def _softmax_parts(q, k_list, extra_logit):
    scale = HEAD_DIM ** -0.5
    s_list = []
    for k, mask in k_list:
        s = lax.dot_general(q, k, (((1,), (1,)), ((), ())), preferred_element_type=F32) * scale
        if mask is not None:
            s = jnp.where(mask, s, NEG_INF)
        s_list.append(s)
    m = extra_logit
    for s in s_list:
        m = jnp.maximum(m, jnp.max(s, axis=-1, keepdims=True))
    p_list = [jnp.exp(s - m) for s in s_list]
    denom = jnp.exp(extra_logit - m)
    for p in p_list:
        denom = denom + jnp.sum(p, axis=-1, keepdims=True)
    return p_list, 1.0 / denom


ATT_Q_BLOCKS = 16


def _win_attn_kernel(sink_ref, q_ref, kp_ref, kc_ref, kn_ref, vp_ref, vc_ref, vn_ref,
                     kx_ref, vx_ref, o_ref, *, seq_len):
    i = pl.program_id(1)
    blk = ATT_BLOCK
    qb = q_ref.shape[1] // blk
    scale = HEAD_DIM ** -0.5
    g = ATT_HEADS // ATT_KV_HEADS
    kw = jnp.concatenate([kp_ref[0], kc_ref[0], kn_ref[0]], axis=0)
    vw = jnp.concatenate([vp_ref[0], vc_ref[0], vn_ref[0]], axis=0).astype(BF16)
    kwt = kw.T.astype(BF16)
    kxt = kx_ref[0].T.astype(BF16)
    vx = vx_ref[0].astype(BF16)
    row = lax.broadcasted_iota(jnp.int32, (g * blk, 3 * blk), 0) % blk
    col = lax.broadcasted_iota(jnp.int32, (g * blk, 3 * blk), 1)
    band_bias = jnp.where(jnp.abs(col - blk - row) <= WINDOW, 0.0, NEG_INF)
    col1 = lax.broadcasted_iota(jnp.int32, (1, 3 * blk), 1)
    for j in range(qb):
        q_blk = i * qb + j
        k_pos = (q_blk - 1) * blk + col1
        edge_bias = jnp.where(k_pos >= 0, jnp.where(k_pos < seq_len, 0.0, NEG_INF), NEG_INF)
        bias = band_bias + edge_bias
        outs = []
        for kv in range(ATT_KV_HEADS):
            ksl = slice(kv * HEAD_DIM, (kv + 1) * HEAD_DIM)
            heads = range(kv * g, (kv + 1) * g)
            qs = (jnp.concatenate([q_ref[0, j * blk:(j + 1) * blk, h * HEAD_DIM:(h + 1) * HEAD_DIM]
                                   for h in heads], axis=0) * scale).astype(BF16)
            sink = jnp.concatenate([jnp.full((blk, 1), sink_ref[h], F32) for h in heads], axis=0)
            s_win = jnp.dot(qs, kwt[ksl, j * blk:(j + 3) * blk], preferred_element_type=F32) + bias
            s_ctx = jnp.dot(qs, kxt[ksl, :], preferred_element_type=F32)
            m = jnp.maximum(jnp.maximum(jnp.max(s_win, axis=-1, keepdims=True),
                                        jnp.max(s_ctx, axis=-1, keepdims=True)), sink)
            p_win = jnp.exp(s_win - m)
            p_ctx = jnp.exp(s_ctx - m)
            denom = (jnp.exp(sink - m) + jnp.sum(p_win, axis=-1, keepdims=True)
                     + jnp.sum(p_ctx, axis=-1, keepdims=True))
            o = (jnp.dot(p_win.astype(BF16), vw[j * blk:(j + 3) * blk, ksl], preferred_element_type=F32)
                 + jnp.dot(p_ctx.astype(BF16), vx[:, ksl], preferred_element_type=F32)) * (1.0 / denom)
            outs.extend([o[k * blk:(k + 1) * blk] for k in range(g)])
        o_ref[0, j * blk:(j + 1) * blk, :] = jnp.concatenate(outs, axis=-1)


def window_attention(at_x, at_c, sink):
    B, S, _ = at_x.shape
    C = at_c.shape[1]
    blk = ATT_BLOCK
    qb = ATT_Q_BLOCKS
    nb = S // blk
    kcol = QK_COLS // 128 - 1
    vcol = kcol + 1

    def edge_spec(col, off):
        return pl.BlockSpec((1, blk, 128), lambda b, i, s: (b, jnp.clip(i * qb + off, 0, nb - 1), col))

    def mid_spec(col):
        return pl.BlockSpec((1, qb * blk, 128), lambda b, i, s: (b, i, col))

    grid_spec = pltpu.PrefetchScalarGridSpec(
        num_scalar_prefetch=1,
        grid=(B, nb // qb),
        in_specs=[pl.BlockSpec((1, qb * blk, ATT_HEADS * HEAD_DIM), lambda b, i, s: (b, i, 0)),
                  edge_spec(kcol, -1), mid_spec(kcol), edge_spec(kcol, qb),
                  edge_spec(vcol, -1), mid_spec(vcol), edge_spec(vcol, qb),
                  pl.BlockSpec((1, C, 128), lambda b, i, s: (b, 0, kcol)),
                  pl.BlockSpec((1, C, 128), lambda b, i, s: (b, 0, vcol))],
        out_specs=pl.BlockSpec((1, qb * blk, ATT_HEADS * HEAD_DIM), lambda b, i, s: (b, i, 0)),
    )
    return pl.pallas_call(
        functools.partial(_win_attn_kernel, seq_len=S),
        grid_spec=grid_spec,
        out_shape=jax.ShapeDtypeStruct((B, S, ATT_HEADS * HEAD_DIM), F32),
        compiler_params=_cparams("parallel", "parallel"),
        name="window_attention",
    )(sink.astype(F32), at_x, at_x, at_x, at_x, at_x, at_x, at_x, at_c, at_c)


def _ctx_attn_kernel(sink_ref, q_ref, kx_ref, vx_ref, o_ref):
    q = q_ref[0].astype(BF16)
    kx = kx_ref[0].astype(BF16)
    vx = vx_ref[0].astype(BF16)
    g = ATT_HEADS // ATT_KV_HEADS
    outs = []
    for h in range(ATT_HEADS):
        kv = h // g
        qs = q[:, h * HEAD_DIM:(h + 1) * HEAD_DIM]
        ksl = slice(kv * HEAD_DIM, (kv + 1) * HEAD_DIM)
        (p_ctx,), inv = _softmax_parts(qs, [(kx[:, ksl], None)], sink_ref[h])
        outs.append(jnp.dot(p_ctx.astype(BF16), vx[:, ksl], preferred_element_type=F32) * inv)
    o_ref[0] = jnp.concatenate(outs, axis=-1)


def context_attention(at_c, sink):
    B, C, _ = at_c.shape
    kcol = QK_COLS // 128 - 1
    grid_spec = pltpu.PrefetchScalarGridSpec(
        num_scalar_prefetch=1,
        grid=(B,),
        in_specs=[pl.BlockSpec((1, C, ATT_HEADS * HEAD_DIM), lambda b, s: (b, 0, 0)),
                  pl.BlockSpec((1, C, 128), lambda b, s: (b, 0, kcol)),
                  pl.BlockSpec((1, C, 128), lambda b, s: (b, 0, kcol + 1))],
        out_specs=pl.BlockSpec((1, C, ATT_HEADS * HEAD_DIM), lambda b, s: (b, 0, 0)),
    )
    return pl.pallas_call(
        _ctx_attn_kernel,
        grid_spec=grid_spec,
        out_shape=jax.ShapeDtypeStruct((B, C, ATT_HEADS * HEAD_DIM), F32),
        compiler_params=_cparams("parallel"),
        name="context_attention",
    )(sink.astype(F32), at_c, at_c, at_c)


def _out_proj_kernel(y0_ref, y1_ref, y2_ref, y3_ref, h_ref, g1_ref, gng_ref, w_ref,
                     n2g_ref, sh_ref, sc_ref, wr_ref, br_ref, ho_ref, lg_ref):
    parts = []
    for k, y_ref in enumerate((y0_ref, y1_ref, y2_ref, y3_ref)):
        y = y_ref[0]
        ms = jnp.mean(y * y, axis=-1, keepdims=True)
        yn = y * lax.rsqrt(ms + EPS) * gng_ref[:, k * D_GROUP:(k + 1) * D_GROUP]
        parts.append(yn.astype(BF16))
    yn = jnp.concatenate(parts, axis=-1)
    proj = jnp.dot(yn, w_ref[...], preferred_element_type=F32)
    h = h_ref[0] + g1_ref[0] * proj
    ho_ref[0] = h
    ms = jnp.mean(h * h, axis=-1, keepdims=True)
    n = h * lax.rsqrt(ms + EPS) * n2g_ref[...]
    n = n * (1.0 + sc_ref[0]) + sh_ref[0]
    lg_ref[0] = jnp.dot(n.astype(BF16), wr_ref[...], preferred_element_type=F32) + br_ref[...]


def out_proj(ys, h, g1, gng, w_out, n2g, sh2, sc2, w_router, b_router, tm):
    B, L, D = h.shape
    row3 = lambda n: pl.BlockSpec((1, tm, n), lambda b, i: (b, i, 0))
    mod = pl.BlockSpec((1, 1, D), lambda b, i: (b, 0, 0))
    full = lambda r, c: pl.BlockSpec((r, c), lambda b, i: (0, 0))
    return pl.pallas_call(
        _out_proj_kernel,
        grid=(B, L // tm),
        in_specs=[row3(D_GROUP)] * 4 + [row3(D), mod, full(1, D), full(D, D), full(1, D), mod, mod,
                                        full(D, ROUTER_COLS), full(1, ROUTER_COLS)],
        out_specs=[row3(D), row3(ROUTER_COLS)],
        out_shape=[jax.ShapeDtypeStruct((B, L, D), F32), jax.ShapeDtypeStruct((B, L, ROUTER_COLS), F32)],
        compiler_params=_cparams("parallel", "parallel"),
        name="out_proj",
    )(*ys, h, g1, gng.reshape(1, D), w_out, n2g.reshape(1, D), sh2, sc2, w_router, b_router)


N_PAIRS = EXP_PER_GROUP * (EXP_PER_GROUP - 1) // 2
N_CLASSES = N_GROUPS * N_PAIRS
ROUTE_TOKENS = 512
INFO_CLASS, INFO_RANK, INFO_WA, INFO_WB = 0, 1, 2, 3


SUBLANES = 8


def _route_kernel(lg_ref, below_ref, info_ref, cnt_ref, ids_ref, slots_ref, run):
    i = pl.program_id(0)

    @pl.when(i == 0)
    def _():
        run[...] = jnp.zeros_like(run)

    lg = lg_ref[...]
    li = lax.broadcasted_iota(jnp.int32, lg.shape, 1).astype(F32)
    big = float(ROUTER_COLS)

    def first_argmax(vals):
        m = jnp.max(vals, axis=-1, keepdims=True)
        return m, jnp.min(jnp.where(vals == m, li, big), axis=-1, keepdims=True)

    gl = jnp.where(li < N_GROUPS, lg, NEG_INF)
    gmax, g_idx = first_argmax(gl)
    g_prob = 1.0 / jnp.sum(jnp.exp(gl - gmax), axis=-1, keepdims=True)
    lo = N_GROUPS + EXP_PER_GROUP * g_idx
    el = jnp.where(li >= lo, jnp.where(li < lo + EXP_PER_GROUP, lg, NEG_INF), NEG_INF)
    m1, i1 = first_argmax(el)
    m2, i2 = first_argmax(jnp.where(li == i1, NEG_INF, el))
    e2 = jnp.exp(m2 - m1)
    w1 = g_prob / (1.0 + e2)
    w2 = g_prob * e2 / (1.0 + e2)
    j1 = i1 - lo
    j2 = i2 - lo
    a = jnp.minimum(j1, j2)
    b = jnp.maximum(j1, j2)
    cls = g_idx * N_PAIRS + (a * (2 * EXP_PER_GROUP - 1 - a)) * 0.5 + (b - a - 1.0)
    w_a = jnp.where(j1 < j2, w1, w2)
    w_b = jnp.where(j1 < j2, w2, w1)

    hit = li == cls
    onehot = jnp.where(hit, 1.0, 0.0)
    before = jnp.dot(below_ref[...], onehot.astype(BF16), preferred_element_type=F32)
    rank = jnp.sum(jnp.where(hit, before + run[...], 0.0), axis=-1, keepdims=True)
    run[...] = run[...] + jnp.sum(onehot, axis=0, keepdims=True)
    cnt_ref[...] = run[...]
    info = jnp.where(li == INFO_CLASS, cls, 0.0)
    info = jnp.where(li == INFO_RANK, rank, info)
    info = jnp.where(li == INFO_WA, w_a, info)
    info = jnp.where(li == INFO_WB, w_b, info)
    info_ref[...] = info
    ids_ref[0] = info.T[:SUBLANES].astype(jnp.int32)
    slots_ref[...] = jnp.zeros_like(slots_ref)


def route_tokens(logits, slot_rows):
    T = logits.shape[0]
    tb = ROUTE_TOKENS
    steps = T // tb
    rows_per_step = -(-slot_rows // (steps * SUBLANES)) * SUBLANES
    below = (jnp.arange(tb)[None, :] < jnp.arange(tb)[:, None]).astype(BF16)
    return pl.pallas_call(
        _route_kernel,
        grid=(steps,),
        in_specs=[pl.BlockSpec((tb, ROUTER_COLS), lambda i: (i, 0)),
                  pl.BlockSpec((tb, tb), lambda i: (0, 0))],
        out_specs=[pl.BlockSpec((tb, ROUTER_COLS), lambda i: (i, 0)),
                   pl.BlockSpec((1, ROUTER_COLS), lambda i: (0, 0)),
                   pl.BlockSpec((1, SUBLANES, tb), lambda i: (i, 0, 0)),
                   pl.BlockSpec((rows_per_step, LANES), lambda i: (i, 0))],
        out_shape=[jax.ShapeDtypeStruct((T, ROUTER_COLS), F32), jax.ShapeDtypeStruct((1, ROUTER_COLS), F32),
                   jax.ShapeDtypeStruct((steps, SUBLANES, tb), jnp.int32),
                   jax.ShapeDtypeStruct((steps * rows_per_step, LANES), F32)],
        scratch_shapes=[pltpu.VMEM((1, ROUTER_COLS), F32)],
        compiler_params=_cparams("arbitrary"),
        name="moe_route",
    )(logits, below)


def _pair_tables():
    a_tab, b_tab = [], []
    for g in range(N_GROUPS):
        for a in range(EXP_PER_GROUP):
            for b in range(a + 1, EXP_PER_GROUP):
                a_tab.append(g * EXP_PER_GROUP + a)
                b_tab.append(g * EXP_PER_GROUP + b)
    return jnp.array(a_tab, jnp.int32), jnp.array(b_tab, jnp.int32)


def _slot_kernel(ids_ref, start_ref, dest_ref):
    cls = ids_ref[0, INFO_CLASS:INFO_CLASS + 1, :]
    rank = ids_ref[0, INFO_RANK:INFO_RANK + 1, :]
    ci = lax.broadcasted_iota(jnp.int32, (ROUTER_COLS, cls.shape[1]), 0)
    start = jnp.sum(jnp.where(ci == cls, start_ref[...], 0), axis=0, keepdims=True)
    dest_ref[0] = jnp.broadcast_to(start + rank, dest_ref.shape[1:])


def slot_plan(ids, counts, n_blocks):
    nt, _, tb = ids.shape
    cnt = counts[0, :N_CLASSES].astype(jnp.int32)
    padded = (cnt + MOE_BLOCK - 1) // MOE_BLOCK * MOE_BLOCK
    upto = jnp.arange(N_CLASSES)[None, :] <= jnp.arange(N_CLASSES)[:, None]
    pad_end = jnp.sum(jnp.where(upto, padded[None, :], 0), axis=1)
    class_start = jnp.zeros((ROUTER_COLS, 1), jnp.int32).at[:N_CLASSES, 0].set(pad_end - padded)
    dest = pl.pallas_call(
        _slot_kernel,
        grid=(nt,),
        in_specs=[pl.BlockSpec((1, SUBLANES, tb), lambda i: (i, 0, 0)),
                  pl.BlockSpec((ROUTER_COLS, 1), lambda i: (0, 0))],
        out_specs=pl.BlockSpec((1, SUBLANES, tb), lambda i: (i, 0, 0)),
        out_shape=jax.ShapeDtypeStruct((nt, SUBLANES, tb), jnp.int32),
        compiler_params=_cparams("parallel"),
        name="moe_slots",
    )(ids, class_start)[:, 0, :].reshape(nt * tb)
    n_used = (pad_end[-1] // MOE_BLOCK).astype(jnp.int32).reshape(1)
    blk_first = jnp.arange(n_blocks, dtype=jnp.int32) * MOE_BLOCK
    blk_cls = jnp.minimum(jnp.sum((pad_end[None, :] <= blk_first[:, None]).astype(jnp.int32), axis=1),
                          N_CLASSES - 1)
    a_tab, b_tab = _pair_tables()
    hit = blk_cls[:, None] == jnp.arange(N_CLASSES)[None, :]
    pick = lambda tab: jnp.sum(jnp.where(hit, tab[None, :], 0), axis=1).astype(jnp.int32)
    return dest, pick(a_tab), pick(b_tab), n_used


DISPATCH_TOKENS = 1024
COLLECT_TOKENS = 512


def _wait_rows(buf, sem):
    pltpu.make_async_copy(buf, buf, sem).wait()


DMA_UNROLL = 8
TOKEN_TILE_ROWS = 8


def _store_token_tiles(tiles_ref, offset, pitch, x):
    n = x.shape[0]
    for j in range(x.shape[1] // LANES):
        tiles_ref[pl.ds(offset + j, n, stride=pitch), :] = x[:, j * LANES:(j + 1) * LANES]


def _load_token_tiles(tiles_ref, offset, pitch, n, width):
    return jnp.concatenate([tiles_ref[pl.ds(offset + j, n, stride=pitch), :] for j in range(width // LANES)],
                           axis=-1)


def _dispatch_kernel(dest_ref, hx_ref, hc_ref, g_ref, shx_ref, scx_ref, shc_ref, scc_ref, zeros_hbm,
                     xs_hbm, rows, sems, *, n_latent_blocks):
    del zeros_hbm
    i = pl.program_id(0)
    n = pl.num_programs(0)
    slot = i % 2
    tb = hx_ref.shape[0]

    @pl.when(i >= 2)
    def _():
        _wait_rows(rows.at[slot], sems.at[slot])

    def normed(h_ref, sh_ref, sc_ref):
        h = h_ref[...]
        ms = jnp.mean(h * h, axis=-1, keepdims=True)
        return h * lax.rsqrt(ms + EPS) * g_ref[...] * (1.0 + sc_ref[0]) + sh_ref[0]

    @pl.when(i < n_latent_blocks)
    def _():
        _store_token_tiles(rows.at[slot], 0, TOKEN_TILE_ROWS, normed(hx_ref, shx_ref, scx_ref))

    @pl.when(i >= n_latent_blocks)
    def _():
        _store_token_tiles(rows.at[slot], 0, TOKEN_TILE_ROWS, normed(hc_ref, shc_ref, scc_ref))

    def body(g, carry):
        for u in range(DMA_UNROLL):
            r = g * DMA_UNROLL + u
            dst = pl.multiple_of(dest_ref[0, 0, r] * TOKEN_TILE_ROWS, TOKEN_TILE_ROWS)
            pltpu.make_async_copy(rows.at[slot, pl.ds(r * TOKEN_TILE_ROWS, TOKEN_TILE_ROWS)],
                                  xs_hbm.at[pl.ds(dst, TOKEN_TILE_ROWS)], sems.at[slot]).start(priority=u % 2)
        return carry
    lax.fori_loop(0, tb // DMA_UNROLL, body, 0)

    @pl.when(i == n - 1)
    def _():
        _wait_rows(rows.at[slot], sems.at[slot])

        @pl.when(n >= 2)
        def _():
            _wait_rows(rows.at[1 - slot], sems.at[1 - slot])


def moe_dispatch(h_x, h_c, dest, zero_slots, n2g, mod_x, mod_c, tokens_per_batch):
    Tx, D = h_x.shape
    tb = DISPATCH_TOKENS
    nxb = Tx // tb
    if h_c is None:
        h_c, mod_c, ncb = h_x, mod_x, 0
    else:
        ncb = h_c.shape[0] // tb
    per_b = tokens_per_batch // tb
    tile_rows = D // LANES
    assert tile_rows == TOKEN_TILE_ROWS
    xi = lambda i: jnp.minimum(i, nxb - 1)
    ci = lambda i: jnp.maximum(i - nxb, 0)
    modx = pl.BlockSpec((1, 1, D), lambda i: (xi(i) // per_b, 0, 0))
    modc = pl.BlockSpec((1, 1, D), lambda i: (0, 0, 0))
    return pl.pallas_call(
        functools.partial(_dispatch_kernel, n_latent_blocks=nxb),
        grid=(nxb + ncb,),
        in_specs=[pl.BlockSpec((1, 1, tb), lambda i: (i, 0, 0), memory_space=pltpu.SMEM),
                  pl.BlockSpec((tb, D), lambda i: (xi(i), 0)),
                  pl.BlockSpec((tb, D), lambda i: (ci(i), 0)),
                  pl.BlockSpec((1, D), lambda i: (0, 0)),
                  modx, modx, modc, modc,
                  pl.BlockSpec(memory_space=pl.ANY)],
        out_specs=pl.BlockSpec(memory_space=pl.ANY),
        out_shape=jax.ShapeDtypeStruct(zero_slots.shape, F32),
        scratch_shapes=[pltpu.VMEM((2, tb * tile_rows, LANES), F32), pltpu.SemaphoreType.DMA((2,))],
        input_output_aliases={8: 0},
        compiler_params=_cparams("arbitrary"),
        name="moe_dispatch",
    )(dest.reshape(-1, 1, tb), h_x, h_c, n2g.reshape(1, D), mod_x[0], mod_x[1], mod_c[0], mod_c[1], zero_slots)


def _expert_pair_kernel(ea_ref, eb_ref, nused_ref, xs_ref, wga_ref, wua_ref, wda_ref, wgb_ref, wub_ref, wdb_ref,
                        o_ref):
    del ea_ref, eb_ref
    i = pl.program_id(0)
    D = wga_ref.shape[2]

    @pl.when(i < nused_ref[0])
    def _():
        xb = _load_token_tiles(xs_ref, 0, TOKEN_TILE_ROWS, MOE_BLOCK, D).astype(BF16)
        halves = []
        for wg_ref, wu_ref, wd_ref in ((wga_ref, wua_ref, wda_ref), (wgb_ref, wub_ref, wdb_ref)):
            gate = jnp.dot(xb, wg_ref[0, 0], preferred_element_type=F32)
            up = jnp.dot(xb, wu_ref[0, 0], preferred_element_type=F32)
            hid = (gate * jax.nn.sigmoid(gate) * up).astype(BF16)
            out = jnp.dot(hid, wd_ref[0, 0], preferred_element_type=F32)
            halves.append(lax.bitcast_convert_type(out.astype(BF16).astype(F32), jnp.uint32))
        _store_token_tiles(o_ref, 0, TOKEN_TILE_ROWS, halves[0] | (halves[1] >> 16))

    @pl.when(i >= nused_ref[0])
    def _():
        o_ref[...] = jnp.zeros_like(o_ref)


def _unpack_pair(words):
    hi = lax.bitcast_convert_type(words & jnp.uint32(0xFFFF0000), F32)
    lo = lax.bitcast_convert_type(words << 16, F32)
    return hi, lo


def expert_pairs(xs, n_blocks, blk_a, blk_b, n_used, w_gate, w_up, w_down, layer):
    D, DE = w_gate.shape[2:]
    P = n_blocks * MOE_BLOCK
    wspec = lambda shape, which: pl.BlockSpec(shape, lambda i, ea, eb, nu: (layer, (ea, eb)[which][i], 0, 0))
    grid_spec = pltpu.PrefetchScalarGridSpec(
        num_scalar_prefetch=3,
        grid=(n_blocks,),
        in_specs=[pl.BlockSpec((MOE_BLOCK * TOKEN_TILE_ROWS, LANES), lambda i, ea, eb, nu: (i, 0)),
                  wspec((1, 1, D, DE), 0), wspec((1, 1, D, DE), 0), wspec((1, 1, DE, D), 0),
                  wspec((1, 1, D, DE), 1), wspec((1, 1, D, DE), 1), wspec((1, 1, DE, D), 1)],
        out_specs=pl.BlockSpec((MOE_BLOCK * TOKEN_TILE_ROWS, LANES), lambda i, ea, eb, nu: (i, 0)),
    )
    return pl.pallas_call(
        _expert_pair_kernel,
        grid_spec=grid_spec,
        out_shape=jax.ShapeDtypeStruct((P * TOKEN_TILE_ROWS, LANES), jnp.uint32),
        compiler_params=_cparams("arbitrary"),
        name="moe_experts",
    )(blk_a, blk_b, n_used, xs, w_gate, w_up, w_down, w_gate, w_up, w_down)


def _gather_pairs(idx_ref, src_hbm, buf, sem, n_tokens):
    def body(g, carry):
        for u in range(DMA_UNROLL):
            r = g * DMA_UNROLL + u
            src = pl.multiple_of(idx_ref[0, 0, r] * TOKEN_TILE_ROWS, TOKEN_TILE_ROWS)
            pltpu.make_async_copy(src_hbm.at[pl.ds(src, TOKEN_TILE_ROWS)],
                                  buf.at[pl.ds(r * TOKEN_TILE_ROWS, TOKEN_TILE_ROWS)], sem).start(priority=u % 2)
        return carry
    lax.fori_loop(0, n_tokens // DMA_UNROLL, body, 0)


def _collect_kernel(dest_ref, dest_next_ref, o_hbm, info_ref, h_ref, g2_ref, fg_ref, out_ref, obuf, sems, *,
                    final_norm):
    i = pl.program_id(0)
    n = pl.num_programs(0)
    slot = i % 2
    tb, D = h_ref.shape

    @pl.when(i == 0)
    def _():
        _gather_pairs(dest_ref, o_hbm, obuf.at[0], sems.at[0], tb)

    @pl.when(i + 1 < n)
    def _():
        _gather_pairs(dest_next_ref, o_hbm, obuf.at[1 - slot], sems.at[1 - slot], tb)

    _wait_rows(obuf.at[slot], sems.at[slot])
    e_a, e_b = _unpack_pair(_load_token_tiles(obuf.at[slot], 0, TOKEN_TILE_ROWS, tb, D))
    m = info_ref[:, INFO_WA:INFO_WA + 1] * e_a + info_ref[:, INFO_WB:INFO_WB + 1] * e_b
    h = h_ref[...] + g2_ref[0] * m
    if final_norm:
        ms = jnp.mean(h * h, axis=-1, keepdims=True)
        h = h * lax.rsqrt(ms + EPS) * fg_ref[...]
    out_ref[...] = h


def moe_collect(o_sorted, dest, info, block_offset, h_tokens, g2, tokens_per_batch, final_g, final_norm):
    T, D = h_tokens.shape
    tb = COLLECT_TOKENS
    nt = T // tb
    if g2.shape[0] == 1:
        g2_index = lambda i: 0
    else:
        assert tokens_per_batch % tb == 0
        g2_index = lambda i: i // (tokens_per_batch // tb)
    last = block_offset + nt - 1
    dest3 = dest.reshape(-1, 1, tb)
    return pl.pallas_call(
        functools.partial(_collect_kernel, final_norm=final_norm),
        grid=(nt,),
        in_specs=[pl.BlockSpec((1, 1, tb), lambda i: (block_offset + i, 0, 0), memory_space=pltpu.SMEM),
                  pl.BlockSpec((1, 1, tb), lambda i: (jnp.minimum(block_offset + i + 1, last), 0, 0),
                               memory_space=pltpu.SMEM),
                  pl.BlockSpec(memory_space=pl.ANY),
                  pl.BlockSpec((tb, ROUTER_COLS), lambda i: (block_offset + i, 0)),
                  pl.BlockSpec((tb, D), lambda i: (i, 0)),
                  pl.BlockSpec((1, 1, D), lambda i: (g2_index(i), 0, 0)),
                  pl.BlockSpec((1, D), lambda i: (0, 0))],
        out_specs=pl.BlockSpec((tb, D), lambda i: (i, 0)),
        out_shape=jax.ShapeDtypeStruct((T, D), F32),
        scratch_shapes=[pltpu.VMEM((2, tb * TOKEN_TILE_ROWS, LANES), jnp.uint32), pltpu.SemaphoreType.DMA((2,))],
        compiler_params=_cparams("arbitrary"),
        name="moe_collect",
    )(dest3, dest3, o_sorted, info, h_tokens, g2, final_g.reshape(1, D))


CONV_MARGIN = 16


def _time_chunk(L):
    return min(L, 256)


LANES = 128


def _zero_margins(pad_ref, L):
    zeros = jnp.zeros((CONV_MARGIN, LANES), F32)
    for s in range(pad_ref.shape[0]):
        pad_ref[s, pl.ds(0, CONV_MARGIN), :] = zeros
        pad_ref[s, pl.ds(CONV_MARGIN + L, CONV_MARGIN), :] = zeros


def _dw_conv_slab(pad_ref, s, base, T, w_ref, b_ref, col, taps, pad_left):
    acc = jnp.broadcast_to(b_ref[:, col:col + LANES], (T, LANES))
    for k in range(taps):
        acc = acc + w_ref[k:k + 1, col:col + LANES] * pad_ref[s, pl.ds(base + (CONV_MARGIN - pad_left + k), T), :]
    return acc


def _conformer_kernel(u_ref, w_ref, b_ref, g_ref, beta_ref, o_ref, ypad):
    L = o_ref.shape[1]
    T = _time_chunk(L)
    C = D_GROUP
    n_slabs = C // LANES
    pad = (CONF_KERNEL - 1) // 2
    _zero_margins(ypad, L)

    def glu(j, carry):
        base = pl.multiple_of(j * T, T)
        for s in range(n_slabs):
            a = u_ref[0, pl.ds(base, T), s * LANES:(s + 1) * LANES]
            gate = u_ref[0, pl.ds(base, T), C + s * LANES:C + (s + 1) * LANES]
            ypad[s, pl.ds(CONV_MARGIN + base, T), :] = a * jax.nn.sigmoid(gate)
        return carry
    lax.fori_loop(0, L // T, glu, 0)

    def conv(j, carry):
        base = pl.multiple_of(j * T, T)
        acc = jnp.concatenate([_dw_conv_slab(ypad, s, base, T, w_ref, b_ref, s * LANES, CONF_KERNEL, pad)
                               for s in range(n_slabs)], axis=-1)
        mu = jnp.mean(acc, axis=-1, keepdims=True)
        cen = acc - mu
        var = jnp.mean(cen * cen, axis=-1, keepdims=True)
        y = cen * lax.rsqrt(var + EPS) * g_ref[...] + beta_ref[...]
        o_ref[0, pl.ds(base, T), :] = y * jax.nn.sigmoid(y)
        return carry
    lax.fori_loop(0, L // T, conv, 0)


def conformer_conv(u, w, b, ln_g, ln_b):
    B, L, _ = u.shape
    C = D_GROUP
    vec = pl.BlockSpec((1, C), lambda i: (0, 0))
    return pl.pallas_call(
        _conformer_kernel,
        grid=(B,),
        in_specs=[pl.BlockSpec((1, L, 2 * C), lambda i: (i, 0, 0)),
                  pl.BlockSpec((CONF_KERNEL, C), lambda i: (0, 0)), vec, vec, vec],
        out_specs=pl.BlockSpec((1, L, C), lambda i: (i, 0, 0)),
        out_shape=jax.ShapeDtypeStruct((B, L, C), F32),
        scratch_shapes=[pltpu.VMEM((C // LANES, L + 2 * CONV_MARGIN, LANES), F32)],
        compiler_params=_cparams("parallel"),
        name="conformer_conv",
    )(u, w, b.reshape(1, C), ln_g.reshape(1, C), ln_b.reshape(1, C))


def _gelu_tanh(x):
    return 0.5 * x * (1.0 + jnp.tanh(math.sqrt(2.0 / math.pi) * (x + 0.044715 * (x * x * x))))


def _lru_kernel(uc_ref, ux_ref, cw_ref, cb_ref, wcat_ref, bcat_ref, lam_ref, *rest, need_ctx):
    if need_ctx:
        oc_ref, ox_ref, cpad, xpad, a_s, b_s, yx, yc = rest
    else:
        ox_ref, cpad, xpad, a_s, b_s, yx = rest
        oc_ref = yc = None
    C = D_GROUP
    n_slabs = C // LANES
    Lc = uc_ref.shape[1]
    Lx = ux_ref.shape[1]
    pad_l = (LRU_CONV - 1) // 2

    def fill(pad_ref, u_ref, L):
        T = _time_chunk(L)
        _zero_margins(pad_ref, L)

        def body(j, carry):
            base = pl.multiple_of(j * T, T)
            for s in range(n_slabs):
                pad_ref[s, pl.ds(CONV_MARGIN + base, T), :] = u_ref[0, pl.ds(base, T),
                                                                    C + s * LANES:C + (s + 1) * LANES]
            return carry
        lax.fori_loop(0, L // T, body, 0)

    fill(cpad, uc_ref, Lc)
    fill(xpad, ux_ref, Lx)

    def coeffs(pad_ref, base, T, d):
        x = jnp.concatenate([_dw_conv_slab(pad_ref, s, base, T, cw_ref, cb_ref, s * LANES, LRU_CONV, pad_l)
                             for s in range(n_slabs)], axis=-1)
        t = jnp.tanh(jnp.dot(x.astype(BF16), wcat_ref[:, 2 * d * C:2 * (d + 1) * C],
                             preferred_element_type=F32) + bcat_ref[:, 2 * d * C:2 * (d + 1) * C])
        i = 0.5 * t[:, C:] + 0.5
        z = -lam_ref[d:d + 1, :]
        softplus = jnp.maximum(z, 0.0) + jnp.log(1.0 + jnp.exp(-jnp.abs(z)))
        half_rate = (-0.5 * LRU_C) * softplus
        a = jnp.exp(half_rate * t[:, :C] + half_rate)
        b = jnp.sqrt(1.0 - a * a) * (i * x)
        for s in range(n_slabs):
            a_s[d * n_slabs + s, pl.ds(0, T), :] = a[:, s * LANES:(s + 1) * LANES]
            b_s[d * n_slabs + s, pl.ds(0, T), :] = b[:, s * LANES:(s + 1) * LANES]

    def run(pad_ref, L, h, y_ref):
        T = _time_chunk(L)
        n = L // T

        def chunk(j, h):
            base_f = pl.multiple_of(j * T, T)
            base_b = pl.multiple_of((n - 1 - j) * T, T)
            coeffs(pad_ref, base_f, T, 0)
            coeffs(pad_ref, base_b, T, 1)

            def step(t, h):
                new = []
                for d, (base, row) in enumerate(((base_f, t), (base_b, T - 1 - t))):
                    for s in range(n_slabs):
                        k = d * n_slabs + s
                        hs = a_s[k, pl.ds(row, 1), :] * h[k] + b_s[k, pl.ds(row, 1), :]
                        if y_ref is not None:
                            y_ref[k, pl.ds(base + row, 1), :] = hs
                        new.append(hs)
                return tuple(new)
            return lax.fori_loop(0, T, step, h, unroll=8)
        return lax.fori_loop(0, n, chunk, h)

    h = tuple(jnp.zeros((1, LANES), F32) for _ in range(2 * n_slabs))
    h = run(cpad, Lc, h, yc)
    run(xpad, Lx, h, yx)

    def finish(u_ref, y_ref, o_ref, L):
        T = _time_chunk(L)

        def body(j, carry):
            base = pl.multiple_of(j * T, T)
            y = jnp.concatenate([y_ref[s, pl.ds(base, T), :] + y_ref[n_slabs + s, pl.ds(base, T), :]
                                 for s in range(n_slabs)], axis=-1)
            o_ref[0, pl.ds(base, T), :] = _gelu_tanh(u_ref[0, pl.ds(base, T), :C]) * y
            return carry
        lax.fori_loop(0, L // T, body, 0)

    finish(ux_ref, yx, ox_ref, Lx)
    if need_ctx:
        finish(uc_ref, yc, oc_ref, Lc)


def _block_diag(w):
    H, n, _ = w.shape
    eye = jnp.eye(H, dtype=w.dtype)
    return (eye[:, None, :, None] * w[:, :, None, :]).reshape(H * n, H * n)


def rglru_mixer(uc, ux, lp, need_ctx):
    B, Lc, _ = uc.shape
    Lx = ux.shape[1]
    C = D_GROUP
    wcat = (0.5 * jnp.concatenate([_block_diag(lp["lru_wa"][0]), _block_diag(lp["lru_wx"][0]),
                                   _block_diag(lp["lru_wa"][1]), _block_diag(lp["lru_wx"][1])], axis=1)).astype(BF16)
    bcat = 0.5 * jnp.concatenate([lp["lru_ba"][0], lp["lru_bx"][0], lp["lru_ba"][1], lp["lru_bx"][1]]).reshape(1, 4 * C)
    full = lambda r, c: pl.BlockSpec((r, c), lambda i: (0, 0))
    seq = lambda L, n: pl.BlockSpec((1, L, n), lambda i: (i, 0, 0))
    out_specs = [seq(Lx, C)]
    out_shape = [jax.ShapeDtypeStruct((B, Lx, C), F32)]
    if need_ctx:
        out_specs = [seq(Lc, C)] + out_specs
        out_shape = [jax.ShapeDtypeStruct((B, Lc, C), F32)] + out_shape
    T = _time_chunk(Lx)
    slab = lambda rows, n=1: pltpu.VMEM((n * C // LANES, rows, LANES), F32)
    scratch = [slab(Lc + 2 * CONV_MARGIN), slab(Lx + 2 * CONV_MARGIN), slab(T, 2), slab(T, 2), slab(Lx, 2)]
    if need_ctx:
        scratch.append(slab(Lc, 2))
    res = pl.pallas_call(
        functools.partial(_lru_kernel, need_ctx=need_ctx),
        grid=(B,),
        in_specs=[seq(Lc, 2 * C), seq(Lx, 2 * C), full(LRU_CONV, C), full(1, C), full(C, 4 * C),
                  full(1, 4 * C), full(2, C)],
        out_specs=out_specs,
        out_shape=out_shape,
        scratch_shapes=scratch,
        compiler_params=_cparams("parallel"),
        name="rglru",
    )(uc, ux, lp["lru_conv_w"], lp["lru_conv_b"].reshape(1, C), wcat, bcat, lp["lru_lambda"])
    if need_ctx:
        return res[0], res[1]
    return None, res[0]


HY_SHORT = 3


def _short_conv(pad_ref, base, T, w_ref, b_ref, c0, c1):
    return jnp.concatenate([_dw_conv_slab(pad_ref, col // LANES, base, T, w_ref, b_ref, col, HY_SHORT, 1)
                            for col in range(c0, c1, LANES)], axis=-1)


def _fill_padded(pad_ref, u_ref, L, T):
    _zero_margins(pad_ref, L)

    def body(j, carry):
        base = pl.multiple_of(j * T, T)
        for s in range(pad_ref.shape[0]):
            pad_ref[s, pl.ds(CONV_MARGIN + base, T), :] = u_ref[0, pl.ds(base, T), s * LANES:(s + 1) * LANES]
        return carry
    lax.fori_loop(0, L // T, body, 0)


def _hyena_pre_kernel(u_ref, w_ref, b_ref, z_ref, upad):
    L = u_ref.shape[1]
    T = _time_chunk(L)
    C = D_GROUP
    _fill_padded(upad, u_ref, L, T)

    def body(j, carry):
        base = pl.multiple_of(j * T, T)
        x1 = _short_conv(upad, base, T, w_ref, b_ref, C, 2 * C)
        v = _short_conv(upad, base, T, w_ref, b_ref, 2 * C, 3 * C)
        z_ref[pl.ds(base, T), :] = (x1 * v).astype(BF16)
        return carry
    lax.fori_loop(0, L // T, body, 0)


def _hyena_post_kernel(u_ref, y_ref, w_ref, b_ref, bias_ref, o_ref, upad):
    L = u_ref.shape[1]
    T = _time_chunk(L)
    C = D_GROUP
    _fill_padded(upad, u_ref, L, T)

    def body(j, carry):
        base = pl.multiple_of(j * T, T)
        x0 = _short_conv(upad, base, T, w_ref, b_ref, 0, C)
        x1 = _short_conv(upad, base, T, w_ref, b_ref, C, 2 * C)
        v = _short_conv(upad, base, T, w_ref, b_ref, 2 * C, 3 * C)
        o_ref[0, pl.ds(base, T), :] = x0 * (y_ref[pl.ds(base, T), :] + (x1 * v) * bias_ref[...])
        return carry
    lax.fori_loop(0, L // T, body, 0)


def _spectrum_kernel(f_ref, z_ref, ha_ref, hb_ref, hc_ref, y_ref):
    tf = ha_ref.shape[0]
    acc = jnp.dot(f_ref[...], z_ref[...], preferred_element_type=F32)
    zr = acc[:tf]
    zi = acc[tf:]
    y_ref[:tf, :] = (zr * ha_ref[...] - zi * hb_ref[...]).astype(BF16)
    y_ref[tf:, :] = (zr * hb_ref[...] + zi * hc_ref[...]).astype(BF16)


def _idft_kernel(f_ref, y_ref, o_ref):
    o_ref[...] = jnp.dot(f_ref[...], y_ref[...], preferred_element_type=F32)


def dft_tables(L):
    N = 2 * L
    tf = min(256, L)
    k = jnp.arange(L, dtype=jnp.int32)
    n = jnp.arange(L, dtype=jnp.int32)
    ang = (2.0 * math.pi / N) * ((k[:, None] * n[None, :]) % N).astype(F32)
    cos = jnp.cos(ang)
    sin = jnp.sin(ang)
    nyq = jnp.where(n % 2 == 0, 1.0, -1.0).astype(F32)
    f_re = cos
    f_im = (-sin).at[0].set(nyq)
    fwd = jnp.stack([f_re.reshape(L // tf, tf, L), f_im.reshape(L // tf, tf, L)], axis=1).reshape(N, L)
    ck = jnp.where(k == 0, 1.0, 2.0).astype(F32)[:, None] / N
    i_re = cos * ck
    i_im = (-sin * ck).at[0].set(nyq / N)
    inv = jnp.stack([i_re.reshape(L // tf, tf, L), i_im.reshape(L // tf, tf, L)], axis=1).reshape(N, L).T
    return fwd.astype(BF16), inv.astype(BF16)


def filter_spectrum(k_slabs, fwd):
    n_slabs, N, _ = k_slabs.shape
    L = N // 2
    C = n_slabs * LANES
    tf = min(256, L)
    halves = jnp.concatenate([k_slabs[s, h * L:(h + 1) * L] for h in range(2) for s in range(n_slabs)], axis=1)
    r = pl.pallas_call(
        _idft_kernel,
        grid=(N // (2 * tf), 1),
        in_specs=[pl.BlockSpec((2 * tf, L), lambda i, j: (i, 0)),
                  pl.BlockSpec((L, 2 * C), lambda i, j: (0, 0))],
        out_specs=pl.BlockSpec((2 * tf, 2 * C), lambda i, j: (i, 0)),
        out_shape=jax.ShapeDtypeStruct((N, 2 * C), F32),
        compiler_params=_cparams("parallel", "parallel"),
        name="hyena_filter_dft",
    )(fwd, halves.astype(BF16)).reshape(L // tf, 2, tf, 2 * C)
    sign = jnp.where(jnp.arange(L) % 2 == 0, 1.0, -1.0).astype(F32)[:, None]
    re = r[:, 0].reshape(L, 2 * C)
    im = r[:, 1].reshape(L, 2 * C)
    hr = re[:, :C] + sign * re[:, C:]
    hi = im[:, :C] + sign * im[:, C:]
    return hr, hi.at[0].set(0.0), hr.at[0].set(hi[0])


def hyena_mixer(u, lp, tables):
    B, L, _ = u.shape
    C = D_GROUP
    N = 2 * L
    fwd, inv = tables
    tf = min(256, L)
    T = _time_chunk(L)
    w, bsh = lp["hy_short_w"], lp["hy_short_b"].reshape(1, 3 * C)
    z2 = pl.pallas_call(
        _hyena_pre_kernel,
        grid=(B,),
        in_specs=[pl.BlockSpec((1, L, 3 * C), lambda b: (b, 0, 0)),
                  pl.BlockSpec((HY_SHORT, 3 * C), lambda b: (0, 0)),
                  pl.BlockSpec((1, 3 * C), lambda b: (0, 0))],
        out_specs=pl.BlockSpec((L, C), lambda b: (0, b)),
        out_shape=jax.ShapeDtypeStruct((L, B * C), BF16),
        scratch_shapes=[pltpu.VMEM((3 * C // LANES, L + 2 * CONV_MARGIN, LANES), F32)],
        compiler_params=_cparams("parallel"),
        name="hyena_pre",
    )(u, w, bsh)

    tn = 2 * C
    ha, hb, hc = [jnp.tile(t, (1, tn // C)) for t in filter_spectrum(hyena_filter(L, lp), fwd)]
    hspec = pl.BlockSpec((tf, tn), lambda i, j: (i, 0))
    y2 = pl.pallas_call(
        _spectrum_kernel,
        grid=(L // tf, B * C // tn),
        in_specs=[pl.BlockSpec((2 * tf, L), lambda i, j: (i, 0)),
                  pl.BlockSpec((L, tn), lambda i, j: (0, j)), hspec, hspec, hspec],
        out_specs=pl.BlockSpec((2 * tf, tn), lambda i, j: (i, j)),
        out_shape=jax.ShapeDtypeStruct((N, B * C), BF16),
        compiler_params=_cparams("parallel", "parallel"),
        name="hyena_spectrum",
    )(fwd, z2, ha, hb, hc)

    tl = min(256, L)
    yt = pl.pallas_call(
        _idft_kernel,
        grid=(L // tl, B * C // tn),
        in_specs=[pl.BlockSpec((tl, N), lambda i, j: (i, 0)),
                  pl.BlockSpec((N, tn), lambda i, j: (0, j))],
        out_specs=pl.BlockSpec((tl, tn), lambda i, j: (i, j)),
        out_shape=jax.ShapeDtypeStruct((L, B * C), F32),
        compiler_params=_cparams("parallel", "parallel"),
        name="hyena_idft",
    )(inv, y2)

    return pl.pallas_call(
        _hyena_post_kernel,
        grid=(B,),
        in_specs=[pl.BlockSpec((1, L, 3 * C), lambda b: (b, 0, 0)),
                  pl.BlockSpec((L, C), lambda b: (0, b)),
                  pl.BlockSpec((HY_SHORT, 3 * C), lambda b: (0, 0)),
                  pl.BlockSpec((1, 3 * C), lambda b: (0, 0)),
                  pl.BlockSpec((1, C), lambda b: (0, 0))],
        out_specs=pl.BlockSpec((1, L, C), lambda b: (b, 0, 0)),
        out_shape=jax.ShapeDtypeStruct((B, L, C), F32),
        scratch_shapes=[pltpu.VMEM((3 * C // LANES, L + 2 * CONV_MARGIN, LANES), F32)],
        compiler_params=_cparams("parallel"),
        name="hyena_post",
    )(u, yt, w, bsh, lp["hy_bias"].reshape(1, C))


FFT_N2 = 128
FFT_UNROLL = 8


class _FftPlan:
    def __init__(self, L):
        self.L = L
        self.N = 2 * L
        self.N1 = self.N // FFT_N2
        self.KH = self.N1 // 2 + 1
        self.KP = -(-self.KH // 8) * 8
        self.PA = 2 * self.KP + 4


def fft_tables(L):
    p = _FftPlan(L)
    N, N1, KH, KP = p.N, p.N1, p.KH, p.KP
    n2 = jnp.arange(FFT_N2, dtype=jnp.int32)
    k1 = jnp.arange(KP, dtype=jnp.int32)
    n1 = jnp.arange(N1, dtype=jnp.int32)
    n = FFT_N2 * n1[None, None, :] + n2[:, None, None]
    ang = (2.0 * math.pi / N) * ((k1[None, :, None] * n) % N).astype(F32)
    keep = (k1 < KH)[None, :, None]
    g_re = jnp.where(keep, jnp.cos(ang), 0.0)
    g_im = jnp.where(keep, -jnp.sin(ang), 0.0)
    ga_full = jnp.concatenate([g_re, g_im], axis=1)
    ck = jnp.where((k1 == 0) | (k1 == N1 // 2), 1.0, 2.0) / N
    ga_inv = jnp.swapaxes(ga_full[:, :, :N1 // 2] * jnp.tile(ck, 2)[None, :, None], 1, 2)
    kk = jnp.arange(FFT_N2, dtype=jnp.int32)
    ang2 = (2.0 * math.pi / FFT_N2) * ((kk[:, None] * kk[None, :]) % FFT_N2).astype(F32)
    fr, fi = jnp.cos(ang2), -jnp.sin(ang2)
    fb = jnp.block([[fr, -fi], [fi, fr]])
    fb_inv = jnp.block([[fr, fi], [-fi, fr]])
    return dict(ga_half=ga_full[:, :, :N1 // 2].astype(BF16), ga_full=ga_full.astype(BF16),
                ga_inv=ga_inv.astype(BF16), fb=fb.astype(BF16), fb_inv=fb_inv.astype(BF16))


def _fft_stage_a(x_ref, ga_ref, s_ref, plan, n1_count):
    n_slabs = x_ref.shape[0]

    def body(n2, carry):
        xs = jnp.concatenate([x_ref[s, pl.ds(n2, n1_count, stride=FFT_N2), :] for s in range(n_slabs)], axis=-1)
        a = jnp.dot(ga_ref[n2], xs.astype(BF16), preferred_element_type=F32)
        for s in range(n_slabs):
            s_ref[s, pl.ds(n2 * plan.PA, 2 * plan.KP), :] = a[:, s * LANES:(s + 1) * LANES]
        return carry
    lax.fori_loop(0, FFT_N2, body, 0, unroll=FFT_UNROLL)


def _fft_load_k1(s_ref, k1, plan):
    n_slabs = s_ref.shape[0]
    re = jnp.concatenate([s_ref[s, pl.ds(k1, FFT_N2, stride=plan.PA), :] for s in range(n_slabs)], axis=-1)
    im = jnp.concatenate([s_ref[s, pl.ds(plan.KP + k1, FFT_N2, stride=plan.PA), :] for s in range(n_slabs)], axis=-1)
    return jnp.concatenate([re, im], axis=0).astype(BF16)


def _fft_filter_kernel(k_ref, ga_ref, fb_ref, h_ref, s_ref, *, plan):
    _fft_stage_a(k_ref, ga_ref, s_ref, plan, plan.N1)

    def body(k1, carry):
        h_ref[k1] = jnp.dot(fb_ref[...], _fft_load_k1(s_ref, k1, plan), preferred_element_type=F32).astype(BF16)
        return carry
    lax.fori_loop(0, plan.KH, body, 0)


def _fft_conv_kernel(z_ref, ga_ref, gi_ref, fb_ref, fbi_ref, h_ref, y_ref, s_ref, *, plan):
    zs = z_ref.at[0]
    ys = y_ref.at[0]
    n_slabs = zs.shape[0]
    half = FFT_N2
    _fft_stage_a(zs, ga_ref, s_ref, plan, plan.N1 // 2)

    def body_b(k1, carry):
        x = jnp.dot(fb_ref[...], _fft_load_k1(s_ref, k1, plan), preferred_element_type=F32)
        h = h_ref[k1].astype(F32)
        xr, xi, hr, hi = x[:half], x[half:], h[:half], h[half:]
        y = jnp.concatenate([xr * hr - xi * hi, xr * hi + xi * hr], axis=0).astype(BF16)
        b = jnp.dot(fbi_ref[...], y, preferred_element_type=F32)
        for s in range(n_slabs):
            s_ref[s, pl.ds(k1, FFT_N2, stride=plan.PA), :] = b[:half, s * LANES:(s + 1) * LANES]
            s_ref[s, pl.ds(plan.KP + k1, FFT_N2, stride=plan.PA), :] = b[half:, s * LANES:(s + 1) * LANES]
        return carry
    lax.fori_loop(0, plan.KH, body_b, 0, unroll=3)

    def body_a(n2, carry):
        b = jnp.concatenate([s_ref[s, pl.ds(n2 * plan.PA, 2 * plan.KP), :] for s in range(n_slabs)], axis=-1)
        y = jnp.dot(gi_ref[n2], b.astype(BF16), preferred_element_type=F32)
        for s in range(n_slabs):
            ys[s, pl.ds(n2, plan.N1 // 2, stride=FFT_N2), :] = y[:, s * LANES:(s + 1) * LANES]
        return carry
    lax.fori_loop(0, FFT_N2, body_a, 0, unroll=FFT_UNROLL)


def fft_filter_spectrum(k, tabs):
    n_slabs, N, _ = k.shape
    plan = _FftPlan(N // 2)
    C = n_slabs * LANES
    full = lambda shape: pl.BlockSpec(shape, lambda i: (0,) * len(shape))
    return pl.pallas_call(
        functools.partial(_fft_filter_kernel, plan=plan),
        grid=(1,),
        in_specs=[full((n_slabs, plan.N, LANES)), full((FFT_N2, 2 * plan.KP, plan.N1)),
                  full((2 * FFT_N2, 2 * FFT_N2))],
        out_specs=full((plan.KH, 2 * FFT_N2, C)),
        out_shape=jax.ShapeDtypeStruct((plan.KH, 2 * FFT_N2, C), BF16),
        scratch_shapes=[pltpu.VMEM((n_slabs, FFT_N2 * plan.PA, LANES), F32)],
        compiler_params=_cparams("arbitrary"),
        name="hyena_filter_fft",
    )(k, tabs["ga_full"], tabs["fb"])


def fft_long_conv(z, h_spec, tabs):
    B, n_slabs, L, _ = z.shape
    plan = _FftPlan(L)
    C = n_slabs * LANES
    full = lambda shape: pl.BlockSpec(shape, lambda b: (0,) * len(shape))
    seq = pl.BlockSpec((1, n_slabs, L, LANES), lambda b: (b, 0, 0, 0))
    return pl.pallas_call(
        functools.partial(_fft_conv_kernel, plan=plan),
        grid=(B,),
        in_specs=[seq, full((FFT_N2, 2 * plan.KP, plan.N1 // 2)), full((FFT_N2, plan.N1 // 2, 2 * plan.KP)),
                  full((2 * FFT_N2, 2 * FFT_N2)), full((2 * FFT_N2, 2 * FFT_N2)),
                  full((plan.KH, 2 * FFT_N2, C))],
        out_specs=seq,
        out_shape=jax.ShapeDtypeStruct((B, n_slabs, L, LANES), F32),
        scratch_shapes=[pltpu.VMEM((n_slabs, FFT_N2 * plan.PA, LANES), F32)],
        compiler_params=_cparams("parallel"),
        name="hyena_fft_conv",
    )(z, tabs["ga_half"], tabs["ga_inv"], tabs["fb"], tabs["fb_inv"], h_spec)


def _hyena_pre_slab_kernel(u_ref, w_ref, b_ref, z_ref, upad):
    L = u_ref.shape[1]
    T = _time_chunk(L)
    C = D_GROUP
    _fill_padded(upad, u_ref, L, T)

    def body(j, carry):
        base = pl.multiple_of(j * T, T)
        for s in range(C // LANES):
            x1 = _dw_conv_slab(upad, C // LANES + s, base, T, w_ref, b_ref, C + s * LANES, HY_SHORT, 1)
            v = _dw_conv_slab(upad, 2 * C // LANES + s, base, T, w_ref, b_ref, 2 * C + s * LANES, HY_SHORT, 1)
            z_ref[0, s, pl.ds(base, T), :] = x1 * v
        return carry
    lax.fori_loop(0, L // T, body, 0)


def _hyena_post_slab_kernel(u0_ref, z_ref, y_ref, w_ref, b_ref, bias_ref, o_ref, upad):
    L = u0_ref.shape[1]
    T = _time_chunk(L)
    C = D_GROUP
    _fill_padded(upad, u0_ref, L, T)

    def body(j, carry):
        base = pl.multiple_of(j * T, T)
        x0 = _short_conv(upad, base, T, w_ref, b_ref, 0, C)
        z = jnp.concatenate([z_ref[0, s, pl.ds(base, T), :] for s in range(C // LANES)], axis=-1)
        y = jnp.concatenate([y_ref[0, s, pl.ds(base, T), :] for s in range(C // LANES)], axis=-1)
        o_ref[0, pl.ds(base, T), :] = x0 * (y + z * bias_ref[...])
        return carry
    lax.fori_loop(0, L // T, body, 0)


def hyena_mixer_fft(u, lp, tabs):
    B, L, _ = u.shape
    C = D_GROUP
    n_slabs = C // LANES
    w, bsh = lp["hy_short_w"], lp["hy_short_b"].reshape(1, 3 * C)
    useq = pl.BlockSpec((1, L, 3 * C), lambda b: (b, 0, 0))
    slabs = pl.BlockSpec((1, n_slabs, L, LANES), lambda b: (b, 0, 0, 0))
    wspec = pl.BlockSpec((HY_SHORT, 3 * C), lambda b: (0, 0))
    bspec = pl.BlockSpec((1, 3 * C), lambda b: (0, 0))
    pad_scratch = pltpu.VMEM((3 * C // LANES, L + 2 * CONV_MARGIN, LANES), F32)
    z = pl.pallas_call(
        _hyena_pre_slab_kernel,
        grid=(B,),
        in_specs=[useq, wspec, bspec],
        out_specs=slabs,
        out_shape=jax.ShapeDtypeStruct((B, n_slabs, L, LANES), F32),
        scratch_shapes=[pad_scratch],
        compiler_params=_cparams("parallel"),
        name="hyena_pre",
    )(u, w, bsh)
    y = fft_long_conv(z, fft_filter_spectrum(hyena_filter(L, lp), tabs), tabs)
    return pl.pallas_call(
        _hyena_post_slab_kernel,
        grid=(B,),
        in_specs=[pl.BlockSpec((1, L, C), lambda b: (b, 0, 0)), slabs, slabs, wspec, bspec,
                  pl.BlockSpec((1, C), lambda b: (0, 0))],
        out_specs=pl.BlockSpec((1, L, C), lambda b: (b, 0, 0)),
        out_shape=jax.ShapeDtypeStruct((B, L, C), F32),
        scratch_shapes=[pltpu.VMEM((n_slabs, L + 2 * CONV_MARGIN, LANES), F32)],
        compiler_params=_cparams("parallel"),
        name="hyena_post",
    )(u, z, y, w, bsh, lp["hy_bias"].reshape(1, C))


def _filter_gen_kernel(z_ref, zr_ref, w1_ref, b1_ref, w2_ref, b2_ref, w3_ref, dl_ref, k_ref, nrm):
    L = z_ref.shape[0]
    T = min(L, 512)
    C = D_GROUP
    hi = lax.Precision.HIGHEST

    def decayed(zz):
        h1 = jnp.sin(jnp.dot(zz, w1_ref[...], preferred_element_type=F32, precision=hi) + b1_ref[...])
        h2 = jnp.sin(jnp.dot(h1, w2_ref[...], preferred_element_type=F32, precision=hi) + b2_ref[...])
        h = jnp.dot(h2, w3_ref[...], preferred_element_type=F32, precision=hi)
        decay = jnp.exp(-zz[:, 0:1] * dl_ref[...])
        return h[:, :C] * decay, h[:, C:] * decay

    nrm[...] = jnp.zeros_like(nrm)

    def emit(j, carry):
        base = pl.multiple_of(j * T, T)
        fwd, _ = decayed(z_ref[pl.ds(base, T), :])
        _, bwd_rev = decayed(zr_ref[pl.ds(base, T), :])
        nrm[...] = nrm[...] + jnp.sum(jnp.abs(fwd) + jnp.abs(bwd_rev), axis=0, keepdims=True)
        for s in range(C // LANES):
            k_ref[s, pl.ds(base, T), :] = fwd[:, s * LANES:(s + 1) * LANES]
            k_ref[s, pl.ds(L + base, T), :] = bwd_rev[:, s * LANES:(s + 1) * LANES]
        return carry
    lax.fori_loop(0, L // T, emit, 0)

    inv = 1.0 / (nrm[...] + EPS)

    def scale(j, carry):
        base = pl.multiple_of(j * T, T)
        row = base + lax.broadcasted_iota(jnp.int32, (T, 1), 0)
        for s in range(C // LANES):
            inv_s = inv[:, s * LANES:(s + 1) * LANES]
            k_ref[s, pl.ds(base, T), :] = k_ref[s, pl.ds(base, T), :] * inv_s
            k_ref[s, pl.ds(L + base, T), :] = jnp.where(row == 0, 0.0, k_ref[s, pl.ds(L + base, T), :] * inv_s)
        return carry
    lax.fori_loop(0, L // T, scale, 0)


def hyena_filter(L, lp):
    C = D_GROUP
    n_slabs = C // LANES
    t = jnp.linspace(0.0, 1.0, L, dtype=F32)[:, None]
    bands = (HY_EMB - 1) // 2
    w = 2.0 * math.pi * jnp.arange(L, dtype=F32)[:, None] / L
    f = jnp.linspace(1e-4, bands - 1, bands, dtype=F32)[None]
    z = jnp.concatenate([t, jnp.cos(f * w), -jnp.sin(f * w)], axis=-1)
    z = jnp.pad(z, ((0, 0), (0, LANES - HY_EMB)))
    z_rev = jnp.concatenate([z[:1], z[1:][::-1]], axis=0)
    w1 = jnp.pad(lp["hy_ffn_w1"], ((0, LANES - HY_EMB), (0, 0)))
    H = w1.shape[1]
    max_decay = math.log(HY_TARGET) / HY_FAST_DECAY
    min_decay = math.log(HY_TARGET) / HY_SLOW_DECAY
    abs_deltas = jnp.abs(jnp.linspace(min_decay, max_decay, C, dtype=F32)).reshape(1, C)
    full = lambda *shape: pl.BlockSpec(shape, lambda i: (0,) * len(shape))
    return pl.pallas_call(
        _filter_gen_kernel,
        grid=(1,),
        in_specs=[full(L, LANES), full(L, LANES), full(LANES, H), full(1, H), full(H, H), full(1, H),
                  full(H, 2 * C), full(1, C)],
        out_specs=full(n_slabs, 2 * L, LANES),
        out_shape=jax.ShapeDtypeStruct((n_slabs, 2 * L, LANES), F32),
        scratch_shapes=[pltpu.VMEM((1, C), F32)],
        compiler_params=_cparams("arbitrary"),
        name="hyena_filter_gen",
    )(z, z_rev, w1, lp["hy_ffn_b1"].reshape(1, H), lp["hy_ffn_w2"], lp["hy_ffn_b2"].reshape(1, H),
      lp["hy_ffn_w3"], abs_deltas)


def _layer(hc, hx, c_silu_all, lp, need_ctx, final_g, final_norm, tables_x, tables_c, experts, layer):
    B, S, D = hx.shape
    C = hc.shape[1]
    mod = small_linear(c_silu_all, lp["ada_w"], lp["ada_b"])
    mod_x = mod[:B].reshape(B, 6, 1, D)
    mod_c = jnp.broadcast_to(mod[B].reshape(1, 6, 1, D), (B, 6, 1, D))
    w_ext = extend_w_in(lp["w_in"])
    cos_x, sin_x = rope_tables(S, True)
    cos_c, sin_c = rope_tables(C, False)
    tm_x = 512
    if can_fuse_weight_cast(B * (S // tm_x), experts):
        (hy_x, cf_x, at_x, lr_x), experts_bf16 = in_proj(hx, mod_x[:, 0], mod_x[:, 1], lp["norm1_g"], w_ext,
                                                         cos_x, sin_x, tm=tm_x, cast_weights=experts, layer=layer)
    else:
        (hy_x, cf_x, at_x, lr_x), _ = in_proj(hx, mod_x[:, 0], mod_x[:, 1], lp["norm1_g"], w_ext, cos_x, sin_x,
                                              tm=tm_x)
        experts_bf16 = tuple(w[layer:layer + 1].astype(BF16) for w in experts)
    (hy_c, cf_c, at_c, lr_c), _ = in_proj(hc, mod_c[:, 0], mod_c[:, 1], lp["norm1_g"], w_ext, cos_c, sin_c, tm=256)

    yd_c, yd_x = rglru_mixer(lr_c, lr_x, lp, need_ctx)
    conf = lambda u: conformer_conv(u, lp["conf_dw_w"], lp["conf_dw_b"], lp["conf_ln_g"], lp["conf_ln_b"])
    ys_x = [hyena_mixer_fft(hy_x, lp, tables_x), conf(cf_x),
            window_attention(at_x, at_c, lp["attn_sink"]), yd_x]

    w_out = lp["w_out"].astype(BF16)
    w_router = jnp.zeros((D, ROUTER_COLS), F32)
    w_router = w_router.at[:, :N_GROUPS].set(lp["router_g_w"]).at[:, N_GROUPS:N_GROUPS + N_EXPERTS].set(lp["router_e_w"])
    w_router = w_router.astype(BF16)
    b_router = jnp.zeros((1, ROUTER_COLS), F32)
    b_router = b_router.at[0, :N_GROUPS].set(lp["router_g_b"]).at[0, N_GROUPS:N_GROUPS + N_EXPERTS].set(lp["router_e_b"])

    hx1, lg_x = out_proj(ys_x, hx, mod_x[:, 2], lp["group_norm_g"], w_out, lp["norm2_g"],
                         mod_x[:, 3], mod_x[:, 4], w_router, b_router, tm=1024)
    h_tok = hx1.reshape(B * S, D)
    hc_tok = None
    lg = lg_x.reshape(B * S, ROUTER_COLS)
    if need_ctx:
        ys_c = [hyena_mixer(hy_c, lp, tables_c), conf(cf_c),
                context_attention(at_c, lp["attn_sink"]), yd_c]
        hc1, lg_c = out_proj(ys_c, hc, mod_c[:, 2], lp["group_norm_g"], w_out, lp["norm2_g"],
                             mod_c[:, 3], mod_c[:, 4], w_router, b_router, tm=256)
        hc_tok = hc1.reshape(B * C, D)
        lg = jnp.concatenate([lg, lg_c.reshape(B * C, ROUTER_COLS)], axis=0)

    T = lg.shape[0]
    n_blocks = -(-T // MOE_BLOCK) + N_CLASSES
    info, counts, ids, zero_slots = route_tokens(lg, n_blocks * MOE_BLOCK * TOKEN_TILE_ROWS)
    dest, blk_a, blk_b, n_used = slot_plan(ids, counts, n_blocks)
    xs = moe_dispatch(h_tok, hc_tok, dest, zero_slots, lp["norm2_g"], (mod_x[:, 3], mod_x[:, 4]),
                      (mod_c[:, 3], mod_c[:, 4]), S)
    o_sorted = expert_pairs(xs, n_blocks, blk_a, blk_b, n_used, *experts_bf16, 0)
    hx2 = moe_collect(o_sorted, dest, info, 0, h_tok, mod_x[:, 5], S, final_g, final_norm)
    hx2 = hx2.reshape(B, S, D)
    if need_ctx:
        hc2 = moe_collect(o_sorted, dest, info, B * S // COLLECT_TOKENS, hc_tok, mod_c[:1, 5], C,
                          final_g, False).reshape(B, C, D)
    else:
        hc2 = hc
    return hc2, hx2


def kernel(x, c, ctx, c_ctx, norm1_g, norm2_g, ada_w, ada_b, w_in, hy_short_w, hy_short_b, hy_ffn_w1, hy_ffn_b1, hy_ffn_w2, hy_ffn_b2, hy_ffn_w3, hy_bias, conf_dw_w, conf_dw_b, conf_ln_g, conf_ln_b, attn_sink, lru_conv_w, lru_conv_b, lru_wa, lru_ba, lru_wx, lru_bx, lru_lambda, group_norm_g, w_out, router_g_w, router_g_b, router_e_w, router_e_b, exp_w_gate, exp_w_up, exp_w_down, final_norm_g):
    stacked = dict(norm1_g=norm1_g, norm2_g=norm2_g, ada_w=ada_w, ada_b=ada_b, w_in=w_in,
                   hy_short_w=hy_short_w, hy_short_b=hy_short_b, hy_ffn_w1=hy_ffn_w1, hy_ffn_b1=hy_ffn_b1,
                   hy_ffn_w2=hy_ffn_w2, hy_ffn_b2=hy_ffn_b2, hy_ffn_w3=hy_ffn_w3, hy_bias=hy_bias,
                   conf_dw_w=conf_dw_w, conf_dw_b=conf_dw_b, conf_ln_g=conf_ln_g, conf_ln_b=conf_ln_b,
                   attn_sink=attn_sink, lru_conv_w=lru_conv_w, lru_conv_b=lru_conv_b, lru_wa=lru_wa,
                   lru_ba=lru_ba, lru_wx=lru_wx, lru_bx=lru_bx, lru_lambda=lru_lambda,
                   group_norm_g=group_norm_g, w_out=w_out, router_g_w=router_g_w, router_g_b=router_g_b,
                   router_e_w=router_e_w, router_e_b=router_e_b)
    experts = (exp_w_gate, exp_w_up, exp_w_down)
    depth = norm1_g.shape[0]
    B = x.shape[0]
    cs = jnp.concatenate([jax.nn.silu(c), jnp.broadcast_to(jax.nn.silu(c_ctx)[None], (8, c.shape[1]))], axis=0)
    hc, hx = ctx, x
    tables_x = fft_tables(x.shape[1])
    tables_c = dft_tables(ctx.shape[1])
    for l in range(depth):
        lp = {k: v[l] for k, v in stacked.items()}
        hc, hx = _layer(hc, hx, cs, lp, need_ctx=(l < depth - 1), final_g=final_norm_g,
                        final_norm=(l == depth - 1), tables_x=tables_x, tables_c=tables_c,
                        experts=experts, layer=l)
    return hx
```
